```python
import math
import jax
import jax.numpy as jnp
from jax import lax

D_MODEL = 1024
BATCH = 4
SEQ = 4096
DEPTH = 2

GRID_W = 64
CTX_LEN = 256
D_MIX = 1024

HY_WIDTH = 256
HY_ORDER = 2
HY_BANDS = 16
HY_POS_DIM = 1 + 2 * HY_BANDS
HY_FILTER_HIDDEN = 64
HY_DECAY_FAST = math.log(1e-2) / 0.3
HY_DECAY_SLOW = math.log(1e-2) / 1.5

DA_HEADS = 4
DA_HEAD_DIM = 32

MLA_HEADS = 4
MLA_Q_RANK = 192
MLA_KV_RANK = 128
MLA_NOPE_DIM = 64
MLA_ROPE_DIM = 32
MLA_V_DIM = 64

HG_HEADS = 4
HG_KEY_DIM = 64
HG_VAL_DIM = 64
HG_CHUNK = 64

N_EXPERTS = 64
N_EXPERT_GROUPS = 8
TOPK_GROUPS = 4
TOP_K = 8
EXPERT_FF = 256
SHARED_FF = 256
ROUTED_SCALE = 2.5
MOE_BLOCK = 128

Q_BLOCK = 128
ROPE_BASE = 10000.0
NORM_EPS = 1e-6

kernel_name = 'hybrid_diffusion_hymba_moe_trunk'


def _layout():
    widths = (
        ('hy', (HY_ORDER + 1) * HY_WIDTH),
        ('da_q', 2 * DA_HEADS * DA_HEAD_DIM),
        ('da_k', 2 * DA_HEADS * DA_HEAD_DIM),
        ('da_v', 2 * DA_HEADS * DA_HEAD_DIM),
        ('mla_q', MLA_Q_RANK),
        ('mla_kv', MLA_KV_RANK),
        ('mla_kr', MLA_ROPE_DIM),
        ('hg_q', HG_HEADS * HG_KEY_DIM),
        ('hg_ff', HG_HEADS * HG_KEY_DIM),
        ('hg_fb', HG_HEADS * HG_KEY_DIM),
        ('hg_i', HG_HEADS * HG_VAL_DIM),
        ('hg_g', HG_HEADS * HG_VAL_DIM),
    )
    lay, off = {}, 0
    for name, w in widths:
        lay[name] = (off, w)
        off += w
    return lay, off


def _rmsnorm(x, g):
    xf = x.astype(jnp.float32)
    y = xf * lax.rsqrt(jnp.mean(xf * xf, axis=-1, keepdims=True) + NORM_EPS)
    return y.astype(x.dtype) * g


def _ada(cond, w, b, n):
    d = cond.shape[-1]
    m = jax.nn.silu(cond) @ w[:, :n * d] + b[:n * d]
    return jnp.split(m, n, axis=-1)


def _heads(a, n):
    b, s, _ = a.shape
    return a.reshape(b, s, n, -1).transpose(0, 2, 1, 3)


def _merge_heads(a):
    b, h, s, d = a.shape
    return a.transpose(0, 2, 1, 3).reshape(b, s, h * d)


def _rope_1d(x, pos):
    n = x.shape[-1] // 2
    inv = ROPE_BASE ** (-jnp.arange(n, dtype=jnp.float32) / n)
    ang = pos.astype(jnp.float32)[:, None] * inv
    cos, sin = jnp.cos(ang).astype(x.dtype), jnp.sin(ang).astype(x.dtype)
    x1, x2 = x[..., :n], x[..., n:]
    return jnp.concatenate([x1 * cos - x2 * sin, x2 * cos + x1 * sin], axis=-1)


def _rope_2d(x, row, col):
    h = x.shape[-1] // 2
    return jnp.concatenate([_rope_1d(x[..., :h], row), _rope_1d(x[..., h:], col)], axis=-1)


def _attend(q, k, v, scale):
    b, h, sq, dq = q.shape
    nb = sq // Q_BLOCK
    qb = jnp.moveaxis(q.reshape(b, h, nb, Q_BLOCK, dq), 2, 0)

    def one(qi):
        s = jnp.einsum('bhqd,bhkd->bhqk', qi, k, preferred_element_type=jnp.float32) * scale
        p = jax.nn.softmax(s, axis=-1).astype(v.dtype)
        return jnp.einsum('bhqk,bhkd->bhqd', p, v)

    o = lax.map(one, qb)
    return jnp.moveaxis(o, 0, 2).reshape(b, h, sq, v.shape[-1])


def _short_conv3(u, w, b):
    up = jnp.pad(u, ((0, 0), (1, 1), (0, 0)))
    return up[:, :-2] * w[0] + up[:, 1:-1] * w[1] + up[:, 2:] * w[2] + b


def _hyena_filters(n, w1, b1, w2, b2, w3, b3, sin_freq, decay):
    t = jnp.arange(n, dtype=jnp.float32) / n
    ang = 2.0 * jnp.pi * t[:, None] * jnp.arange(1, HY_BANDS + 1, dtype=jnp.float32)
    z = jnp.concatenate([t[:, None], jnp.sin(ang), jnp.cos(ang)], axis=-1)
    hid = jnp.sin(sin_freq[0] * (z @ w1 + b1))
    hid = jnp.sin(sin_freq[1] * (hid @ w2 + b2))
    filt = (hid @ w3 + b3) * jnp.exp(-t[:, None] * jnp.abs(decay))
    return filt.astype(jnp.float32).reshape(n, HY_ORDER, 2, HY_WIDTH)


def _long_conv(u, h_fwd, h_bwd, bias):
    n, ch = h_fwd.shape
    h = jnp.concatenate([h_fwd[:1] + h_bwd[:1], h_fwd[1:], jnp.zeros((1, ch), jnp.float32), h_bwd[:0:-1]], axis=0)
    h = h / jnp.sum(jnp.abs(h), axis=0, keepdims=True)
    uf = u.astype(jnp.float32)
    spec = jnp.fft.rfft(uf, n=2 * n, axis=1) * jnp.fft.rfft(h, axis=0)
    y = jnp.fft.irfft(spec, n=2 * n, axis=1)[:, :n]
    return (y + uf * bias.astype(jnp.float32)).astype(u.dtype)


def _hyena(proj, conv_w, conv_b, filt, bias):
    u = _short_conv3(proj, conv_w, conv_b)
    parts = jnp.split(u, HY_ORDER + 1, axis=-1)
    z = parts[0]
    for o in range(HY_ORDER):
        z = parts[o + 1] * _long_conv(z, filt[:, o, 0], filt[:, o, 1], bias[o])
    return z


def _diff_attention(q, k, v, k_c, v_c, q_c, lam_p, subln_g, lam_init, row, col):
    def split_qk(a, rope):
        b, s, _ = a.shape
        a = a.reshape(b, s, DA_HEADS, 2, DA_HEAD_DIM).transpose(3, 0, 2, 1, 4)
        if rope:
            a = _rope_2d(a, row, col)
        return a[0], a[1]

    q1, q2 = split_qk(q, True)
    k1, k2 = split_qk(k, True)
    k1c, k2c = split_qk(k_c, False)
    vh, vch = _heads(v, DA_HEADS), _heads(v_c, DA_HEADS)
    keys1 = jnp.concatenate([k1c, k1], axis=2)
    keys2 = jnp.concatenate([k2c, k2], axis=2)
    vals = jnp.concatenate([vch, vh], axis=2)
    lp = lam_p.astype(jnp.float32)
    lam = jnp.exp(jnp.sum(lp[0] * lp[1])) - jnp.exp(jnp.sum(lp[2] * lp[3])) + lam_init
    scale = DA_HEAD_DIM ** -0.5

    def combine(o1, o2):
        o = o1 - lam.astype(o1.dtype) * o2
        return _merge_heads(_rmsnorm(o, subln_g) * (1.0 - lam_init))

    y = combine(_attend(q1, keys1, vals, scale), _attend(q2, keys2, vals, scale))
    y_c = None
    if q_c is not None:
        q1c, q2c = split_qk(q_c, False)
        y_c = combine(_attend(q1c, k1c, vch, scale), _attend(q2c, k2c, vch, scale))
    return y, y_c


def _mla(q_d, kv_d, k_r, kv_d_c, k_r_c, q_d_c, q_norm_g, w_q_up, kv_norm_g, w_kv_up, row, col):
    def queries(qd, rope):
        qh = _heads(_rmsnorm(qd, q_norm_g) @ w_q_up, MLA_HEADS)
        qn, qr = qh[..., :MLA_NOPE_DIM], qh[..., MLA_NOPE_DIM:]
        if rope:
            qr = _rope_2d(qr, row, col)
        return jnp.concatenate([qn, qr], axis=-1)

    def keys_values(kvd, kr, rope):
        kvh = _heads(_rmsnorm(kvd, kv_norm_g) @ w_kv_up, MLA_HEADS)
        kn, vv = kvh[..., :MLA_NOPE_DIM], kvh[..., MLA_NOPE_DIM:]
        if rope:
            kr = _rope_2d(kr, row, col)
        kr = jnp.broadcast_to(kr[:, None], kn.shape[:-1] + (MLA_ROPE_DIM,))
        return jnp.concatenate([kn, kr], axis=-1), vv

    k_lat, v_lat = keys_values(kv_d, k_r, True)
    k_ctx, v_ctx = keys_values(kv_d_c, k_r_c, False)
    scale = (MLA_NOPE_DIM + MLA_ROPE_DIM) ** -0.5
    y = _merge_heads(_attend(queries(q_d, True), jnp.concatenate([k_ctx, k_lat], axis=2),
                             jnp.concatenate([v_ctx, v_lat], axis=2), scale))
    y_c = None
    if q_d_c is not None:
        y_c = _merge_heads(_attend(queries(q_d_c, False), k_ctx, v_ctx, scale))
    return y, y_c


def _forget_terms(f, lb):
    log_g = jnp.logaddexp(jnp.log(lb), jnp.log1p(-lb) + jax.nn.log_sigmoid(f))
    return log_g, (1.0 - lb) * jax.nn.sigmoid(-f)


def _chunked_scan(q, k, v, log_g, s0):
    with_out = q is not None
    b, h, n, _ = k.shape
    nc = n // HG_CHUNK

    def chunks(a):
        return jnp.moveaxis(a.reshape(b, h, nc, HG_CHUNK, a.shape[-1]), 2, 0)

    xs = (chunks(k), chunks(v), chunks(log_g)) + ((chunks(q),) if with_out else ())
    tri = jnp.tril(jnp.ones((HG_CHUNK, HG_CHUNK), dtype=bool))[:, :, None]

    def step(s, inp):
        kc, vc, gc = inp[:3]
        cum = jnp.cumsum(gc, axis=-2)
        last = cum[..., -1:, :]
        s_new = jnp.exp(last)[..., 0, :, None] * s + jnp.einsum('bhck,bhcv->bhkv', kc * jnp.exp(last - cum), vc)
        if not with_out:
            return s_new, None
        qc = inp[3]
        o_prev = jnp.einsum('bhck,bhkv->bhcv', qc * jnp.exp(cum), s)
        decay = jnp.exp(jnp.where(tri, cum[..., :, None, :] - cum[..., None, :, :], -jnp.inf))
        scores = jnp.einsum('bhtk,bhsk,bhtsk->bhts', qc, kc, decay)
        return s_new, o_prev + jnp.einsum('bhts,bhsv->bhtv', scores, vc)

    s_fin, o = lax.scan(step, s0, xs)
    if with_out:
        o = jnp.moveaxis(o, 0, 2).reshape(b, h, n, v.shape[-1])
    return s_fin, o


def _hgrn2(q, f_fwd, f_bwd, i, g, f_fwd_c, f_bwd_c, i_c, q_c, g_c, lb_fwd, lb_bwd, norm_g):
    need_ctx = q_c is not None

    def heads(a):
        return _heads(a.astype(jnp.float32), HG_HEADS)

    qh, ih, ich = heads(q), heads(i), heads(i_c)
    qch = heads(q_c) if need_ctx else None
    s0 = jnp.zeros((qh.shape[0], HG_HEADS, HG_KEY_DIM, HG_VAL_DIM), jnp.float32)

    def direction(f, f_c, lb, rev):
        lb = lb.astype(jnp.float32).reshape(HG_HEADS, 1, HG_KEY_DIM)

        def flip(a):
            return jnp.flip(a, axis=2) if rev else a

        lg_c, k_c = _forget_terms(flip(heads(f_c)), lb)
        s_ctx, o_c = _chunked_scan(flip(qch) if need_ctx else None, k_c, flip(ich), lg_c, s0)
        lg, k = _forget_terms(flip(heads(f)), lb)
        _, o = _chunked_scan(flip(qh), k, flip(ih), lg, s_ctx)
        return flip(o), (flip(o_c) if need_ctx else None)

    o_f, oc_f = direction(f_fwd, f_fwd_c, lb_fwd, False)
    o_b, oc_b = direction(f_bwd, f_bwd_c, lb_bwd, True)

    def readout(o, gate):
        return _merge_heads(_rmsnorm(o, norm_g)).astype(gate.dtype) * jax.nn.silu(gate)

    y = readout(o_f + o_b, g)
    y_c = readout(oc_f + oc_b, g_c) if need_ctx else None
    return y, y_c


def _moe(h, w_router, e_bias, w_gate, w_up, w_down, s_gate, s_up, s_down):
    b, s, d = h.shape
    blocks = h.reshape(b * s // MOE_BLOCK, MOE_BLOCK, d)
    per_group = N_EXPERTS // N_EXPERT_GROUPS

    def one(hb):
        scores = jax.nn.sigmoid(jnp.einsum('td,de->te', hb, w_router, preferred_element_type=jnp.float32))
        choice = scores + e_bias.astype(jnp.float32)
        grp_score = lax.top_k(choice.reshape(-1, N_EXPERT_GROUPS, per_group), 2)[0].sum(-1)
        _, gidx = lax.top_k(grp_score, TOPK_GROUPS)
        gmask = jax.nn.one_hot(gidx, N_EXPERT_GROUPS).sum(-2) > 0
        emask = jnp.repeat(gmask, per_group, axis=-1)
        _, eidx = lax.top_k(jnp.where(emask, choice, -jnp.inf), TOP_K)
        w = jnp.take_along_axis(scores, eidx, axis=-1)
        w = w / jnp.sum(w, axis=-1, keepdims=True) * ROUTED_SCALE
        gate = jnp.einsum('tk,tke->te', w, jax.nn.one_hot(eidx, N_EXPERTS))
        a = jnp.einsum('td,edf->tef', hb, w_gate)
        u = jnp.einsum('td,edf->tef', hb, w_up)
        act = jax.nn.silu(a) * u * gate[..., None].astype(hb.dtype)
        routed = jnp.einsum('tef,efd->td', act, w_down)
        shared = (jax.nn.silu(hb @ s_gate) * (hb @ s_up)) @ s_down
        return (routed + shared).astype(hb.dtype)

    return lax.map(one, blocks).reshape(b, s, d)


def setup_inputs(seed: int = 0) -> dict:
    key = jax.random.key(seed)
    keys = jax.random.split(key, 64)
    counter = [0]

    def nrm(shape, scale):
        k = keys[counter[0]]
        counter[0] += 1
        return jax.random.normal(k, shape, jnp.float32) * scale

    def gain(shape):
        return 1.0 + nrm(shape, 0.02)

    _, n_in = _layout()
    d, dp = D_MODEL, DEPTH
    hy_cols = HY_ORDER * 2 * HY_WIDTH
    decay_base = jnp.linspace(HY_DECAY_FAST, HY_DECAY_SLOW, hy_cols, dtype=jnp.float32)
    return {
        'x': nrm((BATCH, SEQ, d), 1.0),
        'c': nrm((BATCH, d), 1.0),
        'ctx': nrm((BATCH, CTX_LEN, d), 1.0),
        'c_ctx': nrm((d,), 1.0),
        'w_ada': nrm((dp, d, 6 * d), 0.5 * d ** -0.5),
        'b_ada': nrm((dp, 6 * d), 0.02),
        'norm1_g': gain((dp, d)),
        'norm2_g': gain((dp, d)),
        'w_in': nrm((dp, d, n_in), d ** -0.5),
        'w_out': nrm((dp, D_MIX, d), D_MIX ** -0.5),
        'hy_conv_w': nrm((dp, 3, (HY_ORDER + 1) * HY_WIDTH), 3 ** -0.5),
        'hy_conv_b': nrm((dp, (HY_ORDER + 1) * HY_WIDTH), 0.02),
        'hy_w1': nrm((dp, HY_POS_DIM, HY_FILTER_HIDDEN), HY_POS_DIM ** -0.5),
        'hy_b1': nrm((dp, HY_FILTER_HIDDEN), 0.02),
        'hy_w2': nrm((dp, HY_FILTER_HIDDEN, HY_FILTER_HIDDEN), HY_FILTER_HIDDEN ** -0.5),
        'hy_b2': nrm((dp, HY_FILTER_HIDDEN), 0.02),
        'hy_w3': nrm((dp, HY_FILTER_HIDDEN, hy_cols), HY_FILTER_HIDDEN ** -0.5),
        'hy_b3': nrm((dp, hy_cols), 0.02),
        'hy_sin_freq': gain((dp, 2, HY_FILTER_HIDDEN)),
        'hy_decay': decay_base + nrm((dp, hy_cols), 0.1),
        'hy_bias': nrm((dp, HY_ORDER, HY_WIDTH), 1.0),
        'da_lambda': nrm((dp, 4, DA_HEAD_DIM), 0.1),
        'da_subln_g': gain((dp, 2 * DA_HEAD_DIM)),
        'mla_q_norm_g': gain((dp, MLA_Q_RANK)),
        'mla_w_q_up': nrm((dp, MLA_Q_RANK, MLA_HEADS * (MLA_NOPE_DIM + MLA_ROPE_DIM)), MLA_Q_RANK ** -0.5),
        'mla_kv_norm_g': gain((dp, MLA_KV_RANK)),
        'mla_w_kv_up': nrm((dp, MLA_KV_RANK, MLA_HEADS * (MLA_NOPE_DIM + MLA_V_DIM)), MLA_KV_RANK ** -0.5),
        'hg_lower_bounds': nrm((2, dp, HG_HEADS * HG_KEY_DIM), 0.1),
        'hg_norm_g': gain((dp, HG_VAL_DIM)),
        'moe_w_router': nrm((dp, d, N_EXPERTS), d ** -0.5),
        'moe_bias': nrm((dp, N_EXPERTS), 0.01),
        'moe_w_gate': nrm((dp, N_EXPERTS, d, EXPERT_FF), d ** -0.5),
        'moe_w_up': nrm((dp, N_EXPERTS, d, EXPERT_FF), d ** -0.5),
        'moe_w_down': nrm((dp, N_EXPERTS, EXPERT_FF, d), EXPERT_FF ** -0.5),
        'moe_sh_gate': nrm((dp, d, SHARED_FF), d ** -0.5),
        'moe_sh_up': nrm((dp, d, SHARED_FF), d ** -0.5),
        'moe_sh_down': nrm((dp, SHARED_FF, d), SHARED_FF ** -0.5),
        'final_norm_g': gain((d,)),
    }


def reference(x, c, ctx, c_ctx, w_ada, b_ada, norm1_g, norm2_g, w_in, w_out,
              hy_conv_w, hy_conv_b, hy_w1, hy_b1, hy_w2, hy_b2, hy_w3, hy_b3, hy_sin_freq, hy_decay, hy_bias,
              da_lambda, da_subln_g, mla_q_norm_g, mla_w_q_up, mla_kv_norm_g, mla_w_kv_up,
              hg_lower_bounds, hg_norm_g, moe_w_router, moe_bias, moe_w_gate, moe_w_up, moe_w_down,
              moe_sh_gate, moe_sh_up, moe_sh_down, final_norm_g):
    lay, _ = _layout()
    n_lat, n_ctx = x.shape[1], ctx.shape[1]
    rows = n_lat // GRID_W
    row_pos = jnp.repeat(jnp.arange(rows, dtype=jnp.int32), GRID_W)
    col_pos = jnp.tile(jnp.arange(GRID_W, dtype=jnp.int32), rows)
    lbs = jnp.cumsum(jax.nn.softmax(hg_lower_bounds.astype(jnp.float32), axis=1), axis=1)
    lbs = lbs - lbs[:, :1]

    for l in range(DEPTH):
        ctx_out = l < DEPTH - 1
        sh1, sc1, g1, sh2, sc2, g2 = [m[:, None, :] for m in _ada(c, w_ada[l], b_ada[l], 6)]
        mc = _ada(c_ctx, w_ada[l], b_ada[l], 6 if ctx_out else 2)
        h = _rmsnorm(x, norm1_g[l]) * (1.0 + sc1) + sh1
        hc = _rmsnorm(ctx, norm1_g[l]) * (1.0 + mc[1]) + mc[0]
        p = h @ w_in[l]

        def lat(name):
            off, w = lay[name]
            return p[..., off:off + w]

        def cpr(name):
            off, w = lay[name]
            return hc @ w_in[l][:, off:off + w]

        hy_args = (hy_w1[l], hy_b1[l], hy_w2[l], hy_b2[l], hy_w3[l], hy_b3[l], hy_sin_freq[l], hy_decay[l])
        y_hy = _hyena(lat('hy'), hy_conv_w[l], hy_conv_b[l], _hyena_filters(n_lat, *hy_args), hy_bias[l])
        lam_init = 0.8 - 0.6 * math.exp(-0.3 * l)
        y_da, yc_da = _diff_attention(lat('da_q'), lat('da_k'), lat('da_v'), cpr('da_k'), cpr('da_v'),
                                      cpr('da_q') if ctx_out else None, da_lambda[l], da_subln_g[l],
                                      lam_init, row_pos, col_pos)
        y_mla, yc_mla = _mla(lat('mla_q'), lat('mla_kv'), lat('mla_kr'), cpr('mla_kv'), cpr('mla_kr'),
                             cpr('mla_q') if ctx_out else None, mla_q_norm_g[l], mla_w_q_up[l],
                             mla_kv_norm_g[l], mla_w_kv_up[l], row_pos, col_pos)
        y_hg, yc_hg = _hgrn2(lat('hg_q'), lat('hg_ff'), lat('hg_fb'), lat('hg_i'), lat('hg_g'),
                             cpr('hg_ff'), cpr('hg_fb'), cpr('hg_i'),
                             cpr('hg_q') if ctx_out else None, cpr('hg_g') if ctx_out else None,
                             lbs[0, l], lbs[1, l], hg_norm_g[l])
        moe_args = (moe_w_router[l], moe_bias[l], moe_w_gate[l], moe_w_up[l], moe_w_down[l],
                    moe_sh_gate[l], moe_sh_up[l], moe_sh_down[l])

        if ctx_out:
            yc_hy = _hyena(cpr('hy'), hy_conv_w[l], hy_conv_b[l], _hyena_filters(n_ctx, *hy_args), hy_bias[l])
            mix_c = jnp.concatenate([yc_hy, yc_da, yc_mla, yc_hg], axis=-1)
            ctx = ctx + mc[2] * (mix_c @ w_out[l])
            h2c = _rmsnorm(ctx, norm2_g[l]) * (1.0 + mc[4]) + mc[3]
            ctx = ctx + mc[5] * _moe(h2c, *moe_args)

        mix = jnp.concatenate([y_hy, y_da, y_mla, y_hg], axis=-1)
        x = x + g1 * (mix @ w_out[l])
        h2 = _rmsnorm(x, norm2_g[l]) * (1.0 + sc2) + sh2
        x = x + g2 * _moe(h2, *moe_args)

    return _rmsnorm(x, final_norm_g)
```

```python
import functools
import math

import numpy as np
import jax
import jax.numpy as jnp
from jax import lax
from jax.experimental import pallas as pl
from jax.experimental.pallas import tpu as pltpu

F32 = jnp.float32
BF16 = jnp.bfloat16
HIGHEST = lax.Precision.HIGHEST

D_MODEL = 1024
GRID_W = 64
HY_WIDTH = 256
HY_ORDER = 2
HY_BANDS = 16
DA_HEADS = 4
DA_HEAD_DIM = 32
MLA_HEADS = 4
MLA_Q_RANK = 192
MLA_KV_RANK = 128
MLA_NOPE_DIM = 64
MLA_ROPE_DIM = 32
MLA_V_DIM = 64
HG_HEADS = 4
HG_KEY_DIM = 64
HG_VAL_DIM = 64
HG_CHUNK = 64
HG_SUB = 16
N_EXPERTS = 64
N_EXPERT_GROUPS = 8
TOPK_GROUPS = 4
TOP_K = 8
EXPERT_FF = 256
ROUTED_SCALE = 2.5
ROPE_BASE = 10000.0
NORM_EPS = 1e-6

V7X_VMEM_LIMIT_BYTES = 56 * 1024 * 1024
LANES = 128

_SEGMENTS = (
    ('hy_v', HY_WIDTH), ('hy_x1', HY_WIDTH), ('hy_x2', HY_WIDTH),
    ('da_q', 2 * DA_HEADS * DA_HEAD_DIM), ('da_k', 2 * DA_HEADS * DA_HEAD_DIM), ('da_v', 2 * DA_HEADS * DA_HEAD_DIM),
    ('mla_q', MLA_Q_RANK), ('mla_kv', MLA_KV_RANK), ('mla_kr', MLA_ROPE_DIM),
    ('hg_q', HG_HEADS * HG_KEY_DIM), ('hg_ff', HG_HEADS * HG_KEY_DIM), ('hg_fb', HG_HEADS * HG_KEY_DIM),
    ('hg_i', HG_HEADS * HG_VAL_DIM), ('hg_g', HG_HEADS * HG_VAL_DIM),
)


def _params(*semantics):
    return pltpu.CompilerParams(dimension_semantics=semantics, vmem_limit_bytes=V7X_VMEM_LIMIT_BYTES)


def _const_spec(shape):
    nd = len(shape)
    return pl.BlockSpec(shape, lambda *_: (0,) * nd)


def _rms(x, eps=NORM_EPS):
    return x * lax.rsqrt(jnp.mean(x * x, axis=-1, keepdims=True) + eps)


def _silu(x):
    return x * jax.nn.sigmoid(x)


def _dot_nt(a, b, **kw):
    return lax.dot_general(a, b, (((1,), (1,)), ((), ())), preferred_element_type=F32, **kw)


def _ada_kernel(c_ref, w_ref, b_ref, o_ref):
    s = _silu(c_ref[...])
    o_ref[...] = jnp.dot(s, w_ref[...], precision=HIGHEST, preferred_element_type=F32) + b_ref[...]


def _ada(cond, w, b):
    r, d = cond.shape
    n = w.shape[1]
    tn = 1536
    return pl.pallas_call(
        _ada_kernel,
        grid=(n // tn,),
        in_specs=[_const_spec((r, d)), pl.BlockSpec((d, tn), lambda j: (0, j)), pl.BlockSpec((1, tn), lambda j: (0, j))],
        out_specs=pl.BlockSpec((r, tn), lambda j: (0, j)),
        out_shape=jax.ShapeDtypeStruct((r, n), F32),
        compiler_params=_params('arbitrary'),
        name='ada',
    )(cond, w, b.reshape(1, n))


def _norm_proj_kernel(*refs, n_w, modulate):
    x_ref, g_ref = refs[0], refs[1]
    pos = 2
    if modulate:
        sc_ref, sh_ref = refs[2], refs[3]
        pos = 4
    w_refs = refs[pos:pos + n_w]
    o_refs = refs[pos + n_w:]
    y = _rms(x_ref[0]) * g_ref[...]
    if modulate:
        y = y * (1.0 + sc_ref[0]) + sh_ref[0]
    yb = y.astype(BF16)
    for w_ref, o_ref in zip(w_refs, o_refs):
        o_ref[0] = jnp.dot(yb, w_ref[...], preferred_element_type=F32).astype(o_ref.dtype)


def _norm_proj(x, g, ws, scale=None, shift=None, tm=512):
    b, s, k = x.shape
    tm = min(tm, s)
    modulate = scale is not None
    ins = [x, g.reshape(1, k)]
    in_specs = [pl.BlockSpec((1, tm, k), lambda i, j: (i, j, 0)), _const_spec((1, k))]
    if modulate:
        ins += [scale, shift]
        in_specs += [pl.BlockSpec((1, 1, k), lambda i, j: (i, 0, 0))] * 2
    for w in ws:
        ins.append(w)
        in_specs.append(_const_spec(w.shape))
    return pl.pallas_call(
        functools.partial(_norm_proj_kernel, n_w=len(ws), modulate=modulate),
        grid=(b, s // tm),
        in_specs=in_specs,
        out_specs=[pl.BlockSpec((1, tm, w.shape[1]), lambda i, j: (i, j, 0)) for w in ws],
        out_shape=[jax.ShapeDtypeStruct((b, s, w.shape[1]), F32) for w in ws],
        compiler_params=_params('parallel', 'parallel'),
        name='norm_proj',
    )(*ins)


def _hy_filter_kernel(w1t_ref, w1s_ref, w1c_ref, b1_ref, w2_ref, b2_ref, w3_ref, b3_ref, fr_ref, dec_ref, o_ref, *, n):
    t = lax.broadcasted_iota(jnp.int32, (n, 1), 0).astype(F32) / n
    bands = lax.broadcasted_iota(jnp.int32, (1, HY_BANDS), 1).astype(F32) + 1.0
    ang = (2.0 * jnp.pi) * t * bands
    pre = (t * w1t_ref[...]
           + jnp.dot(jnp.sin(ang), w1s_ref[...], precision=HIGHEST, preferred_element_type=F32)
           + jnp.dot(jnp.cos(ang), w1c_ref[...], precision=HIGHEST, preferred_element_type=F32)
           + b1_ref[...])
    hid = jnp.sin(fr_ref[0:1, :] * pre)
    hid = jnp.sin(fr_ref[1:2, :] * (jnp.dot(hid, w2_ref[...], precision=HIGHEST, preferred_element_type=F32) + b2_ref[...]))
    filt = jnp.dot(hid, w3_ref[...], precision=HIGHEST, preferred_element_type=F32) + b3_ref[...]
    filt = filt * jnp.exp(-t * jnp.abs(dec_ref[...]))
    col = jnp.sum(jnp.abs(filt), axis=0, keepdims=True) - jnp.abs(filt[0:1, :])
    w = HY_WIDTH
    for o in range(HY_ORDER):
        lo = o * 2 * w
        f0 = filt[0:1, lo:lo + w] + filt[0:1, lo + w:lo + 2 * w]
        inv = 1.0 / (col[:, lo:lo + w] + col[:, lo + w:lo + 2 * w] + jnp.abs(f0))
        o_ref[:, lo:lo + w] = filt[:, lo:lo + w] * inv
        o_ref[:, lo + w:lo + 2 * w] = filt[:, lo + w:lo + 2 * w] * inv


def _hy_filters(n, w1, b1, w2, b2, w3, b3, freq, decay):
    cols = w3.shape[1]
    ins = [w1[0:1], w1[1:1 + HY_BANDS], w1[1 + HY_BANDS:], b1.reshape(1, -1), w2, b2.reshape(1, -1), w3,
           b3.reshape(1, -1), freq, decay.reshape(1, -1)]
    out = pl.pallas_call(
        functools.partial(_hy_filter_kernel, n=n),
        grid=(1,),
        in_specs=[_const_spec(a.shape) for a in ins],
        out_specs=_const_spec((n, cols)),
        out_shape=jax.ShapeDtypeStruct((n, cols), F32),
        compiler_params=_params('arbitrary'),
        name='hy_filter',
    )(*ins)
    return out.reshape(n, HY_ORDER, 2, HY_WIDTH)


def _two_sided(filt_n):
    n = filt_n.shape[0]
    hf, hb = filt_n[:, :, 0], filt_n[:, :, 1]
    h = jnp.concatenate([hf[:1] + hb[:1], hf[1:], jnp.zeros((1,) + hf.shape[1:], F32), hb[:0:-1]], axis=0)
    return h.reshape(2 * n, HY_ORDER * HY_WIDTH)


def _short_conv_kernel(*refs, s):
    x_refs, w_refs, b_refs, o_refs = refs[0:3], refs[3:6], refs[6:9], refs[9:12]
    row = lax.broadcasted_iota(jnp.int32, (s, 1), 0)
    for x_ref, w_ref, b_ref, o_ref in zip(x_refs, w_refs, b_refs, o_refs):
        x = x_ref[0]
        prev = jnp.where(row == 0, 0.0, pltpu.roll(x, 1, axis=0))
        nxt = jnp.where(row == s - 1, 0.0, pltpu.roll(x, s - 1, axis=0))
        o_ref[0] = prev * w_ref[0:1, :] + x * w_ref[1:2, :] + nxt * w_ref[2:3, :] + b_ref[...]


def _short_conv(parts, conv_w, conv_b):
    b, s, c = parts[0].shape
    tc = LANES
    ws = [conv_w[:, i * c:(i + 1) * c] for i in range(3)]
    bs = [conv_b[i * c:(i + 1) * c].reshape(1, c) for i in range(3)]
    xspec = pl.BlockSpec((1, s, tc), lambda i, j: (i, 0, j))
    return pl.pallas_call(
        functools.partial(_short_conv_kernel, s=s),
        grid=(b, c // tc),
        in_specs=[xspec] * 3 + [pl.BlockSpec((3, tc), lambda i, j: (0, j))] * 3 + [pl.BlockSpec((1, tc), lambda i, j: (0, j))] * 3,
        out_specs=[xspec] * 3,
        out_shape=[jax.ShapeDtypeStruct((b, s, c), F32)] * 3,
        compiler_params=_params('parallel', 'parallel'),
        name='short_conv',
    )(*parts, *ws, *bs)


def _dft_cos_sin(rows, cols, period):
    ang = 2.0 * np.pi * ((np.arange(rows)[:, None] * np.arange(cols)[None, :]) % period) / period
    return np.cos(ang), np.sin(ang)


def _fft_tables(n, inner):
    big = 2 * n
    n1 = big // inner
    c1, s1 = _dft_cos_sin(n1, n1, n1)
    h = n1 // 2
    outer_data = np.block([[c1[:, :h], s1[:, :h]], [-s1[:, :h], c1[:, :h]]])
    outer_real = np.concatenate([c1, -s1], axis=0)
    outer_inv = np.block([[c1[:h, :], -s1[:h, :]], [s1[:h, :], c1[:h, :]]]) / big
    c2, s2 = _dft_cos_sin(inner, inner, inner)
    inner_fwd = np.block([[c2, s2], [-s2, c2]])
    inner_inv = np.block([[c2, -s2], [s2, c2]])
    ct, st = _dft_cos_sin(n1, inner, big)
    f = lambda a: jnp.asarray(a, F32)
    return dict(n1=n1, inner=inner, outer_data=f(outer_data), outer_real=f(outer_real), outer_inv=f(outer_inv),
                inner_fwd=f(inner_fwd), inner_inv=f(inner_inv),
                tw_cos=f(ct).reshape(n1, inner, 1), tw_sin=f(st).reshape(n1, inner, 1))


def _left_mm_kernel(m_ref, x_ref, o_ref):
    o_ref[0] = jnp.dot(m_ref[...], x_ref[0], precision=HIGHEST, preferred_element_type=F32)


def _left_mm(m, x, tl=4096):
    p, k, l = x.shape
    mm = m.shape[0]
    tl = min(tl, l)
    return pl.pallas_call(
        _left_mm_kernel,
        grid=(p, l // tl),
        in_specs=[_const_spec(m.shape), pl.BlockSpec((1, k, tl), lambda i, j: (i, 0, j))],
        out_specs=pl.BlockSpec((1, mm, tl), lambda i, j: (i, 0, j)),
        out_shape=jax.ShapeDtypeStruct((p, mm, l), F32),
        compiler_params=_params('parallel', 'parallel'),
        name='fft_outer',
    )(m, x)


def _inner_kernel(a_ref, twc_ref, tws_ref, gf_ref, *rest, convolve, inner):
    ar, ai = a_ref[0, 0, 0], a_ref[0, 1, 0]
    tc, ts = twc_ref[0], tws_ref[0]
    br = ar * tc + ai * ts
    bi = ai * tc - ar * ts
    x = jnp.dot(gf_ref[...], jnp.concatenate([br, bi], axis=0), precision=HIGHEST, preferred_element_type=F32)
    if not convolve:
        o_ref = rest[0]
        o_ref[0, 0, 0] = x[:inner]
        o_ref[0, 1, 0] = x[inner:]
        return
    h_ref, gi_ref, o_ref = rest
    xr, xi = x[:inner], x[inner:]
    hr, hi = h_ref[0, 0, 0], h_ref[0, 1, 0]
    yr = xr * hr - xi * hi
    yi = xr * hi + xi * hr
    z = jnp.dot(gi_ref[...], jnp.concatenate([yr, yi], axis=0), precision=HIGHEST, preferred_element_type=F32)
    zr, zi = z[:inner], z[inner:]
    o_ref[0, 0, 0] = zr * tc - zi * ts
    o_ref[0, 1, 0] = zi * tc + zr * ts


def _fft_inner(a, tab, c, h=None, h_block=0):
    p = a.shape[0]
    n1, inner = tab['n1'], tab['inner']
    a5 = a.reshape(p, 2, n1, inner, c)
    tc = LANES
    blk = pl.BlockSpec((1, 2, 1, inner, tc), lambda i, k, j: (i, 0, k, 0, j))
    tw_spec = pl.BlockSpec((1, inner, 1), lambda i, k, j: (k, 0, 0))
    ins = [a5, tab['tw_cos'], tab['tw_sin'], tab['inner_fwd']]
    in_specs = [blk, tw_spec, tw_spec, _const_spec((2 * inner, 2 * inner))]
    if h is not None:
        ch = h.shape[-1] // inner
        nb = c // tc
        ins += [h.reshape(1, 2, n1, inner, ch), tab['inner_inv']]
        in_specs += [pl.BlockSpec((1, 2, 1, inner, tc), lambda i, k, j: (0, 0, k, 0, h_block * nb + j)),
                     _const_spec((2 * inner, 2 * inner))]
    out = pl.pallas_call(
        functools.partial(_inner_kernel, convolve=h is not None, inner=inner),
        grid=(p, n1, c // tc),
        in_specs=in_specs,
        out_specs=blk,
        out_shape=jax.ShapeDtypeStruct(a5.shape, F32),
        compiler_params=_params('parallel', 'parallel', 'parallel'),
        name='fft_inner',
    )(*ins)
    return out.reshape(p, 2 * n1, inner * c)


def _gate_kernel(m_ref, z_ref, u_ref, x_ref, bias_ref, *rest, chain):
    y = jnp.dot(m_ref[...], z_ref[0], precision=HIGHEST, preferred_element_type=F32)
    nxt = x_ref[0] * (y + u_ref[0] * bias_ref[...])
    if chain:
        mf_ref, o_ref, a_ref = rest
        o_ref[0] = nxt
        a_ref[0] = jnp.dot(mf_ref[...], nxt, precision=HIGHEST, preferred_element_type=F32)
    else:
        rest[0][0] = nxt


def _fft_gate(tab, z, u, x, bias_l, chain, tl=4096):
    p, k2, l = z.shape
    n1 = tab['n1']
    tl = min(tl, l)
    row = pl.BlockSpec((1, n1, tl), lambda i, j: (i, 0, j))
    ins = [tab['outer_inv'], z, u, x, bias_l]
    in_specs = [_const_spec((n1, k2)), pl.BlockSpec((1, k2, tl), lambda i, j: (i, 0, j)), row, row,
                pl.BlockSpec((1, tl), lambda i, j: (0, j))]
    out_specs = [row]
    out_shape = [jax.ShapeDtypeStruct((p, n1, l), F32)]
    if chain:
        ins.append(tab['outer_data'])
        in_specs.append(_const_spec((k2, n1)))
        out_specs.append(pl.BlockSpec((1, k2, tl), lambda i, j: (i, 0, j)))
        out_shape.append(jax.ShapeDtypeStruct((p, k2, l), F32))
    return pl.pallas_call(
        functools.partial(_gate_kernel, chain=chain),
        grid=(p, l // tl),
        in_specs=in_specs,
        out_specs=out_specs,
        out_shape=out_shape,
        compiler_params=_params('parallel', 'parallel'),
        name='fft_gate',
    )(*ins)


def _hyena(parts, conv_w, conv_b, filt_n, bias, inner):
    b, s, c = parts[0].shape
    tab = _fft_tables(s, inner)
    n1 = tab['n1']
    lanes = inner * c
    h_taps = _two_sided(filt_n).reshape(1, n1, inner * HY_ORDER * c)
    h_spec = _fft_inner(_left_mm(tab['outer_real'], h_taps), tab, HY_ORDER * c)
    v, x1, x2 = [a.reshape(b // 2, n1, lanes) for a in _short_conv(parts, conv_w, conv_b)]
    bias_l = [jnp.tile(bias[o], inner).reshape(1, lanes) for o in range(HY_ORDER)]
    a = _left_mm(tab['outer_data'], v)
    z = _fft_inner(a, tab, c, h_spec, 0)
    z2, a = _fft_gate(tab, z, v, x1, bias_l[0], chain=True)
    z = _fft_inner(a, tab, c, h_spec, 1)
    (z3,) = _fft_gate(tab, z, z2, x2, bias_l[1], chain=False)
    return z3.reshape(b, s, c)


def _attn_kernel(*refs, ncomp, scale, post_scale):
    if ncomp == 2:
        q_ref, k_ref, v_ref, lam_ref, g_ref, o_ref = refs
    else:
        q_ref, k_ref, v_ref, o_ref = refs
    v = v_ref[0]
    outs = []
    for c in range(ncomp):
        s = _dot_nt(q_ref[0, c], k_ref[0, c]) * scale
        m = jnp.max(s, axis=-1, keepdims=True)
        p = jnp.exp(s - m)
        l = jnp.sum(p, axis=-1, keepdims=True)
        outs.append(jnp.dot(p.astype(BF16), v, preferred_element_type=F32) / l)
    if ncomp == 2:
        o = outs[0] - lam_ref[0] * outs[1]
        o = _rms(o) * g_ref[...] * post_scale
    else:
        o = outs[0]
    o_ref[0] = o


def _attention(q, k, v, scale, lam=None, subln_g=None, post_scale=1.0, tq=256):
    g, ncomp, sq, d = q.shape
    sk, dv = v.shape[1], v.shape[2]
    tq = min(tq, sq)
    ins = [q, k, v]
    in_specs = [pl.BlockSpec((1, ncomp, tq, d), lambda i, j: (i, 0, j, 0)),
                pl.BlockSpec((1, ncomp, sk, d), lambda i, j: (i, 0, 0, 0)),
                pl.BlockSpec((1, sk, dv), lambda i, j: (i, 0, 0))]
    if ncomp == 2:
        ins += [lam.reshape(1), subln_g.reshape(1, dv)]
        in_specs += [pl.BlockSpec(memory_space=pltpu.SMEM), _const_spec((1, dv))]
    return pl.pallas_call(
        functools.partial(_attn_kernel, ncomp=ncomp, scale=scale, post_scale=post_scale),
        grid=(g, sq // tq),
        in_specs=in_specs,
        out_specs=pl.BlockSpec((1, tq, dv), lambda i, j: (i, j, 0)),
        out_shape=jax.ShapeDtypeStruct((g, sq, dv), F32),
        compiler_params=_params('parallel', 'parallel'),
        name='attention',
    )(*ins)


def _forget_terms(f, log_lb, log_1m_lb, one_m_lb):
    log_sig = jnp.minimum(f, 0.0) - jnp.log1p(jnp.exp(-jnp.abs(f)))
    b = log_1m_lb + log_sig
    log_g = jnp.maximum(log_lb, b) + jnp.log1p(jnp.exp(-jnp.abs(log_lb - b)))
    return log_g, one_m_lb * jax.nn.sigmoid(-f)


def _hg_chunk(q, k, v, lg, st, tri, rev):
    ck, sub = HG_CHUNK, HG_SUB
    cum = jnp.dot(tri, lg, precision=HIGHEST, preferred_element_type=F32)
    tot = cum[0:1] if rev else cum[ck - 1:ck]
    o_carry = _dot_nt((q * jnp.exp(cum)).astype(BF16), st.astype(BF16))
    kd = (k * jnp.exp(tot - cum)).astype(BF16)
    st_new = st * jnp.exp(tot) + jnp.dot(v.T.astype(BF16), kd, preferred_element_type=F32)
    r = lax.broadcasted_iota(jnp.int32, (sub * sub, 1), 0)
    shift = sub.bit_length() - 1
    t_of, s_of = jnp.right_shift(r, shift), jnp.bitwise_and(r, sub - 1)
    keep = (t_of <= s_of) if rev else (t_of >= s_of)
    ones = jnp.ones((k.shape[1], v.shape[1]), F32)
    sel_r = lax.broadcasted_iota(jnp.int32, (sub, sub * sub), 0)
    sel_c = lax.broadcasted_iota(jnp.int32, (sub, sub * sub), 1)
    sel = (jnp.right_shift(sel_c, shift) == sel_r).astype(F32)
    outs = []
    for i in range(ck // sub):
        lo, hi = i * sub, (i + 1) * sub
        qi, ki, vi, ci = q[lo:hi], k[lo:hi], v[lo:hi], cum[lo:hi]
        rows = [qi[t:t + 1] * ki * jnp.exp(jnp.minimum(ci[t:t + 1] - ci, 0.0)) for t in range(sub)]
        w = jnp.where(keep, jnp.concatenate(rows, axis=0), 0.0)
        sc = jnp.dot(w, ones, precision=HIGHEST, preferred_element_type=F32)
        oi = jnp.dot(sel, sc * jnp.concatenate([vi] * sub, axis=0), precision=HIGHEST, preferred_element_type=F32)
        far = (hi < ck) if rev else (lo > 0)
        if far:
            cb = cum[hi:hi + 1] if rev else cum[lo - 1:lo]
            ks, vs, cs = (k[hi:], v[hi:], cum[hi:]) if rev else (k[:lo], v[:lo], cum[:lo])
            qd = (qi * jnp.exp(ci - cb)).astype(BF16)
            kf = (ks * jnp.exp(cb - cs)).astype(BF16)
            oi = oi + jnp.dot(_dot_nt(qd, kf).astype(BF16), vs.astype(BF16), preferred_element_type=F32)
        outs.append(oi)
    return o_carry + jnp.concatenate(outs, axis=0), st_new


def _hgrn_kernel(q_ref, ff_ref, fb_ref, i_ref, qc_ref, ffc_ref, fbc_ref, ic_ref, lb_ref, g_ref, o_ref, oc_ref,
                 *, n_lat, n_ctx):
    ck = HG_CHUNK
    r = lax.broadcasted_iota(jnp.int32, (ck, ck), 0)
    c = lax.broadcasted_iota(jnp.int32, (ck, ck), 1)

    def run(rev):
        tri = (r <= c).astype(F32) if rev else (r >= c).astype(F32)
        d = 1 if rev else 0
        lbs = (lb_ref[0, d, 0:1, :], lb_ref[0, d, 1:2, :], lb_ref[0, d, 2:3, :])

        def sweep(qr, fr, ir, out, n, st):
            nc = n // ck

            def body(step, st):
                ci = (nc - 1 - step) if rev else step
                idx = pl.ds(pl.multiple_of(ci * ck, ck), ck)
                lg, kk = _forget_terms(fr[0, 0, idx, :], *lbs)
                o, st = _hg_chunk(qr[0, 0, idx, :], kk, ir[0, 0, idx, :], lg, st, tri, rev)
                if rev:
                    out[0, 0, idx, :] += o
                else:
                    out[0, 0, idx, :] = o
                return st

            return lax.fori_loop(0, nc, body, st)

        st = jnp.zeros((HG_VAL_DIM, HG_KEY_DIM), F32)
        st = sweep(qc_ref, fbc_ref if rev else ffc_ref, ic_ref, oc_ref, n_ctx, st)
        sweep(q_ref, fb_ref if rev else ff_ref, i_ref, o_ref, n_lat, st)

    run(False)
    run(True)
    o_ref[0, 0] = _rms(o_ref[0, 0]) * g_ref[...]
    oc_ref[0, 0] = _rms(oc_ref[0, 0]) * g_ref[...]


def _hgrn(q, ff, fb, iv, qc, ffc, fbc, ic, lb_terms, norm_g):
    b, h, n_lat, dk = q.shape
    n_ctx = qc.shape[2]
    lat = pl.BlockSpec((1, 1, n_lat, dk), lambda i, j: (i, j, 0, 0))
    ctx = pl.BlockSpec((1, 1, n_ctx, dk), lambda i, j: (i, j, 0, 0))
    return pl.pallas_call(
        functools.partial(_hgrn_kernel, n_lat=n_lat, n_ctx=n_ctx),
        grid=(b, h),
        in_specs=[lat] * 4 + [ctx] * 4 + [pl.BlockSpec((1, 2, 3, dk), lambda i, j: (j, 0, 0, 0)), _const_spec((1, dk))],
        out_specs=[lat, ctx],
        out_shape=[jax.ShapeDtypeStruct(q.shape, F32), jax.ShapeDtypeStruct(qc.shape, F32)],
        compiler_params=_params('parallel', 'parallel'),
        name='hgrn2',
    )(q, ff, fb, iv, qc, ffc, fbc, ic, lb_terms, norm_g.reshape(1, dk))


def _out_proj_kernel(hy_ref, da_ref, mla_ref, hg_ref, gate_ref, x_ref, g1_ref, w_ref, o_ref):
    hg = hg_ref[0] * _silu(gate_ref[0])
    acc = None
    for i, part in enumerate((hy_ref[0], da_ref[0], mla_ref[0], hg)):
        c = part.shape[1]
        term = jnp.dot(part.astype(BF16), w_ref[i * c:(i + 1) * c, :], preferred_element_type=F32)
        acc = term if acc is None else acc + term
    o_ref[0] = x_ref[0] + g1_ref[0] * acc


def _out_proj(y_hy, y_da, y_mla, y_hg, gate, x, g1, w_out, tm=512):
    b, s, d = x.shape
    tm = min(tm, s)
    c = y_hy.shape[2]
    part = pl.BlockSpec((1, tm, c), lambda i, j: (i, j, 0))
    row = pl.BlockSpec((1, tm, d), lambda i, j: (i, j, 0))
    return pl.pallas_call(
        _out_proj_kernel,
        grid=(b, s // tm),
        in_specs=[part] * 5 + [row, pl.BlockSpec((1, 1, d), lambda i, j: (i, 0, 0)), _const_spec(w_out.shape)],
        out_specs=row,
        out_shape=jax.ShapeDtypeStruct(x.shape, F32),
        compiler_params=_params('parallel', 'parallel'),
        name='out_proj',
    )(y_hy, y_da, y_mla, y_hg, gate, x, g1, w_out)


def _router_kernel(x_ref, g_ref, sc_ref, sh_ref, wrt_ref, bias_ref, h_ref, gate_ref):
    h = _rms(x_ref[0]) * g_ref[...] * (1.0 + sc_ref[0]) + sh_ref[0]
    h_ref[0] = h.astype(BF16)
    tm = h.shape[0]
    scores = jax.nn.sigmoid(_dot_nt(wrt_ref[...], h, precision=HIGHEST))
    choice = scores + bias_ref[...]
    per = N_EXPERTS // N_EXPERT_GROUPS
    neg = -jnp.inf
    iota_g = lax.broadcasted_iota(jnp.int32, (per, tm), 0)
    grp_rows = []
    for gi in range(N_EXPERT_GROUPS):
        blk = choice[gi * per:(gi + 1) * per]
        m1 = jnp.max(blk, axis=0, keepdims=True)
        first = jnp.min(jnp.where(blk == m1, iota_g, per), axis=0, keepdims=True)
        m2 = jnp.max(jnp.where(iota_g == first, neg, blk), axis=0, keepdims=True)
        grp_rows.append(m1 + m2)
    grp = jnp.concatenate(grp_rows, axis=0)
    iota_n = lax.broadcasted_iota(jnp.int32, (N_EXPERT_GROUPS, tm), 0)
    gsel = jnp.zeros((N_EXPERT_GROUPS, tm), F32)
    for _ in range(TOPK_GROUPS):
        m = jnp.max(grp, axis=0, keepdims=True)
        first = jnp.min(jnp.where(grp == m, iota_n, N_EXPERT_GROUPS), axis=0, keepdims=True)
        hit = iota_n == first
        gsel = jnp.where(hit, 1.0, gsel)
        grp = jnp.where(hit, neg, grp)
    emask = jnp.concatenate([jnp.broadcast_to(gsel[gi:gi + 1], (per, tm)) for gi in range(N_EXPERT_GROUPS)], axis=0)
    cand = jnp.where(emask > 0.0, choice, neg)
    iota_e = lax.broadcasted_iota(jnp.int32, (N_EXPERTS, tm), 0)
    sel = jnp.zeros((N_EXPERTS, tm), F32)
    for _ in range(TOP_K):
        m = jnp.max(cand, axis=0, keepdims=True)
        first = jnp.min(jnp.where(cand == m, iota_e, N_EXPERTS), axis=0, keepdims=True)
        hit = iota_e == first
        sel = jnp.where(hit, 1.0, sel)
        cand = jnp.where(hit, neg, cand)
    w = scores * sel
    gate_ref[0] = w / jnp.sum(w, axis=0, keepdims=True) * ROUTED_SCALE


def _router(x, g, scale, shift, w_router, e_bias, tm=512):
    b, s, d = x.shape
    tm = min(tm, s)
    e = w_router.shape[1]
    row = pl.BlockSpec((1, tm, d), lambda i, j: (i, j, 0))
    mod = pl.BlockSpec((1, 1, d), lambda i, j: (i, 0, 0))
    return pl.pallas_call(
        _router_kernel,
        grid=(b, s // tm),
        in_specs=[row, _const_spec((1, d)), mod, mod, _const_spec((e, d)), _const_spec((e, 1))],
        out_specs=[row, pl.BlockSpec((1, e, tm), lambda i, j: (i, 0, j))],
        out_shape=[jax.ShapeDtypeStruct((b, s, d), BF16), jax.ShapeDtypeStruct((b, e, s), F32)],
        compiler_params=_params('parallel', 'parallel'),
        name='router',
    )(x, g.reshape(1, d), scale, shift, w_router.T, e_bias.reshape(e, 1))


def _moe_kernel(h_ref, x_ref, gate_ref, g2_ref, wg_ref, wu_ref, wd_ref, sg_ref, su_ref, sd_ref, *rest, final):
    if final:
        fg_ref, o_ref, acc_ref = rest
    else:
        o_ref, acc_ref = rest
    e = pl.program_id(2)
    h = h_ref[0]

    @pl.when(e == 0)
    def _():
        a = jnp.dot(h, sg_ref[...], preferred_element_type=F32)
        u = jnp.dot(h, su_ref[...], preferred_element_type=F32)
        acc_ref[...] = jnp.dot((_silu(a) * u).astype(BF16), sd_ref[...], preferred_element_type=F32)

    lane = lax.broadcasted_iota(jnp.int32, gate_ref.shape[1:], 1)
    gcol = jnp.sum(jnp.where(lane == e, gate_ref[0], 0.0), axis=-1, keepdims=True)
    a = jnp.dot(h, wg_ref[0], preferred_element_type=F32)
    u = jnp.dot(h, wu_ref[0], preferred_element_type=F32)
    acc_ref[...] += jnp.dot((_silu(a) * u * gcol).astype(BF16), wd_ref[0], preferred_element_type=F32)

    @pl.when(e == pl.num_programs(2) - 1)
    def _():
        y = x_ref[0] + g2_ref[0] * acc_ref[...]
        if final:
            y = _rms(y) * fg_ref[...]
        o_ref[0] = y


def _moe(h2, x, gate, g2, w_gate, w_up, w_down, s_gate, s_up, s_down, final_g=None, tm=1024):
    b, s, d = x.shape
    tm = min(tm, s)
    e, _, ff = w_gate.shape
    row = pl.BlockSpec((1, tm, d), lambda i, j, k: (i, j, 0))
    ins = [h2, x, gate, g2, w_gate, w_up, w_down, s_gate, s_up, s_down]
    in_specs = [row, row, pl.BlockSpec((1, tm, e), lambda i, j, k: (i, j, 0)),
                pl.BlockSpec((1, 1, d), lambda i, j, k: (i, 0, 0)),
                pl.BlockSpec((1, d, ff), lambda i, j, k: (k, 0, 0)), pl.BlockSpec((1, d, ff), lambda i, j, k: (k, 0, 0)),
                pl.BlockSpec((1, ff, d), lambda i, j, k: (k, 0, 0)),
                _const_spec(s_gate.shape), _const_spec(s_up.shape), _const_spec(s_down.shape)]
    if final_g is not None:
        ins.append(final_g.reshape(1, d))
        in_specs.append(_const_spec((1, d)))
    return pl.pallas_call(
        functools.partial(_moe_kernel, final=final_g is not None),
        grid=(b, s // tm, e),
        in_specs=in_specs,
        out_specs=row,
        out_shape=jax.ShapeDtypeStruct(x.shape, F32),
        scratch_shapes=[pltpu.VMEM((tm, d), F32)],
        compiler_params=_params('parallel', 'parallel', 'arbitrary'),
        name='moe',
    )(*ins)


def _heads(a, n):
    b, s, _ = a.shape
    return a.reshape(b, s, n, -1).transpose(0, 2, 1, 3)


def _merge_heads(a):
    b, h, s, d = a.shape
    return a.transpose(0, 2, 1, 3).reshape(b, s, h * d)


def _rope_1d(x, pos):
    n = x.shape[-1] // 2
    inv = ROPE_BASE ** (-jnp.arange(n, dtype=F32) / n)
    ang = pos.astype(F32)[:, None] * inv
    cos, sin = jnp.cos(ang), jnp.sin(ang)
    x1, x2 = x[..., :n], x[..., n:]
    return jnp.concatenate([x1 * cos - x2 * sin, x2 * cos + x1 * sin], axis=-1)


def _rope_2d(x, row, col):
    h = x.shape[-1] // 2
    return jnp.concatenate([_rope_1d(x[..., :h], row), _rope_1d(x[..., h:], col)], axis=-1)


def _da_split(a, rope, row, col):
    b, s, _ = a.shape
    a = a.reshape(b, s, DA_HEADS, 2, DA_HEAD_DIM).transpose(0, 2, 3, 1, 4)
    return _rope_2d(a, row, col) if rope else a


def _mixers(p, pc, ctx_out, prm, l, lam_init, row, col, lb_terms):
    b, s, _ = p['da_q'].shape
    sc = pc['da_q'].shape[1]

    hy_args = (prm['hy_w1'][l], prm['hy_b1'][l], prm['hy_w2'][l], prm['hy_b2'][l], prm['hy_w3'][l], prm['hy_b3'][l],
               prm['hy_sin_freq'][l], prm['hy_decay'][l])
    y_hy = _hyena([p['hy_v'], p['hy_x1'], p['hy_x2']], prm['hy_conv_w'][l], prm['hy_conv_b'][l],
                  _hy_filters(s, *hy_args), prm['hy_bias'][l], inner=128)
    yc_hy = None
    if ctx_out:
        yc_hy = _hyena([pc['hy_v'], pc['hy_x1'], pc['hy_x2']], prm['hy_conv_w'][l], prm['hy_conv_b'][l],
                       _hy_filters(sc, *hy_args), prm['hy_bias'][l], inner=32)

    g_da = b * DA_HEADS
    q = _da_split(p['da_q'], True, row, col).reshape(g_da, 2, s, DA_HEAD_DIM)
    k = _da_split(p['da_k'], True, row, col).reshape(g_da, 2, s, DA_HEAD_DIM)
    kc = _da_split(pc['da_k'], False, row, col).reshape(g_da, 2, sc, DA_HEAD_DIM)
    vh = _heads(p['da_v'], DA_HEADS).reshape(g_da, s, 2 * DA_HEAD_DIM)
    vch = _heads(pc['da_v'], DA_HEADS).reshape(g_da, sc, 2 * DA_HEAD_DIM)
    keys = jnp.concatenate([kc, k], axis=2).astype(BF16)
    vals = jnp.concatenate([vch, vh], axis=1).astype(BF16)
    lp = prm['da_lambda'][l].astype(F32)
    lam = jnp.exp(jnp.sum(lp[0] * lp[1])) - jnp.exp(jnp.sum(lp[2] * lp[3])) + lam_init
    da_kw = dict(scale=DA_HEAD_DIM ** -0.5, lam=lam, subln_g=prm['da_subln_g'][l], post_scale=1.0 - lam_init)
    y_da = _merge_heads(_attention(q.astype(BF16), keys, vals, **da_kw).reshape(b, DA_HEADS, s, -1))
    yc_da = None
    if ctx_out:
        qc = _da_split(pc['da_q'], False, row, col).reshape(g_da, 2, sc, DA_HEAD_DIM)
        yc_da = _merge_heads(_attention(qc.astype(BF16), kc.astype(BF16), vch.astype(BF16), **da_kw)
                             .reshape(b, DA_HEADS, sc, -1))

    wq = prm['mla_w_q_up'][l].astype(BF16)
    wkv = prm['mla_w_kv_up'][l].astype(BF16)

    def queries(qd, rope):
        (qu,) = _norm_proj(qd, prm['mla_q_norm_g'][l], [wq])
        qh = _heads(qu, MLA_HEADS)
        qn, qr = qh[..., :MLA_NOPE_DIM], qh[..., MLA_NOPE_DIM:]
        if rope:
            qr = _rope_2d(qr, row, col)
        return jnp.concatenate([qn, qr], axis=-1)

    def keys_values(kvd, kr, rope):
        (kvu,) = _norm_proj(kvd, prm['mla_kv_norm_g'][l], [wkv])
        kvh = _heads(kvu, MLA_HEADS)
        kn, vv = kvh[..., :MLA_NOPE_DIM], kvh[..., MLA_NOPE_DIM:]
        if rope:
            kr = _rope_2d(kr, row, col)
        kr = jnp.broadcast_to(kr[:, None], kn.shape[:-1] + (MLA_ROPE_DIM,))
        return jnp.concatenate([kn, kr], axis=-1), vv

    g_m = b * MLA_HEADS
    dqk = MLA_NOPE_DIM + MLA_ROPE_DIM
    k_lat, v_lat = keys_values(p['mla_kv'], p['mla_kr'], True)
    k_ctx, v_ctx = keys_values(pc['mla_kv'], pc['mla_kr'], False)
    mk = jnp.concatenate([k_ctx, k_lat], axis=2).astype(BF16).reshape(g_m, 1, sc + s, dqk)
    mv = jnp.concatenate([v_ctx, v_lat], axis=2).astype(BF16).reshape(g_m, sc + s, MLA_V_DIM)
    mq = queries(p['mla_q'], True).astype(BF16).reshape(g_m, 1, s, dqk)
    y_mla = _merge_heads(_attention(mq, mk, mv, scale=dqk ** -0.5).reshape(b, MLA_HEADS, s, -1))
    yc_mla = None
    if ctx_out:
        mqc = queries(pc['mla_q'], False).astype(BF16).reshape(g_m, 1, sc, dqk)
        yc_mla = _merge_heads(_attention(mqc, k_ctx.astype(BF16).reshape(g_m, 1, sc, dqk),
                                         v_ctx.astype(BF16).reshape(g_m, sc, MLA_V_DIM), scale=dqk ** -0.5)
                              .reshape(b, MLA_HEADS, sc, -1))

    hg = lambda a: _heads(a, HG_HEADS)
    o, oc = _hgrn(hg(p['hg_q']), hg(p['hg_ff']), hg(p['hg_fb']), hg(p['hg_i']),
                  hg(pc['hg_q']), hg(pc['hg_ff']), hg(pc['hg_fb']), hg(pc['hg_i']), lb_terms, prm['hg_norm_g'][l])
    return (y_hy, y_da, y_mla, _merge_heads(o)), (yc_hy, yc_da, yc_mla, _merge_heads(oc))


def kernel(x, c, ctx, c_ctx, w_ada, b_ada, norm1_g, norm2_g, w_in, w_out, hy_conv_w, hy_conv_b, hy_w1, hy_b1, hy_w2, hy_b2, hy_w3, hy_b3, hy_sin_freq, hy_decay, hy_bias, da_lambda, da_subln_g, mla_q_norm_g, mla_w_q_up, mla_kv_norm_g, mla_w_kv_up, hg_lower_bounds, hg_norm_g, moe_w_router, moe_bias, moe_w_gate, moe_w_up, moe_w_down, moe_sh_gate, moe_sh_up, moe_sh_down, final_norm_g):
    prm = dict(hy_conv_w=hy_conv_w, hy_conv_b=hy_conv_b, hy_w1=hy_w1, hy_b1=hy_b1, hy_w2=hy_w2, hy_b2=hy_b2,
               hy_w3=hy_w3, hy_b3=hy_b3, hy_sin_freq=hy_sin_freq, hy_decay=hy_decay, hy_bias=hy_bias,
               da_lambda=da_lambda, da_subln_g=da_subln_g, mla_q_norm_g=mla_q_norm_g, mla_w_q_up=mla_w_q_up,
               mla_kv_norm_g=mla_kv_norm_g, mla_w_kv_up=mla_w_kv_up, hg_norm_g=hg_norm_g)
    b, n_lat, d = x.shape
    depth = w_in.shape[0]
    rows = n_lat // GRID_W
    row_pos = jnp.repeat(jnp.arange(rows, dtype=jnp.int32), GRID_W)
    col_pos = jnp.tile(jnp.arange(GRID_W, dtype=jnp.int32), rows)
    lbs = jnp.cumsum(jax.nn.softmax(hg_lower_bounds.astype(F32), axis=1), axis=1)
    lbs = lbs - lbs[:, :1]
    cond = jnp.concatenate([c, c_ctx[None], jnp.zeros((8 - b - 1, d), F32)], axis=0)

    for l in range(depth):
        ctx_out = l < depth - 1
        mods = _ada(cond, w_ada[l], b_ada[l])
        sh1, sc1, g1, sh2, sc2, g2 = [m[:, None, :] for m in jnp.split(mods[:b], 6, axis=-1)]
        mc = [jnp.broadcast_to(m[:, None, :], (b, 1, d)) for m in jnp.split(mods[b:b + 1], 6, axis=-1)]

        off = 0
        ws = []
        for _, wdt in _SEGMENTS:
            ws.append(w_in[l][:, off:off + wdt].astype(BF16))
            off += wdt
        names = [nm for nm, _ in _SEGMENTS]
        p = dict(zip(names, _norm_proj(x, norm1_g[l], ws, sc1, sh1)))
        pc = dict(zip(names, _norm_proj(ctx, norm1_g[l], ws, mc[1], mc[0])))

        lb = lbs[:, l].reshape(2, HG_HEADS, HG_KEY_DIM)
        lb_terms = jnp.stack([jnp.log(lb), jnp.log1p(-lb), 1.0 - lb], axis=2).transpose(1, 0, 2, 3)
        lam_init = 0.8 - 0.6 * math.exp(-0.3 * l)
        lat_parts, ctx_parts = _mixers(p, pc, ctx_out, prm, l, lam_init, row_pos, col_pos, lb_terms)

        w_out_b = w_out[l].astype(BF16)
        moe_w = (moe_w_gate[l].astype(BF16), moe_w_up[l].astype(BF16), moe_w_down[l].astype(BF16),
                 moe_sh_gate[l].astype(BF16), moe_sh_up[l].astype(BF16), moe_sh_down[l].astype(BF16))

        if ctx_out:
            ctx = _out_proj(*ctx_parts, pc['hg_g'], ctx, mc[2], w_out_b)
            h2c, gate_c = _router(ctx, norm2_g[l], mc[4], mc[3], moe_w_router[l], moe_bias[l])
            ctx = _moe(h2c, ctx, gate_c.transpose(0, 2, 1), mc[5], *moe_w)

        x = _out_proj(*lat_parts, p['hg_g'], x, g1, w_out_b)
        h2, gate = _router(x, norm2_g[l], sc2, sh2, moe_w_router[l], moe_bias[l])
        x = _moe(h2, x, gate.transpose(0, 2, 1), g2, *moe_w, final_g=None if ctx_out else final_norm_g)

    return x
```

```python
import functools
import math

import numpy as np
import jax
import jax.numpy as jnp
from jax import lax
from jax.experimental import pallas as pl
from jax.experimental.pallas import tpu as pltpu

F32 = jnp.float32
BF16 = jnp.bfloat16
HIGHEST = lax.Precision.HIGHEST

D_MODEL = 1024
GRID_W = 64
HY_WIDTH = 256
HY_ORDER = 2
HY_BANDS = 16
DA_HEADS = 4
DA_HEAD_DIM = 32
MLA_HEADS = 4
MLA_Q_RANK = 192
MLA_KV_RANK = 128
MLA_NOPE_DIM = 64
MLA_ROPE_DIM = 32
MLA_V_DIM = 64
HG_HEADS = 4
HG_KEY_DIM = 64
HG_VAL_DIM = 64
HG_CHUNK = 64
HG_SUB = 8
N_EXPERTS = 64
N_EXPERT_GROUPS = 8
TOPK_GROUPS = 4
TOP_K = 8
EXPERT_FF = 256
ROUTED_SCALE = 2.5
ROPE_BASE = 10000.0
NORM_EPS = 1e-6

V7X_VMEM_LIMIT_BYTES = 56 * 1024 * 1024
LANES = 128

_SEGMENTS = (
    ('hy_v', HY_WIDTH), ('hy_x1', HY_WIDTH), ('hy_x2', HY_WIDTH),
    ('da_q', 2 * DA_HEADS * DA_HEAD_DIM), ('da_k', 2 * DA_HEADS * DA_HEAD_DIM), ('da_v', 2 * DA_HEADS * DA_HEAD_DIM),
    ('mla_q', MLA_Q_RANK), ('mla_kv', MLA_KV_RANK), ('mla_kr', MLA_ROPE_DIM),
    ('hg_q', HG_HEADS * HG_KEY_DIM), ('hg_ff', HG_HEADS * HG_KEY_DIM), ('hg_fb', HG_HEADS * HG_KEY_DIM),
    ('hg_i', HG_HEADS * HG_VAL_DIM), ('hg_g', HG_HEADS * HG_VAL_DIM),
)


def _params(*semantics):
    return pltpu.CompilerParams(dimension_semantics=semantics, vmem_limit_bytes=V7X_VMEM_LIMIT_BYTES)


def _const_spec(shape):
    nd = len(shape)
    return pl.BlockSpec(shape, lambda *_: (0,) * nd)


def _rms(x, eps=NORM_EPS):
    return x * lax.rsqrt(jnp.mean(x * x, axis=-1, keepdims=True) + eps)


def _silu(x):
    return x * jax.nn.sigmoid(x)


def _dot_nt(a, b, **kw):
    return lax.dot_general(a, b, (((1,), (1,)), ((), ())), preferred_element_type=F32, **kw)


def _ada_kernel(c_ref, w_ref, b_ref, o_ref):
    s = _silu(c_ref[...])
    o_ref[...] = jnp.dot(s, w_ref[...], precision=HIGHEST, preferred_element_type=F32) + b_ref[...]


def _ada(cond, w, b):
    r, d = cond.shape
    n = w.shape[1]
    tn = 1536
    return pl.pallas_call(
        _ada_kernel,
        grid=(n // tn,),
        in_specs=[_const_spec((r, d)), pl.BlockSpec((d, tn), lambda j: (0, j)), pl.BlockSpec((1, tn), lambda j: (0, j))],
        out_specs=pl.BlockSpec((r, tn), lambda j: (0, j)),
        out_shape=jax.ShapeDtypeStruct((r, n), F32),
        compiler_params=_params('arbitrary'),
        name='ada',
    )(cond, w, b.reshape(1, n))


def _norm_proj_kernel(*refs, n_w, modulate):
    x_ref, g_ref = refs[0], refs[1]
    pos = 2
    if modulate:
        sc_ref, sh_ref = refs[2], refs[3]
        pos = 4
    w_refs = refs[pos:pos + n_w]
    o_refs = refs[pos + n_w:]
    y = _rms(x_ref[0]) * g_ref[...]
    if modulate:
        y = y * (1.0 + sc_ref[0]) + sh_ref[0]
    yb = y.astype(BF16)
    for w_ref, o_ref in zip(w_refs, o_refs):
        o_ref[0] = jnp.dot(yb, w_ref[...], preferred_element_type=F32).astype(o_ref.dtype)


def _norm_proj(x, g, ws, scale=None, shift=None, tm=512):
    b, s, k = x.shape
    tm = min(tm, s)
    modulate = scale is not None
    ins = [x, g.reshape(1, k)]
    in_specs = [pl.BlockSpec((1, tm, k), lambda i, j: (i, j, 0)), _const_spec((1, k))]
    if modulate:
        ins += [scale, shift]
        in_specs += [pl.BlockSpec((1, 1, k), lambda i, j: (i, 0, 0))] * 2
    for w in ws:
        ins.append(w)
        in_specs.append(_const_spec(w.shape))
    return pl.pallas_call(
        functools.partial(_norm_proj_kernel, n_w=len(ws), modulate=modulate),
        grid=(b, s // tm),
        in_specs=in_specs,
        out_specs=[pl.BlockSpec((1, tm, w.shape[1]), lambda i, j: (i, j, 0)) for w in ws],
        out_shape=[jax.ShapeDtypeStruct((b, s, w.shape[1]), F32) for w in ws],
        compiler_params=_params('parallel', 'parallel'),
        name='norm_proj',
    )(*ins)


def _hy_filter_kernel(w1t_ref, w1s_ref, w1c_ref, b1_ref, w2_ref, b2_ref, w3_ref, b3_ref, fr_ref, dec_ref, o_ref, *, n):
    t = lax.broadcasted_iota(jnp.int32, (n, 1), 0).astype(F32) / n
    bands = lax.broadcasted_iota(jnp.int32, (1, HY_BANDS), 1).astype(F32) + 1.0
    ang = (2.0 * jnp.pi) * t * bands
    pre = (t * w1t_ref[...]
           + jnp.dot(jnp.sin(ang), w1s_ref[...], precision=HIGHEST, preferred_element_type=F32)
           + jnp.dot(jnp.cos(ang), w1c_ref[...], precision=HIGHEST, preferred_element_type=F32)
           + b1_ref[...])
    hid = jnp.sin(fr_ref[0:1, :] * pre)
    hid = jnp.sin(fr_ref[1:2, :] * (jnp.dot(hid, w2_ref[...], precision=HIGHEST, preferred_element_type=F32) + b2_ref[...]))
    filt = jnp.dot(hid, w3_ref[...], precision=HIGHEST, preferred_element_type=F32) + b3_ref[...]
    filt = filt * jnp.exp(-t * jnp.abs(dec_ref[...]))
    col = jnp.sum(jnp.abs(filt), axis=0, keepdims=True) - jnp.abs(filt[0:1, :])
    w = HY_WIDTH
    for o in range(HY_ORDER):
        lo = o * 2 * w
        f0 = filt[0:1, lo:lo + w] + filt[0:1, lo + w:lo + 2 * w]
        inv = 1.0 / (col[:, lo:lo + w] + col[:, lo + w:lo + 2 * w] + jnp.abs(f0))
        o_ref[:, lo:lo + w] = filt[:, lo:lo + w] * inv
        o_ref[:, lo + w:lo + 2 * w] = filt[:, lo + w:lo + 2 * w] * inv


def _hy_filters(n, w1, b1, w2, b2, w3, b3, freq, decay):
    cols = w3.shape[1]
    ins = [w1[0:1], w1[1:1 + HY_BANDS], w1[1 + HY_BANDS:], b1.reshape(1, -1), w2, b2.reshape(1, -1), w3,
           b3.reshape(1, -1), freq, decay.reshape(1, -1)]
    out = pl.pallas_call(
        functools.partial(_hy_filter_kernel, n=n),
        grid=(1,),
        in_specs=[_const_spec(a.shape) for a in ins],
        out_specs=_const_spec((n, cols)),
        out_shape=jax.ShapeDtypeStruct((n, cols), F32),
        compiler_params=_params('arbitrary'),
        name='hy_filter',
    )(*ins)
    return out.reshape(n, HY_ORDER, 2, HY_WIDTH)


def _two_sided(filt_n):
    n = filt_n.shape[0]
    hf, hb = filt_n[:, :, 0], filt_n[:, :, 1]
    h = jnp.concatenate([hf[:1] + hb[:1], hf[1:], jnp.zeros((1,) + hf.shape[1:], F32), hb[:0:-1]], axis=0)
    return h.reshape(2 * n, HY_ORDER * HY_WIDTH)


def _short_conv_kernel(*refs, s):
    x_refs, w_refs, b_refs, o_refs = refs[0:3], refs[3:6], refs[6:9], refs[9:12]
    row = lax.broadcasted_iota(jnp.int32, (s, 1), 0)
    for x_ref, w_ref, b_ref, o_ref in zip(x_refs, w_refs, b_refs, o_refs):
        x = x_ref[0]
        prev = jnp.where(row == 0, 0.0, pltpu.roll(x, 1, axis=0))
        nxt = jnp.where(row == s - 1, 0.0, pltpu.roll(x, s - 1, axis=0))
        o_ref[0] = prev * w_ref[0:1, :] + x * w_ref[1:2, :] + nxt * w_ref[2:3, :] + b_ref[...]


def _short_conv(parts, conv_w, conv_b):
    b, s, c = parts[0].shape
    tc = LANES
    ws = [conv_w[:, i * c:(i + 1) * c] for i in range(3)]
    bs = [conv_b[i * c:(i + 1) * c].reshape(1, c) for i in range(3)]
    xspec = pl.BlockSpec((1, s, tc), lambda i, j: (i, 0, j))
    return pl.pallas_call(
        functools.partial(_short_conv_kernel, s=s),
        grid=(b, c // tc),
        in_specs=[xspec] * 3 + [pl.BlockSpec((3, tc), lambda i, j: (0, j))] * 3 + [pl.BlockSpec((1, tc), lambda i, j: (0, j))] * 3,
        out_specs=[xspec] * 3,
        out_shape=[jax.ShapeDtypeStruct((b, s, c), F32)] * 3,
        compiler_params=_params('parallel', 'parallel'),
        name='short_conv',
    )(*parts, *ws, *bs)


def _dft_cos_sin(rows, cols, period):
    ang = 2.0 * np.pi * ((np.arange(rows)[:, None] * np.arange(cols)[None, :]) % period) / period
    return np.cos(ang), np.sin(ang)


def _fft_tables(n, inner):
    big = 2 * n
    n1 = big // inner
    c1, s1 = _dft_cos_sin(n1, n1, n1)
    h = n1 // 2
    outer_data = np.block([[c1[:, :h], s1[:, :h]], [-s1[:, :h], c1[:, :h]]])
    outer_real = np.concatenate([c1, -s1], axis=0)
    outer_inv = np.block([[c1[:h, :], -s1[:h, :]], [s1[:h, :], c1[:h, :]]]) / big
    c2, s2 = _dft_cos_sin(inner, inner, inner)
    inner_fwd = np.block([[c2, s2], [-s2, c2]])
    inner_inv = np.block([[c2, -s2], [s2, c2]])
    ct, st = _dft_cos_sin(n1, inner, big)
    f = lambda a: jnp.asarray(a, F32)
    return dict(n1=n1, inner=inner, outer_data=f(outer_data), outer_real=f(outer_real), outer_inv=f(outer_inv),
                inner_fwd=_hi_lo_cols(inner_fwd), inner_inv=_hi_lo_cols(inner_inv),
                tw_cos=f(ct).reshape(n1, inner, 1), tw_sin=f(st).reshape(n1, inner, 1))


def _left_mm_kernel(m_ref, x_ref, o_ref):
    o_ref[0] = jnp.dot(m_ref[...], x_ref[0], precision=HIGHEST, preferred_element_type=F32)


def _left_mm(m, x, tl=4096):
    p, k, l = x.shape
    mm = m.shape[0]
    tl = min(tl, l)
    return pl.pallas_call(
        _left_mm_kernel,
        grid=(p, l // tl),
        in_specs=[_const_spec(m.shape), pl.BlockSpec((1, k, tl), lambda i, j: (i, 0, j))],
        out_specs=pl.BlockSpec((1, mm, tl), lambda i, j: (i, 0, j)),
        out_shape=jax.ShapeDtypeStruct((p, mm, l), F32),
        compiler_params=_params('parallel', 'parallel'),
        name='fft_outer',
    )(m, x)


def _hi_lo_cols(m):
    m = np.asarray(m, np.float32)
    hi = m.astype(BF16)
    lo = (m - hi.astype(np.float32)).astype(BF16)
    return jnp.asarray(np.concatenate([hi, hi, lo], axis=1))


def _hi_lo_rows(x):
    hi = x.astype(BF16)
    lo = (x - hi.astype(F32)).astype(BF16)
    return jnp.concatenate([hi, lo, hi], axis=0)


def _inner_kernel(a_ref, twc_ref, tws_ref, gf_ref, *rest, convolve, inner, kb):
    for s in range(kb):
        ar, ai = a_ref[0, 0, s], a_ref[0, 1, s]
        tc, ts = twc_ref[s], tws_ref[s]
        br = ar * tc + ai * ts
        bi = ai * tc - ar * ts
        x = jnp.dot(gf_ref[...], _hi_lo_rows(jnp.concatenate([br, bi], axis=0)), preferred_element_type=F32)
        if not convolve:
            o_ref = rest[0]
            o_ref[0, 0, s] = x[:inner]
            o_ref[0, 1, s] = x[inner:]
            continue
        h_ref, gi_ref, o_ref = rest
        xr, xi = x[:inner], x[inner:]
        hr, hi = h_ref[0, 0, s], h_ref[0, 1, s]
        yr = xr * hr - xi * hi
        yi = xr * hi + xi * hr
        z = jnp.dot(gi_ref[...], _hi_lo_rows(jnp.concatenate([yr, yi], axis=0)), preferred_element_type=F32)
        zr, zi = z[:inner], z[inner:]
        o_ref[0, 0, s] = zr * tc - zi * ts
        o_ref[0, 1, s] = zi * tc + zr * ts


def _fft_inner(a, tab, c, h=None, h_block=0):
    p = a.shape[0]
    n1, inner = tab['n1'], tab['inner']
    a5 = a.reshape(p, 2, n1, inner, c)
    tc = 2 * LANES
    kb = 4
    blk = pl.BlockSpec((1, 2, kb, inner, tc), lambda i, k, j: (i, 0, k, 0, j))
    tw_spec = pl.BlockSpec((kb, inner, 1), lambda i, k, j: (k, 0, 0))
    ins = [a5, tab['tw_cos'], tab['tw_sin'], tab['inner_fwd']]
    in_specs = [blk, tw_spec, tw_spec, _const_spec(tab['inner_fwd'].shape)]
    if h is not None:
        ch = h.shape[-1] // inner
        nb = c // tc
        ins += [h.reshape(1, 2, n1, inner, ch), tab['inner_inv']]
        in_specs += [pl.BlockSpec((1, 2, kb, inner, tc), lambda i, k, j: (0, 0, k, 0, h_block * nb + j)),
                     _const_spec(tab['inner_inv'].shape)]
    out = pl.pallas_call(
        functools.partial(_inner_kernel, convolve=h is not None, inner=inner, kb=kb),
        grid=(p, n1 // kb, c // tc),
        in_specs=in_specs,
        out_specs=blk,
        out_shape=jax.ShapeDtypeStruct(a5.shape, F32),
        compiler_params=_params('parallel', 'parallel', 'parallel'),
        name='fft_inner',
    )(*ins)
    return out.reshape(p, 2 * n1, inner * c)


def _gate_kernel(m_ref, z_ref, u_ref, x_ref, bias_ref, *rest, chain):
    y = jnp.dot(m_ref[...], z_ref[0], precision=HIGHEST, preferred_element_type=F32)
    nxt = x_ref[0] * (y + u_ref[0] * bias_ref[...])
    if chain:
        mf_ref, o_ref, a_ref = rest
        o_ref[0] = nxt
        a_ref[0] = jnp.dot(mf_ref[...], nxt, precision=HIGHEST, preferred_element_type=F32)
    else:
        rest[0][0] = nxt


def _fft_gate(tab, z, u, x, bias_l, chain, tl=4096):
    p, k2, l = z.shape
    n1 = tab['n1']
    tl = min(tl, l)
    row = pl.BlockSpec((1, n1, tl), lambda i, j: (i, 0, j))
    ins = [tab['outer_inv'], z, u, x, bias_l]
    in_specs = [_const_spec((n1, k2)), pl.BlockSpec((1, k2, tl), lambda i, j: (i, 0, j)), row, row,
                pl.BlockSpec((1, tl), lambda i, j: (0, j))]
    out_specs = [row]
    out_shape = [jax.ShapeDtypeStruct((p, n1, l), F32)]
    if chain:
        ins.append(tab['outer_data'])
        in_specs.append(_const_spec((k2, n1)))
        out_specs.append(pl.BlockSpec((1, k2, tl), lambda i, j: (i, 0, j)))
        out_shape.append(jax.ShapeDtypeStruct((p, k2, l), F32))
    return pl.pallas_call(
        functools.partial(_gate_kernel, chain=chain),
        grid=(p, l // tl),
        in_specs=in_specs,
        out_specs=out_specs,
        out_shape=out_shape,
        compiler_params=_params('parallel', 'parallel'),
        name='fft_gate',
    )(*ins)


def _hyena(parts, conv_w, conv_b, filt_n, bias, inner):
    b, s, c = parts[0].shape
    tab = _fft_tables(s, inner)
    n1 = tab['n1']
    lanes = inner * c
    h_taps = _two_sided(filt_n).reshape(1, n1, inner * HY_ORDER * c)
    h_spec = _fft_inner(_left_mm(tab['outer_real'], h_taps), tab, HY_ORDER * c)
    v, x1, x2 = [a.reshape(b // 2, n1, lanes) for a in _short_conv(parts, conv_w, conv_b)]
    bias_l = [jnp.tile(bias[o], inner).reshape(1, lanes) for o in range(HY_ORDER)]
    a = _left_mm(tab['outer_data'], v)
    z = _fft_inner(a, tab, c, h_spec, 0)
    z2, a = _fft_gate(tab, z, v, x1, bias_l[0], chain=True)
    z = _fft_inner(a, tab, c, h_spec, 1)
    (z3,) = _fft_gate(tab, z, z2, x2, bias_l[1], chain=False)
    return z3.reshape(b, s, c)


def _attn_kernel(*refs, ncomp, scale, post_scale):
    if ncomp == 2:
        q_ref, k_ref, v_ref, lam_ref, g_ref, o_ref = refs
    else:
        q_ref, k_ref, v_ref, o_ref = refs
    v = v_ref[0]
    outs = []
    for c in range(ncomp):
        qs = (q_ref[0, c] * (scale * math.log2(math.e))).astype(BF16)
        s = _dot_nt(qs, k_ref[0, c])
        m = jnp.max(s, axis=-1, keepdims=True)
        p = jnp.exp2(s - m)
        l = jnp.sum(p, axis=-1, keepdims=True)
        outs.append(jnp.dot(p.astype(BF16), v, preferred_element_type=F32) / l)
    if ncomp == 2:
        o = outs[0] - lam_ref[0] * outs[1]
        o = _rms(o) * g_ref[...] * post_scale
    else:
        o = outs[0]
    o_ref[0] = o


def _attention(q, k, v, scale, lam=None, subln_g=None, post_scale=1.0, tq=256):
    g, ncomp, sq, d = q.shape
    sk, dv = v.shape[1], v.shape[2]
    tq = min(tq, sq)
    ins = [q, k, v]
    in_specs = [pl.BlockSpec((1, ncomp, tq, d), lambda i, j: (i, 0, j, 0)),
                pl.BlockSpec((1, ncomp, sk, d), lambda i, j: (i, 0, 0, 0)),
                pl.BlockSpec((1, sk, dv), lambda i, j: (i, 0, 0))]
    if ncomp == 2:
        ins += [lam.reshape(1), subln_g.reshape(1, dv)]
        in_specs += [pl.BlockSpec(memory_space=pltpu.SMEM), _const_spec((1, dv))]
    return pl.pallas_call(
        functools.partial(_attn_kernel, ncomp=ncomp, scale=scale, post_scale=post_scale),
        grid=(g, sq // tq),
        in_specs=in_specs,
        out_specs=pl.BlockSpec((1, tq, dv), lambda i, j: (i, j, 0)),
        out_shape=jax.ShapeDtypeStruct((g, sq, dv), F32),
        compiler_params=_params('parallel', 'parallel'),
        name='attention',
    )(*ins)


def _forget_terms(f, log_lb, log_1m_lb, one_m_lb):
    log_sig = jnp.minimum(f, 0.0) - jnp.log1p(jnp.exp(-jnp.abs(f)))
    b = log_1m_lb + log_sig
    log_g = jnp.maximum(log_lb, b) + jnp.log1p(jnp.exp(-jnp.abs(log_lb - b)))
    return log_g, one_m_lb * jax.nn.sigmoid(-f)


def _hg_tables():
    ck, sub = HG_CHUNK, HG_SUB
    t = np.arange(ck)
    cum_mats, half_masks, group_masks, sels = [], [], [], []
    for rev in (False, True):
        mats = [(t[None, :] >= t[:, None]) if rev else (t[None, :] <= t[:, None])]
        halves = []
        hs = ck // 2
        while hs >= sub:
            pos = t % (2 * hs)
            b = t - pos + hs
            mats.append((t[None, :] >= b[:, None]) if rev else (t[None, :] < b[:, None]))
            q_half = (pos < hs) if rev else (pos >= hs)
            halves.append(np.stack([q_half, ~q_half]))
            if not rev:
                grp = (t[:, None] // (2 * hs)) == (t[None, :] // (2 * hs))
                group_masks.append(np.concatenate([grp, grp], axis=0))
            hs //= 2
        cum_mats.append(np.concatenate(mats, axis=0))
        half_masks.append(np.stack(halves))
        r, c = np.arange(ck)[:, None], np.arange(ck * sub)[None, :]
        same = (c // sub) == r
        tt, ss = (c // sub) % sub, c % sub
        sels.append(same & ((ss >= tt) if rev else (ss <= tt)))
    lanes = 2 * HG_KEY_DIM
    ln = np.arange(lanes)
    bd = (ln[:, None] // HG_KEY_DIM) == (ln[None, :] // HG_KEY_DIM)
    hm = np.broadcast_to(np.stack(half_masks)[..., None], (2, len(half_masks[0]), 2, ck, lanes))
    return (jnp.asarray(np.stack(cum_mats), BF16), jnp.asarray(hm, F32), jnp.asarray(np.stack(group_masks), F32),
            jnp.asarray(np.stack(sels), BF16), jnp.asarray(bd, F32))


def _split3(x):
    a = x.astype(BF16)
    r = x - a.astype(F32)
    b = r.astype(BF16)
    return a, b, (r - b.astype(F32)).astype(BF16)


def _hg_chunk(q, k, v, lg, st, rev, cm, hm, gm, sel, bd, m0, m1):
    ck, sub = HG_CHUNK, HG_SUB
    call = sum(jnp.dot(cm, piece, preferred_element_type=F32) for piece in _split3(lg))
    cum = call[0:ck]
    tot = cum[0:1] if rev else cum[ck - 1:ck]
    o = _dot_nt((q * jnp.exp(cum)).astype(BF16), st.astype(BF16))
    kd = (k * jnp.exp(tot - cum)).astype(BF16)
    st_new = st * jnp.exp(tot) + bd * jnp.dot(v.T.astype(BF16), kd, preferred_element_type=F32)
    s2 = None
    for lv in range(gm.shape[0]):
        cb = call[(lv + 1) * ck:(lv + 2) * ck]
        qd = q * jnp.exp(jnp.minimum(cum - cb, 0.0)) * hm[lv, 0]
        kf = (k * jnp.exp(jnp.minimum(cb - cum, 0.0)) * hm[lv, 1]).astype(BF16)
        q2 = jnp.concatenate([qd * m0, qd * m1], axis=0).astype(BF16)
        term = _dot_nt(q2, kf) * gm[lv]
        s2 = term if s2 is None else s2 + term
    r = jnp.dot(s2.astype(BF16), v.astype(BF16), preferred_element_type=F32)
    o = o + m0 * r[:ck] + m1 * r[ck:]
    rows, vts = [], []
    for i in range(ck // sub):
        lo, hi = i * sub, (i + 1) * sub
        ki, ci = k[lo:hi], cum[lo:hi]
        for t in range(lo, hi):
            rows.append((q[t:t + 1] * ki * jnp.exp(jnp.minimum(cum[t:t + 1] - ci, 0.0))).astype(BF16))
            vts.append(v[lo:hi])
    sc = jnp.dot(jnp.concatenate(rows, axis=0), bd.astype(BF16), preferred_element_type=F32)
    o = o + jnp.dot(sel, (sc * jnp.concatenate(vts, axis=0)).astype(BF16), preferred_element_type=F32)
    return o, st_new


def _hgrn_kernel(q_ref, ff_ref, fb_ref, i_ref, qc_ref, ffc_ref, fbc_ref, ic_ref, lb_ref, g_ref,
                 cm_ref, hm_ref, gm_ref, sel_ref, bd_ref, o_ref, oc_ref, or_ref, ocr_ref, st_ref, *, n_lat, n_ctx):
    ck = HG_CHUNK
    lanes = o_ref.shape[-1]
    lane = lax.broadcasted_iota(jnp.int32, (1, lanes), 1)
    m0 = (lane < HG_KEY_DIM).astype(F32)
    m1 = 1.0 - m0
    bd = bd_ref[...]
    gm = gm_ref[...]

    def one(q, f, v, rev):
        d = 1 if rev else 0
        lg, k = _forget_terms(f, lb_ref[d, 0:1, :], lb_ref[d, 1:2, :], lb_ref[d, 2:3, :])
        o, st = _hg_chunk(q, k, v, lg, st_ref[d], rev, cm_ref[d], hm_ref[d], gm, sel_ref[d], bd, m0, m1)
        st_ref[d] = st
        return o

    def sweep(qr, ffr, fbr, ir, out_f, out_r, n):
        nc = n // ck

        def body(step, carry):
            idf = pl.ds(pl.multiple_of(step * ck, ck), ck)
            idr = pl.ds(pl.multiple_of((nc - 1 - step) * ck, ck), ck)
            out_f[0, idf, :] = one(qr[0, idf, :], ffr[0, idf, :], ir[0, idf, :], False)
            out_r[idr, :] = one(qr[0, idr, :], fbr[0, idr, :], ir[0, idr, :], True)
            return carry

        lax.fori_loop(0, nc, body, 0)

    st_ref[...] = jnp.zeros(st_ref.shape, F32)
    sweep(qc_ref, ffc_ref, fbc_ref, ic_ref, oc_ref, ocr_ref, n_ctx)
    sweep(q_ref, ff_ref, fb_ref, i_ref, o_ref, or_ref, n_lat)

    mean_mat = bd * (1.0 / HG_VAL_DIM)

    def readout(out, out_r, n):
        tile = min(n, 512)

        def body(step, carry):
            idx = pl.ds(pl.multiple_of(step * tile, tile), tile)
            x = out[0, idx, :] + out_r[idx, :]
            ms = jnp.dot(x * x, mean_mat, precision=HIGHEST, preferred_element_type=F32)
            out[0, idx, :] = x * lax.rsqrt(ms + NORM_EPS) * g_ref[...]
            return carry

        lax.fori_loop(0, n // tile, body, 0)

    readout(oc_ref, ocr_ref, n_ctx)
    readout(o_ref, or_ref, n_lat)


def _hgrn(q, ff, fb, iv, qc, ffc, fbc, ic, lb_terms, norm_g):
    b, n_lat, width = q.shape
    n_ctx = qc.shape[1]
    lanes = 2 * HG_KEY_DIM
    tables = _hg_tables()
    lat = pl.BlockSpec((1, n_lat, lanes), lambda i, j: (i, 0, j))
    ctx = pl.BlockSpec((1, n_ctx, lanes), lambda i, j: (i, 0, j))
    g2 = jnp.tile(norm_g, 2).reshape(1, lanes)
    return pl.pallas_call(
        functools.partial(_hgrn_kernel, n_lat=n_lat, n_ctx=n_ctx),
        grid=(b, width // lanes),
        in_specs=[lat] * 4 + [ctx] * 4 + [pl.BlockSpec((2, 3, lanes), lambda i, j: (0, 0, j)), _const_spec((1, lanes))]
                 + [_const_spec(t.shape) for t in tables],
        out_specs=[lat, ctx],
        out_shape=[jax.ShapeDtypeStruct(q.shape, F32), jax.ShapeDtypeStruct(qc.shape, F32)],
        scratch_shapes=[pltpu.VMEM((n_lat, lanes), F32), pltpu.VMEM((n_ctx, lanes), F32), pltpu.VMEM((2, lanes, lanes), F32)],
        compiler_params=_params('parallel', 'parallel'),
        name='hgrn2',
    )(q, ff, fb, iv, qc, ffc, fbc, ic, lb_terms, g2, *tables)


def _out_proj_kernel(hy_ref, da_ref, mla_ref, hg_ref, gate_ref, x_ref, g1_ref, w_ref, o_ref):
    hg = hg_ref[0] * _silu(gate_ref[0])
    acc = None
    for i, part in enumerate((hy_ref[0], da_ref[0], mla_ref[0], hg)):
        c = part.shape[1]
        term = jnp.dot(part.astype(BF16), w_ref[i * c:(i + 1) * c, :], preferred_element_type=F32)
        acc = term if acc is None else acc + term
    o_ref[0] = x_ref[0] + g1_ref[0] * acc


def _out_proj(y_hy, y_da, y_mla, y_hg, gate, x, g1, w_out, tm=512):
    b, s, d = x.shape
    tm = min(tm, s)
    c = y_hy.shape[2]
    part = pl.BlockSpec((1, tm, c), lambda i, j: (i, j, 0))
    row = pl.BlockSpec((1, tm, d), lambda i, j: (i, j, 0))
    return pl.pallas_call(
        _out_proj_kernel,
        grid=(b, s // tm),
        in_specs=[part] * 5 + [row, pl.BlockSpec((1, 1, d), lambda i, j: (i, 0, 0)), _const_spec(w_out.shape)],
        out_specs=row,
        out_shape=jax.ShapeDtypeStruct(x.shape, F32),
        compiler_params=_params('parallel', 'parallel'),
        name='out_proj',
    )(y_hy, y_da, y_mla, y_hg, gate, x, g1, w_out)


def _router_kernel(x_ref, g_ref, sc_ref, sh_ref, wrt_ref, bias_ref, h_ref, gate_ref):
    h = _rms(x_ref[0]) * g_ref[...] * (1.0 + sc_ref[0]) + sh_ref[0]
    h_ref[0] = h.astype(BF16)
    tm = h.shape[0]
    scores = jax.nn.sigmoid(_dot_nt(wrt_ref[...], h, precision=HIGHEST))
    choice = scores + bias_ref[...]
    per = N_EXPERTS // N_EXPERT_GROUPS
    neg = -jnp.inf
    iota_g = lax.broadcasted_iota(jnp.int32, (per, tm), 0)
    grp_rows = []
    for gi in range(N_EXPERT_GROUPS):
        blk = choice[gi * per:(gi + 1) * per]
        m1 = jnp.max(blk, axis=0, keepdims=True)
        first = jnp.min(jnp.where(blk == m1, iota_g, per), axis=0, keepdims=True)
        m2 = jnp.max(jnp.where(iota_g == first, neg, blk), axis=0, keepdims=True)
        grp_rows.append(m1 + m2)
    grp = jnp.concatenate(grp_rows, axis=0)
    iota_n = lax.broadcasted_iota(jnp.int32, (N_EXPERT_GROUPS, tm), 0)
    gsel = jnp.zeros((N_EXPERT_GROUPS, tm), F32)
    for _ in range(TOPK_GROUPS):
        m = jnp.max(grp, axis=0, keepdims=True)
        first = jnp.min(jnp.where(grp == m, iota_n, N_EXPERT_GROUPS), axis=0, keepdims=True)
        hit = iota_n == first
        gsel = jnp.where(hit, 1.0, gsel)
        grp = jnp.where(hit, neg, grp)
    emask = jnp.concatenate([jnp.broadcast_to(gsel[gi:gi + 1], (per, tm)) for gi in range(N_EXPERT_GROUPS)], axis=0)
    cand = jnp.where(emask > 0.0, choice, neg)
    iota_e = lax.broadcasted_iota(jnp.int32, (N_EXPERTS, tm), 0)
    sel = jnp.zeros((N_EXPERTS, tm), F32)
    for _ in range(TOP_K):
        m = jnp.max(cand, axis=0, keepdims=True)
        first = jnp.min(jnp.where(cand == m, iota_e, N_EXPERTS), axis=0, keepdims=True)
        hit = iota_e == first
        sel = jnp.where(hit, 1.0, sel)
        cand = jnp.where(hit, neg, cand)
    w = scores * sel
    gate_ref[0] = w / jnp.sum(w, axis=0, keepdims=True) * ROUTED_SCALE


def _router(x, g, scale, shift, w_router, e_bias, tm=512):
    b, s, d = x.shape
    tm = min(tm, s)
    e = w_router.shape[1]
    row = pl.BlockSpec((1, tm, d), lambda i, j: (i, j, 0))
    mod = pl.BlockSpec((1, 1, d), lambda i, j: (i, 0, 0))
    return pl.pallas_call(
        _router_kernel,
        grid=(b, s // tm),
        in_specs=[row, _const_spec((1, d)), mod, mod, _const_spec((e, d)), _const_spec((e, 1))],
        out_specs=[row, pl.BlockSpec((1, e, tm), lambda i, j: (i, 0, j))],
        out_shape=[jax.ShapeDtypeStruct((b, s, d), BF16), jax.ShapeDtypeStruct((b, e, s), F32)],
        compiler_params=_params('parallel', 'parallel'),
        name='router',
    )(x, g.reshape(1, d), scale, shift, w_router.T, e_bias.reshape(e, 1))


def _moe_kernel(h_ref, x_ref, gate_ref, g2_ref, wg_ref, wu_ref, wd_ref, sg_ref, su_ref, sd_ref, *rest, final):
    if final:
        fg_ref, o_ref, acc_ref = rest
    else:
        o_ref, acc_ref = rest
    e = pl.program_id(2)
    h = h_ref[0]

    @pl.when(e == 0)
    def _():
        a = jnp.dot(h, sg_ref[...], preferred_element_type=F32)
        u = jnp.dot(h, su_ref[...], preferred_element_type=F32)
        acc_ref[...] = jnp.dot((_silu(a) * u).astype(BF16), sd_ref[...], preferred_element_type=F32)

    lane = lax.broadcasted_iota(jnp.int32, gate_ref.shape[1:], 1)
    gcol = jnp.sum(jnp.where(lane == e, gate_ref[0], 0.0), axis=-1, keepdims=True)
    a = jnp.dot(h, wg_ref[0], preferred_element_type=F32)
    u = jnp.dot(h, wu_ref[0], preferred_element_type=F32)
    acc_ref[...] += jnp.dot((_silu(a) * u * gcol).astype(BF16), wd_ref[0], preferred_element_type=F32)

    @pl.when(e == pl.num_programs(2) - 1)
    def _():
        y = x_ref[0] + g2_ref[0] * acc_ref[...]
        if final:
            y = _rms(y) * fg_ref[...]
        o_ref[0] = y


def _moe(h2, x, gate, g2, w_gate, w_up, w_down, s_gate, s_up, s_down, final_g=None, tm=1024):
    b, s, d = x.shape
    tm = min(tm, s)
    e, _, ff = w_gate.shape
    row = pl.BlockSpec((1, tm, d), lambda i, j, k: (i, j, 0))
    ins = [h2, x, gate, g2, w_gate, w_up, w_down, s_gate, s_up, s_down]
    in_specs = [row, row, pl.BlockSpec((1, tm, e), lambda i, j, k: (i, j, 0)),
                pl.BlockSpec((1, 1, d), lambda i, j, k: (i, 0, 0)),
                pl.BlockSpec((1, d, ff), lambda i, j, k: (k, 0, 0)), pl.BlockSpec((1, d, ff), lambda i, j, k: (k, 0, 0)),
                pl.BlockSpec((1, ff, d), lambda i, j, k: (k, 0, 0)),
                _const_spec(s_gate.shape), _const_spec(s_up.shape), _const_spec(s_down.shape)]
    if final_g is not None:
        ins.append(final_g.reshape(1, d))
        in_specs.append(_const_spec((1, d)))
    return pl.pallas_call(
        functools.partial(_moe_kernel, final=final_g is not None),
        grid=(b, s // tm, e),
        in_specs=in_specs,
        out_specs=row,
        out_shape=jax.ShapeDtypeStruct(x.shape, F32),
        scratch_shapes=[pltpu.VMEM((tm, d), F32)],
        compiler_params=_params('parallel', 'parallel', 'arbitrary'),
        name='moe',
    )(*ins)


def _heads(a, n):
    b, s, _ = a.shape
    return a.reshape(b, s, n, -1).transpose(0, 2, 1, 3)


def _merge_heads(a):
    b, h, s, d = a.shape
    return a.transpose(0, 2, 1, 3).reshape(b, s, h * d)


def _rope_1d(x, pos):
    n = x.shape[-1] // 2
    inv = ROPE_BASE ** (-jnp.arange(n, dtype=F32) / n)
    ang = pos.astype(F32)[:, None] * inv
    cos, sin = jnp.cos(ang), jnp.sin(ang)
    x1, x2 = x[..., :n], x[..., n:]
    return jnp.concatenate([x1 * cos - x2 * sin, x2 * cos + x1 * sin], axis=-1)


def _rope_2d(x, row, col):
    h = x.shape[-1] // 2
    return jnp.concatenate([_rope_1d(x[..., :h], row), _rope_1d(x[..., h:], col)], axis=-1)


def _da_split(a, rope, row, col):
    b, s, _ = a.shape
    a = a.reshape(b, s, DA_HEADS, 2, DA_HEAD_DIM).transpose(0, 2, 3, 1, 4)
    return _rope_2d(a, row, col) if rope else a


def _mixers(p, pc, ctx_out, prm, l, lam_init, row, col, lb_terms):
    b, s, _ = p['da_q'].shape
    sc = pc['da_q'].shape[1]

    hy_args = (prm['hy_w1'][l], prm['hy_b1'][l], prm['hy_w2'][l], prm['hy_b2'][l], prm['hy_w3'][l], prm['hy_b3'][l],
               prm['hy_sin_freq'][l], prm['hy_decay'][l])
    y_hy = _hyena([p['hy_v'], p['hy_x1'], p['hy_x2']], prm['hy_conv_w'][l], prm['hy_conv_b'][l],
                  _hy_filters(s, *hy_args), prm['hy_bias'][l], inner=128)
    yc_hy = None
    if ctx_out:
        yc_hy = _hyena([pc['hy_v'], pc['hy_x1'], pc['hy_x2']], prm['hy_conv_w'][l], prm['hy_conv_b'][l],
                       _hy_filters(sc, *hy_args), prm['hy_bias'][l], inner=32)

    g_da = b * DA_HEADS
    q = _da_split(p['da_q'], True, row, col).reshape(g_da, 2, s, DA_HEAD_DIM)
    k = _da_split(p['da_k'], True, row, col).reshape(g_da, 2, s, DA_HEAD_DIM)
    kc = _da_split(pc['da_k'], False, row, col).reshape(g_da, 2, sc, DA_HEAD_DIM)
    vh = _heads(p['da_v'], DA_HEADS).reshape(g_da, s, 2 * DA_HEAD_DIM)
    vch = _heads(pc['da_v'], DA_HEADS).reshape(g_da, sc, 2 * DA_HEAD_DIM)
    keys = jnp.concatenate([kc, k], axis=2).astype(BF16)
    vals = jnp.concatenate([vch, vh], axis=1).astype(BF16)
    lp = prm['da_lambda'][l].astype(F32)
    lam = jnp.exp(jnp.sum(lp[0] * lp[1])) - jnp.exp(jnp.sum(lp[2] * lp[3])) + lam_init
    da_kw = dict(scale=DA_HEAD_DIM ** -0.5, lam=lam, subln_g=prm['da_subln_g'][l], post_scale=1.0 - lam_init)
    y_da = _merge_heads(_attention(q, keys, vals, **da_kw).reshape(b, DA_HEADS, s, -1))
    yc_da = None
    if ctx_out:
        qc = _da_split(pc['da_q'], False, row, col).reshape(g_da, 2, sc, DA_HEAD_DIM)
        yc_da = _merge_heads(_attention(qc, kc.astype(BF16), vch.astype(BF16), **da_kw)
                             .reshape(b, DA_HEADS, sc, -1))

    wq = prm['mla_w_q_up'][l].astype(BF16)
    wkv = prm['mla_w_kv_up'][l].astype(BF16)

    def queries(qd, rope):
        (qu,) = _norm_proj(qd, prm['mla_q_norm_g'][l], [wq])
        qh = _heads(qu, MLA_HEADS)
        qn, qr = qh[..., :MLA_NOPE_DIM], qh[..., MLA_NOPE_DIM:]
        if rope:
            qr = _rope_2d(qr, row, col)
        return jnp.concatenate([qn, qr], axis=-1)

    def keys_values(kvd, kr, rope):
        (kvu,) = _norm_proj(kvd, prm['mla_kv_norm_g'][l], [wkv])
        kvh = _heads(kvu, MLA_HEADS)
        kn, vv = kvh[..., :MLA_NOPE_DIM], kvh[..., MLA_NOPE_DIM:]
        if rope:
            kr = _rope_2d(kr, row, col)
        kr = jnp.broadcast_to(kr[:, None], kn.shape[:-1] + (MLA_ROPE_DIM,))
        return jnp.concatenate([kn, kr], axis=-1), vv

    g_m = b * MLA_HEADS
    dqk = MLA_NOPE_DIM + MLA_ROPE_DIM
    k_lat, v_lat = keys_values(p['mla_kv'], p['mla_kr'], True)
    k_ctx, v_ctx = keys_values(pc['mla_kv'], pc['mla_kr'], False)
    mk = jnp.concatenate([k_ctx, k_lat], axis=2).astype(BF16).reshape(g_m, 1, sc + s, dqk)
    mv = jnp.concatenate([v_ctx, v_lat], axis=2).astype(BF16).reshape(g_m, sc + s, MLA_V_DIM)
    mq = queries(p['mla_q'], True).reshape(g_m, 1, s, dqk)
    y_mla = _merge_heads(_attention(mq, mk, mv, scale=dqk ** -0.5).reshape(b, MLA_HEADS, s, -1))
    yc_mla = None
    if ctx_out:
        mqc = queries(pc['mla_q'], False).reshape(g_m, 1, sc, dqk)
        yc_mla = _merge_heads(_attention(mqc, k_ctx.astype(BF16).reshape(g_m, 1, sc, dqk),
                                         v_ctx.astype(BF16).reshape(g_m, sc, MLA_V_DIM), scale=dqk ** -0.5)
                              .reshape(b, MLA_HEADS, sc, -1))

    o, oc = _hgrn(p['hg_q'], p['hg_ff'], p['hg_fb'], p['hg_i'], pc['hg_q'], pc['hg_ff'], pc['hg_fb'], pc['hg_i'],
                  lb_terms, prm['hg_norm_g'][l])
    return (y_hy, y_da, y_mla, o), (yc_hy, yc_da, yc_mla, oc)


def kernel(x, c, ctx, c_ctx, w_ada, b_ada, norm1_g, norm2_g, w_in, w_out, hy_conv_w, hy_conv_b, hy_w1, hy_b1, hy_w2, hy_b2, hy_w3, hy_b3, hy_sin_freq, hy_decay, hy_bias, da_lambda, da_subln_g, mla_q_norm_g, mla_w_q_up, mla_kv_norm_g, mla_w_kv_up, hg_lower_bounds, hg_norm_g, moe_w_router, moe_bias, moe_w_gate, moe_w_up, moe_w_down, moe_sh_gate, moe_sh_up, moe_sh_down, final_norm_g):
    prm = dict(hy_conv_w=hy_conv_w, hy_conv_b=hy_conv_b, hy_w1=hy_w1, hy_b1=hy_b1, hy_w2=hy_w2, hy_b2=hy_b2,
               hy_w3=hy_w3, hy_b3=hy_b3, hy_sin_freq=hy_sin_freq, hy_decay=hy_decay, hy_bias=hy_bias,
               da_lambda=da_lambda, da_subln_g=da_subln_g, mla_q_norm_g=mla_q_norm_g, mla_w_q_up=mla_w_q_up,
               mla_kv_norm_g=mla_kv_norm_g, mla_w_kv_up=mla_w_kv_up, hg_norm_g=hg_norm_g)
    b, n_lat, d = x.shape
    depth = w_in.shape[0]
    rows = n_lat // GRID_W
    row_pos = jnp.repeat(jnp.arange(rows, dtype=jnp.int32), GRID_W)
    col_pos = jnp.tile(jnp.arange(GRID_W, dtype=jnp.int32), rows)
    lbs = jnp.cumsum(jax.nn.softmax(hg_lower_bounds.astype(F32), axis=1), axis=1)
    lbs = lbs - lbs[:, :1]
    cond = jnp.concatenate([c, c_ctx[None], jnp.zeros((8 - b - 1, d), F32)], axis=0)

    for l in range(depth):
        ctx_out = l < depth - 1
        mods = _ada(cond, w_ada[l], b_ada[l])
        sh1, sc1, g1, sh2, sc2, g2 = [m[:, None, :] for m in jnp.split(mods[:b], 6, axis=-1)]
        mc = [jnp.broadcast_to(m[:, None, :], (b, 1, d)) for m in jnp.split(mods[b:b + 1], 6, axis=-1)]

        off = 0
        ws = []
        for _, wdt in _SEGMENTS:
            ws.append(w_in[l][:, off:off + wdt].astype(BF16))
            off += wdt
        names = [nm for nm, _ in _SEGMENTS]
        p = dict(zip(names, _norm_proj(x, norm1_g[l], ws, sc1, sh1)))
        pc = dict(zip(names, _norm_proj(ctx, norm1_g[l], ws, mc[1], mc[0])))

        lb = lbs[:, l]
        lb_terms = jnp.stack([jnp.log(lb), jnp.log1p(-lb), 1.0 - lb], axis=1)
        lam_init = 0.8 - 0.6 * math.exp(-0.3 * l)
        lat_parts, ctx_parts = _mixers(p, pc, ctx_out, prm, l, lam_init, row_pos, col_pos, lb_terms)

        w_out_b = w_out[l].astype(BF16)
        moe_w = (moe_w_gate[l].astype(BF16), moe_w_up[l].astype(BF16), moe_w_down[l].astype(BF16),
                 moe_sh_gate[l].astype(BF16), moe_sh_up[l].astype(BF16), moe_sh_down[l].astype(BF16))

        if ctx_out:
            ctx = _out_proj(*ctx_parts, pc['hg_g'], ctx, mc[2], w_out_b)
            h2c, gate_c = _router(ctx, norm2_g[l], mc[4], mc[3], moe_w_router[l], moe_bias[l])
            ctx = _moe(h2c, ctx, gate_c.transpose(0, 2, 1), mc[5], *moe_w)

        x = _out_proj(*lat_parts, p['hg_g'], x, g1, w_out_b)
        h2, gate = _router(x, norm2_g[l], sc2, sh2, moe_w_router[l], moe_bias[l])
        x = _moe(h2, x, gate.transpose(0, 2, 1), g2, *moe_w, final_g=None if ctx_out else final_norm_g)

    return x
```

```python
import functools
import math

import numpy as np
import jax
import jax.numpy as jnp
from jax import lax
from jax.experimental import pallas as pl
from jax.experimental.pallas import tpu as pltpu

F32 = jnp.float32
BF16 = jnp.bfloat16
HIGHEST = lax.Precision.HIGHEST

D_MODEL = 1024
GRID_W = 64
HY_WIDTH = 256
HY_ORDER = 2
HY_BANDS = 16
DA_HEADS = 4
DA_HEAD_DIM = 32
MLA_HEADS = 4
MLA_Q_RANK = 192
MLA_KV_RANK = 128
MLA_NOPE_DIM = 64
MLA_ROPE_DIM = 32
MLA_V_DIM = 64
HG_HEADS = 4
HG_KEY_DIM = 64
HG_VAL_DIM = 64
HG_CHUNK = 64
HG_SUB = 8
N_EXPERTS = 64
N_EXPERT_GROUPS = 8
TOPK_GROUPS = 4
TOP_K = 8
EXPERT_FF = 256
ROUTED_SCALE = 2.5
ROPE_BASE = 10000.0
NORM_EPS = 1e-6

V7X_VMEM_LIMIT_BYTES = 56 * 1024 * 1024
LANES = 128

_SEGMENTS = (
    ('hy_v', HY_WIDTH, F32, False, 1), ('hy_x1', HY_WIDTH, F32, False, 1), ('hy_x2', HY_WIDTH, F32, False, 1),
    ('da_q', 2 * DA_HEADS * DA_HEAD_DIM, F32, True, 1), ('da_k', 2 * DA_HEADS * DA_HEAD_DIM, BF16, True, 1),
    ('da_v', 2 * DA_HEADS * DA_HEAD_DIM, BF16, False, 1),
    ('mla_q', MLA_Q_RANK, F32, False, 1), ('mla_kv', MLA_KV_RANK, F32, False, 1),
    ('mla_kr', MLA_ROPE_DIM, BF16, True, MLA_HEADS),
    ('hg_q', HG_HEADS * HG_KEY_DIM, F32, False, 1), ('hg_ff', HG_HEADS * HG_KEY_DIM, F32, False, 1),
    ('hg_fb', HG_HEADS * HG_KEY_DIM, F32, False, 1), ('hg_i', HG_HEADS * HG_VAL_DIM, F32, False, 1),
    ('hg_g', HG_HEADS * HG_VAL_DIM, F32, False, 1),
)


def _params(*semantics):
    return pltpu.CompilerParams(dimension_semantics=semantics, vmem_limit_bytes=V7X_VMEM_LIMIT_BYTES)


def _const_spec(shape):
    nd = len(shape)
    return pl.BlockSpec(shape, lambda *_: (0,) * nd)


def _rms(x, eps=NORM_EPS):
    return x * lax.rsqrt(jnp.mean(x * x, axis=-1, keepdims=True) + eps)


def _silu(x):
    return x * jax.nn.sigmoid(x)


def _dot_nt(a, b, **kw):
    return lax.dot_general(a, b, (((1,), (1,)), ((), ())), preferred_element_type=F32, **kw)


def _ada_kernel(c_ref, w_ref, b_ref, o_ref):
    s = _silu(c_ref[...])
    o_ref[...] = jnp.dot(s, w_ref[...], precision=HIGHEST, preferred_element_type=F32) + b_ref[...]


def _ada(cond, w, b):
    r, d = cond.shape
    n = w.shape[1]
    tn = 1536
    return pl.pallas_call(
        _ada_kernel,
        grid=(n // tn,),
        in_specs=[_const_spec((r, d)), pl.BlockSpec((d, tn), lambda j: (0, j)), pl.BlockSpec((1, tn), lambda j: (0, j))],
        out_specs=pl.BlockSpec((r, tn), lambda j: (0, j)),
        out_shape=jax.ShapeDtypeStruct((r, n), F32),
        compiler_params=_params('arbitrary'),
        name='ada',
    )(cond, w, b.reshape(1, n))


ROPE_UNIT = 32


def _rope_tables(row, col, width):
    n = ROPE_UNIT // 4
    inv = ROPE_BASE ** (-jnp.arange(n, dtype=F32) / n)
    units = width // ROPE_UNIT
    parts_c, parts_a, parts_b = [], [], []
    zero = jnp.zeros((row.shape[0], n), F32)
    for pos in (row, col):
        ang = pos.astype(F32)[:, None] * inv
        cos, sin = jnp.cos(ang), jnp.sin(ang)
        parts_c += [cos, cos]
        parts_a += [zero, sin]
        parts_b += [-sin, zero]
    tile = lambda ps: jnp.tile(jnp.concatenate(ps, axis=1), (1, units))
    return tile(parts_c), tile(parts_a), tile(parts_b)


def _norm_proj_kernel(*refs, n_w, modulate, ropes):
    x_ref, g_ref = refs[0], refs[1]
    pos = 2
    if modulate:
        sc_ref, sh_ref = refs[2], refs[3]
        pos = 4
    if any(ropes):
        rc_ref, ra_ref, rb_ref = refs[pos:pos + 3]
        pos += 3
    w_refs = refs[pos:pos + n_w]
    o_refs = refs[pos + n_w:]
    y = _rms(x_ref[0]) * g_ref[...]
    if modulate:
        y = y * (1.0 + sc_ref[0]) + sh_ref[0]
    yb = y.astype(BF16)
    for w_ref, o_ref, rope in zip(w_refs, o_refs, ropes):
        o = jnp.dot(yb, w_ref[...], preferred_element_type=F32)
        if rope:
            wd = o.shape[1]
            shift = ROPE_UNIT // 4
            o = (o * rc_ref[:, :wd] + pltpu.roll(o, shift, axis=1) * ra_ref[:, :wd]
                 + pltpu.roll(o, wd - shift, axis=1) * rb_ref[:, :wd])
        o_ref[0] = o.astype(o_ref.dtype)


def _norm_proj(x, g, outs, scale=None, shift=None, rope_tabs=None, tm=512):
    b, s, k = x.shape
    tm = min(tm, s)
    modulate = scale is not None
    ropes = tuple(bool(r) and rope_tabs is not None for _, _, r in outs)
    ins = [x, g.reshape(1, k)]
    in_specs = [pl.BlockSpec((1, tm, k), lambda i, j: (i, j, 0)), _const_spec((1, k))]
    if modulate:
        ins += [scale, shift]
        in_specs += [pl.BlockSpec((1, 1, k), lambda i, j: (i, 0, 0))] * 2
    if any(ropes):
        ins += list(rope_tabs)
        in_specs += [pl.BlockSpec((tm, rope_tabs[0].shape[1]), lambda i, j: (j, 0))] * 3
    for w, _, _ in outs:
        ins.append(w)
        in_specs.append(_const_spec(w.shape))
    return pl.pallas_call(
        functools.partial(_norm_proj_kernel, n_w=len(outs), modulate=modulate, ropes=ropes),
        grid=(b, s // tm),
        in_specs=in_specs,
        out_specs=[pl.BlockSpec((1, tm, w.shape[1]), lambda i, j: (i, j, 0)) for w, _, _ in outs],
        out_shape=[jax.ShapeDtypeStruct((b, s, w.shape[1]), dt) for w, dt, _ in outs],
        compiler_params=_params('parallel', 'parallel'),
        name='norm_proj',
    )(*ins)


def _hy_filter_kernel(w1t_ref, w1s_ref, w1c_ref, b1_ref, w2_ref, b2_ref, w3_ref, b3_ref, fr_ref, dec_ref, o_ref, *, n):
    t = lax.broadcasted_iota(jnp.int32, (n, 1), 0).astype(F32) / n
    bands = lax.broadcasted_iota(jnp.int32, (1, HY_BANDS), 1).astype(F32) + 1.0
    ang = (2.0 * jnp.pi) * t * bands
    pre = (t * w1t_ref[...]
           + jnp.dot(jnp.sin(ang), w1s_ref[...], precision=HIGHEST, preferred_element_type=F32)
           + jnp.dot(jnp.cos(ang), w1c_ref[...], precision=HIGHEST, preferred_element_type=F32)
           + b1_ref[...])
    hid = jnp.sin(fr_ref[0:1, :] * pre)
    hid = jnp.sin(fr_ref[1:2, :] * (jnp.dot(hid, w2_ref[...], precision=HIGHEST, preferred_element_type=F32) + b2_ref[...]))
    filt = jnp.dot(hid, w3_ref[...], precision=HIGHEST, preferred_element_type=F32) + b3_ref[...]
    filt = filt * jnp.exp(-t * jnp.abs(dec_ref[...]))
    col = jnp.sum(jnp.abs(filt), axis=0, keepdims=True) - jnp.abs(filt[0:1, :])
    w = HY_WIDTH
    for o in range(HY_ORDER):
        lo = o * 2 * w
        f0 = filt[0:1, lo:lo + w] + filt[0:1, lo + w:lo + 2 * w]
        inv = 1.0 / (col[:, lo:lo + w] + col[:, lo + w:lo + 2 * w] + jnp.abs(f0))
        o_ref[:, lo:lo + w] = filt[:, lo:lo + w] * inv
        o_ref[:, lo + w:lo + 2 * w] = filt[:, lo + w:lo + 2 * w] * inv


def _hy_filters(n, w1, b1, w2, b2, w3, b3, freq, decay):
    cols = w3.shape[1]
    ins = [w1[0:1], w1[1:1 + HY_BANDS], w1[1 + HY_BANDS:], b1.reshape(1, -1), w2, b2.reshape(1, -1), w3,
           b3.reshape(1, -1), freq, decay.reshape(1, -1)]
    out = pl.pallas_call(
        functools.partial(_hy_filter_kernel, n=n),
        grid=(1,),
        in_specs=[_const_spec(a.shape) for a in ins],
        out_specs=_const_spec((n, cols)),
        out_shape=jax.ShapeDtypeStruct((n, cols), F32),
        compiler_params=_params('arbitrary'),
        name='hy_filter',
    )(*ins)
    return out.reshape(n, HY_ORDER, 2, HY_WIDTH)


def _two_sided(filt_n):
    n = filt_n.shape[0]
    hf, hb = filt_n[:, :, 0], filt_n[:, :, 1]
    h = jnp.concatenate([hf[:1] + hb[:1], hf[1:], jnp.zeros((1,) + hf.shape[1:], F32), hb[:0:-1]], axis=0)
    return h.reshape(2 * n, HY_ORDER * HY_WIDTH)


def _short_conv_kernel(*refs, s):
    x_refs, w_refs, b_refs, o_refs = refs[0:3], refs[3:6], refs[6:9], refs[9:12]
    row = lax.broadcasted_iota(jnp.int32, (s, 1), 0)
    for x_ref, w_ref, b_ref, o_ref in zip(x_refs, w_refs, b_refs, o_refs):
        x = x_ref[0]
        prev = jnp.where(row == 0, 0.0, pltpu.roll(x, 1, axis=0))
        nxt = jnp.where(row == s - 1, 0.0, pltpu.roll(x, s - 1, axis=0))
        o_ref[0] = prev * w_ref[0:1, :] + x * w_ref[1:2, :] + nxt * w_ref[2:3, :] + b_ref[...]


def _short_conv(parts, conv_w, conv_b):
    b, s, c = parts[0].shape
    tc = LANES
    ws = [conv_w[:, i * c:(i + 1) * c] for i in range(3)]
    bs = [conv_b[i * c:(i + 1) * c].reshape(1, c) for i in range(3)]
    xspec = pl.BlockSpec((1, s, tc), lambda i, j: (i, 0, j))
    return pl.pallas_call(
        functools.partial(_short_conv_kernel, s=s),
        grid=(b, c // tc),
        in_specs=[xspec] * 3 + [pl.BlockSpec((3, tc), lambda i, j: (0, j))] * 3 + [pl.BlockSpec((1, tc), lambda i, j: (0, j))] * 3,
        out_specs=[xspec] * 3,
        out_shape=[jax.ShapeDtypeStruct((b, s, c), F32)] * 3,
        compiler_params=_params('parallel', 'parallel'),
        name='short_conv',
    )(*parts, *ws, *bs)


def _dft_cos_sin(rows, cols, period):
    ang = 2.0 * np.pi * ((np.arange(rows)[:, None] * np.arange(cols)[None, :]) % period) / period
    return np.cos(ang), np.sin(ang)


def _fft_tables(n, inner):
    big = 2 * n
    n1 = big // inner
    c1, s1 = _dft_cos_sin(n1, n1, n1)
    h = n1 // 2
    outer_data = np.block([[c1[:, :h], s1[:, :h]], [-s1[:, :h], c1[:, :h]]])
    outer_real = np.concatenate([c1, -s1], axis=0)
    outer_inv = np.block([[c1[:h, :], -s1[:h, :]], [s1[:h, :], c1[:h, :]]]) / big
    c2, s2 = _dft_cos_sin(inner, inner, inner)
    inner_fwd = np.block([[c2, s2], [-s2, c2]])
    inner_inv = np.block([[c2, -s2], [s2, c2]])
    ct, st = _dft_cos_sin(n1, inner, big)
    f = lambda a: jnp.asarray(a, F32)
    return dict(n1=n1, inner=inner, outer_data=f(outer_data), outer_real=f(outer_real), outer_inv=f(outer_inv),
                inner_fwd=_hi_lo_cols(inner_fwd), inner_inv=_hi_lo_cols(inner_inv),
                tw_cos=f(ct).reshape(n1, inner, 1), tw_sin=f(st).reshape(n1, inner, 1))


def _left_mm_kernel(m_ref, x_ref, o_ref):
    o_ref[0] = jnp.dot(m_ref[...], x_ref[0], precision=HIGHEST, preferred_element_type=F32)


def _left_mm(m, x, tl=4096):
    p, k, l = x.shape
    mm = m.shape[0]
    tl = min(tl, l)
    return pl.pallas_call(
        _left_mm_kernel,
        grid=(p, l // tl),
        in_specs=[_const_spec(m.shape), pl.BlockSpec((1, k, tl), lambda i, j: (i, 0, j))],
        out_specs=pl.BlockSpec((1, mm, tl), lambda i, j: (i, 0, j)),
        out_shape=jax.ShapeDtypeStruct((p, mm, l), F32),
        compiler_params=_params('parallel', 'parallel'),
        name='fft_outer',
    )(m, x)


def _hi_lo_cols(m):
    m = np.asarray(m, np.float32)
    hi = m.astype(BF16)
    lo = (m - hi.astype(np.float32)).astype(BF16)
    return jnp.asarray(np.concatenate([hi, hi, lo], axis=1))


def _hi_lo_rows(x):
    hi = x.astype(BF16)
    lo = (x - hi.astype(F32)).astype(BF16)
    return jnp.concatenate([hi, lo, hi], axis=0)


def _inner_kernel(a_ref, twc_ref, tws_ref, gf_ref, *rest, convolve, inner, kb):
    for s in range(kb):
        ar, ai = a_ref[0, 0, s], a_ref[0, 1, s]
        tc, ts = twc_ref[s], tws_ref[s]
        br = ar * tc + ai * ts
        bi = ai * tc - ar * ts
        x = jnp.dot(gf_ref[...], _hi_lo_rows(jnp.concatenate([br, bi], axis=0)), preferred_element_type=F32)
        if not convolve:
            o_ref = rest[0]
            o_ref[0, 0, s] = x[:inner]
            o_ref[0, 1, s] = x[inner:]
            continue
        h_ref, gi_ref, o_ref = rest
        xr, xi = x[:inner], x[inner:]
        hr, hi = h_ref[0, 0, s], h_ref[0, 1, s]
        yr = xr * hr - xi * hi
        yi = xr * hi + xi * hr
        z = jnp.dot(gi_ref[...], _hi_lo_rows(jnp.concatenate([yr, yi], axis=0)), preferred_element_type=F32)
        zr, zi = z[:inner], z[inner:]
        o_ref[0, 0, s] = zr * tc - zi * ts
        o_ref[0, 1, s] = zi * tc + zr * ts


def _fft_inner(a, tab, c, h=None, h_block=0):
    p = a.shape[0]
    n1, inner = tab['n1'], tab['inner']
    a5 = a.reshape(p, 2, n1, inner, c)
    tc = 2 * LANES
    kb = 4
    blk = pl.BlockSpec((1, 2, kb, inner, tc), lambda i, k, j: (i, 0, k, 0, j))
    tw_spec = pl.BlockSpec((kb, inner, 1), lambda i, k, j: (k, 0, 0))
    ins = [a5, tab['tw_cos'], tab['tw_sin'], tab['inner_fwd']]
    in_specs = [blk, tw_spec, tw_spec, _const_spec(tab['inner_fwd'].shape)]
    if h is not None:
        ch = h.shape[-1] // inner
        nb = c // tc
        ins += [h.reshape(1, 2, n1, inner, ch), tab['inner_inv']]
        in_specs += [pl.BlockSpec((1, 2, kb, inner, tc), lambda i, k, j: (0, 0, k, 0, h_block * nb + j)),
                     _const_spec(tab['inner_inv'].shape)]
    out = pl.pallas_call(
        functools.partial(_inner_kernel, convolve=h is not None, inner=inner, kb=kb),
        grid=(p, n1 // kb, c // tc),
        in_specs=in_specs,
        out_specs=blk,
        out_shape=jax.ShapeDtypeStruct(a5.shape, F32),
        compiler_params=_params('parallel', 'parallel', 'parallel'),
        name='fft_inner',
    )(*ins)
    return out.reshape(p, 2 * n1, inner * c)


def _gate_kernel(m_ref, z_ref, u_ref, x_ref, bias_ref, *rest, chain):
    y = jnp.dot(m_ref[...], z_ref[0], precision=HIGHEST, preferred_element_type=F32)
    nxt = x_ref[0] * (y + u_ref[0] * bias_ref[...])
    if chain:
        mf_ref, o_ref, a_ref = rest
        o_ref[0] = nxt
        a_ref[0] = jnp.dot(mf_ref[...], nxt, precision=HIGHEST, preferred_element_type=F32)
    else:
        rest[0][0] = nxt


def _fft_gate(tab, z, u, x, bias_l, chain, tl=4096):
    p, k2, l = z.shape
    n1 = tab['n1']
    tl = min(tl, l)
    row = pl.BlockSpec((1, n1, tl), lambda i, j: (i, 0, j))
    ins = [tab['outer_inv'], z, u, x, bias_l]
    in_specs = [_const_spec((n1, k2)), pl.BlockSpec((1, k2, tl), lambda i, j: (i, 0, j)), row, row,
                pl.BlockSpec((1, tl), lambda i, j: (0, j))]
    out_specs = [row]
    out_shape = [jax.ShapeDtypeStruct((p, n1, l), F32)]
    if chain:
        ins.append(tab['outer_data'])
        in_specs.append(_const_spec((k2, n1)))
        out_specs.append(pl.BlockSpec((1, k2, tl), lambda i, j: (i, 0, j)))
        out_shape.append(jax.ShapeDtypeStruct((p, k2, l), F32))
    return pl.pallas_call(
        functools.partial(_gate_kernel, chain=chain),
        grid=(p, l // tl),
        in_specs=in_specs,
        out_specs=out_specs,
        out_shape=out_shape,
        compiler_params=_params('parallel', 'parallel'),
        name='fft_gate',
    )(*ins)


def _hyena(parts, conv_w, conv_b, filt_n, bias, inner):
    b, s, c = parts[0].shape
    tab = _fft_tables(s, inner)
    n1 = tab['n1']
    lanes = inner * c
    h_taps = _two_sided(filt_n).reshape(1, n1, inner * HY_ORDER * c)
    h_spec = _fft_inner(_left_mm(tab['outer_real'], h_taps), tab, HY_ORDER * c)
    v, x1, x2 = [a.reshape(b // 2, n1, lanes) for a in _short_conv(parts, conv_w, conv_b)]
    bias_l = [jnp.tile(bias[o], inner).reshape(1, lanes) for o in range(HY_ORDER)]
    a = _left_mm(tab['outer_data'], v)
    z = _fft_inner(a, tab, c, h_spec, 0)
    z2, a = _fft_gate(tab, z, v, x1, bias_l[0], chain=True)
    z = _fft_inner(a, tab, c, h_spec, 1)
    (z3,) = _fft_gate(tab, z, z2, x2, bias_l[1], chain=False)
    return z3.reshape(b, s, c)


def _attn_kernel(*refs, widths, n_pieces, heads, ncomp, scale, post_scale):
    n_q = len(widths)
    q_refs = refs[:n_q]
    pos = n_q
    pieces = []
    for _ in range(n_pieces):
        pieces.append((refs[pos:pos + n_q], refs[pos + n_q]))
        pos += n_q + 1
    if ncomp == 2:
        lam_ref, g_ref, bd_ref = refs[pos:pos + 3]
        pos += 3
    o_ref = refs[pos]
    groups = heads * ncomp
    tq = o_ref.shape[1]
    stacked = []
    for q_ref, wd in zip(q_refs, widths):
        q = q_ref[0] * (scale * math.log2(math.e))
        unit_shift = (wd // groups).bit_length() - 1
        owner = jnp.right_shift(lax.broadcasted_iota(jnp.int32, (1, wd), 1), unit_shift)
        stacked.append(jnp.concatenate([jnp.where(owner == g, q, 0.0) for g in range(groups)], axis=0).astype(BF16))
    scores = []
    for k_refs, _ in pieces:
        s = None
        for qs, k_ref in zip(stacked, k_refs):
            t = _dot_nt(qs, k_ref[0])
            s = t if s is None else s + t
        scores.append(s)
    m = None
    for s in scores:
        mp = jnp.max(s, axis=-1, keepdims=True)
        m = mp if m is None else jnp.maximum(m, mp)
    l, o = None, None
    for s, (_, v_ref) in zip(scores, pieces):
        p = jnp.exp2(s - m)
        lp = jnp.sum(p, axis=-1, keepdims=True)
        op = jnp.dot(p.astype(BF16), v_ref[0], preferred_element_type=F32)
        l = lp if l is None else l + lp
        o = op if o is None else o + op
    o = o / l
    wv = o.shape[1]
    head_shift = (wv // heads).bit_length() - 1
    head_of = jnp.right_shift(lax.broadcasted_iota(jnp.int32, (1, wv), 1), head_shift)
    out = jnp.zeros((tq, wv), F32)
    for h in range(heads):
        r0 = h * ncomp * tq
        oh = o[r0:r0 + tq]
        if ncomp == 2:
            oh = oh - lam_ref[0] * o[r0 + tq:r0 + 2 * tq]
        out = jnp.where(head_of == h, oh, out)
    if ncomp == 2:
        ms = jnp.dot(out * out, bd_ref[...], precision=HIGHEST, preferred_element_type=F32)
        out = out * lax.rsqrt(ms + NORM_EPS) * g_ref[...] * post_scale
    o_ref[0] = out


def _attention(q_parts, pieces, heads, ncomp, scale, tq, lam=None, subln_g=None, post_scale=1.0):
    b, sq, _ = q_parts[0].shape
    wv = pieces[0][1].shape[2]
    tq = min(tq, sq)
    widths = tuple(q.shape[2] for q in q_parts)
    ins = list(q_parts)
    in_specs = [pl.BlockSpec((1, tq, w), lambda i, j: (i, j, 0)) for w in widths]
    for k_parts, v in pieces:
        for a in list(k_parts) + [v]:
            ins.append(a)
            in_specs.append(pl.BlockSpec((1,) + a.shape[1:], lambda i, j: (i, 0, 0)))
    if ncomp == 2:
        dv = wv // heads
        ln = np.arange(wv)
        bd = ((ln[:, None] // dv) == (ln[None, :] // dv)).astype(np.float32) / dv
        ins += [lam.reshape(1), jnp.tile(subln_g, heads).reshape(1, wv), jnp.asarray(bd)]
        in_specs += [pl.BlockSpec(memory_space=pltpu.SMEM), _const_spec((1, wv)), _const_spec((wv, wv))]
    return pl.pallas_call(
        functools.partial(_attn_kernel, widths=widths, n_pieces=len(pieces), heads=heads, ncomp=ncomp,
                          scale=scale, post_scale=post_scale),
        grid=(b, sq // tq),
        in_specs=in_specs,
        out_specs=pl.BlockSpec((1, tq, wv), lambda i, j: (i, j, 0)),
        out_shape=jax.ShapeDtypeStruct((b, sq, wv), F32),
        compiler_params=_params('parallel', 'parallel'),
        name='attention',
    )(*ins)


def _forget_terms(f, log_lb, log_1m_lb, one_m_lb):
    log_sig = jnp.minimum(f, 0.0) - jnp.log1p(jnp.exp(-jnp.abs(f)))
    b = log_1m_lb + log_sig
    log_g = jnp.maximum(log_lb, b) + jnp.log1p(jnp.exp(-jnp.abs(log_lb - b)))
    return log_g, one_m_lb * jax.nn.sigmoid(-f)


def _hg_tables():
    ck, sub = HG_CHUNK, HG_SUB
    t = np.arange(ck)
    cum_mats, half_masks, group_masks, sels = [], [], [], []
    for rev in (False, True):
        mats = [(t[None, :] >= t[:, None]) if rev else (t[None, :] <= t[:, None])]
        halves = []
        hs = ck // 2
        while hs >= sub:
            pos = t % (2 * hs)
            b = t - pos + hs
            mats.append((t[None, :] >= b[:, None]) if rev else (t[None, :] < b[:, None]))
            q_half = (pos < hs) if rev else (pos >= hs)
            halves.append(np.stack([q_half, ~q_half]))
            if not rev:
                grp = (t[:, None] // (2 * hs)) == (t[None, :] // (2 * hs))
                group_masks.append(np.concatenate([grp, grp], axis=0))
            hs //= 2
        cum_mats.append(np.concatenate(mats, axis=0))
        half_masks.append(np.stack(halves))
        r, c = np.arange(ck)[:, None], np.arange(ck * sub)[None, :]
        same = (c // sub) == r
        tt, ss = (c // sub) % sub, c % sub
        sels.append(same & ((ss >= tt) if rev else (ss <= tt)))
    lanes = 2 * HG_KEY_DIM
    ln = np.arange(lanes)
    bd = (ln[:, None] // HG_KEY_DIM) == (ln[None, :] // HG_KEY_DIM)
    hm = np.broadcast_to(np.stack(half_masks)[..., None], (2, len(half_masks[0]), 2, ck, lanes))
    return (jnp.asarray(np.stack(cum_mats), BF16), jnp.asarray(hm, F32), jnp.asarray(np.stack(group_masks), F32),
            jnp.asarray(np.stack(sels), BF16), jnp.asarray(bd, F32))


def _split3(x):
    a = x.astype(BF16)
    r = x - a.astype(F32)
    b = r.astype(BF16)
    return a, b, (r - b.astype(F32)).astype(BF16)


def _hg_chunk(q, k, v, lg, st, rev, cm, hm, gm, sel, bd, m0, m1):
    ck, sub = HG_CHUNK, HG_SUB
    call = sum(jnp.dot(cm, piece, preferred_element_type=F32) for piece in _split3(lg))
    cum = call[0:ck]
    tot = cum[0:1] if rev else cum[ck - 1:ck]
    o = _dot_nt((q * jnp.exp(cum)).astype(BF16), st.astype(BF16))
    kd = (k * jnp.exp(tot - cum)).astype(BF16)
    st_new = st * jnp.exp(tot) + bd * jnp.dot(v.T.astype(BF16), kd, preferred_element_type=F32)
    s2 = None
    for lv in range(gm.shape[0]):
        cb = call[(lv + 1) * ck:(lv + 2) * ck]
        qd = q * jnp.exp(jnp.minimum(cum - cb, 0.0)) * hm[lv, 0]
        kf = (k * jnp.exp(jnp.minimum(cb - cum, 0.0)) * hm[lv, 1]).astype(BF16)
        q2 = jnp.concatenate([qd * m0, qd * m1], axis=0).astype(BF16)
        term = _dot_nt(q2, kf) * gm[lv]
        s2 = term if s2 is None else s2 + term
    r = jnp.dot(s2.astype(BF16), v.astype(BF16), preferred_element_type=F32)
    o = o + m0 * r[:ck] + m1 * r[ck:]
    rows, vts = [], []
    for i in range(ck // sub):
        lo, hi = i * sub, (i + 1) * sub
        ki, ci = k[lo:hi], cum[lo:hi]
        for t in range(lo, hi):
            rows.append((q[t:t + 1] * ki * jnp.exp(jnp.minimum(cum[t:t + 1] - ci, 0.0))).astype(BF16))
            vts.append(v[lo:hi])
    sc = jnp.dot(jnp.concatenate(rows, axis=0), bd.astype(BF16), preferred_element_type=F32)
    o = o + jnp.dot(sel, (sc * jnp.concatenate(vts, axis=0)).astype(BF16), preferred_element_type=F32)
    return o, st_new


def _hgrn_kernel(q_ref, ff_ref, fb_ref, i_ref, qc_ref, ffc_ref, fbc_ref, ic_ref, lb_ref, g_ref,
                 cm_ref, hm_ref, gm_ref, sel_ref, bd_ref, o_ref, oc_ref, or_ref, ocr_ref, st_ref, *, n_lat, n_ctx):
    ck = HG_CHUNK
    lanes = o_ref.shape[-1]
    lane = lax.broadcasted_iota(jnp.int32, (1, lanes), 1)
    m0 = (lane < HG_KEY_DIM).astype(F32)
    m1 = 1.0 - m0
    bd = bd_ref[...]
    gm = gm_ref[...]

    def one(q, f, v, rev):
        d = 1 if rev else 0
        lg, k = _forget_terms(f, lb_ref[d, 0:1, :], lb_ref[d, 1:2, :], lb_ref[d, 2:3, :])
        o, st = _hg_chunk(q, k, v, lg, st_ref[d], rev, cm_ref[d], hm_ref[d], gm, sel_ref[d], bd, m0, m1)
        st_ref[d] = st
        return o

    def sweep(qr, ffr, fbr, ir, out_f, out_r, n):
        nc = n // ck

        def body(step, carry):
            idf = pl.ds(pl.multiple_of(step * ck, ck), ck)
            idr = pl.ds(pl.multiple_of((nc - 1 - step) * ck, ck), ck)
            out_f[0, idf, :] = one(qr[0, idf, :], ffr[0, idf, :], ir[0, idf, :], False)
            out_r[idr, :] = one(qr[0, idr, :], fbr[0, idr, :], ir[0, idr, :], True)
            return carry

        lax.fori_loop(0, nc, body, 0)

    st_ref[...] = jnp.zeros(st_ref.shape, F32)
    sweep(qc_ref, ffc_ref, fbc_ref, ic_ref, oc_ref, ocr_ref, n_ctx)
    sweep(q_ref, ff_ref, fb_ref, i_ref, o_ref, or_ref, n_lat)

    mean_mat = bd * (1.0 / HG_VAL_DIM)

    def readout(out, out_r, n):
        tile = min(n, 512)

        def body(step, carry):
            idx = pl.ds(pl.multiple_of(step * tile, tile), tile)
            x = out[0, idx, :] + out_r[idx, :]
            ms = jnp.dot(x * x, mean_mat, precision=HIGHEST, preferred_element_type=F32)
            out[0, idx, :] = x * lax.rsqrt(ms + NORM_EPS) * g_ref[...]
            return carry

        lax.fori_loop(0, n // tile, body, 0)

    readout(oc_ref, ocr_ref, n_ctx)
    readout(o_ref, or_ref, n_lat)


def _hgrn(q, ff, fb, iv, qc, ffc, fbc, ic, lb_terms, norm_g):
    b, n_lat, width = q.shape
    n_ctx = qc.shape[1]
    lanes = 2 * HG_KEY_DIM
    tables = _hg_tables()
    lat = pl.BlockSpec((1, n_lat, lanes), lambda i, j: (i, 0, j))
    ctx = pl.BlockSpec((1, n_ctx, lanes), lambda i, j: (i, 0, j))
    g2 = jnp.tile(norm_g, 2).reshape(1, lanes)
    return pl.pallas_call(
        functools.partial(_hgrn_kernel, n_lat=n_lat, n_ctx=n_ctx),
        grid=(b, width // lanes),
        in_specs=[lat] * 4 + [ctx] * 4 + [pl.BlockSpec((2, 3, lanes), lambda i, j: (0, 0, j)), _const_spec((1, lanes))]
                 + [_const_spec(t.shape) for t in tables],
        out_specs=[lat, ctx],
        out_shape=[jax.ShapeDtypeStruct(q.shape, F32), jax.ShapeDtypeStruct(qc.shape, F32)],
        scratch_shapes=[pltpu.VMEM((n_lat, lanes), F32), pltpu.VMEM((n_ctx, lanes), F32), pltpu.VMEM((2, lanes, lanes), F32)],
        compiler_params=_params('parallel', 'parallel'),
        name='hgrn2',
    )(q, ff, fb, iv, qc, ffc, fbc, ic, lb_terms, g2, *tables)


def _out_proj_kernel(hy_ref, da_ref, mla_ref, hg_ref, gate_ref, x_ref, g1_ref, w_ref, o_ref):
    hg = hg_ref[0] * _silu(gate_ref[0])
    acc = None
    for i, part in enumerate((hy_ref[0], da_ref[0], mla_ref[0], hg)):
        c = part.shape[1]
        term = jnp.dot(part.astype(BF16), w_ref[i * c:(i + 1) * c, :], preferred_element_type=F32)
        acc = term if acc is None else acc + term
    o_ref[0] = x_ref[0] + g1_ref[0] * acc


def _out_proj(y_hy, y_da, y_mla, y_hg, gate, x, g1, w_out, tm=512):
    b, s, d = x.shape
    tm = min(tm, s)
    c = y_hy.shape[2]
    part = pl.BlockSpec((1, tm, c), lambda i, j: (i, j, 0))
    row = pl.BlockSpec((1, tm, d), lambda i, j: (i, j, 0))
    return pl.pallas_call(
        _out_proj_kernel,
        grid=(b, s // tm),
        in_specs=[part] * 5 + [row, pl.BlockSpec((1, 1, d), lambda i, j: (i, 0, 0)), _const_spec(w_out.shape)],
        out_specs=row,
        out_shape=jax.ShapeDtypeStruct(x.shape, F32),
        compiler_params=_params('parallel', 'parallel'),
        name='out_proj',
    )(y_hy, y_da, y_mla, y_hg, gate, x, g1, w_out)


def _router_kernel(x_ref, g_ref, sc_ref, sh_ref, wrt_ref, bias_ref, h_ref, gate_ref):
    h = _rms(x_ref[0]) * g_ref[...] * (1.0 + sc_ref[0]) + sh_ref[0]
    h_ref[0] = h.astype(BF16)
    tm = h.shape[0]
    scores = jax.nn.sigmoid(_dot_nt(wrt_ref[...], h, precision=HIGHEST))
    choice = scores + bias_ref[...]
    per = N_EXPERTS // N_EXPERT_GROUPS
    neg = -jnp.inf
    iota_g = lax.broadcasted_iota(jnp.int32, (per, tm), 0)
    grp_rows = []
    for gi in range(N_EXPERT_GROUPS):
        blk = choice[gi * per:(gi + 1) * per]
        m1 = jnp.max(blk, axis=0, keepdims=True)
        first = jnp.min(jnp.where(blk == m1, iota_g, per), axis=0, keepdims=True)
        m2 = jnp.max(jnp.where(iota_g == first, neg, blk), axis=0, keepdims=True)
        grp_rows.append(m1 + m2)
    grp = jnp.concatenate(grp_rows, axis=0)
    iota_n = lax.broadcasted_iota(jnp.int32, (N_EXPERT_GROUPS, tm), 0)
    gsel = jnp.zeros((N_EXPERT_GROUPS, tm), F32)
    for _ in range(TOPK_GROUPS):
        m = jnp.max(grp, axis=0, keepdims=True)
        first = jnp.min(jnp.where(grp == m, iota_n, N_EXPERT_GROUPS), axis=0, keepdims=True)
        hit = iota_n == first
        gsel = jnp.where(hit, 1.0, gsel)
        grp = jnp.where(hit, neg, grp)
    emask = jnp.concatenate([jnp.broadcast_to(gsel[gi:gi + 1], (per, tm)) for gi in range(N_EXPERT_GROUPS)], axis=0)
    cand = jnp.where(emask > 0.0, choice, neg)
    iota_e = lax.broadcasted_iota(jnp.int32, (N_EXPERTS, tm), 0)
    sel = jnp.zeros((N_EXPERTS, tm), F32)
    for _ in range(TOP_K):
        m = jnp.max(cand, axis=0, keepdims=True)
        first = jnp.min(jnp.where(cand == m, iota_e, N_EXPERTS), axis=0, keepdims=True)
        hit = iota_e == first
        sel = jnp.where(hit, 1.0, sel)
        cand = jnp.where(hit, neg, cand)
    w = scores * sel
    gate_ref[0] = w / jnp.sum(w, axis=0, keepdims=True) * ROUTED_SCALE


def _router(x, g, scale, shift, w_router, e_bias, tm=512):
    b, s, d = x.shape
    tm = min(tm, s)
    e = w_router.shape[1]
    row = pl.BlockSpec((1, tm, d), lambda i, j: (i, j, 0))
    mod = pl.BlockSpec((1, 1, d), lambda i, j: (i, 0, 0))
    return pl.pallas_call(
        _router_kernel,
        grid=(b, s // tm),
        in_specs=[row, _const_spec((1, d)), mod, mod, _const_spec((e, d)), _const_spec((e, 1))],
        out_specs=[row, pl.BlockSpec((1, e, tm), lambda i, j: (i, 0, j))],
        out_shape=[jax.ShapeDtypeStruct((b, s, d), BF16), jax.ShapeDtypeStruct((b, e, s), F32)],
        compiler_params=_params('parallel', 'parallel'),
        name='router',
    )(x, g.reshape(1, d), scale, shift, w_router.T, e_bias.reshape(e, 1))


def _moe_kernel(h_ref, x_ref, gate_ref, g2_ref, wg_ref, wu_ref, wd_ref, sg_ref, su_ref, sd_ref, *rest, final):
    if final:
        fg_ref, o_ref, acc_ref = rest
    else:
        o_ref, acc_ref = rest
    e = pl.program_id(2)
    h = h_ref[0]

    @pl.when(e == 0)
    def _():
        a = jnp.dot(h, sg_ref[...], preferred_element_type=F32)
        u = jnp.dot(h, su_ref[...], preferred_element_type=F32)
        acc_ref[...] = jnp.dot((_silu(a) * u).astype(BF16), sd_ref[...], preferred_element_type=F32)

    lane = lax.broadcasted_iota(jnp.int32, gate_ref.shape[1:], 1)
    gcol = jnp.sum(jnp.where(lane == e, gate_ref[0], 0.0), axis=-1, keepdims=True)
    a = jnp.dot(h, wg_ref[0].astype(BF16), preferred_element_type=F32)
    u = jnp.dot(h, wu_ref[0].astype(BF16), preferred_element_type=F32)
    acc_ref[...] += jnp.dot((_silu(a) * u * gcol).astype(BF16), wd_ref[0].astype(BF16), preferred_element_type=F32)

    @pl.when(e == pl.num_programs(2) - 1)
    def _():
        y = x_ref[0] + g2_ref[0] * acc_ref[...]
        if final:
            y = _rms(y) * fg_ref[...]
        o_ref[0] = y


def _moe(h2, x, gate, g2, layer, w_gate, w_up, w_down, s_gate, s_up, s_down, final_g=None, tm=1024):
    b, s, d = x.shape
    tm = min(tm, s)
    _, e, _, ff = w_gate.shape
    row = pl.BlockSpec((1, tm, d), lambda i, j, k: (i, j, 0))
    ins = [h2, x, gate, g2, w_gate, w_up, w_down, s_gate, s_up, s_down]
    in_specs = [row, row, pl.BlockSpec((1, tm, e), lambda i, j, k: (i, j, 0)),
                pl.BlockSpec((1, 1, d), lambda i, j, k: (i, 0, 0)),
                pl.BlockSpec((None, 1, d, ff), lambda i, j, k: (layer, k, 0, 0)),
                pl.BlockSpec((None, 1, d, ff), lambda i, j, k: (layer, k, 0, 0)),
                pl.BlockSpec((None, 1, ff, d), lambda i, j, k: (layer, k, 0, 0)),
                _const_spec(s_gate.shape), _const_spec(s_up.shape), _const_spec(s_down.shape)]
    if final_g is not None:
        ins.append(final_g.reshape(1, d))
        in_specs.append(_const_spec((1, d)))
    return pl.pallas_call(
        functools.partial(_moe_kernel, final=final_g is not None),
        grid=(b, s // tm, e),
        in_specs=in_specs,
        out_specs=row,
        out_shape=jax.ShapeDtypeStruct(x.shape, F32),
        scratch_shapes=[pltpu.VMEM((tm, d), F32)],
        compiler_params=_params('parallel', 'parallel', 'arbitrary'),
        name='moe',
    )(*ins)


def _mixers(p, pc, ctx_out, prm, l, lam_init, rope_tabs, lb_terms):
    s = p['da_q'].shape[1]
    sc = pc['da_q'].shape[1]

    hy_args = (prm['hy_w1'][l], prm['hy_b1'][l], prm['hy_w2'][l], prm['hy_b2'][l], prm['hy_w3'][l], prm['hy_b3'][l],
               prm['hy_sin_freq'][l], prm['hy_decay'][l])
    y_hy = _hyena([p['hy_v'], p['hy_x1'], p['hy_x2']], prm['hy_conv_w'][l], prm['hy_conv_b'][l],
                  _hy_filters(s, *hy_args), prm['hy_bias'][l], inner=128)
    yc_hy = None
    if ctx_out:
        yc_hy = _hyena([pc['hy_v'], pc['hy_x1'], pc['hy_x2']], prm['hy_conv_w'][l], prm['hy_conv_b'][l],
                       _hy_filters(sc, *hy_args), prm['hy_bias'][l], inner=32)

    lp = prm['da_lambda'][l].astype(F32)
    lam = jnp.exp(jnp.sum(lp[0] * lp[1])) - jnp.exp(jnp.sum(lp[2] * lp[3])) + lam_init
    da_kw = dict(heads=DA_HEADS, ncomp=2, scale=DA_HEAD_DIM ** -0.5, tq=64, lam=lam, subln_g=prm['da_subln_g'][l],
                 post_scale=1.0 - lam_init)
    da_ctx = ([pc['da_k']], pc['da_v'])
    y_da = _attention([p['da_q']], [da_ctx, ([p['da_k']], p['da_v'])], **da_kw)
    yc_da = _attention([pc['da_q']], [da_ctx], **da_kw) if ctx_out else None

    wq = prm['mla_w_q_up'][l].reshape(MLA_Q_RANK, MLA_HEADS, MLA_NOPE_DIM + MLA_ROPE_DIM)
    wq_n = wq[:, :, :MLA_NOPE_DIM].reshape(MLA_Q_RANK, -1).astype(BF16)
    wq_r = wq[:, :, MLA_NOPE_DIM:].reshape(MLA_Q_RANK, -1).astype(BF16)
    wkv = prm['mla_w_kv_up'][l].reshape(MLA_KV_RANK, MLA_HEADS, MLA_NOPE_DIM + MLA_V_DIM)
    wkv_n = wkv[:, :, :MLA_NOPE_DIM].reshape(MLA_KV_RANK, -1).astype(BF16)
    wkv_v = wkv[:, :, MLA_NOPE_DIM:].reshape(MLA_KV_RANK, -1).astype(BF16)

    def queries(qd, tabs):
        return _norm_proj(qd, prm['mla_q_norm_g'][l], [(wq_n, F32, False), (wq_r, F32, True)], rope_tabs=tabs)

    def keys_values(kvd):
        return _norm_proj(kvd, prm['mla_kv_norm_g'][l], [(wkv_n, BF16, False), (wkv_v, BF16, False)])

    kn_l, v_l = keys_values(p['mla_kv'])
    kn_c, v_c = keys_values(pc['mla_kv'])
    mla_kw = dict(heads=MLA_HEADS, ncomp=1, scale=(MLA_NOPE_DIM + MLA_ROPE_DIM) ** -0.5, tq=128)
    mla_ctx = ([kn_c, pc['mla_kr']], v_c)
    y_mla = _attention(queries(p['mla_q'], rope_tabs), [mla_ctx, ([kn_l, p['mla_kr']], v_l)], **mla_kw)
    yc_mla = _attention(queries(pc['mla_q'], None), [mla_ctx], **mla_kw) if ctx_out else None

    o, oc = _hgrn(p['hg_q'], p['hg_ff'], p['hg_fb'], p['hg_i'], pc['hg_q'], pc['hg_ff'], pc['hg_fb'], pc['hg_i'],
                  lb_terms, prm['hg_norm_g'][l])
    return (y_hy, y_da, y_mla, o), (yc_hy, yc_da, yc_mla, oc)


def kernel(x, c, ctx, c_ctx, w_ada, b_ada, norm1_g, norm2_g, w_in, w_out, hy_conv_w, hy_conv_b, hy_w1, hy_b1, hy_w2, hy_b2, hy_w3, hy_b3, hy_sin_freq, hy_decay, hy_bias, da_lambda, da_subln_g, mla_q_norm_g, mla_w_q_up, mla_kv_norm_g, mla_w_kv_up, hg_lower_bounds, hg_norm_g, moe_w_router, moe_bias, moe_w_gate, moe_w_up, moe_w_down, moe_sh_gate, moe_sh_up, moe_sh_down, final_norm_g):
    prm = dict(hy_conv_w=hy_conv_w, hy_conv_b=hy_conv_b, hy_w1=hy_w1, hy_b1=hy_b1, hy_w2=hy_w2, hy_b2=hy_b2,
               hy_w3=hy_w3, hy_b3=hy_b3, hy_sin_freq=hy_sin_freq, hy_decay=hy_decay, hy_bias=hy_bias,
               da_lambda=da_lambda, da_subln_g=da_subln_g, mla_q_norm_g=mla_q_norm_g, mla_w_q_up=mla_w_q_up,
               mla_kv_norm_g=mla_kv_norm_g, mla_w_kv_up=mla_w_kv_up, hg_norm_g=hg_norm_g)
    b, n_lat, d = x.shape
    depth = w_in.shape[0]
    rows = n_lat // GRID_W
    row_pos = jnp.repeat(jnp.arange(rows, dtype=jnp.int32), GRID_W)
    col_pos = jnp.tile(jnp.arange(GRID_W, dtype=jnp.int32), rows)
    rope_tabs = _rope_tables(row_pos, col_pos, 2 * DA_HEADS * DA_HEAD_DIM)
    lbs = jnp.cumsum(jax.nn.softmax(hg_lower_bounds.astype(F32), axis=1), axis=1)
    lbs = lbs - lbs[:, :1]
    cond = jnp.concatenate([c, c_ctx[None], jnp.zeros((8 - b - 1, d), F32)], axis=0)

    for l in range(depth):
        ctx_out = l < depth - 1
        mods = _ada(cond, w_ada[l], b_ada[l])
        sh1, sc1, g1, sh2, sc2, g2 = [m[:, None, :] for m in jnp.split(mods[:b], 6, axis=-1)]
        mc = [jnp.broadcast_to(m[:, None, :], (b, 1, d)) for m in jnp.split(mods[b:b + 1], 6, axis=-1)]

        off = 0
        outs = []
        for _, wdt, dt, rope, rep in _SEGMENTS:
            w = w_in[l][:, off:off + wdt].astype(BF16)
            outs.append((jnp.tile(w, (1, rep)) if rep > 1 else w, dt, rope))
            off += wdt
        names = [seg[0] for seg in _SEGMENTS]
        p = dict(zip(names, _norm_proj(x, norm1_g[l], outs, sc1, sh1, rope_tabs=rope_tabs)))
        pc = dict(zip(names, _norm_proj(ctx, norm1_g[l], outs, mc[1], mc[0])))

        lb = lbs[:, l]
        lb_terms = jnp.stack([jnp.log(lb), jnp.log1p(-lb), 1.0 - lb], axis=1)
        lam_init = 0.8 - 0.6 * math.exp(-0.3 * l)
        lat_parts, ctx_parts = _mixers(p, pc, ctx_out, prm, l, lam_init, rope_tabs, lb_terms)

        w_out_b = w_out[l].astype(BF16)
        moe_w = (l, moe_w_gate, moe_w_up, moe_w_down,
                 moe_sh_gate[l].astype(BF16), moe_sh_up[l].astype(BF16), moe_sh_down[l].astype(BF16))

        if ctx_out:
            ctx = _out_proj(*ctx_parts, pc['hg_g'], ctx, mc[2], w_out_b)
            h2c, gate_c = _router(ctx, norm2_g[l], mc[4], mc[3], moe_w_router[l], moe_bias[l])
            ctx = _moe(h2c, ctx, gate_c.transpose(0, 2, 1), mc[5], *moe_w)

        x = _out_proj(*lat_parts, p['hg_g'], x, g1, w_out_b)
        h2, gate = _router(x, norm2_g[l], sc2, sh2, moe_w_router[l], moe_bias[l])
        x = _moe(h2, x, gate.transpose(0, 2, 1), g2, *moe_w, final_g=None if ctx_out else final_norm_g)

    return x
```

```python
import functools
import math

import numpy as np
import jax
import jax.numpy as jnp
from jax import lax
from jax.experimental import pallas as pl
from jax.experimental.pallas import tpu as pltpu

F32 = jnp.float32
BF16 = jnp.bfloat16
HIGHEST = lax.Precision.HIGHEST

D_MODEL = 1024
GRID_W = 64
HY_WIDTH = 256
HY_ORDER = 2
HY_BANDS = 16
DA_HEADS = 4
DA_HEAD_DIM = 32
MLA_HEADS = 4
MLA_Q_RANK = 192
MLA_KV_RANK = 128
MLA_NOPE_DIM = 64
MLA_ROPE_DIM = 32
MLA_V_DIM = 64
HG_HEADS = 4
HG_KEY_DIM = 64
HG_VAL_DIM = 64
HG_CHUNK = 64
HG_SUB = 8
N_EXPERTS = 64
N_EXPERT_GROUPS = 8
TOPK_GROUPS = 4
TOP_K = 8
EXPERT_FF = 256
ROUTED_SCALE = 2.5
ROPE_BASE = 10000.0
NORM_EPS = 1e-6

V7X_VMEM_LIMIT_BYTES = 56 * 1024 * 1024
LANES = 128

_SEGMENTS = (
    ('hy_v', HY_WIDTH, F32, False, 0, 1), ('hy_x1', HY_WIDTH, F32, False, 0, 1), ('hy_x2', HY_WIDTH, F32, False, 0, 1),
    ('da_q', 2 * DA_HEADS * DA_HEAD_DIM, F32, True, 2 * DA_HEADS, 1),
    ('da_k', 2 * DA_HEADS * DA_HEAD_DIM, BF16, True, 2 * DA_HEADS, 1),
    ('da_v', 2 * DA_HEADS * DA_HEAD_DIM, BF16, False, DA_HEADS, 1),
    ('mla_q', MLA_Q_RANK, F32, False, 0, 1), ('mla_kv', MLA_KV_RANK, F32, False, 0, 1),
    ('mla_kr', MLA_ROPE_DIM, BF16, True, MLA_HEADS, MLA_HEADS),
    ('hg_q', HG_HEADS * HG_KEY_DIM, F32, False, 0, 1), ('hg_ff', HG_HEADS * HG_KEY_DIM, F32, False, 0, 1),
    ('hg_fb', HG_HEADS * HG_KEY_DIM, F32, False, 0, 1), ('hg_i', HG_HEADS * HG_VAL_DIM, F32, False, 0, 1),
    ('hg_g', HG_HEADS * HG_VAL_DIM, F32, False, 0, 1),
)


def _params(*semantics):
    return pltpu.CompilerParams(dimension_semantics=semantics, vmem_limit_bytes=V7X_VMEM_LIMIT_BYTES)


def _const_spec(shape):
    nd = len(shape)
    return pl.BlockSpec(shape, lambda *_: (0,) * nd)


def _rms(x, eps=NORM_EPS):
    return x * lax.rsqrt(jnp.mean(x * x, axis=-1, keepdims=True) + eps)


def _silu(x):
    return x * jax.nn.sigmoid(x)


def _dot_nt(a, b, **kw):
    return lax.dot_general(a, b, (((1,), (1,)), ((), ())), preferred_element_type=F32, **kw)


def _ada_kernel(c_ref, w_ref, b_ref, o_ref):
    s = _silu(c_ref[...])
    o_ref[...] = jnp.dot(s, w_ref[...], precision=HIGHEST, preferred_element_type=F32) + b_ref[...]


def _ada(cond, w, b):
    r, d = cond.shape
    n = w.shape[1]
    tn = 1536
    return pl.pallas_call(
        _ada_kernel,
        grid=(n // tn,),
        in_specs=[_const_spec((r, d)), pl.BlockSpec((d, tn), lambda j: (0, j)), pl.BlockSpec((1, tn), lambda j: (0, j))],
        out_specs=pl.BlockSpec((r, tn), lambda j: (0, j)),
        out_shape=jax.ShapeDtypeStruct((r, n), F32),
        compiler_params=_params('arbitrary'),
        name='ada',
    )(cond, w, b.reshape(1, n))


ROPE_UNIT = 32


def _rope_tables(row, col, width):
    n = ROPE_UNIT // 4
    inv = ROPE_BASE ** (-jnp.arange(n, dtype=F32) / n)
    units = width // ROPE_UNIT
    parts_c, parts_a, parts_b = [], [], []
    zero = jnp.zeros((row.shape[0], n), F32)
    for pos in (row, col):
        ang = pos.astype(F32)[:, None] * inv
        cos, sin = jnp.cos(ang), jnp.sin(ang)
        parts_c += [cos, cos]
        parts_a += [zero, sin]
        parts_b += [-sin, zero]
    tile = lambda ps: jnp.tile(jnp.concatenate(ps, axis=1), (1, units))
    return tile(parts_c), tile(parts_a), tile(parts_b)


def _norm_proj_kernel(*refs, n_w, modulate, ropes, splits):
    x_ref, g_ref = refs[0], refs[1]
    pos = 2
    if modulate:
        sc_ref, sh_ref = refs[2], refs[3]
        pos = 4
    if any(ropes):
        rc_ref, ra_ref, rb_ref = refs[pos:pos + 3]
        pos += 3
    w_refs = refs[pos:pos + n_w]
    o_refs = refs[pos + n_w:]
    y = _rms(x_ref[0]) * g_ref[...]
    if modulate:
        y = y * (1.0 + sc_ref[0]) + sh_ref[0]
    yb = y.astype(BF16)
    for w_ref, o_ref, rope, split in zip(w_refs, o_refs, ropes, splits):
        o = jnp.dot(yb, w_ref[...], preferred_element_type=F32)
        if rope:
            wd = o.shape[1]
            shift = ROPE_UNIT // 4
            o = (o * rc_ref[:, :wd] + pltpu.roll(o, shift, axis=1) * ra_ref[:, :wd]
                 + pltpu.roll(o, wd - shift, axis=1) * rb_ref[:, :wd])
        if split:
            unit = o.shape[1] // split
            for u in range(split):
                o_ref[0, u] = o[:, u * unit:(u + 1) * unit].astype(o_ref.dtype)
        else:
            o_ref[0] = o.astype(o_ref.dtype)


def _norm_proj(x, g, outs, scale=None, shift=None, rope_tabs=None, tm=512):
    b, s, k = x.shape
    tm = min(tm, s)
    modulate = scale is not None
    ropes = tuple(bool(o[2]) and rope_tabs is not None for o in outs)
    splits = tuple(o[3] for o in outs)
    ins = [x, g.reshape(1, k)]
    in_specs = [pl.BlockSpec((1, tm, k), lambda i, j: (i, j, 0)), _const_spec((1, k))]
    if modulate:
        ins += [scale, shift]
        in_specs += [pl.BlockSpec((1, 1, k), lambda i, j: (i, 0, 0))] * 2
    if any(ropes):
        ins += list(rope_tabs)
        in_specs += [pl.BlockSpec((tm, rope_tabs[0].shape[1]), lambda i, j: (j, 0))] * 3
    out_specs, out_shape = [], []
    for w, dt, _, split in outs:
        ins.append(w)
        in_specs.append(_const_spec(w.shape))
        n = w.shape[1]
        if split:
            out_specs.append(pl.BlockSpec((1, split, tm, n // split), lambda i, j: (i, 0, j, 0)))
            out_shape.append(jax.ShapeDtypeStruct((b, split, s, n // split), dt))
        else:
            out_specs.append(pl.BlockSpec((1, tm, n), lambda i, j: (i, j, 0)))
            out_shape.append(jax.ShapeDtypeStruct((b, s, n), dt))
    return pl.pallas_call(
        functools.partial(_norm_proj_kernel, n_w=len(outs), modulate=modulate, ropes=ropes, splits=splits),
        grid=(b, s // tm),
        in_specs=in_specs,
        out_specs=out_specs,
        out_shape=out_shape,
        compiler_params=_params('parallel', 'parallel'),
        name='norm_proj',
    )(*ins)


def _hy_filter_kernel(w1t_ref, w1s_ref, w1c_ref, b1_ref, w2_ref, b2_ref, w3_ref, b3_ref, fr_ref, dec_ref, o_ref, *, n):
    t = lax.broadcasted_iota(jnp.int32, (n, 1), 0).astype(F32) / n
    bands = lax.broadcasted_iota(jnp.int32, (1, HY_BANDS), 1).astype(F32) + 1.0
    ang = (2.0 * jnp.pi) * t * bands
    pre = (t * w1t_ref[...]
           + jnp.dot(jnp.sin(ang), w1s_ref[...], precision=HIGHEST, preferred_element_type=F32)
           + jnp.dot(jnp.cos(ang), w1c_ref[...], precision=HIGHEST, preferred_element_type=F32)
           + b1_ref[...])
    hid = jnp.sin(fr_ref[0:1, :] * pre)
    hid = jnp.sin(fr_ref[1:2, :] * (jnp.dot(hid, w2_ref[...], precision=HIGHEST, preferred_element_type=F32) + b2_ref[...]))
    filt = jnp.dot(hid, w3_ref[...], precision=HIGHEST, preferred_element_type=F32) + b3_ref[...]
    filt = filt * jnp.exp(-t * jnp.abs(dec_ref[...]))
    col = jnp.sum(jnp.abs(filt), axis=0, keepdims=True) - jnp.abs(filt[0:1, :])
    w = HY_WIDTH
    for o in range(HY_ORDER):
        lo = o * 2 * w
        f0 = filt[0:1, lo:lo + w] + filt[0:1, lo + w:lo + 2 * w]
        inv = 1.0 / (col[:, lo:lo + w] + col[:, lo + w:lo + 2 * w] + jnp.abs(f0))
        o_ref[:, lo:lo + w] = filt[:, lo:lo + w] * inv
        o_ref[:, lo + w:lo + 2 * w] = filt[:, lo + w:lo + 2 * w] * inv


def _hy_filters(n, w1, b1, w2, b2, w3, b3, freq, decay):
    cols = w3.shape[1]
    ins = [w1[0:1], w1[1:1 + HY_BANDS], w1[1 + HY_BANDS:], b1.reshape(1, -1), w2, b2.reshape(1, -1), w3,
           b3.reshape(1, -1), freq, decay.reshape(1, -1)]
    out = pl.pallas_call(
        functools.partial(_hy_filter_kernel, n=n),
        grid=(1,),
        in_specs=[_const_spec(a.shape) for a in ins],
        out_specs=_const_spec((n, cols)),
        out_shape=jax.ShapeDtypeStruct((n, cols), F32),
        compiler_params=_params('arbitrary'),
        name='hy_filter',
    )(*ins)
    return out.reshape(n, HY_ORDER, 2, HY_WIDTH)


def _two_sided(filt_n):
    n = filt_n.shape[0]
    hf, hb = filt_n[:, :, 0], filt_n[:, :, 1]
    h = jnp.concatenate([hf[:1] + hb[:1], hf[1:], jnp.zeros((1,) + hf.shape[1:], F32), hb[:0:-1]], axis=0)
    return h.reshape(2 * n, HY_ORDER * HY_WIDTH)


def _short_conv_kernel(*refs, s):
    x_refs, w_refs, b_refs, o_refs = refs[0:3], refs[3:6], refs[6:9], refs[9:12]
    row = lax.broadcasted_iota(jnp.int32, (s, 1), 0)
    for x_ref, w_ref, b_ref, o_ref in zip(x_refs, w_refs, b_refs, o_refs):
        x = x_ref[0]
        prev = jnp.where(row == 0, 0.0, pltpu.roll(x, 1, axis=0))
        nxt = jnp.where(row == s - 1, 0.0, pltpu.roll(x, s - 1, axis=0))
        o_ref[0] = prev * w_ref[0:1, :] + x * w_ref[1:2, :] + nxt * w_ref[2:3, :] + b_ref[...]


def _short_conv(parts, conv_w, conv_b):
    b, s, c = parts[0].shape
    tc = LANES
    ws = [conv_w[:, i * c:(i + 1) * c] for i in range(3)]
    bs = [conv_b[i * c:(i + 1) * c].reshape(1, c) for i in range(3)]
    xspec = pl.BlockSpec((1, s, tc), lambda i, j: (i, 0, j))
    return pl.pallas_call(
        functools.partial(_short_conv_kernel, s=s),
        grid=(b, c // tc),
        in_specs=[xspec] * 3 + [pl.BlockSpec((3, tc), lambda i, j: (0, j))] * 3 + [pl.BlockSpec((1, tc), lambda i, j: (0, j))] * 3,
        out_specs=[xspec] * 3,
        out_shape=[jax.ShapeDtypeStruct((b, s, c), F32)] * 3,
        compiler_params=_params('parallel', 'parallel'),
        name='short_conv',
    )(*parts, *ws, *bs)


def _dft_cos_sin(rows, cols, period):
    ang = 2.0 * np.pi * ((np.arange(rows)[:, None] * np.arange(cols)[None, :]) % period) / period
    return np.cos(ang), np.sin(ang)


def _fft_tables(n, inner):
    big = 2 * n
    n1 = big // inner
    c1, s1 = _dft_cos_sin(n1, n1, n1)
    h = n1 // 2
    outer_data = np.block([[c1[:, :h], s1[:, :h]], [-s1[:, :h], c1[:, :h]]])
    outer_real = np.concatenate([c1, -s1], axis=0)
    outer_inv = np.block([[c1[:h, :], -s1[:h, :]], [s1[:h, :], c1[:h, :]]]) / big
    c2, s2 = _dft_cos_sin(inner, inner, inner)
    inner_fwd = np.block([[c2, s2], [-s2, c2]])
    inner_inv = np.block([[c2, -s2], [s2, c2]])
    ct, st = _dft_cos_sin(n1, inner, big)
    f = lambda a: jnp.asarray(a, F32)
    return dict(n1=n1, inner=inner, outer_data=f(outer_data), outer_real=f(outer_real), outer_inv=f(outer_inv),
                inner_fwd=_hi_lo_cols(inner_fwd), inner_inv=_hi_lo_cols(inner_inv),
                tw_cos=f(ct).reshape(n1, inner, 1), tw_sin=f(st).reshape(n1, inner, 1))


def _left_mm_kernel(m_ref, x_ref, o_ref):
    o_ref[0] = jnp.dot(m_ref[...], x_ref[0], precision=HIGHEST, preferred_element_type=F32)


def _left_mm(m, x, tl=4096):
    p, k, l = x.shape
    mm = m.shape[0]
    tl = min(tl, l)
    return pl.pallas_call(
        _left_mm_kernel,
        grid=(p, l // tl),
        in_specs=[_const_spec(m.shape), pl.BlockSpec((1, k, tl), lambda i, j: (i, 0, j))],
        out_specs=pl.BlockSpec((1, mm, tl), lambda i, j: (i, 0, j)),
        out_shape=jax.ShapeDtypeStruct((p, mm, l), F32),
        compiler_params=_params('parallel', 'parallel'),
        name='fft_outer',
    )(m, x)


def _hi_lo_cols(m):
    m = np.asarray(m, np.float32)
    hi = m.astype(BF16)
    lo = (m - hi.astype(np.float32)).astype(BF16)
    return jnp.asarray(np.concatenate([hi, hi, lo], axis=1))


def _hi_lo_rows(x):
    hi = x.astype(BF16)
    lo = (x - hi.astype(F32)).astype(BF16)
    return jnp.concatenate([hi, lo, hi], axis=0)


def _inner_kernel(a_ref, twc_ref, tws_ref, gf_ref, *rest, convolve, inner, kb):
    for s in range(kb):
        ar, ai = a_ref[0, 0, s], a_ref[0, 1, s]
        tc, ts = twc_ref[s], tws_ref[s]
        br = ar * tc + ai * ts
        bi = ai * tc - ar * ts
        x = jnp.dot(gf_ref[...], _hi_lo_rows(jnp.concatenate([br, bi], axis=0)), preferred_element_type=F32)
        if not convolve:
            o_ref = rest[0]
            o_ref[0, 0, s] = x[:inner]
            o_ref[0, 1, s] = x[inner:]
            continue
        h_ref, gi_ref, o_ref = rest
        xr, xi = x[:inner], x[inner:]
        hr, hi = h_ref[0, 0, s], h_ref[0, 1, s]
        yr = xr * hr - xi * hi
        yi = xr * hi + xi * hr
        z = jnp.dot(gi_ref[...], _hi_lo_rows(jnp.concatenate([yr, yi], axis=0)), preferred_element_type=F32)
        zr, zi = z[:inner], z[inner:]
        o_ref[0, 0, s] = zr * tc - zi * ts
        o_ref[0, 1, s] = zi * tc + zr * ts


def _fft_inner(a, tab, c, h=None, h_block=0):
    p = a.shape[0]
    n1, inner = tab['n1'], tab['inner']
    a5 = a.reshape(p, 2, n1, inner, c)
    tc = 2 * LANES
    kb = 4
    blk = pl.BlockSpec((1, 2, kb, inner, tc), lambda i, k, j: (i, 0, k, 0, j))
    tw_spec = pl.BlockSpec((kb, inner, 1), lambda i, k, j: (k, 0, 0))
    ins = [a5, tab['tw_cos'], tab['tw_sin'], tab['inner_fwd']]
    in_specs = [blk, tw_spec, tw_spec, _const_spec(tab['inner_fwd'].shape)]
    if h is not None:
        ch = h.shape[-1] // inner
        nb = c // tc
        ins += [h.reshape(1, 2, n1, inner, ch), tab['inner_inv']]
        in_specs += [pl.BlockSpec((1, 2, kb, inner, tc), lambda i, k, j: (0, 0, k, 0, h_block * nb + j)),
                     _const_spec(tab['inner_inv'].shape)]
    out = pl.pallas_call(
        functools.partial(_inner_kernel, convolve=h is not None, inner=inner, kb=kb),
        grid=(p, n1 // kb, c // tc),
        in_specs=in_specs,
        out_specs=blk,
        out_shape=jax.ShapeDtypeStruct(a5.shape, F32),
        compiler_params=_params('parallel', 'parallel', 'parallel'),
        name='fft_inner',
    )(*ins)
    return out.reshape(p, 2 * n1, inner * c)


def _gate_kernel(m_ref, z_ref, u_ref, x_ref, bias_ref, *rest, chain):
    y = jnp.dot(m_ref[...], z_ref[0], precision=HIGHEST, preferred_element_type=F32)
    nxt = x_ref[0] * (y + u_ref[0] * bias_ref[...])
    if chain:
        mf_ref, o_ref, a_ref = rest
        o_ref[0] = nxt
        a_ref[0] = jnp.dot(mf_ref[...], nxt, precision=HIGHEST, preferred_element_type=F32)
    else:
        rest[0][0] = nxt


def _fft_gate(tab, z, u, x, bias_l, chain, tl=4096):
    p, k2, l = z.shape
    n1 = tab['n1']
    tl = min(tl, l)
    row = pl.BlockSpec((1, n1, tl), lambda i, j: (i, 0, j))
    ins = [tab['outer_inv'], z, u, x, bias_l]
    in_specs = [_const_spec((n1, k2)), pl.BlockSpec((1, k2, tl), lambda i, j: (i, 0, j)), row, row,
                pl.BlockSpec((1, tl), lambda i, j: (0, j))]
    out_specs = [row]
    out_shape = [jax.ShapeDtypeStruct((p, n1, l), F32)]
    if chain:
        ins.append(tab['outer_data'])
        in_specs.append(_const_spec((k2, n1)))
        out_specs.append(pl.BlockSpec((1, k2, tl), lambda i, j: (i, 0, j)))
        out_shape.append(jax.ShapeDtypeStruct((p, k2, l), F32))
    return pl.pallas_call(
        functools.partial(_gate_kernel, chain=chain),
        grid=(p, l // tl),
        in_specs=in_specs,
        out_specs=out_specs,
        out_shape=out_shape,
        compiler_params=_params('parallel', 'parallel'),
        name='fft_gate',
    )(*ins)


def _hyena(parts, conv_w, conv_b, filt_n, bias, inner):
    b, s, c = parts[0].shape
    tab = _fft_tables(s, inner)
    n1 = tab['n1']
    lanes = inner * c
    h_taps = _two_sided(filt_n).reshape(1, n1, inner * HY_ORDER * c)
    h_spec = _fft_inner(_left_mm(tab['outer_real'], h_taps), tab, HY_ORDER * c)
    v, x1, x2 = [a.reshape(b // 2, n1, lanes) for a in _short_conv(parts, conv_w, conv_b)]
    bias_l = [jnp.tile(bias[o], inner).reshape(1, lanes) for o in range(HY_ORDER)]
    a = _left_mm(tab['outer_data'], v)
    z = _fft_inner(a, tab, c, h_spec, 0)
    z2, a = _fft_gate(tab, z, v, x1, bias_l[0], chain=True)
    z = _fft_inner(a, tab, c, h_spec, 1)
    (z3,) = _fft_gate(tab, z, z2, x2, bias_l[1], chain=False)
    return z3.reshape(b, s, c)


def _attn_kernel(*refs, n_q, n_pieces, ncomp, scale, post_scale):
    q_refs = refs[:n_q]
    pos = n_q
    pieces = []
    for _ in range(n_pieces):
        pieces.append((refs[pos:pos + n_q], refs[pos + n_q]))
        pos += n_q + 1
    if ncomp == 2:
        lam_ref, g_ref = refs[pos:pos + 2]
        pos += 2
    o_ref = refs[pos]
    outs = []
    for c in range(ncomp):
        qs = [(q_ref[0, c] * (scale * math.log2(math.e))).astype(BF16) for q_ref in q_refs]
        scores = []
        for k_refs, _ in pieces:
            s = None
            for q, k_ref in zip(qs, k_refs):
                t = _dot_nt(q, k_ref[0, c if k_ref.shape[1] == ncomp else 0])
                s = t if s is None else s + t
            scores.append(s)
        m = None
        for s in scores:
            mp = jnp.max(s, axis=-1, keepdims=True)
            m = mp if m is None else jnp.maximum(m, mp)
        l, o = None, None
        for s, (_, v_ref) in zip(scores, pieces):
            p = jnp.exp2(s - m)
            lp = jnp.sum(p, axis=-1, keepdims=True)
            op = jnp.dot(p.astype(BF16), v_ref[0, 0], preferred_element_type=F32)
            l = lp if l is None else l + lp
            o = op if o is None else o + op
        outs.append(o / l)
    if ncomp == 2:
        o = outs[0] - lam_ref[0] * outs[1]
        o = _rms(o) * g_ref[...] * post_scale
    else:
        o = outs[0]
    o_ref[0, 0] = o


def _attention(q_parts, pieces, heads, ncomp, scale, tq=256, lam=None, subln_g=None, post_scale=1.0):
    b, _, sq, _ = q_parts[0].shape
    dv = pieces[0][1].shape[3]
    tq = min(tq, sq)
    ins = list(q_parts)
    in_specs = [pl.BlockSpec((1, ncomp, tq, q.shape[3]), lambda i, h, j: (i, h, j, 0)) for q in q_parts]
    for k_parts, v in pieces:
        for k in k_parts:
            ins.append(k)
            if k.shape[1] == 1:
                in_specs.append(pl.BlockSpec((1, 1) + k.shape[2:], lambda i, h, j: (i, 0, 0, 0)))
            else:
                in_specs.append(pl.BlockSpec((1, ncomp) + k.shape[2:], lambda i, h, j: (i, h, 0, 0)))
        ins.append(v)
        in_specs.append(pl.BlockSpec((1, 1) + v.shape[2:], lambda i, h, j: (i, h, 0, 0)))
    if ncomp == 2:
        ins += [lam.reshape(1), subln_g.reshape(1, dv)]
        in_specs += [pl.BlockSpec(memory_space=pltpu.SMEM), _const_spec((1, dv))]
    return pl.pallas_call(
        functools.partial(_attn_kernel, n_q=len(q_parts), n_pieces=len(pieces), ncomp=ncomp, scale=scale,
                          post_scale=post_scale),
        grid=(b, heads, sq // tq),
        in_specs=in_specs,
        out_specs=pl.BlockSpec((1, 1, tq, dv), lambda i, h, j: (i, h, j, 0)),
        out_shape=jax.ShapeDtypeStruct((b, heads, sq, dv), F32),
        compiler_params=_params('parallel', 'parallel', 'parallel'),
        name='attention',
    )(*ins)


def _forget_terms(f, log_lb, log_1m_lb, one_m_lb):
    log_sig = jnp.minimum(f, 0.0) - jnp.log1p(jnp.exp(-jnp.abs(f)))
    b = log_1m_lb + log_sig
    log_g = jnp.maximum(log_lb, b) + jnp.log1p(jnp.exp(-jnp.abs(log_lb - b)))
    return log_g, one_m_lb * jax.nn.sigmoid(-f)


def _hg_tables():
    ck, sub = HG_CHUNK, HG_SUB
    t = np.arange(ck)
    cum_mats, half_masks, group_masks, sels = [], [], [], []
    for rev in (False, True):
        mats = [(t[None, :] >= t[:, None]) if rev else (t[None, :] <= t[:, None])]
        halves = []
        hs = ck // 2
        while hs >= sub:
            pos = t % (2 * hs)
            b = t - pos + hs
            mats.append((t[None, :] >= b[:, None]) if rev else (t[None, :] < b[:, None]))
            q_half = (pos < hs) if rev else (pos >= hs)
            halves.append(np.stack([q_half, ~q_half]))
            if not rev:
                grp = (t[:, None] // (2 * hs)) == (t[None, :] // (2 * hs))
                group_masks.append(np.concatenate([grp, grp], axis=0))
            hs //= 2
        cum_mats.append(np.concatenate(mats, axis=0))
        half_masks.append(np.stack(halves))
        r, c = np.arange(ck)[:, None], np.arange(ck * sub)[None, :]
        same = (c // sub) == r
        tt, ss = (c // sub) % sub, c % sub
        sels.append(same & ((ss >= tt) if rev else (ss <= tt)))
    lanes = 2 * HG_KEY_DIM
    ln = np.arange(lanes)
    bd = (ln[:, None] // HG_KEY_DIM) == (ln[None, :] // HG_KEY_DIM)
    hm = np.broadcast_to(np.stack(half_masks)[..., None], (2, len(half_masks[0]), 2, ck, lanes))
    return (jnp.asarray(np.stack(cum_mats), BF16), jnp.asarray(hm, F32), jnp.asarray(np.stack(group_masks), F32),
            jnp.asarray(np.stack(sels), BF16), jnp.asarray(bd, F32))


def _split3(x):
    a = x.astype(BF16)
    r = x - a.astype(F32)
    b = r.astype(BF16)
    return a, b, (r - b.astype(F32)).astype(BF16)


def _hg_chunk(q, k, v, lg, st, rev, cm, hm, gm, sel, bd, m0, m1):
    ck, sub = HG_CHUNK, HG_SUB
    call = sum(jnp.dot(cm, piece, preferred_element_type=F32) for piece in _split3(lg))
    cum = call[0:ck]
    tot = cum[0:1] if rev else cum[ck - 1:ck]
    o = _dot_nt((q * jnp.exp(cum)).astype(BF16), st.astype(BF16))
    kd = (k * jnp.exp(tot - cum)).astype(BF16)
    st_new = st * jnp.exp(tot) + bd * jnp.dot(v.T.astype(BF16), kd, preferred_element_type=F32)
    s2 = None
    for lv in range(gm.shape[0]):
        cb = call[(lv + 1) * ck:(lv + 2) * ck]
        qd = q * jnp.exp(jnp.minimum(cum - cb, 0.0)) * hm[lv, 0]
        kf = (k * jnp.exp(jnp.minimum(cb - cum, 0.0)) * hm[lv, 1]).astype(BF16)
        q2 = jnp.concatenate([qd * m0, qd * m1], axis=0).astype(BF16)
        term = _dot_nt(q2, kf) * gm[lv]
        s2 = term if s2 is None else s2 + term
    r = jnp.dot(s2.astype(BF16), v.astype(BF16), preferred_element_type=F32)
    o = o + m0 * r[:ck] + m1 * r[ck:]
    rows, vts = [], []
    for i in range(ck // sub):
        lo, hi = i * sub, (i + 1) * sub
        ki, ci = k[lo:hi], cum[lo:hi]
        for t in range(lo, hi):
            rows.append((q[t:t + 1] * ki * jnp.exp(jnp.minimum(cum[t:t + 1] - ci, 0.0))).astype(BF16))
            vts.append(v[lo:hi])
    sc = jnp.dot(jnp.concatenate(rows, axis=0), bd.astype(BF16), preferred_element_type=F32)
    o = o + jnp.dot(sel, (sc * jnp.concatenate(vts, axis=0)).astype(BF16), preferred_element_type=F32)
    return o, st_new


def _hgrn_kernel(q_ref, ff_ref, fb_ref, i_ref, qc_ref, ffc_ref, fbc_ref, ic_ref, lb_ref, g_ref,
                 cm_ref, hm_ref, gm_ref, sel_ref, bd_ref, o_ref, oc_ref, or_ref, ocr_ref, st_ref, *, n_lat, n_ctx):
    ck = HG_CHUNK
    lanes = o_ref.shape[-1]
    lane = lax.broadcasted_iota(jnp.int32, (1, lanes), 1)
    m0 = (lane < HG_KEY_DIM).astype(F32)
    m1 = 1.0 - m0
    bd = bd_ref[...]
    gm = gm_ref[...]

    def one(q, f, v, rev):
        d = 1 if rev else 0
        lg, k = _forget_terms(f, lb_ref[d, 0:1, :], lb_ref[d, 1:2, :], lb_ref[d, 2:3, :])
        o, st = _hg_chunk(q, k, v, lg, st_ref[d], rev, cm_ref[d], hm_ref[d], gm, sel_ref[d], bd, m0, m1)
        st_ref[d] = st
        return o

    def sweep(qr, ffr, fbr, ir, out_f, out_r, n):
        nc = n // ck

        def body(step, carry):
            idf = pl.ds(pl.multiple_of(step * ck, ck), ck)
            idr = pl.ds(pl.multiple_of((nc - 1 - step) * ck, ck), ck)
            out_f[0, idf, :] = one(qr[0, idf, :], ffr[0, idf, :], ir[0, idf, :], False)
            out_r[idr, :] = one(qr[0, idr, :], fbr[0, idr, :], ir[0, idr, :], True)
            return carry

        lax.fori_loop(0, nc, body, 0)

    st_ref[...] = jnp.zeros(st_ref.shape, F32)
    sweep(qc_ref, ffc_ref, fbc_ref, ic_ref, oc_ref, ocr_ref, n_ctx)
    sweep(q_ref, ff_ref, fb_ref, i_ref, o_ref, or_ref, n_lat)

    mean_mat = bd * (1.0 / HG_VAL_DIM)

    def readout(out, out_r, n):
        tile = min(n, 512)

        def body(step, carry):
            idx = pl.ds(pl.multiple_of(step * tile, tile), tile)
            x = out[0, idx, :] + out_r[idx, :]
            ms = jnp.dot(x * x, mean_mat, precision=HIGHEST, preferred_element_type=F32)
            out[0, idx, :] = x * lax.rsqrt(ms + NORM_EPS) * g_ref[...]
            return carry

        lax.fori_loop(0, n // tile, body, 0)

    readout(oc_ref, ocr_ref, n_ctx)
    readout(o_ref, or_ref, n_lat)


def _hgrn(q, ff, fb, iv, qc, ffc, fbc, ic, lb_terms, norm_g):
    b, n_lat, width = q.shape
    n_ctx = qc.shape[1]
    lanes = 2 * HG_KEY_DIM
    tables = _hg_tables()
    lat = pl.BlockSpec((1, n_lat, lanes), lambda i, j: (i, 0, j))
    ctx = pl.BlockSpec((1, n_ctx, lanes), lambda i, j: (i, 0, j))
    g2 = jnp.tile(norm_g, 2).reshape(1, lanes)
    return pl.pallas_call(
        functools.partial(_hgrn_kernel, n_lat=n_lat, n_ctx=n_ctx),
        grid=(b, width // lanes),
        in_specs=[lat] * 4 + [ctx] * 4 + [pl.BlockSpec((2, 3, lanes), lambda i, j: (0, 0, j)), _const_spec((1, lanes))]
                 + [_const_spec(t.shape) for t in tables],
        out_specs=[lat, ctx],
        out_shape=[jax.ShapeDtypeStruct(q.shape, F32), jax.ShapeDtypeStruct(qc.shape, F32)],
        scratch_shapes=[pltpu.VMEM((n_lat, lanes), F32), pltpu.VMEM((n_ctx, lanes), F32), pltpu.VMEM((2, lanes, lanes), F32)],
        compiler_params=_params('parallel', 'parallel'),
        name='hgrn2',
    )(q, ff, fb, iv, qc, ffc, fbc, ic, lb_terms, g2, *tables)


def _out_proj_kernel(hy_ref, da_ref, mla_ref, hg_ref, gate_ref, x_ref, g1_ref, w_ref, o_ref):
    c = hy_ref.shape[2]
    acc = jnp.dot(hy_ref[0].astype(BF16), w_ref[0:c, :], preferred_element_type=F32)
    for i, head_ref in ((1, da_ref), (2, mla_ref)):
        dv = head_ref.shape[3]
        for h in range(head_ref.shape[1]):
            lo = i * c + h * dv
            acc = acc + jnp.dot(head_ref[0, h].astype(BF16), w_ref[lo:lo + dv, :], preferred_element_type=F32)
    hg = hg_ref[0] * _silu(gate_ref[0])
    acc = acc + jnp.dot(hg.astype(BF16), w_ref[3 * c:4 * c, :], preferred_element_type=F32)
    o_ref[0] = x_ref[0] + g1_ref[0] * acc


def _out_proj(y_hy, y_da, y_mla, y_hg, gate, x, g1, w_out, tm=512):
    b, s, d = x.shape
    tm = min(tm, s)
    c = y_hy.shape[2]
    part = pl.BlockSpec((1, tm, c), lambda i, j: (i, j, 0))
    headed = lambda a: pl.BlockSpec((1, a.shape[1], tm, a.shape[3]), lambda i, j: (i, 0, j, 0))
    row = pl.BlockSpec((1, tm, d), lambda i, j: (i, j, 0))
    return pl.pallas_call(
        _out_proj_kernel,
        grid=(b, s // tm),
        in_specs=[part, headed(y_da), headed(y_mla), part, part, row, pl.BlockSpec((1, 1, d), lambda i, j: (i, 0, 0)),
                  _const_spec(w_out.shape)],
        out_specs=row,
        out_shape=jax.ShapeDtypeStruct(x.shape, F32),
        compiler_params=_params('parallel', 'parallel'),
        name='out_proj',
    )(y_hy, y_da, y_mla, y_hg, gate, x, g1, w_out)


def _router_kernel(x_ref, g_ref, sc_ref, sh_ref, wrt_ref, bias_ref, h_ref, gate_ref):
    h = _rms(x_ref[0]) * g_ref[...] * (1.0 + sc_ref[0]) + sh_ref[0]
    h_ref[0] = h.astype(BF16)
    tm = h.shape[0]
    scores = jax.nn.sigmoid(_dot_nt(wrt_ref[...], h, precision=HIGHEST))
    choice = scores + bias_ref[...]
    per = N_EXPERTS // N_EXPERT_GROUPS
    neg = -jnp.inf
    iota_g = lax.broadcasted_iota(jnp.int32, (per, tm), 0)
    grp_rows = []
    for gi in range(N_EXPERT_GROUPS):
        blk = choice[gi * per:(gi + 1) * per]
        m1 = jnp.max(blk, axis=0, keepdims=True)
        first = jnp.min(jnp.where(blk == m1, iota_g, per), axis=0, keepdims=True)
        m2 = jnp.max(jnp.where(iota_g == first, neg, blk), axis=0, keepdims=True)
        grp_rows.append(m1 + m2)
    grp = jnp.concatenate(grp_rows, axis=0)
    iota_n = lax.broadcasted_iota(jnp.int32, (N_EXPERT_GROUPS, tm), 0)
    gsel = jnp.zeros((N_EXPERT_GROUPS, tm), F32)
    for _ in range(TOPK_GROUPS):
        m = jnp.max(grp, axis=0, keepdims=True)
        first = jnp.min(jnp.where(grp == m, iota_n, N_EXPERT_GROUPS), axis=0, keepdims=True)
        hit = iota_n == first
        gsel = jnp.where(hit, 1.0, gsel)
        grp = jnp.where(hit, neg, grp)
    emask = jnp.concatenate([jnp.broadcast_to(gsel[gi:gi + 1], (per, tm)) for gi in range(N_EXPERT_GROUPS)], axis=0)
    cand = jnp.where(emask > 0.0, choice, neg)
    iota_e = lax.broadcasted_iota(jnp.int32, (N_EXPERTS, tm), 0)
    sel = jnp.zeros((N_EXPERTS, tm), F32)
    for _ in range(TOP_K):
        m = jnp.max(cand, axis=0, keepdims=True)
        first = jnp.min(jnp.where(cand == m, iota_e, N_EXPERTS), axis=0, keepdims=True)
        hit = iota_e == first
        sel = jnp.where(hit, 1.0, sel)
        cand = jnp.where(hit, neg, cand)
    w = scores * sel
    gate_ref[0] = w / jnp.sum(w, axis=0, keepdims=True) * ROUTED_SCALE


def _router(x, g, scale, shift, w_router, e_bias, tm=512):
    b, s, d = x.shape
    tm = min(tm, s)
    e = w_router.shape[1]
    row = pl.BlockSpec((1, tm, d), lambda i, j: (i, j, 0))
    mod = pl.BlockSpec((1, 1, d), lambda i, j: (i, 0, 0))
    return pl.pallas_call(
        _router_kernel,
        grid=(b, s // tm),
        in_specs=[row, _const_spec((1, d)), mod, mod, _const_spec((e, d)), _const_spec((e, 1))],
        out_specs=[row, pl.BlockSpec((1, e, tm), lambda i, j: (i, 0, j))],
        out_shape=[jax.ShapeDtypeStruct((b, s, d), BF16), jax.ShapeDtypeStruct((b, e, s), F32)],
        compiler_params=_params('parallel', 'parallel'),
        name='router',
    )(x, g.reshape(1, d), scale, shift, w_router.T, e_bias.reshape(e, 1))


def _moe_kernel(h_ref, x_ref, gate_ref, g2_ref, wg_ref, wu_ref, wd_ref, sg_ref, su_ref, sd_ref, *rest, final):
    if final:
        fg_ref, o_ref, acc_ref = rest
    else:
        o_ref, acc_ref = rest
    e = pl.program_id(2)
    h = h_ref[0]

    @pl.when(e == 0)
    def _():
        a = jnp.dot(h, sg_ref[...], preferred_element_type=F32)
        u = jnp.dot(h, su_ref[...], preferred_element_type=F32)
        acc_ref[...] = jnp.dot((_silu(a) * u).astype(BF16), sd_ref[...], preferred_element_type=F32)

    lane = lax.broadcasted_iota(jnp.int32, gate_ref.shape[1:], 1)
    gcol = jnp.sum(jnp.where(lane == e, gate_ref[0], 0.0), axis=-1, keepdims=True)
    a = jnp.dot(h, wg_ref[0].astype(BF16), preferred_element_type=F32)
    u = jnp.dot(h, wu_ref[0].astype(BF16), preferred_element_type=F32)
    acc_ref[...] += jnp.dot((_silu(a) * u * gcol).astype(BF16), wd_ref[0].astype(BF16), preferred_element_type=F32)

    @pl.when(e == pl.num_programs(2) - 1)
    def _():
        y = x_ref[0] + g2_ref[0] * acc_ref[...]
        if final:
            y = _rms(y) * fg_ref[...]
        o_ref[0] = y


def _moe(h2, x, gate, g2, layer, w_gate, w_up, w_down, s_gate, s_up, s_down, final_g=None, tm=1024):
    b, s, d = x.shape
    tm = min(tm, s)
    _, e, _, ff = w_gate.shape
    row = pl.BlockSpec((1, tm, d), lambda i, j, k: (i, j, 0))
    ins = [h2, x, gate, g2, w_gate, w_up, w_down, s_gate, s_up, s_down]
    in_specs = [row, row, pl.BlockSpec((1, tm, e), lambda i, j, k: (i, j, 0)),
                pl.BlockSpec((1, 1, d), lambda i, j, k: (i, 0, 0)),
                pl.BlockSpec((None, 1, d, ff), lambda i, j, k: (layer, k, 0, 0)),
                pl.BlockSpec((None, 1, d, ff), lambda i, j, k: (layer, k, 0, 0)),
                pl.BlockSpec((None, 1, ff, d), lambda i, j, k: (layer, k, 0, 0)),
                _const_spec(s_gate.shape), _const_spec(s_up.shape), _const_spec(s_down.shape)]
    if final_g is not None:
        ins.append(final_g.reshape(1, d))
        in_specs.append(_const_spec((1, d)))
    return pl.pallas_call(
        functools.partial(_moe_kernel, final=final_g is not None),
        grid=(b, s // tm, e),
        in_specs=in_specs,
        out_specs=row,
        out_shape=jax.ShapeDtypeStruct(x.shape, F32),
        scratch_shapes=[pltpu.VMEM((tm, d), F32)],
        compiler_params=_params('parallel', 'parallel', 'arbitrary'),
        name='moe',
    )(*ins)


def _mixers(p, pc, ctx_out, prm, l, lam_init, rope_tabs, lb_terms):
    s = p['hy_v'].shape[1]
    sc = pc['hy_v'].shape[1]

    hy_args = (prm['hy_w1'][l], prm['hy_b1'][l], prm['hy_w2'][l], prm['hy_b2'][l], prm['hy_w3'][l], prm['hy_b3'][l],
               prm['hy_sin_freq'][l], prm['hy_decay'][l])
    y_hy = _hyena([p['hy_v'], p['hy_x1'], p['hy_x2']], prm['hy_conv_w'][l], prm['hy_conv_b'][l],
                  _hy_filters(s, *hy_args), prm['hy_bias'][l], inner=128)
    yc_hy = None
    if ctx_out:
        yc_hy = _hyena([pc['hy_v'], pc['hy_x1'], pc['hy_x2']], prm['hy_conv_w'][l], prm['hy_conv_b'][l],
                       _hy_filters(sc, *hy_args), prm['hy_bias'][l], inner=32)

    lp = prm['da_lambda'][l].astype(F32)
    lam = jnp.exp(jnp.sum(lp[0] * lp[1])) - jnp.exp(jnp.sum(lp[2] * lp[3])) + lam_init
    da_kw = dict(heads=DA_HEADS, ncomp=2, scale=DA_HEAD_DIM ** -0.5, lam=lam, subln_g=prm['da_subln_g'][l],
                 post_scale=1.0 - lam_init)
    da_ctx = ([pc['da_k']], pc['da_v'])
    y_da = _attention([p['da_q']], [da_ctx, ([p['da_k']], p['da_v'])], **da_kw)
    yc_da = _attention([pc['da_q']], [da_ctx], **da_kw) if ctx_out else None

    wq = prm['mla_w_q_up'][l].reshape(MLA_Q_RANK, MLA_HEADS, MLA_NOPE_DIM + MLA_ROPE_DIM)
    wq_n = wq[:, :, :MLA_NOPE_DIM].reshape(MLA_Q_RANK, -1).astype(BF16)
    wq_r = wq[:, :, MLA_NOPE_DIM:].reshape(MLA_Q_RANK, -1).astype(BF16)
    wkv = prm['mla_w_kv_up'][l].reshape(MLA_KV_RANK, MLA_HEADS, MLA_NOPE_DIM + MLA_V_DIM)
    wkv_n = wkv[:, :, :MLA_NOPE_DIM].reshape(MLA_KV_RANK, -1).astype(BF16)
    wkv_v = wkv[:, :, MLA_NOPE_DIM:].reshape(MLA_KV_RANK, -1).astype(BF16)

    def queries(qd, tabs):
        return _norm_proj(qd, prm['mla_q_norm_g'][l], [(wq_n, F32, False, MLA_HEADS), (wq_r, F32, True, MLA_HEADS)],
                          rope_tabs=tabs)

    def keys_values(kvd):
        return _norm_proj(kvd, prm['mla_kv_norm_g'][l], [(wkv_n, BF16, False, MLA_HEADS), (wkv_v, BF16, False, MLA_HEADS)])

    kn_l, v_l = keys_values(p['mla_kv'])
    kn_c, v_c = keys_values(pc['mla_kv'])
    mla_kw = dict(heads=MLA_HEADS, ncomp=1, scale=(MLA_NOPE_DIM + MLA_ROPE_DIM) ** -0.5)
    mla_ctx = ([kn_c, pc['mla_kr']], v_c)
    y_mla = _attention(queries(p['mla_q'], rope_tabs), [mla_ctx, ([kn_l, p['mla_kr']], v_l)], **mla_kw)
    yc_mla = _attention(queries(pc['mla_q'], None), [mla_ctx], **mla_kw) if ctx_out else None

    o, oc = _hgrn(p['hg_q'], p['hg_ff'], p['hg_fb'], p['hg_i'], pc['hg_q'], pc['hg_ff'], pc['hg_fb'], pc['hg_i'],
                  lb_terms, prm['hg_norm_g'][l])
    return (y_hy, y_da, y_mla, o), (yc_hy, yc_da, yc_mla, oc)


def kernel(x, c, ctx, c_ctx, w_ada, b_ada, norm1_g, norm2_g, w_in, w_out, hy_conv_w, hy_conv_b, hy_w1, hy_b1, hy_w2, hy_b2, hy_w3, hy_b3, hy_sin_freq, hy_decay, hy_bias, da_lambda, da_subln_g, mla_q_norm_g, mla_w_q_up, mla_kv_norm_g, mla_w_kv_up, hg_lower_bounds, hg_norm_g, moe_w_router, moe_bias, moe_w_gate, moe_w_up, moe_w_down, moe_sh_gate, moe_sh_up, moe_sh_down, final_norm_g):
    prm = dict(hy_conv_w=hy_conv_w, hy_conv_b=hy_conv_b, hy_w1=hy_w1, hy_b1=hy_b1, hy_w2=hy_w2, hy_b2=hy_b2,
               hy_w3=hy_w3, hy_b3=hy_b3, hy_sin_freq=hy_sin_freq, hy_decay=hy_decay, hy_bias=hy_bias,
               da_lambda=da_lambda, da_subln_g=da_subln_g, mla_q_norm_g=mla_q_norm_g, mla_w_q_up=mla_w_q_up,
               mla_kv_norm_g=mla_kv_norm_g, mla_w_kv_up=mla_w_kv_up, hg_norm_g=hg_norm_g)
    b, n_lat, d = x.shape
    depth = w_in.shape[0]
    rows = n_lat // GRID_W
    row_pos = jnp.repeat(jnp.arange(rows, dtype=jnp.int32), GRID_W)
    col_pos = jnp.tile(jnp.arange(GRID_W, dtype=jnp.int32), rows)
    rope_tabs = _rope_tables(row_pos, col_pos, 2 * DA_HEADS * DA_HEAD_DIM)
    lbs = jnp.cumsum(jax.nn.softmax(hg_lower_bounds.astype(F32), axis=1), axis=1)
    lbs = lbs - lbs[:, :1]
    cond = jnp.concatenate([c, c_ctx[None], jnp.zeros((8 - b - 1, d), F32)], axis=0)

    for l in range(depth):
        ctx_out = l < depth - 1
        mods = _ada(cond, w_ada[l], b_ada[l])
        sh1, sc1, g1, sh2, sc2, g2 = [m[:, None, :] for m in jnp.split(mods[:b], 6, axis=-1)]
        mc = [jnp.broadcast_to(m[:, None, :], (b, 1, d)) for m in jnp.split(mods[b:b + 1], 6, axis=-1)]

        off = 0
        outs = []
        for _, wdt, dt, rope, split, rep in _SEGMENTS:
            w = w_in[l][:, off:off + wdt].astype(BF16)
            outs.append((jnp.tile(w, (1, rep)) if rep > 1 else w, dt, rope, split))
            off += wdt
        names = [seg[0] for seg in _SEGMENTS]
        p = dict(zip(names, _norm_proj(x, norm1_g[l], outs, sc1, sh1, rope_tabs=rope_tabs)))
        pc = dict(zip(names, _norm_proj(ctx, norm1_g[l], outs, mc[1], mc[0])))

        lb = lbs[:, l]
        lb_terms = jnp.stack([jnp.log(lb), jnp.log1p(-lb), 1.0 - lb], axis=1)
        lam_init = 0.8 - 0.6 * math.exp(-0.3 * l)
        lat_parts, ctx_parts = _mixers(p, pc, ctx_out, prm, l, lam_init, rope_tabs, lb_terms)

        w_out_b = w_out[l].astype(BF16)
        moe_w = (l, moe_w_gate, moe_w_up, moe_w_down,
                 moe_sh_gate[l].astype(BF16), moe_sh_up[l].astype(BF16), moe_sh_down[l].astype(BF16))

        if ctx_out:
            ctx = _out_proj(*ctx_parts, pc['hg_g'], ctx, mc[2], w_out_b)
            h2c, gate_c = _router(ctx, norm2_g[l], mc[4], mc[3], moe_w_router[l], moe_bias[l])
            ctx = _moe(h2c, ctx, gate_c.transpose(0, 2, 1), mc[5], *moe_w)

        x = _out_proj(*lat_parts, p['hg_g'], x, g1, w_out_b)
        h2, gate = _router(x, norm2_g[l], sc2, sh2, moe_w_router[l], moe_bias[l])
        x = _moe(h2, x, gate.transpose(0, 2, 1), g2, *moe_w, final_g=None if ctx_out else final_norm_g)

    return x
```

```python
import functools
import math

import numpy as np
import jax
import jax.numpy as jnp
from jax import lax
from jax.experimental import pallas as pl
from jax.experimental.pallas import tpu as pltpu
from jax.experimental.pallas import tpu_sc as plsc

F32 = jnp.float32
BF16 = jnp.bfloat16
HIGHEST = lax.Precision.HIGHEST

D_MODEL = 1024
GRID_W = 64
HY_WIDTH = 256
HY_ORDER = 2
HY_BANDS = 16
DA_HEADS = 4
DA_HEAD_DIM = 32
MLA_HEADS = 4
MLA_Q_RANK = 192
MLA_KV_RANK = 128
MLA_NOPE_DIM = 64
MLA_ROPE_DIM = 32
MLA_V_DIM = 64
HG_HEADS = 4
HG_KEY_DIM = 64
HG_VAL_DIM = 64
HG_CHUNK = 64
HG_SUB = 8
N_EXPERTS = 64
N_EXPERT_GROUPS = 8
TOPK_GROUPS = 4
TOP_K = 8
EXPERT_FF = 256
ROUTED_SCALE = 2.5
ROPE_BASE = 10000.0
NORM_EPS = 1e-6

V7X_VMEM_LIMIT_BYTES = 56 * 1024 * 1024
LANES = 128

_SEGMENTS = (
    ('hy_v', HY_WIDTH, F32, False, 0, 1), ('hy_x1', HY_WIDTH, F32, False, 0, 1), ('hy_x2', HY_WIDTH, F32, False, 0, 1),
    ('da_q', 2 * DA_HEADS * DA_HEAD_DIM, F32, True, 2 * DA_HEADS, 1),
    ('da_k', 2 * DA_HEADS * DA_HEAD_DIM, BF16, True, 2 * DA_HEADS, 1),
    ('da_v', 2 * DA_HEADS * DA_HEAD_DIM, BF16, False, DA_HEADS, 1),
    ('mla_q', MLA_Q_RANK, F32, False, 0, 1), ('mla_kv', MLA_KV_RANK, F32, False, 0, 1),
    ('mla_kr', MLA_ROPE_DIM, BF16, True, MLA_HEADS, MLA_HEADS),
    ('hg_q', HG_HEADS * HG_KEY_DIM, F32, False, 0, 1), ('hg_ff', HG_HEADS * HG_KEY_DIM, F32, False, 0, 1),
    ('hg_fb', HG_HEADS * HG_KEY_DIM, F32, False, 0, 1), ('hg_i', HG_HEADS * HG_VAL_DIM, F32, False, 0, 1),
    ('hg_g', HG_HEADS * HG_VAL_DIM, F32, False, 0, 1),
)


def _params(*semantics):
    return pltpu.CompilerParams(dimension_semantics=semantics, vmem_limit_bytes=V7X_VMEM_LIMIT_BYTES)


def _const_spec(shape):
    nd = len(shape)
    return pl.BlockSpec(shape, lambda *_: (0,) * nd)


def _rms(x, eps=NORM_EPS):
    return x * lax.rsqrt(jnp.mean(x * x, axis=-1, keepdims=True) + eps)


def _silu(x):
    return x * jax.nn.sigmoid(x)


def _dot_nt(a, b, **kw):
    return lax.dot_general(a, b, (((1,), (1,)), ((), ())), preferred_element_type=F32, **kw)


def _ada_kernel(c_ref, w_ref, b_ref, o_ref):
    s = _silu(c_ref[...])
    o_ref[...] = jnp.dot(s, w_ref[...], precision=HIGHEST, preferred_element_type=F32) + b_ref[...]


def _ada(cond, w, b):
    r, d = cond.shape
    n = w.shape[1]
    tn = 1536
    return pl.pallas_call(
        _ada_kernel,
        grid=(n // tn,),
        in_specs=[_const_spec((r, d)), pl.BlockSpec((d, tn), lambda j: (0, j)), pl.BlockSpec((1, tn), lambda j: (0, j))],
        out_specs=pl.BlockSpec((r, tn), lambda j: (0, j)),
        out_shape=jax.ShapeDtypeStruct((r, n), F32),
        compiler_params=_params('arbitrary'),
        name='ada',
    )(cond, w, b.reshape(1, n))


ROPE_UNIT = 32


def _rope_tables(row, col, width):
    n = ROPE_UNIT // 4
    inv = ROPE_BASE ** (-jnp.arange(n, dtype=F32) / n)
    units = width // ROPE_UNIT
    parts_c, parts_a, parts_b = [], [], []
    zero = jnp.zeros((row.shape[0], n), F32)
    for pos in (row, col):
        ang = pos.astype(F32)[:, None] * inv
        cos, sin = jnp.cos(ang), jnp.sin(ang)
        parts_c += [cos, cos]
        parts_a += [zero, sin]
        parts_b += [-sin, zero]
    tile = lambda ps: jnp.tile(jnp.concatenate(ps, axis=1), (1, units))
    return tile(parts_c), tile(parts_a), tile(parts_b)


def _norm_proj_kernel(*refs, n_w, modulate, ropes, splits):
    x_ref, g_ref = refs[0], refs[1]
    pos = 2
    if modulate:
        sc_ref, sh_ref = refs[2], refs[3]
        pos = 4
    if any(ropes):
        rc_ref, ra_ref, rb_ref = refs[pos:pos + 3]
        pos += 3
    w_refs = refs[pos:pos + n_w]
    o_refs = refs[pos + n_w:]
    y = _rms(x_ref[0]) * g_ref[...]
    if modulate:
        y = y * (1.0 + sc_ref[0]) + sh_ref[0]
    yb = y.astype(BF16)
    for w_ref, o_ref, rope, split in zip(w_refs, o_refs, ropes, splits):
        o = jnp.dot(yb, w_ref[...], preferred_element_type=F32)
        if rope:
            wd = o.shape[1]
            shift = ROPE_UNIT // 4
            o = (o * rc_ref[:, :wd] + pltpu.roll(o, shift, axis=1) * ra_ref[:, :wd]
                 + pltpu.roll(o, wd - shift, axis=1) * rb_ref[:, :wd])
        if split:
            unit = o.shape[1] // split
            for u in range(split):
                o_ref[0, u] = o[:, u * unit:(u + 1) * unit].astype(o_ref.dtype)
        else:
            o_ref[0] = o.astype(o_ref.dtype)


def _norm_proj(x, g, outs, scale=None, shift=None, rope_tabs=None, tm=512):
    b, s, k = x.shape
    tm = min(tm, s)
    modulate = scale is not None
    ropes = tuple(bool(o[2]) and rope_tabs is not None for o in outs)
    splits = tuple(o[3] for o in outs)
    ins = [x, g.reshape(1, k)]
    in_specs = [pl.BlockSpec((1, tm, k), lambda i, j: (i, j, 0)), _const_spec((1, k))]
    if modulate:
        ins += [scale, shift]
        in_specs += [pl.BlockSpec((1, 1, k), lambda i, j: (i, 0, 0))] * 2
    if any(ropes):
        ins += list(rope_tabs)
        in_specs += [pl.BlockSpec((tm, rope_tabs[0].shape[1]), lambda i, j: (j, 0))] * 3
    out_specs, out_shape = [], []
    for w, dt, _, split in outs:
        ins.append(w)
        in_specs.append(_const_spec(w.shape))
        n = w.shape[1]
        if split:
            out_specs.append(pl.BlockSpec((1, split, tm, n // split), lambda i, j: (i, 0, j, 0)))
            out_shape.append(jax.ShapeDtypeStruct((b, split, s, n // split), dt))
        else:
            out_specs.append(pl.BlockSpec((1, tm, n), lambda i, j: (i, j, 0)))
            out_shape.append(jax.ShapeDtypeStruct((b, s, n), dt))
    return pl.pallas_call(
        functools.partial(_norm_proj_kernel, n_w=len(outs), modulate=modulate, ropes=ropes, splits=splits),
        grid=(b, s // tm),
        in_specs=in_specs,
        out_specs=out_specs,
        out_shape=out_shape,
        compiler_params=_params('parallel', 'parallel'),
        name='norm_proj',
    )(*ins)


def _hy_filter_kernel(w1t_ref, w1s_ref, w1c_ref, b1_ref, w2_ref, b2_ref, w3_ref, b3_ref, fr_ref, dec_ref, o_ref, *, n):
    t = lax.broadcasted_iota(jnp.int32, (n, 1), 0).astype(F32) / n
    bands = lax.broadcasted_iota(jnp.int32, (1, HY_BANDS), 1).astype(F32) + 1.0
    ang = (2.0 * jnp.pi) * t * bands
    pre = (t * w1t_ref[...]
           + jnp.dot(jnp.sin(ang), w1s_ref[...], precision=HIGHEST, preferred_element_type=F32)
           + jnp.dot(jnp.cos(ang), w1c_ref[...], precision=HIGHEST, preferred_element_type=F32)
           + b1_ref[...])
    hid = jnp.sin(fr_ref[0:1, :] * pre)
    hid = jnp.sin(fr_ref[1:2, :] * (jnp.dot(hid, w2_ref[...], precision=HIGHEST, preferred_element_type=F32) + b2_ref[...]))
    filt = jnp.dot(hid, w3_ref[...], precision=HIGHEST, preferred_element_type=F32) + b3_ref[...]
    filt = filt * jnp.exp(-t * jnp.abs(dec_ref[...]))
    col = jnp.sum(jnp.abs(filt), axis=0, keepdims=True) - jnp.abs(filt[0:1, :])
    w = HY_WIDTH
    for o in range(HY_ORDER):
        lo = o * 2 * w
        f0 = filt[0:1, lo:lo + w] + filt[0:1, lo + w:lo + 2 * w]
        inv = 1.0 / (col[:, lo:lo + w] + col[:, lo + w:lo + 2 * w] + jnp.abs(f0))
        o_ref[:, lo:lo + w] = filt[:, lo:lo + w] * inv
        o_ref[:, lo + w:lo + 2 * w] = filt[:, lo + w:lo + 2 * w] * inv


def _hy_filters(n, w1, b1, w2, b2, w3, b3, freq, decay):
    cols = w3.shape[1]
    ins = [w1[0:1], w1[1:1 + HY_BANDS], w1[1 + HY_BANDS:], b1.reshape(1, -1), w2, b2.reshape(1, -1), w3,
           b3.reshape(1, -1), freq, decay.reshape(1, -1)]
    out = pl.pallas_call(
        functools.partial(_hy_filter_kernel, n=n),
        grid=(1,),
        in_specs=[_const_spec(a.shape) for a in ins],
        out_specs=_const_spec((n, cols)),
        out_shape=jax.ShapeDtypeStruct((n, cols), F32),
        compiler_params=_params('arbitrary'),
        name='hy_filter',
    )(*ins)
    return out.reshape(n, HY_ORDER, 2, HY_WIDTH)


def _two_sided(filt_n):
    n = filt_n.shape[0]
    hf, hb = filt_n[:, :, 0], filt_n[:, :, 1]
    h = jnp.concatenate([hf[:1] + hb[:1], hf[1:], jnp.zeros((1,) + hf.shape[1:], F32), hb[:0:-1]], axis=0)
    return h.reshape(2 * n, HY_ORDER * HY_WIDTH)


def _short_conv_kernel(*refs, s):
    x_refs, w_refs, b_refs, o_refs = refs[0:3], refs[3:6], refs[6:9], refs[9:12]
    row = lax.broadcasted_iota(jnp.int32, (s, 1), 0)
    for x_ref, w_ref, b_ref, o_ref in zip(x_refs, w_refs, b_refs, o_refs):
        x = x_ref[0]
        prev = jnp.where(row == 0, 0.0, pltpu.roll(x, 1, axis=0))
        nxt = jnp.where(row == s - 1, 0.0, pltpu.roll(x, s - 1, axis=0))
        o_ref[0] = prev * w_ref[0:1, :] + x * w_ref[1:2, :] + nxt * w_ref[2:3, :] + b_ref[...]


def _short_conv(parts, conv_w, conv_b):
    b, s, c = parts[0].shape
    tc = LANES
    ws = [conv_w[:, i * c:(i + 1) * c] for i in range(3)]
    bs = [conv_b[i * c:(i + 1) * c].reshape(1, c) for i in range(3)]
    xspec = pl.BlockSpec((1, s, tc), lambda i, j: (i, 0, j))
    return pl.pallas_call(
        functools.partial(_short_conv_kernel, s=s),
        grid=(b, c // tc),
        in_specs=[xspec] * 3 + [pl.BlockSpec((3, tc), lambda i, j: (0, j))] * 3 + [pl.BlockSpec((1, tc), lambda i, j: (0, j))] * 3,
        out_specs=[xspec] * 3,
        out_shape=[jax.ShapeDtypeStruct((b, s, c), F32)] * 3,
        compiler_params=_params('parallel', 'parallel'),
        name='short_conv',
    )(*parts, *ws, *bs)


def _dft_cos_sin(rows, cols, period):
    ang = 2.0 * np.pi * ((np.arange(rows)[:, None] * np.arange(cols)[None, :]) % period) / period
    return np.cos(ang), np.sin(ang)


def _fft_tables(n, inner):
    big = 2 * n
    n1 = big // inner
    c1, s1 = _dft_cos_sin(n1, n1, n1)
    h = n1 // 2
    outer_data = np.block([[c1[:, :h], s1[:, :h]], [-s1[:, :h], c1[:, :h]]])
    outer_real = np.concatenate([c1, -s1], axis=0)
    outer_inv = np.block([[c1[:h, :], -s1[:h, :]], [s1[:h, :], c1[:h, :]]]) / big
    c2, s2 = _dft_cos_sin(inner, inner, inner)
    inner_fwd = np.block([[c2, s2], [-s2, c2]])
    inner_inv = np.block([[c2, -s2], [s2, c2]])
    ct, st = _dft_cos_sin(n1, inner, big)
    f = lambda a: jnp.asarray(a, F32)
    return dict(n1=n1, inner=inner, outer_data=f(outer_data), outer_real=f(outer_real), outer_inv=f(outer_inv),
                inner_fwd=_hi_lo_cols(inner_fwd), inner_inv=_hi_lo_cols(inner_inv),
                tw_cos=f(ct).reshape(n1, inner, 1), tw_sin=f(st).reshape(n1, inner, 1))


def _left_mm_kernel(m_ref, x_ref, o_ref):
    o_ref[0] = jnp.dot(m_ref[...], x_ref[0], precision=HIGHEST, preferred_element_type=F32)


def _left_mm(m, x, tl=4096):
    p, k, l = x.shape
    mm = m.shape[0]
    tl = min(tl, l)
    return pl.pallas_call(
        _left_mm_kernel,
        grid=(p, l // tl),
        in_specs=[_const_spec(m.shape), pl.BlockSpec((1, k, tl), lambda i, j: (i, 0, j))],
        out_specs=pl.BlockSpec((1, mm, tl), lambda i, j: (i, 0, j)),
        out_shape=jax.ShapeDtypeStruct((p, mm, l), F32),
        compiler_params=_params('parallel', 'parallel'),
        name='fft_outer',
    )(m, x)


def _hi_lo_cols(m):
    m = np.asarray(m, np.float32)
    hi = m.astype(BF16)
    lo = (m - hi.astype(np.float32)).astype(BF16)
    return jnp.asarray(np.concatenate([hi, hi, lo], axis=1))


def _hi_lo_rows(x):
    hi = x.astype(BF16)
    lo = (x - hi.astype(F32)).astype(BF16)
    return jnp.concatenate([hi, lo, hi], axis=0)


def _inner_kernel(a_ref, twc_ref, tws_ref, gf_ref, *rest, convolve, inner, kb):
    for s in range(kb):
        ar, ai = a_ref[0, 0, s], a_ref[0, 1, s]
        tc, ts = twc_ref[s], tws_ref[s]
        br = ar * tc + ai * ts
        bi = ai * tc - ar * ts
        x = jnp.dot(gf_ref[...], _hi_lo_rows(jnp.concatenate([br, bi], axis=0)), preferred_element_type=F32)
        if not convolve:
            o_ref = rest[0]
            o_ref[0, 0, s] = x[:inner]
            o_ref[0, 1, s] = x[inner:]
            continue
        h_ref, gi_ref, o_ref = rest
        xr, xi = x[:inner], x[inner:]
        hr, hi = h_ref[0, 0, s], h_ref[0, 1, s]
        yr = xr * hr - xi * hi
        yi = xr * hi + xi * hr
        z = jnp.dot(gi_ref[...], _hi_lo_rows(jnp.concatenate([yr, yi], axis=0)), preferred_element_type=F32)
        zr, zi = z[:inner], z[inner:]
        o_ref[0, 0, s] = zr * tc - zi * ts
        o_ref[0, 1, s] = zi * tc + zr * ts


def _fft_inner(a, tab, c, h=None, h_block=0):
    p = a.shape[0]
    n1, inner = tab['n1'], tab['inner']
    a5 = a.reshape(p, 2, n1, inner, c)
    tc = 2 * LANES
    kb = 4
    blk = pl.BlockSpec((1, 2, kb, inner, tc), lambda i, k, j: (i, 0, k, 0, j))
    tw_spec = pl.BlockSpec((kb, inner, 1), lambda i, k, j: (k, 0, 0))
    ins = [a5, tab['tw_cos'], tab['tw_sin'], tab['inner_fwd']]
    in_specs = [blk, tw_spec, tw_spec, _const_spec(tab['inner_fwd'].shape)]
    if h is not None:
        ch = h.shape[-1] // inner
        nb = c // tc
        ins += [h.reshape(1, 2, n1, inner, ch), tab['inner_inv']]
        in_specs += [pl.BlockSpec((1, 2, kb, inner, tc), lambda i, k, j: (0, 0, k, 0, h_block * nb + j)),
                     _const_spec(tab['inner_inv'].shape)]
    out = pl.pallas_call(
        functools.partial(_inner_kernel, convolve=h is not None, inner=inner, kb=kb),
        grid=(p, n1 // kb, c // tc),
        in_specs=in_specs,
        out_specs=blk,
        out_shape=jax.ShapeDtypeStruct(a5.shape, F32),
        compiler_params=_params('parallel', 'parallel', 'parallel'),
        name='fft_inner',
    )(*ins)
    return out.reshape(p, 2 * n1, inner * c)


def _gate_kernel(m_ref, z_ref, u_ref, x_ref, bias_ref, *rest, chain):
    y = jnp.dot(m_ref[...], z_ref[0], precision=HIGHEST, preferred_element_type=F32)
    nxt = x_ref[0] * (y + u_ref[0] * bias_ref[...])
    if chain:
        mf_ref, o_ref, a_ref = rest
        o_ref[0] = nxt
        a_ref[0] = jnp.dot(mf_ref[...], nxt, precision=HIGHEST, preferred_element_type=F32)
    else:
        rest[0][0] = nxt


def _fft_gate(tab, z, u, x, bias_l, chain, tl=4096):
    p, k2, l = z.shape
    n1 = tab['n1']
    tl = min(tl, l)
    row = pl.BlockSpec((1, n1, tl), lambda i, j: (i, 0, j))
    ins = [tab['outer_inv'], z, u, x, bias_l]
    in_specs = [_const_spec((n1, k2)), pl.BlockSpec((1, k2, tl), lambda i, j: (i, 0, j)), row, row,
                pl.BlockSpec((1, tl), lambda i, j: (0, j))]
    out_specs = [row]
    out_shape = [jax.ShapeDtypeStruct((p, n1, l), F32)]
    if chain:
        ins.append(tab['outer_data'])
        in_specs.append(_const_spec((k2, n1)))
        out_specs.append(pl.BlockSpec((1, k2, tl), lambda i, j: (i, 0, j)))
        out_shape.append(jax.ShapeDtypeStruct((p, k2, l), F32))
    return pl.pallas_call(
        functools.partial(_gate_kernel, chain=chain),
        grid=(p, l // tl),
        in_specs=in_specs,
        out_specs=out_specs,
        out_shape=out_shape,
        compiler_params=_params('parallel', 'parallel'),
        name='fft_gate',
    )(*ins)


def _hyena(parts, conv_w, conv_b, filt_n, bias, inner):
    b, s, c = parts[0].shape
    tab = _fft_tables(s, inner)
    n1 = tab['n1']
    lanes = inner * c
    h_taps = _two_sided(filt_n).reshape(1, n1, inner * HY_ORDER * c)
    h_spec = _fft_inner(_left_mm(tab['outer_real'], h_taps), tab, HY_ORDER * c)
    v, x1, x2 = [a.reshape(b // 2, n1, lanes) for a in _short_conv(parts, conv_w, conv_b)]
    bias_l = [jnp.tile(bias[o], inner).reshape(1, lanes) for o in range(HY_ORDER)]
    a = _left_mm(tab['outer_data'], v)
    z = _fft_inner(a, tab, c, h_spec, 0)
    z2, a = _fft_gate(tab, z, v, x1, bias_l[0], chain=True)
    z = _fft_inner(a, tab, c, h_spec, 1)
    (z3,) = _fft_gate(tab, z, z2, x2, bias_l[1], chain=False)
    return z3.reshape(b, s, c)


def _attn_kernel(*refs, n_q, n_pieces, ncomp, scale, post_scale):
    q_refs = refs[:n_q]
    pos = n_q
    pieces = []
    for _ in range(n_pieces):
        pieces.append((refs[pos:pos + n_q], refs[pos + n_q]))
        pos += n_q + 1
    if ncomp == 2:
        lam_ref, g_ref = refs[pos:pos + 2]
        pos += 2
    o_ref = refs[pos]
    outs = []
    for c in range(ncomp):
        qs = [(q_ref[0, c] * (scale * math.log2(math.e))).astype(BF16) for q_ref in q_refs]
        scores = []
        for k_refs, _ in pieces:
            s = None
            for q, k_ref in zip(qs, k_refs):
                t = _dot_nt(q, k_ref[0, c if k_ref.shape[1] == ncomp else 0])
                s = t if s is None else s + t
            scores.append(s)
        m = None
        for s in scores:
            mp = jnp.max(s, axis=-1, keepdims=True)
            m = mp if m is None else jnp.maximum(m, mp)
        l, o = None, None
        for s, (_, v_ref) in zip(scores, pieces):
            p = jnp.exp2(s - m)
            lp = jnp.sum(p, axis=-1, keepdims=True)
            op = jnp.dot(p.astype(BF16), v_ref[0, 0], preferred_element_type=F32)
            l = lp if l is None else l + lp
            o = op if o is None else o + op
        outs.append(o / l)
    if ncomp == 2:
        o = outs[0] - lam_ref[0] * outs[1]
        o = _rms(o) * g_ref[...] * post_scale
    else:
        o = outs[0]
    o_ref[0, 0] = o


def _attention(q_parts, pieces, heads, ncomp, scale, tq=256, lam=None, subln_g=None, post_scale=1.0):
    b, _, sq, _ = q_parts[0].shape
    dv = pieces[0][1].shape[3]
    tq = min(tq, sq)
    ins = list(q_parts)
    in_specs = [pl.BlockSpec((1, ncomp, tq, q.shape[3]), lambda i, h, j: (i, h, j, 0)) for q in q_parts]
    for k_parts, v in pieces:
        for k in k_parts:
            ins.append(k)
            if k.shape[1] == 1:
                in_specs.append(pl.BlockSpec((1, 1) + k.shape[2:], lambda i, h, j: (i, 0, 0, 0)))
            else:
                in_specs.append(pl.BlockSpec((1, ncomp) + k.shape[2:], lambda i, h, j: (i, h, 0, 0)))
        ins.append(v)
        in_specs.append(pl.BlockSpec((1, 1) + v.shape[2:], lambda i, h, j: (i, h, 0, 0)))
    if ncomp == 2:
        ins += [lam.reshape(1), subln_g.reshape(1, dv)]
        in_specs += [pl.BlockSpec(memory_space=pltpu.SMEM), _const_spec((1, dv))]
    return pl.pallas_call(
        functools.partial(_attn_kernel, n_q=len(q_parts), n_pieces=len(pieces), ncomp=ncomp, scale=scale,
                          post_scale=post_scale),
        grid=(b, heads, sq // tq),
        in_specs=in_specs,
        out_specs=pl.BlockSpec((1, 1, tq, dv), lambda i, h, j: (i, h, j, 0)),
        out_shape=jax.ShapeDtypeStruct((b, heads, sq, dv), F32),
        compiler_params=_params('parallel', 'parallel', 'parallel'),
        name='attention',
    )(*ins)


def _forget_terms(f, log_lb, log_1m_lb, one_m_lb):
    log_sig = jnp.minimum(f, 0.0) - jnp.log1p(jnp.exp(-jnp.abs(f)))
    b = log_1m_lb + log_sig
    log_g = jnp.maximum(log_lb, b) + jnp.log1p(jnp.exp(-jnp.abs(log_lb - b)))
    return log_g, one_m_lb * jax.nn.sigmoid(-f)


def _hg_tables():
    ck, sub = HG_CHUNK, HG_SUB
    t = np.arange(ck)
    cum_mats, half_masks, group_masks, sels = [], [], [], []
    for rev in (False, True):
        mats = [(t[None, :] >= t[:, None]) if rev else (t[None, :] <= t[:, None])]
        halves = []
        hs = ck // 2
        while hs >= sub:
            pos = t % (2 * hs)
            b = t - pos + hs
            mats.append((t[None, :] >= b[:, None]) if rev else (t[None, :] < b[:, None]))
            q_half = (pos < hs) if rev else (pos >= hs)
            halves.append(np.stack([q_half, ~q_half]))
            if not rev:
                grp = (t[:, None] // (2 * hs)) == (t[None, :] // (2 * hs))
                group_masks.append(np.concatenate([grp, grp], axis=0))
            hs //= 2
        cum_mats.append(np.concatenate(mats, axis=0))
        half_masks.append(np.stack(halves))
        r, c = np.arange(ck)[:, None], np.arange(ck * sub)[None, :]
        same = (c // sub) == r
        tt, ss = (c // sub) % sub, c % sub
        sels.append(same & ((ss >= tt) if rev else (ss <= tt)))
    lanes = 2 * HG_KEY_DIM
    ln = np.arange(lanes)
    bd = (ln[:, None] // HG_KEY_DIM) == (ln[None, :] // HG_KEY_DIM)
    hm = np.broadcast_to(np.stack(half_masks)[..., None], (2, len(half_masks[0]), 2, ck, lanes))
    return (jnp.asarray(np.stack(cum_mats), BF16), jnp.asarray(hm, F32), jnp.asarray(np.stack(group_masks), F32),
            jnp.asarray(np.stack(sels), BF16), jnp.asarray(bd, F32))


def _split3(x):
    a = x.astype(BF16)
    r = x - a.astype(F32)
    b = r.astype(BF16)
    return a, b, (r - b.astype(F32)).astype(BF16)


def _hg_chunk(q, k, v, lg, st, rev, cm, hm, gm, sel, bd, m0, m1):
    ck, sub = HG_CHUNK, HG_SUB
    call = sum(jnp.dot(cm, piece, preferred_element_type=F32) for piece in _split3(lg))
    cum = call[0:ck]
    tot = cum[0:1] if rev else cum[ck - 1:ck]
    o = _dot_nt((q * jnp.exp(cum)).astype(BF16), st.astype(BF16))
    kd = (k * jnp.exp(tot - cum)).astype(BF16)
    st_new = st * jnp.exp(tot) + bd * jnp.dot(v.T.astype(BF16), kd, preferred_element_type=F32)
    s2 = None
    for lv in range(gm.shape[0]):
        cb = call[(lv + 1) * ck:(lv + 2) * ck]
        qd = q * jnp.exp(jnp.minimum(cum - cb, 0.0)) * hm[lv, 0]
        kf = (k * jnp.exp(jnp.minimum(cb - cum, 0.0)) * hm[lv, 1]).astype(BF16)
        q2 = jnp.concatenate([qd * m0, qd * m1], axis=0).astype(BF16)
        term = _dot_nt(q2, kf) * gm[lv]
        s2 = term if s2 is None else s2 + term
    r = jnp.dot(s2.astype(BF16), v.astype(BF16), preferred_element_type=F32)
    o = o + m0 * r[:ck] + m1 * r[ck:]
    rows, vts = [], []
    for i in range(ck // sub):
        lo, hi = i * sub, (i + 1) * sub
        ki, ci = k[lo:hi], cum[lo:hi]
        for t in range(lo, hi):
            rows.append((q[t:t + 1] * ki * jnp.exp(jnp.minimum(cum[t:t + 1] - ci, 0.0))).astype(BF16))
            vts.append(v[lo:hi])
    sc = jnp.dot(jnp.concatenate(rows, axis=0), bd.astype(BF16), preferred_element_type=F32)
    o = o + jnp.dot(sel, (sc * jnp.concatenate(vts, axis=0)).astype(BF16), preferred_element_type=F32)
    return o, st_new


def _hgrn_kernel(q_ref, ff_ref, fb_ref, i_ref, qc_ref, ffc_ref, fbc_ref, ic_ref, lb_ref, g_ref,
                 cm_ref, hm_ref, gm_ref, sel_ref, bd_ref, o_ref, oc_ref, or_ref, ocr_ref, st_ref, *, n_lat, n_ctx):
    ck = HG_CHUNK
    lanes = o_ref.shape[-1]
    lane = lax.broadcasted_iota(jnp.int32, (1, lanes), 1)
    m0 = (lane < HG_KEY_DIM).astype(F32)
    m1 = 1.0 - m0
    bd = bd_ref[...]
    gm = gm_ref[...]

    def one(q, f, v, rev):
        d = 1 if rev else 0
        lg, k = _forget_terms(f, lb_ref[d, 0:1, :], lb_ref[d, 1:2, :], lb_ref[d, 2:3, :])
        o, st = _hg_chunk(q, k, v, lg, st_ref[d], rev, cm_ref[d], hm_ref[d], gm, sel_ref[d], bd, m0, m1)
        st_ref[d] = st
        return o

    def sweep(qr, ffr, fbr, ir, out_f, out_r, n):
        nc = n // ck

        def body(step, carry):
            idf = pl.ds(pl.multiple_of(step * ck, ck), ck)
            idr = pl.ds(pl.multiple_of((nc - 1 - step) * ck, ck), ck)
            out_f[0, idf, :] = one(qr[0, idf, :], ffr[0, idf, :], ir[0, idf, :], False)
            out_r[idr, :] = one(qr[0, idr, :], fbr[0, idr, :], ir[0, idr, :], True)
            return carry

        lax.fori_loop(0, nc, body, 0)

    st_ref[...] = jnp.zeros(st_ref.shape, F32)
    sweep(qc_ref, ffc_ref, fbc_ref, ic_ref, oc_ref, ocr_ref, n_ctx)
    sweep(q_ref, ff_ref, fb_ref, i_ref, o_ref, or_ref, n_lat)

    mean_mat = bd * (1.0 / HG_VAL_DIM)

    def readout(out, out_r, n):
        tile = min(n, 512)

        def body(step, carry):
            idx = pl.ds(pl.multiple_of(step * tile, tile), tile)
            x = out[0, idx, :] + out_r[idx, :]
            ms = jnp.dot(x * x, mean_mat, precision=HIGHEST, preferred_element_type=F32)
            out[0, idx, :] = x * lax.rsqrt(ms + NORM_EPS) * g_ref[...]
            return carry

        lax.fori_loop(0, n // tile, body, 0)

    readout(oc_ref, ocr_ref, n_ctx)
    readout(o_ref, or_ref, n_lat)


def _hgrn(q, ff, fb, iv, qc, ffc, fbc, ic, lb_terms, norm_g):
    b, n_lat, width = q.shape
    n_ctx = qc.shape[1]
    lanes = 2 * HG_KEY_DIM
    tables = _hg_tables()
    lat = pl.BlockSpec((1, n_lat, lanes), lambda i, j: (i, 0, j))
    ctx = pl.BlockSpec((1, n_ctx, lanes), lambda i, j: (i, 0, j))
    g2 = jnp.tile(norm_g, 2).reshape(1, lanes)
    return pl.pallas_call(
        functools.partial(_hgrn_kernel, n_lat=n_lat, n_ctx=n_ctx),
        grid=(b, width // lanes),
        in_specs=[lat] * 4 + [ctx] * 4 + [pl.BlockSpec((2, 3, lanes), lambda i, j: (0, 0, j)), _const_spec((1, lanes))]
                 + [_const_spec(t.shape) for t in tables],
        out_specs=[lat, ctx],
        out_shape=[jax.ShapeDtypeStruct(q.shape, F32), jax.ShapeDtypeStruct(qc.shape, F32)],
        scratch_shapes=[pltpu.VMEM((n_lat, lanes), F32), pltpu.VMEM((n_ctx, lanes), F32), pltpu.VMEM((2, lanes, lanes), F32)],
        compiler_params=_params('parallel', 'parallel'),
        name='hgrn2',
    )(q, ff, fb, iv, qc, ffc, fbc, ic, lb_terms, g2, *tables)


def _out_proj_kernel(hy_ref, da_ref, mla_ref, hg_ref, gate_ref, x_ref, g1_ref, w_ref, o_ref):
    c = hy_ref.shape[2]
    acc = jnp.dot(hy_ref[0].astype(BF16), w_ref[0:c, :], preferred_element_type=F32)
    for i, head_ref in ((1, da_ref), (2, mla_ref)):
        dv = head_ref.shape[3]
        for h in range(head_ref.shape[1]):
            lo = i * c + h * dv
            acc = acc + jnp.dot(head_ref[0, h].astype(BF16), w_ref[lo:lo + dv, :], preferred_element_type=F32)
    hg = hg_ref[0] * _silu(gate_ref[0])
    acc = acc + jnp.dot(hg.astype(BF16), w_ref[3 * c:4 * c, :], preferred_element_type=F32)
    o_ref[0] = x_ref[0] + g1_ref[0] * acc


def _out_proj(y_hy, y_da, y_mla, y_hg, gate, x, g1, w_out, tm=512):
    b, s, d = x.shape
    tm = min(tm, s)
    c = y_hy.shape[2]
    part = pl.BlockSpec((1, tm, c), lambda i, j: (i, j, 0))
    headed = lambda a: pl.BlockSpec((1, a.shape[1], tm, a.shape[3]), lambda i, j: (i, 0, j, 0))
    row = pl.BlockSpec((1, tm, d), lambda i, j: (i, j, 0))
    return pl.pallas_call(
        _out_proj_kernel,
        grid=(b, s // tm),
        in_specs=[part, headed(y_da), headed(y_mla), part, part, row, pl.BlockSpec((1, 1, d), lambda i, j: (i, 0, 0)),
                  _const_spec(w_out.shape)],
        out_specs=row,
        out_shape=jax.ShapeDtypeStruct(x.shape, F32),
        compiler_params=_params('parallel', 'parallel'),
        name='out_proj',
    )(y_hy, y_da, y_mla, y_hg, gate, x, g1, w_out)


def _router_kernel(x_ref, g_ref, sc_ref, sh_ref, wrt_ref, bias_ref, *rest, compact):
    h = _rms(x_ref[0]) * g_ref[...] * (1.0 + sc_ref[0]) + sh_ref[0]
    tm = h.shape[0]
    scores = jax.nn.sigmoid(_dot_nt(wrt_ref[...], h, precision=HIGHEST))
    choice = scores + bias_ref[...]
    per = N_EXPERTS // N_EXPERT_GROUPS
    neg = -jnp.inf
    iota_g = lax.broadcasted_iota(jnp.int32, (per, tm), 0)
    grp_rows = []
    for gi in range(N_EXPERT_GROUPS):
        blk = choice[gi * per:(gi + 1) * per]
        m1 = jnp.max(blk, axis=0, keepdims=True)
        first = jnp.min(jnp.where(blk == m1, iota_g, per), axis=0, keepdims=True)
        m2 = jnp.max(jnp.where(iota_g == first, neg, blk), axis=0, keepdims=True)
        grp_rows.append(m1 + m2)
    grp = jnp.concatenate(grp_rows, axis=0)
    iota_n = lax.broadcasted_iota(jnp.int32, (N_EXPERT_GROUPS, tm), 0)
    gsel = jnp.zeros((N_EXPERT_GROUPS, tm), F32)
    for _ in range(TOPK_GROUPS):
        m = jnp.max(grp, axis=0, keepdims=True)
        first = jnp.min(jnp.where(grp == m, iota_n, N_EXPERT_GROUPS), axis=0, keepdims=True)
        hit = iota_n == first
        gsel = jnp.where(hit, 1.0, gsel)
        grp = jnp.where(hit, neg, grp)
    emask = jnp.concatenate([jnp.broadcast_to(gsel[gi:gi + 1], (per, tm)) for gi in range(N_EXPERT_GROUPS)], axis=0)
    cand = jnp.where(emask > 0.0, choice, neg)
    iota_e = lax.broadcasted_iota(jnp.int32, (N_EXPERTS, tm), 0)
    sel = jnp.zeros((N_EXPERTS, tm), F32)
    chosen = []
    for _ in range(TOP_K):
        m = jnp.max(cand, axis=0, keepdims=True)
        first = jnp.min(jnp.where(cand == m, iota_e, N_EXPERTS), axis=0, keepdims=True)
        hit = iota_e == first
        sel = jnp.where(hit, 1.0, sel)
        cand = jnp.where(hit, neg, cand)
        chosen.append(first)
    w = scores * sel
    gate = w / jnp.sum(w, axis=0, keepdims=True) * ROUTED_SCALE
    if not compact:
        h_ref, gate_ref = rest
        h_ref[0] = h.astype(BF16)
        gate_ref[0] = gate
        return
    hp_ref, eid_ref, rank_ref, w_ref, cnt_out_ref, cnt_ref = rest
    hp_ref[0] = _pack_halves(h)

    @pl.when((pl.program_id(0) == 0) & (pl.program_id(1) == 0))
    def _():
        cnt_ref[...] = jnp.zeros(cnt_ref.shape, F32)

    src = lax.broadcasted_iota(jnp.int32, (tm, tm), 0)
    dst = lax.broadcasted_iota(jnp.int32, (tm, tm), 1)
    running = jnp.dot(sel.astype(BF16), (src <= dst).astype(BF16), preferred_element_type=F32)
    rank_dense = cnt_ref[:, 0:1] + running - 1.0
    e_rows, r_rows, w_rows = [], [], []
    for first in chosen:
        hit = iota_e == first
        e_rows.append(first)
        r_rows.append(jnp.sum(jnp.where(hit, rank_dense, 0.0), axis=0, keepdims=True))
        w_rows.append(jnp.sum(jnp.where(hit, gate, 0.0), axis=0, keepdims=True))
    eid_ref[...] = jnp.concatenate(e_rows, axis=0)
    rank_ref[...] = jnp.concatenate(r_rows, axis=0).astype(jnp.int32)
    w_ref[...] = jnp.concatenate(w_rows, axis=0)
    cnt_ref[...] = cnt_ref[...] + running[:, tm - 1:tm]
    cnt_out_ref[...] = cnt_ref[...]


def _router(x, g, scale, shift, w_router, e_bias, tm=512, compact=False):
    b, s, d = x.shape
    tm = min(tm, s)
    e = w_router.shape[1]
    row = pl.BlockSpec((1, tm, d), lambda i, j: (i, j, 0))
    mod = pl.BlockSpec((1, 1, d), lambda i, j: (i, 0, 0))
    if compact:
        nj = s // tm
        tok = pl.BlockSpec((TOP_K, tm), lambda i, j: (0, i * nj + j))
        out_specs = [pl.BlockSpec((1, tm, d // 2), lambda i, j: (i, j, 0)), tok, tok, tok, _const_spec((e, LANES))]
        out_shape = [jax.ShapeDtypeStruct((b, s, d // 2), jnp.int32), jax.ShapeDtypeStruct((TOP_K, b * s), jnp.int32),
                     jax.ShapeDtypeStruct((TOP_K, b * s), jnp.int32), jax.ShapeDtypeStruct((TOP_K, b * s), F32),
                     jax.ShapeDtypeStruct((e, LANES), F32)]
        scratch = [pltpu.VMEM((e, LANES), F32)]
        semantics = ('arbitrary', 'arbitrary')
    else:
        out_specs = [row, pl.BlockSpec((1, e, tm), lambda i, j: (i, 0, j))]
        out_shape = [jax.ShapeDtypeStruct((b, s, d), BF16), jax.ShapeDtypeStruct((b, e, s), F32)]
        scratch = []
        semantics = ('parallel', 'parallel')
    return pl.pallas_call(
        functools.partial(_router_kernel, compact=compact),
        grid=(b, s // tm),
        in_specs=[row, _const_spec((1, d)), mod, mod, _const_spec((e, d)), _const_spec((e, 1))],
        out_specs=out_specs,
        out_shape=out_shape,
        scratch_shapes=scratch,
        compiler_params=_params(*semantics),
        name='router',
    )(x, g.reshape(1, d), scale, shift, w_router.T, e_bias.reshape(e, 1))


def _moe_kernel(h_ref, x_ref, gate_ref, g2_ref, wg_ref, wu_ref, wd_ref, sg_ref, su_ref, sd_ref, *rest, final):
    if final:
        fg_ref, o_ref, acc_ref = rest
    else:
        o_ref, acc_ref = rest
    e = pl.program_id(2)
    h = h_ref[0]

    @pl.when(e == 0)
    def _():
        a = jnp.dot(h, sg_ref[...], preferred_element_type=F32)
        u = jnp.dot(h, su_ref[...], preferred_element_type=F32)
        acc_ref[...] = jnp.dot((_silu(a) * u).astype(BF16), sd_ref[...], preferred_element_type=F32)

    lane = lax.broadcasted_iota(jnp.int32, gate_ref.shape[1:], 1)
    gcol = jnp.sum(jnp.where(lane == e, gate_ref[0], 0.0), axis=-1, keepdims=True)
    a = jnp.dot(h, wg_ref[0].astype(BF16), preferred_element_type=F32)
    u = jnp.dot(h, wu_ref[0].astype(BF16), preferred_element_type=F32)
    acc_ref[...] += jnp.dot((_silu(a) * u * gcol).astype(BF16), wd_ref[0].astype(BF16), preferred_element_type=F32)

    @pl.when(e == pl.num_programs(2) - 1)
    def _():
        y = x_ref[0] + g2_ref[0] * acc_ref[...]
        if final:
            y = _rms(y) * fg_ref[...]
        o_ref[0] = y


def _moe(h2, x, gate, g2, layer, w_gate, w_up, w_down, s_gate, s_up, s_down, final_g=None, tm=1024):
    b, s, d = x.shape
    tm = min(tm, s)
    _, e, _, ff = w_gate.shape
    row = pl.BlockSpec((1, tm, d), lambda i, j, k: (i, j, 0))
    ins = [h2, x, gate, g2, w_gate, w_up, w_down, s_gate, s_up, s_down]
    in_specs = [row, row, pl.BlockSpec((1, tm, e), lambda i, j, k: (i, j, 0)),
                pl.BlockSpec((1, 1, d), lambda i, j, k: (i, 0, 0)),
                pl.BlockSpec((None, 1, d, ff), lambda i, j, k: (layer, k, 0, 0)),
                pl.BlockSpec((None, 1, d, ff), lambda i, j, k: (layer, k, 0, 0)),
                pl.BlockSpec((None, 1, ff, d), lambda i, j, k: (layer, k, 0, 0)),
                _const_spec(s_gate.shape), _const_spec(s_up.shape), _const_spec(s_down.shape)]
    if final_g is not None:
        ins.append(final_g.reshape(1, d))
        in_specs.append(_const_spec((1, d)))
    return pl.pallas_call(
        functools.partial(_moe_kernel, final=final_g is not None),
        grid=(b, s // tm, e),
        in_specs=in_specs,
        out_specs=row,
        out_shape=jax.ShapeDtypeStruct(x.shape, F32),
        scratch_shapes=[pltpu.VMEM((tm, d), F32)],
        compiler_params=_params('parallel', 'parallel', 'arbitrary'),
        name='moe',
    )(*ins)


MOE_ROW_TILE = 256
SC_ROWS = 64
V7X_SC_CORES = 2
V7X_SC_SUBCORES = 16


def _pack_halves(x):
    n = x.shape[1] // 2
    lo = pltpu.bitcast(x[:, :n].astype(BF16).astype(F32), jnp.int32)
    hi = pltpu.bitcast(x[:, n:].astype(BF16).astype(F32), jnp.int32)
    return jnp.bitwise_or(jnp.bitwise_and(hi, -65536), lax.shift_right_logical(lo, 16))


def _unpack_halves(p):
    lo = pltpu.bitcast(lax.shift_left(p, 16), F32).astype(BF16)
    hi = pltpu.bitcast(jnp.bitwise_and(p, -65536), F32).astype(BF16)
    return lo, hi


def _route_pos_kernel(off_ref, eid_ref, rank_ref, pos_ref):
    eid = eid_ref[...]
    base = jnp.zeros(eid.shape, jnp.int32)
    for e in range(N_EXPERTS):
        base = jnp.where(eid == e, off_ref[e], base)
    pos_ref[...] = base + rank_ref[...]


def _route_pos(offsets, eid, rank):
    return pl.pallas_call(
        _route_pos_kernel,
        grid=(1,),
        in_specs=[pl.BlockSpec(memory_space=pltpu.SMEM), _const_spec(eid.shape), _const_spec(rank.shape)],
        out_specs=_const_spec(eid.shape),
        out_shape=jax.ShapeDtypeStruct(eid.shape, jnp.int32),
        compiler_params=_params('arbitrary'),
        name='route_pos',
    )(offsets, eid, rank)


def _sc_mesh():
    return plsc.VectorSubcoreMesh(core_axis_name='c', subcore_axis_name='s', num_cores=V7X_SC_CORES,
                                  num_subcores=V7X_SC_SUBCORES)


def _sc_dispatch(hp, pos, n_rows):
    t, w = hp.shape
    k = pos.shape[0]
    workers = V7X_SC_CORES * V7X_SC_SUBCORES
    per_worker = t // workers
    pos_flat = pos.reshape(k * t)

    @functools.partial(pl.kernel, mesh=_sc_mesh(), out_type=jax.ShapeDtypeStruct((n_rows, w), jnp.int32),
                       scratch_types=[pltpu.VMEM((SC_ROWS,), jnp.int32), pltpu.VMEM((SC_ROWS, w), jnp.int32),
                                      pltpu.SemaphoreType.DMA])
    def scatter(hp_hbm, pos_hbm, out_hbm, idx_v, rows_v, sem):
        wid = lax.axis_index('s') * V7X_SC_CORES + lax.axis_index('c')

        @pl.loop(0, per_worker // SC_ROWS)
        def _(i):
            t0 = pl.multiple_of(wid * per_worker + i * SC_ROWS, SC_ROWS)
            pltpu.sync_copy(hp_hbm.at[pl.ds(t0, SC_ROWS)], rows_v)
            for j in range(k):
                pltpu.sync_copy(pos_hbm.at[pl.ds(pl.multiple_of(j * t + t0, SC_ROWS), SC_ROWS)], idx_v)
                pltpu.async_copy(rows_v, out_hbm.at[idx_v], sem).wait()

    return scatter(hp, pos_flat)


def _sc_collect(yp, pos):
    _, w = yp.shape
    k, t = pos.shape
    workers = V7X_SC_CORES * V7X_SC_SUBCORES
    per_worker = k * t // workers
    pos_flat = pos.reshape(k * t)

    @functools.partial(pl.kernel, mesh=_sc_mesh(), out_type=jax.ShapeDtypeStruct((k * t, w), jnp.int32),
                       scratch_types=[pltpu.VMEM((SC_ROWS,), jnp.int32), pltpu.VMEM((SC_ROWS, w), jnp.int32),
                                      pltpu.SemaphoreType.DMA])
    def gather(yp_hbm, pos_hbm, out_hbm, idx_v, rows_v, sem):
        wid = lax.axis_index('s') * V7X_SC_CORES + lax.axis_index('c')

        @pl.loop(0, per_worker // SC_ROWS)
        def _(i):
            r0 = pl.multiple_of(wid * per_worker + i * SC_ROWS, SC_ROWS)
            pltpu.sync_copy(pos_hbm.at[pl.ds(r0, SC_ROWS)], idx_v)
            pltpu.async_copy(yp_hbm.at[idx_v], rows_v, sem).wait()
            pltpu.sync_copy(rows_v, out_hbm.at[pl.ds(r0, SC_ROWS)])

    return gather(yp, pos_flat)


def _expert_kernel(te_ref, nu_ref, x_ref, wg_ref, wu_ref, wd_ref, o_ref):
    @pl.when(pl.program_id(0) < nu_ref[0])
    def _():
        lo, hi = _unpack_halves(x_ref[...])
        half = lo.shape[1]
        wg, wu = wg_ref[0].astype(BF16), wu_ref[0].astype(BF16)
        a = (jnp.dot(lo, wg[:half], preferred_element_type=F32) + jnp.dot(hi, wg[half:], preferred_element_type=F32))
        u = (jnp.dot(lo, wu[:half], preferred_element_type=F32) + jnp.dot(hi, wu[half:], preferred_element_type=F32))
        y = jnp.dot((_silu(a) * u).astype(BF16), wd_ref[0].astype(BF16), preferred_element_type=F32)
        o_ref[...] = _pack_halves(y)


def _experts(xp, tile_expert, n_used, layer, w_gate, w_up, w_down):
    n_rows, half = xp.shape
    _, _, d, ff = w_gate.shape
    r = MOE_ROW_TILE
    row = pl.BlockSpec((r, half), lambda i, te, nu: (i, 0))
    grid_spec = pltpu.PrefetchScalarGridSpec(
        num_scalar_prefetch=2,
        grid=(n_rows // r,),
        in_specs=[row,
                  pl.BlockSpec((None, 1, d, ff), lambda i, te, nu: (layer, te[i], 0, 0)),
                  pl.BlockSpec((None, 1, d, ff), lambda i, te, nu: (layer, te[i], 0, 0)),
                  pl.BlockSpec((None, 1, ff, d), lambda i, te, nu: (layer, te[i], 0, 0))],
        out_specs=row,
    )
    return pl.pallas_call(
        _expert_kernel,
        grid_spec=grid_spec,
        out_shape=jax.ShapeDtypeStruct((n_rows, half), jnp.int32),
        compiler_params=_params('arbitrary'),
        name='experts',
    )(tile_expert, n_used, xp, w_gate, w_up, w_down)


def _combine_kernel(yg_ref, w_ref, hp_ref, x_ref, g2_ref, sg_ref, su_ref, sd_ref, *rest, final):
    if final:
        fg_ref, o_ref = rest
    else:
        (o_ref,) = rest
    half = hp_ref.shape[2]
    lo, hi = _unpack_halves(hp_ref[0])
    sg, su = sg_ref[...], su_ref[...]
    a = jnp.dot(lo, sg[:half], preferred_element_type=F32) + jnp.dot(hi, sg[half:], preferred_element_type=F32)
    u = jnp.dot(lo, su[:half], preferred_element_type=F32) + jnp.dot(hi, su[half:], preferred_element_type=F32)
    acc = jnp.dot((_silu(a) * u).astype(BF16), sd_ref[...], preferred_element_type=F32)
    acc_lo, acc_hi = acc[:, :half], acc[:, half:]
    wts = w_ref[0]
    for k in range(yg_ref.shape[0]):
        ylo, yhi = _unpack_halves(yg_ref[k, 0])
        wk = wts[:, k:k + 1]
        acc_lo = acc_lo + wk * ylo.astype(F32)
        acc_hi = acc_hi + wk * yhi.astype(F32)
    y = x_ref[0] + g2_ref[0] * jnp.concatenate([acc_lo, acc_hi], axis=1)
    if final:
        y = _rms(y) * fg_ref[...]
    o_ref[0] = y


def _combine(yg, wts, hp, x, g2, s_gate, s_up, s_down, final_g=None, tm=256):
    b, s, d = x.shape
    k = yg.shape[0]
    half = d // 2
    row = pl.BlockSpec((1, tm, d), lambda i, j: (i, j, 0))
    prow = pl.BlockSpec((1, tm, half), lambda i, j: (i, j, 0))
    ins = [yg, wts, hp, x, g2, s_gate, s_up, s_down]
    in_specs = [pl.BlockSpec((k, 1, tm, half), lambda i, j: (0, i, j, 0)), pl.BlockSpec((1, tm, k), lambda i, j: (i, j, 0)),
                prow, row, pl.BlockSpec((1, 1, d), lambda i, j: (i, 0, 0)),
                _const_spec(s_gate.shape), _const_spec(s_up.shape), _const_spec(s_down.shape)]
    if final_g is not None:
        ins.append(final_g.reshape(1, d))
        in_specs.append(_const_spec((1, d)))
    return pl.pallas_call(
        functools.partial(_combine_kernel, final=final_g is not None),
        grid=(b, s // tm),
        in_specs=in_specs,
        out_specs=row,
        out_shape=jax.ShapeDtypeStruct(x.shape, F32),
        compiler_params=_params('parallel', 'parallel'),
        name='moe_combine',
    )(*ins)


def _routed_moe(x, g, scale, shift, g2, w_router, e_bias, layer, w_gate, w_up, w_down, s_gate, s_up, s_down, final_g=None):
    b, s, d = x.shape
    t = b * s
    hp, eid, rank, wts, counts = _router(x, g, scale, shift, w_router, e_bias, compact=True)
    counts = counts[:, 0].astype(jnp.int32)
    r = MOE_ROW_TILE
    padded = (counts + (r - 1)) // r * r
    ends = jnp.cumsum(padded)
    offsets = ends - padded
    n_rows = t * TOP_K + N_EXPERTS * r
    tile_start = jnp.arange(n_rows // r, dtype=jnp.int32) * r
    tile_expert = jnp.minimum(jnp.sum((tile_start[:, None] >= ends[None, :]).astype(jnp.int32), axis=1), N_EXPERTS - 1)
    n_used = (ends[-1] // r).reshape(1).astype(jnp.int32)
    pos = _route_pos(offsets.astype(jnp.int32), eid, rank)
    xp = _sc_dispatch(hp.reshape(t, d // 2), pos, n_rows)
    yp = _experts(xp, tile_expert.astype(jnp.int32), n_used, layer, w_gate, w_up, w_down)
    yg = _sc_collect(yp, pos).reshape(TOP_K, b, s, d // 2)
    return _combine(yg, wts.T.reshape(b, s, TOP_K), hp, x, g2, s_gate, s_up, s_down, final_g)


def _mixers(p, pc, ctx_out, prm, l, lam_init, rope_tabs, lb_terms):
    s = p['hy_v'].shape[1]
    sc = pc['hy_v'].shape[1]

    hy_args = (prm['hy_w1'][l], prm['hy_b1'][l], prm['hy_w2'][l], prm['hy_b2'][l], prm['hy_w3'][l], prm['hy_b3'][l],
               prm['hy_sin_freq'][l], prm['hy_decay'][l])
    y_hy = _hyena([p['hy_v'], p['hy_x1'], p['hy_x2']], prm['hy_conv_w'][l], prm['hy_conv_b'][l],
                  _hy_filters(s, *hy_args), prm['hy_bias'][l], inner=128)
    yc_hy = None
    if ctx_out:
        yc_hy = _hyena([pc['hy_v'], pc['hy_x1'], pc['hy_x2']], prm['hy_conv_w'][l], prm['hy_conv_b'][l],
                       _hy_filters(sc, *hy_args), prm['hy_bias'][l], inner=32)

    lp = prm['da_lambda'][l].astype(F32)
    lam = jnp.exp(jnp.sum(lp[0] * lp[1])) - jnp.exp(jnp.sum(lp[2] * lp[3])) + lam_init
    da_kw = dict(heads=DA_HEADS, ncomp=2, scale=DA_HEAD_DIM ** -0.5, lam=lam, subln_g=prm['da_subln_g'][l],
                 post_scale=1.0 - lam_init)
    da_ctx = ([pc['da_k']], pc['da_v'])
    y_da = _attention([p['da_q']], [da_ctx, ([p['da_k']], p['da_v'])], **da_kw)
    yc_da = _attention([pc['da_q']], [da_ctx], **da_kw) if ctx_out else None

    wq = prm['mla_w_q_up'][l].reshape(MLA_Q_RANK, MLA_HEADS, MLA_NOPE_DIM + MLA_ROPE_DIM)
    wq_n = wq[:, :, :MLA_NOPE_DIM].reshape(MLA_Q_RANK, -1).astype(BF16)
    wq_r = wq[:, :, MLA_NOPE_DIM:].reshape(MLA_Q_RANK, -1).astype(BF16)
    wkv = prm['mla_w_kv_up'][l].reshape(MLA_KV_RANK, MLA_HEADS, MLA_NOPE_DIM + MLA_V_DIM)
    wkv_n = wkv[:, :, :MLA_NOPE_DIM].reshape(MLA_KV_RANK, -1).astype(BF16)
    wkv_v = wkv[:, :, MLA_NOPE_DIM:].reshape(MLA_KV_RANK, -1).astype(BF16)

    def queries(qd, tabs):
        return _norm_proj(qd, prm['mla_q_norm_g'][l], [(wq_n, F32, False, MLA_HEADS), (wq_r, F32, True, MLA_HEADS)],
                          rope_tabs=tabs)

    def keys_values(kvd):
        return _norm_proj(kvd, prm['mla_kv_norm_g'][l], [(wkv_n, BF16, False, MLA_HEADS), (wkv_v, BF16, False, MLA_HEADS)])

    kn_l, v_l = keys_values(p['mla_kv'])
    kn_c, v_c = keys_values(pc['mla_kv'])
    mla_kw = dict(heads=MLA_HEADS, ncomp=1, scale=(MLA_NOPE_DIM + MLA_ROPE_DIM) ** -0.5)
    mla_ctx = ([kn_c, pc['mla_kr']], v_c)
    y_mla = _attention(queries(p['mla_q'], rope_tabs), [mla_ctx, ([kn_l, p['mla_kr']], v_l)], **mla_kw)
    yc_mla = _attention(queries(pc['mla_q'], None), [mla_ctx], **mla_kw) if ctx_out else None

    o, oc = _hgrn(p['hg_q'], p['hg_ff'], p['hg_fb'], p['hg_i'], pc['hg_q'], pc['hg_ff'], pc['hg_fb'], pc['hg_i'],
                  lb_terms, prm['hg_norm_g'][l])
    return (y_hy, y_da, y_mla, o), (yc_hy, yc_da, yc_mla, oc)


def kernel(x, c, ctx, c_ctx, w_ada, b_ada, norm1_g, norm2_g, w_in, w_out, hy_conv_w, hy_conv_b, hy_w1, hy_b1, hy_w2, hy_b2, hy_w3, hy_b3, hy_sin_freq, hy_decay, hy_bias, da_lambda, da_subln_g, mla_q_norm_g, mla_w_q_up, mla_kv_norm_g, mla_w_kv_up, hg_lower_bounds, hg_norm_g, moe_w_router, moe_bias, moe_w_gate, moe_w_up, moe_w_down, moe_sh_gate, moe_sh_up, moe_sh_down, final_norm_g):
    prm = dict(hy_conv_w=hy_conv_w, hy_conv_b=hy_conv_b, hy_w1=hy_w1, hy_b1=hy_b1, hy_w2=hy_w2, hy_b2=hy_b2,
               hy_w3=hy_w3, hy_b3=hy_b3, hy_sin_freq=hy_sin_freq, hy_decay=hy_decay, hy_bias=hy_bias,
               da_lambda=da_lambda, da_subln_g=da_subln_g, mla_q_norm_g=mla_q_norm_g, mla_w_q_up=mla_w_q_up,
               mla_kv_norm_g=mla_kv_norm_g, mla_w_kv_up=mla_w_kv_up, hg_norm_g=hg_norm_g)
    b, n_lat, d = x.shape
    depth = w_in.shape[0]
    rows = n_lat // GRID_W
    row_pos = jnp.repeat(jnp.arange(rows, dtype=jnp.int32), GRID_W)
    col_pos = jnp.tile(jnp.arange(GRID_W, dtype=jnp.int32), rows)
    rope_tabs = _rope_tables(row_pos, col_pos, 2 * DA_HEADS * DA_HEAD_DIM)
    lbs = jnp.cumsum(jax.nn.softmax(hg_lower_bounds.astype(F32), axis=1), axis=1)
    lbs = lbs - lbs[:, :1]
    cond = jnp.concatenate([c, c_ctx[None], jnp.zeros((8 - b - 1, d), F32)], axis=0)

    for l in range(depth):
        ctx_out = l < depth - 1
        mods = _ada(cond, w_ada[l], b_ada[l])
        sh1, sc1, g1, sh2, sc2, g2 = [m[:, None, :] for m in jnp.split(mods[:b], 6, axis=-1)]
        mc = [jnp.broadcast_to(m[:, None, :], (b, 1, d)) for m in jnp.split(mods[b:b + 1], 6, axis=-1)]

        off = 0
        outs = []
        for _, wdt, dt, rope, split, rep in _SEGMENTS:
            w = w_in[l][:, off:off + wdt].astype(BF16)
            outs.append((jnp.tile(w, (1, rep)) if rep > 1 else w, dt, rope, split))
            off += wdt
        names = [seg[0] for seg in _SEGMENTS]
        p = dict(zip(names, _norm_proj(x, norm1_g[l], outs, sc1, sh1, rope_tabs=rope_tabs)))
        pc = dict(zip(names, _norm_proj(ctx, norm1_g[l], outs, mc[1], mc[0])))

        lb = lbs[:, l]
        lb_terms = jnp.stack([jnp.log(lb), jnp.log1p(-lb), 1.0 - lb], axis=1)
        lam_init = 0.8 - 0.6 * math.exp(-0.3 * l)
        lat_parts, ctx_parts = _mixers(p, pc, ctx_out, prm, l, lam_init, rope_tabs, lb_terms)

        w_out_b = w_out[l].astype(BF16)
        moe_w = (l, moe_w_gate, moe_w_up, moe_w_down,
                 moe_sh_gate[l].astype(BF16), moe_sh_up[l].astype(BF16), moe_sh_down[l].astype(BF16))

        if ctx_out:
            ctx = _out_proj(*ctx_parts, pc['hg_g'], ctx, mc[2], w_out_b)
            h2c, gate_c = _router(ctx, norm2_g[l], mc[4], mc[3], moe_w_router[l], moe_bias[l])
            ctx = _moe(h2c, ctx, gate_c.transpose(0, 2, 1), mc[5], *moe_w)

        x = _out_proj(*lat_parts, p['hg_g'], x, g1, w_out_b)
        x = _routed_moe(x, norm2_g[l], sc2, sh2, g2, moe_w_router[l], moe_bias[l], *moe_w,
                        final_g=None if ctx_out else final_norm_g)

    return x
```

```python
import functools
import math

import numpy as np
import jax
import jax.numpy as jnp
from jax import lax
from jax.experimental import pallas as pl
from jax.experimental.pallas import tpu as pltpu
from jax.experimental.pallas import tpu_sc as plsc

F32 = jnp.float32
BF16 = jnp.bfloat16
HIGHEST = lax.Precision.HIGHEST

D_MODEL = 1024
GRID_W = 64
HY_WIDTH = 256
HY_ORDER = 2
HY_BANDS = 16
DA_HEADS = 4
DA_HEAD_DIM = 32
MLA_HEADS = 4
MLA_Q_RANK = 192
MLA_KV_RANK = 128
MLA_NOPE_DIM = 64
MLA_ROPE_DIM = 32
MLA_V_DIM = 64
HG_HEADS = 4
HG_KEY_DIM = 64
HG_VAL_DIM = 64
HG_CHUNK = 64
HG_SUB = 8
N_EXPERTS = 64
N_EXPERT_GROUPS = 8
TOPK_GROUPS = 4
TOP_K = 8
EXPERT_FF = 256
ROUTED_SCALE = 2.5
ROPE_BASE = 10000.0
NORM_EPS = 1e-6

V7X_VMEM_LIMIT_BYTES = 56 * 1024 * 1024
LANES = 128

_SEGMENTS = (
    ('hy_v', HY_WIDTH, F32, False, 0, 1), ('hy_x1', HY_WIDTH, F32, False, 0, 1), ('hy_x2', HY_WIDTH, F32, False, 0, 1),
    ('da_q', 2 * DA_HEADS * DA_HEAD_DIM, F32, True, 2 * DA_HEADS, 1),
    ('da_k', 2 * DA_HEADS * DA_HEAD_DIM, BF16, True, 2 * DA_HEADS, 1),
    ('da_v', 2 * DA_HEADS * DA_HEAD_DIM, BF16, False, DA_HEADS, 1),
    ('mla_q', MLA_Q_RANK, F32, False, 0, 1), ('mla_kv', MLA_KV_RANK, F32, False, 0, 1),
    ('mla_kr', MLA_ROPE_DIM, BF16, True, MLA_HEADS, MLA_HEADS),
    ('hg_q', HG_HEADS * HG_KEY_DIM, F32, False, 0, 1), ('hg_ff', HG_HEADS * HG_KEY_DIM, F32, False, 0, 1),
    ('hg_fb', HG_HEADS * HG_KEY_DIM, F32, False, 0, 1), ('hg_i', HG_HEADS * HG_VAL_DIM, F32, False, 0, 1),
    ('hg_g', HG_HEADS * HG_VAL_DIM, F32, False, 0, 1),
)


def _params(*semantics):
    return pltpu.CompilerParams(dimension_semantics=semantics, vmem_limit_bytes=V7X_VMEM_LIMIT_BYTES)


def _const_spec(shape):
    nd = len(shape)
    return pl.BlockSpec(shape, lambda *_: (0,) * nd)


def _rms(x, eps=NORM_EPS):
    return x * lax.rsqrt(jnp.mean(x * x, axis=-1, keepdims=True) + eps)


def _silu(x):
    return x * jax.nn.sigmoid(x)


def _dot_nt(a, b, **kw):
    return lax.dot_general(a, b, (((1,), (1,)), ((), ())), preferred_element_type=F32, **kw)


def _ada_kernel(c_ref, w_ref, b_ref, o_ref):
    s = _silu(c_ref[...])
    o_ref[...] = jnp.dot(s, w_ref[...], precision=HIGHEST, preferred_element_type=F32) + b_ref[...]


def _ada(cond, w, b):
    r, d = cond.shape
    n = w.shape[1]
    tn = 1536
    return pl.pallas_call(
        _ada_kernel,
        grid=(n // tn,),
        in_specs=[_const_spec((r, d)), pl.BlockSpec((d, tn), lambda j: (0, j)), pl.BlockSpec((1, tn), lambda j: (0, j))],
        out_specs=pl.BlockSpec((r, tn), lambda j: (0, j)),
        out_shape=jax.ShapeDtypeStruct((r, n), F32),
        compiler_params=_params('arbitrary'),
        name='ada',
    )(cond, w, b.reshape(1, n))


ROPE_UNIT = 32


def _rope_tables(row, col, width):
    n = ROPE_UNIT // 4
    inv = ROPE_BASE ** (-jnp.arange(n, dtype=F32) / n)
    units = width // ROPE_UNIT
    parts_c, parts_a, parts_b = [], [], []
    zero = jnp.zeros((row.shape[0], n), F32)
    for pos in (row, col):
        ang = pos.astype(F32)[:, None] * inv
        cos, sin = jnp.cos(ang), jnp.sin(ang)
        parts_c += [cos, cos]
        parts_a += [zero, sin]
        parts_b += [-sin, zero]
    tile = lambda ps: jnp.tile(jnp.concatenate(ps, axis=1), (1, units))
    return tile(parts_c), tile(parts_a), tile(parts_b)


def _norm_proj_kernel(*refs, n_w, modulate, ropes, splits):
    x_ref, g_ref = refs[0], refs[1]
    pos = 2
    if modulate:
        sc_ref, sh_ref = refs[2], refs[3]
        pos = 4
    if any(ropes):
        rc_ref, ra_ref, rb_ref = refs[pos:pos + 3]
        pos += 3
    w_refs = refs[pos:pos + n_w]
    o_refs = refs[pos + n_w:]
    y = _rms(x_ref[0]) * g_ref[...]
    if modulate:
        y = y * (1.0 + sc_ref[0]) + sh_ref[0]
    yb = y.astype(BF16)
    for w_ref, o_ref, rope, split in zip(w_refs, o_refs, ropes, splits):
        o = jnp.dot(yb, w_ref[...], preferred_element_type=F32)
        if rope:
            wd = o.shape[1]
            shift = ROPE_UNIT // 4
            o = (o * rc_ref[:, :wd] + pltpu.roll(o, shift, axis=1) * ra_ref[:, :wd]
                 + pltpu.roll(o, wd - shift, axis=1) * rb_ref[:, :wd])
        if split:
            unit = o.shape[1] // split
            for u in range(split):
                o_ref[0, u] = o[:, u * unit:(u + 1) * unit].astype(o_ref.dtype)
        else:
            o_ref[0] = o.astype(o_ref.dtype)


def _norm_proj(x, g, outs, scale=None, shift=None, rope_tabs=None, tm=512):
    b, s, k = x.shape
    tm = min(tm, s)
    modulate = scale is not None
    ropes = tuple(bool(o[2]) and rope_tabs is not None for o in outs)
    splits = tuple(o[3] for o in outs)
    ins = [x, g.reshape(1, k)]
    in_specs = [pl.BlockSpec((1, tm, k), lambda i, j: (i, j, 0)), _const_spec((1, k))]
    if modulate:
        ins += [scale, shift]
        in_specs += [pl.BlockSpec((1, 1, k), lambda i, j: (i, 0, 0))] * 2
    if any(ropes):
        ins += list(rope_tabs)
        in_specs += [pl.BlockSpec((tm, rope_tabs[0].shape[1]), lambda i, j: (j, 0))] * 3
    out_specs, out_shape = [], []
    for w, dt, _, split in outs:
        ins.append(w)
        in_specs.append(_const_spec(w.shape))
        n = w.shape[1]
        if split:
            out_specs.append(pl.BlockSpec((1, split, tm, n // split), lambda i, j: (i, 0, j, 0)))
            out_shape.append(jax.ShapeDtypeStruct((b, split, s, n // split), dt))
        else:
            out_specs.append(pl.BlockSpec((1, tm, n), lambda i, j: (i, j, 0)))
            out_shape.append(jax.ShapeDtypeStruct((b, s, n), dt))
    return pl.pallas_call(
        functools.partial(_norm_proj_kernel, n_w=len(outs), modulate=modulate, ropes=ropes, splits=splits),
        grid=(b, s // tm),
        in_specs=in_specs,
        out_specs=out_specs,
        out_shape=out_shape,
        compiler_params=_params('parallel', 'parallel'),
        name='norm_proj',
    )(*ins)


def _hy_filter_kernel(w1t_ref, w1s_ref, w1c_ref, b1_ref, w2_ref, b2_ref, w3_ref, b3_ref, fr_ref, dec_ref, o_ref, *, n):
    t = lax.broadcasted_iota(jnp.int32, (n, 1), 0).astype(F32) / n
    bands = lax.broadcasted_iota(jnp.int32, (1, HY_BANDS), 1).astype(F32) + 1.0
    ang = (2.0 * jnp.pi) * t * bands
    pre = (t * w1t_ref[...]
           + jnp.dot(jnp.sin(ang), w1s_ref[...], precision=HIGHEST, preferred_element_type=F32)
           + jnp.dot(jnp.cos(ang), w1c_ref[...], precision=HIGHEST, preferred_element_type=F32)
           + b1_ref[...])
    hid = jnp.sin(fr_ref[0:1, :] * pre)
    hid = jnp.sin(fr_ref[1:2, :] * (jnp.dot(hid, w2_ref[...], precision=HIGHEST, preferred_element_type=F32) + b2_ref[...]))
    filt = jnp.dot(hid, w3_ref[...], precision=HIGHEST, preferred_element_type=F32) + b3_ref[...]
    filt = filt * jnp.exp(-t * jnp.abs(dec_ref[...]))
    col = jnp.sum(jnp.abs(filt), axis=0, keepdims=True) - jnp.abs(filt[0:1, :])
    w = HY_WIDTH
    for o in range(HY_ORDER):
        lo = o * 2 * w
        f0 = filt[0:1, lo:lo + w] + filt[0:1, lo + w:lo + 2 * w]
        inv = 1.0 / (col[:, lo:lo + w] + col[:, lo + w:lo + 2 * w] + jnp.abs(f0))
        o_ref[:, lo:lo + w] = filt[:, lo:lo + w] * inv
        o_ref[:, lo + w:lo + 2 * w] = filt[:, lo + w:lo + 2 * w] * inv


def _hy_filters(n, w1, b1, w2, b2, w3, b3, freq, decay):
    cols = w3.shape[1]
    ins = [w1[0:1], w1[1:1 + HY_BANDS], w1[1 + HY_BANDS:], b1.reshape(1, -1), w2, b2.reshape(1, -1), w3,
           b3.reshape(1, -1), freq, decay.reshape(1, -1)]
    out = pl.pallas_call(
        functools.partial(_hy_filter_kernel, n=n),
        grid=(1,),
        in_specs=[_const_spec(a.shape) for a in ins],
        out_specs=_const_spec((n, cols)),
        out_shape=jax.ShapeDtypeStruct((n, cols), F32),
        compiler_params=_params('arbitrary'),
        name='hy_filter',
    )(*ins)
    return out.reshape(n, HY_ORDER, 2, HY_WIDTH)


def _two_sided(filt_n):
    n = filt_n.shape[0]
    hf, hb = filt_n[:, :, 0], filt_n[:, :, 1]
    h = jnp.concatenate([hf[:1] + hb[:1], hf[1:], jnp.zeros((1,) + hf.shape[1:], F32), hb[:0:-1]], axis=0)
    return h.reshape(2 * n, HY_ORDER * HY_WIDTH)


def _short_conv_kernel(*refs, s):
    x_refs, w_refs, b_refs, o_refs = refs[0:3], refs[3:6], refs[6:9], refs[9:12]
    row = lax.broadcasted_iota(jnp.int32, (s, 1), 0)
    for x_ref, w_ref, b_ref, o_ref in zip(x_refs, w_refs, b_refs, o_refs):
        x = x_ref[0]
        prev = jnp.where(row == 0, 0.0, pltpu.roll(x, 1, axis=0))
        nxt = jnp.where(row == s - 1, 0.0, pltpu.roll(x, s - 1, axis=0))
        o_ref[0] = prev * w_ref[0:1, :] + x * w_ref[1:2, :] + nxt * w_ref[2:3, :] + b_ref[...]


def _short_conv(parts, conv_w, conv_b):
    b, s, c = parts[0].shape
    tc = LANES
    ws = [conv_w[:, i * c:(i + 1) * c] for i in range(3)]
    bs = [conv_b[i * c:(i + 1) * c].reshape(1, c) for i in range(3)]
    xspec = pl.BlockSpec((1, s, tc), lambda i, j: (i, 0, j))
    return pl.pallas_call(
        functools.partial(_short_conv_kernel, s=s),
        grid=(b, c // tc),
        in_specs=[xspec] * 3 + [pl.BlockSpec((3, tc), lambda i, j: (0, j))] * 3 + [pl.BlockSpec((1, tc), lambda i, j: (0, j))] * 3,
        out_specs=[xspec] * 3,
        out_shape=[jax.ShapeDtypeStruct((b, s, c), F32)] * 3,
        compiler_params=_params('parallel', 'parallel'),
        name='short_conv',
    )(*parts, *ws, *bs)


def _dft_cos_sin(rows, cols, period):
    ang = 2.0 * np.pi * ((np.arange(rows)[:, None] * np.arange(cols)[None, :]) % period) / period
    return np.cos(ang), np.sin(ang)


def _fft_tables(n, inner):
    big = 2 * n
    n1 = big // inner
    c1, s1 = _dft_cos_sin(n1, n1, n1)
    h = n1 // 2
    outer_data = np.block([[c1[:, :h], s1[:, :h]], [-s1[:, :h], c1[:, :h]]])
    outer_real = np.concatenate([c1, -s1], axis=0)
    outer_inv = np.block([[c1[:h, :], -s1[:h, :]], [s1[:h, :], c1[:h, :]]]) / big
    c2, s2 = _dft_cos_sin(inner, inner, inner)
    inner_fwd = np.block([[c2, s2], [-s2, c2]])
    inner_inv = np.block([[c2, -s2], [s2, c2]])
    ct, st = _dft_cos_sin(n1, inner, big)
    f = lambda a: jnp.asarray(a, F32)
    return dict(n1=n1, inner=inner, outer_data=f(outer_data), outer_real=f(outer_real), outer_inv=f(outer_inv),
                inner_fwd=_hi_lo_cols(inner_fwd), inner_inv=_hi_lo_cols(inner_inv),
                tw_cos=f(ct).reshape(n1, inner, 1), tw_sin=f(st).reshape(n1, inner, 1))


def _left_mm_kernel(m_ref, x_ref, o_ref):
    o_ref[0] = jnp.dot(m_ref[...], x_ref[0], precision=HIGHEST, preferred_element_type=F32)


def _left_mm(m, x, tl=4096):
    p, k, l = x.shape
    mm = m.shape[0]
    tl = min(tl, l)
    return pl.pallas_call(
        _left_mm_kernel,
        grid=(p, l // tl),
        in_specs=[_const_spec(m.shape), pl.BlockSpec((1, k, tl), lambda i, j: (i, 0, j))],
        out_specs=pl.BlockSpec((1, mm, tl), lambda i, j: (i, 0, j)),
        out_shape=jax.ShapeDtypeStruct((p, mm, l), F32),
        compiler_params=_params('parallel', 'parallel'),
        name='fft_outer',
    )(m, x)


def _hi_lo_cols(m):
    m = np.asarray(m, np.float32)
    hi = m.astype(BF16)
    lo = (m - hi.astype(np.float32)).astype(BF16)
    return jnp.asarray(np.concatenate([hi, hi, lo], axis=1))


def _hi_lo_rows(x):
    hi = x.astype(BF16)
    lo = (x - hi.astype(F32)).astype(BF16)
    return jnp.concatenate([hi, lo, hi], axis=0)


def _inner_kernel(a_ref, twc_ref, tws_ref, gf_ref, *rest, convolve, inner, kb):
    for s in range(kb):
        ar, ai = a_ref[0, 0, s], a_ref[0, 1, s]
        tc, ts = twc_ref[s], tws_ref[s]
        br = ar * tc + ai * ts
        bi = ai * tc - ar * ts
        x = jnp.dot(gf_ref[...], _hi_lo_rows(jnp.concatenate([br, bi], axis=0)), preferred_element_type=F32)
        if not convolve:
            o_ref = rest[0]
            o_ref[0, 0, s] = x[:inner]
            o_ref[0, 1, s] = x[inner:]
            continue
        h_ref, gi_ref, o_ref = rest
        xr, xi = x[:inner], x[inner:]
        hr, hi = h_ref[0, 0, s], h_ref[0, 1, s]
        yr = xr * hr - xi * hi
        yi = xr * hi + xi * hr
        z = jnp.dot(gi_ref[...], _hi_lo_rows(jnp.concatenate([yr, yi], axis=0)), preferred_element_type=F32)
        zr, zi = z[:inner], z[inner:]
        o_ref[0, 0, s] = zr * tc - zi * ts
        o_ref[0, 1, s] = zi * tc + zr * ts


def _fft_inner(a, tab, c, h=None, h_block=0):
    p = a.shape[0]
    n1, inner = tab['n1'], tab['inner']
    a5 = a.reshape(p, 2, n1, inner, c)
    tc = 2 * LANES
    kb = 4
    blk = pl.BlockSpec((1, 2, kb, inner, tc), lambda i, k, j: (i, 0, k, 0, j))
    tw_spec = pl.BlockSpec((kb, inner, 1), lambda i, k, j: (k, 0, 0))
    ins = [a5, tab['tw_cos'], tab['tw_sin'], tab['inner_fwd']]
    in_specs = [blk, tw_spec, tw_spec, _const_spec(tab['inner_fwd'].shape)]
    if h is not None:
        ch = h.shape[-1] // inner
        nb = c // tc
        ins += [h.reshape(1, 2, n1, inner, ch), tab['inner_inv']]
        in_specs += [pl.BlockSpec((1, 2, kb, inner, tc), lambda i, k, j: (0, 0, k, 0, h_block * nb + j)),
                     _const_spec(tab['inner_inv'].shape)]
    out = pl.pallas_call(
        functools.partial(_inner_kernel, convolve=h is not None, inner=inner, kb=kb),
        grid=(p, n1 // kb, c // tc),
        in_specs=in_specs,
        out_specs=blk,
        out_shape=jax.ShapeDtypeStruct(a5.shape, F32),
        compiler_params=_params('parallel', 'parallel', 'parallel'),
        name='fft_inner',
    )(*ins)
    return out.reshape(p, 2 * n1, inner * c)


def _gate_kernel(m_ref, z_ref, u_ref, x_ref, bias_ref, *rest, chain):
    y = jnp.dot(m_ref[...], z_ref[0], precision=HIGHEST, preferred_element_type=F32)
    nxt = x_ref[0] * (y + u_ref[0] * bias_ref[...])
    if chain:
        mf_ref, o_ref, a_ref = rest
        o_ref[0] = nxt
        a_ref[0] = jnp.dot(mf_ref[...], nxt, precision=HIGHEST, preferred_element_type=F32)
    else:
        rest[0][0] = nxt


def _fft_gate(tab, z, u, x, bias_l, chain, tl=4096):
    p, k2, l = z.shape
    n1 = tab['n1']
    tl = min(tl, l)
    row = pl.BlockSpec((1, n1, tl), lambda i, j: (i, 0, j))
    ins = [tab['outer_inv'], z, u, x, bias_l]
    in_specs = [_const_spec((n1, k2)), pl.BlockSpec((1, k2, tl), lambda i, j: (i, 0, j)), row, row,
                pl.BlockSpec((1, tl), lambda i, j: (0, j))]
    out_specs = [row]
    out_shape = [jax.ShapeDtypeStruct((p, n1, l), F32)]
    if chain:
        ins.append(tab['outer_data'])
        in_specs.append(_const_spec((k2, n1)))
        out_specs.append(pl.BlockSpec((1, k2, tl), lambda i, j: (i, 0, j)))
        out_shape.append(jax.ShapeDtypeStruct((p, k2, l), F32))
    return pl.pallas_call(
        functools.partial(_gate_kernel, chain=chain),
        grid=(p, l // tl),
        in_specs=in_specs,
        out_specs=out_specs,
        out_shape=out_shape,
        compiler_params=_params('parallel', 'parallel'),
        name='fft_gate',
    )(*ins)


def _hyena(parts, conv_w, conv_b, filt_n, bias, inner):
    b, s, c = parts[0].shape
    tab = _fft_tables(s, inner)
    n1 = tab['n1']
    lanes = inner * c
    h_taps = _two_sided(filt_n).reshape(1, n1, inner * HY_ORDER * c)
    h_spec = _fft_inner(_left_mm(tab['outer_real'], h_taps), tab, HY_ORDER * c)
    v, x1, x2 = [a.reshape(b // 2, n1, lanes) for a in _short_conv(parts, conv_w, conv_b)]
    bias_l = [jnp.tile(bias[o], inner).reshape(1, lanes) for o in range(HY_ORDER)]
    a = _left_mm(tab['outer_data'], v)
    z = _fft_inner(a, tab, c, h_spec, 0)
    z2, a = _fft_gate(tab, z, v, x1, bias_l[0], chain=True)
    z = _fft_inner(a, tab, c, h_spec, 1)
    (z3,) = _fft_gate(tab, z, z2, x2, bias_l[1], chain=False)
    return z3.reshape(b, s, c)


def _attn_kernel(*refs, n_q, n_pieces, ncomp, scale, post_scale):
    q_refs = refs[:n_q]
    pos = n_q
    pieces = []
    for _ in range(n_pieces):
        pieces.append((refs[pos:pos + n_q], refs[pos + n_q]))
        pos += n_q + 1
    if ncomp == 2:
        lam_ref, g_ref = refs[pos:pos + 2]
        pos += 2
    o_ref = refs[pos]
    outs = []
    for c in range(ncomp):
        qs = [(q_ref[0, c] * (scale * math.log2(math.e))).astype(BF16) for q_ref in q_refs]
        scores = []
        for k_refs, _ in pieces:
            s = None
            for q, k_ref in zip(qs, k_refs):
                t = _dot_nt(q, k_ref[0, c if k_ref.shape[1] == ncomp else 0])
                s = t if s is None else s + t
            scores.append(s)
        m = None
        for s in scores:
            mp = jnp.max(s, axis=-1, keepdims=True)
            m = mp if m is None else jnp.maximum(m, mp)
        l, o = None, None
        for s, (_, v_ref) in zip(scores, pieces):
            p = jnp.exp2(s - m)
            lp = jnp.sum(p, axis=-1, keepdims=True)
            op = jnp.dot(p.astype(BF16), v_ref[0, 0], preferred_element_type=F32)
            l = lp if l is None else l + lp
            o = op if o is None else o + op
        outs.append(o / l)
    if ncomp == 2:
        o = outs[0] - lam_ref[0] * outs[1]
        o = _rms(o) * g_ref[...] * post_scale
    else:
        o = outs[0]
    o_ref[0, 0] = o


def _attention(q_parts, pieces, heads, ncomp, scale, tq=256, lam=None, subln_g=None, post_scale=1.0):
    b, _, sq, _ = q_parts[0].shape
    dv = pieces[0][1].shape[3]
    tq = min(tq, sq)
    ins = list(q_parts)
    in_specs = [pl.BlockSpec((1, ncomp, tq, q.shape[3]), lambda i, h, j: (i, h, j, 0)) for q in q_parts]
    for k_parts, v in pieces:
        for k in k_parts:
            ins.append(k)
            if k.shape[1] == 1:
                in_specs.append(pl.BlockSpec((1, 1) + k.shape[2:], lambda i, h, j: (i, 0, 0, 0)))
            else:
                in_specs.append(pl.BlockSpec((1, ncomp) + k.shape[2:], lambda i, h, j: (i, h, 0, 0)))
        ins.append(v)
        in_specs.append(pl.BlockSpec((1, 1) + v.shape[2:], lambda i, h, j: (i, h, 0, 0)))
    if ncomp == 2:
        ins += [lam.reshape(1), subln_g.reshape(1, dv)]
        in_specs += [pl.BlockSpec(memory_space=pltpu.SMEM), _const_spec((1, dv))]
    return pl.pallas_call(
        functools.partial(_attn_kernel, n_q=len(q_parts), n_pieces=len(pieces), ncomp=ncomp, scale=scale,
                          post_scale=post_scale),
        grid=(b, heads, sq // tq),
        in_specs=in_specs,
        out_specs=pl.BlockSpec((1, 1, tq, dv), lambda i, h, j: (i, h, j, 0)),
        out_shape=jax.ShapeDtypeStruct((b, heads, sq, dv), F32),
        compiler_params=_params('parallel', 'parallel', 'parallel'),
        name='attention',
    )(*ins)


def _forget_terms(f, log_lb, log_1m_lb, one_m_lb):
    log_sig = jnp.minimum(f, 0.0) - jnp.log1p(jnp.exp(-jnp.abs(f)))
    b = log_1m_lb + log_sig
    log_g = jnp.maximum(log_lb, b) + jnp.log1p(jnp.exp(-jnp.abs(log_lb - b)))
    return log_g, one_m_lb * jax.nn.sigmoid(-f)


def _hg_tables():
    ck, sub = HG_CHUNK, HG_SUB
    t = np.arange(ck)
    cum_mats, half_masks, group_masks, sels = [], [], [], []
    for rev in (False, True):
        mats = [(t[None, :] >= t[:, None]) if rev else (t[None, :] <= t[:, None])]
        halves = []
        hs = ck // 2
        while hs >= sub:
            pos = t % (2 * hs)
            b = t - pos + hs
            mats.append((t[None, :] >= b[:, None]) if rev else (t[None, :] < b[:, None]))
            q_half = (pos < hs) if rev else (pos >= hs)
            halves.append(np.stack([q_half, ~q_half]))
            if not rev:
                grp = (t[:, None] // (2 * hs)) == (t[None, :] // (2 * hs))
                group_masks.append(np.concatenate([grp, grp], axis=0))
            hs //= 2
        cum_mats.append(np.concatenate(mats, axis=0))
        half_masks.append(np.stack(halves))
        r, c = np.arange(ck)[:, None], np.arange(ck * sub)[None, :]
        same = (c // sub) == r
        tt, ss = (c // sub) % sub, c % sub
        sels.append(same & ((ss >= tt) if rev else (ss <= tt)))
    lanes = 2 * HG_KEY_DIM
    ln = np.arange(lanes)
    bd = (ln[:, None] // HG_KEY_DIM) == (ln[None, :] // HG_KEY_DIM)
    hm = np.broadcast_to(np.stack(half_masks)[..., None], (2, len(half_masks[0]), 2, ck, lanes))
    return (jnp.asarray(np.stack(cum_mats), BF16), jnp.asarray(hm, F32), jnp.asarray(np.stack(group_masks), F32),
            jnp.asarray(np.stack(sels), BF16), jnp.asarray(bd, F32))


def _split3(x):
    a = x.astype(BF16)
    r = x - a.astype(F32)
    b = r.astype(BF16)
    return a, b, (r - b.astype(F32)).astype(BF16)


def _hg_chunk(q, k, v, lg, st, rev, cm, hm, gm, sel, bd, m0, m1):
    ck, sub = HG_CHUNK, HG_SUB
    call = sum(jnp.dot(cm, piece, preferred_element_type=F32) for piece in _split3(lg))
    cum = call[0:ck]
    tot = cum[0:1] if rev else cum[ck - 1:ck]
    o = _dot_nt((q * jnp.exp(cum)).astype(BF16), st.astype(BF16))
    kd = (k * jnp.exp(tot - cum)).astype(BF16)
    st_new = st * jnp.exp(tot) + bd * jnp.dot(v.T.astype(BF16), kd, preferred_element_type=F32)
    s2 = None
    for lv in range(gm.shape[0]):
        cb = call[(lv + 1) * ck:(lv + 2) * ck]
        qd = q * jnp.exp(jnp.minimum(cum - cb, 0.0)) * hm[lv, 0]
        kf = (k * jnp.exp(jnp.minimum(cb - cum, 0.0)) * hm[lv, 1]).astype(BF16)
        q2 = jnp.concatenate([qd * m0, qd * m1], axis=0).astype(BF16)
        term = _dot_nt(q2, kf) * gm[lv]
        s2 = term if s2 is None else s2 + term
    r = jnp.dot(s2.astype(BF16), v.astype(BF16), preferred_element_type=F32)
    o = o + m0 * r[:ck] + m1 * r[ck:]
    rows, vts = [], []
    for i in range(ck // sub):
        lo, hi = i * sub, (i + 1) * sub
        ki, ci = k[lo:hi], cum[lo:hi]
        for t in range(lo, hi):
            rows.append((q[t:t + 1] * ki * jnp.exp(jnp.minimum(cum[t:t + 1] - ci, 0.0))).astype(BF16))
            vts.append(v[lo:hi])
    sc = jnp.dot(jnp.concatenate(rows, axis=0), bd.astype(BF16), preferred_element_type=F32)
    o = o + jnp.dot(sel, (sc * jnp.concatenate(vts, axis=0)).astype(BF16), preferred_element_type=F32)
    return o, st_new


def _hgrn_kernel(q_ref, ff_ref, fb_ref, i_ref, qc_ref, ffc_ref, fbc_ref, ic_ref, lb_ref, g_ref,
                 cm_ref, hm_ref, gm_ref, sel_ref, bd_ref, o_ref, oc_ref, or_ref, ocr_ref, st_ref, *, n_lat, n_ctx):
    ck = HG_CHUNK
    lanes = o_ref.shape[-1]
    lane = lax.broadcasted_iota(jnp.int32, (1, lanes), 1)
    m0 = (lane < HG_KEY_DIM).astype(F32)
    m1 = 1.0 - m0
    bd = bd_ref[...]
    gm = gm_ref[...]

    def one(q, f, v, rev):
        d = 1 if rev else 0
        lg, k = _forget_terms(f, lb_ref[d, 0:1, :], lb_ref[d, 1:2, :], lb_ref[d, 2:3, :])
        o, st = _hg_chunk(q, k, v, lg, st_ref[d], rev, cm_ref[d], hm_ref[d], gm, sel_ref[d], bd, m0, m1)
        st_ref[d] = st
        return o

    def sweep(qr, ffr, fbr, ir, out_f, out_r, n):
        nc = n // ck

        def body(step, carry):
            idf = pl.ds(pl.multiple_of(step * ck, ck), ck)
            idr = pl.ds(pl.multiple_of((nc - 1 - step) * ck, ck), ck)
            out_f[0, idf, :] = one(qr[0, idf, :], ffr[0, idf, :], ir[0, idf, :], False)
            out_r[idr, :] = one(qr[0, idr, :], fbr[0, idr, :], ir[0, idr, :], True)
            return carry

        lax.fori_loop(0, nc, body, 0)

    st_ref[...] = jnp.zeros(st_ref.shape, F32)
    sweep(qc_ref, ffc_ref, fbc_ref, ic_ref, oc_ref, ocr_ref, n_ctx)
    sweep(q_ref, ff_ref, fb_ref, i_ref, o_ref, or_ref, n_lat)

    mean_mat = bd * (1.0 / HG_VAL_DIM)

    def readout(out, out_r, n):
        tile = min(n, 512)

        def body(step, carry):
            idx = pl.ds(pl.multiple_of(step * tile, tile), tile)
            x = out[0, idx, :] + out_r[idx, :]
            ms = jnp.dot(x * x, mean_mat, precision=HIGHEST, preferred_element_type=F32)
            out[0, idx, :] = x * lax.rsqrt(ms + NORM_EPS) * g_ref[...]
            return carry

        lax.fori_loop(0, n // tile, body, 0)

    readout(oc_ref, ocr_ref, n_ctx)
    readout(o_ref, or_ref, n_lat)


def _hgrn(q, ff, fb, iv, qc, ffc, fbc, ic, lb_terms, norm_g):
    b, n_lat, width = q.shape
    n_ctx = qc.shape[1]
    lanes = 2 * HG_KEY_DIM
    tables = _hg_tables()
    lat = pl.BlockSpec((1, n_lat, lanes), lambda i, j: (i, 0, j))
    ctx = pl.BlockSpec((1, n_ctx, lanes), lambda i, j: (i, 0, j))
    g2 = jnp.tile(norm_g, 2).reshape(1, lanes)
    return pl.pallas_call(
        functools.partial(_hgrn_kernel, n_lat=n_lat, n_ctx=n_ctx),
        grid=(b, width // lanes),
        in_specs=[lat] * 4 + [ctx] * 4 + [pl.BlockSpec((2, 3, lanes), lambda i, j: (0, 0, j)), _const_spec((1, lanes))]
                 + [_const_spec(t.shape) for t in tables],
        out_specs=[lat, ctx],
        out_shape=[jax.ShapeDtypeStruct(q.shape, F32), jax.ShapeDtypeStruct(qc.shape, F32)],
        scratch_shapes=[pltpu.VMEM((n_lat, lanes), F32), pltpu.VMEM((n_ctx, lanes), F32), pltpu.VMEM((2, lanes, lanes), F32)],
        compiler_params=_params('parallel', 'parallel'),
        name='hgrn2',
    )(q, ff, fb, iv, qc, ffc, fbc, ic, lb_terms, g2, *tables)


def _out_proj_kernel(hy_ref, da_ref, mla_ref, hg_ref, gate_ref, x_ref, g1_ref, w_ref, o_ref):
    c = hy_ref.shape[2]
    acc = jnp.dot(hy_ref[0].astype(BF16), w_ref[0:c, :], preferred_element_type=F32)
    for i, head_ref in ((1, da_ref), (2, mla_ref)):
        dv = head_ref.shape[3]
        for h in range(head_ref.shape[1]):
            lo = i * c + h * dv
            acc = acc + jnp.dot(head_ref[0, h].astype(BF16), w_ref[lo:lo + dv, :], preferred_element_type=F32)
    hg = hg_ref[0] * _silu(gate_ref[0])
    acc = acc + jnp.dot(hg.astype(BF16), w_ref[3 * c:4 * c, :], preferred_element_type=F32)
    o_ref[0] = x_ref[0] + g1_ref[0] * acc


def _out_proj(y_hy, y_da, y_mla, y_hg, gate, x, g1, w_out, tm=512):
    b, s, d = x.shape
    tm = min(tm, s)
    c = y_hy.shape[2]
    part = pl.BlockSpec((1, tm, c), lambda i, j: (i, j, 0))
    headed = lambda a: pl.BlockSpec((1, a.shape[1], tm, a.shape[3]), lambda i, j: (i, 0, j, 0))
    row = pl.BlockSpec((1, tm, d), lambda i, j: (i, j, 0))
    return pl.pallas_call(
        _out_proj_kernel,
        grid=(b, s // tm),
        in_specs=[part, headed(y_da), headed(y_mla), part, part, row, pl.BlockSpec((1, 1, d), lambda i, j: (i, 0, 0)),
                  _const_spec(w_out.shape)],
        out_specs=row,
        out_shape=jax.ShapeDtypeStruct(x.shape, F32),
        compiler_params=_params('parallel', 'parallel'),
        name='out_proj',
    )(y_hy, y_da, y_mla, y_hg, gate, x, g1, w_out)


def _router_kernel(x_ref, g_ref, sc_ref, sh_ref, wrt_ref, bias_ref, *rest, compact):
    h = _rms(x_ref[0]) * g_ref[...] * (1.0 + sc_ref[0]) + sh_ref[0]
    tm = h.shape[0]
    scores = jax.nn.sigmoid(_dot_nt(wrt_ref[...], h, precision=HIGHEST))
    choice = scores + bias_ref[...]
    per = N_EXPERTS // N_EXPERT_GROUPS
    neg = -jnp.inf
    iota_g = lax.broadcasted_iota(jnp.int32, (per, tm), 0)
    grp_rows = []
    for gi in range(N_EXPERT_GROUPS):
        blk = choice[gi * per:(gi + 1) * per]
        m1 = jnp.max(blk, axis=0, keepdims=True)
        first = jnp.min(jnp.where(blk == m1, iota_g, per), axis=0, keepdims=True)
        m2 = jnp.max(jnp.where(iota_g == first, neg, blk), axis=0, keepdims=True)
        grp_rows.append(m1 + m2)
    grp = jnp.concatenate(grp_rows, axis=0)
    iota_n = lax.broadcasted_iota(jnp.int32, (N_EXPERT_GROUPS, tm), 0)
    gsel = jnp.zeros((N_EXPERT_GROUPS, tm), F32)
    for _ in range(TOPK_GROUPS):
        m = jnp.max(grp, axis=0, keepdims=True)
        first = jnp.min(jnp.where(grp == m, iota_n, N_EXPERT_GROUPS), axis=0, keepdims=True)
        hit = iota_n == first
        gsel = jnp.where(hit, 1.0, gsel)
        grp = jnp.where(hit, neg, grp)
    emask = jnp.concatenate([jnp.broadcast_to(gsel[gi:gi + 1], (per, tm)) for gi in range(N_EXPERT_GROUPS)], axis=0)
    cand = jnp.where(emask > 0.0, choice, neg)
    iota_e = lax.broadcasted_iota(jnp.int32, (N_EXPERTS, tm), 0)
    sel = jnp.zeros((N_EXPERTS, tm), F32)
    chosen = []
    for _ in range(TOP_K):
        m = jnp.max(cand, axis=0, keepdims=True)
        first = jnp.min(jnp.where(cand == m, iota_e, N_EXPERTS), axis=0, keepdims=True)
        hit = iota_e == first
        sel = jnp.where(hit, 1.0, sel)
        cand = jnp.where(hit, neg, cand)
        chosen.append(first)
    w = scores * sel
    gate = w / jnp.sum(w, axis=0, keepdims=True) * ROUTED_SCALE
    if not compact:
        h_ref, gate_ref = rest
        h_ref[0] = h.astype(BF16)
        gate_ref[0] = gate
        return
    hp_ref, eid_ref, rank_ref, w_ref, cnt_out_ref, cnt_ref = rest
    hp_ref[0] = _pack_halves(h)

    @pl.when((pl.program_id(0) == 0) & (pl.program_id(1) == 0))
    def _():
        cnt_ref[...] = jnp.zeros(cnt_ref.shape, F32)

    src = lax.broadcasted_iota(jnp.int32, (tm, tm), 0)
    dst = lax.broadcasted_iota(jnp.int32, (tm, tm), 1)
    running = jnp.dot(sel.astype(BF16), (src <= dst).astype(BF16), preferred_element_type=F32)
    rank_dense = cnt_ref[:, 0:1] + running - 1.0
    e_rows, r_rows, w_rows = [], [], []
    for first in chosen:
        hit = iota_e == first
        e_rows.append(first)
        r_rows.append(jnp.sum(jnp.where(hit, rank_dense, 0.0), axis=0, keepdims=True))
        w_rows.append(jnp.sum(jnp.where(hit, gate, 0.0), axis=0, keepdims=True))
    eid_ref[...] = jnp.concatenate(e_rows, axis=0)
    rank_ref[...] = jnp.concatenate(r_rows, axis=0).astype(jnp.int32)
    w_ref[...] = jnp.concatenate(w_rows, axis=0)
    cnt_ref[...] = cnt_ref[...] + running[:, tm - 1:tm]
    cnt_out_ref[...] = cnt_ref[...]


def _router(x, g, scale, shift, w_router, e_bias, tm=512, compact=False):
    b, s, d = x.shape
    tm = min(tm, s)
    e = w_router.shape[1]
    row = pl.BlockSpec((1, tm, d), lambda i, j: (i, j, 0))
    mod = pl.BlockSpec((1, 1, d), lambda i, j: (i, 0, 0))
    if compact:
        nj = s // tm
        tok = pl.BlockSpec((TOP_K, tm), lambda i, j: (0, i * nj + j))
        out_specs = [pl.BlockSpec((1, tm, d // 2), lambda i, j: (i, j, 0)), tok, tok, tok, _const_spec((e, LANES))]
        out_shape = [jax.ShapeDtypeStruct((b, s, d // 2), jnp.int32), jax.ShapeDtypeStruct((TOP_K, b * s), jnp.int32),
                     jax.ShapeDtypeStruct((TOP_K, b * s), jnp.int32), jax.ShapeDtypeStruct((TOP_K, b * s), F32),
                     jax.ShapeDtypeStruct((e, LANES), F32)]
        scratch = [pltpu.VMEM((e, LANES), F32)]
        semantics = ('arbitrary', 'arbitrary')
    else:
        out_specs = [row, pl.BlockSpec((1, e, tm), lambda i, j: (i, 0, j))]
        out_shape = [jax.ShapeDtypeStruct((b, s, d), BF16), jax.ShapeDtypeStruct((b, e, s), F32)]
        scratch = []
        semantics = ('parallel', 'parallel')
    return pl.pallas_call(
        functools.partial(_router_kernel, compact=compact),
        grid=(b, s // tm),
        in_specs=[row, _const_spec((1, d)), mod, mod, _const_spec((e, d)), _const_spec((e, 1))],
        out_specs=out_specs,
        out_shape=out_shape,
        scratch_shapes=scratch,
        compiler_params=_params(*semantics),
        name='router',
    )(x, g.reshape(1, d), scale, shift, w_router.T, e_bias.reshape(e, 1))


def _moe_kernel(h_ref, x_ref, gate_ref, g2_ref, wg_ref, wu_ref, wd_ref, sg_ref, su_ref, sd_ref, *rest, final):
    if final:
        fg_ref, o_ref, acc_ref = rest
    else:
        o_ref, acc_ref = rest
    e = pl.program_id(2)
    h = h_ref[0]

    @pl.when(e == 0)
    def _():
        a = jnp.dot(h, sg_ref[...], preferred_element_type=F32)
        u = jnp.dot(h, su_ref[...], preferred_element_type=F32)
        acc_ref[...] = jnp.dot((_silu(a) * u).astype(BF16), sd_ref[...], preferred_element_type=F32)

    lane = lax.broadcasted_iota(jnp.int32, gate_ref.shape[1:], 1)
    gcol = jnp.sum(jnp.where(lane == e, gate_ref[0], 0.0), axis=-1, keepdims=True)
    a = jnp.dot(h, wg_ref[0].astype(BF16), preferred_element_type=F32)
    u = jnp.dot(h, wu_ref[0].astype(BF16), preferred_element_type=F32)
    acc_ref[...] += jnp.dot((_silu(a) * u * gcol).astype(BF16), wd_ref[0].astype(BF16), preferred_element_type=F32)

    @pl.when(e == pl.num_programs(2) - 1)
    def _():
        y = x_ref[0] + g2_ref[0] * acc_ref[...]
        if final:
            y = _rms(y) * fg_ref[...]
        o_ref[0] = y


def _moe(h2, x, gate, g2, layer, w_gate, w_up, w_down, s_gate, s_up, s_down, final_g=None, tm=1024):
    b, s, d = x.shape
    tm = min(tm, s)
    _, e, _, ff = w_gate.shape
    row = pl.BlockSpec((1, tm, d), lambda i, j, k: (i, j, 0))
    ins = [h2, x, gate, g2, w_gate, w_up, w_down, s_gate, s_up, s_down]
    in_specs = [row, row, pl.BlockSpec((1, tm, e), lambda i, j, k: (i, j, 0)),
                pl.BlockSpec((1, 1, d), lambda i, j, k: (i, 0, 0)),
                pl.BlockSpec((None, 1, d, ff), lambda i, j, k: (layer, k, 0, 0)),
                pl.BlockSpec((None, 1, d, ff), lambda i, j, k: (layer, k, 0, 0)),
                pl.BlockSpec((None, 1, ff, d), lambda i, j, k: (layer, k, 0, 0)),
                _const_spec(s_gate.shape), _const_spec(s_up.shape), _const_spec(s_down.shape)]
    if final_g is not None:
        ins.append(final_g.reshape(1, d))
        in_specs.append(_const_spec((1, d)))
    return pl.pallas_call(
        functools.partial(_moe_kernel, final=final_g is not None),
        grid=(b, s // tm, e),
        in_specs=in_specs,
        out_specs=row,
        out_shape=jax.ShapeDtypeStruct(x.shape, F32),
        scratch_shapes=[pltpu.VMEM((tm, d), F32)],
        compiler_params=_params('parallel', 'parallel', 'arbitrary'),
        name='moe',
    )(*ins)


MOE_ROW_TILE = 256
SC_ROWS = 64
V7X_SC_CORES = 2
V7X_SC_SUBCORES = 16


def _pack_halves(x):
    n = x.shape[1] // 2
    lo = pltpu.bitcast(x[:, :n].astype(BF16).astype(F32), jnp.int32)
    hi = pltpu.bitcast(x[:, n:].astype(BF16).astype(F32), jnp.int32)
    return jnp.bitwise_or(jnp.bitwise_and(hi, -65536), lax.shift_right_logical(lo, 16))


def _unpack_halves(p):
    lo = pltpu.bitcast(lax.shift_left(p, 16), F32).astype(BF16)
    hi = pltpu.bitcast(jnp.bitwise_and(p, -65536), F32).astype(BF16)
    return lo, hi


def _route_pos_kernel(off_ref, eid_ref, rank_ref, pos_ref):
    eid = eid_ref[...]
    base = jnp.zeros(eid.shape, jnp.int32)
    for e in range(N_EXPERTS):
        base = jnp.where(eid == e, off_ref[e], base)
    pos_ref[...] = base + rank_ref[...]


def _route_pos(offsets, eid, rank):
    return pl.pallas_call(
        _route_pos_kernel,
        grid=(1,),
        in_specs=[pl.BlockSpec(memory_space=pltpu.SMEM), _const_spec(eid.shape), _const_spec(rank.shape)],
        out_specs=_const_spec(eid.shape),
        out_shape=jax.ShapeDtypeStruct(eid.shape, jnp.int32),
        compiler_params=_params('arbitrary'),
        name='route_pos',
    )(offsets, eid, rank)


def _sc_mesh():
    return plsc.VectorSubcoreMesh(core_axis_name='c', subcore_axis_name='s', num_cores=V7X_SC_CORES,
                                  num_subcores=V7X_SC_SUBCORES)


def _sc_dispatch(hp, pos, n_rows):
    t, w = hp.shape
    k = pos.shape[0]
    workers = V7X_SC_CORES * V7X_SC_SUBCORES
    per_worker = t // workers
    pos_flat = pos.reshape(k * t)

    @functools.partial(pl.kernel, mesh=_sc_mesh(), out_type=jax.ShapeDtypeStruct((n_rows, w), jnp.int32),
                       scratch_types=[pltpu.VMEM((SC_ROWS,), jnp.int32), pltpu.VMEM((SC_ROWS, w), jnp.int32),
                                      pltpu.SemaphoreType.DMA])
    def scatter(hp_hbm, pos_hbm, out_hbm, idx_v, rows_v, sem):
        wid = lax.axis_index('s') * V7X_SC_CORES + lax.axis_index('c')

        @pl.loop(0, per_worker // SC_ROWS)
        def _(i):
            t0 = pl.multiple_of(wid * per_worker + i * SC_ROWS, SC_ROWS)
            pltpu.sync_copy(hp_hbm.at[pl.ds(t0, SC_ROWS)], rows_v)
            for j in range(k):
                pltpu.sync_copy(pos_hbm.at[pl.ds(pl.multiple_of(j * t + t0, SC_ROWS), SC_ROWS)], idx_v)
                pltpu.async_copy(rows_v, out_hbm.at[idx_v], sem).wait()

    return scatter(hp, pos_flat)


def _sc_collect(yp, pos):
    _, w = yp.shape
    k, t = pos.shape
    workers = V7X_SC_CORES * V7X_SC_SUBCORES
    per_worker = k * t // workers
    pos_flat = pos.reshape(k * t)

    @functools.partial(pl.kernel, mesh=_sc_mesh(), out_type=jax.ShapeDtypeStruct((k * t, w), jnp.int32),
                       scratch_types=[pltpu.VMEM((SC_ROWS,), jnp.int32), pltpu.VMEM((SC_ROWS, w), jnp.int32),
                                      pltpu.SemaphoreType.DMA])
    def gather(yp_hbm, pos_hbm, out_hbm, idx_v, rows_v, sem):
        wid = lax.axis_index('s') * V7X_SC_CORES + lax.axis_index('c')

        @pl.loop(0, per_worker // SC_ROWS)
        def _(i):
            r0 = pl.multiple_of(wid * per_worker + i * SC_ROWS, SC_ROWS)
            pltpu.sync_copy(pos_hbm.at[pl.ds(r0, SC_ROWS)], idx_v)
            pltpu.async_copy(yp_hbm.at[idx_v], rows_v, sem).wait()
            pltpu.sync_copy(rows_v, out_hbm.at[pl.ds(r0, SC_ROWS)])

    return gather(yp, pos_flat)


def _expert_kernel(te_ref, nu_ref, x_ref, wg_ref, wu_ref, wd_ref, o_ref, wg_s, wu_s, wd_s):
    i = pl.program_id(0)

    @pl.when(i < nu_ref[0])
    def _():
        @pl.when((i == 0) | (te_ref[i] != te_ref[jnp.maximum(i - 1, 0)]))
        def _():
            wg_s[...] = wg_ref[0].astype(BF16)
            wu_s[...] = wu_ref[0].astype(BF16)
            wd_s[...] = wd_ref[0].astype(BF16)

        lo, hi = _unpack_halves(x_ref[...])
        half = lo.shape[1]
        a = (jnp.dot(lo, wg_s[:half, :], preferred_element_type=F32)
             + jnp.dot(hi, wg_s[half:, :], preferred_element_type=F32))
        u = (jnp.dot(lo, wu_s[:half, :], preferred_element_type=F32)
             + jnp.dot(hi, wu_s[half:, :], preferred_element_type=F32))
        y = jnp.dot((_silu(a) * u).astype(BF16), wd_s[...], preferred_element_type=F32)
        o_ref[...] = _pack_halves(y)


def _experts(xp, tile_expert, n_used, layer, w_gate, w_up, w_down):
    n_rows, half = xp.shape
    _, _, d, ff = w_gate.shape
    r = MOE_ROW_TILE
    row = pl.BlockSpec((r, half), lambda i, te, nu: (i, 0))
    grid_spec = pltpu.PrefetchScalarGridSpec(
        num_scalar_prefetch=2,
        grid=(n_rows // r,),
        in_specs=[row,
                  pl.BlockSpec((None, 1, d, ff), lambda i, te, nu: (layer, te[i], 0, 0)),
                  pl.BlockSpec((None, 1, d, ff), lambda i, te, nu: (layer, te[i], 0, 0)),
                  pl.BlockSpec((None, 1, ff, d), lambda i, te, nu: (layer, te[i], 0, 0))],
        out_specs=row,
        scratch_shapes=[pltpu.VMEM((d, ff), BF16), pltpu.VMEM((d, ff), BF16), pltpu.VMEM((ff, d), BF16)],
    )
    return pl.pallas_call(
        _expert_kernel,
        grid_spec=grid_spec,
        out_shape=jax.ShapeDtypeStruct((n_rows, half), jnp.int32),
        compiler_params=_params('arbitrary'),
        name='experts',
    )(tile_expert, n_used, xp, w_gate, w_up, w_down)


def _combine_kernel(yg_ref, w_ref, hp_ref, x_ref, g2_ref, sg_ref, su_ref, sd_ref, *rest, final):
    if final:
        fg_ref, o_ref = rest
    else:
        (o_ref,) = rest
    half = hp_ref.shape[2]
    lo, hi = _unpack_halves(hp_ref[0])
    sg, su = sg_ref[...], su_ref[...]
    a = jnp.dot(lo, sg[:half], preferred_element_type=F32) + jnp.dot(hi, sg[half:], preferred_element_type=F32)
    u = jnp.dot(lo, su[:half], preferred_element_type=F32) + jnp.dot(hi, su[half:], preferred_element_type=F32)
    acc = jnp.dot((_silu(a) * u).astype(BF16), sd_ref[...], preferred_element_type=F32)
    acc_lo, acc_hi = acc[:, :half], acc[:, half:]
    wts = w_ref[0]
    for k in range(yg_ref.shape[0]):
        ylo, yhi = _unpack_halves(yg_ref[k, 0])
        wk = wts[:, k:k + 1]
        acc_lo = acc_lo + wk * ylo.astype(F32)
        acc_hi = acc_hi + wk * yhi.astype(F32)
    y = x_ref[0] + g2_ref[0] * jnp.concatenate([acc_lo, acc_hi], axis=1)
    if final:
        y = _rms(y) * fg_ref[...]
    o_ref[0] = y


def _combine(yg, wts, hp, x, g2, s_gate, s_up, s_down, final_g=None, tm=256):
    b, s, d = x.shape
    k = yg.shape[0]
    half = d // 2
    row = pl.BlockSpec((1, tm, d), lambda i, j: (i, j, 0))
    prow = pl.BlockSpec((1, tm, half), lambda i, j: (i, j, 0))
    ins = [yg, wts, hp, x, g2, s_gate, s_up, s_down]
    in_specs = [pl.BlockSpec((k, 1, tm, half), lambda i, j: (0, i, j, 0)), pl.BlockSpec((1, tm, k), lambda i, j: (i, j, 0)),
                prow, row, pl.BlockSpec((1, 1, d), lambda i, j: (i, 0, 0)),
                _const_spec(s_gate.shape), _const_spec(s_up.shape), _const_spec(s_down.shape)]
    if final_g is not None:
        ins.append(final_g.reshape(1, d))
        in_specs.append(_const_spec((1, d)))
    return pl.pallas_call(
        functools.partial(_combine_kernel, final=final_g is not None),
        grid=(b, s // tm),
        in_specs=in_specs,
        out_specs=row,
        out_shape=jax.ShapeDtypeStruct(x.shape, F32),
        compiler_params=_params('parallel', 'parallel'),
        name='moe_combine',
    )(*ins)


def _routed_moe(x, g, scale, shift, g2, w_router, e_bias, layer, w_gate, w_up, w_down, s_gate, s_up, s_down, final_g=None):
    b, s, d = x.shape
    t = b * s
    hp, eid, rank, wts, counts = _router(x, g, scale, shift, w_router, e_bias, compact=True)
    counts = counts[:, 0].astype(jnp.int32)
    r = MOE_ROW_TILE
    padded = (counts + (r - 1)) // r * r
    ends = jnp.cumsum(padded)
    offsets = ends - padded
    n_rows = t * TOP_K + N_EXPERTS * r
    tile_start = jnp.arange(n_rows // r, dtype=jnp.int32) * r
    tile_expert = jnp.minimum(jnp.sum((tile_start[:, None] >= ends[None, :]).astype(jnp.int32), axis=1), N_EXPERTS - 1)
    n_used = (ends[-1] // r).reshape(1).astype(jnp.int32)
    pos = _route_pos(offsets.astype(jnp.int32), eid, rank)
    xp = _sc_dispatch(hp.reshape(t, d // 2), pos, n_rows)
    yp = _experts(xp, tile_expert.astype(jnp.int32), n_used, layer, w_gate, w_up, w_down)
    yg = _sc_collect(yp, pos).reshape(TOP_K, b, s, d // 2)
    return _combine(yg, wts.T.reshape(b, s, TOP_K), hp, x, g2, s_gate, s_up, s_down, final_g)


def _mixers(p, pc, ctx_out, prm, l, lam_init, rope_tabs, lb_terms):
    s = p['hy_v'].shape[1]
    sc = pc['hy_v'].shape[1]

    hy_args = (prm['hy_w1'][l], prm['hy_b1'][l], prm['hy_w2'][l], prm['hy_b2'][l], prm['hy_w3'][l], prm['hy_b3'][l],
               prm['hy_sin_freq'][l], prm['hy_decay'][l])
    y_hy = _hyena([p['hy_v'], p['hy_x1'], p['hy_x2']], prm['hy_conv_w'][l], prm['hy_conv_b'][l],
                  _hy_filters(s, *hy_args), prm['hy_bias'][l], inner=128)
    yc_hy = None
    if ctx_out:
        yc_hy = _hyena([pc['hy_v'], pc['hy_x1'], pc['hy_x2']], prm['hy_conv_w'][l], prm['hy_conv_b'][l],
                       _hy_filters(sc, *hy_args), prm['hy_bias'][l], inner=32)

    lp = prm['da_lambda'][l].astype(F32)
    lam = jnp.exp(jnp.sum(lp[0] * lp[1])) - jnp.exp(jnp.sum(lp[2] * lp[3])) + lam_init
    da_kw = dict(heads=DA_HEADS, ncomp=2, scale=DA_HEAD_DIM ** -0.5, lam=lam, subln_g=prm['da_subln_g'][l],
                 post_scale=1.0 - lam_init)
    da_ctx = ([pc['da_k']], pc['da_v'])
    y_da = _attention([p['da_q']], [da_ctx, ([p['da_k']], p['da_v'])], **da_kw)
    yc_da = _attention([pc['da_q']], [da_ctx], **da_kw) if ctx_out else None

    wq = prm['mla_w_q_up'][l].reshape(MLA_Q_RANK, MLA_HEADS, MLA_NOPE_DIM + MLA_ROPE_DIM)
    wq_n = wq[:, :, :MLA_NOPE_DIM].reshape(MLA_Q_RANK, -1).astype(BF16)
    wq_r = wq[:, :, MLA_NOPE_DIM:].reshape(MLA_Q_RANK, -1).astype(BF16)
    wkv = prm['mla_w_kv_up'][l].reshape(MLA_KV_RANK, MLA_HEADS, MLA_NOPE_DIM + MLA_V_DIM)
    wkv_n = wkv[:, :, :MLA_NOPE_DIM].reshape(MLA_KV_RANK, -1).astype(BF16)
    wkv_v = wkv[:, :, MLA_NOPE_DIM:].reshape(MLA_KV_RANK, -1).astype(BF16)

    def queries(qd, tabs):
        return _norm_proj(qd, prm['mla_q_norm_g'][l], [(wq_n, F32, False, MLA_HEADS), (wq_r, F32, True, MLA_HEADS)],
                          rope_tabs=tabs)

    def keys_values(kvd):
        return _norm_proj(kvd, prm['mla_kv_norm_g'][l], [(wkv_n, BF16, False, MLA_HEADS), (wkv_v, BF16, False, MLA_HEADS)])

    kn_l, v_l = keys_values(p['mla_kv'])
    kn_c, v_c = keys_values(pc['mla_kv'])
    mla_kw = dict(heads=MLA_HEADS, ncomp=1, scale=(MLA_NOPE_DIM + MLA_ROPE_DIM) ** -0.5)
    mla_ctx = ([kn_c, pc['mla_kr']], v_c)
    y_mla = _attention(queries(p['mla_q'], rope_tabs), [mla_ctx, ([kn_l, p['mla_kr']], v_l)], **mla_kw)
    yc_mla = _attention(queries(pc['mla_q'], None), [mla_ctx], **mla_kw) if ctx_out else None

    o, oc = _hgrn(p['hg_q'], p['hg_ff'], p['hg_fb'], p['hg_i'], pc['hg_q'], pc['hg_ff'], pc['hg_fb'], pc['hg_i'],
                  lb_terms, prm['hg_norm_g'][l])
    return (y_hy, y_da, y_mla, o), (yc_hy, yc_da, yc_mla, oc)


def kernel(x, c, ctx, c_ctx, w_ada, b_ada, norm1_g, norm2_g, w_in, w_out, hy_conv_w, hy_conv_b, hy_w1, hy_b1, hy_w2, hy_b2, hy_w3, hy_b3, hy_sin_freq, hy_decay, hy_bias, da_lambda, da_subln_g, mla_q_norm_g, mla_w_q_up, mla_kv_norm_g, mla_w_kv_up, hg_lower_bounds, hg_norm_g, moe_w_router, moe_bias, moe_w_gate, moe_w_up, moe_w_down, moe_sh_gate, moe_sh_up, moe_sh_down, final_norm_g):
    prm = dict(hy_conv_w=hy_conv_w, hy_conv_b=hy_conv_b, hy_w1=hy_w1, hy_b1=hy_b1, hy_w2=hy_w2, hy_b2=hy_b2,
               hy_w3=hy_w3, hy_b3=hy_b3, hy_sin_freq=hy_sin_freq, hy_decay=hy_decay, hy_bias=hy_bias,
               da_lambda=da_lambda, da_subln_g=da_subln_g, mla_q_norm_g=mla_q_norm_g, mla_w_q_up=mla_w_q_up,
               mla_kv_norm_g=mla_kv_norm_g, mla_w_kv_up=mla_w_kv_up, hg_norm_g=hg_norm_g)
    b, n_lat, d = x.shape
    depth = w_in.shape[0]
    rows = n_lat // GRID_W
    row_pos = jnp.repeat(jnp.arange(rows, dtype=jnp.int32), GRID_W)
    col_pos = jnp.tile(jnp.arange(GRID_W, dtype=jnp.int32), rows)
    rope_tabs = _rope_tables(row_pos, col_pos, 2 * DA_HEADS * DA_HEAD_DIM)
    lbs = jnp.cumsum(jax.nn.softmax(hg_lower_bounds.astype(F32), axis=1), axis=1)
    lbs = lbs - lbs[:, :1]
    cond = jnp.concatenate([c, c_ctx[None], jnp.zeros((8 - b - 1, d), F32)], axis=0)

    for l in range(depth):
        ctx_out = l < depth - 1
        mods = _ada(cond, w_ada[l], b_ada[l])
        sh1, sc1, g1, sh2, sc2, g2 = [m[:, None, :] for m in jnp.split(mods[:b], 6, axis=-1)]
        mc = [jnp.broadcast_to(m[:, None, :], (b, 1, d)) for m in jnp.split(mods[b:b + 1], 6, axis=-1)]

        off = 0
        outs = []
        for _, wdt, dt, rope, split, rep in _SEGMENTS:
            w = w_in[l][:, off:off + wdt].astype(BF16)
            outs.append((jnp.tile(w, (1, rep)) if rep > 1 else w, dt, rope, split))
            off += wdt
        names = [seg[0] for seg in _SEGMENTS]
        p = dict(zip(names, _norm_proj(x, norm1_g[l], outs, sc1, sh1, rope_tabs=rope_tabs)))
        pc = dict(zip(names, _norm_proj(ctx, norm1_g[l], outs, mc[1], mc[0])))

        lb = lbs[:, l]
        lb_terms = jnp.stack([jnp.log(lb), jnp.log1p(-lb), 1.0 - lb], axis=1)
        lam_init = 0.8 - 0.6 * math.exp(-0.3 * l)
        lat_parts, ctx_parts = _mixers(p, pc, ctx_out, prm, l, lam_init, rope_tabs, lb_terms)

        w_out_b = w_out[l].astype(BF16)
        moe_w = (l, moe_w_gate, moe_w_up, moe_w_down,
                 moe_sh_gate[l].astype(BF16), moe_sh_up[l].astype(BF16), moe_sh_down[l].astype(BF16))

        if ctx_out:
            ctx = _out_proj(*ctx_parts, pc['hg_g'], ctx, mc[2], w_out_b)
            flat = ctx.reshape(1, -1, d)
            h2c, gate_c = _router(flat, norm2_g[l], mc[4][:1], mc[3][:1], moe_w_router[l], moe_bias[l])
            ctx = _moe(h2c, flat, gate_c.transpose(0, 2, 1), mc[5][:1], *moe_w).reshape(ctx.shape)

        x = _out_proj(*lat_parts, p['hg_g'], x, g1, w_out_b)
        x = _routed_moe(x, norm2_g[l], sc2, sh2, g2, moe_w_router[l], moe_bias[l], *moe_w,
                        final_g=None if ctx_out else final_norm_g)

    return x
```

```python
import functools
import math

import numpy as np
import jax
import jax.numpy as jnp
from jax import lax
from jax.experimental import pallas as pl
from jax.experimental.pallas import tpu as pltpu
from jax.experimental.pallas import tpu_sc as plsc

F32 = jnp.float32
BF16 = jnp.bfloat16
HIGHEST = lax.Precision.HIGHEST

D_MODEL = 1024
GRID_W = 64
HY_WIDTH = 256
HY_ORDER = 2
HY_BANDS = 16
DA_HEADS = 4
DA_HEAD_DIM = 32
MLA_HEADS = 4
MLA_Q_RANK = 192
MLA_KV_RANK = 128
MLA_NOPE_DIM = 64
MLA_ROPE_DIM = 32
MLA_V_DIM = 64
HG_HEADS = 4
HG_KEY_DIM = 64
HG_VAL_DIM = 64
HG_CHUNK = 64
HG_SUB = 8
N_EXPERTS = 64
N_EXPERT_GROUPS = 8
TOPK_GROUPS = 4
TOP_K = 8
EXPERT_FF = 256
ROUTED_SCALE = 2.5
ROPE_BASE = 10000.0
NORM_EPS = 1e-6

V7X_VMEM_LIMIT_BYTES = 56 * 1024 * 1024
LANES = 128

_SEGMENTS = (
    ('hy_v', HY_WIDTH, F32, False, 0, 1), ('hy_x1', HY_WIDTH, F32, False, 0, 1), ('hy_x2', HY_WIDTH, F32, False, 0, 1),
    ('da_q', 2 * DA_HEADS * DA_HEAD_DIM, F32, True, 2 * DA_HEADS, 1),
    ('da_k', 2 * DA_HEADS * DA_HEAD_DIM, BF16, True, 2 * DA_HEADS, 1),
    ('da_v', 2 * DA_HEADS * DA_HEAD_DIM, BF16, False, DA_HEADS, 1),
    ('mla_q', MLA_Q_RANK, F32, False, 0, 1), ('mla_kv', MLA_KV_RANK, F32, False, 0, 1),
    ('mla_kr', MLA_ROPE_DIM, BF16, True, MLA_HEADS, MLA_HEADS),
    ('hg_q', HG_HEADS * HG_KEY_DIM, F32, False, 0, 1), ('hg_ff', HG_HEADS * HG_KEY_DIM, F32, False, 0, 1),
    ('hg_fb', HG_HEADS * HG_KEY_DIM, F32, False, 0, 1), ('hg_i', HG_HEADS * HG_VAL_DIM, F32, False, 0, 1),
    ('hg_g', HG_HEADS * HG_VAL_DIM, F32, False, 0, 1),
)


def _params(*semantics):
    return pltpu.CompilerParams(dimension_semantics=semantics, vmem_limit_bytes=V7X_VMEM_LIMIT_BYTES)


def _const_spec(shape):
    nd = len(shape)
    return pl.BlockSpec(shape, lambda *_: (0,) * nd)


def _rms(x, eps=NORM_EPS):
    return x * lax.rsqrt(jnp.mean(x * x, axis=-1, keepdims=True) + eps)


def _silu(x):
    return x * jax.nn.sigmoid(x)


def _dot_nt(a, b, **kw):
    return lax.dot_general(a, b, (((1,), (1,)), ((), ())), preferred_element_type=F32, **kw)


def _ada_kernel(c_ref, w_ref, b_ref, o_ref):
    s = _silu(c_ref[...])
    o_ref[...] = jnp.dot(s, w_ref[...], precision=HIGHEST, preferred_element_type=F32) + b_ref[...]


def _ada(cond, w, b):
    r, d = cond.shape
    n = w.shape[1]
    tn = 1536
    return pl.pallas_call(
        _ada_kernel,
        grid=(n // tn,),
        in_specs=[_const_spec((r, d)), pl.BlockSpec((d, tn), lambda j: (0, j)), pl.BlockSpec((1, tn), lambda j: (0, j))],
        out_specs=pl.BlockSpec((r, tn), lambda j: (0, j)),
        out_shape=jax.ShapeDtypeStruct((r, n), F32),
        compiler_params=_params('arbitrary'),
        name='ada',
    )(cond, w, b.reshape(1, n))


ROPE_UNIT = 32


def _rope_tables(row, col, width):
    n = ROPE_UNIT // 4
    inv = ROPE_BASE ** (-jnp.arange(n, dtype=F32) / n)
    units = width // ROPE_UNIT
    parts_c, parts_a, parts_b = [], [], []
    zero = jnp.zeros((row.shape[0], n), F32)
    for pos in (row, col):
        ang = pos.astype(F32)[:, None] * inv
        cos, sin = jnp.cos(ang), jnp.sin(ang)
        parts_c += [cos, cos]
        parts_a += [zero, sin]
        parts_b += [-sin, zero]
    tile = lambda ps: jnp.tile(jnp.concatenate(ps, axis=1), (1, units))
    return tile(parts_c), tile(parts_a), tile(parts_b)


def _norm_proj_kernel(*refs, n_w, modulate, ropes, splits):
    x_ref, g_ref = refs[0], refs[1]
    pos = 2
    if modulate:
        sc_ref, sh_ref = refs[2], refs[3]
        pos = 4
    if any(ropes):
        rc_ref, ra_ref, rb_ref = refs[pos:pos + 3]
        pos += 3
    w_refs = refs[pos:pos + n_w]
    o_refs = refs[pos + n_w:]
    y = _rms(x_ref[0]) * g_ref[...]
    if modulate:
        y = y * (1.0 + sc_ref[0]) + sh_ref[0]
    yb = y.astype(BF16)
    for w_ref, o_ref, rope, split in zip(w_refs, o_refs, ropes, splits):
        o = jnp.dot(yb, w_ref[...], preferred_element_type=F32)
        if rope:
            wd = o.shape[1]
            shift = ROPE_UNIT // 4
            o = (o * rc_ref[:, :wd] + pltpu.roll(o, shift, axis=1) * ra_ref[:, :wd]
                 + pltpu.roll(o, wd - shift, axis=1) * rb_ref[:, :wd])
        if split:
            unit = o.shape[1] // split
            for u in range(split):
                o_ref[0, u] = o[:, u * unit:(u + 1) * unit].astype(o_ref.dtype)
        else:
            o_ref[0] = o.astype(o_ref.dtype)


def _norm_proj(x, g, outs, scale=None, shift=None, rope_tabs=None, tm=512):
    b, s, k = x.shape
    tm = min(tm, s)
    modulate = scale is not None
    ropes = tuple(bool(o[2]) and rope_tabs is not None for o in outs)
    splits = tuple(o[3] for o in outs)
    ins = [x, g.reshape(1, k)]
    in_specs = [pl.BlockSpec((1, tm, k), lambda i, j: (i, j, 0)), _const_spec((1, k))]
    if modulate:
        ins += [scale, shift]
        in_specs += [pl.BlockSpec((1, 1, k), lambda i, j: (i, 0, 0))] * 2
    if any(ropes):
        ins += list(rope_tabs)
        in_specs += [pl.BlockSpec((tm, rope_tabs[0].shape[1]), lambda i, j: (j, 0))] * 3
    out_specs, out_shape = [], []
    for w, dt, _, split in outs:
        ins.append(w)
        in_specs.append(_const_spec(w.shape))
        n = w.shape[1]
        if split:
            out_specs.append(pl.BlockSpec((1, split, tm, n // split), lambda i, j: (i, 0, j, 0)))
            out_shape.append(jax.ShapeDtypeStruct((b, split, s, n // split), dt))
        else:
            out_specs.append(pl.BlockSpec((1, tm, n), lambda i, j: (i, j, 0)))
            out_shape.append(jax.ShapeDtypeStruct((b, s, n), dt))
    return pl.pallas_call(
        functools.partial(_norm_proj_kernel, n_w=len(outs), modulate=modulate, ropes=ropes, splits=splits),
        grid=(b, s // tm),
        in_specs=in_specs,
        out_specs=out_specs,
        out_shape=out_shape,
        compiler_params=_params('parallel', 'parallel'),
        name='norm_proj',
    )(*ins)


def _hy_filter_kernel(w1t_ref, w1s_ref, w1c_ref, b1_ref, w2_ref, b2_ref, w3_ref, b3_ref, fr_ref, dec_ref, o_ref, *, n):
    t = lax.broadcasted_iota(jnp.int32, (n, 1), 0).astype(F32) / n
    bands = lax.broadcasted_iota(jnp.int32, (1, HY_BANDS), 1).astype(F32) + 1.0
    ang = (2.0 * jnp.pi) * t * bands
    pre = (t * w1t_ref[...]
           + jnp.dot(jnp.sin(ang), w1s_ref[...], precision=HIGHEST, preferred_element_type=F32)
           + jnp.dot(jnp.cos(ang), w1c_ref[...], precision=HIGHEST, preferred_element_type=F32)
           + b1_ref[...])
    hid = jnp.sin(fr_ref[0:1, :] * pre)
    hid = jnp.sin(fr_ref[1:2, :] * (jnp.dot(hid, w2_ref[...], precision=HIGHEST, preferred_element_type=F32) + b2_ref[...]))
    filt = jnp.dot(hid, w3_ref[...], precision=HIGHEST, preferred_element_type=F32) + b3_ref[...]
    filt = filt * jnp.exp(-t * jnp.abs(dec_ref[...]))
    col = jnp.sum(jnp.abs(filt), axis=0, keepdims=True) - jnp.abs(filt[0:1, :])
    w = HY_WIDTH
    for o in range(HY_ORDER):
        lo = o * 2 * w
        f0 = filt[0:1, lo:lo + w] + filt[0:1, lo + w:lo + 2 * w]
        inv = 1.0 / (col[:, lo:lo + w] + col[:, lo + w:lo + 2 * w] + jnp.abs(f0))
        o_ref[:, lo:lo + w] = filt[:, lo:lo + w] * inv
        o_ref[:, lo + w:lo + 2 * w] = filt[:, lo + w:lo + 2 * w] * inv


def _hy_filters(n, w1, b1, w2, b2, w3, b3, freq, decay):
    cols = w3.shape[1]
    ins = [w1[0:1], w1[1:1 + HY_BANDS], w1[1 + HY_BANDS:], b1.reshape(1, -1), w2, b2.reshape(1, -1), w3,
           b3.reshape(1, -1), freq, decay.reshape(1, -1)]
    out = pl.pallas_call(
        functools.partial(_hy_filter_kernel, n=n),
        grid=(1,),
        in_specs=[_const_spec(a.shape) for a in ins],
        out_specs=_const_spec((n, cols)),
        out_shape=jax.ShapeDtypeStruct((n, cols), F32),
        compiler_params=_params('arbitrary'),
        name='hy_filter',
    )(*ins)
    return out.reshape(n, HY_ORDER, 2, HY_WIDTH)


def _two_sided(filt_n):
    n = filt_n.shape[0]
    hf, hb = filt_n[:, :, 0], filt_n[:, :, 1]
    h = jnp.concatenate([hf[:1] + hb[:1], hf[1:], jnp.zeros((1,) + hf.shape[1:], F32), hb[:0:-1]], axis=0)
    return h.reshape(2 * n, HY_ORDER * HY_WIDTH)


def _short_conv_kernel(*refs, s):
    x_refs, w_refs, b_refs, o_refs = refs[0:3], refs[3:6], refs[6:9], refs[9:12]
    row = lax.broadcasted_iota(jnp.int32, (s, 1), 0)
    for x_ref, w_ref, b_ref, o_ref in zip(x_refs, w_refs, b_refs, o_refs):
        x = x_ref[0]
        prev = jnp.where(row == 0, 0.0, pltpu.roll(x, 1, axis=0))
        nxt = jnp.where(row == s - 1, 0.0, pltpu.roll(x, s - 1, axis=0))
        o_ref[0] = prev * w_ref[0:1, :] + x * w_ref[1:2, :] + nxt * w_ref[2:3, :] + b_ref[...]


def _short_conv(parts, conv_w, conv_b):
    b, s, c = parts[0].shape
    tc = LANES
    ws = [conv_w[:, i * c:(i + 1) * c] for i in range(3)]
    bs = [conv_b[i * c:(i + 1) * c].reshape(1, c) for i in range(3)]
    xspec = pl.BlockSpec((1, s, tc), lambda i, j: (i, 0, j))
    return pl.pallas_call(
        functools.partial(_short_conv_kernel, s=s),
        grid=(b, c // tc),
        in_specs=[xspec] * 3 + [pl.BlockSpec((3, tc), lambda i, j: (0, j))] * 3 + [pl.BlockSpec((1, tc), lambda i, j: (0, j))] * 3,
        out_specs=[xspec] * 3,
        out_shape=[jax.ShapeDtypeStruct((b, s, c), F32)] * 3,
        compiler_params=_params('parallel', 'parallel'),
        name='short_conv',
    )(*parts, *ws, *bs)


def _dft_cos_sin(rows, cols, period):
    ang = 2.0 * np.pi * ((np.arange(rows)[:, None] * np.arange(cols)[None, :]) % period) / period
    return np.cos(ang), np.sin(ang)


def _fft_tables(n, inner):
    big = 2 * n
    n1 = big // inner
    c1, s1 = _dft_cos_sin(n1, n1, n1)
    h = n1 // 2
    outer_data = np.block([[c1[:, :h], s1[:, :h]], [-s1[:, :h], c1[:, :h]]])
    outer_real = np.concatenate([c1, -s1], axis=0)
    outer_inv = np.block([[c1[:h, :], -s1[:h, :]], [s1[:h, :], c1[:h, :]]]) / big
    c2, s2 = _dft_cos_sin(inner, inner, inner)
    inner_fwd = np.block([[c2, s2], [-s2, c2]])
    inner_inv = np.block([[c2, -s2], [s2, c2]])
    ct, st = _dft_cos_sin(n1, inner, big)
    f = lambda a: jnp.asarray(a, F32)
    return dict(n1=n1, inner=inner, outer_data=f(outer_data), outer_real=f(outer_real), outer_inv=f(outer_inv),
                inner_fwd=_hi_lo_cols(inner_fwd), inner_inv=_hi_lo_cols(inner_inv),
                tw_cos=f(ct).reshape(n1, inner, 1), tw_sin=f(st).reshape(n1, inner, 1))


def _left_mm_kernel(m_ref, x_ref, o_ref):
    o_ref[0] = jnp.dot(m_ref[...], x_ref[0], precision=HIGHEST, preferred_element_type=F32)


def _left_mm(m, x, tl=4096):
    p, k, l = x.shape
    mm = m.shape[0]
    tl = min(tl, l)
    return pl.pallas_call(
        _left_mm_kernel,
        grid=(p, l // tl),
        in_specs=[_const_spec(m.shape), pl.BlockSpec((1, k, tl), lambda i, j: (i, 0, j))],
        out_specs=pl.BlockSpec((1, mm, tl), lambda i, j: (i, 0, j)),
        out_shape=jax.ShapeDtypeStruct((p, mm, l), F32),
        compiler_params=_params('parallel', 'parallel'),
        name='fft_outer',
    )(m, x)


def _hi_lo_cols(m):
    m = np.asarray(m, np.float32)
    hi = m.astype(BF16)
    lo = (m - hi.astype(np.float32)).astype(BF16)
    return jnp.asarray(np.concatenate([hi, hi, lo], axis=1))


def _hi_lo_rows(x):
    hi = x.astype(BF16)
    lo = (x - hi.astype(F32)).astype(BF16)
    return jnp.concatenate([hi, lo, hi], axis=0)


def _inner_kernel(a_ref, twc_ref, tws_ref, gf_ref, *rest, convolve, inner, kb):
    for s in range(kb):
        ar, ai = a_ref[0, 0, s], a_ref[0, 1, s]
        tc, ts = twc_ref[s], tws_ref[s]
        br = ar * tc + ai * ts
        bi = ai * tc - ar * ts
        x = jnp.dot(gf_ref[...], _hi_lo_rows(jnp.concatenate([br, bi], axis=0)), preferred_element_type=F32)
        if not convolve:
            o_ref = rest[0]
            o_ref[0, 0, s] = x[:inner]
            o_ref[0, 1, s] = x[inner:]
            continue
        h_ref, gi_ref, o_ref = rest
        xr, xi = x[:inner], x[inner:]
        hr, hi = h_ref[0, 0, s], h_ref[0, 1, s]
        yr = xr * hr - xi * hi
        yi = xr * hi + xi * hr
        z = jnp.dot(gi_ref[...], _hi_lo_rows(jnp.concatenate([yr, yi], axis=0)), preferred_element_type=F32)
        zr, zi = z[:inner], z[inner:]
        o_ref[0, 0, s] = zr * tc - zi * ts
        o_ref[0, 1, s] = zi * tc + zr * ts


def _fft_inner(a, tab, c, h=None, h_block=0):
    p = a.shape[0]
    n1, inner = tab['n1'], tab['inner']
    a5 = a.reshape(p, 2, n1, inner, c)
    tc = 2 * LANES
    kb = 4
    blk = pl.BlockSpec((1, 2, kb, inner, tc), lambda i, k, j: (i, 0, k, 0, j))
    tw_spec = pl.BlockSpec((kb, inner, 1), lambda i, k, j: (k, 0, 0))
    ins = [a5, tab['tw_cos'], tab['tw_sin'], tab['inner_fwd']]
    in_specs = [blk, tw_spec, tw_spec, _const_spec(tab['inner_fwd'].shape)]
    if h is not None:
        ch = h.shape[-1] // inner
        nb = c // tc
        ins += [h.reshape(1, 2, n1, inner, ch), tab['inner_inv']]
        in_specs += [pl.BlockSpec((1, 2, kb, inner, tc), lambda i, k, j: (0, 0, k, 0, h_block * nb + j)),
                     _const_spec(tab['inner_inv'].shape)]
    out = pl.pallas_call(
        functools.partial(_inner_kernel, convolve=h is not None, inner=inner, kb=kb),
        grid=(p, n1 // kb, c // tc),
        in_specs=in_specs,
        out_specs=blk,
        out_shape=jax.ShapeDtypeStruct(a5.shape, F32),
        compiler_params=_params('parallel', 'parallel', 'parallel'),
        name='fft_inner',
    )(*ins)
    return out.reshape(p, 2 * n1, inner * c)


def _gate_kernel(m_ref, z_ref, u_ref, x_ref, bias_ref, *rest, chain):
    y = jnp.dot(m_ref[...], z_ref[0], precision=HIGHEST, preferred_element_type=F32)
    nxt = x_ref[0] * (y + u_ref[0] * bias_ref[...])
    if chain:
        mf_ref, o_ref, a_ref = rest
        o_ref[0] = nxt
        a_ref[0] = jnp.dot(mf_ref[...], nxt, precision=HIGHEST, preferred_element_type=F32)
    else:
        rest[0][0] = nxt


def _fft_gate(tab, z, u, x, bias_l, chain, tl=4096):
    p, k2, l = z.shape
    n1 = tab['n1']
    tl = min(tl, l)
    row = pl.BlockSpec((1, n1, tl), lambda i, j: (i, 0, j))
    ins = [tab['outer_inv'], z, u, x, bias_l]
    in_specs = [_const_spec((n1, k2)), pl.BlockSpec((1, k2, tl), lambda i, j: (i, 0, j)), row, row,
                pl.BlockSpec((1, tl), lambda i, j: (0, j))]
    out_specs = [row]
    out_shape = [jax.ShapeDtypeStruct((p, n1, l), F32)]
    if chain:
        ins.append(tab['outer_data'])
        in_specs.append(_const_spec((k2, n1)))
        out_specs.append(pl.BlockSpec((1, k2, tl), lambda i, j: (i, 0, j)))
        out_shape.append(jax.ShapeDtypeStruct((p, k2, l), F32))
    return pl.pallas_call(
        functools.partial(_gate_kernel, chain=chain),
        grid=(p, l // tl),
        in_specs=in_specs,
        out_specs=out_specs,
        out_shape=out_shape,
        compiler_params=_params('parallel', 'parallel'),
        name='fft_gate',
    )(*ins)


def _hyena(parts, conv_w, conv_b, filt_n, bias, inner):
    b, s, c = parts[0].shape
    tab = _fft_tables(s, inner)
    n1 = tab['n1']
    lanes = inner * c
    h_taps = _two_sided(filt_n).reshape(1, n1, inner * HY_ORDER * c)
    h_spec = _fft_inner(_left_mm(tab['outer_real'], h_taps), tab, HY_ORDER * c)
    v, x1, x2 = [a.reshape(b // 2, n1, lanes) for a in _short_conv(parts, conv_w, conv_b)]
    bias_l = [jnp.tile(bias[o], inner).reshape(1, lanes) for o in range(HY_ORDER)]
    a = _left_mm(tab['outer_data'], v)
    z = _fft_inner(a, tab, c, h_spec, 0)
    z2, a = _fft_gate(tab, z, v, x1, bias_l[0], chain=True)
    z = _fft_inner(a, tab, c, h_spec, 1)
    (z3,) = _fft_gate(tab, z, z2, x2, bias_l[1], chain=False)
    return z3.reshape(b, s, c)


def _attn_kernel(*refs, n_q, n_pieces, ncomp, scale, post_scale):
    q_refs = refs[:n_q]
    pos = n_q
    pieces = []
    for _ in range(n_pieces):
        pieces.append((refs[pos:pos + n_q], refs[pos + n_q]))
        pos += n_q + 1
    if ncomp == 2:
        lam_ref, g_ref = refs[pos:pos + 2]
        pos += 2
    o_ref = refs[pos]
    outs = []
    for c in range(ncomp):
        qs = [(q_ref[0, c] * (scale * math.log2(math.e))).astype(BF16) for q_ref in q_refs]
        scores = []
        for k_refs, _ in pieces:
            s = None
            for q, k_ref in zip(qs, k_refs):
                t = _dot_nt(q, k_ref[0, c if k_ref.shape[1] == ncomp else 0])
                s = t if s is None else s + t
            scores.append(s)
        m = None
        for s in scores:
            mp = jnp.max(s, axis=-1, keepdims=True)
            m = mp if m is None else jnp.maximum(m, mp)
        l, o = None, None
        for s, (_, v_ref) in zip(scores, pieces):
            p = jnp.exp2(s - m)
            lp = jnp.sum(p, axis=-1, keepdims=True)
            op = jnp.dot(p.astype(BF16), v_ref[0, 0], preferred_element_type=F32)
            l = lp if l is None else l + lp
            o = op if o is None else o + op
        outs.append(o / l)
    if ncomp == 2:
        o = outs[0] - lam_ref[0] * outs[1]
        o = _rms(o) * g_ref[...] * post_scale
    else:
        o = outs[0]
    o_ref[0, 0] = o


def _attention(q_parts, pieces, heads, ncomp, scale, tq=256, lam=None, subln_g=None, post_scale=1.0):
    b, _, sq, _ = q_parts[0].shape
    dv = pieces[0][1].shape[3]
    tq = min(tq, sq)
    ins = list(q_parts)
    in_specs = [pl.BlockSpec((1, ncomp, tq, q.shape[3]), lambda i, h, j: (i, h, j, 0)) for q in q_parts]
    for k_parts, v in pieces:
        for k in k_parts:
            ins.append(k)
            if k.shape[1] == 1:
                in_specs.append(pl.BlockSpec((1, 1) + k.shape[2:], lambda i, h, j: (i, 0, 0, 0)))
            else:
                in_specs.append(pl.BlockSpec((1, ncomp) + k.shape[2:], lambda i, h, j: (i, h, 0, 0)))
        ins.append(v)
        in_specs.append(pl.BlockSpec((1, 1) + v.shape[2:], lambda i, h, j: (i, h, 0, 0)))
    if ncomp == 2:
        ins += [lam.reshape(1), subln_g.reshape(1, dv)]
        in_specs += [pl.BlockSpec(memory_space=pltpu.SMEM), _const_spec((1, dv))]
    return pl.pallas_call(
        functools.partial(_attn_kernel, n_q=len(q_parts), n_pieces=len(pieces), ncomp=ncomp, scale=scale,
                          post_scale=post_scale),
        grid=(b, heads, sq // tq),
        in_specs=in_specs,
        out_specs=pl.BlockSpec((1, 1, tq, dv), lambda i, h, j: (i, h, j, 0)),
        out_shape=jax.ShapeDtypeStruct((b, heads, sq, dv), F32),
        compiler_params=_params('parallel', 'parallel', 'parallel'),
        name='attention',
    )(*ins)


def _forget_terms(f, log_lb, log_1m_lb, one_m_lb):
    log_sig = jnp.minimum(f, 0.0) - jnp.log1p(jnp.exp(-jnp.abs(f)))
    b = log_1m_lb + log_sig
    log_g = jnp.maximum(log_lb, b) + jnp.log1p(jnp.exp(-jnp.abs(log_lb - b)))
    return log_g, one_m_lb * jax.nn.sigmoid(-f)


def _hg_tables():
    ck, sub = HG_CHUNK, HG_SUB
    t = np.arange(ck)
    cum_mats, half_masks, group_masks, sels = [], [], [], []
    for rev in (False, True):
        mats = [(t[None, :] >= t[:, None]) if rev else (t[None, :] <= t[:, None])]
        halves = []
        hs = ck // 2
        while hs >= sub:
            pos = t % (2 * hs)
            b = t - pos + hs
            mats.append((t[None, :] >= b[:, None]) if rev else (t[None, :] < b[:, None]))
            q_half = (pos < hs) if rev else (pos >= hs)
            halves.append(np.stack([q_half, ~q_half]))
            if not rev:
                grp = (t[:, None] // (2 * hs)) == (t[None, :] // (2 * hs))
                group_masks.append(np.concatenate([grp, grp], axis=0))
            hs //= 2
        cum_mats.append(np.concatenate(mats, axis=0))
        half_masks.append(np.stack(halves))
        r, c = np.arange(ck)[:, None], np.arange(ck * sub)[None, :]
        same = (c // sub) == r
        tt, ss = (c // sub) % sub, c % sub
        sels.append(same & ((ss >= tt) if rev else (ss <= tt)))
    lanes = 2 * HG_KEY_DIM
    ln = np.arange(lanes)
    bd = (ln[:, None] // HG_KEY_DIM) == (ln[None, :] // HG_KEY_DIM)
    hm = np.broadcast_to(np.stack(half_masks)[..., None], (2, len(half_masks[0]), 2, ck, lanes))
    return (jnp.asarray(np.stack(cum_mats), BF16), jnp.asarray(hm, F32), jnp.asarray(np.stack(group_masks), F32),
            jnp.asarray(np.stack(sels), BF16), jnp.asarray(bd, F32))


def _split3(x):
    a = x.astype(BF16)
    r = x - a.astype(F32)
    b = r.astype(BF16)
    return a, b, (r - b.astype(F32)).astype(BF16)


def _hg_chunk(q, k, v, lg, st, rev, cm, hm, gm, sel, bd, m0, m1):
    ck, sub = HG_CHUNK, HG_SUB
    call = sum(jnp.dot(cm, piece, preferred_element_type=F32) for piece in _split3(lg))
    cum = call[0:ck]
    tot = cum[0:1] if rev else cum[ck - 1:ck]
    o = _dot_nt((q * jnp.exp(cum)).astype(BF16), st.astype(BF16))
    kd = (k * jnp.exp(tot - cum)).astype(BF16)
    st_new = st * jnp.exp(tot) + bd * jnp.dot(v.T.astype(BF16), kd, preferred_element_type=F32)
    s2 = None
    for lv in range(gm.shape[0]):
        cb = call[(lv + 1) * ck:(lv + 2) * ck]
        qd = q * jnp.exp(jnp.minimum(cum - cb, 0.0)) * hm[lv, 0]
        kf = (k * jnp.exp(jnp.minimum(cb - cum, 0.0)) * hm[lv, 1]).astype(BF16)
        q2 = jnp.concatenate([qd * m0, qd * m1], axis=0).astype(BF16)
        term = _dot_nt(q2, kf) * gm[lv]
        s2 = term if s2 is None else s2 + term
    r = jnp.dot(s2.astype(BF16), v.astype(BF16), preferred_element_type=F32)
    o = o + m0 * r[:ck] + m1 * r[ck:]
    rows, vts = [], []
    for i in range(ck // sub):
        lo, hi = i * sub, (i + 1) * sub
        ki, ci = k[lo:hi], cum[lo:hi]
        for t in range(lo, hi):
            rows.append((q[t:t + 1] * ki * jnp.exp(jnp.minimum(cum[t:t + 1] - ci, 0.0))).astype(BF16))
            vts.append(v[lo:hi])
    sc = jnp.dot(jnp.concatenate(rows, axis=0), bd.astype(BF16), preferred_element_type=F32)
    o = o + jnp.dot(sel, (sc * jnp.concatenate(vts, axis=0)).astype(BF16), preferred_element_type=F32)
    return o, st_new


def _hgrn_kernel(q_ref, ff_ref, fb_ref, i_ref, qc_ref, ffc_ref, fbc_ref, ic_ref, lb_ref, g_ref,
                 cm_ref, hm_ref, gm_ref, sel_ref, bd_ref, o_ref, oc_ref, or_ref, ocr_ref, st_ref, *, n_lat, n_ctx):
    ck = HG_CHUNK
    lanes = o_ref.shape[-1]
    lane = lax.broadcasted_iota(jnp.int32, (1, lanes), 1)
    m0 = (lane < HG_KEY_DIM).astype(F32)
    m1 = 1.0 - m0
    bd = bd_ref[...]
    gm = gm_ref[...]

    def one(q, f, v, rev):
        d = 1 if rev else 0
        lg, k = _forget_terms(f, lb_ref[d, 0:1, :], lb_ref[d, 1:2, :], lb_ref[d, 2:3, :])
        o, st = _hg_chunk(q, k, v, lg, st_ref[d], rev, cm_ref[d], hm_ref[d], gm, sel_ref[d], bd, m0, m1)
        st_ref[d] = st
        return o

    def sweep(qr, ffr, fbr, ir, out_f, out_r, n):
        nc = n // ck

        def body(step, carry):
            idf = pl.ds(pl.multiple_of(step * ck, ck), ck)
            idr = pl.ds(pl.multiple_of((nc - 1 - step) * ck, ck), ck)
            out_f[0, idf, :] = one(qr[0, idf, :], ffr[0, idf, :], ir[0, idf, :], False)
            out_r[idr, :] = one(qr[0, idr, :], fbr[0, idr, :], ir[0, idr, :], True)
            return carry

        lax.fori_loop(0, nc, body, 0, unroll=2)

    st_ref[...] = jnp.zeros(st_ref.shape, F32)
    sweep(qc_ref, ffc_ref, fbc_ref, ic_ref, oc_ref, ocr_ref, n_ctx)
    sweep(q_ref, ff_ref, fb_ref, i_ref, o_ref, or_ref, n_lat)

    mean_mat = bd * (1.0 / HG_VAL_DIM)

    def readout(out, out_r, n):
        tile = min(n, 512)

        def body(step, carry):
            idx = pl.ds(pl.multiple_of(step * tile, tile), tile)
            x = out[0, idx, :] + out_r[idx, :]
            ms = jnp.dot(x * x, mean_mat, precision=HIGHEST, preferred_element_type=F32)
            out[0, idx, :] = x * lax.rsqrt(ms + NORM_EPS) * g_ref[...]
            return carry

        lax.fori_loop(0, n // tile, body, 0)

    readout(oc_ref, ocr_ref, n_ctx)
    readout(o_ref, or_ref, n_lat)


def _hgrn(q, ff, fb, iv, qc, ffc, fbc, ic, lb_terms, norm_g):
    b, n_lat, width = q.shape
    n_ctx = qc.shape[1]
    lanes = 2 * HG_KEY_DIM
    tables = _hg_tables()
    lat = pl.BlockSpec((1, n_lat, lanes), lambda i, j: (i, 0, j))
    ctx = pl.BlockSpec((1, n_ctx, lanes), lambda i, j: (i, 0, j))
    g2 = jnp.tile(norm_g, 2).reshape(1, lanes)
    return pl.pallas_call(
        functools.partial(_hgrn_kernel, n_lat=n_lat, n_ctx=n_ctx),
        grid=(b, width // lanes),
        in_specs=[lat] * 4 + [ctx] * 4 + [pl.BlockSpec((2, 3, lanes), lambda i, j: (0, 0, j)), _const_spec((1, lanes))]
                 + [_const_spec(t.shape) for t in tables],
        out_specs=[lat, ctx],
        out_shape=[jax.ShapeDtypeStruct(q.shape, F32), jax.ShapeDtypeStruct(qc.shape, F32)],
        scratch_shapes=[pltpu.VMEM((n_lat, lanes), F32), pltpu.VMEM((n_ctx, lanes), F32), pltpu.VMEM((2, lanes, lanes), F32)],
        compiler_params=_params('parallel', 'parallel'),
        name='hgrn2',
    )(q, ff, fb, iv, qc, ffc, fbc, ic, lb_terms, g2, *tables)


def _out_proj_kernel(hy_ref, da_ref, mla_ref, hg_ref, gate_ref, x_ref, g1_ref, w_ref, o_ref):
    c = hy_ref.shape[2]
    acc = jnp.dot(hy_ref[0].astype(BF16), w_ref[0:c, :], preferred_element_type=F32)
    for i, head_ref in ((1, da_ref), (2, mla_ref)):
        dv = head_ref.shape[3]
        for h in range(head_ref.shape[1]):
            lo = i * c + h * dv
            acc = acc + jnp.dot(head_ref[0, h].astype(BF16), w_ref[lo:lo + dv, :], preferred_element_type=F32)
    hg = hg_ref[0] * _silu(gate_ref[0])
    acc = acc + jnp.dot(hg.astype(BF16), w_ref[3 * c:4 * c, :], preferred_element_type=F32)
    o_ref[0] = x_ref[0] + g1_ref[0] * acc


def _out_proj(y_hy, y_da, y_mla, y_hg, gate, x, g1, w_out, tm=512):
    b, s, d = x.shape
    tm = min(tm, s)
    c = y_hy.shape[2]
    part = pl.BlockSpec((1, tm, c), lambda i, j: (i, j, 0))
    headed = lambda a: pl.BlockSpec((1, a.shape[1], tm, a.shape[3]), lambda i, j: (i, 0, j, 0))
    row = pl.BlockSpec((1, tm, d), lambda i, j: (i, j, 0))
    return pl.pallas_call(
        _out_proj_kernel,
        grid=(b, s // tm),
        in_specs=[part, headed(y_da), headed(y_mla), part, part, row, pl.BlockSpec((1, 1, d), lambda i, j: (i, 0, 0)),
                  _const_spec(w_out.shape)],
        out_specs=row,
        out_shape=jax.ShapeDtypeStruct(x.shape, F32),
        compiler_params=_params('parallel', 'parallel'),
        name='out_proj',
    )(y_hy, y_da, y_mla, y_hg, gate, x, g1, w_out)


def _router_kernel(x_ref, g_ref, sc_ref, sh_ref, wrt_ref, bias_ref, *rest, compact):
    h = _rms(x_ref[0]) * g_ref[...] * (1.0 + sc_ref[0]) + sh_ref[0]
    tm = h.shape[0]
    scores = jax.nn.sigmoid(_dot_nt(wrt_ref[...], h, precision=HIGHEST))
    choice = scores + bias_ref[...]
    per = N_EXPERTS // N_EXPERT_GROUPS
    neg = -jnp.inf
    iota_g = lax.broadcasted_iota(jnp.int32, (per, tm), 0)
    grp_rows = []
    for gi in range(N_EXPERT_GROUPS):
        blk = choice[gi * per:(gi + 1) * per]
        m1 = jnp.max(blk, axis=0, keepdims=True)
        first = jnp.min(jnp.where(blk == m1, iota_g, per), axis=0, keepdims=True)
        m2 = jnp.max(jnp.where(iota_g == first, neg, blk), axis=0, keepdims=True)
        grp_rows.append(m1 + m2)
    grp = jnp.concatenate(grp_rows, axis=0)
    iota_n = lax.broadcasted_iota(jnp.int32, (N_EXPERT_GROUPS, tm), 0)
    gsel = jnp.zeros((N_EXPERT_GROUPS, tm), F32)
    for _ in range(TOPK_GROUPS):
        m = jnp.max(grp, axis=0, keepdims=True)
        first = jnp.min(jnp.where(grp == m, iota_n, N_EXPERT_GROUPS), axis=0, keepdims=True)
        hit = iota_n == first
        gsel = jnp.where(hit, 1.0, gsel)
        grp = jnp.where(hit, neg, grp)
    emask = jnp.concatenate([jnp.broadcast_to(gsel[gi:gi + 1], (per, tm)) for gi in range(N_EXPERT_GROUPS)], axis=0)
    cand = jnp.where(emask > 0.0, choice, neg)
    iota_e = lax.broadcasted_iota(jnp.int32, (N_EXPERTS, tm), 0)
    sel = jnp.zeros((N_EXPERTS, tm), F32)
    chosen = []
    for _ in range(TOP_K):
        m = jnp.max(cand, axis=0, keepdims=True)
        first = jnp.min(jnp.where(cand == m, iota_e, N_EXPERTS), axis=0, keepdims=True)
        hit = iota_e == first
        sel = jnp.where(hit, 1.0, sel)
        cand = jnp.where(hit, neg, cand)
        chosen.append(first)
    w = scores * sel
    gate = w / jnp.sum(w, axis=0, keepdims=True) * ROUTED_SCALE
    if not compact:
        h_ref, gate_ref = rest
        h_ref[0] = h.astype(BF16)
        gate_ref[0] = gate
        return
    hp_ref, eid_ref, rank_ref, w_ref, cnt_out_ref, cnt_ref = rest
    hp_ref[0] = _pack_halves(h)

    @pl.when((pl.program_id(0) == 0) & (pl.program_id(1) == 0))
    def _():
        cnt_ref[...] = jnp.zeros(cnt_ref.shape, F32)

    src = lax.broadcasted_iota(jnp.int32, (tm, tm), 0)
    dst = lax.broadcasted_iota(jnp.int32, (tm, tm), 1)
    running = jnp.dot(sel.astype(BF16), (src <= dst).astype(BF16), preferred_element_type=F32)
    rank_dense = cnt_ref[:, 0:1] + running - 1.0
    e_rows, r_rows, w_rows = [], [], []
    for first in chosen:
        hit = iota_e == first
        e_rows.append(first)
        r_rows.append(jnp.sum(jnp.where(hit, rank_dense, 0.0), axis=0, keepdims=True))
        w_rows.append(jnp.sum(jnp.where(hit, gate, 0.0), axis=0, keepdims=True))
    eid_ref[...] = jnp.concatenate(e_rows, axis=0)
    rank_ref[...] = jnp.concatenate(r_rows, axis=0).astype(jnp.int32)
    w_ref[...] = jnp.concatenate(w_rows, axis=0)
    cnt_ref[...] = cnt_ref[...] + running[:, tm - 1:tm]
    cnt_out_ref[...] = cnt_ref[...]


def _router(x, g, scale, shift, w_router, e_bias, tm=512, compact=False):
    b, s, d = x.shape
    tm = min(tm, s)
    e = w_router.shape[1]
    row = pl.BlockSpec((1, tm, d), lambda i, j: (i, j, 0))
    mod = pl.BlockSpec((1, 1, d), lambda i, j: (i, 0, 0))
    if compact:
        nj = s // tm
        tok = pl.BlockSpec((TOP_K, tm), lambda i, j: (0, i * nj + j))
        out_specs = [pl.BlockSpec((1, tm, d // 2), lambda i, j: (i, j, 0)), tok, tok, tok, _const_spec((e, LANES))]
        out_shape = [jax.ShapeDtypeStruct((b, s, d // 2), jnp.int32), jax.ShapeDtypeStruct((TOP_K, b * s), jnp.int32),
                     jax.ShapeDtypeStruct((TOP_K, b * s), jnp.int32), jax.ShapeDtypeStruct((TOP_K, b * s), F32),
                     jax.ShapeDtypeStruct((e, LANES), F32)]
        scratch = [pltpu.VMEM((e, LANES), F32)]
        semantics = ('arbitrary', 'arbitrary')
    else:
        out_specs = [row, pl.BlockSpec((1, e, tm), lambda i, j: (i, 0, j))]
        out_shape = [jax.ShapeDtypeStruct((b, s, d), BF16), jax.ShapeDtypeStruct((b, e, s), F32)]
        scratch = []
        semantics = ('parallel', 'parallel')
    return pl.pallas_call(
        functools.partial(_router_kernel, compact=compact),
        grid=(b, s // tm),
        in_specs=[row, _const_spec((1, d)), mod, mod, _const_spec((e, d)), _const_spec((e, 1))],
        out_specs=out_specs,
        out_shape=out_shape,
        scratch_shapes=scratch,
        compiler_params=_params(*semantics),
        name='router',
    )(x, g.reshape(1, d), scale, shift, w_router.T, e_bias.reshape(e, 1))


def _moe_kernel(h_ref, x_ref, gate_ref, g2_ref, wg_ref, wu_ref, wd_ref, sg_ref, su_ref, sd_ref, *rest, final):
    if final:
        fg_ref, o_ref, acc_ref = rest
    else:
        o_ref, acc_ref = rest
    e = pl.program_id(2)
    h = h_ref[0]

    @pl.when(e == 0)
    def _():
        a = jnp.dot(h, sg_ref[...], preferred_element_type=F32)
        u = jnp.dot(h, su_ref[...], preferred_element_type=F32)
        acc_ref[...] = jnp.dot((_silu(a) * u).astype(BF16), sd_ref[...], preferred_element_type=F32)

    lane = lax.broadcasted_iota(jnp.int32, gate_ref.shape[1:], 1)
    gcol = jnp.sum(jnp.where(lane == e, gate_ref[0], 0.0), axis=-1, keepdims=True)
    a = jnp.dot(h, wg_ref[0].astype(BF16), preferred_element_type=F32)
    u = jnp.dot(h, wu_ref[0].astype(BF16), preferred_element_type=F32)
    acc_ref[...] += jnp.dot((_silu(a) * u * gcol).astype(BF16), wd_ref[0].astype(BF16), preferred_element_type=F32)

    @pl.when(e == pl.num_programs(2) - 1)
    def _():
        y = x_ref[0] + g2_ref[0] * acc_ref[...]
        if final:
            y = _rms(y) * fg_ref[...]
        o_ref[0] = y


def _moe(h2, x, gate, g2, layer, w_gate, w_up, w_down, s_gate, s_up, s_down, final_g=None, tm=1024):
    b, s, d = x.shape
    tm = min(tm, s)
    _, e, _, ff = w_gate.shape
    row = pl.BlockSpec((1, tm, d), lambda i, j, k: (i, j, 0))
    ins = [h2, x, gate, g2, w_gate, w_up, w_down, s_gate, s_up, s_down]
    in_specs = [row, row, pl.BlockSpec((1, tm, e), lambda i, j, k: (i, j, 0)),
                pl.BlockSpec((1, 1, d), lambda i, j, k: (i, 0, 0)),
                pl.BlockSpec((None, 1, d, ff), lambda i, j, k: (layer, k, 0, 0)),
                pl.BlockSpec((None, 1, d, ff), lambda i, j, k: (layer, k, 0, 0)),
                pl.BlockSpec((None, 1, ff, d), lambda i, j, k: (layer, k, 0, 0)),
                _const_spec(s_gate.shape), _const_spec(s_up.shape), _const_spec(s_down.shape)]
    if final_g is not None:
        ins.append(final_g.reshape(1, d))
        in_specs.append(_const_spec((1, d)))
    return pl.pallas_call(
        functools.partial(_moe_kernel, final=final_g is not None),
        grid=(b, s // tm, e),
        in_specs=in_specs,
        out_specs=row,
        out_shape=jax.ShapeDtypeStruct(x.shape, F32),
        scratch_shapes=[pltpu.VMEM((tm, d), F32)],
        compiler_params=_params('parallel', 'parallel', 'arbitrary'),
        name='moe',
    )(*ins)


MOE_ROW_TILE = 512
SC_ROWS = 128
V7X_SC_CORES = 2
V7X_SC_SUBCORES = 16


def _pack_halves(x):
    n = x.shape[1] // 2
    lo = pltpu.bitcast(x[:, :n].astype(BF16).astype(F32), jnp.int32)
    hi = pltpu.bitcast(x[:, n:].astype(BF16).astype(F32), jnp.int32)
    return jnp.bitwise_or(jnp.bitwise_and(hi, -65536), lax.shift_right_logical(lo, 16))


def _unpack_halves(p):
    lo = pltpu.bitcast(lax.shift_left(p, 16), F32).astype(BF16)
    hi = pltpu.bitcast(jnp.bitwise_and(p, -65536), F32).astype(BF16)
    return lo, hi


def _route_pos_kernel(off_ref, eid_ref, rank_ref, pos_ref):
    eid = eid_ref[...]
    base = jnp.zeros(eid.shape, jnp.int32)
    for e in range(N_EXPERTS):
        base = jnp.where(eid == e, off_ref[e], base)
    pos_ref[...] = base + rank_ref[...]


def _route_pos(offsets, eid, rank):
    return pl.pallas_call(
        _route_pos_kernel,
        grid=(1,),
        in_specs=[pl.BlockSpec(memory_space=pltpu.SMEM), _const_spec(eid.shape), _const_spec(rank.shape)],
        out_specs=_const_spec(eid.shape),
        out_shape=jax.ShapeDtypeStruct(eid.shape, jnp.int32),
        compiler_params=_params('arbitrary'),
        name='route_pos',
    )(offsets, eid, rank)


def _sc_mesh():
    return plsc.VectorSubcoreMesh(core_axis_name='c', subcore_axis_name='s', num_cores=V7X_SC_CORES,
                                  num_subcores=V7X_SC_SUBCORES)


def _sc_dispatch(hp, pos, n_rows):
    t, w = hp.shape
    k = pos.shape[0]
    workers = V7X_SC_CORES * V7X_SC_SUBCORES
    per_worker = t // workers
    pos_flat = pos.reshape(k * t)

    @functools.partial(pl.kernel, mesh=_sc_mesh(), out_type=jax.ShapeDtypeStruct((n_rows, w), jnp.int32),
                       scratch_types=[pltpu.VMEM((SC_ROWS,), jnp.int32), pltpu.VMEM((SC_ROWS, w), jnp.int32),
                                      pltpu.SemaphoreType.DMA])
    def scatter(hp_hbm, pos_hbm, out_hbm, idx_v, rows_v, sem):
        wid = lax.axis_index('s') * V7X_SC_CORES + lax.axis_index('c')

        @pl.loop(0, per_worker // SC_ROWS)
        def _(i):
            t0 = pl.multiple_of(wid * per_worker + i * SC_ROWS, SC_ROWS)
            pltpu.sync_copy(hp_hbm.at[pl.ds(t0, SC_ROWS)], rows_v)
            for j in range(k):
                pltpu.sync_copy(pos_hbm.at[pl.ds(pl.multiple_of(j * t + t0, SC_ROWS), SC_ROWS)], idx_v)
                pltpu.async_copy(rows_v, out_hbm.at[idx_v], sem).wait()

    return scatter(hp, pos_flat)


def _sc_collect(yp, pos):
    _, w = yp.shape
    k, t = pos.shape
    workers = V7X_SC_CORES * V7X_SC_SUBCORES
    per_worker = k * t // workers
    pos_flat = pos.reshape(k * t)

    @functools.partial(pl.kernel, mesh=_sc_mesh(), out_type=jax.ShapeDtypeStruct((k * t, w), jnp.int32),
                       scratch_types=[pltpu.VMEM((SC_ROWS,), jnp.int32), pltpu.VMEM((SC_ROWS, w), jnp.int32),
                                      pltpu.SemaphoreType.DMA])
    def gather(yp_hbm, pos_hbm, out_hbm, idx_v, rows_v, sem):
        wid = lax.axis_index('s') * V7X_SC_CORES + lax.axis_index('c')

        @pl.loop(0, per_worker // SC_ROWS)
        def _(i):
            r0 = pl.multiple_of(wid * per_worker + i * SC_ROWS, SC_ROWS)
            pltpu.sync_copy(pos_hbm.at[pl.ds(r0, SC_ROWS)], idx_v)
            pltpu.async_copy(yp_hbm.at[idx_v], rows_v, sem).wait()
            pltpu.sync_copy(rows_v, out_hbm.at[pl.ds(r0, SC_ROWS)])

    return gather(yp, pos_flat)


def _expert_kernel(te_ref, nu_ref, x_ref, wg_ref, wu_ref, wd_ref, o_ref, wg_s, wu_s, wd_s):
    i = pl.program_id(0)

    @pl.when(i < nu_ref[0])
    def _():
        @pl.when((i == 0) | (te_ref[i] != te_ref[jnp.maximum(i - 1, 0)]))
        def _():
            wg_s[...] = wg_ref[0].astype(BF16)
            wu_s[...] = wu_ref[0].astype(BF16)
            wd_s[...] = wd_ref[0].astype(BF16)

        lo, hi = _unpack_halves(x_ref[...])
        half = lo.shape[1]
        a = (jnp.dot(lo, wg_s[:half, :], preferred_element_type=F32)
             + jnp.dot(hi, wg_s[half:, :], preferred_element_type=F32))
        u = (jnp.dot(lo, wu_s[:half, :], preferred_element_type=F32)
             + jnp.dot(hi, wu_s[half:, :], preferred_element_type=F32))
        y = jnp.dot((_silu(a) * u).astype(BF16), wd_s[...], preferred_element_type=F32)
        o_ref[...] = _pack_halves(y)


def _experts(xp, tile_expert, n_used, layer, w_gate, w_up, w_down):
    n_rows, half = xp.shape
    _, _, d, ff = w_gate.shape
    r = MOE_ROW_TILE
    row = pl.BlockSpec((r, half), lambda i, te, nu: (i, 0))
    grid_spec = pltpu.PrefetchScalarGridSpec(
        num_scalar_prefetch=2,
        grid=(n_rows // r,),
        in_specs=[row,
                  pl.BlockSpec((None, 1, d, ff), lambda i, te, nu: (layer, te[i], 0, 0)),
                  pl.BlockSpec((None, 1, d, ff), lambda i, te, nu: (layer, te[i], 0, 0)),
                  pl.BlockSpec((None, 1, ff, d), lambda i, te, nu: (layer, te[i], 0, 0))],
        out_specs=row,
        scratch_shapes=[pltpu.VMEM((d, ff), BF16), pltpu.VMEM((d, ff), BF16), pltpu.VMEM((ff, d), BF16)],
    )
    return pl.pallas_call(
        _expert_kernel,
        grid_spec=grid_spec,
        out_shape=jax.ShapeDtypeStruct((n_rows, half), jnp.int32),
        compiler_params=_params('arbitrary'),
        name='experts',
    )(tile_expert, n_used, xp, w_gate, w_up, w_down)


def _combine_kernel(yg_ref, w_ref, hp_ref, x_ref, g2_ref, sg_ref, su_ref, sd_ref, *rest, final):
    if final:
        fg_ref, o_ref = rest
    else:
        (o_ref,) = rest
    half = hp_ref.shape[2]
    lo, hi = _unpack_halves(hp_ref[0])
    sg, su = sg_ref[...], su_ref[...]
    a = jnp.dot(lo, sg[:half], preferred_element_type=F32) + jnp.dot(hi, sg[half:], preferred_element_type=F32)
    u = jnp.dot(lo, su[:half], preferred_element_type=F32) + jnp.dot(hi, su[half:], preferred_element_type=F32)
    acc = jnp.dot((_silu(a) * u).astype(BF16), sd_ref[...], preferred_element_type=F32)
    acc_lo, acc_hi = acc[:, :half], acc[:, half:]
    wts = w_ref[0]
    for k in range(yg_ref.shape[0]):
        ylo, yhi = _unpack_halves(yg_ref[k, 0])
        wk = wts[:, k:k + 1]
        acc_lo = acc_lo + wk * ylo.astype(F32)
        acc_hi = acc_hi + wk * yhi.astype(F32)
    y = x_ref[0] + g2_ref[0] * jnp.concatenate([acc_lo, acc_hi], axis=1)
    if final:
        y = _rms(y) * fg_ref[...]
    o_ref[0] = y


def _combine(yg, wts, hp, x, g2, s_gate, s_up, s_down, final_g=None, tm=256):
    b, s, d = x.shape
    k = yg.shape[0]
    half = d // 2
    row = pl.BlockSpec((1, tm, d), lambda i, j: (i, j, 0))
    prow = pl.BlockSpec((1, tm, half), lambda i, j: (i, j, 0))
    ins = [yg, wts, hp, x, g2, s_gate, s_up, s_down]
    in_specs = [pl.BlockSpec((k, 1, tm, half), lambda i, j: (0, i, j, 0)), pl.BlockSpec((1, tm, k), lambda i, j: (i, j, 0)),
                prow, row, pl.BlockSpec((1, 1, d), lambda i, j: (i, 0, 0)),
                _const_spec(s_gate.shape), _const_spec(s_up.shape), _const_spec(s_down.shape)]
    if final_g is not None:
        ins.append(final_g.reshape(1, d))
        in_specs.append(_const_spec((1, d)))
    return pl.pallas_call(
        functools.partial(_combine_kernel, final=final_g is not None),
        grid=(b, s // tm),
        in_specs=in_specs,
        out_specs=row,
        out_shape=jax.ShapeDtypeStruct(x.shape, F32),
        compiler_params=_params('parallel', 'parallel'),
        name='moe_combine',
    )(*ins)


def _routed_moe(x, g, scale, shift, g2, w_router, e_bias, layer, w_gate, w_up, w_down, s_gate, s_up, s_down, final_g=None):
    b, s, d = x.shape
    t = b * s
    hp, eid, rank, wts, counts = _router(x, g, scale, shift, w_router, e_bias, compact=True)
    counts = counts[:, 0].astype(jnp.int32)
    r = MOE_ROW_TILE
    padded = (counts + (r - 1)) // r * r
    ends = jnp.cumsum(padded)
    offsets = ends - padded
    n_rows = t * TOP_K + N_EXPERTS * r
    tile_start = jnp.arange(n_rows // r, dtype=jnp.int32) * r
    tile_expert = jnp.minimum(jnp.sum((tile_start[:, None] >= ends[None, :]).astype(jnp.int32), axis=1), N_EXPERTS - 1)
    n_used = (ends[-1] // r).reshape(1).astype(jnp.int32)
    pos = _route_pos(offsets.astype(jnp.int32), eid, rank)
    xp = _sc_dispatch(hp.reshape(t, d // 2), pos, n_rows)
    yp = _experts(xp, tile_expert.astype(jnp.int32), n_used, layer, w_gate, w_up, w_down)
    yg = _sc_collect(yp, pos).reshape(TOP_K, b, s, d // 2)
    return _combine(yg, wts.T.reshape(b, s, TOP_K), hp, x, g2, s_gate, s_up, s_down, final_g)


def _mixers(p, pc, ctx_out, prm, l, lam_init, rope_tabs, lb_terms):
    s = p['hy_v'].shape[1]
    sc = pc['hy_v'].shape[1]

    hy_args = (prm['hy_w1'][l], prm['hy_b1'][l], prm['hy_w2'][l], prm['hy_b2'][l], prm['hy_w3'][l], prm['hy_b3'][l],
               prm['hy_sin_freq'][l], prm['hy_decay'][l])
    y_hy = _hyena([p['hy_v'], p['hy_x1'], p['hy_x2']], prm['hy_conv_w'][l], prm['hy_conv_b'][l],
                  _hy_filters(s, *hy_args), prm['hy_bias'][l], inner=128)
    yc_hy = None
    if ctx_out:
        yc_hy = _hyena([pc['hy_v'], pc['hy_x1'], pc['hy_x2']], prm['hy_conv_w'][l], prm['hy_conv_b'][l],
                       _hy_filters(sc, *hy_args), prm['hy_bias'][l], inner=32)

    lp = prm['da_lambda'][l].astype(F32)
    lam = jnp.exp(jnp.sum(lp[0] * lp[1])) - jnp.exp(jnp.sum(lp[2] * lp[3])) + lam_init
    da_kw = dict(heads=DA_HEADS, ncomp=2, scale=DA_HEAD_DIM ** -0.5, lam=lam, subln_g=prm['da_subln_g'][l],
                 post_scale=1.0 - lam_init)
    da_ctx = ([pc['da_k']], pc['da_v'])
    y_da = _attention([p['da_q']], [da_ctx, ([p['da_k']], p['da_v'])], **da_kw)
    yc_da = _attention([pc['da_q']], [da_ctx], **da_kw) if ctx_out else None

    wq = prm['mla_w_q_up'][l].reshape(MLA_Q_RANK, MLA_HEADS, MLA_NOPE_DIM + MLA_ROPE_DIM)
    wq_n = wq[:, :, :MLA_NOPE_DIM].reshape(MLA_Q_RANK, -1).astype(BF16)
    wq_r = wq[:, :, MLA_NOPE_DIM:].reshape(MLA_Q_RANK, -1).astype(BF16)
    wkv = prm['mla_w_kv_up'][l].reshape(MLA_KV_RANK, MLA_HEADS, MLA_NOPE_DIM + MLA_V_DIM)
    wkv_n = wkv[:, :, :MLA_NOPE_DIM].reshape(MLA_KV_RANK, -1).astype(BF16)
    wkv_v = wkv[:, :, MLA_NOPE_DIM:].reshape(MLA_KV_RANK, -1).astype(BF16)

    def queries(qd, tabs):
        return _norm_proj(qd, prm['mla_q_norm_g'][l], [(wq_n, F32, False, MLA_HEADS), (wq_r, F32, True, MLA_HEADS)],
                          rope_tabs=tabs)

    def keys_values(kvd):
        return _norm_proj(kvd, prm['mla_kv_norm_g'][l], [(wkv_n, BF16, False, MLA_HEADS), (wkv_v, BF16, False, MLA_HEADS)])

    kn_l, v_l = keys_values(p['mla_kv'])
    kn_c, v_c = keys_values(pc['mla_kv'])
    mla_kw = dict(heads=MLA_HEADS, ncomp=1, scale=(MLA_NOPE_DIM + MLA_ROPE_DIM) ** -0.5)
    mla_ctx = ([kn_c, pc['mla_kr']], v_c)
    y_mla = _attention(queries(p['mla_q'], rope_tabs), [mla_ctx, ([kn_l, p['mla_kr']], v_l)], **mla_kw)
    yc_mla = _attention(queries(pc['mla_q'], None), [mla_ctx], **mla_kw) if ctx_out else None

    o, oc = _hgrn(p['hg_q'], p['hg_ff'], p['hg_fb'], p['hg_i'], pc['hg_q'], pc['hg_ff'], pc['hg_fb'], pc['hg_i'],
                  lb_terms, prm['hg_norm_g'][l])
    return (y_hy, y_da, y_mla, o), (yc_hy, yc_da, yc_mla, oc)


def kernel(x, c, ctx, c_ctx, w_ada, b_ada, norm1_g, norm2_g, w_in, w_out, hy_conv_w, hy_conv_b, hy_w1, hy_b1, hy_w2, hy_b2, hy_w3, hy_b3, hy_sin_freq, hy_decay, hy_bias, da_lambda, da_subln_g, mla_q_norm_g, mla_w_q_up, mla_kv_norm_g, mla_w_kv_up, hg_lower_bounds, hg_norm_g, moe_w_router, moe_bias, moe_w_gate, moe_w_up, moe_w_down, moe_sh_gate, moe_sh_up, moe_sh_down, final_norm_g):
    prm = dict(hy_conv_w=hy_conv_w, hy_conv_b=hy_conv_b, hy_w1=hy_w1, hy_b1=hy_b1, hy_w2=hy_w2, hy_b2=hy_b2,
               hy_w3=hy_w3, hy_b3=hy_b3, hy_sin_freq=hy_sin_freq, hy_decay=hy_decay, hy_bias=hy_bias,
               da_lambda=da_lambda, da_subln_g=da_subln_g, mla_q_norm_g=mla_q_norm_g, mla_w_q_up=mla_w_q_up,
               mla_kv_norm_g=mla_kv_norm_g, mla_w_kv_up=mla_w_kv_up, hg_norm_g=hg_norm_g)
    b, n_lat, d = x.shape
    depth = w_in.shape[0]
    rows = n_lat // GRID_W
    row_pos = jnp.repeat(jnp.arange(rows, dtype=jnp.int32), GRID_W)
    col_pos = jnp.tile(jnp.arange(GRID_W, dtype=jnp.int32), rows)
    rope_tabs = _rope_tables(row_pos, col_pos, 2 * DA_HEADS * DA_HEAD_DIM)
    lbs = jnp.cumsum(jax.nn.softmax(hg_lower_bounds.astype(F32), axis=1), axis=1)
    lbs = lbs - lbs[:, :1]
    cond = jnp.concatenate([c, c_ctx[None], jnp.zeros((8 - b - 1, d), F32)], axis=0)

    for l in range(depth):
        ctx_out = l < depth - 1
        mods = _ada(cond, w_ada[l], b_ada[l])
        sh1, sc1, g1, sh2, sc2, g2 = [m[:, None, :] for m in jnp.split(mods[:b], 6, axis=-1)]
        mc = [jnp.broadcast_to(m[:, None, :], (b, 1, d)) for m in jnp.split(mods[b:b + 1], 6, axis=-1)]

        off = 0
        outs = []
        for _, wdt, dt, rope, split, rep in _SEGMENTS:
            w = w_in[l][:, off:off + wdt].astype(BF16)
            outs.append((jnp.tile(w, (1, rep)) if rep > 1 else w, dt, rope, split))
            off += wdt
        names = [seg[0] for seg in _SEGMENTS]
        p = dict(zip(names, _norm_proj(x, norm1_g[l], outs, sc1, sh1, rope_tabs=rope_tabs)))
        pc = dict(zip(names, _norm_proj(ctx, norm1_g[l], outs, mc[1], mc[0])))

        lb = lbs[:, l]
        lb_terms = jnp.stack([jnp.log(lb), jnp.log1p(-lb), 1.0 - lb], axis=1)
        lam_init = 0.8 - 0.6 * math.exp(-0.3 * l)
        lat_parts, ctx_parts = _mixers(p, pc, ctx_out, prm, l, lam_init, rope_tabs, lb_terms)

        w_out_b = w_out[l].astype(BF16)
        moe_w = (l, moe_w_gate, moe_w_up, moe_w_down,
                 moe_sh_gate[l].astype(BF16), moe_sh_up[l].astype(BF16), moe_sh_down[l].astype(BF16))

        if ctx_out:
            ctx = _out_proj(*ctx_parts, pc['hg_g'], ctx, mc[2], w_out_b)
            flat = ctx.reshape(1, -1, d)
            h2c, gate_c = _router(flat, norm2_g[l], mc[4][:1], mc[3][:1], moe_w_router[l], moe_bias[l])
            ctx = _moe(h2c, flat, gate_c.transpose(0, 2, 1), mc[5][:1], *moe_w).reshape(ctx.shape)

        x = _out_proj(*lat_parts, p['hg_g'], x, g1, w_out_b)
        x = _routed_moe(x, norm2_g[l], sc2, sh2, g2, moe_w_router[l], moe_bias[l], *moe_w,
                        final_g=None if ctx_out else final_norm_g)

    return x
```

```python
import functools
import math

import numpy as np
import jax
import jax.numpy as jnp
from jax import lax
from jax.experimental import pallas as pl
from jax.experimental.pallas import tpu as pltpu
from jax.experimental.pallas import tpu_sc as plsc

F32 = jnp.float32
BF16 = jnp.bfloat16
HIGHEST = lax.Precision.HIGHEST

D_MODEL = 1024
GRID_W = 64
HY_WIDTH = 256
HY_ORDER = 2
HY_BANDS = 16
DA_HEADS = 4
DA_HEAD_DIM = 32
MLA_HEADS = 4
MLA_Q_RANK = 192
MLA_KV_RANK = 128
MLA_NOPE_DIM = 64
MLA_ROPE_DIM = 32
MLA_V_DIM = 64
HG_HEADS = 4
HG_KEY_DIM = 64
HG_VAL_DIM = 64
HG_CHUNK = 64
HG_SUB = 8
N_EXPERTS = 64
N_EXPERT_GROUPS = 8
TOPK_GROUPS = 4
TOP_K = 8
EXPERT_FF = 256
ROUTED_SCALE = 2.5
ROPE_BASE = 10000.0
NORM_EPS = 1e-6

V7X_VMEM_LIMIT_BYTES = 56 * 1024 * 1024
LANES = 128

_SEGMENTS = (
    ('hy_v', HY_WIDTH, F32, False, 0, 1), ('hy_x1', HY_WIDTH, F32, False, 0, 1), ('hy_x2', HY_WIDTH, F32, False, 0, 1),
    ('da_q', 2 * DA_HEADS * DA_HEAD_DIM, F32, True, 2 * DA_HEADS, 1),
    ('da_k', 2 * DA_HEADS * DA_HEAD_DIM, BF16, True, 2 * DA_HEADS, 1),
    ('da_v', 2 * DA_HEADS * DA_HEAD_DIM, BF16, False, DA_HEADS, 1),
    ('mla_q', MLA_Q_RANK, F32, False, 0, 1), ('mla_kv', MLA_KV_RANK, F32, False, 0, 1),
    ('mla_kr', MLA_ROPE_DIM, BF16, True, MLA_HEADS, MLA_HEADS),
    ('hg_q', HG_HEADS * HG_KEY_DIM, F32, False, 0, 1), ('hg_ff', HG_HEADS * HG_KEY_DIM, F32, False, 0, 1),
    ('hg_fb', HG_HEADS * HG_KEY_DIM, F32, False, 0, 1), ('hg_i', HG_HEADS * HG_VAL_DIM, F32, False, 0, 1),
    ('hg_g', HG_HEADS * HG_VAL_DIM, F32, False, 0, 1),
)


def _params(*semantics):
    return pltpu.CompilerParams(dimension_semantics=semantics, vmem_limit_bytes=V7X_VMEM_LIMIT_BYTES)


def _const_spec(shape):
    nd = len(shape)
    return pl.BlockSpec(shape, lambda *_: (0,) * nd)


def _rms(x, eps=NORM_EPS):
    return x * lax.rsqrt(jnp.mean(x * x, axis=-1, keepdims=True) + eps)


def _silu(x):
    return x * jax.nn.sigmoid(x)


def _dot_nt(a, b, **kw):
    return lax.dot_general(a, b, (((1,), (1,)), ((), ())), preferred_element_type=F32, **kw)


def _ada_kernel(c_ref, w_ref, b_ref, o_ref):
    s = _silu(c_ref[...])
    o_ref[...] = jnp.dot(s, w_ref[...], precision=HIGHEST, preferred_element_type=F32) + b_ref[...]


def _ada(cond, w, b):
    r, d = cond.shape
    n = w.shape[1]
    tn = 1536
    return pl.pallas_call(
        _ada_kernel,
        grid=(n // tn,),
        in_specs=[_const_spec((r, d)), pl.BlockSpec((d, tn), lambda j: (0, j)), pl.BlockSpec((1, tn), lambda j: (0, j))],
        out_specs=pl.BlockSpec((r, tn), lambda j: (0, j)),
        out_shape=jax.ShapeDtypeStruct((r, n), F32),
        compiler_params=_params('arbitrary'),
        name='ada',
    )(cond, w, b.reshape(1, n))


ROPE_UNIT = 32


def _rope_tables(row, col, width):
    n = ROPE_UNIT // 4
    inv = ROPE_BASE ** (-jnp.arange(n, dtype=F32) / n)
    units = width // ROPE_UNIT
    parts_c, parts_a, parts_b = [], [], []
    zero = jnp.zeros((row.shape[0], n), F32)
    for pos in (row, col):
        ang = pos.astype(F32)[:, None] * inv
        cos, sin = jnp.cos(ang), jnp.sin(ang)
        parts_c += [cos, cos]
        parts_a += [zero, sin]
        parts_b += [-sin, zero]
    tile = lambda ps: jnp.tile(jnp.concatenate(ps, axis=1), (1, units))
    return tile(parts_c), tile(parts_a), tile(parts_b)


def _norm_proj_kernel(*refs, n_w, modulate, ropes, splits):
    x_ref, g_ref = refs[0], refs[1]
    pos = 2
    if modulate:
        sc_ref, sh_ref = refs[2], refs[3]
        pos = 4
    if any(ropes):
        rc_ref, ra_ref, rb_ref = refs[pos:pos + 3]
        pos += 3
    w_refs = refs[pos:pos + n_w]
    o_refs = refs[pos + n_w:]
    y = _rms(x_ref[0]) * g_ref[...]
    if modulate:
        y = y * (1.0 + sc_ref[0]) + sh_ref[0]
    yb = y.astype(BF16)
    for w_ref, o_ref, rope, split in zip(w_refs, o_refs, ropes, splits):
        o = jnp.dot(yb, w_ref[...], preferred_element_type=F32)
        if rope:
            wd = o.shape[1]
            shift = ROPE_UNIT // 4
            o = (o * rc_ref[:, :wd] + pltpu.roll(o, shift, axis=1) * ra_ref[:, :wd]
                 + pltpu.roll(o, wd - shift, axis=1) * rb_ref[:, :wd])
        if split:
            unit = o.shape[1] // split
            for u in range(split):
                o_ref[0, u] = o[:, u * unit:(u + 1) * unit].astype(o_ref.dtype)
        else:
            o_ref[0] = o.astype(o_ref.dtype)


def _norm_proj(x, g, outs, scale=None, shift=None, rope_tabs=None, tm=512):
    b, s, k = x.shape
    tm = min(tm, s)
    modulate = scale is not None
    ropes = tuple(bool(o[2]) and rope_tabs is not None for o in outs)
    splits = tuple(o[3] for o in outs)
    ins = [x, g.reshape(1, k)]
    in_specs = [pl.BlockSpec((1, tm, k), lambda i, j: (i, j, 0)), _const_spec((1, k))]
    if modulate:
        ins += [scale, shift]
        in_specs += [pl.BlockSpec((1, 1, k), lambda i, j: (i, 0, 0))] * 2
    if any(ropes):
        ins += list(rope_tabs)
        in_specs += [pl.BlockSpec((tm, rope_tabs[0].shape[1]), lambda i, j: (j, 0))] * 3
    out_specs, out_shape = [], []
    for w, dt, _, split in outs:
        ins.append(w)
        in_specs.append(_const_spec(w.shape))
        n = w.shape[1]
        if split:
            out_specs.append(pl.BlockSpec((1, split, tm, n // split), lambda i, j: (i, 0, j, 0)))
            out_shape.append(jax.ShapeDtypeStruct((b, split, s, n // split), dt))
        else:
            out_specs.append(pl.BlockSpec((1, tm, n), lambda i, j: (i, j, 0)))
            out_shape.append(jax.ShapeDtypeStruct((b, s, n), dt))
    return pl.pallas_call(
        functools.partial(_norm_proj_kernel, n_w=len(outs), modulate=modulate, ropes=ropes, splits=splits),
        grid=(b, s // tm),
        in_specs=in_specs,
        out_specs=out_specs,
        out_shape=out_shape,
        compiler_params=_params('parallel', 'parallel'),
        name='norm_proj',
    )(*ins)


def _hy_filter_kernel(w1t_ref, w1s_ref, w1c_ref, b1_ref, w2_ref, b2_ref, w3_ref, b3_ref, fr_ref, dec_ref, o_ref, *, n):
    t = lax.broadcasted_iota(jnp.int32, (n, 1), 0).astype(F32) / n
    bands = lax.broadcasted_iota(jnp.int32, (1, HY_BANDS), 1).astype(F32) + 1.0
    ang = (2.0 * jnp.pi) * t * bands
    pre = (t * w1t_ref[...]
           + jnp.dot(jnp.sin(ang), w1s_ref[...], precision=HIGHEST, preferred_element_type=F32)
           + jnp.dot(jnp.cos(ang), w1c_ref[...], precision=HIGHEST, preferred_element_type=F32)
           + b1_ref[...])
    hid = jnp.sin(fr_ref[0:1, :] * pre)
    hid = jnp.sin(fr_ref[1:2, :] * (jnp.dot(hid, w2_ref[...], precision=HIGHEST, preferred_element_type=F32) + b2_ref[...]))
    filt = jnp.dot(hid, w3_ref[...], precision=HIGHEST, preferred_element_type=F32) + b3_ref[...]
    filt = filt * jnp.exp(-t * jnp.abs(dec_ref[...]))
    col = jnp.sum(jnp.abs(filt), axis=0, keepdims=True) - jnp.abs(filt[0:1, :])
    w = HY_WIDTH
    for o in range(HY_ORDER):
        lo = o * 2 * w
        f0 = filt[0:1, lo:lo + w] + filt[0:1, lo + w:lo + 2 * w]
        inv = 1.0 / (col[:, lo:lo + w] + col[:, lo + w:lo + 2 * w] + jnp.abs(f0))
        o_ref[:, lo:lo + w] = filt[:, lo:lo + w] * inv
        o_ref[:, lo + w:lo + 2 * w] = filt[:, lo + w:lo + 2 * w] * inv


def _hy_filters(n, w1, b1, w2, b2, w3, b3, freq, decay):
    cols = w3.shape[1]
    ins = [w1[0:1], w1[1:1 + HY_BANDS], w1[1 + HY_BANDS:], b1.reshape(1, -1), w2, b2.reshape(1, -1), w3,
           b3.reshape(1, -1), freq, decay.reshape(1, -1)]
    out = pl.pallas_call(
        functools.partial(_hy_filter_kernel, n=n),
        grid=(1,),
        in_specs=[_const_spec(a.shape) for a in ins],
        out_specs=_const_spec((n, cols)),
        out_shape=jax.ShapeDtypeStruct((n, cols), F32),
        compiler_params=_params('arbitrary'),
        name='hy_filter',
    )(*ins)
    return out.reshape(n, HY_ORDER, 2, HY_WIDTH)


def _two_sided(filt_n):
    n = filt_n.shape[0]
    hf, hb = filt_n[:, :, 0], filt_n[:, :, 1]
    h = jnp.concatenate([hf[:1] + hb[:1], hf[1:], jnp.zeros((1,) + hf.shape[1:], F32), hb[:0:-1]], axis=0)
    return h.reshape(2 * n, HY_ORDER * HY_WIDTH)


def _short_conv_kernel(*refs, s):
    x_refs, w_refs, b_refs, o_refs = refs[0:3], refs[3:6], refs[6:9], refs[9:12]
    row = lax.broadcasted_iota(jnp.int32, (s, 1), 0)
    for x_ref, w_ref, b_ref, o_ref in zip(x_refs, w_refs, b_refs, o_refs):
        x = x_ref[0]
        prev = jnp.where(row == 0, 0.0, pltpu.roll(x, 1, axis=0))
        nxt = jnp.where(row == s - 1, 0.0, pltpu.roll(x, s - 1, axis=0))
        o_ref[0] = prev * w_ref[0:1, :] + x * w_ref[1:2, :] + nxt * w_ref[2:3, :] + b_ref[...]


def _short_conv(parts, conv_w, conv_b):
    b, s, c = parts[0].shape
    tc = LANES
    ws = [conv_w[:, i * c:(i + 1) * c] for i in range(3)]
    bs = [conv_b[i * c:(i + 1) * c].reshape(1, c) for i in range(3)]
    xspec = pl.BlockSpec((1, s, tc), lambda i, j: (i, 0, j))
    return pl.pallas_call(
        functools.partial(_short_conv_kernel, s=s),
        grid=(b, c // tc),
        in_specs=[xspec] * 3 + [pl.BlockSpec((3, tc), lambda i, j: (0, j))] * 3 + [pl.BlockSpec((1, tc), lambda i, j: (0, j))] * 3,
        out_specs=[xspec] * 3,
        out_shape=[jax.ShapeDtypeStruct((b, s, c), F32)] * 3,
        compiler_params=_params('parallel', 'parallel'),
        name='short_conv',
    )(*parts, *ws, *bs)


def _dft_cos_sin(rows, cols, period):
    ang = 2.0 * np.pi * ((np.arange(rows)[:, None] * np.arange(cols)[None, :]) % period) / period
    return np.cos(ang), np.sin(ang)


def _fft_tables(n, inner):
    big = 2 * n
    n1 = big // inner
    c1, s1 = _dft_cos_sin(n1, n1, n1)
    h = n1 // 2
    outer_data = np.block([[c1[:, :h], s1[:, :h]], [-s1[:, :h], c1[:, :h]]])
    outer_real = np.concatenate([c1, -s1], axis=0)
    outer_inv = np.block([[c1[:h, :], -s1[:h, :]], [s1[:h, :], c1[:h, :]]]) / big
    c2, s2 = _dft_cos_sin(inner, inner, inner)
    inner_fwd = np.block([[c2, s2], [-s2, c2]])
    inner_inv = np.block([[c2, -s2], [s2, c2]])
    ct, st = _dft_cos_sin(n1, inner, big)
    f = lambda a: jnp.asarray(a, F32)
    return dict(n1=n1, inner=inner, outer_data=f(outer_data), outer_real=f(outer_real), outer_inv=f(outer_inv),
                inner_fwd=_hi_lo_cols(inner_fwd), inner_inv=_hi_lo_cols(inner_inv),
                tw_cos=f(ct).reshape(n1, inner, 1), tw_sin=f(st).reshape(n1, inner, 1))


def _left_mm_kernel(m_ref, x_ref, o_ref):
    o_ref[0] = jnp.dot(m_ref[...], x_ref[0], precision=HIGHEST, preferred_element_type=F32)


def _left_mm(m, x, tl=4096):
    p, k, l = x.shape
    mm = m.shape[0]
    tl = min(tl, l)
    return pl.pallas_call(
        _left_mm_kernel,
        grid=(p, l // tl),
        in_specs=[_const_spec(m.shape), pl.BlockSpec((1, k, tl), lambda i, j: (i, 0, j))],
        out_specs=pl.BlockSpec((1, mm, tl), lambda i, j: (i, 0, j)),
        out_shape=jax.ShapeDtypeStruct((p, mm, l), F32),
        compiler_params=_params('parallel', 'parallel'),
        name='fft_outer',
    )(m, x)


def _hi_lo_cols(m):
    m = np.asarray(m, np.float32)
    hi = m.astype(BF16)
    lo = (m - hi.astype(np.float32)).astype(BF16)
    return jnp.asarray(np.concatenate([hi, hi, lo], axis=1))


def _hi_lo_rows(x):
    hi = x.astype(BF16)
    lo = (x - hi.astype(F32)).astype(BF16)
    return jnp.concatenate([hi, lo, hi], axis=0)


def _inner_kernel(a_ref, twc_ref, tws_ref, gf_ref, *rest, convolve, inner, kb):
    for s in range(kb):
        ar, ai = a_ref[0, 0, s], a_ref[0, 1, s]
        tc, ts = twc_ref[s], tws_ref[s]
        br = ar * tc + ai * ts
        bi = ai * tc - ar * ts
        x = jnp.dot(gf_ref[...], _hi_lo_rows(jnp.concatenate([br, bi], axis=0)), preferred_element_type=F32)
        if not convolve:
            o_ref = rest[0]
            o_ref[0, 0, s] = x[:inner]
            o_ref[0, 1, s] = x[inner:]
            continue
        h_ref, gi_ref, o_ref = rest
        xr, xi = x[:inner], x[inner:]
        hr, hi = h_ref[0, 0, s], h_ref[0, 1, s]
        yr = xr * hr - xi * hi
        yi = xr * hi + xi * hr
        z = jnp.dot(gi_ref[...], _hi_lo_rows(jnp.concatenate([yr, yi], axis=0)), preferred_element_type=F32)
        zr, zi = z[:inner], z[inner:]
        o_ref[0, 0, s] = zr * tc - zi * ts
        o_ref[0, 1, s] = zi * tc + zr * ts


def _fft_inner(a, tab, c, h=None, h_block=0):
    p = a.shape[0]
    n1, inner = tab['n1'], tab['inner']
    a5 = a.reshape(p, 2, n1, inner, c)
    tc = 2 * LANES
    kb = 4
    blk = pl.BlockSpec((1, 2, kb, inner, tc), lambda i, k, j: (i, 0, k, 0, j))
    tw_spec = pl.BlockSpec((kb, inner, 1), lambda i, k, j: (k, 0, 0))
    ins = [a5, tab['tw_cos'], tab['tw_sin'], tab['inner_fwd']]
    in_specs = [blk, tw_spec, tw_spec, _const_spec(tab['inner_fwd'].shape)]
    if h is not None:
        ch = h.shape[-1] // inner
        nb = c // tc
        ins += [h.reshape(1, 2, n1, inner, ch), tab['inner_inv']]
        in_specs += [pl.BlockSpec((1, 2, kb, inner, tc), lambda i, k, j: (0, 0, k, 0, h_block * nb + j)),
                     _const_spec(tab['inner_inv'].shape)]
    out = pl.pallas_call(
        functools.partial(_inner_kernel, convolve=h is not None, inner=inner, kb=kb),
        grid=(p, n1 // kb, c // tc),
        in_specs=in_specs,
        out_specs=blk,
        out_shape=jax.ShapeDtypeStruct(a5.shape, F32),
        compiler_params=_params('parallel', 'parallel', 'parallel'),
        name='fft_inner',
    )(*ins)
    return out.reshape(p, 2 * n1, inner * c)


def _gate_kernel(m_ref, z_ref, u_ref, x_ref, bias_ref, *rest, chain):
    y = jnp.dot(m_ref[...], z_ref[0], precision=HIGHEST, preferred_element_type=F32)
    nxt = x_ref[0] * (y + u_ref[0] * bias_ref[...])
    if chain:
        mf_ref, o_ref, a_ref = rest
        o_ref[0] = nxt
        a_ref[0] = jnp.dot(mf_ref[...], nxt, precision=HIGHEST, preferred_element_type=F32)
    else:
        rest[0][0] = nxt


def _fft_gate(tab, z, u, x, bias_l, chain, tl=4096):
    p, k2, l = z.shape
    n1 = tab['n1']
    tl = min(tl, l)
    row = pl.BlockSpec((1, n1, tl), lambda i, j: (i, 0, j))
    ins = [tab['outer_inv'], z, u, x, bias_l]
    in_specs = [_const_spec((n1, k2)), pl.BlockSpec((1, k2, tl), lambda i, j: (i, 0, j)), row, row,
                pl.BlockSpec((1, tl), lambda i, j: (0, j))]
    out_specs = [row]
    out_shape = [jax.ShapeDtypeStruct((p, n1, l), F32)]
    if chain:
        ins.append(tab['outer_data'])
        in_specs.append(_const_spec((k2, n1)))
        out_specs.append(pl.BlockSpec((1, k2, tl), lambda i, j: (i, 0, j)))
        out_shape.append(jax.ShapeDtypeStruct((p, k2, l), F32))
    return pl.pallas_call(
        functools.partial(_gate_kernel, chain=chain),
        grid=(p, l // tl),
        in_specs=in_specs,
        out_specs=out_specs,
        out_shape=out_shape,
        compiler_params=_params('parallel', 'parallel'),
        name='fft_gate',
    )(*ins)


def _hyena(parts, conv_w, conv_b, filt_n, bias, inner):
    b, s, c = parts[0].shape
    tab = _fft_tables(s, inner)
    n1 = tab['n1']
    lanes = inner * c
    h_taps = _two_sided(filt_n).reshape(1, n1, inner * HY_ORDER * c)
    h_spec = _fft_inner(_left_mm(tab['outer_real'], h_taps), tab, HY_ORDER * c)
    v, x1, x2 = [a.reshape(b // 2, n1, lanes) for a in _short_conv(parts, conv_w, conv_b)]
    bias_l = [jnp.tile(bias[o], inner).reshape(1, lanes) for o in range(HY_ORDER)]
    a = _left_mm(tab['outer_data'], v)
    z = _fft_inner(a, tab, c, h_spec, 0)
    z2, a = _fft_gate(tab, z, v, x1, bias_l[0], chain=True)
    z = _fft_inner(a, tab, c, h_spec, 1)
    (z3,) = _fft_gate(tab, z, z2, x2, bias_l[1], chain=False)
    return z3.reshape(b, s, c)


def _attn_kernel(*refs, n_q, n_pieces, ncomp, scale, post_scale):
    q_refs = refs[:n_q]
    pos = n_q
    pieces = []
    for _ in range(n_pieces):
        pieces.append((refs[pos:pos + n_q], refs[pos + n_q]))
        pos += n_q + 1
    if ncomp == 2:
        lam_ref, g_ref = refs[pos:pos + 2]
        pos += 2
    o_ref = refs[pos]
    outs = []
    for c in range(ncomp):
        qs = [(q_ref[0, c] * (scale * math.log2(math.e))).astype(BF16) for q_ref in q_refs]
        scores = []
        for k_refs, _ in pieces:
            s = None
            for q, k_ref in zip(qs, k_refs):
                t = _dot_nt(q, k_ref[0, c if k_ref.shape[1] == ncomp else 0])
                s = t if s is None else s + t
            scores.append(s)
        m = None
        for s in scores:
            mp = jnp.max(s, axis=-1, keepdims=True)
            m = mp if m is None else jnp.maximum(m, mp)
        l, o = None, None
        for s, (_, v_ref) in zip(scores, pieces):
            p = jnp.exp2(s - m)
            lp = jnp.sum(p, axis=-1, keepdims=True)
            op = jnp.dot(p.astype(BF16), v_ref[0, 0], preferred_element_type=F32)
            l = lp if l is None else l + lp
            o = op if o is None else o + op
        outs.append(o / l)
    if ncomp == 2:
        o = outs[0] - lam_ref[0] * outs[1]
        o = _rms(o) * g_ref[...] * post_scale
    else:
        o = outs[0]
    o_ref[0, 0] = o


def _attention(q_parts, pieces, heads, ncomp, scale, tq=256, lam=None, subln_g=None, post_scale=1.0):
    b, _, sq, _ = q_parts[0].shape
    dv = pieces[0][1].shape[3]
    tq = min(tq, sq)
    ins = list(q_parts)
    in_specs = [pl.BlockSpec((1, ncomp, tq, q.shape[3]), lambda i, h, j: (i, h, j, 0)) for q in q_parts]
    for k_parts, v in pieces:
        for k in k_parts:
            ins.append(k)
            if k.shape[1] == 1:
                in_specs.append(pl.BlockSpec((1, 1) + k.shape[2:], lambda i, h, j: (i, 0, 0, 0)))
            else:
                in_specs.append(pl.BlockSpec((1, ncomp) + k.shape[2:], lambda i, h, j: (i, h, 0, 0)))
        ins.append(v)
        in_specs.append(pl.BlockSpec((1, 1) + v.shape[2:], lambda i, h, j: (i, h, 0, 0)))
    if ncomp == 2:
        ins += [lam.reshape(1), subln_g.reshape(1, dv)]
        in_specs += [pl.BlockSpec(memory_space=pltpu.SMEM), _const_spec((1, dv))]
    return pl.pallas_call(
        functools.partial(_attn_kernel, n_q=len(q_parts), n_pieces=len(pieces), ncomp=ncomp, scale=scale,
                          post_scale=post_scale),
        grid=(b, heads, sq // tq),
        in_specs=in_specs,
        out_specs=pl.BlockSpec((1, 1, tq, dv), lambda i, h, j: (i, h, j, 0)),
        out_shape=jax.ShapeDtypeStruct((b, heads, sq, dv), F32),
        compiler_params=_params('parallel', 'parallel', 'parallel'),
        name='attention',
    )(*ins)


def _forget_terms(f, log_lb, log_1m_lb, one_m_lb):
    log_sig = jnp.minimum(f, 0.0) - jnp.log1p(jnp.exp(-jnp.abs(f)))
    b = log_1m_lb + log_sig
    log_g = jnp.maximum(log_lb, b) + jnp.log1p(jnp.exp(-jnp.abs(log_lb - b)))
    return log_g, one_m_lb * jax.nn.sigmoid(-f)


def _hg_tables():
    ck, sub = HG_CHUNK, HG_SUB
    t = np.arange(ck)
    cum_mats, half_masks, group_masks, sels = [], [], [], []
    for rev in (False, True):
        mats = [(t[None, :] >= t[:, None]) if rev else (t[None, :] <= t[:, None])]
        halves = []
        hs = ck // 2
        while hs >= sub:
            pos = t % (2 * hs)
            b = t - pos + hs
            mats.append((t[None, :] >= b[:, None]) if rev else (t[None, :] < b[:, None]))
            q_half = (pos < hs) if rev else (pos >= hs)
            halves.append(np.stack([q_half, ~q_half]))
            if not rev:
                grp = (t[:, None] // (2 * hs)) == (t[None, :] // (2 * hs))
                group_masks.append(np.concatenate([grp, grp], axis=0))
            hs //= 2
        cum_mats.append(np.concatenate(mats, axis=0))
        half_masks.append(np.stack(halves))
        r, c = np.arange(ck)[:, None], np.arange(ck * sub)[None, :]
        same = (c // sub) == r
        tt, ss = (c // sub) % sub, c % sub
        sels.append(same & ((ss >= tt) if rev else (ss <= tt)))
    lanes = 2 * HG_KEY_DIM
    ln = np.arange(lanes)
    bd = (ln[:, None] // HG_KEY_DIM) == (ln[None, :] // HG_KEY_DIM)
    hm = np.broadcast_to(np.stack(half_masks)[..., None], (2, len(half_masks[0]), 2, ck, lanes))
    return (jnp.asarray(np.stack(cum_mats), BF16), jnp.asarray(hm, F32), jnp.asarray(np.stack(group_masks), F32),
            jnp.asarray(np.stack(sels), BF16), jnp.asarray(bd, F32))


def _split3(x):
    a = x.astype(BF16)
    r = x - a.astype(F32)
    b = r.astype(BF16)
    return a, b, (r - b.astype(F32)).astype(BF16)


def _hg_chunk(q, k, v, lg, st, rev, cm, hm, gm, sel, bd, m0, m1):
    ck, sub = HG_CHUNK, HG_SUB
    call = sum(jnp.dot(cm, piece, preferred_element_type=F32) for piece in _split3(lg))
    cum = call[0:ck]
    tot = cum[0:1] if rev else cum[ck - 1:ck]
    o = _dot_nt((q * jnp.exp(cum)).astype(BF16), st.astype(BF16))
    kd = (k * jnp.exp(tot - cum)).astype(BF16)
    st_new = st * jnp.exp(tot) + bd * jnp.dot(v.T.astype(BF16), kd, preferred_element_type=F32)
    s2 = None
    for lv in range(gm.shape[0]):
        cb = call[(lv + 1) * ck:(lv + 2) * ck]
        qd = q * jnp.exp(jnp.minimum(cum - cb, 0.0)) * hm[lv, 0]
        kf = (k * jnp.exp(jnp.minimum(cb - cum, 0.0)) * hm[lv, 1]).astype(BF16)
        q2 = jnp.concatenate([qd * m0, qd * m1], axis=0).astype(BF16)
        term = _dot_nt(q2, kf) * gm[lv]
        s2 = term if s2 is None else s2 + term
    r = jnp.dot(s2.astype(BF16), v.astype(BF16), preferred_element_type=F32)
    o = o + m0 * r[:ck] + m1 * r[ck:]
    rows, vts = [], []
    for i in range(ck // sub):
        lo, hi = i * sub, (i + 1) * sub
        ki, ci = k[lo:hi], cum[lo:hi]
        for t in range(lo, hi):
            rows.append((q[t:t + 1] * ki * jnp.exp(jnp.minimum(cum[t:t + 1] - ci, 0.0))).astype(BF16))
            vts.append(v[lo:hi])
    sc = jnp.dot(jnp.concatenate(rows, axis=0), bd.astype(BF16), preferred_element_type=F32)
    o = o + jnp.dot(sel, (sc * jnp.concatenate(vts, axis=0)).astype(BF16), preferred_element_type=F32)
    return o, st_new


def _hgrn_kernel(q_ref, ff_ref, fb_ref, i_ref, qc_ref, ffc_ref, fbc_ref, ic_ref, lb_ref, g_ref,
                 cm_ref, hm_ref, gm_ref, sel_ref, bd_ref, o_ref, oc_ref, or_ref, ocr_ref, st_ref, *, n_lat, n_ctx):
    ck = HG_CHUNK
    lanes = o_ref.shape[-1]
    lane = lax.broadcasted_iota(jnp.int32, (1, lanes), 1)
    m0 = (lane < HG_KEY_DIM).astype(F32)
    m1 = 1.0 - m0
    bd = bd_ref[...]
    gm = gm_ref[...]

    def one(q, f, v, rev):
        d = 1 if rev else 0
        lg, k = _forget_terms(f, lb_ref[d, 0:1, :], lb_ref[d, 1:2, :], lb_ref[d, 2:3, :])
        o, st = _hg_chunk(q, k, v, lg, st_ref[d], rev, cm_ref[d], hm_ref[d], gm, sel_ref[d], bd, m0, m1)
        st_ref[d] = st
        return o

    def sweep(qr, ffr, fbr, ir, out_f, out_r, n):
        nc = n // ck

        def body(step, carry):
            idf = pl.ds(pl.multiple_of(step * ck, ck), ck)
            idr = pl.ds(pl.multiple_of((nc - 1 - step) * ck, ck), ck)
            out_f[0, idf, :] = one(qr[0, idf, :], ffr[0, idf, :], ir[0, idf, :], False)
            out_r[idr, :] = one(qr[0, idr, :], fbr[0, idr, :], ir[0, idr, :], True)
            return carry

        lax.fori_loop(0, nc, body, 0, unroll=2)

    st_ref[...] = jnp.zeros(st_ref.shape, F32)
    sweep(qc_ref, ffc_ref, fbc_ref, ic_ref, oc_ref, ocr_ref, n_ctx)
    sweep(q_ref, ff_ref, fb_ref, i_ref, o_ref, or_ref, n_lat)

    mean_mat = bd * (1.0 / HG_VAL_DIM)

    def readout(out, out_r, n):
        tile = min(n, 512)

        def body(step, carry):
            idx = pl.ds(pl.multiple_of(step * tile, tile), tile)
            x = out[0, idx, :] + out_r[idx, :]
            ms = jnp.dot(x * x, mean_mat, precision=HIGHEST, preferred_element_type=F32)
            out[0, idx, :] = x * lax.rsqrt(ms + NORM_EPS) * g_ref[...]
            return carry

        lax.fori_loop(0, n // tile, body, 0)

    readout(oc_ref, ocr_ref, n_ctx)
    readout(o_ref, or_ref, n_lat)


def _hgrn(q, ff, fb, iv, qc, ffc, fbc, ic, lb_terms, norm_g):
    b, n_lat, width = q.shape
    n_ctx = qc.shape[1]
    lanes = 2 * HG_KEY_DIM
    tables = _hg_tables()
    lat = pl.BlockSpec((1, n_lat, lanes), lambda i, j: (i, 0, j))
    ctx = pl.BlockSpec((1, n_ctx, lanes), lambda i, j: (i, 0, j))
    g2 = jnp.tile(norm_g, 2).reshape(1, lanes)
    return pl.pallas_call(
        functools.partial(_hgrn_kernel, n_lat=n_lat, n_ctx=n_ctx),
        grid=(b, width // lanes),
        in_specs=[lat] * 4 + [ctx] * 4 + [pl.BlockSpec((2, 3, lanes), lambda i, j: (0, 0, j)), _const_spec((1, lanes))]
                 + [_const_spec(t.shape) for t in tables],
        out_specs=[lat, ctx],
        out_shape=[jax.ShapeDtypeStruct(q.shape, F32), jax.ShapeDtypeStruct(qc.shape, F32)],
        scratch_shapes=[pltpu.VMEM((n_lat, lanes), F32), pltpu.VMEM((n_ctx, lanes), F32), pltpu.VMEM((2, lanes, lanes), F32)],
        compiler_params=_params('parallel', 'parallel'),
        name='hgrn2',
    )(q, ff, fb, iv, qc, ffc, fbc, ic, lb_terms, g2, *tables)


def _out_proj_kernel(hy_ref, da_ref, mla_ref, hg_ref, gate_ref, x_ref, g1_ref, w_ref, o_ref):
    c = hy_ref.shape[2]
    acc = jnp.dot(hy_ref[0].astype(BF16), w_ref[0:c, :], preferred_element_type=F32)
    for i, head_ref in ((1, da_ref), (2, mla_ref)):
        dv = head_ref.shape[3]
        for h in range(head_ref.shape[1]):
            lo = i * c + h * dv
            acc = acc + jnp.dot(head_ref[0, h].astype(BF16), w_ref[lo:lo + dv, :], preferred_element_type=F32)
    hg = hg_ref[0] * _silu(gate_ref[0])
    acc = acc + jnp.dot(hg.astype(BF16), w_ref[3 * c:4 * c, :], preferred_element_type=F32)
    o_ref[0] = x_ref[0] + g1_ref[0] * acc


def _out_proj(y_hy, y_da, y_mla, y_hg, gate, x, g1, w_out, tm=512):
    b, s, d = x.shape
    tm = min(tm, s)
    c = y_hy.shape[2]
    part = pl.BlockSpec((1, tm, c), lambda i, j: (i, j, 0))
    headed = lambda a: pl.BlockSpec((1, a.shape[1], tm, a.shape[3]), lambda i, j: (i, 0, j, 0))
    row = pl.BlockSpec((1, tm, d), lambda i, j: (i, j, 0))
    return pl.pallas_call(
        _out_proj_kernel,
        grid=(b, s // tm),
        in_specs=[part, headed(y_da), headed(y_mla), part, part, row, pl.BlockSpec((1, 1, d), lambda i, j: (i, 0, 0)),
                  _const_spec(w_out.shape)],
        out_specs=row,
        out_shape=jax.ShapeDtypeStruct(x.shape, F32),
        compiler_params=_params('parallel', 'parallel'),
        name='out_proj',
    )(y_hy, y_da, y_mla, y_hg, gate, x, g1, w_out)


def _router_kernel(x_ref, g_ref, sc_ref, sh_ref, wrt_ref, bias_ref, *rest, compact):
    h = _rms(x_ref[0]) * g_ref[...] * (1.0 + sc_ref[0]) + sh_ref[0]
    tm = h.shape[0]
    scores = jax.nn.sigmoid(_dot_nt(wrt_ref[...], h, precision=HIGHEST))
    choice = scores + bias_ref[...]
    per = N_EXPERTS // N_EXPERT_GROUPS
    neg = -jnp.inf
    iota_g = lax.broadcasted_iota(jnp.int32, (per, tm), 0)
    grp_rows = []
    for gi in range(N_EXPERT_GROUPS):
        blk = choice[gi * per:(gi + 1) * per]
        m1 = jnp.max(blk, axis=0, keepdims=True)
        first = jnp.min(jnp.where(blk == m1, iota_g, per), axis=0, keepdims=True)
        m2 = jnp.max(jnp.where(iota_g == first, neg, blk), axis=0, keepdims=True)
        grp_rows.append(m1 + m2)
    grp = jnp.concatenate(grp_rows, axis=0)
    iota_n = lax.broadcasted_iota(jnp.int32, (N_EXPERT_GROUPS, tm), 0)
    gsel = jnp.zeros((N_EXPERT_GROUPS, tm), F32)
    for _ in range(TOPK_GROUPS):
        m = jnp.max(grp, axis=0, keepdims=True)
        first = jnp.min(jnp.where(grp == m, iota_n, N_EXPERT_GROUPS), axis=0, keepdims=True)
        hit = iota_n == first
        gsel = jnp.where(hit, 1.0, gsel)
        grp = jnp.where(hit, neg, grp)
    emask = jnp.concatenate([jnp.broadcast_to(gsel[gi:gi + 1], (per, tm)) for gi in range(N_EXPERT_GROUPS)], axis=0)
    cand = jnp.where(emask > 0.0, choice, neg)
    iota_e = lax.broadcasted_iota(jnp.int32, (N_EXPERTS, tm), 0)
    sel = jnp.zeros((N_EXPERTS, tm), F32)
    chosen = []
    for _ in range(TOP_K):
        m = jnp.max(cand, axis=0, keepdims=True)
        first = jnp.min(jnp.where(cand == m, iota_e, N_EXPERTS), axis=0, keepdims=True)
        hit = iota_e == first
        sel = jnp.where(hit, 1.0, sel)
        cand = jnp.where(hit, neg, cand)
        chosen.append(first)
    w = scores * sel
    gate = w / jnp.sum(w, axis=0, keepdims=True) * ROUTED_SCALE
    if not compact:
        h_ref, gate_ref = rest
        h_ref[0] = h.astype(BF16)
        gate_ref[0] = gate
        return
    hp_ref, eid_ref, rank_ref, w_ref, cnt_out_ref, cnt_ref = rest
    hp_ref[0] = _pack_halves(h)

    @pl.when((pl.program_id(0) == 0) & (pl.program_id(1) == 0))
    def _():
        cnt_ref[...] = jnp.zeros(cnt_ref.shape, F32)

    src = lax.broadcasted_iota(jnp.int32, (tm, tm), 0)
    dst = lax.broadcasted_iota(jnp.int32, (tm, tm), 1)
    running = jnp.dot(sel.astype(BF16), (src <= dst).astype(BF16), preferred_element_type=F32)
    rank_dense = cnt_ref[:, 0:1] + running - 1.0
    e_rows, r_rows, w_rows = [], [], []
    for first in chosen:
        hit = iota_e == first
        e_rows.append(first)
        r_rows.append(jnp.sum(jnp.where(hit, rank_dense, 0.0), axis=0, keepdims=True))
        w_rows.append(jnp.sum(jnp.where(hit, gate, 0.0), axis=0, keepdims=True))
    eid_ref[...] = jnp.concatenate(e_rows, axis=0)
    rank_ref[...] = jnp.concatenate(r_rows, axis=0).astype(jnp.int32)
    w_ref[...] = jnp.concatenate(w_rows, axis=0)
    cnt_ref[...] = cnt_ref[...] + running[:, tm - 1:tm]
    cnt_out_ref[...] = cnt_ref[...]


def _router(x, g, scale, shift, w_router, e_bias, tm=512, compact=False):
    b, s, d = x.shape
    tm = min(tm, s)
    e = w_router.shape[1]
    row = pl.BlockSpec((1, tm, d), lambda i, j: (i, j, 0))
    mod = pl.BlockSpec((1, 1, d), lambda i, j: (i, 0, 0))
    if compact:
        nj = s // tm
        tok = pl.BlockSpec((TOP_K, tm), lambda i, j: (0, i * nj + j))
        out_specs = [pl.BlockSpec((1, tm, d // 2), lambda i, j: (i, j, 0)), tok, tok, tok, _const_spec((e, LANES))]
        out_shape = [jax.ShapeDtypeStruct((b, s, d // 2), jnp.int32), jax.ShapeDtypeStruct((TOP_K, b * s), jnp.int32),
                     jax.ShapeDtypeStruct((TOP_K, b * s), jnp.int32), jax.ShapeDtypeStruct((TOP_K, b * s), F32),
                     jax.ShapeDtypeStruct((e, LANES), F32)]
        scratch = [pltpu.VMEM((e, LANES), F32)]
        semantics = ('arbitrary', 'arbitrary')
    else:
        out_specs = [row, pl.BlockSpec((1, e, tm), lambda i, j: (i, 0, j))]
        out_shape = [jax.ShapeDtypeStruct((b, s, d), BF16), jax.ShapeDtypeStruct((b, e, s), F32)]
        scratch = []
        semantics = ('parallel', 'parallel')
    return pl.pallas_call(
        functools.partial(_router_kernel, compact=compact),
        grid=(b, s // tm),
        in_specs=[row, _const_spec((1, d)), mod, mod, _const_spec((e, d)), _const_spec((e, 1))],
        out_specs=out_specs,
        out_shape=out_shape,
        scratch_shapes=scratch,
        compiler_params=_params(*semantics),
        name='router',
    )(x, g.reshape(1, d), scale, shift, w_router.T, e_bias.reshape(e, 1))


def _moe_kernel(h_ref, x_ref, gate_ref, g2_ref, wg_ref, wu_ref, wd_ref, sg_ref, su_ref, sd_ref, *rest, final):
    if final:
        fg_ref, o_ref, acc_ref = rest
    else:
        o_ref, acc_ref = rest
    e = pl.program_id(2)
    h = h_ref[0]

    @pl.when(e == 0)
    def _():
        a = jnp.dot(h, sg_ref[...], preferred_element_type=F32)
        u = jnp.dot(h, su_ref[...], preferred_element_type=F32)
        acc_ref[...] = jnp.dot((_silu(a) * u).astype(BF16), sd_ref[...], preferred_element_type=F32)

    lane = lax.broadcasted_iota(jnp.int32, gate_ref.shape[1:], 1)
    gcol = jnp.sum(jnp.where(lane == e, gate_ref[0], 0.0), axis=-1, keepdims=True)
    a = jnp.dot(h, wg_ref[0].astype(BF16), preferred_element_type=F32)
    u = jnp.dot(h, wu_ref[0].astype(BF16), preferred_element_type=F32)
    acc_ref[...] += jnp.dot((_silu(a) * u * gcol).astype(BF16), wd_ref[0].astype(BF16), preferred_element_type=F32)

    @pl.when(e == pl.num_programs(2) - 1)
    def _():
        y = x_ref[0] + g2_ref[0] * acc_ref[...]
        if final:
            y = _rms(y) * fg_ref[...]
        o_ref[0] = y


def _moe(h2, x, gate, g2, layer, w_gate, w_up, w_down, s_gate, s_up, s_down, final_g=None, tm=1024):
    b, s, d = x.shape
    tm = min(tm, s)
    _, e, _, ff = w_gate.shape
    row = pl.BlockSpec((1, tm, d), lambda i, j, k: (i, j, 0))
    ins = [h2, x, gate, g2, w_gate, w_up, w_down, s_gate, s_up, s_down]
    in_specs = [row, row, pl.BlockSpec((1, tm, e), lambda i, j, k: (i, j, 0)),
                pl.BlockSpec((1, 1, d), lambda i, j, k: (i, 0, 0)),
                pl.BlockSpec((None, 1, d, ff), lambda i, j, k: (layer, k, 0, 0)),
                pl.BlockSpec((None, 1, d, ff), lambda i, j, k: (layer, k, 0, 0)),
                pl.BlockSpec((None, 1, ff, d), lambda i, j, k: (layer, k, 0, 0)),
                _const_spec(s_gate.shape), _const_spec(s_up.shape), _const_spec(s_down.shape)]
    if final_g is not None:
        ins.append(final_g.reshape(1, d))
        in_specs.append(_const_spec((1, d)))
    return pl.pallas_call(
        functools.partial(_moe_kernel, final=final_g is not None),
        grid=(b, s // tm, e),
        in_specs=in_specs,
        out_specs=row,
        out_shape=jax.ShapeDtypeStruct(x.shape, F32),
        scratch_shapes=[pltpu.VMEM((tm, d), F32)],
        compiler_params=_params('parallel', 'parallel', 'arbitrary'),
        name='moe',
    )(*ins)


MOE_ROW_TILE = 512
SC_ROWS = 128
V7X_SC_CORES = 2
V7X_SC_SUBCORES = 16


def _pack_halves(x):
    n = x.shape[1] // 2
    lo = pltpu.bitcast(x[:, :n].astype(BF16).astype(F32), jnp.int32)
    hi = pltpu.bitcast(x[:, n:].astype(BF16).astype(F32), jnp.int32)
    return jnp.bitwise_or(jnp.bitwise_and(hi, -65536), lax.shift_right_logical(lo, 16))


def _unpack_halves(p):
    lo = pltpu.bitcast(lax.shift_left(p, 16), F32).astype(BF16)
    hi = pltpu.bitcast(jnp.bitwise_and(p, -65536), F32).astype(BF16)
    return lo, hi


def _route_pos_kernel(off_ref, eid_ref, rank_ref, pos_ref):
    eid = eid_ref[...]
    base = jnp.zeros(eid.shape, jnp.int32)
    for e in range(N_EXPERTS):
        base = jnp.where(eid == e, off_ref[e], base)
    pos_ref[...] = base + rank_ref[...]


def _route_pos(offsets, eid, rank):
    return pl.pallas_call(
        _route_pos_kernel,
        grid=(1,),
        in_specs=[pl.BlockSpec(memory_space=pltpu.SMEM), _const_spec(eid.shape), _const_spec(rank.shape)],
        out_specs=_const_spec(eid.shape),
        out_shape=jax.ShapeDtypeStruct(eid.shape, jnp.int32),
        compiler_params=_params('arbitrary'),
        name='route_pos',
    )(offsets, eid, rank)


def _sc_mesh():
    return plsc.VectorSubcoreMesh(core_axis_name='c', subcore_axis_name='s', num_cores=V7X_SC_CORES,
                                  num_subcores=V7X_SC_SUBCORES)


def _sc_dispatch(hp, pos, n_rows):
    t, w = hp.shape
    k = pos.shape[0]
    workers = V7X_SC_CORES * V7X_SC_SUBCORES
    per_worker = t // workers
    pos_flat = pos.reshape(k * t)

    @functools.partial(pl.kernel, mesh=_sc_mesh(), out_type=jax.ShapeDtypeStruct((n_rows, w), jnp.int32),
                       scratch_types=[pltpu.VMEM((SC_ROWS,), jnp.int32), pltpu.VMEM((SC_ROWS, w), jnp.int32),
                                      pltpu.SemaphoreType.DMA])
    def scatter(hp_hbm, pos_hbm, out_hbm, idx_v, rows_v, sem):
        wid = lax.axis_index('s') * V7X_SC_CORES + lax.axis_index('c')

        @pl.loop(0, per_worker // SC_ROWS)
        def _(i):
            t0 = pl.multiple_of(wid * per_worker + i * SC_ROWS, SC_ROWS)
            pltpu.sync_copy(hp_hbm.at[pl.ds(t0, SC_ROWS)], rows_v)
            for j in range(k):
                pltpu.sync_copy(pos_hbm.at[pl.ds(pl.multiple_of(j * t + t0, SC_ROWS), SC_ROWS)], idx_v)
                pltpu.async_copy(rows_v, out_hbm.at[idx_v], sem).wait()

    return scatter(hp, pos_flat)


def _sc_collect(yp, pos):
    _, w = yp.shape
    k, t = pos.shape
    workers = V7X_SC_CORES * V7X_SC_SUBCORES
    per_worker = k * t // workers
    pos_flat = pos.reshape(k * t)

    @functools.partial(pl.kernel, mesh=_sc_mesh(), out_type=jax.ShapeDtypeStruct((k * t, w), jnp.int32),
                       scratch_types=[pltpu.VMEM((SC_ROWS,), jnp.int32), pltpu.VMEM((SC_ROWS, w), jnp.int32),
                                      pltpu.SemaphoreType.DMA])
    def gather(yp_hbm, pos_hbm, out_hbm, idx_v, rows_v, sem):
        wid = lax.axis_index('s') * V7X_SC_CORES + lax.axis_index('c')

        @pl.loop(0, per_worker // SC_ROWS)
        def _(i):
            r0 = pl.multiple_of(wid * per_worker + i * SC_ROWS, SC_ROWS)
            pltpu.sync_copy(pos_hbm.at[pl.ds(r0, SC_ROWS)], idx_v)
            pltpu.async_copy(yp_hbm.at[idx_v], rows_v, sem).wait()
            pltpu.sync_copy(rows_v, out_hbm.at[pl.ds(r0, SC_ROWS)])

    return gather(yp, pos_flat)


def _expert_kernel(te_ref, nu_ref, x_ref, wg_ref, wu_ref, wd_ref, o_ref, wg_s, wu_s, wd_s):
    i = pl.program_id(0)

    @pl.when(i < nu_ref[0])
    def _():
        @pl.when((i == 0) | (te_ref[i] != te_ref[jnp.maximum(i - 1, 0)]))
        def _():
            wg_s[...] = wg_ref[0].astype(BF16)
            wu_s[...] = wu_ref[0].astype(BF16)
            wd_s[...] = wd_ref[0].astype(BF16)

        lo, hi = _unpack_halves(x_ref[...])
        half = lo.shape[1]
        a = (jnp.dot(lo, wg_s[:half, :], preferred_element_type=F32)
             + jnp.dot(hi, wg_s[half:, :], preferred_element_type=F32))
        u = (jnp.dot(lo, wu_s[:half, :], preferred_element_type=F32)
             + jnp.dot(hi, wu_s[half:, :], preferred_element_type=F32))
        y = jnp.dot((_silu(a) * u).astype(BF16), wd_s[...], preferred_element_type=F32)
        o_ref[...] = _pack_halves(y)


def _experts(xp, tile_expert, n_used, layer, w_gate, w_up, w_down):
    n_rows, half = xp.shape
    _, _, d, ff = w_gate.shape
    r = MOE_ROW_TILE
    row = pl.BlockSpec((r, half), lambda i, te, nu: (i, 0))
    grid_spec = pltpu.PrefetchScalarGridSpec(
        num_scalar_prefetch=2,
        grid=(n_rows // r,),
        in_specs=[row,
                  pl.BlockSpec((None, 1, d, ff), lambda i, te, nu: (layer, te[i], 0, 0)),
                  pl.BlockSpec((None, 1, d, ff), lambda i, te, nu: (layer, te[i], 0, 0)),
                  pl.BlockSpec((None, 1, ff, d), lambda i, te, nu: (layer, te[i], 0, 0))],
        out_specs=row,
        scratch_shapes=[pltpu.VMEM((d, ff), BF16), pltpu.VMEM((d, ff), BF16), pltpu.VMEM((ff, d), BF16)],
    )
    return pl.pallas_call(
        _expert_kernel,
        grid_spec=grid_spec,
        out_shape=jax.ShapeDtypeStruct((n_rows, half), jnp.int32),
        compiler_params=_params('arbitrary'),
        name='experts',
    )(tile_expert, n_used, xp, w_gate, w_up, w_down)


def _combine_kernel(yg_ref, w_ref, hp_ref, x_ref, g2_ref, sg_ref, su_ref, sd_ref, *rest, final):
    if final:
        fg_ref, o_ref = rest
    else:
        (o_ref,) = rest
    half = hp_ref.shape[2]
    lo, hi = _unpack_halves(hp_ref[0])
    sg, su = sg_ref[...], su_ref[...]
    a = jnp.dot(lo, sg[:half], preferred_element_type=F32) + jnp.dot(hi, sg[half:], preferred_element_type=F32)
    u = jnp.dot(lo, su[:half], preferred_element_type=F32) + jnp.dot(hi, su[half:], preferred_element_type=F32)
    acc = jnp.dot((_silu(a) * u).astype(BF16), sd_ref[...], preferred_element_type=F32)
    acc_lo, acc_hi = acc[:, :half], acc[:, half:]
    wts = w_ref[0]
    for k in range(yg_ref.shape[0]):
        ylo, yhi = _unpack_halves(yg_ref[k, 0])
        wk = wts[:, k:k + 1]
        acc_lo = acc_lo + wk * ylo.astype(F32)
        acc_hi = acc_hi + wk * yhi.astype(F32)
    y = x_ref[0] + g2_ref[0] * jnp.concatenate([acc_lo, acc_hi], axis=1)
    if final:
        y = _rms(y) * fg_ref[...]
    o_ref[0] = y


def _combine(yg, wts, hp, x, g2, s_gate, s_up, s_down, final_g=None, tm=256):
    b, s, d = x.shape
    k = yg.shape[0]
    half = d // 2
    row = pl.BlockSpec((1, tm, d), lambda i, j: (i, j, 0))
    prow = pl.BlockSpec((1, tm, half), lambda i, j: (i, j, 0))
    ins = [yg, wts, hp, x, g2, s_gate, s_up, s_down]
    in_specs = [pl.BlockSpec((k, 1, tm, half), lambda i, j: (0, i, j, 0)), pl.BlockSpec((1, tm, k), lambda i, j: (i, j, 0)),
                prow, row, pl.BlockSpec((1, 1, d), lambda i, j: (i, 0, 0)),
                _const_spec(s_gate.shape), _const_spec(s_up.shape), _const_spec(s_down.shape)]
    if final_g is not None:
        ins.append(final_g.reshape(1, d))
        in_specs.append(_const_spec((1, d)))
    return pl.pallas_call(
        functools.partial(_combine_kernel, final=final_g is not None),
        grid=(b, s // tm),
        in_specs=in_specs,
        out_specs=row,
        out_shape=jax.ShapeDtypeStruct(x.shape, F32),
        compiler_params=_params('parallel', 'parallel'),
        name='moe_combine',
    )(*ins)


def _routed_moe(x, g, scale, shift, g2, w_router, e_bias, layer, w_gate, w_up, w_down, s_gate, s_up, s_down, final_g=None):
    b, s, d = x.shape
    t = b * s
    hp, eid, rank, wts, counts = _router(x, g, scale, shift, w_router, e_bias, compact=True)
    counts = counts[:, 0].astype(jnp.int32)
    r = MOE_ROW_TILE
    padded = (counts + (r - 1)) // r * r
    ends = jnp.cumsum(padded)
    offsets = ends - padded
    n_rows = t * TOP_K + N_EXPERTS * r
    tile_start = jnp.arange(n_rows // r, dtype=jnp.int32) * r
    tile_expert = jnp.minimum(jnp.sum((tile_start[:, None] >= ends[None, :]).astype(jnp.int32), axis=1), N_EXPERTS - 1)
    n_used = (ends[-1] // r).reshape(1).astype(jnp.int32)
    pos = _route_pos(offsets.astype(jnp.int32), eid, rank)
    xp = _sc_dispatch(hp.reshape(t, d // 2), pos, n_rows)
    yp = _experts(xp, tile_expert.astype(jnp.int32), n_used, layer, w_gate, w_up, w_down)
    yg = _sc_collect(yp, pos).reshape(TOP_K, b, s, d // 2)
    return _combine(yg, wts.T.reshape(b, s, TOP_K), hp, x, g2, s_gate, s_up, s_down, final_g)


def _mixers(p, pc, ctx_out, prm, l, lam_init, rope_tabs, lb_terms):
    s = p['hy_v'].shape[1]
    sc = pc['hy_v'].shape[1]

    hy_args = (prm['hy_w1'][l], prm['hy_b1'][l], prm['hy_w2'][l], prm['hy_b2'][l], prm['hy_w3'][l], prm['hy_b3'][l],
               prm['hy_sin_freq'][l], prm['hy_decay'][l])
    y_hy = _hyena([p['hy_v'], p['hy_x1'], p['hy_x2']], prm['hy_conv_w'][l], prm['hy_conv_b'][l],
                  _hy_filters(s, *hy_args), prm['hy_bias'][l], inner=128)
    yc_hy = None
    if ctx_out:
        yc_hy = _hyena([pc['hy_v'], pc['hy_x1'], pc['hy_x2']], prm['hy_conv_w'][l], prm['hy_conv_b'][l],
                       _hy_filters(sc, *hy_args), prm['hy_bias'][l], inner=32)

    lp = prm['da_lambda'][l].astype(F32)
    lam = jnp.exp(jnp.sum(lp[0] * lp[1])) - jnp.exp(jnp.sum(lp[2] * lp[3])) + lam_init
    da_kw = dict(heads=DA_HEADS, ncomp=2, scale=DA_HEAD_DIM ** -0.5, lam=lam, subln_g=prm['da_subln_g'][l],
                 post_scale=1.0 - lam_init, tq=(128, 512)[l])
    da_ctx = ([pc['da_k']], pc['da_v'])
    y_da = _attention([p['da_q']], [da_ctx, ([p['da_k']], p['da_v'])], **da_kw)
    yc_da = _attention([pc['da_q']], [da_ctx], **da_kw) if ctx_out else None

    wq = prm['mla_w_q_up'][l].reshape(MLA_Q_RANK, MLA_HEADS, MLA_NOPE_DIM + MLA_ROPE_DIM)
    wq_n = wq[:, :, :MLA_NOPE_DIM].reshape(MLA_Q_RANK, -1).astype(BF16)
    wq_r = wq[:, :, MLA_NOPE_DIM:].reshape(MLA_Q_RANK, -1).astype(BF16)
    wkv = prm['mla_w_kv_up'][l].reshape(MLA_KV_RANK, MLA_HEADS, MLA_NOPE_DIM + MLA_V_DIM)
    wkv_n = wkv[:, :, :MLA_NOPE_DIM].reshape(MLA_KV_RANK, -1).astype(BF16)
    wkv_v = wkv[:, :, MLA_NOPE_DIM:].reshape(MLA_KV_RANK, -1).astype(BF16)

    def queries(qd, tabs):
        return _norm_proj(qd, prm['mla_q_norm_g'][l], [(wq_n, F32, False, MLA_HEADS), (wq_r, F32, True, MLA_HEADS)],
                          rope_tabs=tabs)

    def keys_values(kvd):
        return _norm_proj(kvd, prm['mla_kv_norm_g'][l], [(wkv_n, BF16, False, MLA_HEADS), (wkv_v, BF16, False, MLA_HEADS)])

    kn_l, v_l = keys_values(p['mla_kv'])
    kn_c, v_c = keys_values(pc['mla_kv'])
    mla_kw = dict(heads=MLA_HEADS, ncomp=1, scale=(MLA_NOPE_DIM + MLA_ROPE_DIM) ** -0.5, tq=(128, 512)[l])
    mla_ctx = ([kn_c, pc['mla_kr']], v_c)
    y_mla = _attention(queries(p['mla_q'], rope_tabs), [mla_ctx, ([kn_l, p['mla_kr']], v_l)], **mla_kw)
    yc_mla = _attention(queries(pc['mla_q'], None), [mla_ctx], **mla_kw) if ctx_out else None

    o, oc = _hgrn(p['hg_q'], p['hg_ff'], p['hg_fb'], p['hg_i'], pc['hg_q'], pc['hg_ff'], pc['hg_fb'], pc['hg_i'],
                  lb_terms, prm['hg_norm_g'][l])
    return (y_hy, y_da, y_mla, o), (yc_hy, yc_da, yc_mla, oc)


def kernel(x, c, ctx, c_ctx, w_ada, b_ada, norm1_g, norm2_g, w_in, w_out, hy_conv_w, hy_conv_b, hy_w1, hy_b1, hy_w2, hy_b2, hy_w3, hy_b3, hy_sin_freq, hy_decay, hy_bias, da_lambda, da_subln_g, mla_q_norm_g, mla_w_q_up, mla_kv_norm_g, mla_w_kv_up, hg_lower_bounds, hg_norm_g, moe_w_router, moe_bias, moe_w_gate, moe_w_up, moe_w_down, moe_sh_gate, moe_sh_up, moe_sh_down, final_norm_g):
    prm = dict(hy_conv_w=hy_conv_w, hy_conv_b=hy_conv_b, hy_w1=hy_w1, hy_b1=hy_b1, hy_w2=hy_w2, hy_b2=hy_b2,
               hy_w3=hy_w3, hy_b3=hy_b3, hy_sin_freq=hy_sin_freq, hy_decay=hy_decay, hy_bias=hy_bias,
               da_lambda=da_lambda, da_subln_g=da_subln_g, mla_q_norm_g=mla_q_norm_g, mla_w_q_up=mla_w_q_up,
               mla_kv_norm_g=mla_kv_norm_g, mla_w_kv_up=mla_w_kv_up, hg_norm_g=hg_norm_g)
    b, n_lat, d = x.shape
    depth = w_in.shape[0]
    rows = n_lat // GRID_W
    row_pos = jnp.repeat(jnp.arange(rows, dtype=jnp.int32), GRID_W)
    col_pos = jnp.tile(jnp.arange(GRID_W, dtype=jnp.int32), rows)
    rope_tabs = _rope_tables(row_pos, col_pos, 2 * DA_HEADS * DA_HEAD_DIM)
    lbs = jnp.cumsum(jax.nn.softmax(hg_lower_bounds.astype(F32), axis=1), axis=1)
    lbs = lbs - lbs[:, :1]
    cond = jnp.concatenate([c, c_ctx[None], jnp.zeros((8 - b - 1, d), F32)], axis=0)

    for l in range(depth):
        ctx_out = l < depth - 1
        mods = _ada(cond, w_ada[l], b_ada[l])
        sh1, sc1, g1, sh2, sc2, g2 = [m[:, None, :] for m in jnp.split(mods[:b], 6, axis=-1)]
        mc = [jnp.broadcast_to(m[:, None, :], (b, 1, d)) for m in jnp.split(mods[b:b + 1], 6, axis=-1)]

        off = 0
        outs = []
        for _, wdt, dt, rope, split, rep in _SEGMENTS:
            w = w_in[l][:, off:off + wdt].astype(BF16)
            outs.append((jnp.tile(w, (1, rep)) if rep > 1 else w, dt, rope, split))
            off += wdt
        names = [seg[0] for seg in _SEGMENTS]
        p = dict(zip(names, _norm_proj(x, norm1_g[l], outs, sc1, sh1, rope_tabs=rope_tabs)))
        pc = dict(zip(names, _norm_proj(ctx, norm1_g[l], outs, mc[1], mc[0])))

        lb = lbs[:, l]
        lb_terms = jnp.stack([jnp.log(lb), jnp.log1p(-lb), 1.0 - lb], axis=1)
        lam_init = 0.8 - 0.6 * math.exp(-0.3 * l)
        lat_parts, ctx_parts = _mixers(p, pc, ctx_out, prm, l, lam_init, rope_tabs, lb_terms)

        w_out_b = w_out[l].astype(BF16)
        moe_w = (l, moe_w_gate, moe_w_up, moe_w_down,
                 moe_sh_gate[l].astype(BF16), moe_sh_up[l].astype(BF16), moe_sh_down[l].astype(BF16))

        if ctx_out:
            ctx = _out_proj(*ctx_parts, pc['hg_g'], ctx, mc[2], w_out_b)
            flat = ctx.reshape(1, -1, d)
            h2c, gate_c = _router(flat, norm2_g[l], mc[4][:1], mc[3][:1], moe_w_router[l], moe_bias[l])
            ctx = _moe(h2c, flat, gate_c.transpose(0, 2, 1), mc[5][:1], *moe_w).reshape(ctx.shape)

        x = _out_proj(*lat_parts, p['hg_g'], x, g1, w_out_b)
        x = _routed_moe(x, norm2_g[l], sc2, sh2, g2, moe_w_router[l], moe_bias[l], *moe_w,
                        final_g=None if ctx_out else final_norm_g)

    return x
```

```python
import functools
import math

import numpy as np
import jax
import jax.numpy as jnp
from jax import lax
from jax.experimental import pallas as pl
from jax.experimental.pallas import tpu as pltpu
from jax.experimental.pallas import tpu_sc as plsc

F32 = jnp.float32
BF16 = jnp.bfloat16
HIGHEST = lax.Precision.HIGHEST

D_MODEL = 1024
GRID_W = 64
HY_WIDTH = 256
HY_ORDER = 2
HY_BANDS = 16
DA_HEADS = 4
DA_HEAD_DIM = 32
MLA_HEADS = 4
MLA_Q_RANK = 192
MLA_KV_RANK = 128
MLA_NOPE_DIM = 64
MLA_ROPE_DIM = 32
MLA_V_DIM = 64
HG_HEADS = 4
HG_KEY_DIM = 64
HG_VAL_DIM = 64
HG_CHUNK = 64
HG_SUB = 8
N_EXPERTS = 64
N_EXPERT_GROUPS = 8
TOPK_GROUPS = 4
TOP_K = 8
EXPERT_FF = 256
ROUTED_SCALE = 2.5
ROPE_BASE = 10000.0
NORM_EPS = 1e-6

V7X_VMEM_LIMIT_BYTES = 56 * 1024 * 1024
LANES = 128

_SEGMENTS = (
    ('hy_v', HY_WIDTH, F32, False, 0, 1), ('hy_x1', HY_WIDTH, F32, False, 0, 1), ('hy_x2', HY_WIDTH, F32, False, 0, 1),
    ('da_q', 2 * DA_HEADS * DA_HEAD_DIM, F32, True, 2 * DA_HEADS, 1),
    ('da_k', 2 * DA_HEADS * DA_HEAD_DIM, BF16, True, 2 * DA_HEADS, 1),
    ('da_v', 2 * DA_HEADS * DA_HEAD_DIM, BF16, False, DA_HEADS, 1),
    ('mla_q', MLA_Q_RANK, F32, False, 0, 1), ('mla_kv', MLA_KV_RANK, F32, False, 0, 1),
    ('mla_kr', MLA_ROPE_DIM, BF16, True, MLA_HEADS, MLA_HEADS),
    ('hg_q', HG_HEADS * HG_KEY_DIM, F32, False, 0, 1), ('hg_ff', HG_HEADS * HG_KEY_DIM, F32, False, 0, 1),
    ('hg_fb', HG_HEADS * HG_KEY_DIM, F32, False, 0, 1), ('hg_i', HG_HEADS * HG_VAL_DIM, F32, False, 0, 1),
    ('hg_g', HG_HEADS * HG_VAL_DIM, F32, False, 0, 1),
)


def _params(*semantics):
    return pltpu.CompilerParams(dimension_semantics=semantics, vmem_limit_bytes=V7X_VMEM_LIMIT_BYTES)


def _const_spec(shape):
    nd = len(shape)
    return pl.BlockSpec(shape, lambda *_: (0,) * nd)


def _rms(x, eps=NORM_EPS):
    return x * lax.rsqrt(jnp.mean(x * x, axis=-1, keepdims=True) + eps)


def _silu(x):
    return x * jax.nn.sigmoid(x)


def _dot_nt(a, b, **kw):
    return lax.dot_general(a, b, (((1,), (1,)), ((), ())), preferred_element_type=F32, **kw)


def _ada_kernel(c_ref, w_ref, b_ref, o_ref):
    s = _silu(c_ref[...])
    o_ref[...] = jnp.dot(s, w_ref[...], precision=HIGHEST, preferred_element_type=F32) + b_ref[...]


def _ada(cond, w, b):
    r, d = cond.shape
    n = w.shape[1]
    tn = 1536
    return pl.pallas_call(
        _ada_kernel,
        grid=(n // tn,),
        in_specs=[_const_spec((r, d)), pl.BlockSpec((d, tn), lambda j: (0, j)), pl.BlockSpec((1, tn), lambda j: (0, j))],
        out_specs=pl.BlockSpec((r, tn), lambda j: (0, j)),
        out_shape=jax.ShapeDtypeStruct((r, n), F32),
        compiler_params=_params('arbitrary'),
        name='ada',
    )(cond, w, b.reshape(1, n))


ROPE_UNIT = 32


def _rope_tables(row, col, width):
    n = ROPE_UNIT // 4
    inv = ROPE_BASE ** (-jnp.arange(n, dtype=F32) / n)
    units = width // ROPE_UNIT
    parts_c, parts_a, parts_b = [], [], []
    zero = jnp.zeros((row.shape[0], n), F32)
    for pos in (row, col):
        ang = pos.astype(F32)[:, None] * inv
        cos, sin = jnp.cos(ang), jnp.sin(ang)
        parts_c += [cos, cos]
        parts_a += [zero, sin]
        parts_b += [-sin, zero]
    tile = lambda ps: jnp.tile(jnp.concatenate(ps, axis=1), (1, units))
    return tile(parts_c), tile(parts_a), tile(parts_b)


def _norm_proj_kernel(*refs, n_w, modulate, ropes, splits):
    x_ref, g_ref = refs[0], refs[1]
    pos = 2
    if modulate:
        sc_ref, sh_ref = refs[2], refs[3]
        pos = 4
    if any(ropes):
        rc_ref, ra_ref, rb_ref = refs[pos:pos + 3]
        pos += 3
    w_refs = refs[pos:pos + n_w]
    o_refs = refs[pos + n_w:]
    y = _rms(x_ref[0]) * g_ref[...]
    if modulate:
        y = y * (1.0 + sc_ref[0]) + sh_ref[0]
    yb = y.astype(BF16)
    for w_ref, o_ref, rope, split in zip(w_refs, o_refs, ropes, splits):
        o = jnp.dot(yb, w_ref[...], preferred_element_type=F32)
        if rope:
            wd = o.shape[1]
            shift = ROPE_UNIT // 4
            o = (o * rc_ref[:, :wd] + pltpu.roll(o, shift, axis=1) * ra_ref[:, :wd]
                 + pltpu.roll(o, wd - shift, axis=1) * rb_ref[:, :wd])
        if split:
            unit = o.shape[1] // split
            for u in range(split):
                o_ref[0, u] = o[:, u * unit:(u + 1) * unit].astype(o_ref.dtype)
        else:
            o_ref[0] = o.astype(o_ref.dtype)


def _norm_proj(x, g, outs, scale=None, shift=None, rope_tabs=None, tm=512):
    b, s, k = x.shape
    tm = min(tm, s)
    modulate = scale is not None
    ropes = tuple(bool(o[2]) and rope_tabs is not None for o in outs)
    splits = tuple(o[3] for o in outs)
    ins = [x, g.reshape(1, k)]
    in_specs = [pl.BlockSpec((1, tm, k), lambda i, j: (i, j, 0)), _const_spec((1, k))]
    if modulate:
        ins += [scale, shift]
        in_specs += [pl.BlockSpec((1, 1, k), lambda i, j: (i, 0, 0))] * 2
    if any(ropes):
        ins += list(rope_tabs)
        in_specs += [pl.BlockSpec((tm, rope_tabs[0].shape[1]), lambda i, j: (j, 0))] * 3
    out_specs, out_shape = [], []
    for w, dt, _, split in outs:
        ins.append(w)
        in_specs.append(_const_spec(w.shape))
        n = w.shape[1]
        if split:
            out_specs.append(pl.BlockSpec((1, split, tm, n // split), lambda i, j: (i, 0, j, 0)))
            out_shape.append(jax.ShapeDtypeStruct((b, split, s, n // split), dt))
        else:
            out_specs.append(pl.BlockSpec((1, tm, n), lambda i, j: (i, j, 0)))
            out_shape.append(jax.ShapeDtypeStruct((b, s, n), dt))
    return pl.pallas_call(
        functools.partial(_norm_proj_kernel, n_w=len(outs), modulate=modulate, ropes=ropes, splits=splits),
        grid=(b, s // tm),
        in_specs=in_specs,
        out_specs=out_specs,
        out_shape=out_shape,
        compiler_params=_params('parallel', 'parallel'),
        name='norm_proj',
    )(*ins)


def _hy_filter_kernel(w1t_ref, w1s_ref, w1c_ref, b1_ref, w2_ref, b2_ref, w3_ref, b3_ref, fr_ref, dec_ref, o_ref, *, n):
    t = lax.broadcasted_iota(jnp.int32, (n, 1), 0).astype(F32) / n
    bands = lax.broadcasted_iota(jnp.int32, (1, HY_BANDS), 1).astype(F32) + 1.0
    ang = (2.0 * jnp.pi) * t * bands
    pre = (t * w1t_ref[...]
           + jnp.dot(jnp.sin(ang), w1s_ref[...], precision=HIGHEST, preferred_element_type=F32)
           + jnp.dot(jnp.cos(ang), w1c_ref[...], precision=HIGHEST, preferred_element_type=F32)
           + b1_ref[...])
    hid = jnp.sin(fr_ref[0:1, :] * pre)
    hid = jnp.sin(fr_ref[1:2, :] * (jnp.dot(hid, w2_ref[...], precision=HIGHEST, preferred_element_type=F32) + b2_ref[...]))
    filt = jnp.dot(hid, w3_ref[...], precision=HIGHEST, preferred_element_type=F32) + b3_ref[...]
    filt = filt * jnp.exp(-t * jnp.abs(dec_ref[...]))
    col = jnp.sum(jnp.abs(filt), axis=0, keepdims=True) - jnp.abs(filt[0:1, :])
    w = HY_WIDTH
    for o in range(HY_ORDER):
        lo = o * 2 * w
        f0 = filt[0:1, lo:lo + w] + filt[0:1, lo + w:lo + 2 * w]
        inv = 1.0 / (col[:, lo:lo + w] + col[:, lo + w:lo + 2 * w] + jnp.abs(f0))
        o_ref[:, lo:lo + w] = filt[:, lo:lo + w] * inv
        o_ref[:, lo + w:lo + 2 * w] = filt[:, lo + w:lo + 2 * w] * inv


def _hy_filters(n, w1, b1, w2, b2, w3, b3, freq, decay):
    cols = w3.shape[1]
    ins = [w1[0:1], w1[1:1 + HY_BANDS], w1[1 + HY_BANDS:], b1.reshape(1, -1), w2, b2.reshape(1, -1), w3,
           b3.reshape(1, -1), freq, decay.reshape(1, -1)]
    out = pl.pallas_call(
        functools.partial(_hy_filter_kernel, n=n),
        grid=(1,),
        in_specs=[_const_spec(a.shape) for a in ins],
        out_specs=_const_spec((n, cols)),
        out_shape=jax.ShapeDtypeStruct((n, cols), F32),
        compiler_params=_params('arbitrary'),
        name='hy_filter',
    )(*ins)
    return out.reshape(n, HY_ORDER, 2, HY_WIDTH)


def _two_sided(filt_n):
    n = filt_n.shape[0]
    hf, hb = filt_n[:, :, 0], filt_n[:, :, 1]
    h = jnp.concatenate([hf[:1] + hb[:1], hf[1:], jnp.zeros((1,) + hf.shape[1:], F32), hb[:0:-1]], axis=0)
    return h.reshape(2 * n, HY_ORDER * HY_WIDTH)


def _short_conv_kernel(*refs, s):
    x_refs, w_refs, b_refs, o_refs = refs[0:3], refs[3:6], refs[6:9], refs[9:12]
    row = lax.broadcasted_iota(jnp.int32, (s, 1), 0)
    for x_ref, w_ref, b_ref, o_ref in zip(x_refs, w_refs, b_refs, o_refs):
        x = x_ref[0]
        prev = jnp.where(row == 0, 0.0, pltpu.roll(x, 1, axis=0))
        nxt = jnp.where(row == s - 1, 0.0, pltpu.roll(x, s - 1, axis=0))
        o_ref[0] = prev * w_ref[0:1, :] + x * w_ref[1:2, :] + nxt * w_ref[2:3, :] + b_ref[...]


def _short_conv(parts, conv_w, conv_b):
    b, s, c = parts[0].shape
    tc = LANES
    ws = [conv_w[:, i * c:(i + 1) * c] for i in range(3)]
    bs = [conv_b[i * c:(i + 1) * c].reshape(1, c) for i in range(3)]
    xspec = pl.BlockSpec((1, s, tc), lambda i, j: (i, 0, j))
    return pl.pallas_call(
        functools.partial(_short_conv_kernel, s=s),
        grid=(b, c // tc),
        in_specs=[xspec] * 3 + [pl.BlockSpec((3, tc), lambda i, j: (0, j))] * 3 + [pl.BlockSpec((1, tc), lambda i, j: (0, j))] * 3,
        out_specs=[xspec] * 3,
        out_shape=[jax.ShapeDtypeStruct((b, s, c), F32)] * 3,
        compiler_params=_params('parallel', 'parallel'),
        name='short_conv',
    )(*parts, *ws, *bs)


def _dft_cos_sin(rows, cols, period):
    ang = 2.0 * np.pi * ((np.arange(rows)[:, None] * np.arange(cols)[None, :]) % period) / period
    return np.cos(ang), np.sin(ang)


def _fft_tables(n, inner):
    big = 2 * n
    n1 = big // inner
    c1, s1 = _dft_cos_sin(n1, n1, n1)
    h = n1 // 2
    outer_data = np.block([[c1[:, :h], s1[:, :h]], [-s1[:, :h], c1[:, :h]]])
    outer_real = np.concatenate([c1, -s1], axis=0)
    outer_inv = np.block([[c1[:h, :], -s1[:h, :]], [s1[:h, :], c1[:h, :]]]) / big
    c2, s2 = _dft_cos_sin(inner, inner, inner)
    inner_fwd = np.block([[c2, s2], [-s2, c2]])
    inner_inv = np.block([[c2, -s2], [s2, c2]])
    ct, st = _dft_cos_sin(n1, inner, big)
    f = lambda a: jnp.asarray(a, F32)
    return dict(n1=n1, inner=inner, outer_data=f(outer_data), outer_real=f(outer_real), outer_inv=f(outer_inv),
                inner_fwd=_hi_lo_cols(inner_fwd), inner_inv=_hi_lo_cols(inner_inv),
                tw_cos=f(ct).reshape(n1, inner, 1), tw_sin=f(st).reshape(n1, inner, 1))


def _left_mm_kernel(m_ref, x_ref, o_ref):
    o_ref[0] = jnp.dot(m_ref[...], x_ref[0], precision=HIGHEST, preferred_element_type=F32)


def _left_mm(m, x, tl=4096):
    p, k, l = x.shape
    mm = m.shape[0]
    tl = min(tl, l)
    return pl.pallas_call(
        _left_mm_kernel,
        grid=(p, l // tl),
        in_specs=[_const_spec(m.shape), pl.BlockSpec((1, k, tl), lambda i, j: (i, 0, j))],
        out_specs=pl.BlockSpec((1, mm, tl), lambda i, j: (i, 0, j)),
        out_shape=jax.ShapeDtypeStruct((p, mm, l), F32),
        compiler_params=_params('parallel', 'parallel'),
        name='fft_outer',
    )(m, x)


def _hi_lo_cols(m):
    m = np.asarray(m, np.float32)
    hi = m.astype(BF16)
    lo = (m - hi.astype(np.float32)).astype(BF16)
    return jnp.asarray(np.concatenate([hi, hi, lo], axis=1))


def _hi_lo_rows(x):
    hi = x.astype(BF16)
    lo = (x - hi.astype(F32)).astype(BF16)
    return jnp.concatenate([hi, lo, hi], axis=0)


def _inner_kernel(a_ref, twc_ref, tws_ref, gf_ref, *rest, convolve, inner, kb):
    for s in range(kb):
        ar, ai = a_ref[0, 0, s], a_ref[0, 1, s]
        tc, ts = twc_ref[s], tws_ref[s]
        br = ar * tc + ai * ts
        bi = ai * tc - ar * ts
        x = jnp.dot(gf_ref[...], _hi_lo_rows(jnp.concatenate([br, bi], axis=0)), preferred_element_type=F32)
        if not convolve:
            o_ref = rest[0]
            o_ref[0, 0, s] = x[:inner]
            o_ref[0, 1, s] = x[inner:]
            continue
        h_ref, gi_ref, o_ref = rest
        xr, xi = x[:inner], x[inner:]
        hr, hi = h_ref[0, 0, s], h_ref[0, 1, s]
        yr = xr * hr - xi * hi
        yi = xr * hi + xi * hr
        z = jnp.dot(gi_ref[...], _hi_lo_rows(jnp.concatenate([yr, yi], axis=0)), preferred_element_type=F32)
        zr, zi = z[:inner], z[inner:]
        o_ref[0, 0, s] = zr * tc - zi * ts
        o_ref[0, 1, s] = zi * tc + zr * ts


def _fft_inner(a, tab, c, h=None, h_block=0):
    p = a.shape[0]
    n1, inner = tab['n1'], tab['inner']
    a5 = a.reshape(p, 2, n1, inner, c)
    tc = 2 * LANES
    kb = 4
    blk = pl.BlockSpec((1, 2, kb, inner, tc), lambda i, k, j: (i, 0, k, 0, j))
    tw_spec = pl.BlockSpec((kb, inner, 1), lambda i, k, j: (k, 0, 0))
    ins = [a5, tab['tw_cos'], tab['tw_sin'], tab['inner_fwd']]
    in_specs = [blk, tw_spec, tw_spec, _const_spec(tab['inner_fwd'].shape)]
    if h is not None:
        ch = h.shape[-1] // inner
        nb = c // tc
        ins += [h.reshape(1, 2, n1, inner, ch), tab['inner_inv']]
        in_specs += [pl.BlockSpec((1, 2, kb, inner, tc), lambda i, k, j: (0, 0, k, 0, h_block * nb + j)),
                     _const_spec(tab['inner_inv'].shape)]
    out = pl.pallas_call(
        functools.partial(_inner_kernel, convolve=h is not None, inner=inner, kb=kb),
        grid=(p, n1 // kb, c // tc),
        in_specs=in_specs,
        out_specs=blk,
        out_shape=jax.ShapeDtypeStruct(a5.shape, F32),
        compiler_params=_params('parallel', 'parallel', 'parallel'),
        name='fft_inner',
    )(*ins)
    return out.reshape(p, 2 * n1, inner * c)


def _gate_kernel(m_ref, z_ref, u_ref, x_ref, bias_ref, *rest, chain):
    y = jnp.dot(m_ref[...], z_ref[0], precision=HIGHEST, preferred_element_type=F32)
    nxt = x_ref[0] * (y + u_ref[0] * bias_ref[...])
    if chain:
        mf_ref, o_ref, a_ref = rest
        o_ref[0] = nxt
        a_ref[0] = jnp.dot(mf_ref[...], nxt, precision=HIGHEST, preferred_element_type=F32)
    else:
        rest[0][0] = nxt


def _fft_gate(tab, z, u, x, bias_l, chain, tl=4096):
    p, k2, l = z.shape
    n1 = tab['n1']
    tl = min(tl, l)
    row = pl.BlockSpec((1, n1, tl), lambda i, j: (i, 0, j))
    ins = [tab['outer_inv'], z, u, x, bias_l]
    in_specs = [_const_spec((n1, k2)), pl.BlockSpec((1, k2, tl), lambda i, j: (i, 0, j)), row, row,
                pl.BlockSpec((1, tl), lambda i, j: (0, j))]
    out_specs = [row]
    out_shape = [jax.ShapeDtypeStruct((p, n1, l), F32)]
    if chain:
        ins.append(tab['outer_data'])
        in_specs.append(_const_spec((k2, n1)))
        out_specs.append(pl.BlockSpec((1, k2, tl), lambda i, j: (i, 0, j)))
        out_shape.append(jax.ShapeDtypeStruct((p, k2, l), F32))
    return pl.pallas_call(
        functools.partial(_gate_kernel, chain=chain),
        grid=(p, l // tl),
        in_specs=in_specs,
        out_specs=out_specs,
        out_shape=out_shape,
        compiler_params=_params('parallel', 'parallel'),
        name='fft_gate',
    )(*ins)


def _hyena(parts, conv_w, conv_b, filt_n, bias, inner):
    b, s, c = parts[0].shape
    tab = _fft_tables(s, inner)
    n1 = tab['n1']
    lanes = inner * c
    h_taps = _two_sided(filt_n).reshape(1, n1, inner * HY_ORDER * c)
    h_spec = _fft_inner(_left_mm(tab['outer_real'], h_taps), tab, HY_ORDER * c)
    v, x1, x2 = [a.reshape(b // 2, n1, lanes) for a in _short_conv(parts, conv_w, conv_b)]
    bias_l = [jnp.tile(bias[o], inner).reshape(1, lanes) for o in range(HY_ORDER)]
    a = _left_mm(tab['outer_data'], v)
    z = _fft_inner(a, tab, c, h_spec, 0)
    z2, a = _fft_gate(tab, z, v, x1, bias_l[0], chain=True)
    z = _fft_inner(a, tab, c, h_spec, 1)
    (z3,) = _fft_gate(tab, z, z2, x2, bias_l[1], chain=False)
    return z3.reshape(b, s, c)


def _attn_kernel(*refs, n_q, n_pieces, ncomp, scale, post_scale):
    q_refs = refs[:n_q]
    pos = n_q
    pieces = []
    for _ in range(n_pieces):
        pieces.append((refs[pos:pos + n_q], refs[pos + n_q]))
        pos += n_q + 1
    if ncomp == 2:
        lam_ref, g_ref = refs[pos:pos + 2]
        pos += 2
    o_ref, kcat_ref, vcat_ref = refs[pos:pos + 3]

    @pl.when(pl.program_id(2) == 0)
    def _():
        row = 0
        for k_refs, v_ref in pieces:
            n = v_ref.shape[2]
            for c in range(ncomp):
                parts = [k_ref[0, c if k_ref.shape[1] == ncomp else 0] for k_ref in k_refs]
                kcat_ref[c, row:row + n, :] = parts[0] if n_q == 1 else jnp.concatenate(parts, axis=1)
            vcat_ref[row:row + n, :] = v_ref[0, 0]
            row += n

    outs = []
    for c in range(ncomp):
        q = q_refs[0][0, c] if n_q == 1 else jnp.concatenate([q_ref[0, c] for q_ref in q_refs], axis=1)
        s = _dot_nt((q * (scale * math.log2(math.e))).astype(BF16), kcat_ref[c])
        m = jnp.max(s, axis=-1, keepdims=True)
        p = jnp.exp2(s - m)
        l = jnp.sum(p, axis=-1, keepdims=True)
        outs.append(jnp.dot(p.astype(BF16), vcat_ref[...], preferred_element_type=F32) / l)
    if ncomp == 2:
        o = outs[0] - lam_ref[0] * outs[1]
        o = _rms(o) * g_ref[...] * post_scale
    else:
        o = outs[0]
    o_ref[0, 0] = o


def _attention(q_parts, pieces, heads, ncomp, scale, tq=256, lam=None, subln_g=None, post_scale=1.0):
    b, _, sq, _ = q_parts[0].shape
    dv = pieces[0][1].shape[3]
    tq = min(tq, sq)
    ins = list(q_parts)
    in_specs = [pl.BlockSpec((1, ncomp, tq, q.shape[3]), lambda i, h, j: (i, h, j, 0)) for q in q_parts]
    for k_parts, v in pieces:
        for k in k_parts:
            ins.append(k)
            if k.shape[1] == 1:
                in_specs.append(pl.BlockSpec((1, 1) + k.shape[2:], lambda i, h, j: (i, 0, 0, 0)))
            else:
                in_specs.append(pl.BlockSpec((1, ncomp) + k.shape[2:], lambda i, h, j: (i, h, 0, 0)))
        ins.append(v)
        in_specs.append(pl.BlockSpec((1, 1) + v.shape[2:], lambda i, h, j: (i, h, 0, 0)))
    if ncomp == 2:
        ins += [lam.reshape(1), subln_g.reshape(1, dv)]
        in_specs += [pl.BlockSpec(memory_space=pltpu.SMEM), _const_spec((1, dv))]
    sk = sum(v.shape[2] for _, v in pieces)
    dqk = sum(q.shape[3] for q in q_parts)
    return pl.pallas_call(
        functools.partial(_attn_kernel, n_q=len(q_parts), n_pieces=len(pieces), ncomp=ncomp, scale=scale,
                          post_scale=post_scale),
        grid=(b, heads, sq // tq),
        in_specs=in_specs,
        out_specs=pl.BlockSpec((1, 1, tq, dv), lambda i, h, j: (i, h, j, 0)),
        out_shape=jax.ShapeDtypeStruct((b, heads, sq, dv), F32),
        scratch_shapes=[pltpu.VMEM((ncomp, sk, dqk), BF16), pltpu.VMEM((sk, dv), BF16)],
        compiler_params=_params('parallel', 'parallel', 'arbitrary'),
        name='attention',
    )(*ins)


def _forget_terms(f, log_lb, log_1m_lb, one_m_lb):
    log_sig = jnp.minimum(f, 0.0) - jnp.log1p(jnp.exp(-jnp.abs(f)))
    b = log_1m_lb + log_sig
    log_g = jnp.maximum(log_lb, b) + jnp.log1p(jnp.exp(-jnp.abs(log_lb - b)))
    return log_g, one_m_lb * jax.nn.sigmoid(-f)


def _hg_tables():
    ck, sub = HG_CHUNK, HG_SUB
    t = np.arange(ck)
    cum_mats, half_masks, group_masks, sels = [], [], [], []
    for rev in (False, True):
        mats = [(t[None, :] >= t[:, None]) if rev else (t[None, :] <= t[:, None])]
        halves = []
        hs = ck // 2
        while hs >= sub:
            pos = t % (2 * hs)
            b = t - pos + hs
            mats.append((t[None, :] >= b[:, None]) if rev else (t[None, :] < b[:, None]))
            q_half = (pos < hs) if rev else (pos >= hs)
            halves.append(np.stack([q_half, ~q_half]))
            if not rev:
                grp = (t[:, None] // (2 * hs)) == (t[None, :] // (2 * hs))
                group_masks.append(np.concatenate([grp, grp], axis=0))
            hs //= 2
        cum_mats.append(np.concatenate(mats, axis=0))
        half_masks.append(np.stack(halves))
        r, c = np.arange(ck)[:, None], np.arange(ck * sub)[None, :]
        same = (c // sub) == r
        tt, ss = (c // sub) % sub, c % sub
        sels.append(same & ((ss >= tt) if rev else (ss <= tt)))
    lanes = 2 * HG_KEY_DIM
    ln = np.arange(lanes)
    bd = (ln[:, None] // HG_KEY_DIM) == (ln[None, :] // HG_KEY_DIM)
    hm = np.broadcast_to(np.stack(half_masks)[..., None], (2, len(half_masks[0]), 2, ck, lanes))
    return (jnp.asarray(np.stack(cum_mats), BF16), jnp.asarray(hm, F32), jnp.asarray(np.stack(group_masks), F32),
            jnp.asarray(np.stack(sels), BF16), jnp.asarray(bd, F32))


def _split3(x):
    a = x.astype(BF16)
    r = x - a.astype(F32)
    b = r.astype(BF16)
    return a, b, (r - b.astype(F32)).astype(BF16)


def _hg_chunk(q, k, v, lg, st, rev, cm, hm, gm, sel, bd, m0, m1):
    ck, sub = HG_CHUNK, HG_SUB
    call = sum(jnp.dot(cm, piece, preferred_element_type=F32) for piece in _split3(lg))
    cum = call[0:ck]
    tot = cum[0:1] if rev else cum[ck - 1:ck]
    o = _dot_nt((q * jnp.exp(cum)).astype(BF16), st.astype(BF16))
    kd = (k * jnp.exp(tot - cum)).astype(BF16)
    st_new = st * jnp.exp(tot) + bd * jnp.dot(v.T.astype(BF16), kd, preferred_element_type=F32)
    s2 = None
    for lv in range(gm.shape[0]):
        cb = call[(lv + 1) * ck:(lv + 2) * ck]
        qd = q * jnp.exp(jnp.minimum(cum - cb, 0.0)) * hm[lv, 0]
        kf = (k * jnp.exp(jnp.minimum(cb - cum, 0.0)) * hm[lv, 1]).astype(BF16)
        q2 = jnp.concatenate([qd * m0, qd * m1], axis=0).astype(BF16)
        term = _dot_nt(q2, kf) * gm[lv]
        s2 = term if s2 is None else s2 + term
    r = jnp.dot(s2.astype(BF16), v.astype(BF16), preferred_element_type=F32)
    o = o + m0 * r[:ck] + m1 * r[ck:]
    rows, vts = [], []
    for i in range(ck // sub):
        lo, hi = i * sub, (i + 1) * sub
        ki, ci = k[lo:hi], cum[lo:hi]
        for t in range(lo, hi):
            rows.append((q[t:t + 1] * ki * jnp.exp(jnp.minimum(cum[t:t + 1] - ci, 0.0))).astype(BF16))
            vts.append(v[lo:hi])
    sc = jnp.dot(jnp.concatenate(rows, axis=0), bd.astype(BF16), preferred_element_type=F32)
    o = o + jnp.dot(sel, (sc * jnp.concatenate(vts, axis=0)).astype(BF16), preferred_element_type=F32)
    return o, st_new


def _hgrn_kernel(q_ref, ff_ref, fb_ref, i_ref, qc_ref, ffc_ref, fbc_ref, ic_ref, lb_ref, g_ref,
                 cm_ref, hm_ref, gm_ref, sel_ref, bd_ref, o_ref, oc_ref, or_ref, ocr_ref, st_ref, *, n_lat, n_ctx):
    ck = HG_CHUNK
    lanes = o_ref.shape[-1]
    lane = lax.broadcasted_iota(jnp.int32, (1, lanes), 1)
    m0 = (lane < HG_KEY_DIM).astype(F32)
    m1 = 1.0 - m0
    bd = bd_ref[...]
    gm = gm_ref[...]

    def one(q, f, v, rev):
        d = 1 if rev else 0
        lg, k = _forget_terms(f, lb_ref[d, 0:1, :], lb_ref[d, 1:2, :], lb_ref[d, 2:3, :])
        o, st = _hg_chunk(q, k, v, lg, st_ref[d], rev, cm_ref[d], hm_ref[d], gm, sel_ref[d], bd, m0, m1)
        st_ref[d] = st
        return o

    def sweep(qr, ffr, fbr, ir, out_f, out_r, n):
        nc = n // ck

        def body(step, carry):
            idf = pl.ds(pl.multiple_of(step * ck, ck), ck)
            idr = pl.ds(pl.multiple_of((nc - 1 - step) * ck, ck), ck)
            out_f[0, idf, :] = one(qr[0, idf, :], ffr[0, idf, :], ir[0, idf, :], False)
            out_r[idr, :] = one(qr[0, idr, :], fbr[0, idr, :], ir[0, idr, :], True)
            return carry

        lax.fori_loop(0, nc, body, 0, unroll=2)

    st_ref[...] = jnp.zeros(st_ref.shape, F32)
    sweep(qc_ref, ffc_ref, fbc_ref, ic_ref, oc_ref, ocr_ref, n_ctx)
    sweep(q_ref, ff_ref, fb_ref, i_ref, o_ref, or_ref, n_lat)

    mean_mat = bd * (1.0 / HG_VAL_DIM)

    def readout(out, out_r, n):
        tile = min(n, 512)

        def body(step, carry):
            idx = pl.ds(pl.multiple_of(step * tile, tile), tile)
            x = out[0, idx, :] + out_r[idx, :]
            ms = jnp.dot(x * x, mean_mat, precision=HIGHEST, preferred_element_type=F32)
            out[0, idx, :] = x * lax.rsqrt(ms + NORM_EPS) * g_ref[...]
            return carry

        lax.fori_loop(0, n // tile, body, 0)

    readout(oc_ref, ocr_ref, n_ctx)
    readout(o_ref, or_ref, n_lat)


def _hgrn(q, ff, fb, iv, qc, ffc, fbc, ic, lb_terms, norm_g):
    b, n_lat, width = q.shape
    n_ctx = qc.shape[1]
    lanes = 2 * HG_KEY_DIM
    tables = _hg_tables()
    lat = pl.BlockSpec((1, n_lat, lanes), lambda i, j: (i, 0, j))
    ctx = pl.BlockSpec((1, n_ctx, lanes), lambda i, j: (i, 0, j))
    g2 = jnp.tile(norm_g, 2).reshape(1, lanes)
    return pl.pallas_call(
        functools.partial(_hgrn_kernel, n_lat=n_lat, n_ctx=n_ctx),
        grid=(b, width // lanes),
        in_specs=[lat] * 4 + [ctx] * 4 + [pl.BlockSpec((2, 3, lanes), lambda i, j: (0, 0, j)), _const_spec((1, lanes))]
                 + [_const_spec(t.shape) for t in tables],
        out_specs=[lat, ctx],
        out_shape=[jax.ShapeDtypeStruct(q.shape, F32), jax.ShapeDtypeStruct(qc.shape, F32)],
        scratch_shapes=[pltpu.VMEM((n_lat, lanes), F32), pltpu.VMEM((n_ctx, lanes), F32), pltpu.VMEM((2, lanes, lanes), F32)],
        compiler_params=_params('parallel', 'parallel'),
        name='hgrn2',
    )(q, ff, fb, iv, qc, ffc, fbc, ic, lb_terms, g2, *tables)


def _out_proj_kernel(hy_ref, da_ref, mla_ref, hg_ref, gate_ref, x_ref, g1_ref, w_ref, o_ref):
    c = hy_ref.shape[2]
    acc = jnp.dot(hy_ref[0].astype(BF16), w_ref[0:c, :], preferred_element_type=F32)
    for i, head_ref in ((1, da_ref), (2, mla_ref)):
        dv = head_ref.shape[3]
        for h in range(head_ref.shape[1]):
            lo = i * c + h * dv
            acc = acc + jnp.dot(head_ref[0, h].astype(BF16), w_ref[lo:lo + dv, :], preferred_element_type=F32)
    hg = hg_ref[0] * _silu(gate_ref[0])
    acc = acc + jnp.dot(hg.astype(BF16), w_ref[3 * c:4 * c, :], preferred_element_type=F32)
    o_ref[0] = x_ref[0] + g1_ref[0] * acc


def _out_proj(y_hy, y_da, y_mla, y_hg, gate, x, g1, w_out, tm=512):
    b, s, d = x.shape
    tm = min(tm, s)
    c = y_hy.shape[2]
    part = pl.BlockSpec((1, tm, c), lambda i, j: (i, j, 0))
    headed = lambda a: pl.BlockSpec((1, a.shape[1], tm, a.shape[3]), lambda i, j: (i, 0, j, 0))
    row = pl.BlockSpec((1, tm, d), lambda i, j: (i, j, 0))
    return pl.pallas_call(
        _out_proj_kernel,
        grid=(b, s // tm),
        in_specs=[part, headed(y_da), headed(y_mla), part, part, row, pl.BlockSpec((1, 1, d), lambda i, j: (i, 0, 0)),
                  _const_spec(w_out.shape)],
        out_specs=row,
        out_shape=jax.ShapeDtypeStruct(x.shape, F32),
        compiler_params=_params('parallel', 'parallel'),
        name='out_proj',
    )(y_hy, y_da, y_mla, y_hg, gate, x, g1, w_out)


def _router_kernel(x_ref, g_ref, sc_ref, sh_ref, wrt_ref, bias_ref, *rest, compact):
    h = _rms(x_ref[0]) * g_ref[...] * (1.0 + sc_ref[0]) + sh_ref[0]
    tm = h.shape[0]
    scores = jax.nn.sigmoid(_dot_nt(wrt_ref[...], h, precision=HIGHEST))
    choice = scores + bias_ref[...]
    per = N_EXPERTS // N_EXPERT_GROUPS
    neg = -jnp.inf
    iota_g = lax.broadcasted_iota(jnp.int32, (per, tm), 0)
    grp_rows = []
    for gi in range(N_EXPERT_GROUPS):
        blk = choice[gi * per:(gi + 1) * per]
        m1 = jnp.max(blk, axis=0, keepdims=True)
        first = jnp.min(jnp.where(blk == m1, iota_g, per), axis=0, keepdims=True)
        m2 = jnp.max(jnp.where(iota_g == first, neg, blk), axis=0, keepdims=True)
        grp_rows.append(m1 + m2)
    grp = jnp.concatenate(grp_rows, axis=0)
    iota_n = lax.broadcasted_iota(jnp.int32, (N_EXPERT_GROUPS, tm), 0)
    gsel = jnp.zeros((N_EXPERT_GROUPS, tm), F32)
    for _ in range(TOPK_GROUPS):
        m = jnp.max(grp, axis=0, keepdims=True)
        first = jnp.min(jnp.where(grp == m, iota_n, N_EXPERT_GROUPS), axis=0, keepdims=True)
        hit = iota_n == first
        gsel = jnp.where(hit, 1.0, gsel)
        grp = jnp.where(hit, neg, grp)
    emask = jnp.concatenate([jnp.broadcast_to(gsel[gi:gi + 1], (per, tm)) for gi in range(N_EXPERT_GROUPS)], axis=0)
    cand = jnp.where(emask > 0.0, choice, neg)
    iota_e = lax.broadcasted_iota(jnp.int32, (N_EXPERTS, tm), 0)
    sel = jnp.zeros((N_EXPERTS, tm), F32)
    chosen = []
    for _ in range(TOP_K):
        m = jnp.max(cand, axis=0, keepdims=True)
        first = jnp.min(jnp.where(cand == m, iota_e, N_EXPERTS), axis=0, keepdims=True)
        hit = iota_e == first
        sel = jnp.where(hit, 1.0, sel)
        cand = jnp.where(hit, neg, cand)
        chosen.append(first)
    w = scores * sel
    gate = w / jnp.sum(w, axis=0, keepdims=True) * ROUTED_SCALE
    if not compact:
        h_ref, gate_ref = rest
        h_ref[0] = h.astype(BF16)
        gate_ref[0] = gate
        return
    hp_ref, eid_ref, rank_ref, w_ref, cnt_out_ref, cnt_ref = rest
    hp_ref[0] = _pack_halves(h)

    @pl.when((pl.program_id(0) == 0) & (pl.program_id(1) == 0))
    def _():
        cnt_ref[...] = jnp.zeros(cnt_ref.shape, F32)

    src = lax.broadcasted_iota(jnp.int32, (tm, tm), 0)
    dst = lax.broadcasted_iota(jnp.int32, (tm, tm), 1)
    running = jnp.dot(sel.astype(BF16), (src <= dst).astype(BF16), preferred_element_type=F32)
    rank_dense = cnt_ref[:, 0:1] + running - 1.0
    e_rows, r_rows, w_rows = [], [], []
    for first in chosen:
        hit = iota_e == first
        e_rows.append(first)
        r_rows.append(jnp.sum(jnp.where(hit, rank_dense, 0.0), axis=0, keepdims=True))
        w_rows.append(jnp.sum(jnp.where(hit, gate, 0.0), axis=0, keepdims=True))
    eid_ref[...] = jnp.concatenate(e_rows, axis=0)
    rank_ref[...] = jnp.concatenate(r_rows, axis=0).astype(jnp.int32)
    w_ref[...] = jnp.concatenate(w_rows, axis=0)
    cnt_ref[...] = cnt_ref[...] + running[:, tm - 1:tm]
    cnt_out_ref[...] = cnt_ref[...]


def _router(x, g, scale, shift, w_router, e_bias, tm=512, compact=False):
    b, s, d = x.shape
    tm = min(tm, s)
    e = w_router.shape[1]
    row = pl.BlockSpec((1, tm, d), lambda i, j: (i, j, 0))
    mod = pl.BlockSpec((1, 1, d), lambda i, j: (i, 0, 0))
    if compact:
        nj = s // tm
        tok = pl.BlockSpec((TOP_K, tm), lambda i, j: (0, i * nj + j))
        out_specs = [pl.BlockSpec((1, tm, d // 2), lambda i, j: (i, j, 0)), tok, tok, tok, _const_spec((e, LANES))]
        out_shape = [jax.ShapeDtypeStruct((b, s, d // 2), jnp.int32), jax.ShapeDtypeStruct((TOP_K, b * s), jnp.int32),
                     jax.ShapeDtypeStruct((TOP_K, b * s), jnp.int32), jax.ShapeDtypeStruct((TOP_K, b * s), F32),
                     jax.ShapeDtypeStruct((e, LANES), F32)]
        scratch = [pltpu.VMEM((e, LANES), F32)]
        semantics = ('arbitrary', 'arbitrary')
    else:
        out_specs = [row, pl.BlockSpec((1, e, tm), lambda i, j: (i, 0, j))]
        out_shape = [jax.ShapeDtypeStruct((b, s, d), BF16), jax.ShapeDtypeStruct((b, e, s), F32)]
        scratch = []
        semantics = ('parallel', 'parallel')
    return pl.pallas_call(
        functools.partial(_router_kernel, compact=compact),
        grid=(b, s // tm),
        in_specs=[row, _const_spec((1, d)), mod, mod, _const_spec((e, d)), _const_spec((e, 1))],
        out_specs=out_specs,
        out_shape=out_shape,
        scratch_shapes=scratch,
        compiler_params=_params(*semantics),
        name='router',
    )(x, g.reshape(1, d), scale, shift, w_router.T, e_bias.reshape(e, 1))


def _moe_kernel(h_ref, x_ref, gate_ref, g2_ref, wg_ref, wu_ref, wd_ref, sg_ref, su_ref, sd_ref, *rest, final):
    if final:
        fg_ref, o_ref, acc_ref = rest
    else:
        o_ref, acc_ref = rest
    e = pl.program_id(2)
    h = h_ref[0]

    @pl.when(e == 0)
    def _():
        a = jnp.dot(h, sg_ref[...], preferred_element_type=F32)
        u = jnp.dot(h, su_ref[...], preferred_element_type=F32)
        acc_ref[...] = jnp.dot((_silu(a) * u).astype(BF16), sd_ref[...], preferred_element_type=F32)

    lane = lax.broadcasted_iota(jnp.int32, gate_ref.shape[1:], 1)
    gcol = jnp.sum(jnp.where(lane == e, gate_ref[0], 0.0), axis=-1, keepdims=True)
    a = jnp.dot(h, wg_ref[0].astype(BF16), preferred_element_type=F32)
    u = jnp.dot(h, wu_ref[0].astype(BF16), preferred_element_type=F32)
    acc_ref[...] += jnp.dot((_silu(a) * u * gcol).astype(BF16), wd_ref[0].astype(BF16), preferred_element_type=F32)

    @pl.when(e == pl.num_programs(2) - 1)
    def _():
        y = x_ref[0] + g2_ref[0] * acc_ref[...]
        if final:
            y = _rms(y) * fg_ref[...]
        o_ref[0] = y


def _moe(h2, x, gate, g2, layer, w_gate, w_up, w_down, s_gate, s_up, s_down, final_g=None, tm=1024):
    b, s, d = x.shape
    tm = min(tm, s)
    _, e, _, ff = w_gate.shape
    row = pl.BlockSpec((1, tm, d), lambda i, j, k: (i, j, 0))
    ins = [h2, x, gate, g2, w_gate, w_up, w_down, s_gate, s_up, s_down]
    in_specs = [row, row, pl.BlockSpec((1, tm, e), lambda i, j, k: (i, j, 0)),
                pl.BlockSpec((1, 1, d), lambda i, j, k: (i, 0, 0)),
                pl.BlockSpec((None, 1, d, ff), lambda i, j, k: (layer, k, 0, 0)),
                pl.BlockSpec((None, 1, d, ff), lambda i, j, k: (layer, k, 0, 0)),
                pl.BlockSpec((None, 1, ff, d), lambda i, j, k: (layer, k, 0, 0)),
                _const_spec(s_gate.shape), _const_spec(s_up.shape), _const_spec(s_down.shape)]
    if final_g is not None:
        ins.append(final_g.reshape(1, d))
        in_specs.append(_const_spec((1, d)))
    return pl.pallas_call(
        functools.partial(_moe_kernel, final=final_g is not None),
        grid=(b, s // tm, e),
        in_specs=in_specs,
        out_specs=row,
        out_shape=jax.ShapeDtypeStruct(x.shape, F32),
        scratch_shapes=[pltpu.VMEM((tm, d), F32)],
        compiler_params=_params('parallel', 'parallel', 'arbitrary'),
        name='moe',
    )(*ins)


MOE_ROW_TILE = 512
SC_ROWS = 128
V7X_SC_CORES = 2
V7X_SC_SUBCORES = 16


def _pack_halves(x):
    n = x.shape[1] // 2
    lo = pltpu.bitcast(x[:, :n].astype(BF16).astype(F32), jnp.int32)
    hi = pltpu.bitcast(x[:, n:].astype(BF16).astype(F32), jnp.int32)
    return jnp.bitwise_or(jnp.bitwise_and(hi, -65536), lax.shift_right_logical(lo, 16))


def _unpack_halves(p):
    lo = pltpu.bitcast(lax.shift_left(p, 16), F32).astype(BF16)
    hi = pltpu.bitcast(jnp.bitwise_and(p, -65536), F32).astype(BF16)
    return lo, hi


def _route_pos_kernel(off_ref, eid_ref, rank_ref, pos_ref):
    eid = eid_ref[...]
    base = jnp.zeros(eid.shape, jnp.int32)
    for e in range(N_EXPERTS):
        base = jnp.where(eid == e, off_ref[e], base)
    pos_ref[...] = base + rank_ref[...]


def _route_pos(offsets, eid, rank):
    return pl.pallas_call(
        _route_pos_kernel,
        grid=(1,),
        in_specs=[pl.BlockSpec(memory_space=pltpu.SMEM), _const_spec(eid.shape), _const_spec(rank.shape)],
        out_specs=_const_spec(eid.shape),
        out_shape=jax.ShapeDtypeStruct(eid.shape, jnp.int32),
        compiler_params=_params('arbitrary'),
        name='route_pos',
    )(offsets, eid, rank)


def _sc_mesh():
    return plsc.VectorSubcoreMesh(core_axis_name='c', subcore_axis_name='s', num_cores=V7X_SC_CORES,
                                  num_subcores=V7X_SC_SUBCORES)


def _sc_dispatch(hp, pos, n_rows):
    t, w = hp.shape
    k = pos.shape[0]
    workers = V7X_SC_CORES * V7X_SC_SUBCORES
    per_worker = t // workers
    pos_flat = pos.reshape(k * t)

    @functools.partial(pl.kernel, mesh=_sc_mesh(), out_type=jax.ShapeDtypeStruct((n_rows, w), jnp.int32),
                       scratch_types=[pltpu.VMEM((SC_ROWS,), jnp.int32), pltpu.VMEM((SC_ROWS, w), jnp.int32),
                                      pltpu.SemaphoreType.DMA])
    def scatter(hp_hbm, pos_hbm, out_hbm, idx_v, rows_v, sem):
        wid = lax.axis_index('s') * V7X_SC_CORES + lax.axis_index('c')

        @pl.loop(0, per_worker // SC_ROWS)
        def _(i):
            t0 = pl.multiple_of(wid * per_worker + i * SC_ROWS, SC_ROWS)
            pltpu.sync_copy(hp_hbm.at[pl.ds(t0, SC_ROWS)], rows_v)
            for j in range(k):
                pltpu.sync_copy(pos_hbm.at[pl.ds(pl.multiple_of(j * t + t0, SC_ROWS), SC_ROWS)], idx_v)
                pltpu.async_copy(rows_v, out_hbm.at[idx_v], sem).wait()

    return scatter(hp, pos_flat)


def _sc_collect(yp, pos):
    _, w = yp.shape
    k, t = pos.shape
    workers = V7X_SC_CORES * V7X_SC_SUBCORES
    per_worker = k * t // workers
    pos_flat = pos.reshape(k * t)

    @functools.partial(pl.kernel, mesh=_sc_mesh(), out_type=jax.ShapeDtypeStruct((k * t, w), jnp.int32),
                       scratch_types=[pltpu.VMEM((SC_ROWS,), jnp.int32), pltpu.VMEM((SC_ROWS, w), jnp.int32),
                                      pltpu.SemaphoreType.DMA])
    def gather(yp_hbm, pos_hbm, out_hbm, idx_v, rows_v, sem):
        wid = lax.axis_index('s') * V7X_SC_CORES + lax.axis_index('c')

        @pl.loop(0, per_worker // SC_ROWS)
        def _(i):
            r0 = pl.multiple_of(wid * per_worker + i * SC_ROWS, SC_ROWS)
            pltpu.sync_copy(pos_hbm.at[pl.ds(r0, SC_ROWS)], idx_v)
            pltpu.async_copy(yp_hbm.at[idx_v], rows_v, sem).wait()
            pltpu.sync_copy(rows_v, out_hbm.at[pl.ds(r0, SC_ROWS)])

    return gather(yp, pos_flat)


def _expert_kernel(te_ref, nu_ref, x_ref, wg_ref, wu_ref, wd_ref, o_ref, wg_s, wu_s, wd_s):
    i = pl.program_id(0)

    @pl.when(i < nu_ref[0])
    def _():
        @pl.when((i == 0) | (te_ref[i] != te_ref[jnp.maximum(i - 1, 0)]))
        def _():
            wg_s[...] = wg_ref[0].astype(BF16)
            wu_s[...] = wu_ref[0].astype(BF16)
            wd_s[...] = wd_ref[0].astype(BF16)

        lo, hi = _unpack_halves(x_ref[...])
        half = lo.shape[1]
        a = (jnp.dot(lo, wg_s[:half, :], preferred_element_type=F32)
             + jnp.dot(hi, wg_s[half:, :], preferred_element_type=F32))
        u = (jnp.dot(lo, wu_s[:half, :], preferred_element_type=F32)
             + jnp.dot(hi, wu_s[half:, :], preferred_element_type=F32))
        y = jnp.dot((_silu(a) * u).astype(BF16), wd_s[...], preferred_element_type=F32)
        o_ref[...] = _pack_halves(y)


def _experts(xp, tile_expert, n_used, layer, w_gate, w_up, w_down):
    n_rows, half = xp.shape
    _, _, d, ff = w_gate.shape
    r = MOE_ROW_TILE
    row = pl.BlockSpec((r, half), lambda i, te, nu: (i, 0))
    grid_spec = pltpu.PrefetchScalarGridSpec(
        num_scalar_prefetch=2,
        grid=(n_rows // r,),
        in_specs=[row,
                  pl.BlockSpec((None, 1, d, ff), lambda i, te, nu: (layer, te[i], 0, 0)),
                  pl.BlockSpec((None, 1, d, ff), lambda i, te, nu: (layer, te[i], 0, 0)),
                  pl.BlockSpec((None, 1, ff, d), lambda i, te, nu: (layer, te[i], 0, 0))],
        out_specs=row,
        scratch_shapes=[pltpu.VMEM((d, ff), BF16), pltpu.VMEM((d, ff), BF16), pltpu.VMEM((ff, d), BF16)],
    )
    return pl.pallas_call(
        _expert_kernel,
        grid_spec=grid_spec,
        out_shape=jax.ShapeDtypeStruct((n_rows, half), jnp.int32),
        compiler_params=_params('arbitrary'),
        name='experts',
    )(tile_expert, n_used, xp, w_gate, w_up, w_down)


def _combine_kernel(yg_ref, w_ref, hp_ref, x_ref, g2_ref, sg_ref, su_ref, sd_ref, *rest, final):
    if final:
        fg_ref, o_ref = rest
    else:
        (o_ref,) = rest
    half = hp_ref.shape[2]
    lo, hi = _unpack_halves(hp_ref[0])
    sg, su = sg_ref[...], su_ref[...]
    a = jnp.dot(lo, sg[:half], preferred_element_type=F32) + jnp.dot(hi, sg[half:], preferred_element_type=F32)
    u = jnp.dot(lo, su[:half], preferred_element_type=F32) + jnp.dot(hi, su[half:], preferred_element_type=F32)
    acc = jnp.dot((_silu(a) * u).astype(BF16), sd_ref[...], preferred_element_type=F32)
    acc_lo, acc_hi = acc[:, :half], acc[:, half:]
    wts = w_ref[0]
    for k in range(yg_ref.shape[0]):
        ylo, yhi = _unpack_halves(yg_ref[k, 0])
        wk = wts[:, k:k + 1]
        acc_lo = acc_lo + wk * ylo.astype(F32)
        acc_hi = acc_hi + wk * yhi.astype(F32)
    y = x_ref[0] + g2_ref[0] * jnp.concatenate([acc_lo, acc_hi], axis=1)
    if final:
        y = _rms(y) * fg_ref[...]
    o_ref[0] = y


def _combine(yg, wts, hp, x, g2, s_gate, s_up, s_down, final_g=None, tm=256):
    b, s, d = x.shape
    k = yg.shape[0]
    half = d // 2
    row = pl.BlockSpec((1, tm, d), lambda i, j: (i, j, 0))
    prow = pl.BlockSpec((1, tm, half), lambda i, j: (i, j, 0))
    ins = [yg, wts, hp, x, g2, s_gate, s_up, s_down]
    in_specs = [pl.BlockSpec((k, 1, tm, half), lambda i, j: (0, i, j, 0)), pl.BlockSpec((1, tm, k), lambda i, j: (i, j, 0)),
                prow, row, pl.BlockSpec((1, 1, d), lambda i, j: (i, 0, 0)),
                _const_spec(s_gate.shape), _const_spec(s_up.shape), _const_spec(s_down.shape)]
    if final_g is not None:
        ins.append(final_g.reshape(1, d))
        in_specs.append(_const_spec((1, d)))
    return pl.pallas_call(
        functools.partial(_combine_kernel, final=final_g is not None),
        grid=(b, s // tm),
        in_specs=in_specs,
        out_specs=row,
        out_shape=jax.ShapeDtypeStruct(x.shape, F32),
        compiler_params=_params('parallel', 'parallel'),
        name='moe_combine',
    )(*ins)


def _routed_moe(x, g, scale, shift, g2, w_router, e_bias, layer, w_gate, w_up, w_down, s_gate, s_up, s_down, final_g=None):
    b, s, d = x.shape
    t = b * s
    hp, eid, rank, wts, counts = _router(x, g, scale, shift, w_router, e_bias, compact=True)
    counts = counts[:, 0].astype(jnp.int32)
    r = MOE_ROW_TILE
    padded = (counts + (r - 1)) // r * r
    ends = jnp.cumsum(padded)
    offsets = ends - padded
    n_rows = t * TOP_K + N_EXPERTS * r
    tile_start = jnp.arange(n_rows // r, dtype=jnp.int32) * r
    tile_expert = jnp.minimum(jnp.sum((tile_start[:, None] >= ends[None, :]).astype(jnp.int32), axis=1), N_EXPERTS - 1)
    n_used = (ends[-1] // r).reshape(1).astype(jnp.int32)
    pos = _route_pos(offsets.astype(jnp.int32), eid, rank)
    xp = _sc_dispatch(hp.reshape(t, d // 2), pos, n_rows)
    yp = _experts(xp, tile_expert.astype(jnp.int32), n_used, layer, w_gate, w_up, w_down)
    yg = _sc_collect(yp, pos).reshape(TOP_K, b, s, d // 2)
    return _combine(yg, wts.T.reshape(b, s, TOP_K), hp, x, g2, s_gate, s_up, s_down, final_g)


def _mixers(p, pc, ctx_out, prm, l, lam_init, rope_tabs, lb_terms):
    s = p['hy_v'].shape[1]
    sc = pc['hy_v'].shape[1]

    hy_args = (prm['hy_w1'][l], prm['hy_b1'][l], prm['hy_w2'][l], prm['hy_b2'][l], prm['hy_w3'][l], prm['hy_b3'][l],
               prm['hy_sin_freq'][l], prm['hy_decay'][l])
    y_hy = _hyena([p['hy_v'], p['hy_x1'], p['hy_x2']], prm['hy_conv_w'][l], prm['hy_conv_b'][l],
                  _hy_filters(s, *hy_args), prm['hy_bias'][l], inner=128)
    yc_hy = None
    if ctx_out:
        yc_hy = _hyena([pc['hy_v'], pc['hy_x1'], pc['hy_x2']], prm['hy_conv_w'][l], prm['hy_conv_b'][l],
                       _hy_filters(sc, *hy_args), prm['hy_bias'][l], inner=32)

    lp = prm['da_lambda'][l].astype(F32)
    lam = jnp.exp(jnp.sum(lp[0] * lp[1])) - jnp.exp(jnp.sum(lp[2] * lp[3])) + lam_init
    da_kw = dict(heads=DA_HEADS, ncomp=2, scale=DA_HEAD_DIM ** -0.5, lam=lam, subln_g=prm['da_subln_g'][l],
                 post_scale=1.0 - lam_init)
    da_ctx = ([pc['da_k']], pc['da_v'])
    y_da = _attention([p['da_q']], [da_ctx, ([p['da_k']], p['da_v'])], **da_kw)
    yc_da = _attention([pc['da_q']], [da_ctx], **da_kw) if ctx_out else None

    wq = prm['mla_w_q_up'][l].reshape(MLA_Q_RANK, MLA_HEADS, MLA_NOPE_DIM + MLA_ROPE_DIM)
    wq_n = wq[:, :, :MLA_NOPE_DIM].reshape(MLA_Q_RANK, -1).astype(BF16)
    wq_r = wq[:, :, MLA_NOPE_DIM:].reshape(MLA_Q_RANK, -1).astype(BF16)
    wkv = prm['mla_w_kv_up'][l].reshape(MLA_KV_RANK, MLA_HEADS, MLA_NOPE_DIM + MLA_V_DIM)
    wkv_n = wkv[:, :, :MLA_NOPE_DIM].reshape(MLA_KV_RANK, -1).astype(BF16)
    wkv_v = wkv[:, :, MLA_NOPE_DIM:].reshape(MLA_KV_RANK, -1).astype(BF16)

    def queries(qd, tabs):
        return _norm_proj(qd, prm['mla_q_norm_g'][l], [(wq_n, F32, False, MLA_HEADS), (wq_r, F32, True, MLA_HEADS)],
                          rope_tabs=tabs)

    def keys_values(kvd):
        return _norm_proj(kvd, prm['mla_kv_norm_g'][l], [(wkv_n, BF16, False, MLA_HEADS), (wkv_v, BF16, False, MLA_HEADS)])

    kn_l, v_l = keys_values(p['mla_kv'])
    kn_c, v_c = keys_values(pc['mla_kv'])
    mla_kw = dict(heads=MLA_HEADS, ncomp=1, scale=(MLA_NOPE_DIM + MLA_ROPE_DIM) ** -0.5)
    mla_ctx = ([kn_c, pc['mla_kr']], v_c)
    y_mla = _attention(queries(p['mla_q'], rope_tabs), [mla_ctx, ([kn_l, p['mla_kr']], v_l)], **mla_kw)
    yc_mla = _attention(queries(pc['mla_q'], None), [mla_ctx], **mla_kw) if ctx_out else None

    o, oc = _hgrn(p['hg_q'], p['hg_ff'], p['hg_fb'], p['hg_i'], pc['hg_q'], pc['hg_ff'], pc['hg_fb'], pc['hg_i'],
                  lb_terms, prm['hg_norm_g'][l])
    return (y_hy, y_da, y_mla, o), (yc_hy, yc_da, yc_mla, oc)


def kernel(x, c, ctx, c_ctx, w_ada, b_ada, norm1_g, norm2_g, w_in, w_out, hy_conv_w, hy_conv_b, hy_w1, hy_b1, hy_w2, hy_b2, hy_w3, hy_b3, hy_sin_freq, hy_decay, hy_bias, da_lambda, da_subln_g, mla_q_norm_g, mla_w_q_up, mla_kv_norm_g, mla_w_kv_up, hg_lower_bounds, hg_norm_g, moe_w_router, moe_bias, moe_w_gate, moe_w_up, moe_w_down, moe_sh_gate, moe_sh_up, moe_sh_down, final_norm_g):
    prm = dict(hy_conv_w=hy_conv_w, hy_conv_b=hy_conv_b, hy_w1=hy_w1, hy_b1=hy_b1, hy_w2=hy_w2, hy_b2=hy_b2,
               hy_w3=hy_w3, hy_b3=hy_b3, hy_sin_freq=hy_sin_freq, hy_decay=hy_decay, hy_bias=hy_bias,
               da_lambda=da_lambda, da_subln_g=da_subln_g, mla_q_norm_g=mla_q_norm_g, mla_w_q_up=mla_w_q_up,
               mla_kv_norm_g=mla_kv_norm_g, mla_w_kv_up=mla_w_kv_up, hg_norm_g=hg_norm_g)
    b, n_lat, d = x.shape
    depth = w_in.shape[0]
    rows = n_lat // GRID_W
    row_pos = jnp.repeat(jnp.arange(rows, dtype=jnp.int32), GRID_W)
    col_pos = jnp.tile(jnp.arange(GRID_W, dtype=jnp.int32), rows)
    rope_tabs = _rope_tables(row_pos, col_pos, 2 * DA_HEADS * DA_HEAD_DIM)
    lbs = jnp.cumsum(jax.nn.softmax(hg_lower_bounds.astype(F32), axis=1), axis=1)
    lbs = lbs - lbs[:, :1]
    cond = jnp.concatenate([c, c_ctx[None], jnp.zeros((8 - b - 1, d), F32)], axis=0)

    for l in range(depth):
        ctx_out = l < depth - 1
        mods = _ada(cond, w_ada[l], b_ada[l])
        sh1, sc1, g1, sh2, sc2, g2 = [m[:, None, :] for m in jnp.split(mods[:b], 6, axis=-1)]
        mc = [jnp.broadcast_to(m[:, None, :], (b, 1, d)) for m in jnp.split(mods[b:b + 1], 6, axis=-1)]

        off = 0
        outs = []
        for _, wdt, dt, rope, split, rep in _SEGMENTS:
            w = w_in[l][:, off:off + wdt].astype(BF16)
            outs.append((jnp.tile(w, (1, rep)) if rep > 1 else w, dt, rope, split))
            off += wdt
        names = [seg[0] for seg in _SEGMENTS]
        p = dict(zip(names, _norm_proj(x, norm1_g[l], outs, sc1, sh1, rope_tabs=rope_tabs)))
        pc = dict(zip(names, _norm_proj(ctx, norm1_g[l], outs, mc[1], mc[0])))

        lb = lbs[:, l]
        lb_terms = jnp.stack([jnp.log(lb), jnp.log1p(-lb), 1.0 - lb], axis=1)
        lam_init = 0.8 - 0.6 * math.exp(-0.3 * l)
        lat_parts, ctx_parts = _mixers(p, pc, ctx_out, prm, l, lam_init, rope_tabs, lb_terms)

        w_out_b = w_out[l].astype(BF16)
        moe_w = (l, moe_w_gate, moe_w_up, moe_w_down,
                 moe_sh_gate[l].astype(BF16), moe_sh_up[l].astype(BF16), moe_sh_down[l].astype(BF16))

        if ctx_out:
            ctx = _out_proj(*ctx_parts, pc['hg_g'], ctx, mc[2], w_out_b)
            flat = ctx.reshape(1, -1, d)
            h2c, gate_c = _router(flat, norm2_g[l], mc[4][:1], mc[3][:1], moe_w_router[l], moe_bias[l])
            ctx = _moe(h2c, flat, gate_c.transpose(0, 2, 1), mc[5][:1], *moe_w).reshape(ctx.shape)

        x = _out_proj(*lat_parts, p['hg_g'], x, g1, w_out_b)
        x = _routed_moe(x, norm2_g[l], sc2, sh2, g2, moe_w_router[l], moe_bias[l], *moe_w,
                        final_g=None if ctx_out else final_norm_g)

    return x
```

```python
import functools
import math

import numpy as np
import jax
import jax.numpy as jnp
from jax import lax
from jax.experimental import pallas as pl
from jax.experimental.pallas import tpu as pltpu
from jax.experimental.pallas import tpu_sc as plsc

F32 = jnp.float32
BF16 = jnp.bfloat16
HIGHEST = lax.Precision.HIGHEST

D_MODEL = 1024
GRID_W = 64
HY_WIDTH = 256
HY_ORDER = 2
HY_BANDS = 16
DA_HEADS = 4
DA_HEAD_DIM = 32
MLA_HEADS = 4
MLA_Q_RANK = 192
MLA_KV_RANK = 128
MLA_NOPE_DIM = 64
MLA_ROPE_DIM = 32
MLA_V_DIM = 64
HG_HEADS = 4
HG_KEY_DIM = 64
HG_VAL_DIM = 64
HG_CHUNK = 64
HG_SUB = 8
N_EXPERTS = 64
N_EXPERT_GROUPS = 8
TOPK_GROUPS = 4
TOP_K = 8
EXPERT_FF = 256
ROUTED_SCALE = 2.5
ROPE_BASE = 10000.0
NORM_EPS = 1e-6

V7X_VMEM_LIMIT_BYTES = 56 * 1024 * 1024
LANES = 128

_SEGMENTS = (
    ('hy_v', HY_WIDTH, F32, False, 0, 1), ('hy_x1', HY_WIDTH, F32, False, 0, 1), ('hy_x2', HY_WIDTH, F32, False, 0, 1),
    ('da_q', 2 * DA_HEADS * DA_HEAD_DIM, F32, True, 2 * DA_HEADS, 1),
    ('da_k', 2 * DA_HEADS * DA_HEAD_DIM, BF16, True, 2 * DA_HEADS, 1),
    ('da_v', 2 * DA_HEADS * DA_HEAD_DIM, BF16, False, DA_HEADS, 1),
    ('mla_q', MLA_Q_RANK, F32, False, 0, 1), ('mla_kv', MLA_KV_RANK, F32, False, 0, 1),
    ('mla_kr', MLA_ROPE_DIM, BF16, True, MLA_HEADS, MLA_HEADS),
    ('hg_q', HG_HEADS * HG_KEY_DIM, F32, False, 0, 1), ('hg_ff', HG_HEADS * HG_KEY_DIM, F32, False, 0, 1),
    ('hg_fb', HG_HEADS * HG_KEY_DIM, F32, False, 0, 1), ('hg_i', HG_HEADS * HG_VAL_DIM, F32, False, 0, 1),
    ('hg_g', HG_HEADS * HG_VAL_DIM, F32, False, 0, 1),
)


def _params(*semantics):
    return pltpu.CompilerParams(dimension_semantics=semantics, vmem_limit_bytes=V7X_VMEM_LIMIT_BYTES)


def _const_spec(shape):
    nd = len(shape)
    return pl.BlockSpec(shape, lambda *_: (0,) * nd)


def _rms(x, eps=NORM_EPS):
    return x * lax.rsqrt(jnp.mean(x * x, axis=-1, keepdims=True) + eps)


def _silu(x):
    return x * jax.nn.sigmoid(x)


def _dot_nt(a, b, **kw):
    return lax.dot_general(a, b, (((1,), (1,)), ((), ())), preferred_element_type=F32, **kw)


def _ada_kernel(c_ref, w_ref, b_ref, o_ref):
    s = _silu(c_ref[...])
    o_ref[...] = jnp.dot(s, w_ref[...], precision=HIGHEST, preferred_element_type=F32) + b_ref[...]


def _ada(cond, w, b):
    r, d = cond.shape
    n = w.shape[1]
    tn = 1536
    return pl.pallas_call(
        _ada_kernel,
        grid=(n // tn,),
        in_specs=[_const_spec((r, d)), pl.BlockSpec((d, tn), lambda j: (0, j)), pl.BlockSpec((1, tn), lambda j: (0, j))],
        out_specs=pl.BlockSpec((r, tn), lambda j: (0, j)),
        out_shape=jax.ShapeDtypeStruct((r, n), F32),
        compiler_params=_params('arbitrary'),
        name='ada',
    )(cond, w, b.reshape(1, n))


ROPE_UNIT = 32


def _rope_tables(row, col, width):
    n = ROPE_UNIT // 4
    inv = ROPE_BASE ** (-jnp.arange(n, dtype=F32) / n)
    units = width // ROPE_UNIT
    parts_c, parts_a, parts_b = [], [], []
    zero = jnp.zeros((row.shape[0], n), F32)
    for pos in (row, col):
        ang = pos.astype(F32)[:, None] * inv
        cos, sin = jnp.cos(ang), jnp.sin(ang)
        parts_c += [cos, cos]
        parts_a += [zero, sin]
        parts_b += [-sin, zero]
    tile = lambda ps: jnp.tile(jnp.concatenate(ps, axis=1), (1, units))
    return tile(parts_c), tile(parts_a), tile(parts_b)


def _norm_proj_kernel(*refs, n_w, modulate, ropes, splits):
    x_ref, g_ref = refs[0], refs[1]
    pos = 2
    if modulate:
        sc_ref, sh_ref = refs[2], refs[3]
        pos = 4
    if any(ropes):
        rc_ref, ra_ref, rb_ref = refs[pos:pos + 3]
        pos += 3
    w_refs = refs[pos:pos + n_w]
    o_refs = refs[pos + n_w:]
    y = _rms(x_ref[0]) * g_ref[...]
    if modulate:
        y = y * (1.0 + sc_ref[0]) + sh_ref[0]
    yb = y.astype(BF16)
    for w_ref, o_ref, rope, split in zip(w_refs, o_refs, ropes, splits):
        o = jnp.dot(yb, w_ref[...], preferred_element_type=F32)
        if rope:
            wd = o.shape[1]
            shift = ROPE_UNIT // 4
            o = (o * rc_ref[:, :wd] + pltpu.roll(o, shift, axis=1) * ra_ref[:, :wd]
                 + pltpu.roll(o, wd - shift, axis=1) * rb_ref[:, :wd])
        if split:
            unit = o.shape[1] // split
            for u in range(split):
                o_ref[0, u] = o[:, u * unit:(u + 1) * unit].astype(o_ref.dtype)
        else:
            o_ref[0] = o.astype(o_ref.dtype)


def _norm_proj(x, g, outs, scale=None, shift=None, rope_tabs=None, tm=512):
    b, s, k = x.shape
    tm = min(tm, s)
    modulate = scale is not None
    ropes = tuple(bool(o[2]) and rope_tabs is not None for o in outs)
    splits = tuple(o[3] for o in outs)
    ins = [x, g.reshape(1, k)]
    in_specs = [pl.BlockSpec((1, tm, k), lambda i, j: (i, j, 0)), _const_spec((1, k))]
    if modulate:
        ins += [scale, shift]
        in_specs += [pl.BlockSpec((1, 1, k), lambda i, j: (i, 0, 0))] * 2
    if any(ropes):
        ins += list(rope_tabs)
        in_specs += [pl.BlockSpec((tm, rope_tabs[0].shape[1]), lambda i, j: (j, 0))] * 3
    out_specs, out_shape = [], []
    for w, dt, _, split in outs:
        ins.append(w)
        in_specs.append(_const_spec(w.shape))
        n = w.shape[1]
        if split:
            out_specs.append(pl.BlockSpec((1, split, tm, n // split), lambda i, j: (i, 0, j, 0)))
            out_shape.append(jax.ShapeDtypeStruct((b, split, s, n // split), dt))
        else:
            out_specs.append(pl.BlockSpec((1, tm, n), lambda i, j: (i, j, 0)))
            out_shape.append(jax.ShapeDtypeStruct((b, s, n), dt))
    return pl.pallas_call(
        functools.partial(_norm_proj_kernel, n_w=len(outs), modulate=modulate, ropes=ropes, splits=splits),
        grid=(b, s // tm),
        in_specs=in_specs,
        out_specs=out_specs,
        out_shape=out_shape,
        compiler_params=_params('parallel', 'parallel'),
        name='norm_proj',
    )(*ins)


def _hy_filter_kernel(w1t_ref, w1s_ref, w1c_ref, b1_ref, w2_ref, b2_ref, w3_ref, b3_ref, fr_ref, dec_ref, o_ref, *, n):
    t = lax.broadcasted_iota(jnp.int32, (n, 1), 0).astype(F32) / n
    bands = lax.broadcasted_iota(jnp.int32, (1, HY_BANDS), 1).astype(F32) + 1.0
    ang = (2.0 * jnp.pi) * t * bands
    pre = (t * w1t_ref[...]
           + jnp.dot(jnp.sin(ang), w1s_ref[...], precision=HIGHEST, preferred_element_type=F32)
           + jnp.dot(jnp.cos(ang), w1c_ref[...], precision=HIGHEST, preferred_element_type=F32)
           + b1_ref[...])
    hid = jnp.sin(fr_ref[0:1, :] * pre)
    hid = jnp.sin(fr_ref[1:2, :] * (jnp.dot(hid, w2_ref[...], precision=HIGHEST, preferred_element_type=F32) + b2_ref[...]))
    filt = jnp.dot(hid, w3_ref[...], precision=HIGHEST, preferred_element_type=F32) + b3_ref[...]
    filt = filt * jnp.exp(-t * jnp.abs(dec_ref[...]))
    col = jnp.sum(jnp.abs(filt), axis=0, keepdims=True) - jnp.abs(filt[0:1, :])
    w = HY_WIDTH
    for o in range(HY_ORDER):
        lo = o * 2 * w
        f0 = filt[0:1, lo:lo + w] + filt[0:1, lo + w:lo + 2 * w]
        inv = 1.0 / (col[:, lo:lo + w] + col[:, lo + w:lo + 2 * w] + jnp.abs(f0))
        o_ref[:, lo:lo + w] = filt[:, lo:lo + w] * inv
        o_ref[:, lo + w:lo + 2 * w] = filt[:, lo + w:lo + 2 * w] * inv


def _hy_filters(n, w1, b1, w2, b2, w3, b3, freq, decay):
    cols = w3.shape[1]
    ins = [w1[0:1], w1[1:1 + HY_BANDS], w1[1 + HY_BANDS:], b1.reshape(1, -1), w2, b2.reshape(1, -1), w3,
           b3.reshape(1, -1), freq, decay.reshape(1, -1)]
    out = pl.pallas_call(
        functools.partial(_hy_filter_kernel, n=n),
        grid=(1,),
        in_specs=[_const_spec(a.shape) for a in ins],
        out_specs=_const_spec((n, cols)),
        out_shape=jax.ShapeDtypeStruct((n, cols), F32),
        compiler_params=_params('arbitrary'),
        name='hy_filter',
    )(*ins)
    return out.reshape(n, HY_ORDER, 2, HY_WIDTH)


def _two_sided(filt_n):
    n = filt_n.shape[0]
    hf, hb = filt_n[:, :, 0], filt_n[:, :, 1]
    h = jnp.concatenate([hf[:1] + hb[:1], hf[1:], jnp.zeros((1,) + hf.shape[1:], F32), hb[:0:-1]], axis=0)
    return h.reshape(2 * n, HY_ORDER * HY_WIDTH)


def _short_conv_kernel(*refs, s):
    x_refs, w_refs, b_refs, o_refs = refs[0:3], refs[3:6], refs[6:9], refs[9:12]
    row = lax.broadcasted_iota(jnp.int32, (s, 1), 0)
    for x_ref, w_ref, b_ref, o_ref in zip(x_refs, w_refs, b_refs, o_refs):
        x = x_ref[0]
        prev = jnp.where(row == 0, 0.0, pltpu.roll(x, 1, axis=0))
        nxt = jnp.where(row == s - 1, 0.0, pltpu.roll(x, s - 1, axis=0))
        o_ref[0] = prev * w_ref[0:1, :] + x * w_ref[1:2, :] + nxt * w_ref[2:3, :] + b_ref[...]


def _short_conv(parts, conv_w, conv_b):
    b, s, c = parts[0].shape
    tc = LANES
    ws = [conv_w[:, i * c:(i + 1) * c] for i in range(3)]
    bs = [conv_b[i * c:(i + 1) * c].reshape(1, c) for i in range(3)]
    xspec = pl.BlockSpec((1, s, tc), lambda i, j: (i, 0, j))
    return pl.pallas_call(
        functools.partial(_short_conv_kernel, s=s),
        grid=(b, c // tc),
        in_specs=[xspec] * 3 + [pl.BlockSpec((3, tc), lambda i, j: (0, j))] * 3 + [pl.BlockSpec((1, tc), lambda i, j: (0, j))] * 3,
        out_specs=[xspec] * 3,
        out_shape=[jax.ShapeDtypeStruct((b, s, c), F32)] * 3,
        compiler_params=_params('parallel', 'parallel'),
        name='short_conv',
    )(*parts, *ws, *bs)


def _dft_cos_sin(rows, cols, period):
    ang = 2.0 * np.pi * ((np.arange(rows)[:, None] * np.arange(cols)[None, :]) % period) / period
    return np.cos(ang), np.sin(ang)


def _fft_tables(n, inner):
    big = 2 * n
    n1 = big // inner
    c1, s1 = _dft_cos_sin(n1, n1, n1)
    h = n1 // 2
    outer_data = np.block([[c1[:, :h], s1[:, :h]], [-s1[:, :h], c1[:, :h]]])
    outer_real = np.concatenate([c1, -s1], axis=0)
    outer_inv = np.block([[c1[:h, :], -s1[:h, :]], [s1[:h, :], c1[:h, :]]]) / big
    c2, s2 = _dft_cos_sin(inner, inner, inner)
    inner_fwd = np.block([[c2, s2], [-s2, c2]])
    inner_inv = np.block([[c2, -s2], [s2, c2]])
    ct, st = _dft_cos_sin(n1, inner, big)
    f = lambda a: jnp.asarray(a, F32)
    return dict(n1=n1, inner=inner, outer_data=f(outer_data), outer_real=f(outer_real), outer_inv=f(outer_inv),
                inner_fwd=_hi_lo_cols(inner_fwd), inner_inv=_hi_lo_cols(inner_inv),
                tw_cos=f(ct).reshape(n1, inner, 1), tw_sin=f(st).reshape(n1, inner, 1))


def _left_mm_kernel(m_ref, x_ref, o_ref):
    o_ref[0] = jnp.dot(m_ref[...], x_ref[0], precision=HIGHEST, preferred_element_type=F32)


def _left_mm(m, x, tl=4096):
    p, k, l = x.shape
    mm = m.shape[0]
    tl = min(tl, l)
    return pl.pallas_call(
        _left_mm_kernel,
        grid=(p, l // tl),
        in_specs=[_const_spec(m.shape), pl.BlockSpec((1, k, tl), lambda i, j: (i, 0, j))],
        out_specs=pl.BlockSpec((1, mm, tl), lambda i, j: (i, 0, j)),
        out_shape=jax.ShapeDtypeStruct((p, mm, l), F32),
        compiler_params=_params('parallel', 'parallel'),
        name='fft_outer',
    )(m, x)


def _hi_lo_cols(m):
    m = np.asarray(m, np.float32)
    hi = m.astype(BF16)
    lo = (m - hi.astype(np.float32)).astype(BF16)
    return jnp.asarray(np.concatenate([hi, hi, lo], axis=1))


def _hi_lo_rows(x):
    hi = x.astype(BF16)
    lo = (x - hi.astype(F32)).astype(BF16)
    return jnp.concatenate([hi, lo, hi], axis=0)


def _inner_kernel(a_ref, twc_ref, tws_ref, gf_ref, *rest, convolve, inner, kb):
    for s in range(kb):
        ar, ai = a_ref[0, 0, s], a_ref[0, 1, s]
        tc, ts = twc_ref[s], tws_ref[s]
        br = ar * tc + ai * ts
        bi = ai * tc - ar * ts
        x = jnp.dot(gf_ref[...], _hi_lo_rows(jnp.concatenate([br, bi], axis=0)), preferred_element_type=F32)
        if not convolve:
            o_ref = rest[0]
            o_ref[0, 0, s] = x[:inner]
            o_ref[0, 1, s] = x[inner:]
            continue
        h_ref, gi_ref, o_ref = rest
        xr, xi = x[:inner], x[inner:]
        hr, hi = h_ref[0, 0, s], h_ref[0, 1, s]
        yr = xr * hr - xi * hi
        yi = xr * hi + xi * hr
        z = jnp.dot(gi_ref[...], _hi_lo_rows(jnp.concatenate([yr, yi], axis=0)), preferred_element_type=F32)
        zr, zi = z[:inner], z[inner:]
        o_ref[0, 0, s] = zr * tc - zi * ts
        o_ref[0, 1, s] = zi * tc + zr * ts


def _fft_inner(a, tab, c, h=None, h_block=0):
    p = a.shape[0]
    n1, inner = tab['n1'], tab['inner']
    a5 = a.reshape(p, 2, n1, inner, c)
    tc = 2 * LANES
    kb = 4
    blk = pl.BlockSpec((1, 2, kb, inner, tc), lambda i, k, j: (i, 0, k, 0, j))
    tw_spec = pl.BlockSpec((kb, inner, 1), lambda i, k, j: (k, 0, 0))
    ins = [a5, tab['tw_cos'], tab['tw_sin'], tab['inner_fwd']]
    in_specs = [blk, tw_spec, tw_spec, _const_spec(tab['inner_fwd'].shape)]
    if h is not None:
        ch = h.shape[-1] // inner
        nb = c // tc
        ins += [h.reshape(1, 2, n1, inner, ch), tab['inner_inv']]
        in_specs += [pl.BlockSpec((1, 2, kb, inner, tc), lambda i, k, j: (0, 0, k, 0, h_block * nb + j)),
                     _const_spec(tab['inner_inv'].shape)]
    out = pl.pallas_call(
        functools.partial(_inner_kernel, convolve=h is not None, inner=inner, kb=kb),
        grid=(p, n1 // kb, c // tc),
        in_specs=in_specs,
        out_specs=blk,
        out_shape=jax.ShapeDtypeStruct(a5.shape, F32),
        compiler_params=_params('parallel', 'parallel', 'parallel'),
        name='fft_inner',
    )(*ins)
    return out.reshape(p, 2 * n1, inner * c)


def _gate_kernel(m_ref, z_ref, u_ref, x_ref, bias_ref, *rest, chain):
    y = jnp.dot(m_ref[...], z_ref[0], precision=HIGHEST, preferred_element_type=F32)
    nxt = x_ref[0] * (y + u_ref[0] * bias_ref[...])
    if chain:
        mf_ref, o_ref, a_ref = rest
        o_ref[0] = nxt
        a_ref[0] = jnp.dot(mf_ref[...], nxt, precision=HIGHEST, preferred_element_type=F32)
    else:
        rest[0][0] = nxt


def _fft_gate(tab, z, u, x, bias_l, chain, tl=4096):
    p, k2, l = z.shape
    n1 = tab['n1']
    tl = min(tl, l)
    row = pl.BlockSpec((1, n1, tl), lambda i, j: (i, 0, j))
    ins = [tab['outer_inv'], z, u, x, bias_l]
    in_specs = [_const_spec((n1, k2)), pl.BlockSpec((1, k2, tl), lambda i, j: (i, 0, j)), row, row,
                pl.BlockSpec((1, tl), lambda i, j: (0, j))]
    out_specs = [row]
    out_shape = [jax.ShapeDtypeStruct((p, n1, l), F32)]
    if chain:
        ins.append(tab['outer_data'])
        in_specs.append(_const_spec((k2, n1)))
        out_specs.append(pl.BlockSpec((1, k2, tl), lambda i, j: (i, 0, j)))
        out_shape.append(jax.ShapeDtypeStruct((p, k2, l), F32))
    return pl.pallas_call(
        functools.partial(_gate_kernel, chain=chain),
        grid=(p, l // tl),
        in_specs=in_specs,
        out_specs=out_specs,
        out_shape=out_shape,
        compiler_params=_params('parallel', 'parallel'),
        name='fft_gate',
    )(*ins)


def _hyena(parts, conv_w, conv_b, filt_n, bias, inner):
    b, s, c = parts[0].shape
    tab = _fft_tables(s, inner)
    n1 = tab['n1']
    lanes = inner * c
    h_taps = _two_sided(filt_n).reshape(1, n1, inner * HY_ORDER * c)
    h_spec = _fft_inner(_left_mm(tab['outer_real'], h_taps), tab, HY_ORDER * c)
    v, x1, x2 = [a.reshape(b // 2, n1, lanes) for a in _short_conv(parts, conv_w, conv_b)]
    bias_l = [jnp.tile(bias[o], inner).reshape(1, lanes) for o in range(HY_ORDER)]
    a = _left_mm(tab['outer_data'], v)
    z = _fft_inner(a, tab, c, h_spec, 0)
    z2, a = _fft_gate(tab, z, v, x1, bias_l[0], chain=True)
    z = _fft_inner(a, tab, c, h_spec, 1)
    (z3,) = _fft_gate(tab, z, z2, x2, bias_l[1], chain=False)
    return z3.reshape(b, s, c)


def _attn_kernel(*refs, n_q, n_pieces, ncomp, scale, post_scale):
    q_refs = refs[:n_q]
    pos = n_q
    pieces = []
    for _ in range(n_pieces):
        pieces.append((refs[pos:pos + n_q], refs[pos + n_q]))
        pos += n_q + 1
    if ncomp == 2:
        lam_ref, g_ref = refs[pos:pos + 2]
        pos += 2
    o_ref, kcat_ref, vcat_ref = refs[pos:pos + 3]
    dv = o_ref.shape[3]

    @pl.when(pl.program_id(2) == 0)
    def _():
        row = 0
        for k_refs, v_ref in pieces:
            n = v_ref.shape[2]
            for c in range(ncomp):
                parts = [k_ref[0, c if k_ref.shape[1] == ncomp else 0] for k_ref in k_refs]
                kcat_ref[c, row:row + n, :] = parts[0] if n_q == 1 else jnp.concatenate(parts, axis=1)
            vcat_ref[row:row + n, :dv] = v_ref[0, 0]
            vcat_ref[row:row + n, dv:] = jnp.ones((n, dv), BF16)
            row += n

    outs = []
    for c in range(ncomp):
        q = q_refs[0][0, c] if n_q == 1 else jnp.concatenate([q_ref[0, c] for q_ref in q_refs], axis=1)
        s = _dot_nt((q * (scale * math.log2(math.e))).astype(BF16), kcat_ref[c])
        m = jnp.max(s, axis=-1, keepdims=True)
        p = jnp.exp2((s - m).astype(BF16))
        ol = jnp.dot(p, vcat_ref[...], preferred_element_type=F32)
        outs.append(ol[:, :dv] / ol[:, dv:dv + 1])
    if ncomp == 2:
        o = outs[0] - lam_ref[0] * outs[1]
        o = _rms(o) * g_ref[...] * post_scale
    else:
        o = outs[0]
    o_ref[0, 0] = o


def _attention(q_parts, pieces, heads, ncomp, scale, tq=256, lam=None, subln_g=None, post_scale=1.0):
    b, _, sq, _ = q_parts[0].shape
    dv = pieces[0][1].shape[3]
    tq = min(tq, sq)
    ins = list(q_parts)
    in_specs = [pl.BlockSpec((1, ncomp, tq, q.shape[3]), lambda i, h, j: (i, h, j, 0)) for q in q_parts]
    for k_parts, v in pieces:
        for k in k_parts:
            ins.append(k)
            if k.shape[1] == 1:
                in_specs.append(pl.BlockSpec((1, 1) + k.shape[2:], lambda i, h, j: (i, 0, 0, 0)))
            else:
                in_specs.append(pl.BlockSpec((1, ncomp) + k.shape[2:], lambda i, h, j: (i, h, 0, 0)))
        ins.append(v)
        in_specs.append(pl.BlockSpec((1, 1) + v.shape[2:], lambda i, h, j: (i, h, 0, 0)))
    if ncomp == 2:
        ins += [lam.reshape(1), subln_g.reshape(1, dv)]
        in_specs += [pl.BlockSpec(memory_space=pltpu.SMEM), _const_spec((1, dv))]
    sk = sum(v.shape[2] for _, v in pieces)
    dqk = sum(q.shape[3] for q in q_parts)
    return pl.pallas_call(
        functools.partial(_attn_kernel, n_q=len(q_parts), n_pieces=len(pieces), ncomp=ncomp, scale=scale,
                          post_scale=post_scale),
        grid=(b, heads, sq // tq),
        in_specs=in_specs,
        out_specs=pl.BlockSpec((1, 1, tq, dv), lambda i, h, j: (i, h, j, 0)),
        out_shape=jax.ShapeDtypeStruct((b, heads, sq, dv), F32),
        scratch_shapes=[pltpu.VMEM((ncomp, sk, dqk), BF16), pltpu.VMEM((sk, 2 * dv), BF16)],
        compiler_params=_params('parallel', 'parallel', 'arbitrary'),
        name='attention',
    )(*ins)


def _forget_terms(f, log_lb, log_1m_lb, one_m_lb):
    log_sig = jnp.minimum(f, 0.0) - jnp.log1p(jnp.exp(-jnp.abs(f)))
    b = log_1m_lb + log_sig
    log_g = jnp.maximum(log_lb, b) + jnp.log1p(jnp.exp(-jnp.abs(log_lb - b)))
    return log_g, one_m_lb * jax.nn.sigmoid(-f)


def _hg_tables():
    ck, sub = HG_CHUNK, HG_SUB
    t = np.arange(ck)
    cum_mats, half_masks, group_masks, sels = [], [], [], []
    for rev in (False, True):
        mats = [(t[None, :] >= t[:, None]) if rev else (t[None, :] <= t[:, None])]
        halves = []
        hs = ck // 2
        while hs >= sub:
            pos = t % (2 * hs)
            b = t - pos + hs
            mats.append((t[None, :] >= b[:, None]) if rev else (t[None, :] < b[:, None]))
            q_half = (pos < hs) if rev else (pos >= hs)
            halves.append(np.stack([q_half, ~q_half]))
            if not rev:
                grp = (t[:, None] // (2 * hs)) == (t[None, :] // (2 * hs))
                group_masks.append(np.concatenate([grp, grp], axis=0))
            hs //= 2
        cum_mats.append(np.concatenate(mats, axis=0))
        half_masks.append(np.stack(halves))
        r, c = np.arange(ck)[:, None], np.arange(ck * sub)[None, :]
        same = (c // sub) == r
        tt, ss = (c // sub) % sub, c % sub
        sels.append(same & ((ss >= tt) if rev else (ss <= tt)))
    lanes = 2 * HG_KEY_DIM
    ln = np.arange(lanes)
    bd = (ln[:, None] // HG_KEY_DIM) == (ln[None, :] // HG_KEY_DIM)
    hm = np.broadcast_to(np.stack(half_masks)[..., None], (2, len(half_masks[0]), 2, ck, lanes))
    return (jnp.asarray(np.stack(cum_mats), BF16), jnp.asarray(hm, F32), jnp.asarray(np.stack(group_masks), F32),
            jnp.asarray(np.stack(sels), BF16), jnp.asarray(bd, F32))


def _split3(x):
    a = x.astype(BF16)
    r = x - a.astype(F32)
    b = r.astype(BF16)
    return a, b, (r - b.astype(F32)).astype(BF16)


def _hg_chunk(q, k, v, lg, st, rev, cm, hm, gm, sel, bd, m0, m1):
    ck, sub = HG_CHUNK, HG_SUB
    call = sum(jnp.dot(cm, piece, preferred_element_type=F32) for piece in _split3(lg))
    cum = call[0:ck]
    tot = cum[0:1] if rev else cum[ck - 1:ck]
    o = _dot_nt((q * jnp.exp(cum)).astype(BF16), st.astype(BF16))
    kd = (k * jnp.exp(tot - cum)).astype(BF16)
    st_new = st * jnp.exp(tot) + bd * jnp.dot(v.T.astype(BF16), kd, preferred_element_type=F32)
    s2 = None
    for lv in range(gm.shape[0]):
        cb = call[(lv + 1) * ck:(lv + 2) * ck]
        qd = q * jnp.exp(jnp.minimum(cum - cb, 0.0)) * hm[lv, 0]
        kf = (k * jnp.exp(jnp.minimum(cb - cum, 0.0)) * hm[lv, 1]).astype(BF16)
        q2 = jnp.concatenate([qd * m0, qd * m1], axis=0).astype(BF16)
        term = _dot_nt(q2, kf) * gm[lv]
        s2 = term if s2 is None else s2 + term
    r = jnp.dot(s2.astype(BF16), v.astype(BF16), preferred_element_type=F32)
    o = o + m0 * r[:ck] + m1 * r[ck:]
    rows, vts = [], []
    for i in range(ck // sub):
        lo, hi = i * sub, (i + 1) * sub
        ki, ci = k[lo:hi], cum[lo:hi]
        for t in range(lo, hi):
            rows.append((q[t:t + 1] * ki * jnp.exp(jnp.minimum(cum[t:t + 1] - ci, 0.0))).astype(BF16))
            vts.append(v[lo:hi])
    sc = jnp.dot(jnp.concatenate(rows, axis=0), bd.astype(BF16), preferred_element_type=F32)
    o = o + jnp.dot(sel, (sc * jnp.concatenate(vts, axis=0)).astype(BF16), preferred_element_type=F32)
    return o, st_new


def _hgrn_kernel(q_ref, ff_ref, fb_ref, i_ref, qc_ref, ffc_ref, fbc_ref, ic_ref, lb_ref, g_ref,
                 cm_ref, hm_ref, gm_ref, sel_ref, bd_ref, o_ref, oc_ref, or_ref, ocr_ref, st_ref, *, n_lat, n_ctx):
    ck = HG_CHUNK
    lanes = o_ref.shape[-1]
    lane = lax.broadcasted_iota(jnp.int32, (1, lanes), 1)
    m0 = (lane < HG_KEY_DIM).astype(F32)
    m1 = 1.0 - m0
    bd = bd_ref[...]
    gm = gm_ref[...]

    def one(q, f, v, rev):
        d = 1 if rev else 0
        lg, k = _forget_terms(f, lb_ref[d, 0:1, :], lb_ref[d, 1:2, :], lb_ref[d, 2:3, :])
        o, st = _hg_chunk(q, k, v, lg, st_ref[d], rev, cm_ref[d], hm_ref[d], gm, sel_ref[d], bd, m0, m1)
        st_ref[d] = st
        return o

    def sweep(qr, ffr, fbr, ir, out_f, out_r, n):
        nc = n // ck

        def body(step, carry):
            idf = pl.ds(pl.multiple_of(step * ck, ck), ck)
            idr = pl.ds(pl.multiple_of((nc - 1 - step) * ck, ck), ck)
            out_f[0, idf, :] = one(qr[0, idf, :], ffr[0, idf, :], ir[0, idf, :], False)
            out_r[idr, :] = one(qr[0, idr, :], fbr[0, idr, :], ir[0, idr, :], True)
            return carry

        lax.fori_loop(0, nc, body, 0, unroll=2)

    st_ref[...] = jnp.zeros(st_ref.shape, F32)
    sweep(qc_ref, ffc_ref, fbc_ref, ic_ref, oc_ref, ocr_ref, n_ctx)
    sweep(q_ref, ff_ref, fb_ref, i_ref, o_ref, or_ref, n_lat)

    mean_mat = bd * (1.0 / HG_VAL_DIM)

    def readout(out, out_r, n):
        tile = min(n, 512)

        def body(step, carry):
            idx = pl.ds(pl.multiple_of(step * tile, tile), tile)
            x = out[0, idx, :] + out_r[idx, :]
            ms = jnp.dot(x * x, mean_mat, precision=HIGHEST, preferred_element_type=F32)
            out[0, idx, :] = x * lax.rsqrt(ms + NORM_EPS) * g_ref[...]
            return carry

        lax.fori_loop(0, n // tile, body, 0)

    readout(oc_ref, ocr_ref, n_ctx)
    readout(o_ref, or_ref, n_lat)


def _hgrn(q, ff, fb, iv, qc, ffc, fbc, ic, lb_terms, norm_g):
    b, n_lat, width = q.shape
    n_ctx = qc.shape[1]
    lanes = 2 * HG_KEY_DIM
    tables = _hg_tables()
    lat = pl.BlockSpec((1, n_lat, lanes), lambda i, j: (i, 0, j))
    ctx = pl.BlockSpec((1, n_ctx, lanes), lambda i, j: (i, 0, j))
    g2 = jnp.tile(norm_g, 2).reshape(1, lanes)
    return pl.pallas_call(
        functools.partial(_hgrn_kernel, n_lat=n_lat, n_ctx=n_ctx),
        grid=(b, width // lanes),
        in_specs=[lat] * 4 + [ctx] * 4 + [pl.BlockSpec((2, 3, lanes), lambda i, j: (0, 0, j)), _const_spec((1, lanes))]
                 + [_const_spec(t.shape) for t in tables],
        out_specs=[lat, ctx],
        out_shape=[jax.ShapeDtypeStruct(q.shape, F32), jax.ShapeDtypeStruct(qc.shape, F32)],
        scratch_shapes=[pltpu.VMEM((n_lat, lanes), F32), pltpu.VMEM((n_ctx, lanes), F32), pltpu.VMEM((2, lanes, lanes), F32)],
        compiler_params=_params('parallel', 'parallel'),
        name='hgrn2',
    )(q, ff, fb, iv, qc, ffc, fbc, ic, lb_terms, g2, *tables)


def _out_proj_kernel(hy_ref, da_ref, mla_ref, hg_ref, gate_ref, x_ref, g1_ref, w_ref, o_ref):
    c = hy_ref.shape[2]
    acc = jnp.dot(hy_ref[0].astype(BF16), w_ref[0:c, :], preferred_element_type=F32)
    for i, head_ref in ((1, da_ref), (2, mla_ref)):
        dv = head_ref.shape[3]
        for h in range(head_ref.shape[1]):
            lo = i * c + h * dv
            acc = acc + jnp.dot(head_ref[0, h].astype(BF16), w_ref[lo:lo + dv, :], preferred_element_type=F32)
    hg = hg_ref[0] * _silu(gate_ref[0])
    acc = acc + jnp.dot(hg.astype(BF16), w_ref[3 * c:4 * c, :], preferred_element_type=F32)
    o_ref[0] = x_ref[0] + g1_ref[0] * acc


def _out_proj(y_hy, y_da, y_mla, y_hg, gate, x, g1, w_out, tm=512):
    b, s, d = x.shape
    tm = min(tm, s)
    c = y_hy.shape[2]
    part = pl.BlockSpec((1, tm, c), lambda i, j: (i, j, 0))
    headed = lambda a: pl.BlockSpec((1, a.shape[1], tm, a.shape[3]), lambda i, j: (i, 0, j, 0))
    row = pl.BlockSpec((1, tm, d), lambda i, j: (i, j, 0))
    return pl.pallas_call(
        _out_proj_kernel,
        grid=(b, s // tm),
        in_specs=[part, headed(y_da), headed(y_mla), part, part, row, pl.BlockSpec((1, 1, d), lambda i, j: (i, 0, 0)),
                  _const_spec(w_out.shape)],
        out_specs=row,
        out_shape=jax.ShapeDtypeStruct(x.shape, F32),
        compiler_params=_params('parallel', 'parallel'),
        name='out_proj',
    )(y_hy, y_da, y_mla, y_hg, gate, x, g1, w_out)


def _router_kernel(x_ref, g_ref, sc_ref, sh_ref, wrt_ref, bias_ref, *rest, compact):
    h = _rms(x_ref[0]) * g_ref[...] * (1.0 + sc_ref[0]) + sh_ref[0]
    tm = h.shape[0]
    scores = jax.nn.sigmoid(_dot_nt(wrt_ref[...], h, precision=HIGHEST))
    choice = scores + bias_ref[...]
    per = N_EXPERTS // N_EXPERT_GROUPS
    neg = -jnp.inf
    iota_g = lax.broadcasted_iota(jnp.int32, (per, tm), 0)
    grp_rows = []
    for gi in range(N_EXPERT_GROUPS):
        blk = choice[gi * per:(gi + 1) * per]
        m1 = jnp.max(blk, axis=0, keepdims=True)
        first = jnp.min(jnp.where(blk == m1, iota_g, per), axis=0, keepdims=True)
        m2 = jnp.max(jnp.where(iota_g == first, neg, blk), axis=0, keepdims=True)
        grp_rows.append(m1 + m2)
    grp = jnp.concatenate(grp_rows, axis=0)
    iota_n = lax.broadcasted_iota(jnp.int32, (N_EXPERT_GROUPS, tm), 0)
    gsel = jnp.zeros((N_EXPERT_GROUPS, tm), F32)
    for _ in range(TOPK_GROUPS):
        m = jnp.max(grp, axis=0, keepdims=True)
        first = jnp.min(jnp.where(grp == m, iota_n, N_EXPERT_GROUPS), axis=0, keepdims=True)
        hit = iota_n == first
        gsel = jnp.where(hit, 1.0, gsel)
        grp = jnp.where(hit, neg, grp)
    emask = jnp.concatenate([jnp.broadcast_to(gsel[gi:gi + 1], (per, tm)) for gi in range(N_EXPERT_GROUPS)], axis=0)
    cand = jnp.where(emask > 0.0, choice, neg)
    iota_e = lax.broadcasted_iota(jnp.int32, (N_EXPERTS, tm), 0)
    sel = jnp.zeros((N_EXPERTS, tm), F32)
    chosen = []
    for _ in range(TOP_K):
        m = jnp.max(cand, axis=0, keepdims=True)
        first = jnp.min(jnp.where(cand == m, iota_e, N_EXPERTS), axis=0, keepdims=True)
        hit = iota_e == first
        sel = jnp.where(hit, 1.0, sel)
        cand = jnp.where(hit, neg, cand)
        chosen.append(first)
    w = scores * sel
    gate = w / jnp.sum(w, axis=0, keepdims=True) * ROUTED_SCALE
    if not compact:
        h_ref, gate_ref = rest
        h_ref[0] = h.astype(BF16)
        gate_ref[0] = gate
        return
    hp_ref, eid_ref, rank_ref, w_ref, cnt_out_ref, cnt_ref = rest
    hp_ref[0] = _pack_halves(h)

    @pl.when((pl.program_id(0) == 0) & (pl.program_id(1) == 0))
    def _():
        cnt_ref[...] = jnp.zeros(cnt_ref.shape, F32)

    src = lax.broadcasted_iota(jnp.int32, (tm, tm), 0)
    dst = lax.broadcasted_iota(jnp.int32, (tm, tm), 1)
    running = jnp.dot(sel.astype(BF16), (src <= dst).astype(BF16), preferred_element_type=F32)
    rank_dense = cnt_ref[:, 0:1] + running - 1.0
    e_rows, r_rows, w_rows = [], [], []
    for first in chosen:
        hit = iota_e == first
        e_rows.append(first)
        r_rows.append(jnp.sum(jnp.where(hit, rank_dense, 0.0), axis=0, keepdims=True))
        w_rows.append(jnp.sum(jnp.where(hit, gate, 0.0), axis=0, keepdims=True))
    eid_ref[...] = jnp.concatenate(e_rows, axis=0)
    rank_ref[...] = jnp.concatenate(r_rows, axis=0).astype(jnp.int32)
    w_ref[...] = jnp.concatenate(w_rows, axis=0)
    cnt_ref[...] = cnt_ref[...] + running[:, tm - 1:tm]
    cnt_out_ref[...] = cnt_ref[...]


def _router(x, g, scale, shift, w_router, e_bias, tm=512, compact=False):
    b, s, d = x.shape
    tm = min(tm, s)
    e = w_router.shape[1]
    row = pl.BlockSpec((1, tm, d), lambda i, j: (i, j, 0))
    mod = pl.BlockSpec((1, 1, d), lambda i, j: (i, 0, 0))
    if compact:
        nj = s // tm
        tok = pl.BlockSpec((TOP_K, tm), lambda i, j: (0, i * nj + j))
        out_specs = [pl.BlockSpec((1, tm, d // 2), lambda i, j: (i, j, 0)), tok, tok, tok, _const_spec((e, LANES))]
        out_shape = [jax.ShapeDtypeStruct((b, s, d // 2), jnp.int32), jax.ShapeDtypeStruct((TOP_K, b * s), jnp.int32),
                     jax.ShapeDtypeStruct((TOP_K, b * s), jnp.int32), jax.ShapeDtypeStruct((TOP_K, b * s), F32),
                     jax.ShapeDtypeStruct((e, LANES), F32)]
        scratch = [pltpu.VMEM((e, LANES), F32)]
        semantics = ('arbitrary', 'arbitrary')
    else:
        out_specs = [row, pl.BlockSpec((1, e, tm), lambda i, j: (i, 0, j))]
        out_shape = [jax.ShapeDtypeStruct((b, s, d), BF16), jax.ShapeDtypeStruct((b, e, s), F32)]
        scratch = []
        semantics = ('parallel', 'parallel')
    return pl.pallas_call(
        functools.partial(_router_kernel, compact=compact),
        grid=(b, s // tm),
        in_specs=[row, _const_spec((1, d)), mod, mod, _const_spec((e, d)), _const_spec((e, 1))],
        out_specs=out_specs,
        out_shape=out_shape,
        scratch_shapes=scratch,
        compiler_params=_params(*semantics),
        name='router',
    )(x, g.reshape(1, d), scale, shift, w_router.T, e_bias.reshape(e, 1))


def _moe_kernel(h_ref, x_ref, gate_ref, g2_ref, wg_ref, wu_ref, wd_ref, sg_ref, su_ref, sd_ref, *rest, final):
    if final:
        fg_ref, o_ref, acc_ref = rest
    else:
        o_ref, acc_ref = rest
    e = pl.program_id(2)
    h = h_ref[0]

    @pl.when(e == 0)
    def _():
        a = jnp.dot(h, sg_ref[...], preferred_element_type=F32)
        u = jnp.dot(h, su_ref[...], preferred_element_type=F32)
        acc_ref[...] = jnp.dot((_silu(a) * u).astype(BF16), sd_ref[...], preferred_element_type=F32)

    lane = lax.broadcasted_iota(jnp.int32, gate_ref.shape[1:], 1)
    gcol = jnp.sum(jnp.where(lane == e, gate_ref[0], 0.0), axis=-1, keepdims=True)
    a = jnp.dot(h, wg_ref[0].astype(BF16), preferred_element_type=F32)
    u = jnp.dot(h, wu_ref[0].astype(BF16), preferred_element_type=F32)
    acc_ref[...] += jnp.dot((_silu(a) * u * gcol).astype(BF16), wd_ref[0].astype(BF16), preferred_element_type=F32)

    @pl.when(e == pl.num_programs(2) - 1)
    def _():
        y = x_ref[0] + g2_ref[0] * acc_ref[...]
        if final:
            y = _rms(y) * fg_ref[...]
        o_ref[0] = y


def _moe(h2, x, gate, g2, layer, w_gate, w_up, w_down, s_gate, s_up, s_down, final_g=None, tm=1024):
    b, s, d = x.shape
    tm = min(tm, s)
    _, e, _, ff = w_gate.shape
    row = pl.BlockSpec((1, tm, d), lambda i, j, k: (i, j, 0))
    ins = [h2, x, gate, g2, w_gate, w_up, w_down, s_gate, s_up, s_down]
    in_specs = [row, row, pl.BlockSpec((1, tm, e), lambda i, j, k: (i, j, 0)),
                pl.BlockSpec((1, 1, d), lambda i, j, k: (i, 0, 0)),
                pl.BlockSpec((None, 1, d, ff), lambda i, j, k: (layer, k, 0, 0)),
                pl.BlockSpec((None, 1, d, ff), lambda i, j, k: (layer, k, 0, 0)),
                pl.BlockSpec((None, 1, ff, d), lambda i, j, k: (layer, k, 0, 0)),
                _const_spec(s_gate.shape), _const_spec(s_up.shape), _const_spec(s_down.shape)]
    if final_g is not None:
        ins.append(final_g.reshape(1, d))
        in_specs.append(_const_spec((1, d)))
    return pl.pallas_call(
        functools.partial(_moe_kernel, final=final_g is not None),
        grid=(b, s // tm, e),
        in_specs=in_specs,
        out_specs=row,
        out_shape=jax.ShapeDtypeStruct(x.shape, F32),
        scratch_shapes=[pltpu.VMEM((tm, d), F32)],
        compiler_params=_params('parallel', 'parallel', 'arbitrary'),
        name='moe',
    )(*ins)


MOE_ROW_TILE = 512
SC_ROWS = 128
V7X_SC_CORES = 2
V7X_SC_SUBCORES = 16


def _pack_halves(x):
    n = x.shape[1] // 2
    lo = pltpu.bitcast(x[:, :n].astype(BF16).astype(F32), jnp.int32)
    hi = pltpu.bitcast(x[:, n:].astype(BF16).astype(F32), jnp.int32)
    return jnp.bitwise_or(jnp.bitwise_and(hi, -65536), lax.shift_right_logical(lo, 16))


def _unpack_halves(p):
    lo = pltpu.bitcast(lax.shift_left(p, 16), F32).astype(BF16)
    hi = pltpu.bitcast(jnp.bitwise_and(p, -65536), F32).astype(BF16)
    return lo, hi


def _route_pos_kernel(off_ref, eid_ref, rank_ref, pos_ref):
    eid = eid_ref[...]
    base = jnp.zeros(eid.shape, jnp.int32)
    for e in range(N_EXPERTS):
        base = jnp.where(eid == e, off_ref[e], base)
    pos_ref[...] = base + rank_ref[...]


def _route_pos(offsets, eid, rank):
    return pl.pallas_call(
        _route_pos_kernel,
        grid=(1,),
        in_specs=[pl.BlockSpec(memory_space=pltpu.SMEM), _const_spec(eid.shape), _const_spec(rank.shape)],
        out_specs=_const_spec(eid.shape),
        out_shape=jax.ShapeDtypeStruct(eid.shape, jnp.int32),
        compiler_params=_params('arbitrary'),
        name='route_pos',
    )(offsets, eid, rank)


def _sc_mesh():
    return plsc.VectorSubcoreMesh(core_axis_name='c', subcore_axis_name='s', num_cores=V7X_SC_CORES,
                                  num_subcores=V7X_SC_SUBCORES)


def _sc_dispatch(hp, pos, n_rows):
    t, w = hp.shape
    k = pos.shape[0]
    workers = V7X_SC_CORES * V7X_SC_SUBCORES
    per_worker = t // workers
    pos_flat = pos.reshape(k * t)

    @functools.partial(pl.kernel, mesh=_sc_mesh(), out_type=jax.ShapeDtypeStruct((n_rows, w), jnp.int32),
                       scratch_types=[pltpu.VMEM((SC_ROWS,), jnp.int32), pltpu.VMEM((SC_ROWS, w), jnp.int32),
                                      pltpu.SemaphoreType.DMA])
    def scatter(hp_hbm, pos_hbm, out_hbm, idx_v, rows_v, sem):
        wid = lax.axis_index('s') * V7X_SC_CORES + lax.axis_index('c')

        @pl.loop(0, per_worker // SC_ROWS)
        def _(i):
            t0 = pl.multiple_of(wid * per_worker + i * SC_ROWS, SC_ROWS)
            pltpu.sync_copy(hp_hbm.at[pl.ds(t0, SC_ROWS)], rows_v)
            for j in range(k):
                pltpu.sync_copy(pos_hbm.at[pl.ds(pl.multiple_of(j * t + t0, SC_ROWS), SC_ROWS)], idx_v)
                pltpu.async_copy(rows_v, out_hbm.at[idx_v], sem).wait()

    return scatter(hp, pos_flat)


def _sc_collect(yp, pos):
    _, w = yp.shape
    k, t = pos.shape
    workers = V7X_SC_CORES * V7X_SC_SUBCORES
    per_worker = k * t // workers
    pos_flat = pos.reshape(k * t)

    @functools.partial(pl.kernel, mesh=_sc_mesh(), out_type=jax.ShapeDtypeStruct((k * t, w), jnp.int32),
                       scratch_types=[pltpu.VMEM((SC_ROWS,), jnp.int32), pltpu.VMEM((SC_ROWS, w), jnp.int32),
                                      pltpu.SemaphoreType.DMA])
    def gather(yp_hbm, pos_hbm, out_hbm, idx_v, rows_v, sem):
        wid = lax.axis_index('s') * V7X_SC_CORES + lax.axis_index('c')

        @pl.loop(0, per_worker // SC_ROWS)
        def _(i):
            r0 = pl.multiple_of(wid * per_worker + i * SC_ROWS, SC_ROWS)
            pltpu.sync_copy(pos_hbm.at[pl.ds(r0, SC_ROWS)], idx_v)
            pltpu.async_copy(yp_hbm.at[idx_v], rows_v, sem).wait()
            pltpu.sync_copy(rows_v, out_hbm.at[pl.ds(r0, SC_ROWS)])

    return gather(yp, pos_flat)


def _expert_kernel(te_ref, nu_ref, x_ref, wg_ref, wu_ref, wd_ref, o_ref, wg_s, wu_s, wd_s):
    i = pl.program_id(0)

    @pl.when(i < nu_ref[0])
    def _():
        @pl.when((i == 0) | (te_ref[i] != te_ref[jnp.maximum(i - 1, 0)]))
        def _():
            wg_s[...] = wg_ref[0].astype(BF16)
            wu_s[...] = wu_ref[0].astype(BF16)
            wd_s[...] = wd_ref[0].astype(BF16)

        lo, hi = _unpack_halves(x_ref[...])
        half = lo.shape[1]
        a = (jnp.dot(lo, wg_s[:half, :], preferred_element_type=F32)
             + jnp.dot(hi, wg_s[half:, :], preferred_element_type=F32))
        u = (jnp.dot(lo, wu_s[:half, :], preferred_element_type=F32)
             + jnp.dot(hi, wu_s[half:, :], preferred_element_type=F32))
        y = jnp.dot((_silu(a) * u).astype(BF16), wd_s[...], preferred_element_type=F32)
        o_ref[...] = _pack_halves(y)


def _experts(xp, tile_expert, n_used, layer, w_gate, w_up, w_down):
    n_rows, half = xp.shape
    _, _, d, ff = w_gate.shape
    r = MOE_ROW_TILE
    row = pl.BlockSpec((r, half), lambda i, te, nu: (i, 0))
    grid_spec = pltpu.PrefetchScalarGridSpec(
        num_scalar_prefetch=2,
        grid=(n_rows // r,),
        in_specs=[row,
                  pl.BlockSpec((None, 1, d, ff), lambda i, te, nu: (layer, te[i], 0, 0)),
                  pl.BlockSpec((None, 1, d, ff), lambda i, te, nu: (layer, te[i], 0, 0)),
                  pl.BlockSpec((None, 1, ff, d), lambda i, te, nu: (layer, te[i], 0, 0))],
        out_specs=row,
        scratch_shapes=[pltpu.VMEM((d, ff), BF16), pltpu.VMEM((d, ff), BF16), pltpu.VMEM((ff, d), BF16)],
    )
    return pl.pallas_call(
        _expert_kernel,
        grid_spec=grid_spec,
        out_shape=jax.ShapeDtypeStruct((n_rows, half), jnp.int32),
        compiler_params=_params('arbitrary'),
        name='experts',
    )(tile_expert, n_used, xp, w_gate, w_up, w_down)


def _combine_kernel(yg_ref, w_ref, hp_ref, x_ref, g2_ref, sg_ref, su_ref, sd_ref, *rest, final):
    if final:
        fg_ref, o_ref = rest
    else:
        (o_ref,) = rest
    half = hp_ref.shape[2]
    lo, hi = _unpack_halves(hp_ref[0])
    sg, su = sg_ref[...], su_ref[...]
    a = jnp.dot(lo, sg[:half], preferred_element_type=F32) + jnp.dot(hi, sg[half:], preferred_element_type=F32)
    u = jnp.dot(lo, su[:half], preferred_element_type=F32) + jnp.dot(hi, su[half:], preferred_element_type=F32)
    acc = jnp.dot((_silu(a) * u).astype(BF16), sd_ref[...], preferred_element_type=F32)
    acc_lo, acc_hi = acc[:, :half], acc[:, half:]
    wts = w_ref[0]
    for k in range(yg_ref.shape[0]):
        ylo, yhi = _unpack_halves(yg_ref[k, 0])
        wk = wts[:, k:k + 1]
        acc_lo = acc_lo + wk * ylo.astype(F32)
        acc_hi = acc_hi + wk * yhi.astype(F32)
    y = x_ref[0] + g2_ref[0] * jnp.concatenate([acc_lo, acc_hi], axis=1)
    if final:
        y = _rms(y) * fg_ref[...]
    o_ref[0] = y


def _combine(yg, wts, hp, x, g2, s_gate, s_up, s_down, final_g=None, tm=256):
    b, s, d = x.shape
    k = yg.shape[0]
    half = d // 2
    row = pl.BlockSpec((1, tm, d), lambda i, j: (i, j, 0))
    prow = pl.BlockSpec((1, tm, half), lambda i, j: (i, j, 0))
    ins = [yg, wts, hp, x, g2, s_gate, s_up, s_down]
    in_specs = [pl.BlockSpec((k, 1, tm, half), lambda i, j: (0, i, j, 0)), pl.BlockSpec((1, tm, k), lambda i, j: (i, j, 0)),
                prow, row, pl.BlockSpec((1, 1, d), lambda i, j: (i, 0, 0)),
                _const_spec(s_gate.shape), _const_spec(s_up.shape), _const_spec(s_down.shape)]
    if final_g is not None:
        ins.append(final_g.reshape(1, d))
        in_specs.append(_const_spec((1, d)))
    return pl.pallas_call(
        functools.partial(_combine_kernel, final=final_g is not None),
        grid=(b, s // tm),
        in_specs=in_specs,
        out_specs=row,
        out_shape=jax.ShapeDtypeStruct(x.shape, F32),
        compiler_params=_params('parallel', 'parallel'),
        name='moe_combine',
    )(*ins)


def _routed_moe(x, g, scale, shift, g2, w_router, e_bias, layer, w_gate, w_up, w_down, s_gate, s_up, s_down, final_g=None):
    b, s, d = x.shape
    t = b * s
    hp, eid, rank, wts, counts = _router(x, g, scale, shift, w_router, e_bias, compact=True)
    counts = counts[:, 0].astype(jnp.int32)
    r = MOE_ROW_TILE
    padded = (counts + (r - 1)) // r * r
    ends = jnp.cumsum(padded)
    offsets = ends - padded
    n_rows = t * TOP_K + N_EXPERTS * r
    tile_start = jnp.arange(n_rows // r, dtype=jnp.int32) * r
    tile_expert = jnp.minimum(jnp.sum((tile_start[:, None] >= ends[None, :]).astype(jnp.int32), axis=1), N_EXPERTS - 1)
    n_used = (ends[-1] // r).reshape(1).astype(jnp.int32)
    pos = _route_pos(offsets.astype(jnp.int32), eid, rank)
    xp = _sc_dispatch(hp.reshape(t, d // 2), pos, n_rows)
    yp = _experts(xp, tile_expert.astype(jnp.int32), n_used, layer, w_gate, w_up, w_down)
    yg = _sc_collect(yp, pos).reshape(TOP_K, b, s, d // 2)
    return _combine(yg, wts.T.reshape(b, s, TOP_K), hp, x, g2, s_gate, s_up, s_down, final_g)


def _mixers(p, pc, ctx_out, prm, l, lam_init, rope_tabs, lb_terms):
    s = p['hy_v'].shape[1]
    sc = pc['hy_v'].shape[1]

    hy_args = (prm['hy_w1'][l], prm['hy_b1'][l], prm['hy_w2'][l], prm['hy_b2'][l], prm['hy_w3'][l], prm['hy_b3'][l],
               prm['hy_sin_freq'][l], prm['hy_decay'][l])
    y_hy = _hyena([p['hy_v'], p['hy_x1'], p['hy_x2']], prm['hy_conv_w'][l], prm['hy_conv_b'][l],
                  _hy_filters(s, *hy_args), prm['hy_bias'][l], inner=128)
    yc_hy = None
    if ctx_out:
        yc_hy = _hyena([pc['hy_v'], pc['hy_x1'], pc['hy_x2']], prm['hy_conv_w'][l], prm['hy_conv_b'][l],
                       _hy_filters(sc, *hy_args), prm['hy_bias'][l], inner=32)

    lp = prm['da_lambda'][l].astype(F32)
    lam = jnp.exp(jnp.sum(lp[0] * lp[1])) - jnp.exp(jnp.sum(lp[2] * lp[3])) + lam_init
    da_kw = dict(heads=DA_HEADS, ncomp=2, scale=DA_HEAD_DIM ** -0.5, lam=lam, subln_g=prm['da_subln_g'][l],
                 post_scale=1.0 - lam_init)
    da_ctx = ([pc['da_k']], pc['da_v'])
    y_da = _attention([p['da_q']], [da_ctx, ([p['da_k']], p['da_v'])], **da_kw)
    yc_da = _attention([pc['da_q']], [da_ctx], **da_kw) if ctx_out else None

    wq = prm['mla_w_q_up'][l].reshape(MLA_Q_RANK, MLA_HEADS, MLA_NOPE_DIM + MLA_ROPE_DIM)
    wq_n = wq[:, :, :MLA_NOPE_DIM].reshape(MLA_Q_RANK, -1).astype(BF16)
    wq_r = wq[:, :, MLA_NOPE_DIM:].reshape(MLA_Q_RANK, -1).astype(BF16)
    wkv = prm['mla_w_kv_up'][l].reshape(MLA_KV_RANK, MLA_HEADS, MLA_NOPE_DIM + MLA_V_DIM)
    wkv_n = wkv[:, :, :MLA_NOPE_DIM].reshape(MLA_KV_RANK, -1).astype(BF16)
    wkv_v = wkv[:, :, MLA_NOPE_DIM:].reshape(MLA_KV_RANK, -1).astype(BF16)

    def queries(qd, tabs):
        return _norm_proj(qd, prm['mla_q_norm_g'][l], [(wq_n, F32, False, MLA_HEADS), (wq_r, F32, True, MLA_HEADS)],
                          rope_tabs=tabs)

    def keys_values(kvd):
        return _norm_proj(kvd, prm['mla_kv_norm_g'][l], [(wkv_n, BF16, False, MLA_HEADS), (wkv_v, BF16, False, MLA_HEADS)])

    kn_l, v_l = keys_values(p['mla_kv'])
    kn_c, v_c = keys_values(pc['mla_kv'])
    mla_kw = dict(heads=MLA_HEADS, ncomp=1, scale=(MLA_NOPE_DIM + MLA_ROPE_DIM) ** -0.5)
    mla_ctx = ([kn_c, pc['mla_kr']], v_c)
    y_mla = _attention(queries(p['mla_q'], rope_tabs), [mla_ctx, ([kn_l, p['mla_kr']], v_l)], **mla_kw)
    yc_mla = _attention(queries(pc['mla_q'], None), [mla_ctx], **mla_kw) if ctx_out else None

    o, oc = _hgrn(p['hg_q'], p['hg_ff'], p['hg_fb'], p['hg_i'], pc['hg_q'], pc['hg_ff'], pc['hg_fb'], pc['hg_i'],
                  lb_terms, prm['hg_norm_g'][l])
    return (y_hy, y_da, y_mla, o), (yc_hy, yc_da, yc_mla, oc)


def kernel(x, c, ctx, c_ctx, w_ada, b_ada, norm1_g, norm2_g, w_in, w_out, hy_conv_w, hy_conv_b, hy_w1, hy_b1, hy_w2, hy_b2, hy_w3, hy_b3, hy_sin_freq, hy_decay, hy_bias, da_lambda, da_subln_g, mla_q_norm_g, mla_w_q_up, mla_kv_norm_g, mla_w_kv_up, hg_lower_bounds, hg_norm_g, moe_w_router, moe_bias, moe_w_gate, moe_w_up, moe_w_down, moe_sh_gate, moe_sh_up, moe_sh_down, final_norm_g):
    prm = dict(hy_conv_w=hy_conv_w, hy_conv_b=hy_conv_b, hy_w1=hy_w1, hy_b1=hy_b1, hy_w2=hy_w2, hy_b2=hy_b2,
               hy_w3=hy_w3, hy_b3=hy_b3, hy_sin_freq=hy_sin_freq, hy_decay=hy_decay, hy_bias=hy_bias,
               da_lambda=da_lambda, da_subln_g=da_subln_g, mla_q_norm_g=mla_q_norm_g, mla_w_q_up=mla_w_q_up,
               mla_kv_norm_g=mla_kv_norm_g, mla_w_kv_up=mla_w_kv_up, hg_norm_g=hg_norm_g)
    b, n_lat, d = x.shape
    depth = w_in.shape[0]
    rows = n_lat // GRID_W
    row_pos = jnp.repeat(jnp.arange(rows, dtype=jnp.int32), GRID_W)
    col_pos = jnp.tile(jnp.arange(GRID_W, dtype=jnp.int32), rows)
    rope_tabs = _rope_tables(row_pos, col_pos, 2 * DA_HEADS * DA_HEAD_DIM)
    lbs = jnp.cumsum(jax.nn.softmax(hg_lower_bounds.astype(F32), axis=1), axis=1)
    lbs = lbs - lbs[:, :1]
    cond = jnp.concatenate([c, c_ctx[None], jnp.zeros((8 - b - 1, d), F32)], axis=0)

    for l in range(depth):
        ctx_out = l < depth - 1
        mods = _ada(cond, w_ada[l], b_ada[l])
        sh1, sc1, g1, sh2, sc2, g2 = [m[:, None, :] for m in jnp.split(mods[:b], 6, axis=-1)]
        mc = [jnp.broadcast_to(m[:, None, :], (b, 1, d)) for m in jnp.split(mods[b:b + 1], 6, axis=-1)]

        off = 0
        outs = []
        for _, wdt, dt, rope, split, rep in _SEGMENTS:
            w = w_in[l][:, off:off + wdt].astype(BF16)
            outs.append((jnp.tile(w, (1, rep)) if rep > 1 else w, dt, rope, split))
            off += wdt
        names = [seg[0] for seg in _SEGMENTS]
        p = dict(zip(names, _norm_proj(x, norm1_g[l], outs, sc1, sh1, rope_tabs=rope_tabs)))
        pc = dict(zip(names, _norm_proj(ctx, norm1_g[l], outs, mc[1], mc[0])))

        lb = lbs[:, l]
        lb_terms = jnp.stack([jnp.log(lb), jnp.log1p(-lb), 1.0 - lb], axis=1)
        lam_init = 0.8 - 0.6 * math.exp(-0.3 * l)
        lat_parts, ctx_parts = _mixers(p, pc, ctx_out, prm, l, lam_init, rope_tabs, lb_terms)

        w_out_b = w_out[l].astype(BF16)
        moe_w = (l, moe_w_gate, moe_w_up, moe_w_down,
                 moe_sh_gate[l].astype(BF16), moe_sh_up[l].astype(BF16), moe_sh_down[l].astype(BF16))

        if ctx_out:
            ctx = _out_proj(*ctx_parts, pc['hg_g'], ctx, mc[2], w_out_b)
            flat = ctx.reshape(1, -1, d)
            h2c, gate_c = _router(flat, norm2_g[l], mc[4][:1], mc[3][:1], moe_w_router[l], moe_bias[l])
            ctx = _moe(h2c, flat, gate_c.transpose(0, 2, 1), mc[5][:1], *moe_w).reshape(ctx.shape)

        x = _out_proj(*lat_parts, p['hg_g'], x, g1, w_out_b)
        x = _routed_moe(x, norm2_g[l], sc2, sh2, g2, moe_w_router[l], moe_bias[l], *moe_w,
                        final_g=None if ctx_out else final_norm_g)

    return x
```

```python
import functools
import math

import numpy as np
import jax
import jax.numpy as jnp
from jax import lax
from jax.experimental import pallas as pl
from jax.experimental.pallas import tpu as pltpu
from jax.experimental.pallas import tpu_sc as plsc

F32 = jnp.float32
BF16 = jnp.bfloat16
HIGHEST = lax.Precision.HIGHEST

D_MODEL = 1024
GRID_W = 64
HY_WIDTH = 256
HY_ORDER = 2
HY_BANDS = 16
DA_HEADS = 4
DA_HEAD_DIM = 32
MLA_HEADS = 4
MLA_Q_RANK = 192
MLA_KV_RANK = 128
MLA_NOPE_DIM = 64
MLA_ROPE_DIM = 32
MLA_V_DIM = 64
HG_HEADS = 4
HG_KEY_DIM = 64
HG_VAL_DIM = 64
HG_CHUNK = 64
HG_SUB = 8
N_EXPERTS = 64
N_EXPERT_GROUPS = 8
TOPK_GROUPS = 4
TOP_K = 8
EXPERT_FF = 256
ROUTED_SCALE = 2.5
ROPE_BASE = 10000.0
NORM_EPS = 1e-6

V7X_VMEM_LIMIT_BYTES = 56 * 1024 * 1024
LANES = 128

_SEGMENTS = (
    ('hy_v', HY_WIDTH, F32, False, 0, 1), ('hy_x1', HY_WIDTH, F32, False, 0, 1), ('hy_x2', HY_WIDTH, F32, False, 0, 1),
    ('da_q', 2 * DA_HEADS * DA_HEAD_DIM, F32, True, 2 * DA_HEADS, 1),
    ('da_k', 2 * DA_HEADS * DA_HEAD_DIM, BF16, True, 2 * DA_HEADS, 1),
    ('da_v', 2 * DA_HEADS * DA_HEAD_DIM, BF16, False, DA_HEADS, 1),
    ('mla_q', MLA_Q_RANK, F32, False, 0, 1), ('mla_kv', MLA_KV_RANK, F32, False, 0, 1),
    ('mla_kr', MLA_ROPE_DIM, BF16, True, MLA_HEADS, MLA_HEADS),
    ('hg_q', HG_HEADS * HG_KEY_DIM, F32, False, 0, 1), ('hg_ff', HG_HEADS * HG_KEY_DIM, F32, False, 0, 1),
    ('hg_fb', HG_HEADS * HG_KEY_DIM, F32, False, 0, 1), ('hg_i', HG_HEADS * HG_VAL_DIM, F32, False, 0, 1),
    ('hg_g', HG_HEADS * HG_VAL_DIM, F32, False, 0, 1),
)


def _params(*semantics):
    return pltpu.CompilerParams(dimension_semantics=semantics, vmem_limit_bytes=V7X_VMEM_LIMIT_BYTES)


def _const_spec(shape):
    nd = len(shape)
    return pl.BlockSpec(shape, lambda *_: (0,) * nd)


def _rms(x, eps=NORM_EPS):
    return x * lax.rsqrt(jnp.mean(x * x, axis=-1, keepdims=True) + eps)


def _silu(x):
    return x * jax.nn.sigmoid(x)


def _dot_nt(a, b, **kw):
    return lax.dot_general(a, b, (((1,), (1,)), ((), ())), preferred_element_type=F32, **kw)


def _ada_kernel(c_ref, w_ref, b_ref, o_ref):
    s = _silu(c_ref[...])
    o_ref[...] = jnp.dot(s, w_ref[...], precision=HIGHEST, preferred_element_type=F32) + b_ref[...]


def _ada(cond, w, b):
    r, d = cond.shape
    n = w.shape[1]
    tn = 1536
    return pl.pallas_call(
        _ada_kernel,
        grid=(n // tn,),
        in_specs=[_const_spec((r, d)), pl.BlockSpec((d, tn), lambda j: (0, j)), pl.BlockSpec((1, tn), lambda j: (0, j))],
        out_specs=pl.BlockSpec((r, tn), lambda j: (0, j)),
        out_shape=jax.ShapeDtypeStruct((r, n), F32),
        compiler_params=_params('arbitrary'),
        name='ada',
    )(cond, w, b.reshape(1, n))


ROPE_UNIT = 32


def _rope_tables(row, col, width):
    n = ROPE_UNIT // 4
    inv = ROPE_BASE ** (-jnp.arange(n, dtype=F32) / n)
    units = width // ROPE_UNIT
    parts_c, parts_a, parts_b = [], [], []
    zero = jnp.zeros((row.shape[0], n), F32)
    for pos in (row, col):
        ang = pos.astype(F32)[:, None] * inv
        cos, sin = jnp.cos(ang), jnp.sin(ang)
        parts_c += [cos, cos]
        parts_a += [zero, sin]
        parts_b += [-sin, zero]
    tile = lambda ps: jnp.tile(jnp.concatenate(ps, axis=1), (1, units))
    return tile(parts_c), tile(parts_a), tile(parts_b)


def _norm_proj_kernel(*refs, n_w, modulate, ropes, splits):
    x_ref, g_ref = refs[0], refs[1]
    pos = 2
    if modulate:
        sc_ref, sh_ref = refs[2], refs[3]
        pos = 4
    if any(ropes):
        rc_ref, ra_ref, rb_ref = refs[pos:pos + 3]
        pos += 3
    w_refs = refs[pos:pos + n_w]
    o_refs = refs[pos + n_w:]
    y = _rms(x_ref[0]) * g_ref[...]
    if modulate:
        y = y * (1.0 + sc_ref[0]) + sh_ref[0]
    yb = y.astype(BF16)
    for w_ref, o_ref, rope, split in zip(w_refs, o_refs, ropes, splits):
        o = jnp.dot(yb, w_ref[...], preferred_element_type=F32)
        if rope:
            wd = o.shape[1]
            shift = ROPE_UNIT // 4
            o = (o * rc_ref[:, :wd] + pltpu.roll(o, shift, axis=1) * ra_ref[:, :wd]
                 + pltpu.roll(o, wd - shift, axis=1) * rb_ref[:, :wd])
        if split:
            unit = o.shape[1] // split
            for u in range(split):
                o_ref[0, u] = o[:, u * unit:(u + 1) * unit].astype(o_ref.dtype)
        else:
            o_ref[0] = o.astype(o_ref.dtype)


def _norm_proj(x, g, outs, scale=None, shift=None, rope_tabs=None, tm=512):
    b, s, k = x.shape
    tm = min(tm, s)
    modulate = scale is not None
    ropes = tuple(bool(o[2]) and rope_tabs is not None for o in outs)
    splits = tuple(o[3] for o in outs)
    ins = [x, g.reshape(1, k)]
    in_specs = [pl.BlockSpec((1, tm, k), lambda i, j: (i, j, 0)), _const_spec((1, k))]
    if modulate:
        ins += [scale, shift]
        in_specs += [pl.BlockSpec((1, 1, k), lambda i, j: (i, 0, 0))] * 2
    if any(ropes):
        ins += list(rope_tabs)
        in_specs += [pl.BlockSpec((tm, rope_tabs[0].shape[1]), lambda i, j: (j, 0))] * 3
    out_specs, out_shape = [], []
    for w, dt, _, split in outs:
        ins.append(w)
        in_specs.append(_const_spec(w.shape))
        n = w.shape[1]
        if split:
            out_specs.append(pl.BlockSpec((1, split, tm, n // split), lambda i, j: (i, 0, j, 0)))
            out_shape.append(jax.ShapeDtypeStruct((b, split, s, n // split), dt))
        else:
            out_specs.append(pl.BlockSpec((1, tm, n), lambda i, j: (i, j, 0)))
            out_shape.append(jax.ShapeDtypeStruct((b, s, n), dt))
    return pl.pallas_call(
        functools.partial(_norm_proj_kernel, n_w=len(outs), modulate=modulate, ropes=ropes, splits=splits),
        grid=(b, s // tm),
        in_specs=in_specs,
        out_specs=out_specs,
        out_shape=out_shape,
        compiler_params=_params('parallel', 'parallel'),
        name='norm_proj',
    )(*ins)


def _hy_filter_kernel(w1t_ref, w1s_ref, w1c_ref, b1_ref, w2_ref, b2_ref, w3_ref, b3_ref, fr_ref, dec_ref, o_ref, *, n):
    t = lax.broadcasted_iota(jnp.int32, (n, 1), 0).astype(F32) / n
    bands = lax.broadcasted_iota(jnp.int32, (1, HY_BANDS), 1).astype(F32) + 1.0
    ang = (2.0 * jnp.pi) * t * bands
    pre = (t * w1t_ref[...]
           + jnp.dot(jnp.sin(ang), w1s_ref[...], precision=HIGHEST, preferred_element_type=F32)
           + jnp.dot(jnp.cos(ang), w1c_ref[...], precision=HIGHEST, preferred_element_type=F32)
           + b1_ref[...])
    hid = jnp.sin(fr_ref[0:1, :] * pre)
    hid = jnp.sin(fr_ref[1:2, :] * (jnp.dot(hid, w2_ref[...], precision=HIGHEST, preferred_element_type=F32) + b2_ref[...]))
    filt = jnp.dot(hid, w3_ref[...], precision=HIGHEST, preferred_element_type=F32) + b3_ref[...]
    filt = filt * jnp.exp(-t * jnp.abs(dec_ref[...]))
    col = jnp.sum(jnp.abs(filt), axis=0, keepdims=True) - jnp.abs(filt[0:1, :])
    w = HY_WIDTH
    for o in range(HY_ORDER):
        lo = o * 2 * w
        f0 = filt[0:1, lo:lo + w] + filt[0:1, lo + w:lo + 2 * w]
        inv = 1.0 / (col[:, lo:lo + w] + col[:, lo + w:lo + 2 * w] + jnp.abs(f0))
        o_ref[:, lo:lo + w] = filt[:, lo:lo + w] * inv
        o_ref[:, lo + w:lo + 2 * w] = filt[:, lo + w:lo + 2 * w] * inv


def _hy_filters(n, w1, b1, w2, b2, w3, b3, freq, decay):
    cols = w3.shape[1]
    ins = [w1[0:1], w1[1:1 + HY_BANDS], w1[1 + HY_BANDS:], b1.reshape(1, -1), w2, b2.reshape(1, -1), w3,
           b3.reshape(1, -1), freq, decay.reshape(1, -1)]
    out = pl.pallas_call(
        functools.partial(_hy_filter_kernel, n=n),
        grid=(1,),
        in_specs=[_const_spec(a.shape) for a in ins],
        out_specs=_const_spec((n, cols)),
        out_shape=jax.ShapeDtypeStruct((n, cols), F32),
        compiler_params=_params('arbitrary'),
        name='hy_filter',
    )(*ins)
    return out.reshape(n, HY_ORDER, 2, HY_WIDTH)


def _two_sided(filt_n):
    n = filt_n.shape[0]
    hf, hb = filt_n[:, :, 0], filt_n[:, :, 1]
    h = jnp.concatenate([hf[:1] + hb[:1], hf[1:], jnp.zeros((1,) + hf.shape[1:], F32), hb[:0:-1]], axis=0)
    return h.reshape(2 * n, HY_ORDER * HY_WIDTH)


def _short_conv_kernel(*refs, s):
    x_refs, w_refs, b_refs, o_refs = refs[0:3], refs[3:6], refs[6:9], refs[9:12]
    row = lax.broadcasted_iota(jnp.int32, (s, 1), 0)
    for x_ref, w_ref, b_ref, o_ref in zip(x_refs, w_refs, b_refs, o_refs):
        x = x_ref[0]
        prev = jnp.where(row == 0, 0.0, pltpu.roll(x, 1, axis=0))
        nxt = jnp.where(row == s - 1, 0.0, pltpu.roll(x, s - 1, axis=0))
        o_ref[0] = prev * w_ref[0:1, :] + x * w_ref[1:2, :] + nxt * w_ref[2:3, :] + b_ref[...]


def _short_conv(parts, conv_w, conv_b):
    b, s, c = parts[0].shape
    tc = LANES
    ws = [conv_w[:, i * c:(i + 1) * c] for i in range(3)]
    bs = [conv_b[i * c:(i + 1) * c].reshape(1, c) for i in range(3)]
    xspec = pl.BlockSpec((1, s, tc), lambda i, j: (i, 0, j))
    return pl.pallas_call(
        functools.partial(_short_conv_kernel, s=s),
        grid=(b, c // tc),
        in_specs=[xspec] * 3 + [pl.BlockSpec((3, tc), lambda i, j: (0, j))] * 3 + [pl.BlockSpec((1, tc), lambda i, j: (0, j))] * 3,
        out_specs=[xspec] * 3,
        out_shape=[jax.ShapeDtypeStruct((b, s, c), F32)] * 3,
        compiler_params=_params('parallel', 'parallel'),
        name='short_conv',
    )(*parts, *ws, *bs)


def _dft_cos_sin(rows, cols, period):
    ang = 2.0 * np.pi * ((np.arange(rows)[:, None] * np.arange(cols)[None, :]) % period) / period
    return np.cos(ang), np.sin(ang)


def _fft_tables(n, inner):
    big = 2 * n
    n1 = big // inner
    c1, s1 = _dft_cos_sin(n1, n1, n1)
    h = n1 // 2
    outer_data = np.block([[c1[:, :h], s1[:, :h]], [-s1[:, :h], c1[:, :h]]])
    outer_real = np.concatenate([c1, -s1], axis=0)
    outer_inv = np.block([[c1[:h, :], -s1[:h, :]], [s1[:h, :], c1[:h, :]]]) / big
    c2, s2 = _dft_cos_sin(inner, inner, inner)
    inner_fwd = np.block([[c2, s2], [-s2, c2]])
    inner_inv = np.block([[c2, -s2], [s2, c2]])
    ct, st = _dft_cos_sin(n1, inner, big)
    f = lambda a: jnp.asarray(a, F32)
    return dict(n1=n1, inner=inner, outer_data=f(outer_data), outer_real=f(outer_real), outer_inv=f(outer_inv),
                inner_fwd=_hi_lo_cols(inner_fwd), inner_inv=_hi_lo_cols(inner_inv),
                tw_cos=f(ct).reshape(n1, inner, 1), tw_sin=f(st).reshape(n1, inner, 1))


def _left_mm_kernel(m_ref, x_ref, o_ref):
    o_ref[0] = jnp.dot(m_ref[...], x_ref[0], precision=HIGHEST, preferred_element_type=F32)


def _left_mm(m, x, tl=4096):
    p, k, l = x.shape
    mm = m.shape[0]
    tl = min(tl, l)
    return pl.pallas_call(
        _left_mm_kernel,
        grid=(p, l // tl),
        in_specs=[_const_spec(m.shape), pl.BlockSpec((1, k, tl), lambda i, j: (i, 0, j))],
        out_specs=pl.BlockSpec((1, mm, tl), lambda i, j: (i, 0, j)),
        out_shape=jax.ShapeDtypeStruct((p, mm, l), F32),
        compiler_params=_params('parallel', 'parallel'),
        name='fft_outer',
    )(m, x)


def _hi_lo_cols(m):
    m = np.asarray(m, np.float32)
    hi = m.astype(BF16)
    lo = (m - hi.astype(np.float32)).astype(BF16)
    return jnp.asarray(np.concatenate([hi, hi, lo], axis=1))


def _hi_lo_rows(x):
    hi = x.astype(BF16)
    lo = (x - hi.astype(F32)).astype(BF16)
    return jnp.concatenate([hi, lo, hi], axis=0)


def _inner_kernel(a_ref, twc_ref, tws_ref, gf_ref, *rest, convolve, inner, kb):
    for s in range(kb):
        ar, ai = a_ref[0, 0, s], a_ref[0, 1, s]
        tc, ts = twc_ref[s], tws_ref[s]
        br = ar * tc + ai * ts
        bi = ai * tc - ar * ts
        x = jnp.dot(gf_ref[...], _hi_lo_rows(jnp.concatenate([br, bi], axis=0)), preferred_element_type=F32)
        if not convolve:
            o_ref = rest[0]
            o_ref[0, 0, s] = x[:inner]
            o_ref[0, 1, s] = x[inner:]
            continue
        h_ref, gi_ref, o_ref = rest
        xr, xi = x[:inner], x[inner:]
        hr, hi = h_ref[0, 0, s], h_ref[0, 1, s]
        yr = xr * hr - xi * hi
        yi = xr * hi + xi * hr
        z = jnp.dot(gi_ref[...], _hi_lo_rows(jnp.concatenate([yr, yi], axis=0)), preferred_element_type=F32)
        zr, zi = z[:inner], z[inner:]
        o_ref[0, 0, s] = zr * tc - zi * ts
        o_ref[0, 1, s] = zi * tc + zr * ts


def _fft_inner(a, tab, c, h=None, h_block=0):
    p = a.shape[0]
    n1, inner = tab['n1'], tab['inner']
    a5 = a.reshape(p, 2, n1, inner, c)
    tc = 2 * LANES
    kb = 4
    blk = pl.BlockSpec((1, 2, kb, inner, tc), lambda i, k, j: (i, 0, k, 0, j))
    tw_spec = pl.BlockSpec((kb, inner, 1), lambda i, k, j: (k, 0, 0))
    ins = [a5, tab['tw_cos'], tab['tw_sin'], tab['inner_fwd']]
    in_specs = [blk, tw_spec, tw_spec, _const_spec(tab['inner_fwd'].shape)]
    if h is not None:
        ch = h.shape[-1] // inner
        nb = c // tc
        ins += [h.reshape(1, 2, n1, inner, ch), tab['inner_inv']]
        in_specs += [pl.BlockSpec((1, 2, kb, inner, tc), lambda i, k, j: (0, 0, k, 0, h_block * nb + j)),
                     _const_spec(tab['inner_inv'].shape)]
    out = pl.pallas_call(
        functools.partial(_inner_kernel, convolve=h is not None, inner=inner, kb=kb),
        grid=(p, n1 // kb, c // tc),
        in_specs=in_specs,
        out_specs=blk,
        out_shape=jax.ShapeDtypeStruct(a5.shape, F32),
        compiler_params=_params('parallel', 'parallel', 'parallel'),
        name='fft_inner',
    )(*ins)
    return out.reshape(p, 2 * n1, inner * c)


def _gate_kernel(m_ref, z_ref, u_ref, x_ref, bias_ref, *rest, chain):
    y = jnp.dot(m_ref[...], z_ref[0], precision=HIGHEST, preferred_element_type=F32)
    nxt = x_ref[0] * (y + u_ref[0] * bias_ref[...])
    if chain:
        mf_ref, o_ref, a_ref = rest
        o_ref[0] = nxt
        a_ref[0] = jnp.dot(mf_ref[...], nxt, precision=HIGHEST, preferred_element_type=F32)
    else:
        rest[0][0] = nxt


def _fft_gate(tab, z, u, x, bias_l, chain, tl=4096):
    p, k2, l = z.shape
    n1 = tab['n1']
    tl = min(tl, l)
    row = pl.BlockSpec((1, n1, tl), lambda i, j: (i, 0, j))
    ins = [tab['outer_inv'], z, u, x, bias_l]
    in_specs = [_const_spec((n1, k2)), pl.BlockSpec((1, k2, tl), lambda i, j: (i, 0, j)), row, row,
                pl.BlockSpec((1, tl), lambda i, j: (0, j))]
    out_specs = [row]
    out_shape = [jax.ShapeDtypeStruct((p, n1, l), F32)]
    if chain:
        ins.append(tab['outer_data'])
        in_specs.append(_const_spec((k2, n1)))
        out_specs.append(pl.BlockSpec((1, k2, tl), lambda i, j: (i, 0, j)))
        out_shape.append(jax.ShapeDtypeStruct((p, k2, l), F32))
    return pl.pallas_call(
        functools.partial(_gate_kernel, chain=chain),
        grid=(p, l // tl),
        in_specs=in_specs,
        out_specs=out_specs,
        out_shape=out_shape,
        compiler_params=_params('parallel', 'parallel'),
        name='fft_gate',
    )(*ins)


def _hyena(parts, conv_w, conv_b, filt_n, bias, inner):
    b, s, c = parts[0].shape
    tab = _fft_tables(s, inner)
    n1 = tab['n1']
    lanes = inner * c
    h_taps = _two_sided(filt_n).reshape(1, n1, inner * HY_ORDER * c)
    h_spec = _fft_inner(_left_mm(tab['outer_real'], h_taps), tab, HY_ORDER * c)
    v, x1, x2 = [a.reshape(b // 2, n1, lanes) for a in _short_conv(parts, conv_w, conv_b)]
    bias_l = [jnp.tile(bias[o], inner).reshape(1, lanes) for o in range(HY_ORDER)]
    a = _left_mm(tab['outer_data'], v)
    z = _fft_inner(a, tab, c, h_spec, 0)
    z2, a = _fft_gate(tab, z, v, x1, bias_l[0], chain=True)
    z = _fft_inner(a, tab, c, h_spec, 1)
    (z3,) = _fft_gate(tab, z, z2, x2, bias_l[1], chain=False)
    return z3.reshape(b, s, c)


def _attn_kernel(*refs, n_q, n_pieces, ncomp, scale, post_scale):
    q_refs = refs[:n_q]
    pos = n_q
    pieces = []
    for _ in range(n_pieces):
        pieces.append((refs[pos:pos + n_q], refs[pos + n_q]))
        pos += n_q + 1
    if ncomp == 2:
        lam_ref, g_ref = refs[pos:pos + 2]
        pos += 2
    o_ref, kcat_ref, vcat_ref = refs[pos:pos + 3]
    dv = o_ref.shape[3]

    @pl.when(pl.program_id(2) == 0)
    def _():
        row = 0
        for k_refs, v_ref in pieces:
            n = v_ref.shape[2]
            for c in range(ncomp):
                parts = [k_ref[0, c if k_ref.shape[1] == ncomp else 0] for k_ref in k_refs]
                kcat_ref[c, row:row + n, :] = parts[0] if n_q == 1 else jnp.concatenate(parts, axis=1)
            vcat_ref[row:row + n, :dv] = v_ref[0, 0]
            vcat_ref[row:row + n, dv:] = jnp.ones((n, dv), BF16)
            row += n

    outs = []
    for c in range(ncomp):
        q = q_refs[0][0, c] if n_q == 1 else jnp.concatenate([q_ref[0, c] for q_ref in q_refs], axis=1)
        s = _dot_nt((q * (scale * math.log2(math.e))).astype(BF16), kcat_ref[c])
        m = jnp.max(s, axis=-1, keepdims=True)
        p = jnp.exp2((s - m).astype(BF16))
        ol = jnp.dot(p, vcat_ref[...], preferred_element_type=F32)
        outs.append(ol[:, :dv] / ol[:, dv:dv + 1])
    if ncomp == 2:
        o = outs[0] - lam_ref[0] * outs[1]
        o = _rms(o) * g_ref[...] * post_scale
    else:
        o = outs[0]
    o_ref[0, 0] = o


def _attention(q_parts, pieces, heads, ncomp, scale, tq=256, lam=None, subln_g=None, post_scale=1.0):
    b, _, sq, _ = q_parts[0].shape
    dv = pieces[0][1].shape[3]
    tq = min(tq, sq)
    ins = list(q_parts)
    in_specs = [pl.BlockSpec((1, ncomp, tq, q.shape[3]), lambda i, h, j: (i, h, j, 0)) for q in q_parts]
    for k_parts, v in pieces:
        for k in k_parts:
            ins.append(k)
            if k.shape[1] == 1:
                in_specs.append(pl.BlockSpec((1, 1) + k.shape[2:], lambda i, h, j: (i, 0, 0, 0)))
            else:
                in_specs.append(pl.BlockSpec((1, ncomp) + k.shape[2:], lambda i, h, j: (i, h, 0, 0)))
        ins.append(v)
        in_specs.append(pl.BlockSpec((1, 1) + v.shape[2:], lambda i, h, j: (i, h, 0, 0)))
    if ncomp == 2:
        ins += [lam.reshape(1), subln_g.reshape(1, dv)]
        in_specs += [pl.BlockSpec(memory_space=pltpu.SMEM), _const_spec((1, dv))]
    sk = sum(v.shape[2] for _, v in pieces)
    dqk = sum(q.shape[3] for q in q_parts)
    return pl.pallas_call(
        functools.partial(_attn_kernel, n_q=len(q_parts), n_pieces=len(pieces), ncomp=ncomp, scale=scale,
                          post_scale=post_scale),
        grid=(b, heads, sq // tq),
        in_specs=in_specs,
        out_specs=pl.BlockSpec((1, 1, tq, dv), lambda i, h, j: (i, h, j, 0)),
        out_shape=jax.ShapeDtypeStruct((b, heads, sq, dv), F32),
        scratch_shapes=[pltpu.VMEM((ncomp, sk, dqk), BF16), pltpu.VMEM((sk, 2 * dv), BF16)],
        compiler_params=_params('parallel', 'parallel', 'arbitrary'),
        name='attention',
    )(*ins)


def _forget_terms(f, log_lb, log_1m_lb, one_m_lb):
    log_sig = jnp.minimum(f, 0.0) - jnp.log1p(jnp.exp(-jnp.abs(f)))
    b = log_1m_lb + log_sig
    log_g = jnp.maximum(log_lb, b) + jnp.log1p(jnp.exp(-jnp.abs(log_lb - b)))
    return log_g, one_m_lb * jax.nn.sigmoid(-f)


def _hg_tables():
    ck, sub = HG_CHUNK, HG_SUB
    t = np.arange(ck)
    cum_mats, half_masks, group_masks, sels, keeps = [], [], [], [], []
    for rev in (False, True):
        mats = [(t[None, :] >= t[:, None]) if rev else (t[None, :] <= t[:, None])]
        halves = []
        hs = ck // 2
        while hs >= sub:
            pos = t % (2 * hs)
            b = t - pos + hs
            mats.append((t[None, :] >= b[:, None]) if rev else (t[None, :] < b[:, None]))
            q_half = (pos < hs) if rev else (pos >= hs)
            halves.append(np.stack([q_half, ~q_half]))
            if not rev:
                grp = (t[:, None] // (2 * hs)) == (t[None, :] // (2 * hs))
                group_masks.append(np.concatenate([grp, grp], axis=0))
            hs //= 2
        cum_mats.append(np.concatenate(mats, axis=0))
        half_masks.append(np.stack(halves))
        r, c = np.arange(ck)[:, None], np.arange(ck * sub)[None, :]
        same = (c // sub) == r
        tt, ss = (c // sub) % sub, c % sub
        causal = (ss >= tt) if rev else (ss <= tt)
        sels.append(same & causal)
        keeps.append(causal[0])
    lanes = 2 * HG_KEY_DIM
    ln = np.arange(lanes)
    bd = (ln[:, None] // HG_KEY_DIM) == (ln[None, :] // HG_KEY_DIM)
    hm = np.broadcast_to(np.stack(half_masks)[..., None], (2, len(half_masks[0]), 2, ck, lanes))
    return (jnp.asarray(np.stack(cum_mats), BF16), jnp.asarray(hm, F32), jnp.asarray(np.stack(group_masks), F32),
            jnp.asarray(np.stack(sels), BF16), jnp.asarray(bd, F32),
            jnp.asarray(np.broadcast_to(np.stack(keeps)[..., None], (2, ck * sub, lanes)), F32))


def _split3(x):
    a = x.astype(BF16)
    r = x - a.astype(F32)
    b = r.astype(BF16)
    return a, b, (r - b.astype(F32)).astype(BF16)


def _hg_chunk(q, k, v, lg, st, rev, cm, hm, gm, sel, bd, m0, m1, variant, keep=None):
    ck, sub = HG_CHUNK, HG_SUB
    if variant:
        lanes = lg.shape[1]
        c3 = jnp.dot(cm, jnp.concatenate(_split3(lg), axis=1), preferred_element_type=F32)
        call = c3[:, :lanes] + c3[:, lanes:2 * lanes] + c3[:, 2 * lanes:]
    else:
        call = sum(jnp.dot(cm, piece, preferred_element_type=F32) for piece in _split3(lg))
    cum = call[0:ck]
    tot = cum[0:1] if rev else cum[ck - 1:ck]
    o = _dot_nt((q * jnp.exp(cum)).astype(BF16), st.astype(BF16))
    kd = (k * jnp.exp(tot - cum)).astype(BF16)
    st_new = st * jnp.exp(tot) + bd * jnp.dot(v.T.astype(BF16), kd, preferred_element_type=F32)
    s2 = None
    for lv in range(gm.shape[0]):
        cb = call[(lv + 1) * ck:(lv + 2) * ck]
        qd = q * jnp.exp(jnp.minimum(cum - cb, 0.0)) * hm[lv, 0]
        kf = (k * jnp.exp(jnp.minimum(cb - cum, 0.0)) * hm[lv, 1]).astype(BF16)
        q2 = jnp.concatenate([qd * m0, qd * m1], axis=0).astype(BF16)
        term = _dot_nt(q2, kf) * gm[lv]
        s2 = term if s2 is None else s2 + term
    r = jnp.dot(s2.astype(BF16), v.astype(BF16), preferred_element_type=F32)
    o = o + m0 * r[:ck] + m1 * r[ck:]
    rows, vts = [], []
    for i in range(ck // sub):
        lo, hi = i * sub, (i + 1) * sub
        ki, ci = k[lo:hi], cum[lo:hi]
        for t in range(lo, hi):
            rows.append(q[t:t + 1] * ki * jnp.exp(jnp.minimum(cum[t:t + 1] - ci, 0.0)))
            vts.append(v[lo:hi])
    sc = jnp.dot(jnp.concatenate(rows, axis=0).astype(BF16), bd.astype(BF16), preferred_element_type=F32)
    prod = sc * jnp.concatenate(vts, axis=0)
    if variant:
        o = o + jnp.sum((prod * keep).reshape(ck, sub, prod.shape[1]), axis=1)
    else:
        o = o + jnp.dot(sel, prod.astype(BF16), preferred_element_type=F32)
    return o, st_new


def _hgrn_kernel(q_ref, ff_ref, fb_ref, i_ref, qc_ref, ffc_ref, fbc_ref, ic_ref, lb_ref, g_ref,
                 cm_ref, hm_ref, gm_ref, sel_ref, bd_ref, keep_ref, o_ref, oc_ref, or_ref, ocr_ref, st_ref,
                 *, n_lat, n_ctx, variant, unroll):
    ck = HG_CHUNK
    lanes = o_ref.shape[-1]
    lane = lax.broadcasted_iota(jnp.int32, (1, lanes), 1)
    m0 = (lane < HG_KEY_DIM).astype(F32)
    m1 = 1.0 - m0
    bd = bd_ref[...]
    gm = gm_ref[...]

    def one(q, f, v, rev):
        d = 1 if rev else 0
        lg, k = _forget_terms(f, lb_ref[d, 0:1, :], lb_ref[d, 1:2, :], lb_ref[d, 2:3, :])
        o, st = _hg_chunk(q, k, v, lg, st_ref[d], rev, cm_ref[d], hm_ref[d], gm, sel_ref[d], bd, m0, m1, variant,
                          keep_ref[d] if variant else None)
        st_ref[d] = st
        return o

    def sweep(qr, ffr, fbr, ir, out_f, out_r, n):
        nc = n // ck

        def body(step, carry):
            idf = pl.ds(pl.multiple_of(step * ck, ck), ck)
            idr = pl.ds(pl.multiple_of((nc - 1 - step) * ck, ck), ck)
            out_f[0, idf, :] = one(qr[0, idf, :], ffr[0, idf, :], ir[0, idf, :], False)
            out_r[idr, :] = one(qr[0, idr, :], fbr[0, idr, :], ir[0, idr, :], True)
            return carry

        lax.fori_loop(0, nc, body, 0, unroll=unroll)

    st_ref[...] = jnp.zeros(st_ref.shape, F32)
    sweep(qc_ref, ffc_ref, fbc_ref, ic_ref, oc_ref, ocr_ref, n_ctx)
    sweep(q_ref, ff_ref, fb_ref, i_ref, o_ref, or_ref, n_lat)

    mean_mat = bd * (1.0 / HG_VAL_DIM)

    def readout(out, out_r, n):
        tile = min(n, 512)

        def body(step, carry):
            idx = pl.ds(pl.multiple_of(step * tile, tile), tile)
            x = out[0, idx, :] + out_r[idx, :]
            ms = jnp.dot(x * x, mean_mat, precision=HIGHEST, preferred_element_type=F32)
            out[0, idx, :] = x * lax.rsqrt(ms + NORM_EPS) * g_ref[...]
            return carry

        lax.fori_loop(0, n // tile, body, 0)

    readout(oc_ref, ocr_ref, n_ctx)
    readout(o_ref, or_ref, n_lat)


def _hgrn(q, ff, fb, iv, qc, ffc, fbc, ic, lb_terms, norm_g, variant=False, unroll=2):
    b, n_lat, width = q.shape
    n_ctx = qc.shape[1]
    lanes = 2 * HG_KEY_DIM
    tables = _hg_tables()
    lat = pl.BlockSpec((1, n_lat, lanes), lambda i, j: (i, 0, j))
    ctx = pl.BlockSpec((1, n_ctx, lanes), lambda i, j: (i, 0, j))
    g2 = jnp.tile(norm_g, 2).reshape(1, lanes)
    return pl.pallas_call(
        functools.partial(_hgrn_kernel, n_lat=n_lat, n_ctx=n_ctx, variant=variant, unroll=unroll),
        grid=(b, width // lanes),
        in_specs=[lat] * 4 + [ctx] * 4 + [pl.BlockSpec((2, 3, lanes), lambda i, j: (0, 0, j)), _const_spec((1, lanes))]
                 + [_const_spec(t.shape) for t in tables],
        out_specs=[lat, ctx],
        out_shape=[jax.ShapeDtypeStruct(q.shape, F32), jax.ShapeDtypeStruct(qc.shape, F32)],
        scratch_shapes=[pltpu.VMEM((n_lat, lanes), F32), pltpu.VMEM((n_ctx, lanes), F32), pltpu.VMEM((2, lanes, lanes), F32)],
        compiler_params=_params('parallel', 'parallel'),
        name='hgrn2',
    )(q, ff, fb, iv, qc, ffc, fbc, ic, lb_terms, g2, *tables)


def _out_proj_kernel(hy_ref, da_ref, mla_ref, hg_ref, gate_ref, x_ref, g1_ref, w_ref, o_ref):
    c = hy_ref.shape[2]
    acc = jnp.dot(hy_ref[0].astype(BF16), w_ref[0:c, :], preferred_element_type=F32)
    for i, head_ref in ((1, da_ref), (2, mla_ref)):
        dv = head_ref.shape[3]
        for h in range(head_ref.shape[1]):
            lo = i * c + h * dv
            acc = acc + jnp.dot(head_ref[0, h].astype(BF16), w_ref[lo:lo + dv, :], preferred_element_type=F32)
    hg = hg_ref[0] * _silu(gate_ref[0])
    acc = acc + jnp.dot(hg.astype(BF16), w_ref[3 * c:4 * c, :], preferred_element_type=F32)
    o_ref[0] = x_ref[0] + g1_ref[0] * acc


def _out_proj(y_hy, y_da, y_mla, y_hg, gate, x, g1, w_out, tm=512):
    b, s, d = x.shape
    tm = min(tm, s)
    c = y_hy.shape[2]
    part = pl.BlockSpec((1, tm, c), lambda i, j: (i, j, 0))
    headed = lambda a: pl.BlockSpec((1, a.shape[1], tm, a.shape[3]), lambda i, j: (i, 0, j, 0))
    row = pl.BlockSpec((1, tm, d), lambda i, j: (i, j, 0))
    return pl.pallas_call(
        _out_proj_kernel,
        grid=(b, s // tm),
        in_specs=[part, headed(y_da), headed(y_mla), part, part, row, pl.BlockSpec((1, 1, d), lambda i, j: (i, 0, 0)),
                  _const_spec(w_out.shape)],
        out_specs=row,
        out_shape=jax.ShapeDtypeStruct(x.shape, F32),
        compiler_params=_params('parallel', 'parallel'),
        name='out_proj',
    )(y_hy, y_da, y_mla, y_hg, gate, x, g1, w_out)


def _router_kernel(x_ref, g_ref, sc_ref, sh_ref, wrt_ref, bias_ref, *rest, compact):
    h = _rms(x_ref[0]) * g_ref[...] * (1.0 + sc_ref[0]) + sh_ref[0]
    tm = h.shape[0]
    scores = jax.nn.sigmoid(_dot_nt(wrt_ref[...], h, precision=HIGHEST))
    choice = scores + bias_ref[...]
    per = N_EXPERTS // N_EXPERT_GROUPS
    neg = -jnp.inf
    iota_g = lax.broadcasted_iota(jnp.int32, (per, tm), 0)
    grp_rows = []
    for gi in range(N_EXPERT_GROUPS):
        blk = choice[gi * per:(gi + 1) * per]
        m1 = jnp.max(blk, axis=0, keepdims=True)
        first = jnp.min(jnp.where(blk == m1, iota_g, per), axis=0, keepdims=True)
        m2 = jnp.max(jnp.where(iota_g == first, neg, blk), axis=0, keepdims=True)
        grp_rows.append(m1 + m2)
    grp = jnp.concatenate(grp_rows, axis=0)
    iota_n = lax.broadcasted_iota(jnp.int32, (N_EXPERT_GROUPS, tm), 0)
    gsel = jnp.zeros((N_EXPERT_GROUPS, tm), F32)
    for _ in range(TOPK_GROUPS):
        m = jnp.max(grp, axis=0, keepdims=True)
        first = jnp.min(jnp.where(grp == m, iota_n, N_EXPERT_GROUPS), axis=0, keepdims=True)
        hit = iota_n == first
        gsel = jnp.where(hit, 1.0, gsel)
        grp = jnp.where(hit, neg, grp)
    emask = jnp.concatenate([jnp.broadcast_to(gsel[gi:gi + 1], (per, tm)) for gi in range(N_EXPERT_GROUPS)], axis=0)
    cand = jnp.where(emask > 0.0, choice, neg)
    iota_e = lax.broadcasted_iota(jnp.int32, (N_EXPERTS, tm), 0)
    sel = jnp.zeros((N_EXPERTS, tm), F32)
    chosen = []
    for _ in range(TOP_K):
        m = jnp.max(cand, axis=0, keepdims=True)
        first = jnp.min(jnp.where(cand == m, iota_e, N_EXPERTS), axis=0, keepdims=True)
        hit = iota_e == first
        sel = jnp.where(hit, 1.0, sel)
        cand = jnp.where(hit, neg, cand)
        chosen.append(first)
    w = scores * sel
    gate = w / jnp.sum(w, axis=0, keepdims=True) * ROUTED_SCALE
    if not compact:
        h_ref, gate_ref = rest
        h_ref[0] = h.astype(BF16)
        gate_ref[0] = gate
        return
    hp_ref, eid_ref, rank_ref, w_ref, cnt_out_ref, cnt_ref = rest
    hp_ref[0] = _pack_halves(h)

    @pl.when((pl.program_id(0) == 0) & (pl.program_id(1) == 0))
    def _():
        cnt_ref[...] = jnp.zeros(cnt_ref.shape, F32)

    src = lax.broadcasted_iota(jnp.int32, (tm, tm), 0)
    dst = lax.broadcasted_iota(jnp.int32, (tm, tm), 1)
    running = jnp.dot(sel.astype(BF16), (src <= dst).astype(BF16), preferred_element_type=F32)
    rank_dense = cnt_ref[:, 0:1] + running - 1.0
    e_rows, r_rows, w_rows = [], [], []
    for first in chosen:
        hit = iota_e == first
        e_rows.append(first)
        r_rows.append(jnp.sum(jnp.where(hit, rank_dense, 0.0), axis=0, keepdims=True))
        w_rows.append(jnp.sum(jnp.where(hit, gate, 0.0), axis=0, keepdims=True))
    eid_ref[...] = jnp.concatenate(e_rows, axis=0)
    rank_ref[...] = jnp.concatenate(r_rows, axis=0).astype(jnp.int32)
    w_ref[...] = jnp.concatenate(w_rows, axis=0)
    cnt_ref[...] = cnt_ref[...] + running[:, tm - 1:tm]
    cnt_out_ref[...] = cnt_ref[...]


def _router(x, g, scale, shift, w_router, e_bias, tm=512, compact=False):
    b, s, d = x.shape
    tm = min(tm, s)
    e = w_router.shape[1]
    row = pl.BlockSpec((1, tm, d), lambda i, j: (i, j, 0))
    mod = pl.BlockSpec((1, 1, d), lambda i, j: (i, 0, 0))
    if compact:
        nj = s // tm
        tok = pl.BlockSpec((TOP_K, tm), lambda i, j: (0, i * nj + j))
        out_specs = [pl.BlockSpec((1, tm, d // 2), lambda i, j: (i, j, 0)), tok, tok, tok, _const_spec((e, LANES))]
        out_shape = [jax.ShapeDtypeStruct((b, s, d // 2), jnp.int32), jax.ShapeDtypeStruct((TOP_K, b * s), jnp.int32),
                     jax.ShapeDtypeStruct((TOP_K, b * s), jnp.int32), jax.ShapeDtypeStruct((TOP_K, b * s), F32),
                     jax.ShapeDtypeStruct((e, LANES), F32)]
        scratch = [pltpu.VMEM((e, LANES), F32)]
        semantics = ('arbitrary', 'arbitrary')
    else:
        out_specs = [row, pl.BlockSpec((1, e, tm), lambda i, j: (i, 0, j))]
        out_shape = [jax.ShapeDtypeStruct((b, s, d), BF16), jax.ShapeDtypeStruct((b, e, s), F32)]
        scratch = []
        semantics = ('parallel', 'parallel')
    return pl.pallas_call(
        functools.partial(_router_kernel, compact=compact),
        grid=(b, s // tm),
        in_specs=[row, _const_spec((1, d)), mod, mod, _const_spec((e, d)), _const_spec((e, 1))],
        out_specs=out_specs,
        out_shape=out_shape,
        scratch_shapes=scratch,
        compiler_params=_params(*semantics),
        name='router',
    )(x, g.reshape(1, d), scale, shift, w_router.T, e_bias.reshape(e, 1))


def _moe_kernel(h_ref, x_ref, gate_ref, g2_ref, wg_ref, wu_ref, wd_ref, sg_ref, su_ref, sd_ref, *rest, final):
    if final:
        fg_ref, o_ref, acc_ref = rest
    else:
        o_ref, acc_ref = rest
    e = pl.program_id(2)
    h = h_ref[0]

    @pl.when(e == 0)
    def _():
        a = jnp.dot(h, sg_ref[...], preferred_element_type=F32)
        u = jnp.dot(h, su_ref[...], preferred_element_type=F32)
        acc_ref[...] = jnp.dot((_silu(a) * u).astype(BF16), sd_ref[...], preferred_element_type=F32)

    lane = lax.broadcasted_iota(jnp.int32, gate_ref.shape[1:], 1)
    gcol = jnp.sum(jnp.where(lane == e, gate_ref[0], 0.0), axis=-1, keepdims=True)
    a = jnp.dot(h, wg_ref[0].astype(BF16), preferred_element_type=F32)
    u = jnp.dot(h, wu_ref[0].astype(BF16), preferred_element_type=F32)
    acc_ref[...] += jnp.dot((_silu(a) * u * gcol).astype(BF16), wd_ref[0].astype(BF16), preferred_element_type=F32)

    @pl.when(e == pl.num_programs(2) - 1)
    def _():
        y = x_ref[0] + g2_ref[0] * acc_ref[...]
        if final:
            y = _rms(y) * fg_ref[...]
        o_ref[0] = y


def _moe(h2, x, gate, g2, layer, w_gate, w_up, w_down, s_gate, s_up, s_down, final_g=None, tm=1024):
    b, s, d = x.shape
    tm = min(tm, s)
    _, e, _, ff = w_gate.shape
    row = pl.BlockSpec((1, tm, d), lambda i, j, k: (i, j, 0))
    ins = [h2, x, gate, g2, w_gate, w_up, w_down, s_gate, s_up, s_down]
    in_specs = [row, row, pl.BlockSpec((1, tm, e), lambda i, j, k: (i, j, 0)),
                pl.BlockSpec((1, 1, d), lambda i, j, k: (i, 0, 0)),
                pl.BlockSpec((None, 1, d, ff), lambda i, j, k: (layer, k, 0, 0)),
                pl.BlockSpec((None, 1, d, ff), lambda i, j, k: (layer, k, 0, 0)),
                pl.BlockSpec((None, 1, ff, d), lambda i, j, k: (layer, k, 0, 0)),
                _const_spec(s_gate.shape), _const_spec(s_up.shape), _const_spec(s_down.shape)]
    if final_g is not None:
        ins.append(final_g.reshape(1, d))
        in_specs.append(_const_spec((1, d)))
    return pl.pallas_call(
        functools.partial(_moe_kernel, final=final_g is not None),
        grid=(b, s // tm, e),
        in_specs=in_specs,
        out_specs=row,
        out_shape=jax.ShapeDtypeStruct(x.shape, F32),
        scratch_shapes=[pltpu.VMEM((tm, d), F32)],
        compiler_params=_params('parallel', 'parallel', 'arbitrary'),
        name='moe',
    )(*ins)


MOE_ROW_TILE = 512
SC_ROWS = 128
V7X_SC_CORES = 2
V7X_SC_SUBCORES = 16


def _pack_halves(x):
    n = x.shape[1] // 2
    lo = pltpu.bitcast(x[:, :n].astype(BF16).astype(F32), jnp.int32)
    hi = pltpu.bitcast(x[:, n:].astype(BF16).astype(F32), jnp.int32)
    return jnp.bitwise_or(jnp.bitwise_and(hi, -65536), lax.shift_right_logical(lo, 16))


def _unpack_halves(p):
    lo = pltpu.bitcast(lax.shift_left(p, 16), F32).astype(BF16)
    hi = pltpu.bitcast(jnp.bitwise_and(p, -65536), F32).astype(BF16)
    return lo, hi


def _route_pos_kernel(off_ref, eid_ref, rank_ref, pos_ref):
    eid = eid_ref[...]
    base = jnp.zeros(eid.shape, jnp.int32)
    for e in range(N_EXPERTS):
        base = jnp.where(eid == e, off_ref[e], base)
    pos_ref[...] = base + rank_ref[...]


def _route_pos(offsets, eid, rank):
    return pl.pallas_call(
        _route_pos_kernel,
        grid=(1,),
        in_specs=[pl.BlockSpec(memory_space=pltpu.SMEM), _const_spec(eid.shape), _const_spec(rank.shape)],
        out_specs=_const_spec(eid.shape),
        out_shape=jax.ShapeDtypeStruct(eid.shape, jnp.int32),
        compiler_params=_params('arbitrary'),
        name='route_pos',
    )(offsets, eid, rank)


def _sc_mesh():
    return plsc.VectorSubcoreMesh(core_axis_name='c', subcore_axis_name='s', num_cores=V7X_SC_CORES,
                                  num_subcores=V7X_SC_SUBCORES)


def _sc_dispatch(hp, pos, n_rows):
    t, w = hp.shape
    k = pos.shape[0]
    workers = V7X_SC_CORES * V7X_SC_SUBCORES
    per_worker = t // workers
    pos_flat = pos.reshape(k * t)

    @functools.partial(pl.kernel, mesh=_sc_mesh(), out_type=jax.ShapeDtypeStruct((n_rows, w), jnp.int32),
                       scratch_types=[pltpu.VMEM((SC_ROWS,), jnp.int32), pltpu.VMEM((SC_ROWS, w), jnp.int32),
                                      pltpu.SemaphoreType.DMA])
    def scatter(hp_hbm, pos_hbm, out_hbm, idx_v, rows_v, sem):
        wid = lax.axis_index('s') * V7X_SC_CORES + lax.axis_index('c')

        @pl.loop(0, per_worker // SC_ROWS)
        def _(i):
            t0 = pl.multiple_of(wid * per_worker + i * SC_ROWS, SC_ROWS)
            pltpu.sync_copy(hp_hbm.at[pl.ds(t0, SC_ROWS)], rows_v)
            for j in range(k):
                pltpu.sync_copy(pos_hbm.at[pl.ds(pl.multiple_of(j * t + t0, SC_ROWS), SC_ROWS)], idx_v)
                pltpu.async_copy(rows_v, out_hbm.at[idx_v], sem).wait()

    return scatter(hp, pos_flat)


def _sc_collect(yp, pos):
    _, w = yp.shape
    k, t = pos.shape
    workers = V7X_SC_CORES * V7X_SC_SUBCORES
    per_worker = k * t // workers
    pos_flat = pos.reshape(k * t)

    @functools.partial(pl.kernel, mesh=_sc_mesh(), out_type=jax.ShapeDtypeStruct((k * t, w), jnp.int32),
                       scratch_types=[pltpu.VMEM((SC_ROWS,), jnp.int32), pltpu.VMEM((SC_ROWS, w), jnp.int32),
                                      pltpu.SemaphoreType.DMA])
    def gather(yp_hbm, pos_hbm, out_hbm, idx_v, rows_v, sem):
        wid = lax.axis_index('s') * V7X_SC_CORES + lax.axis_index('c')

        @pl.loop(0, per_worker // SC_ROWS)
        def _(i):
            r0 = pl.multiple_of(wid * per_worker + i * SC_ROWS, SC_ROWS)
            pltpu.sync_copy(pos_hbm.at[pl.ds(r0, SC_ROWS)], idx_v)
            pltpu.async_copy(yp_hbm.at[idx_v], rows_v, sem).wait()
            pltpu.sync_copy(rows_v, out_hbm.at[pl.ds(r0, SC_ROWS)])

    return gather(yp, pos_flat)


def _expert_kernel(te_ref, nu_ref, x_ref, wg_ref, wu_ref, wd_ref, o_ref, wg_s, wu_s, wd_s):
    i = pl.program_id(0)

    @pl.when(i < nu_ref[0])
    def _():
        @pl.when((i == 0) | (te_ref[i] != te_ref[jnp.maximum(i - 1, 0)]))
        def _():
            wg_s[...] = wg_ref[0].astype(BF16)
            wu_s[...] = wu_ref[0].astype(BF16)
            wd_s[...] = wd_ref[0].astype(BF16)

        lo, hi = _unpack_halves(x_ref[...])
        half = lo.shape[1]
        a = (jnp.dot(lo, wg_s[:half, :], preferred_element_type=F32)
             + jnp.dot(hi, wg_s[half:, :], preferred_element_type=F32))
        u = (jnp.dot(lo, wu_s[:half, :], preferred_element_type=F32)
             + jnp.dot(hi, wu_s[half:, :], preferred_element_type=F32))
        y = jnp.dot((_silu(a) * u).astype(BF16), wd_s[...], preferred_element_type=F32)
        o_ref[...] = _pack_halves(y)


def _experts(xp, tile_expert, n_used, layer, w_gate, w_up, w_down):
    n_rows, half = xp.shape
    _, _, d, ff = w_gate.shape
    r = MOE_ROW_TILE
    row = pl.BlockSpec((r, half), lambda i, te, nu: (i, 0))
    grid_spec = pltpu.PrefetchScalarGridSpec(
        num_scalar_prefetch=2,
        grid=(n_rows // r,),
        in_specs=[row,
                  pl.BlockSpec((None, 1, d, ff), lambda i, te, nu: (layer, te[i], 0, 0)),
                  pl.BlockSpec((None, 1, d, ff), lambda i, te, nu: (layer, te[i], 0, 0)),
                  pl.BlockSpec((None, 1, ff, d), lambda i, te, nu: (layer, te[i], 0, 0))],
        out_specs=row,
        scratch_shapes=[pltpu.VMEM((d, ff), BF16), pltpu.VMEM((d, ff), BF16), pltpu.VMEM((ff, d), BF16)],
    )
    return pl.pallas_call(
        _expert_kernel,
        grid_spec=grid_spec,
        out_shape=jax.ShapeDtypeStruct((n_rows, half), jnp.int32),
        compiler_params=_params('arbitrary'),
        name='experts',
    )(tile_expert, n_used, xp, w_gate, w_up, w_down)


def _combine_kernel(yg_ref, w_ref, hp_ref, x_ref, g2_ref, sg_ref, su_ref, sd_ref, *rest, final):
    if final:
        fg_ref, o_ref = rest
    else:
        (o_ref,) = rest
    half = hp_ref.shape[2]
    lo, hi = _unpack_halves(hp_ref[0])
    sg, su = sg_ref[...], su_ref[...]
    a = jnp.dot(lo, sg[:half], preferred_element_type=F32) + jnp.dot(hi, sg[half:], preferred_element_type=F32)
    u = jnp.dot(lo, su[:half], preferred_element_type=F32) + jnp.dot(hi, su[half:], preferred_element_type=F32)
    acc = jnp.dot((_silu(a) * u).astype(BF16), sd_ref[...], preferred_element_type=F32)
    acc_lo, acc_hi = acc[:, :half], acc[:, half:]
    wts = w_ref[0]
    for k in range(yg_ref.shape[0]):
        ylo, yhi = _unpack_halves(yg_ref[k, 0])
        wk = wts[:, k:k + 1]
        acc_lo = acc_lo + wk * ylo.astype(F32)
        acc_hi = acc_hi + wk * yhi.astype(F32)
    y = x_ref[0] + g2_ref[0] * jnp.concatenate([acc_lo, acc_hi], axis=1)
    if final:
        y = _rms(y) * fg_ref[...]
    o_ref[0] = y


def _combine(yg, wts, hp, x, g2, s_gate, s_up, s_down, final_g=None, tm=256):
    b, s, d = x.shape
    k = yg.shape[0]
    half = d // 2
    row = pl.BlockSpec((1, tm, d), lambda i, j: (i, j, 0))
    prow = pl.BlockSpec((1, tm, half), lambda i, j: (i, j, 0))
    ins = [yg, wts, hp, x, g2, s_gate, s_up, s_down]
    in_specs = [pl.BlockSpec((k, 1, tm, half), lambda i, j: (0, i, j, 0)), pl.BlockSpec((1, tm, k), lambda i, j: (i, j, 0)),
                prow, row, pl.BlockSpec((1, 1, d), lambda i, j: (i, 0, 0)),
                _const_spec(s_gate.shape), _const_spec(s_up.shape), _const_spec(s_down.shape)]
    if final_g is not None:
        ins.append(final_g.reshape(1, d))
        in_specs.append(_const_spec((1, d)))
    return pl.pallas_call(
        functools.partial(_combine_kernel, final=final_g is not None),
        grid=(b, s // tm),
        in_specs=in_specs,
        out_specs=row,
        out_shape=jax.ShapeDtypeStruct(x.shape, F32),
        compiler_params=_params('parallel', 'parallel'),
        name='moe_combine',
    )(*ins)


def _routed_moe(x, g, scale, shift, g2, w_router, e_bias, layer, w_gate, w_up, w_down, s_gate, s_up, s_down, final_g=None):
    b, s, d = x.shape
    t = b * s
    hp, eid, rank, wts, counts = _router(x, g, scale, shift, w_router, e_bias, compact=True)
    counts = counts[:, 0].astype(jnp.int32)
    r = MOE_ROW_TILE
    padded = (counts + (r - 1)) // r * r
    ends = jnp.cumsum(padded)
    offsets = ends - padded
    n_rows = t * TOP_K + N_EXPERTS * r
    tile_start = jnp.arange(n_rows // r, dtype=jnp.int32) * r
    tile_expert = jnp.minimum(jnp.sum((tile_start[:, None] >= ends[None, :]).astype(jnp.int32), axis=1), N_EXPERTS - 1)
    n_used = (ends[-1] // r).reshape(1).astype(jnp.int32)
    pos = _route_pos(offsets.astype(jnp.int32), eid, rank)
    xp = _sc_dispatch(hp.reshape(t, d // 2), pos, n_rows)
    yp = _experts(xp, tile_expert.astype(jnp.int32), n_used, layer, w_gate, w_up, w_down)
    yg = _sc_collect(yp, pos).reshape(TOP_K, b, s, d // 2)
    return _combine(yg, wts.T.reshape(b, s, TOP_K), hp, x, g2, s_gate, s_up, s_down, final_g)


def _mixers(p, pc, ctx_out, prm, l, lam_init, rope_tabs, lb_terms):
    s = p['hy_v'].shape[1]
    sc = pc['hy_v'].shape[1]

    hy_args = (prm['hy_w1'][l], prm['hy_b1'][l], prm['hy_w2'][l], prm['hy_b2'][l], prm['hy_w3'][l], prm['hy_b3'][l],
               prm['hy_sin_freq'][l], prm['hy_decay'][l])
    y_hy = _hyena([p['hy_v'], p['hy_x1'], p['hy_x2']], prm['hy_conv_w'][l], prm['hy_conv_b'][l],
                  _hy_filters(s, *hy_args), prm['hy_bias'][l], inner=128)
    yc_hy = None
    if ctx_out:
        yc_hy = _hyena([pc['hy_v'], pc['hy_x1'], pc['hy_x2']], prm['hy_conv_w'][l], prm['hy_conv_b'][l],
                       _hy_filters(sc, *hy_args), prm['hy_bias'][l], inner=32)

    lp = prm['da_lambda'][l].astype(F32)
    lam = jnp.exp(jnp.sum(lp[0] * lp[1])) - jnp.exp(jnp.sum(lp[2] * lp[3])) + lam_init
    da_kw = dict(heads=DA_HEADS, ncomp=2, scale=DA_HEAD_DIM ** -0.5, lam=lam, subln_g=prm['da_subln_g'][l],
                 post_scale=1.0 - lam_init)
    da_ctx = ([pc['da_k']], pc['da_v'])
    y_da = _attention([p['da_q']], [da_ctx, ([p['da_k']], p['da_v'])], **da_kw)
    yc_da = _attention([pc['da_q']], [da_ctx], **da_kw) if ctx_out else None

    wq = prm['mla_w_q_up'][l].reshape(MLA_Q_RANK, MLA_HEADS, MLA_NOPE_DIM + MLA_ROPE_DIM)
    wq_n = wq[:, :, :MLA_NOPE_DIM].reshape(MLA_Q_RANK, -1).astype(BF16)
    wq_r = wq[:, :, MLA_NOPE_DIM:].reshape(MLA_Q_RANK, -1).astype(BF16)
    wkv = prm['mla_w_kv_up'][l].reshape(MLA_KV_RANK, MLA_HEADS, MLA_NOPE_DIM + MLA_V_DIM)
    wkv_n = wkv[:, :, :MLA_NOPE_DIM].reshape(MLA_KV_RANK, -1).astype(BF16)
    wkv_v = wkv[:, :, MLA_NOPE_DIM:].reshape(MLA_KV_RANK, -1).astype(BF16)

    def queries(qd, tabs):
        return _norm_proj(qd, prm['mla_q_norm_g'][l], [(wq_n, F32, False, MLA_HEADS), (wq_r, F32, True, MLA_HEADS)],
                          rope_tabs=tabs)

    def keys_values(kvd):
        return _norm_proj(kvd, prm['mla_kv_norm_g'][l], [(wkv_n, BF16, False, MLA_HEADS), (wkv_v, BF16, False, MLA_HEADS)])

    kn_l, v_l = keys_values(p['mla_kv'])
    kn_c, v_c = keys_values(pc['mla_kv'])
    mla_kw = dict(heads=MLA_HEADS, ncomp=1, scale=(MLA_NOPE_DIM + MLA_ROPE_DIM) ** -0.5)
    mla_ctx = ([kn_c, pc['mla_kr']], v_c)
    y_mla = _attention(queries(p['mla_q'], rope_tabs), [mla_ctx, ([kn_l, p['mla_kr']], v_l)], **mla_kw)
    yc_mla = _attention(queries(pc['mla_q'], None), [mla_ctx], **mla_kw) if ctx_out else None

    o, oc = _hgrn(p['hg_q'], p['hg_ff'], p['hg_fb'], p['hg_i'], pc['hg_q'], pc['hg_ff'], pc['hg_fb'], pc['hg_i'],
                  lb_terms, prm['hg_norm_g'][l], variant=(l == 0), unroll=(2, 4)[l])
    return (y_hy, y_da, y_mla, o), (yc_hy, yc_da, yc_mla, oc)


def kernel(x, c, ctx, c_ctx, w_ada, b_ada, norm1_g, norm2_g, w_in, w_out, hy_conv_w, hy_conv_b, hy_w1, hy_b1, hy_w2, hy_b2, hy_w3, hy_b3, hy_sin_freq, hy_decay, hy_bias, da_lambda, da_subln_g, mla_q_norm_g, mla_w_q_up, mla_kv_norm_g, mla_w_kv_up, hg_lower_bounds, hg_norm_g, moe_w_router, moe_bias, moe_w_gate, moe_w_up, moe_w_down, moe_sh_gate, moe_sh_up, moe_sh_down, final_norm_g):
    prm = dict(hy_conv_w=hy_conv_w, hy_conv_b=hy_conv_b, hy_w1=hy_w1, hy_b1=hy_b1, hy_w2=hy_w2, hy_b2=hy_b2,
               hy_w3=hy_w3, hy_b3=hy_b3, hy_sin_freq=hy_sin_freq, hy_decay=hy_decay, hy_bias=hy_bias,
               da_lambda=da_lambda, da_subln_g=da_subln_g, mla_q_norm_g=mla_q_norm_g, mla_w_q_up=mla_w_q_up,
               mla_kv_norm_g=mla_kv_norm_g, mla_w_kv_up=mla_w_kv_up, hg_norm_g=hg_norm_g)
    b, n_lat, d = x.shape
    depth = w_in.shape[0]
    rows = n_lat // GRID_W
    row_pos = jnp.repeat(jnp.arange(rows, dtype=jnp.int32), GRID_W)
    col_pos = jnp.tile(jnp.arange(GRID_W, dtype=jnp.int32), rows)
    rope_tabs = _rope_tables(row_pos, col_pos, 2 * DA_HEADS * DA_HEAD_DIM)
    lbs = jnp.cumsum(jax.nn.softmax(hg_lower_bounds.astype(F32), axis=1), axis=1)
    lbs = lbs - lbs[:, :1]
    cond = jnp.concatenate([c, c_ctx[None], jnp.zeros((8 - b - 1, d), F32)], axis=0)

    for l in range(depth):
        ctx_out = l < depth - 1
        mods = _ada(cond, w_ada[l], b_ada[l])
        sh1, sc1, g1, sh2, sc2, g2 = [m[:, None, :] for m in jnp.split(mods[:b], 6, axis=-1)]
        mc = [jnp.broadcast_to(m[:, None, :], (b, 1, d)) for m in jnp.split(mods[b:b + 1], 6, axis=-1)]

        off = 0
        outs = []
        for _, wdt, dt, rope, split, rep in _SEGMENTS:
            w = w_in[l][:, off:off + wdt].astype(BF16)
            outs.append((jnp.tile(w, (1, rep)) if rep > 1 else w, dt, rope, split))
            off += wdt
        names = [seg[0] for seg in _SEGMENTS]
        p = dict(zip(names, _norm_proj(x, norm1_g[l], outs, sc1, sh1, rope_tabs=rope_tabs)))
        pc = dict(zip(names, _norm_proj(ctx, norm1_g[l], outs, mc[1], mc[0])))

        lb = lbs[:, l]
        lb_terms = jnp.stack([jnp.log(lb), jnp.log1p(-lb), 1.0 - lb], axis=1)
        lam_init = 0.8 - 0.6 * math.exp(-0.3 * l)
        lat_parts, ctx_parts = _mixers(p, pc, ctx_out, prm, l, lam_init, rope_tabs, lb_terms)

        w_out_b = w_out[l].astype(BF16)
        moe_w = (l, moe_w_gate, moe_w_up, moe_w_down,
                 moe_sh_gate[l].astype(BF16), moe_sh_up[l].astype(BF16), moe_sh_down[l].astype(BF16))

        if ctx_out:
            ctx = _out_proj(*ctx_parts, pc['hg_g'], ctx, mc[2], w_out_b)
            flat = ctx.reshape(1, -1, d)
            h2c, gate_c = _router(flat, norm2_g[l], mc[4][:1], mc[3][:1], moe_w_router[l], moe_bias[l])
            ctx = _moe(h2c, flat, gate_c.transpose(0, 2, 1), mc[5][:1], *moe_w).reshape(ctx.shape)

        x = _out_proj(*lat_parts, p['hg_g'], x, g1, w_out_b)
        x = _routed_moe(x, norm2_g[l], sc2, sh2, g2, moe_w_router[l], moe_bias[l], *moe_w,
                        final_g=None if ctx_out else final_norm_g)

    return x
```

```python
import functools
import math

import numpy as np
import jax
import jax.numpy as jnp
from jax import lax
from jax.experimental import pallas as pl
from jax.experimental.pallas import tpu as pltpu
from jax.experimental.pallas import tpu_sc as plsc

F32 = jnp.float32
BF16 = jnp.bfloat16
HIGHEST = lax.Precision.HIGHEST

D_MODEL = 1024
GRID_W = 64
HY_WIDTH = 256
HY_ORDER = 2
HY_BANDS = 16
DA_HEADS = 4
DA_HEAD_DIM = 32
MLA_HEADS = 4
MLA_Q_RANK = 192
MLA_KV_RANK = 128
MLA_NOPE_DIM = 64
MLA_ROPE_DIM = 32
MLA_V_DIM = 64
HG_HEADS = 4
HG_KEY_DIM = 64
HG_VAL_DIM = 64
HG_CHUNK = 64
HG_SUB = 8
N_EXPERTS = 64
N_EXPERT_GROUPS = 8
TOPK_GROUPS = 4
TOP_K = 8
EXPERT_FF = 256
ROUTED_SCALE = 2.5
ROPE_BASE = 10000.0
NORM_EPS = 1e-6

V7X_VMEM_LIMIT_BYTES = 56 * 1024 * 1024
LANES = 128

_SEGMENTS = (
    ('hy_v', HY_WIDTH, F32, False, 0, 1), ('hy_x1', HY_WIDTH, F32, False, 0, 1), ('hy_x2', HY_WIDTH, F32, False, 0, 1),
    ('da_q', 2 * DA_HEADS * DA_HEAD_DIM, F32, True, 2 * DA_HEADS, 1),
    ('da_k', 2 * DA_HEADS * DA_HEAD_DIM, BF16, True, 2 * DA_HEADS, 1),
    ('da_v', 2 * DA_HEADS * DA_HEAD_DIM, BF16, False, DA_HEADS, 1),
    ('mla_q', MLA_Q_RANK, F32, False, 0, 1), ('mla_kv', MLA_KV_RANK, F32, False, 0, 1),
    ('mla_kr', MLA_ROPE_DIM, BF16, True, MLA_HEADS, MLA_HEADS),
    ('hg_q', HG_HEADS * HG_KEY_DIM, F32, False, 0, 1), ('hg_ff', HG_HEADS * HG_KEY_DIM, F32, False, 0, 1),
    ('hg_fb', HG_HEADS * HG_KEY_DIM, F32, False, 0, 1), ('hg_i', HG_HEADS * HG_VAL_DIM, F32, False, 0, 1),
    ('hg_g', HG_HEADS * HG_VAL_DIM, F32, False, 0, 1),
)


def _params(*semantics):
    return pltpu.CompilerParams(dimension_semantics=semantics, vmem_limit_bytes=V7X_VMEM_LIMIT_BYTES)


def _const_spec(shape):
    nd = len(shape)
    return pl.BlockSpec(shape, lambda *_: (0,) * nd)


def _rms(x, eps=NORM_EPS):
    return x * lax.rsqrt(jnp.mean(x * x, axis=-1, keepdims=True) + eps)


def _silu(x):
    return x * jax.nn.sigmoid(x)


def _dot_nt(a, b, **kw):
    return lax.dot_general(a, b, (((1,), (1,)), ((), ())), preferred_element_type=F32, **kw)


def _ada_kernel(c_ref, w_ref, b_ref, o_ref):
    s = _silu(c_ref[...])
    o_ref[...] = jnp.dot(s, w_ref[...], precision=HIGHEST, preferred_element_type=F32) + b_ref[...]


def _ada(cond, w, b):
    r, d = cond.shape
    n = w.shape[1]
    tn = 1536
    return pl.pallas_call(
        _ada_kernel,
        grid=(n // tn,),
        in_specs=[_const_spec((r, d)), pl.BlockSpec((d, tn), lambda j: (0, j)), pl.BlockSpec((1, tn), lambda j: (0, j))],
        out_specs=pl.BlockSpec((r, tn), lambda j: (0, j)),
        out_shape=jax.ShapeDtypeStruct((r, n), F32),
        compiler_params=_params('arbitrary'),
        name='ada',
    )(cond, w, b.reshape(1, n))


ROPE_UNIT = 32


def _rope_tables(row, col, width):
    n = ROPE_UNIT // 4
    inv = ROPE_BASE ** (-jnp.arange(n, dtype=F32) / n)
    units = width // ROPE_UNIT
    parts_c, parts_a, parts_b = [], [], []
    zero = jnp.zeros((row.shape[0], n), F32)
    for pos in (row, col):
        ang = pos.astype(F32)[:, None] * inv
        cos, sin = jnp.cos(ang), jnp.sin(ang)
        parts_c += [cos, cos]
        parts_a += [zero, sin]
        parts_b += [-sin, zero]
    tile = lambda ps: jnp.tile(jnp.concatenate(ps, axis=1), (1, units))
    return tile(parts_c), tile(parts_a), tile(parts_b)


def _norm_proj_kernel(*refs, n_w, modulate, ropes, splits):
    x_ref, g_ref = refs[0], refs[1]
    pos = 2
    if modulate:
        sc_ref, sh_ref = refs[2], refs[3]
        pos = 4
    if any(ropes):
        rc_ref, ra_ref, rb_ref = refs[pos:pos + 3]
        pos += 3
    w_refs = refs[pos:pos + n_w]
    o_refs = refs[pos + n_w:]
    y = _rms(x_ref[0]) * g_ref[...]
    if modulate:
        y = y * (1.0 + sc_ref[0]) + sh_ref[0]
    yb = y.astype(BF16)
    for w_ref, o_ref, rope, split in zip(w_refs, o_refs, ropes, splits):
        o = jnp.dot(yb, w_ref[...], preferred_element_type=F32)
        if rope:
            wd = o.shape[1]
            shift = ROPE_UNIT // 4
            o = (o * rc_ref[:, :wd] + pltpu.roll(o, shift, axis=1) * ra_ref[:, :wd]
                 + pltpu.roll(o, wd - shift, axis=1) * rb_ref[:, :wd])
        if split:
            unit = o.shape[1] // split
            for u in range(split):
                o_ref[0, u] = o[:, u * unit:(u + 1) * unit].astype(o_ref.dtype)
        else:
            o_ref[0] = o.astype(o_ref.dtype)


def _norm_proj(x, g, outs, scale=None, shift=None, rope_tabs=None, tm=512):
    b, s, k = x.shape
    tm = min(tm, s)
    modulate = scale is not None
    ropes = tuple(bool(o[2]) and rope_tabs is not None for o in outs)
    splits = tuple(o[3] for o in outs)
    ins = [x, g.reshape(1, k)]
    in_specs = [pl.BlockSpec((1, tm, k), lambda i, j: (i, j, 0)), _const_spec((1, k))]
    if modulate:
        ins += [scale, shift]
        in_specs += [pl.BlockSpec((1, 1, k), lambda i, j: (i, 0, 0))] * 2
    if any(ropes):
        ins += list(rope_tabs)
        in_specs += [pl.BlockSpec((tm, rope_tabs[0].shape[1]), lambda i, j: (j, 0))] * 3
    out_specs, out_shape = [], []
    for w, dt, _, split in outs:
        ins.append(w)
        in_specs.append(_const_spec(w.shape))
        n = w.shape[1]
        if split:
            out_specs.append(pl.BlockSpec((1, split, tm, n // split), lambda i, j: (i, 0, j, 0)))
            out_shape.append(jax.ShapeDtypeStruct((b, split, s, n // split), dt))
        else:
            out_specs.append(pl.BlockSpec((1, tm, n), lambda i, j: (i, j, 0)))
            out_shape.append(jax.ShapeDtypeStruct((b, s, n), dt))
    return pl.pallas_call(
        functools.partial(_norm_proj_kernel, n_w=len(outs), modulate=modulate, ropes=ropes, splits=splits),
        grid=(b, s // tm),
        in_specs=in_specs,
        out_specs=out_specs,
        out_shape=out_shape,
        compiler_params=_params('parallel', 'parallel'),
        name='norm_proj',
    )(*ins)


def _hy_filter_kernel(w1t_ref, w1s_ref, w1c_ref, b1_ref, w2_ref, b2_ref, w3_ref, b3_ref, fr_ref, dec_ref, o_ref, *, n):
    t = lax.broadcasted_iota(jnp.int32, (n, 1), 0).astype(F32) / n
    bands = lax.broadcasted_iota(jnp.int32, (1, HY_BANDS), 1).astype(F32) + 1.0
    ang = (2.0 * jnp.pi) * t * bands
    pre = (t * w1t_ref[...]
           + jnp.dot(jnp.sin(ang), w1s_ref[...], precision=HIGHEST, preferred_element_type=F32)
           + jnp.dot(jnp.cos(ang), w1c_ref[...], precision=HIGHEST, preferred_element_type=F32)
           + b1_ref[...])
    hid = jnp.sin(fr_ref[0:1, :] * pre)
    hid = jnp.sin(fr_ref[1:2, :] * (jnp.dot(hid, w2_ref[...], precision=HIGHEST, preferred_element_type=F32) + b2_ref[...]))
    filt = jnp.dot(hid, w3_ref[...], precision=HIGHEST, preferred_element_type=F32) + b3_ref[...]
    filt = filt * jnp.exp(-t * jnp.abs(dec_ref[...]))
    col = jnp.sum(jnp.abs(filt), axis=0, keepdims=True) - jnp.abs(filt[0:1, :])
    w = HY_WIDTH
    for o in range(HY_ORDER):
        lo = o * 2 * w
        f0 = filt[0:1, lo:lo + w] + filt[0:1, lo + w:lo + 2 * w]
        inv = 1.0 / (col[:, lo:lo + w] + col[:, lo + w:lo + 2 * w] + jnp.abs(f0))
        o_ref[:, lo:lo + w] = filt[:, lo:lo + w] * inv
        o_ref[:, lo + w:lo + 2 * w] = filt[:, lo + w:lo + 2 * w] * inv


def _hy_filters(n, w1, b1, w2, b2, w3, b3, freq, decay):
    cols = w3.shape[1]
    ins = [w1[0:1], w1[1:1 + HY_BANDS], w1[1 + HY_BANDS:], b1.reshape(1, -1), w2, b2.reshape(1, -1), w3,
           b3.reshape(1, -1), freq, decay.reshape(1, -1)]
    out = pl.pallas_call(
        functools.partial(_hy_filter_kernel, n=n),
        grid=(1,),
        in_specs=[_const_spec(a.shape) for a in ins],
        out_specs=_const_spec((n, cols)),
        out_shape=jax.ShapeDtypeStruct((n, cols), F32),
        compiler_params=_params('arbitrary'),
        name='hy_filter',
    )(*ins)
    return out.reshape(n, HY_ORDER, 2, HY_WIDTH)


def _two_sided(filt_n):
    n = filt_n.shape[0]
    hf, hb = filt_n[:, :, 0], filt_n[:, :, 1]
    h = jnp.concatenate([hf[:1] + hb[:1], hf[1:], jnp.zeros((1,) + hf.shape[1:], F32), hb[:0:-1]], axis=0)
    return h.reshape(2 * n, HY_ORDER * HY_WIDTH)


def _short_conv_kernel(*refs, s):
    x_refs, w_refs, b_refs, o_refs = refs[0:3], refs[3:6], refs[6:9], refs[9:12]
    row = lax.broadcasted_iota(jnp.int32, (s, 1), 0)
    for x_ref, w_ref, b_ref, o_ref in zip(x_refs, w_refs, b_refs, o_refs):
        x = x_ref[0]
        prev = jnp.where(row == 0, 0.0, pltpu.roll(x, 1, axis=0))
        nxt = jnp.where(row == s - 1, 0.0, pltpu.roll(x, s - 1, axis=0))
        o_ref[0] = prev * w_ref[0:1, :] + x * w_ref[1:2, :] + nxt * w_ref[2:3, :] + b_ref[...]


def _short_conv(parts, conv_w, conv_b):
    b, s, c = parts[0].shape
    tc = LANES
    ws = [conv_w[:, i * c:(i + 1) * c] for i in range(3)]
    bs = [conv_b[i * c:(i + 1) * c].reshape(1, c) for i in range(3)]
    xspec = pl.BlockSpec((1, s, tc), lambda i, j: (i, 0, j))
    return pl.pallas_call(
        functools.partial(_short_conv_kernel, s=s),
        grid=(b, c // tc),
        in_specs=[xspec] * 3 + [pl.BlockSpec((3, tc), lambda i, j: (0, j))] * 3 + [pl.BlockSpec((1, tc), lambda i, j: (0, j))] * 3,
        out_specs=[xspec] * 3,
        out_shape=[jax.ShapeDtypeStruct((b, s, c), F32)] * 3,
        compiler_params=_params('parallel', 'parallel'),
        name='short_conv',
    )(*parts, *ws, *bs)


def _dft_cos_sin(rows, cols, period):
    ang = 2.0 * np.pi * ((np.arange(rows)[:, None] * np.arange(cols)[None, :]) % period) / period
    return np.cos(ang), np.sin(ang)


def _fft_tables(n, inner):
    big = 2 * n
    n1 = big // inner
    c1, s1 = _dft_cos_sin(n1, n1, n1)
    h = n1 // 2
    outer_data = np.block([[c1[:, :h], s1[:, :h]], [-s1[:, :h], c1[:, :h]]])
    outer_real = np.concatenate([c1, -s1], axis=0)
    outer_inv = np.block([[c1[:h, :], -s1[:h, :]], [s1[:h, :], c1[:h, :]]]) / big
    c2, s2 = _dft_cos_sin(inner, inner, inner)
    inner_fwd = np.block([[c2, s2], [-s2, c2]])
    inner_inv = np.block([[c2, -s2], [s2, c2]])
    ct, st = _dft_cos_sin(n1, inner, big)
    f = lambda a: jnp.asarray(a, F32)
    return dict(n1=n1, inner=inner, outer_data=f(outer_data), outer_real=f(outer_real), outer_inv=f(outer_inv),
                inner_fwd=_hi_lo_cols(inner_fwd), inner_inv=_hi_lo_cols(inner_inv),
                tw_cos=f(ct).reshape(n1, inner, 1), tw_sin=f(st).reshape(n1, inner, 1))


def _left_mm_kernel(m_ref, x_ref, o_ref):
    o_ref[0] = jnp.dot(m_ref[...], x_ref[0], precision=HIGHEST, preferred_element_type=F32)


def _left_mm(m, x, tl=4096):
    p, k, l = x.shape
    mm = m.shape[0]
    tl = min(tl, l)
    return pl.pallas_call(
        _left_mm_kernel,
        grid=(p, l // tl),
        in_specs=[_const_spec(m.shape), pl.BlockSpec((1, k, tl), lambda i, j: (i, 0, j))],
        out_specs=pl.BlockSpec((1, mm, tl), lambda i, j: (i, 0, j)),
        out_shape=jax.ShapeDtypeStruct((p, mm, l), F32),
        compiler_params=_params('parallel', 'parallel'),
        name='fft_outer',
    )(m, x)


def _hi_lo_cols(m):
    m = np.asarray(m, np.float32)
    hi = m.astype(BF16)
    lo = (m - hi.astype(np.float32)).astype(BF16)
    return jnp.asarray(np.concatenate([hi, hi, lo], axis=1))


def _hi_lo_rows(x):
    hi = x.astype(BF16)
    lo = (x - hi.astype(F32)).astype(BF16)
    return jnp.concatenate([hi, lo, hi], axis=0)


def _inner_kernel(a_ref, twc_ref, tws_ref, gf_ref, *rest, convolve, inner, kb):
    for s in range(kb):
        ar, ai = a_ref[0, 0, s], a_ref[0, 1, s]
        tc, ts = twc_ref[s], tws_ref[s]
        br = ar * tc + ai * ts
        bi = ai * tc - ar * ts
        x = jnp.dot(gf_ref[...], _hi_lo_rows(jnp.concatenate([br, bi], axis=0)), preferred_element_type=F32)
        if not convolve:
            o_ref = rest[0]
            o_ref[0, 0, s] = x[:inner]
            o_ref[0, 1, s] = x[inner:]
            continue
        h_ref, gi_ref, o_ref = rest
        xr, xi = x[:inner], x[inner:]
        hr, hi = h_ref[0, 0, s], h_ref[0, 1, s]
        yr = xr * hr - xi * hi
        yi = xr * hi + xi * hr
        z = jnp.dot(gi_ref[...], _hi_lo_rows(jnp.concatenate([yr, yi], axis=0)), preferred_element_type=F32)
        zr, zi = z[:inner], z[inner:]
        o_ref[0, 0, s] = zr * tc - zi * ts
        o_ref[0, 1, s] = zi * tc + zr * ts


def _fft_inner(a, tab, c, h=None, h_block=0):
    p = a.shape[0]
    n1, inner = tab['n1'], tab['inner']
    a5 = a.reshape(p, 2, n1, inner, c)
    tc = 2 * LANES
    kb = 4
    blk = pl.BlockSpec((1, 2, kb, inner, tc), lambda i, k, j: (i, 0, k, 0, j))
    tw_spec = pl.BlockSpec((kb, inner, 1), lambda i, k, j: (k, 0, 0))
    ins = [a5, tab['tw_cos'], tab['tw_sin'], tab['inner_fwd']]
    in_specs = [blk, tw_spec, tw_spec, _const_spec(tab['inner_fwd'].shape)]
    if h is not None:
        ch = h.shape[-1] // inner
        nb = c // tc
        ins += [h.reshape(1, 2, n1, inner, ch), tab['inner_inv']]
        in_specs += [pl.BlockSpec((1, 2, kb, inner, tc), lambda i, k, j: (0, 0, k, 0, h_block * nb + j)),
                     _const_spec(tab['inner_inv'].shape)]
    out = pl.pallas_call(
        functools.partial(_inner_kernel, convolve=h is not None, inner=inner, kb=kb),
        grid=(p, n1 // kb, c // tc),
        in_specs=in_specs,
        out_specs=blk,
        out_shape=jax.ShapeDtypeStruct(a5.shape, F32),
        compiler_params=_params('parallel', 'parallel', 'parallel'),
        name='fft_inner',
    )(*ins)
    return out.reshape(p, 2 * n1, inner * c)


def _gate_kernel(m_ref, z_ref, u_ref, x_ref, bias_ref, *rest, chain):
    y = jnp.dot(m_ref[...], z_ref[0], precision=HIGHEST, preferred_element_type=F32)
    nxt = x_ref[0] * (y + u_ref[0] * bias_ref[...])
    if chain:
        mf_ref, o_ref, a_ref = rest
        o_ref[0] = nxt
        a_ref[0] = jnp.dot(mf_ref[...], nxt, precision=HIGHEST, preferred_element_type=F32)
    else:
        rest[0][0] = nxt


def _fft_gate(tab, z, u, x, bias_l, chain, tl=4096):
    p, k2, l = z.shape
    n1 = tab['n1']
    tl = min(tl, l)
    row = pl.BlockSpec((1, n1, tl), lambda i, j: (i, 0, j))
    ins = [tab['outer_inv'], z, u, x, bias_l]
    in_specs = [_const_spec((n1, k2)), pl.BlockSpec((1, k2, tl), lambda i, j: (i, 0, j)), row, row,
                pl.BlockSpec((1, tl), lambda i, j: (0, j))]
    out_specs = [row]
    out_shape = [jax.ShapeDtypeStruct((p, n1, l), F32)]
    if chain:
        ins.append(tab['outer_data'])
        in_specs.append(_const_spec((k2, n1)))
        out_specs.append(pl.BlockSpec((1, k2, tl), lambda i, j: (i, 0, j)))
        out_shape.append(jax.ShapeDtypeStruct((p, k2, l), F32))
    return pl.pallas_call(
        functools.partial(_gate_kernel, chain=chain),
        grid=(p, l // tl),
        in_specs=in_specs,
        out_specs=out_specs,
        out_shape=out_shape,
        compiler_params=_params('parallel', 'parallel'),
        name='fft_gate',
    )(*ins)


def _hyena(parts, conv_w, conv_b, filt_n, bias, inner):
    b, s, c = parts[0].shape
    tab = _fft_tables(s, inner)
    n1 = tab['n1']
    lanes = inner * c
    h_taps = _two_sided(filt_n).reshape(1, n1, inner * HY_ORDER * c)
    h_spec = _fft_inner(_left_mm(tab['outer_real'], h_taps), tab, HY_ORDER * c)
    v, x1, x2 = [a.reshape(b // 2, n1, lanes) for a in _short_conv(parts, conv_w, conv_b)]
    bias_l = [jnp.tile(bias[o], inner).reshape(1, lanes) for o in range(HY_ORDER)]
    a = _left_mm(tab['outer_data'], v)
    z = _fft_inner(a, tab, c, h_spec, 0)
    z2, a = _fft_gate(tab, z, v, x1, bias_l[0], chain=True)
    z = _fft_inner(a, tab, c, h_spec, 1)
    (z3,) = _fft_gate(tab, z, z2, x2, bias_l[1], chain=False)
    return z3.reshape(b, s, c)


def _attn_kernel(*refs, n_q, n_pieces, ncomp, scale, post_scale):
    q_refs = refs[:n_q]
    pos = n_q
    pieces = []
    for _ in range(n_pieces):
        pieces.append((refs[pos:pos + n_q], refs[pos + n_q]))
        pos += n_q + 1
    if ncomp == 2:
        lam_ref, g_ref = refs[pos:pos + 2]
        pos += 2
    o_ref, kcat_ref, vcat_ref = refs[pos:pos + 3]
    dv = o_ref.shape[3]

    @pl.when(pl.program_id(2) == 0)
    def _():
        row = 0
        for k_refs, v_ref in pieces:
            n = v_ref.shape[2]
            for c in range(ncomp):
                parts = [k_ref[0, c if k_ref.shape[1] == ncomp else 0] for k_ref in k_refs]
                kcat_ref[c, row:row + n, :] = parts[0] if n_q == 1 else jnp.concatenate(parts, axis=1)
            vcat_ref[row:row + n, :dv] = v_ref[0, 0]
            vcat_ref[row:row + n, dv:] = jnp.ones((n, dv), BF16)
            row += n

    outs = []
    for c in range(ncomp):
        q = q_refs[0][0, c] if n_q == 1 else jnp.concatenate([q_ref[0, c] for q_ref in q_refs], axis=1)
        s = _dot_nt((q * (scale * math.log2(math.e))).astype(BF16), kcat_ref[c])
        m = jnp.max(s, axis=-1, keepdims=True)
        p = jnp.exp2((s - m).astype(BF16))
        ol = jnp.dot(p, vcat_ref[...], preferred_element_type=F32)
        outs.append(ol[:, :dv] / ol[:, dv:dv + 1])
    if ncomp == 2:
        o = outs[0] - lam_ref[0] * outs[1]
        o = _rms(o) * g_ref[...] * post_scale
    else:
        o = outs[0]
    o_ref[0, 0] = o


def _attention(q_parts, pieces, heads, ncomp, scale, tq=256, lam=None, subln_g=None, post_scale=1.0):
    b, _, sq, _ = q_parts[0].shape
    dv = pieces[0][1].shape[3]
    tq = min(tq, sq)
    ins = list(q_parts)
    in_specs = [pl.BlockSpec((1, ncomp, tq, q.shape[3]), lambda i, h, j: (i, h, j, 0)) for q in q_parts]
    for k_parts, v in pieces:
        for k in k_parts:
            ins.append(k)
            if k.shape[1] == 1:
                in_specs.append(pl.BlockSpec((1, 1) + k.shape[2:], lambda i, h, j: (i, 0, 0, 0)))
            else:
                in_specs.append(pl.BlockSpec((1, ncomp) + k.shape[2:], lambda i, h, j: (i, h, 0, 0)))
        ins.append(v)
        in_specs.append(pl.BlockSpec((1, 1) + v.shape[2:], lambda i, h, j: (i, h, 0, 0)))
    if ncomp == 2:
        ins += [lam.reshape(1), subln_g.reshape(1, dv)]
        in_specs += [pl.BlockSpec(memory_space=pltpu.SMEM), _const_spec((1, dv))]
    sk = sum(v.shape[2] for _, v in pieces)
    dqk = sum(q.shape[3] for q in q_parts)
    return pl.pallas_call(
        functools.partial(_attn_kernel, n_q=len(q_parts), n_pieces=len(pieces), ncomp=ncomp, scale=scale,
                          post_scale=post_scale),
        grid=(b, heads, sq // tq),
        in_specs=in_specs,
        out_specs=pl.BlockSpec((1, 1, tq, dv), lambda i, h, j: (i, h, j, 0)),
        out_shape=jax.ShapeDtypeStruct((b, heads, sq, dv), F32),
        scratch_shapes=[pltpu.VMEM((ncomp, sk, dqk), BF16), pltpu.VMEM((sk, 2 * dv), BF16)],
        compiler_params=_params('parallel', 'parallel', 'arbitrary'),
        name='attention',
    )(*ins)


def _forget_terms(f, log_lb, log_1m_lb, one_m_lb):
    log_sig = jnp.minimum(f, 0.0) - jnp.log1p(jnp.exp(-jnp.abs(f)))
    b = log_1m_lb + log_sig
    log_g = jnp.maximum(log_lb, b) + jnp.log1p(jnp.exp(-jnp.abs(log_lb - b)))
    return log_g, one_m_lb * jax.nn.sigmoid(-f)


def _hg_tables():
    ck, sub = HG_CHUNK, HG_SUB
    t = np.arange(ck)
    cum_mats, half_masks, group_masks, sels, keeps = [], [], [], [], []
    for rev in (False, True):
        mats = [(t[None, :] >= t[:, None]) if rev else (t[None, :] <= t[:, None])]
        halves = []
        hs = ck // 2
        while hs >= sub:
            pos = t % (2 * hs)
            b = t - pos + hs
            mats.append((t[None, :] >= b[:, None]) if rev else (t[None, :] < b[:, None]))
            q_half = (pos < hs) if rev else (pos >= hs)
            halves.append(np.stack([q_half, ~q_half]))
            if not rev:
                grp = (t[:, None] // (2 * hs)) == (t[None, :] // (2 * hs))
                group_masks.append(np.concatenate([grp, grp], axis=0))
            hs //= 2
        cum_mats.append(np.concatenate(mats, axis=0))
        half_masks.append(np.stack(halves))
        r, c = np.arange(ck)[:, None], np.arange(ck * sub)[None, :]
        same = (c // sub) == r
        tt, ss = (c // sub) % sub, c % sub
        causal = (ss >= tt) if rev else (ss <= tt)
        sels.append(same & causal)
        keeps.append(causal[0])
    lanes = 2 * HG_KEY_DIM
    ln = np.arange(lanes)
    bd = (ln[:, None] // HG_KEY_DIM) == (ln[None, :] // HG_KEY_DIM)
    hm = np.broadcast_to(np.stack(half_masks)[..., None], (2, len(half_masks[0]), 2, ck, lanes))
    return (jnp.asarray(np.stack(cum_mats), BF16), jnp.asarray(hm, F32), jnp.asarray(np.stack(group_masks), F32),
            jnp.asarray(np.stack(sels), BF16), jnp.asarray(bd, F32),
            jnp.asarray(np.broadcast_to(np.stack(keeps)[..., None], (2, ck * sub, lanes)), F32))


def _split3(x):
    a = x.astype(BF16)
    r = x - a.astype(F32)
    b = r.astype(BF16)
    return a, b, (r - b.astype(F32)).astype(BF16)


def _hg_chunk(q, k, v, lg, st, rev, cm, hm, gm, sel, bd, m0, m1, variant, keep=None):
    ck, sub = HG_CHUNK, HG_SUB
    if variant:
        lanes = lg.shape[1]
        c3 = jnp.dot(cm, jnp.concatenate(_split3(lg), axis=1), preferred_element_type=F32)
        call = c3[:, :lanes] + c3[:, lanes:2 * lanes] + c3[:, 2 * lanes:]
    else:
        call = sum(jnp.dot(cm, piece, preferred_element_type=F32) for piece in _split3(lg))
    cum = call[0:ck]
    tot = cum[0:1] if rev else cum[ck - 1:ck]
    o = _dot_nt((q * jnp.exp(cum)).astype(BF16), st.astype(BF16))
    kd = (k * jnp.exp(tot - cum)).astype(BF16)
    st_new = st * jnp.exp(tot) + bd * jnp.dot(v.T.astype(BF16), kd, preferred_element_type=F32)
    s2 = None
    for lv in range(gm.shape[0]):
        cb = call[(lv + 1) * ck:(lv + 2) * ck]
        qd = q * jnp.exp(jnp.minimum(cum - cb, 0.0)) * hm[lv, 0]
        kf = (k * jnp.exp(jnp.minimum(cb - cum, 0.0)) * hm[lv, 1]).astype(BF16)
        q2 = jnp.concatenate([qd * m0, qd * m1], axis=0).astype(BF16)
        term = _dot_nt(q2, kf) * gm[lv]
        s2 = term if s2 is None else s2 + term
    r = jnp.dot(s2.astype(BF16), v.astype(BF16), preferred_element_type=F32)
    o = o + m0 * r[:ck] + m1 * r[ck:]
    rows, vts = [], []
    for i in range(ck // sub):
        lo, hi = i * sub, (i + 1) * sub
        ki, ci = k[lo:hi], cum[lo:hi]
        for t in range(lo, hi):
            rows.append(q[t:t + 1] * ki * jnp.exp(jnp.minimum(cum[t:t + 1] - ci, 0.0)))
            vts.append(v[lo:hi])
    sc = jnp.dot(jnp.concatenate(rows, axis=0).astype(BF16), bd.astype(BF16), preferred_element_type=F32)
    prod = sc * jnp.concatenate(vts, axis=0)
    if variant:
        o = o + jnp.sum((prod * keep).reshape(ck, sub, prod.shape[1]), axis=1)
    else:
        o = o + jnp.dot(sel, prod.astype(BF16), preferred_element_type=F32)
    return o, st_new


def _hgrn_kernel(q_ref, ff_ref, fb_ref, i_ref, qc_ref, ffc_ref, fbc_ref, ic_ref, lb_ref, g_ref,
                 cm_ref, hm_ref, gm_ref, sel_ref, bd_ref, keep_ref, o_ref, oc_ref, or_ref, ocr_ref, st_ref,
                 *, n_lat, n_ctx, variant, unroll):
    ck = HG_CHUNK
    lanes = o_ref.shape[-1]
    lane = lax.broadcasted_iota(jnp.int32, (1, lanes), 1)
    m0 = (lane < HG_KEY_DIM).astype(F32)
    m1 = 1.0 - m0
    bd = bd_ref[...]
    gm = gm_ref[...]

    def one(q, f, v, rev):
        d = 1 if rev else 0
        lg, k = _forget_terms(f, lb_ref[d, 0:1, :], lb_ref[d, 1:2, :], lb_ref[d, 2:3, :])
        o, st = _hg_chunk(q, k, v, lg, st_ref[d], rev, cm_ref[d], hm_ref[d], gm, sel_ref[d], bd, m0, m1, variant,
                          keep_ref[d] if variant else None)
        st_ref[d] = st
        return o

    def sweep(qr, ffr, fbr, ir, out_f, out_r, n):
        nc = n // ck

        def body(step, carry):
            idf = pl.ds(pl.multiple_of(step * ck, ck), ck)
            idr = pl.ds(pl.multiple_of((nc - 1 - step) * ck, ck), ck)
            out_f[0, idf, :] = one(qr[0, idf, :], ffr[0, idf, :], ir[0, idf, :], False)
            out_r[idr, :] = one(qr[0, idr, :], fbr[0, idr, :], ir[0, idr, :], True)
            return carry

        lax.fori_loop(0, nc, body, 0, unroll=unroll)

    st_ref[...] = jnp.zeros(st_ref.shape, F32)
    sweep(qc_ref, ffc_ref, fbc_ref, ic_ref, oc_ref, ocr_ref, n_ctx)
    sweep(q_ref, ff_ref, fb_ref, i_ref, o_ref, or_ref, n_lat)

    mean_mat = bd * (1.0 / HG_VAL_DIM)

    def readout(out, out_r, n):
        tile = min(n, 512)

        def body(step, carry):
            idx = pl.ds(pl.multiple_of(step * tile, tile), tile)
            x = out[0, idx, :] + out_r[idx, :]
            ms = jnp.dot(x * x, mean_mat, precision=HIGHEST, preferred_element_type=F32)
            out[0, idx, :] = x * lax.rsqrt(ms + NORM_EPS) * g_ref[...]
            return carry

        lax.fori_loop(0, n // tile, body, 0)

    readout(oc_ref, ocr_ref, n_ctx)
    readout(o_ref, or_ref, n_lat)


def _hgrn(q, ff, fb, iv, qc, ffc, fbc, ic, lb_terms, norm_g, variant=False, unroll=2):
    b, n_lat, width = q.shape
    n_ctx = qc.shape[1]
    lanes = 2 * HG_KEY_DIM
    tables = _hg_tables()
    lat = pl.BlockSpec((1, n_lat, lanes), lambda i, j: (i, 0, j))
    ctx = pl.BlockSpec((1, n_ctx, lanes), lambda i, j: (i, 0, j))
    g2 = jnp.tile(norm_g, 2).reshape(1, lanes)
    return pl.pallas_call(
        functools.partial(_hgrn_kernel, n_lat=n_lat, n_ctx=n_ctx, variant=variant, unroll=unroll),
        grid=(b, width // lanes),
        in_specs=[lat] * 4 + [ctx] * 4 + [pl.BlockSpec((2, 3, lanes), lambda i, j: (0, 0, j)), _const_spec((1, lanes))]
                 + [_const_spec(t.shape) for t in tables],
        out_specs=[lat, ctx],
        out_shape=[jax.ShapeDtypeStruct(q.shape, F32), jax.ShapeDtypeStruct(qc.shape, F32)],
        scratch_shapes=[pltpu.VMEM((n_lat, lanes), F32), pltpu.VMEM((n_ctx, lanes), F32), pltpu.VMEM((2, lanes, lanes), F32)],
        compiler_params=_params('parallel', 'parallel'),
        name='hgrn2',
    )(q, ff, fb, iv, qc, ffc, fbc, ic, lb_terms, g2, *tables)


def _out_proj_kernel(hy_ref, da_ref, mla_ref, hg_ref, gate_ref, x_ref, g1_ref, w_ref, o_ref):
    c = hy_ref.shape[2]
    acc = jnp.dot(hy_ref[0].astype(BF16), w_ref[0:c, :], preferred_element_type=F32)
    for i, head_ref in ((1, da_ref), (2, mla_ref)):
        dv = head_ref.shape[3]
        for h in range(head_ref.shape[1]):
            lo = i * c + h * dv
            acc = acc + jnp.dot(head_ref[0, h].astype(BF16), w_ref[lo:lo + dv, :], preferred_element_type=F32)
    hg = hg_ref[0] * _silu(gate_ref[0])
    acc = acc + jnp.dot(hg.astype(BF16), w_ref[3 * c:4 * c, :], preferred_element_type=F32)
    o_ref[0] = x_ref[0] + g1_ref[0] * acc


def _out_proj(y_hy, y_da, y_mla, y_hg, gate, x, g1, w_out, tm=512):
    b, s, d = x.shape
    tm = min(tm, s)
    c = y_hy.shape[2]
    part = pl.BlockSpec((1, tm, c), lambda i, j: (i, j, 0))
    headed = lambda a: pl.BlockSpec((1, a.shape[1], tm, a.shape[3]), lambda i, j: (i, 0, j, 0))
    row = pl.BlockSpec((1, tm, d), lambda i, j: (i, j, 0))
    return pl.pallas_call(
        _out_proj_kernel,
        grid=(b, s // tm),
        in_specs=[part, headed(y_da), headed(y_mla), part, part, row, pl.BlockSpec((1, 1, d), lambda i, j: (i, 0, 0)),
                  _const_spec(w_out.shape)],
        out_specs=row,
        out_shape=jax.ShapeDtypeStruct(x.shape, F32),
        compiler_params=_params('parallel', 'parallel'),
        name='out_proj',
    )(y_hy, y_da, y_mla, y_hg, gate, x, g1, w_out)


def _router_kernel(x_ref, g_ref, sc_ref, sh_ref, wrt_ref, bias_ref, *rest, compact):
    h = _rms(x_ref[0]) * g_ref[...] * (1.0 + sc_ref[0]) + sh_ref[0]
    tm = h.shape[0]
    scores = jax.nn.sigmoid(_dot_nt(wrt_ref[...], h, precision=HIGHEST))
    choice = scores + bias_ref[...]
    per = N_EXPERTS // N_EXPERT_GROUPS
    neg = -jnp.inf
    iota_g = lax.broadcasted_iota(jnp.int32, (per, tm), 0)
    grp_rows = []
    for gi in range(N_EXPERT_GROUPS):
        blk = choice[gi * per:(gi + 1) * per]
        m1 = jnp.max(blk, axis=0, keepdims=True)
        first = jnp.min(jnp.where(blk == m1, iota_g, per), axis=0, keepdims=True)
        m2 = jnp.max(jnp.where(iota_g == first, neg, blk), axis=0, keepdims=True)
        grp_rows.append(m1 + m2)
    grp = jnp.concatenate(grp_rows, axis=0)
    iota_n = lax.broadcasted_iota(jnp.int32, (N_EXPERT_GROUPS, tm), 0)
    gsel = jnp.zeros((N_EXPERT_GROUPS, tm), F32)
    for _ in range(TOPK_GROUPS):
        m = jnp.max(grp, axis=0, keepdims=True)
        first = jnp.min(jnp.where(grp == m, iota_n, N_EXPERT_GROUPS), axis=0, keepdims=True)
        hit = iota_n == first
        gsel = jnp.where(hit, 1.0, gsel)
        grp = jnp.where(hit, neg, grp)
    emask = jnp.concatenate([jnp.broadcast_to(gsel[gi:gi + 1], (per, tm)) for gi in range(N_EXPERT_GROUPS)], axis=0)
    cand = jnp.where(emask > 0.0, choice, neg)
    iota_e = lax.broadcasted_iota(jnp.int32, (N_EXPERTS, tm), 0)
    sel = jnp.zeros((N_EXPERTS, tm), F32)
    chosen = []
    for _ in range(TOP_K):
        m = jnp.max(cand, axis=0, keepdims=True)
        first = jnp.min(jnp.where(cand == m, iota_e, N_EXPERTS), axis=0, keepdims=True)
        hit = iota_e == first
        sel = jnp.where(hit, 1.0, sel)
        cand = jnp.where(hit, neg, cand)
        chosen.append(first)
    w = scores * sel
    gate = w / jnp.sum(w, axis=0, keepdims=True) * ROUTED_SCALE
    if not compact:
        h_ref, gate_ref = rest
        h_ref[0] = h.astype(BF16)
        gate_ref[0] = gate
        return
    hp_ref, eid_ref, rank_ref, w_ref, cnt_out_ref, cnt_ref = rest
    hp_ref[0] = _pack_halves(h)

    @pl.when((pl.program_id(0) == 0) & (pl.program_id(1) == 0))
    def _():
        cnt_ref[...] = jnp.zeros(cnt_ref.shape, F32)

    src = lax.broadcasted_iota(jnp.int32, (tm, tm), 0)
    dst = lax.broadcasted_iota(jnp.int32, (tm, tm), 1)
    running = jnp.dot(sel.astype(BF16), (src <= dst).astype(BF16), preferred_element_type=F32)
    rank_dense = cnt_ref[:, 0:1] + running - 1.0
    e_rows, r_rows, w_rows = [], [], []
    for first in chosen:
        hit = iota_e == first
        e_rows.append(first)
        r_rows.append(jnp.sum(jnp.where(hit, rank_dense, 0.0), axis=0, keepdims=True))
        w_rows.append(jnp.sum(jnp.where(hit, gate, 0.0), axis=0, keepdims=True))
    eid_ref[...] = jnp.concatenate(e_rows, axis=0)
    rank_ref[...] = jnp.concatenate(r_rows, axis=0).astype(jnp.int32)
    w_ref[...] = jnp.concatenate(w_rows, axis=0)
    cnt_ref[...] = cnt_ref[...] + running[:, tm - 1:tm]
    cnt_out_ref[...] = cnt_ref[...]


def _router(x, g, scale, shift, w_router, e_bias, tm=512, compact=False):
    b, s, d = x.shape
    tm = min(tm, s)
    e = w_router.shape[1]
    row = pl.BlockSpec((1, tm, d), lambda i, j: (i, j, 0))
    mod = pl.BlockSpec((1, 1, d), lambda i, j: (i, 0, 0))
    if compact:
        nj = s // tm
        tok = pl.BlockSpec((TOP_K, tm), lambda i, j: (0, i * nj + j))
        out_specs = [pl.BlockSpec((1, tm, d // 2), lambda i, j: (i, j, 0)), tok, tok, tok, _const_spec((e, LANES))]
        out_shape = [jax.ShapeDtypeStruct((b, s, d // 2), jnp.int32), jax.ShapeDtypeStruct((TOP_K, b * s), jnp.int32),
                     jax.ShapeDtypeStruct((TOP_K, b * s), jnp.int32), jax.ShapeDtypeStruct((TOP_K, b * s), F32),
                     jax.ShapeDtypeStruct((e, LANES), F32)]
        scratch = [pltpu.VMEM((e, LANES), F32)]
        semantics = ('arbitrary', 'arbitrary')
    else:
        out_specs = [row, pl.BlockSpec((1, e, tm), lambda i, j: (i, 0, j))]
        out_shape = [jax.ShapeDtypeStruct((b, s, d), BF16), jax.ShapeDtypeStruct((b, e, s), F32)]
        scratch = []
        semantics = ('parallel', 'parallel')
    return pl.pallas_call(
        functools.partial(_router_kernel, compact=compact),
        grid=(b, s // tm),
        in_specs=[row, _const_spec((1, d)), mod, mod, _const_spec((e, d)), _const_spec((e, 1))],
        out_specs=out_specs,
        out_shape=out_shape,
        scratch_shapes=scratch,
        compiler_params=_params(*semantics),
        name='router',
    )(x, g.reshape(1, d), scale, shift, w_router.T, e_bias.reshape(e, 1))


def _moe_kernel(h_ref, x_ref, gate_ref, g2_ref, wg_ref, wu_ref, wd_ref, sg_ref, su_ref, sd_ref, *rest, final):
    if final:
        fg_ref, o_ref, acc_ref = rest
    else:
        o_ref, acc_ref = rest
    e = pl.program_id(2)
    h = h_ref[0]

    @pl.when(e == 0)
    def _():
        a = jnp.dot(h, sg_ref[...], preferred_element_type=F32)
        u = jnp.dot(h, su_ref[...], preferred_element_type=F32)
        acc_ref[...] = jnp.dot((_silu(a) * u).astype(BF16), sd_ref[...], preferred_element_type=F32)

    lane = lax.broadcasted_iota(jnp.int32, gate_ref.shape[1:], 1)
    gcol = jnp.sum(jnp.where(lane == e, gate_ref[0], 0.0), axis=-1, keepdims=True)
    a = jnp.dot(h, wg_ref[0].astype(BF16), preferred_element_type=F32)
    u = jnp.dot(h, wu_ref[0].astype(BF16), preferred_element_type=F32)
    acc_ref[...] += jnp.dot((_silu(a) * u * gcol).astype(BF16), wd_ref[0].astype(BF16), preferred_element_type=F32)

    @pl.when(e == pl.num_programs(2) - 1)
    def _():
        y = x_ref[0] + g2_ref[0] * acc_ref[...]
        if final:
            y = _rms(y) * fg_ref[...]
        o_ref[0] = y


def _moe(h2, x, gate, g2, layer, w_gate, w_up, w_down, s_gate, s_up, s_down, final_g=None, tm=1024):
    b, s, d = x.shape
    tm = min(tm, s)
    _, e, _, ff = w_gate.shape
    row = pl.BlockSpec((1, tm, d), lambda i, j, k: (i, j, 0))
    ins = [h2, x, gate, g2, w_gate, w_up, w_down, s_gate, s_up, s_down]
    in_specs = [row, row, pl.BlockSpec((1, tm, e), lambda i, j, k: (i, j, 0)),
                pl.BlockSpec((1, 1, d), lambda i, j, k: (i, 0, 0)),
                pl.BlockSpec((None, 1, d, ff), lambda i, j, k: (layer, k, 0, 0)),
                pl.BlockSpec((None, 1, d, ff), lambda i, j, k: (layer, k, 0, 0)),
                pl.BlockSpec((None, 1, ff, d), lambda i, j, k: (layer, k, 0, 0)),
                _const_spec(s_gate.shape), _const_spec(s_up.shape), _const_spec(s_down.shape)]
    if final_g is not None:
        ins.append(final_g.reshape(1, d))
        in_specs.append(_const_spec((1, d)))
    return pl.pallas_call(
        functools.partial(_moe_kernel, final=final_g is not None),
        grid=(b, s // tm, e),
        in_specs=in_specs,
        out_specs=row,
        out_shape=jax.ShapeDtypeStruct(x.shape, F32),
        scratch_shapes=[pltpu.VMEM((tm, d), F32)],
        compiler_params=_params('parallel', 'parallel', 'arbitrary'),
        name='moe',
    )(*ins)


MOE_ROW_TILE = 512
SC_ROWS = 128
V7X_SC_CORES = 2
V7X_SC_SUBCORES = 16


def _pack_halves(x):
    n = x.shape[1] // 2
    lo = pltpu.bitcast(x[:, :n].astype(BF16).astype(F32), jnp.int32)
    hi = pltpu.bitcast(x[:, n:].astype(BF16).astype(F32), jnp.int32)
    return jnp.bitwise_or(jnp.bitwise_and(hi, -65536), lax.shift_right_logical(lo, 16))


def _unpack_halves(p):
    lo = pltpu.bitcast(lax.shift_left(p, 16), F32).astype(BF16)
    hi = pltpu.bitcast(jnp.bitwise_and(p, -65536), F32).astype(BF16)
    return lo, hi


def _route_pos_kernel(off_ref, eid_ref, rank_ref, pos_ref):
    eid = eid_ref[...]
    base = jnp.zeros(eid.shape, jnp.int32)
    for e in range(N_EXPERTS):
        base = jnp.where(eid == e, off_ref[e], base)
    pos_ref[...] = base + rank_ref[...]


def _route_pos(offsets, eid, rank):
    return pl.pallas_call(
        _route_pos_kernel,
        grid=(1,),
        in_specs=[pl.BlockSpec(memory_space=pltpu.SMEM), _const_spec(eid.shape), _const_spec(rank.shape)],
        out_specs=_const_spec(eid.shape),
        out_shape=jax.ShapeDtypeStruct(eid.shape, jnp.int32),
        compiler_params=_params('arbitrary'),
        name='route_pos',
    )(offsets, eid, rank)


def _sc_mesh():
    return plsc.VectorSubcoreMesh(core_axis_name='c', subcore_axis_name='s', num_cores=V7X_SC_CORES,
                                  num_subcores=V7X_SC_SUBCORES)


def _sc_dispatch(hp, pos, n_rows):
    t, w = hp.shape
    k = pos.shape[0]
    workers = V7X_SC_CORES * V7X_SC_SUBCORES
    per_worker = t // workers
    pos_flat = pos.reshape(k * t)

    @functools.partial(pl.kernel, mesh=_sc_mesh(), out_type=jax.ShapeDtypeStruct((n_rows, w), jnp.int32),
                       scratch_types=[pltpu.VMEM((SC_ROWS,), jnp.int32), pltpu.VMEM((SC_ROWS, w), jnp.int32),
                                      pltpu.SemaphoreType.DMA])
    def scatter(hp_hbm, pos_hbm, out_hbm, idx_v, rows_v, sem):
        wid = lax.axis_index('s') * V7X_SC_CORES + lax.axis_index('c')

        @pl.loop(0, per_worker // SC_ROWS)
        def _(i):
            t0 = pl.multiple_of(wid * per_worker + i * SC_ROWS, SC_ROWS)
            pltpu.sync_copy(hp_hbm.at[pl.ds(t0, SC_ROWS)], rows_v)
            for j in range(k):
                pltpu.sync_copy(pos_hbm.at[pl.ds(pl.multiple_of(j * t + t0, SC_ROWS), SC_ROWS)], idx_v)
                pltpu.async_copy(rows_v, out_hbm.at[idx_v], sem).wait()

    return scatter(hp, pos_flat)


def _sc_collect(yp, pos):
    _, w = yp.shape
    k, t = pos.shape
    workers = V7X_SC_CORES * V7X_SC_SUBCORES
    per_worker = k * t // workers
    pos_flat = pos.reshape(k * t)

    @functools.partial(pl.kernel, mesh=_sc_mesh(), out_type=jax.ShapeDtypeStruct((k * t, w), jnp.int32),
                       scratch_types=[pltpu.VMEM((SC_ROWS,), jnp.int32), pltpu.VMEM((SC_ROWS, w), jnp.int32),
                                      pltpu.SemaphoreType.DMA])
    def gather(yp_hbm, pos_hbm, out_hbm, idx_v, rows_v, sem):
        wid = lax.axis_index('s') * V7X_SC_CORES + lax.axis_index('c')

        @pl.loop(0, per_worker // SC_ROWS)
        def _(i):
            r0 = pl.multiple_of(wid * per_worker + i * SC_ROWS, SC_ROWS)
            pltpu.sync_copy(pos_hbm.at[pl.ds(r0, SC_ROWS)], idx_v)
            pltpu.async_copy(yp_hbm.at[idx_v], rows_v, sem).wait()
            pltpu.sync_copy(rows_v, out_hbm.at[pl.ds(r0, SC_ROWS)])

    return gather(yp, pos_flat)


def _expert_kernel(te_ref, nu_ref, x_ref, wg_ref, wu_ref, wd_ref, o_ref, wg_s, wu_s, wd_s):
    i = pl.program_id(0)

    @pl.when(i < nu_ref[0])
    def _():
        @pl.when((i == 0) | (te_ref[i] != te_ref[jnp.maximum(i - 1, 0)]))
        def _():
            wg_s[...] = wg_ref[0].astype(BF16)
            wu_s[...] = wu_ref[0].astype(BF16)
            wd_s[...] = wd_ref[0].astype(BF16)

        lo, hi = _unpack_halves(x_ref[...])
        half = lo.shape[1]
        a = (jnp.dot(lo, wg_s[:half, :], preferred_element_type=F32)
             + jnp.dot(hi, wg_s[half:, :], preferred_element_type=F32))
        u = (jnp.dot(lo, wu_s[:half, :], preferred_element_type=F32)
             + jnp.dot(hi, wu_s[half:, :], preferred_element_type=F32))
        y = jnp.dot((_silu(a) * u).astype(BF16), wd_s[...], preferred_element_type=F32)
        o_ref[...] = _pack_halves(y)


def _experts(xp, tile_expert, n_used, layer, w_gate, w_up, w_down):
    n_rows, half = xp.shape
    _, _, d, ff = w_gate.shape
    r = MOE_ROW_TILE
    row = pl.BlockSpec((r, half), lambda i, te, nu: (i, 0))
    grid_spec = pltpu.PrefetchScalarGridSpec(
        num_scalar_prefetch=2,
        grid=(n_rows // r,),
        in_specs=[row,
                  pl.BlockSpec((None, 1, d, ff), lambda i, te, nu: (layer, te[i], 0, 0)),
                  pl.BlockSpec((None, 1, d, ff), lambda i, te, nu: (layer, te[i], 0, 0)),
                  pl.BlockSpec((None, 1, ff, d), lambda i, te, nu: (layer, te[i], 0, 0))],
        out_specs=row,
        scratch_shapes=[pltpu.VMEM((d, ff), BF16), pltpu.VMEM((d, ff), BF16), pltpu.VMEM((ff, d), BF16)],
    )
    return pl.pallas_call(
        _expert_kernel,
        grid_spec=grid_spec,
        out_shape=jax.ShapeDtypeStruct((n_rows, half), jnp.int32),
        compiler_params=_params('arbitrary'),
        name='experts',
    )(tile_expert, n_used, xp, w_gate, w_up, w_down)


def _combine_kernel(yg_ref, w_ref, hp_ref, x_ref, g2_ref, sg_ref, su_ref, sd_ref, *rest, final):
    if final:
        fg_ref, o_ref = rest
    else:
        (o_ref,) = rest
    half = hp_ref.shape[2]
    lo, hi = _unpack_halves(hp_ref[0])
    sg, su = sg_ref[...], su_ref[...]
    a = jnp.dot(lo, sg[:half], preferred_element_type=F32) + jnp.dot(hi, sg[half:], preferred_element_type=F32)
    u = jnp.dot(lo, su[:half], preferred_element_type=F32) + jnp.dot(hi, su[half:], preferred_element_type=F32)
    acc = jnp.dot((_silu(a) * u).astype(BF16), sd_ref[...], preferred_element_type=F32)
    acc_lo, acc_hi = acc[:, :half], acc[:, half:]
    wts = w_ref[0]
    for k in range(yg_ref.shape[0]):
        ylo, yhi = _unpack_halves(yg_ref[k, 0])
        wk = wts[:, k:k + 1]
        acc_lo = acc_lo + wk * ylo.astype(F32)
        acc_hi = acc_hi + wk * yhi.astype(F32)
    y = x_ref[0] + g2_ref[0] * jnp.concatenate([acc_lo, acc_hi], axis=1)
    if final:
        y = _rms(y) * fg_ref[...]
    o_ref[0] = y


def _combine(yg, wts, hp, x, g2, s_gate, s_up, s_down, final_g=None, tm=256):
    b, s, d = x.shape
    k = yg.shape[0]
    half = d // 2
    row = pl.BlockSpec((1, tm, d), lambda i, j: (i, j, 0))
    prow = pl.BlockSpec((1, tm, half), lambda i, j: (i, j, 0))
    ins = [yg, wts, hp, x, g2, s_gate, s_up, s_down]
    in_specs = [pl.BlockSpec((k, 1, tm, half), lambda i, j: (0, i, j, 0)), pl.BlockSpec((1, tm, k), lambda i, j: (i, j, 0)),
                prow, row, pl.BlockSpec((1, 1, d), lambda i, j: (i, 0, 0)),
                _const_spec(s_gate.shape), _const_spec(s_up.shape), _const_spec(s_down.shape)]
    if final_g is not None:
        ins.append(final_g.reshape(1, d))
        in_specs.append(_const_spec((1, d)))
    return pl.pallas_call(
        functools.partial(_combine_kernel, final=final_g is not None),
        grid=(b, s // tm),
        in_specs=in_specs,
        out_specs=row,
        out_shape=jax.ShapeDtypeStruct(x.shape, F32),
        compiler_params=_params('parallel', 'parallel'),
        name='moe_combine',
    )(*ins)


def _routed_moe(x, g, scale, shift, g2, w_router, e_bias, layer, w_gate, w_up, w_down, s_gate, s_up, s_down, final_g=None):
    b, s, d = x.shape
    t = b * s
    hp, eid, rank, wts, counts = _router(x, g, scale, shift, w_router, e_bias, compact=True)
    counts = counts[:, 0].astype(jnp.int32)
    r = MOE_ROW_TILE
    padded = (counts + (r - 1)) // r * r
    ends = jnp.cumsum(padded)
    offsets = ends - padded
    n_rows = t * TOP_K + N_EXPERTS * r
    tile_start = jnp.arange(n_rows // r, dtype=jnp.int32) * r
    tile_expert = jnp.minimum(jnp.sum((tile_start[:, None] >= ends[None, :]).astype(jnp.int32), axis=1), N_EXPERTS - 1)
    n_used = (ends[-1] // r).reshape(1).astype(jnp.int32)
    pos = _route_pos(offsets.astype(jnp.int32), eid, rank)
    xp = _sc_dispatch(hp.reshape(t, d // 2), pos, n_rows)
    yp = _experts(xp, tile_expert.astype(jnp.int32), n_used, layer, w_gate, w_up, w_down)
    yg = _sc_collect(yp, pos).reshape(TOP_K, b, s, d // 2)
    return _combine(yg, wts.T.reshape(b, s, TOP_K), hp, x, g2, s_gate, s_up, s_down, final_g)


def _mixers(p, pc, ctx_out, prm, l, lam_init, rope_tabs, lb_terms):
    s = p['hy_v'].shape[1]
    sc = pc['hy_v'].shape[1]

    hy_args = (prm['hy_w1'][l], prm['hy_b1'][l], prm['hy_w2'][l], prm['hy_b2'][l], prm['hy_w3'][l], prm['hy_b3'][l],
               prm['hy_sin_freq'][l], prm['hy_decay'][l])
    y_hy = _hyena([p['hy_v'], p['hy_x1'], p['hy_x2']], prm['hy_conv_w'][l], prm['hy_conv_b'][l],
                  _hy_filters(s, *hy_args), prm['hy_bias'][l], inner=128)
    yc_hy = None
    if ctx_out:
        yc_hy = _hyena([pc['hy_v'], pc['hy_x1'], pc['hy_x2']], prm['hy_conv_w'][l], prm['hy_conv_b'][l],
                       _hy_filters(sc, *hy_args), prm['hy_bias'][l], inner=32)

    lp = prm['da_lambda'][l].astype(F32)
    lam = jnp.exp(jnp.sum(lp[0] * lp[1])) - jnp.exp(jnp.sum(lp[2] * lp[3])) + lam_init
    da_kw = dict(heads=DA_HEADS, ncomp=2, scale=DA_HEAD_DIM ** -0.5, lam=lam, subln_g=prm['da_subln_g'][l],
                 post_scale=1.0 - lam_init)
    da_ctx = ([pc['da_k']], pc['da_v'])
    y_da = _attention([p['da_q']], [da_ctx, ([p['da_k']], p['da_v'])], **da_kw)
    yc_da = _attention([pc['da_q']], [da_ctx], **da_kw) if ctx_out else None

    wq = prm['mla_w_q_up'][l].reshape(MLA_Q_RANK, MLA_HEADS, MLA_NOPE_DIM + MLA_ROPE_DIM)
    wq_n = wq[:, :, :MLA_NOPE_DIM].reshape(MLA_Q_RANK, -1).astype(BF16)
    wq_r = wq[:, :, MLA_NOPE_DIM:].reshape(MLA_Q_RANK, -1).astype(BF16)
    wkv = prm['mla_w_kv_up'][l].reshape(MLA_KV_RANK, MLA_HEADS, MLA_NOPE_DIM + MLA_V_DIM)
    wkv_n = wkv[:, :, :MLA_NOPE_DIM].reshape(MLA_KV_RANK, -1).astype(BF16)
    wkv_v = wkv[:, :, MLA_NOPE_DIM:].reshape(MLA_KV_RANK, -1).astype(BF16)

    def queries(qd, tabs):
        return _norm_proj(qd, prm['mla_q_norm_g'][l], [(wq_n, F32, False, MLA_HEADS), (wq_r, F32, True, MLA_HEADS)],
                          rope_tabs=tabs)

    def keys_values(kvd):
        return _norm_proj(kvd, prm['mla_kv_norm_g'][l], [(wkv_n, BF16, False, MLA_HEADS), (wkv_v, BF16, False, MLA_HEADS)])

    kn_l, v_l = keys_values(p['mla_kv'])
    kn_c, v_c = keys_values(pc['mla_kv'])
    mla_kw = dict(heads=MLA_HEADS, ncomp=1, scale=(MLA_NOPE_DIM + MLA_ROPE_DIM) ** -0.5)
    mla_ctx = ([kn_c, pc['mla_kr']], v_c)
    y_mla = _attention(queries(p['mla_q'], rope_tabs), [mla_ctx, ([kn_l, p['mla_kr']], v_l)], **mla_kw)
    yc_mla = _attention(queries(pc['mla_q'], None), [mla_ctx], **mla_kw) if ctx_out else None

    o, oc = _hgrn(p['hg_q'], p['hg_ff'], p['hg_fb'], p['hg_i'], pc['hg_q'], pc['hg_ff'], pc['hg_fb'], pc['hg_i'],
                  lb_terms, prm['hg_norm_g'][l], variant=True, unroll=(2, 4)[l])
    return (y_hy, y_da, y_mla, o), (yc_hy, yc_da, yc_mla, oc)


def kernel(x, c, ctx, c_ctx, w_ada, b_ada, norm1_g, norm2_g, w_in, w_out, hy_conv_w, hy_conv_b, hy_w1, hy_b1, hy_w2, hy_b2, hy_w3, hy_b3, hy_sin_freq, hy_decay, hy_bias, da_lambda, da_subln_g, mla_q_norm_g, mla_w_q_up, mla_kv_norm_g, mla_w_kv_up, hg_lower_bounds, hg_norm_g, moe_w_router, moe_bias, moe_w_gate, moe_w_up, moe_w_down, moe_sh_gate, moe_sh_up, moe_sh_down, final_norm_g):
    prm = dict(hy_conv_w=hy_conv_w, hy_conv_b=hy_conv_b, hy_w1=hy_w1, hy_b1=hy_b1, hy_w2=hy_w2, hy_b2=hy_b2,
               hy_w3=hy_w3, hy_b3=hy_b3, hy_sin_freq=hy_sin_freq, hy_decay=hy_decay, hy_bias=hy_bias,
               da_lambda=da_lambda, da_subln_g=da_subln_g, mla_q_norm_g=mla_q_norm_g, mla_w_q_up=mla_w_q_up,
               mla_kv_norm_g=mla_kv_norm_g, mla_w_kv_up=mla_w_kv_up, hg_norm_g=hg_norm_g)
    b, n_lat, d = x.shape
    depth = w_in.shape[0]
    rows = n_lat // GRID_W
    row_pos = jnp.repeat(jnp.arange(rows, dtype=jnp.int32), GRID_W)
    col_pos = jnp.tile(jnp.arange(GRID_W, dtype=jnp.int32), rows)
    rope_tabs = _rope_tables(row_pos, col_pos, 2 * DA_HEADS * DA_HEAD_DIM)
    lbs = jnp.cumsum(jax.nn.softmax(hg_lower_bounds.astype(F32), axis=1), axis=1)
    lbs = lbs - lbs[:, :1]
    cond = jnp.concatenate([c, c_ctx[None], jnp.zeros((8 - b - 1, d), F32)], axis=0)

    for l in range(depth):
        ctx_out = l < depth - 1
        mods = _ada(cond, w_ada[l], b_ada[l])
        sh1, sc1, g1, sh2, sc2, g2 = [m[:, None, :] for m in jnp.split(mods[:b], 6, axis=-1)]
        mc = [jnp.broadcast_to(m[:, None, :], (b, 1, d)) for m in jnp.split(mods[b:b + 1], 6, axis=-1)]

        off = 0
        outs = []
        for _, wdt, dt, rope, split, rep in _SEGMENTS:
            w = w_in[l][:, off:off + wdt].astype(BF16)
            outs.append((jnp.tile(w, (1, rep)) if rep > 1 else w, dt, rope, split))
            off += wdt
        names = [seg[0] for seg in _SEGMENTS]
        p = dict(zip(names, _norm_proj(x, norm1_g[l], outs, sc1, sh1, rope_tabs=rope_tabs)))
        pc = dict(zip(names, _norm_proj(ctx, norm1_g[l], outs, mc[1], mc[0])))

        lb = lbs[:, l]
        lb_terms = jnp.stack([jnp.log(lb), jnp.log1p(-lb), 1.0 - lb], axis=1)
        lam_init = 0.8 - 0.6 * math.exp(-0.3 * l)
        lat_parts, ctx_parts = _mixers(p, pc, ctx_out, prm, l, lam_init, rope_tabs, lb_terms)

        w_out_b = w_out[l].astype(BF16)
        moe_w = (l, moe_w_gate, moe_w_up, moe_w_down,
                 moe_sh_gate[l].astype(BF16), moe_sh_up[l].astype(BF16), moe_sh_down[l].astype(BF16))

        if ctx_out:
            ctx = _out_proj(*ctx_parts, pc['hg_g'], ctx, mc[2], w_out_b)
            flat = ctx.reshape(1, -1, d)
            h2c, gate_c = _router(flat, norm2_g[l], mc[4][:1], mc[3][:1], moe_w_router[l], moe_bias[l])
            ctx = _moe(h2c, flat, gate_c.transpose(0, 2, 1), mc[5][:1], *moe_w).reshape(ctx.shape)

        x = _out_proj(*lat_parts, p['hg_g'], x, g1, w_out_b)
        x = _routed_moe(x, norm2_g[l], sc2, sh2, g2, moe_w_router[l], moe_bias[l], *moe_w,
                        final_g=None if ctx_out else final_norm_g)

    return x
```

```python
import functools
import math

import numpy as np
import jax
import jax.numpy as jnp
from jax import lax
from jax.experimental import pallas as pl
from jax.experimental.pallas import tpu as pltpu
from jax.experimental.pallas import tpu_sc as plsc

F32 = jnp.float32
BF16 = jnp.bfloat16
HIGHEST = lax.Precision.HIGHEST

D_MODEL = 1024
GRID_W = 64
HY_WIDTH = 256
HY_ORDER = 2
HY_BANDS = 16
DA_HEADS = 4
DA_HEAD_DIM = 32
MLA_HEADS = 4
MLA_Q_RANK = 192
MLA_KV_RANK = 128
MLA_NOPE_DIM = 64
MLA_ROPE_DIM = 32
MLA_V_DIM = 64
HG_HEADS = 4
HG_KEY_DIM = 64
HG_VAL_DIM = 64
HG_CHUNK = 64
HG_SUB = 8
HG_UNROLL = 4
N_EXPERTS = 64
N_EXPERT_GROUPS = 8
TOPK_GROUPS = 4
TOP_K = 8
EXPERT_FF = 256
ROUTED_SCALE = 2.5
ROPE_BASE = 10000.0
NORM_EPS = 1e-6

V7X_VMEM_LIMIT_BYTES = 56 * 1024 * 1024
LANES = 128

_SEGMENTS = (
    ('hy_v', HY_WIDTH, F32, False, 0, 1), ('hy_x1', HY_WIDTH, F32, False, 0, 1), ('hy_x2', HY_WIDTH, F32, False, 0, 1),
    ('da_q', 2 * DA_HEADS * DA_HEAD_DIM, F32, True, 2 * DA_HEADS, 1),
    ('da_k', 2 * DA_HEADS * DA_HEAD_DIM, BF16, True, 2 * DA_HEADS, 1),
    ('da_v', 2 * DA_HEADS * DA_HEAD_DIM, BF16, False, DA_HEADS, 1),
    ('mla_q', MLA_Q_RANK, F32, False, 0, 1), ('mla_kv', MLA_KV_RANK, F32, False, 0, 1),
    ('mla_kr', MLA_ROPE_DIM, BF16, True, MLA_HEADS, MLA_HEADS),
    ('hg_q', HG_HEADS * HG_KEY_DIM, F32, False, 0, 1), ('hg_ff', HG_HEADS * HG_KEY_DIM, F32, False, 0, 1),
    ('hg_fb', HG_HEADS * HG_KEY_DIM, F32, False, 0, 1), ('hg_i', HG_HEADS * HG_VAL_DIM, F32, False, 0, 1),
    ('hg_g', HG_HEADS * HG_VAL_DIM, F32, False, 0, 1),
)


def _params(*semantics):
    return pltpu.CompilerParams(dimension_semantics=semantics, vmem_limit_bytes=V7X_VMEM_LIMIT_BYTES)


def _const_spec(shape):
    nd = len(shape)
    return pl.BlockSpec(shape, lambda *_: (0,) * nd)


def _rms(x, eps=NORM_EPS):
    return x * lax.rsqrt(jnp.mean(x * x, axis=-1, keepdims=True) + eps)


def _silu(x):
    return x * jax.nn.sigmoid(x)


def _dot_nt(a, b, **kw):
    return lax.dot_general(a, b, (((1,), (1,)), ((), ())), preferred_element_type=F32, **kw)


def _ada_kernel(c_ref, w_ref, b_ref, o_ref):
    s = _silu(c_ref[...])
    o_ref[...] = jnp.dot(s, w_ref[...], precision=HIGHEST, preferred_element_type=F32) + b_ref[...]


def _ada(cond, w, b):
    r, d = cond.shape
    n = w.shape[1]
    tn = 1536
    return pl.pallas_call(
        _ada_kernel,
        grid=(n // tn,),
        in_specs=[_const_spec((r, d)), pl.BlockSpec((d, tn), lambda j: (0, j)), pl.BlockSpec((1, tn), lambda j: (0, j))],
        out_specs=pl.BlockSpec((r, tn), lambda j: (0, j)),
        out_shape=jax.ShapeDtypeStruct((r, n), F32),
        compiler_params=_params('arbitrary'),
        name='ada',
    )(cond, w, b.reshape(1, n))


ROPE_UNIT = 32


def _rope_tables(row, col, width):
    n = ROPE_UNIT // 4
    inv = ROPE_BASE ** (-jnp.arange(n, dtype=F32) / n)
    units = width // ROPE_UNIT
    parts_c, parts_a, parts_b = [], [], []
    zero = jnp.zeros((row.shape[0], n), F32)
    for pos in (row, col):
        ang = pos.astype(F32)[:, None] * inv
        cos, sin = jnp.cos(ang), jnp.sin(ang)
        parts_c += [cos, cos]
        parts_a += [zero, sin]
        parts_b += [-sin, zero]
    tile = lambda ps: jnp.tile(jnp.concatenate(ps, axis=1), (1, units))
    return tile(parts_c), tile(parts_a), tile(parts_b)


def _norm_proj_kernel(*refs, n_w, modulate, ropes, splits):
    x_ref, g_ref = refs[0], refs[1]
    pos = 2
    if modulate:
        sc_ref, sh_ref = refs[2], refs[3]
        pos = 4
    if any(ropes):
        rc_ref, ra_ref, rb_ref = refs[pos:pos + 3]
        pos += 3
    w_refs = refs[pos:pos + n_w]
    o_refs = refs[pos + n_w:]
    y = _rms(x_ref[0]) * g_ref[...]
    if modulate:
        y = y * (1.0 + sc_ref[0]) + sh_ref[0]
    yb = y.astype(BF16)
    for w_ref, o_ref, rope, split in zip(w_refs, o_refs, ropes, splits):
        o = jnp.dot(yb, w_ref[...], preferred_element_type=F32)
        if rope:
            wd = o.shape[1]
            shift = ROPE_UNIT // 4
            o = (o * rc_ref[:, :wd] + pltpu.roll(o, shift, axis=1) * ra_ref[:, :wd]
                 + pltpu.roll(o, wd - shift, axis=1) * rb_ref[:, :wd])
        if split:
            unit = o.shape[1] // split
            for u in range(split):
                o_ref[0, u] = o[:, u * unit:(u + 1) * unit].astype(o_ref.dtype)
        else:
            o_ref[0] = o.astype(o_ref.dtype)


def _norm_proj(x, g, outs, scale=None, shift=None, rope_tabs=None, tm=512):
    b, s, k = x.shape
    tm = min(tm, s)
    modulate = scale is not None
    ropes = tuple(bool(o[2]) and rope_tabs is not None for o in outs)
    splits = tuple(o[3] for o in outs)
    ins = [x, g.reshape(1, k)]
    in_specs = [pl.BlockSpec((1, tm, k), lambda i, j: (i, j, 0)), _const_spec((1, k))]
    if modulate:
        ins += [scale, shift]
        in_specs += [pl.BlockSpec((1, 1, k), lambda i, j: (i, 0, 0))] * 2
    if any(ropes):
        ins += list(rope_tabs)
        in_specs += [pl.BlockSpec((tm, rope_tabs[0].shape[1]), lambda i, j: (j, 0))] * 3
    out_specs, out_shape = [], []
    for w, dt, _, split in outs:
        ins.append(w)
        in_specs.append(_const_spec(w.shape))
        n = w.shape[1]
        if split:
            out_specs.append(pl.BlockSpec((1, split, tm, n // split), lambda i, j: (i, 0, j, 0)))
            out_shape.append(jax.ShapeDtypeStruct((b, split, s, n // split), dt))
        else:
            out_specs.append(pl.BlockSpec((1, tm, n), lambda i, j: (i, j, 0)))
            out_shape.append(jax.ShapeDtypeStruct((b, s, n), dt))
    return pl.pallas_call(
        functools.partial(_norm_proj_kernel, n_w=len(outs), modulate=modulate, ropes=ropes, splits=splits),
        grid=(b, s // tm),
        in_specs=in_specs,
        out_specs=out_specs,
        out_shape=out_shape,
        compiler_params=_params('parallel', 'parallel'),
        name='norm_proj',
    )(*ins)


def _hy_filter_kernel(w1t_ref, w1s_ref, w1c_ref, b1_ref, w2_ref, b2_ref, w3_ref, b3_ref, fr_ref, dec_ref, o_ref, *, n):
    t = lax.broadcasted_iota(jnp.int32, (n, 1), 0).astype(F32) / n
    bands = lax.broadcasted_iota(jnp.int32, (1, HY_BANDS), 1).astype(F32) + 1.0
    ang = (2.0 * jnp.pi) * t * bands
    pre = (t * w1t_ref[...]
           + jnp.dot(jnp.sin(ang), w1s_ref[...], precision=HIGHEST, preferred_element_type=F32)
           + jnp.dot(jnp.cos(ang), w1c_ref[...], precision=HIGHEST, preferred_element_type=F32)
           + b1_ref[...])
    hid = jnp.sin(fr_ref[0:1, :] * pre)
    hid = jnp.sin(fr_ref[1:2, :] * (jnp.dot(hid, w2_ref[...], precision=HIGHEST, preferred_element_type=F32) + b2_ref[...]))
    filt = jnp.dot(hid, w3_ref[...], precision=HIGHEST, preferred_element_type=F32) + b3_ref[...]
    filt = filt * jnp.exp(-t * jnp.abs(dec_ref[...]))
    col = jnp.sum(jnp.abs(filt), axis=0, keepdims=True) - jnp.abs(filt[0:1, :])
    w = HY_WIDTH
    for o in range(HY_ORDER):
        lo = o * 2 * w
        f0 = filt[0:1, lo:lo + w] + filt[0:1, lo + w:lo + 2 * w]
        inv = 1.0 / (col[:, lo:lo + w] + col[:, lo + w:lo + 2 * w] + jnp.abs(f0))
        o_ref[:, lo:lo + w] = filt[:, lo:lo + w] * inv
        o_ref[:, lo + w:lo + 2 * w] = filt[:, lo + w:lo + 2 * w] * inv


def _hy_filters(n, w1, b1, w2, b2, w3, b3, freq, decay):
    cols = w3.shape[1]
    ins = [w1[0:1], w1[1:1 + HY_BANDS], w1[1 + HY_BANDS:], b1.reshape(1, -1), w2, b2.reshape(1, -1), w3,
           b3.reshape(1, -1), freq, decay.reshape(1, -1)]
    out = pl.pallas_call(
        functools.partial(_hy_filter_kernel, n=n),
        grid=(1,),
        in_specs=[_const_spec(a.shape) for a in ins],
        out_specs=_const_spec((n, cols)),
        out_shape=jax.ShapeDtypeStruct((n, cols), F32),
        compiler_params=_params('arbitrary'),
        name='hy_filter',
    )(*ins)
    return out.reshape(n, HY_ORDER, 2, HY_WIDTH)


def _two_sided(filt_n):
    n = filt_n.shape[0]
    hf, hb = filt_n[:, :, 0], filt_n[:, :, 1]
    h = jnp.concatenate([hf[:1] + hb[:1], hf[1:], jnp.zeros((1,) + hf.shape[1:], F32), hb[:0:-1]], axis=0)
    return h.reshape(2 * n, HY_ORDER * HY_WIDTH)


def _short_conv_kernel(*refs, s):
    x_refs, w_refs, b_refs, o_refs = refs[0:3], refs[3:6], refs[6:9], refs[9:12]
    row = lax.broadcasted_iota(jnp.int32, (s, 1), 0)
    for x_ref, w_ref, b_ref, o_ref in zip(x_refs, w_refs, b_refs, o_refs):
        x = x_ref[0]
        prev = jnp.where(row == 0, 0.0, pltpu.roll(x, 1, axis=0))
        nxt = jnp.where(row == s - 1, 0.0, pltpu.roll(x, s - 1, axis=0))
        o_ref[0] = prev * w_ref[0:1, :] + x * w_ref[1:2, :] + nxt * w_ref[2:3, :] + b_ref[...]


def _short_conv(parts, conv_w, conv_b):
    b, s, c = parts[0].shape
    tc = LANES
    ws = [conv_w[:, i * c:(i + 1) * c] for i in range(3)]
    bs = [conv_b[i * c:(i + 1) * c].reshape(1, c) for i in range(3)]
    xspec = pl.BlockSpec((1, s, tc), lambda i, j: (i, 0, j))
    return pl.pallas_call(
        functools.partial(_short_conv_kernel, s=s),
        grid=(b, c // tc),
        in_specs=[xspec] * 3 + [pl.BlockSpec((3, tc), lambda i, j: (0, j))] * 3 + [pl.BlockSpec((1, tc), lambda i, j: (0, j))] * 3,
        out_specs=[xspec] * 3,
        out_shape=[jax.ShapeDtypeStruct((b, s, c), F32)] * 3,
        compiler_params=_params('parallel', 'parallel'),
        name='short_conv',
    )(*parts, *ws, *bs)


def _dft_cos_sin(rows, cols, period):
    ang = 2.0 * np.pi * ((np.arange(rows)[:, None] * np.arange(cols)[None, :]) % period) / period
    return np.cos(ang), np.sin(ang)


def _fft_tables(n, inner):
    big = 2 * n
    n1 = big // inner
    c1, s1 = _dft_cos_sin(n1, n1, n1)
    h = n1 // 2
    outer_data = np.block([[c1[:, :h], s1[:, :h]], [-s1[:, :h], c1[:, :h]]])
    outer_real = np.concatenate([c1, -s1], axis=0)
    outer_inv = np.block([[c1[:h, :], -s1[:h, :]], [s1[:h, :], c1[:h, :]]]) / big
    c2, s2 = _dft_cos_sin(inner, inner, inner)
    inner_fwd = np.block([[c2, s2], [-s2, c2]])
    inner_inv = np.block([[c2, -s2], [s2, c2]])
    ct, st = _dft_cos_sin(n1, inner, big)
    f = lambda a: jnp.asarray(a, F32)
    return dict(n1=n1, inner=inner, outer_data=_hi_lo_cols(outer_data), outer_real=_hi_lo_cols(outer_real),
                outer_inv=_hi_lo_cols(outer_inv),
                inner_fwd=_hi_lo_cols(inner_fwd), inner_inv=_hi_lo_cols(inner_inv),
                tw_cos=f(ct).reshape(n1, inner, 1), tw_sin=f(st).reshape(n1, inner, 1))


def _left_mm_kernel(m_ref, x_ref, o_ref):
    o_ref[0] = jnp.dot(m_ref[...], _hi_lo_rows(x_ref[0]), preferred_element_type=F32)


def _left_mm(m, x, tl=4096):
    p, k, l = x.shape
    mm = m.shape[0]
    tl = min(tl, l)
    return pl.pallas_call(
        _left_mm_kernel,
        grid=(p, l // tl),
        in_specs=[_const_spec(m.shape), pl.BlockSpec((1, k, tl), lambda i, j: (i, 0, j))],
        out_specs=pl.BlockSpec((1, mm, tl), lambda i, j: (i, 0, j)),
        out_shape=jax.ShapeDtypeStruct((p, mm, l), F32),
        compiler_params=_params('parallel', 'parallel'),
        name='fft_outer',
    )(m, x)


def _hi_lo_cols(m):
    m = np.asarray(m, np.float32)
    hi = m.astype(BF16)
    lo = (m - hi.astype(np.float32)).astype(BF16)
    return jnp.asarray(np.concatenate([hi, hi, lo], axis=1))


def _hi_lo_rows(x):
    hi = x.astype(BF16)
    lo = (x - hi.astype(F32)).astype(BF16)
    return jnp.concatenate([hi, lo, hi], axis=0)


def _inner_kernel(a_ref, twc_ref, tws_ref, gf_ref, *rest, convolve, inner, kb):
    for s in range(kb):
        ar, ai = a_ref[0, 0, s], a_ref[0, 1, s]
        tc, ts = twc_ref[s], tws_ref[s]
        br = ar * tc + ai * ts
        bi = ai * tc - ar * ts
        x = jnp.dot(gf_ref[...], _hi_lo_rows(jnp.concatenate([br, bi], axis=0)), preferred_element_type=F32)
        if not convolve:
            o_ref = rest[0]
            o_ref[0, 0, s] = x[:inner]
            o_ref[0, 1, s] = x[inner:]
            continue
        h_ref, gi_ref, o_ref = rest
        xr, xi = x[:inner], x[inner:]
        hr, hi = h_ref[0, 0, s], h_ref[0, 1, s]
        yr = xr * hr - xi * hi
        yi = xr * hi + xi * hr
        z = jnp.dot(gi_ref[...], _hi_lo_rows(jnp.concatenate([yr, yi], axis=0)), preferred_element_type=F32)
        zr, zi = z[:inner], z[inner:]
        o_ref[0, 0, s] = zr * tc - zi * ts
        o_ref[0, 1, s] = zi * tc + zr * ts


def _fft_inner(a, tab, c, h=None, h_block=0):
    p = a.shape[0]
    n1, inner = tab['n1'], tab['inner']
    a5 = a.reshape(p, 2, n1, inner, c)
    tc = 2 * LANES
    kb = 4
    blk = pl.BlockSpec((1, 2, kb, inner, tc), lambda i, k, j: (i, 0, k, 0, j))
    tw_spec = pl.BlockSpec((kb, inner, 1), lambda i, k, j: (k, 0, 0))
    ins = [a5, tab['tw_cos'], tab['tw_sin'], tab['inner_fwd']]
    in_specs = [blk, tw_spec, tw_spec, _const_spec(tab['inner_fwd'].shape)]
    if h is not None:
        ch = h.shape[-1] // inner
        nb = c // tc
        ins += [h.reshape(1, 2, n1, inner, ch), tab['inner_inv']]
        in_specs += [pl.BlockSpec((1, 2, kb, inner, tc), lambda i, k, j: (0, 0, k, 0, h_block * nb + j)),
                     _const_spec(tab['inner_inv'].shape)]
    out = pl.pallas_call(
        functools.partial(_inner_kernel, convolve=h is not None, inner=inner, kb=kb),
        grid=(p, n1 // kb, c // tc),
        in_specs=in_specs,
        out_specs=blk,
        out_shape=jax.ShapeDtypeStruct(a5.shape, F32),
        compiler_params=_params('parallel', 'parallel', 'parallel'),
        name='fft_inner',
    )(*ins)
    return out.reshape(p, 2 * n1, inner * c)


def _gate_kernel(m_ref, z_ref, u_ref, x_ref, bias_ref, *rest, chain):
    y = jnp.dot(m_ref[...], _hi_lo_rows(z_ref[0]), preferred_element_type=F32)
    nxt = x_ref[0] * (y + u_ref[0] * bias_ref[...])
    if chain:
        mf_ref, o_ref, a_ref = rest
        o_ref[0] = nxt
        a_ref[0] = jnp.dot(mf_ref[...], _hi_lo_rows(nxt), preferred_element_type=F32)
    else:
        rest[0][0] = nxt


def _fft_gate(tab, z, u, x, bias_l, chain, tl=4096):
    p, k2, l = z.shape
    n1 = tab['n1']
    tl = min(tl, l)
    row = pl.BlockSpec((1, n1, tl), lambda i, j: (i, 0, j))
    ins = [tab['outer_inv'], z, u, x, bias_l]
    in_specs = [_const_spec(tab['outer_inv'].shape), pl.BlockSpec((1, k2, tl), lambda i, j: (i, 0, j)), row, row,
                pl.BlockSpec((1, tl), lambda i, j: (0, j))]
    out_specs = [row]
    out_shape = [jax.ShapeDtypeStruct((p, n1, l), F32)]
    if chain:
        ins.append(tab['outer_data'])
        in_specs.append(_const_spec(tab['outer_data'].shape))
        out_specs.append(pl.BlockSpec((1, k2, tl), lambda i, j: (i, 0, j)))
        out_shape.append(jax.ShapeDtypeStruct((p, k2, l), F32))
    return pl.pallas_call(
        functools.partial(_gate_kernel, chain=chain),
        grid=(p, l // tl),
        in_specs=in_specs,
        out_specs=out_specs,
        out_shape=out_shape,
        compiler_params=_params('parallel', 'parallel'),
        name='fft_gate',
    )(*ins)


def _hyena(parts, conv_w, conv_b, filt_n, bias, inner):
    b, s, c = parts[0].shape
    tab = _fft_tables(s, inner)
    n1 = tab['n1']
    lanes = inner * c
    h_taps = _two_sided(filt_n).reshape(1, n1, inner * HY_ORDER * c)
    h_spec = _fft_inner(_left_mm(tab['outer_real'], h_taps), tab, HY_ORDER * c)
    v, x1, x2 = [a.reshape(b // 2, n1, lanes) for a in _short_conv(parts, conv_w, conv_b)]
    bias_l = [jnp.tile(bias[o], inner).reshape(1, lanes) for o in range(HY_ORDER)]
    a = _left_mm(tab['outer_data'], v)
    z = _fft_inner(a, tab, c, h_spec, 0)
    z2, a = _fft_gate(tab, z, v, x1, bias_l[0], chain=True)
    z = _fft_inner(a, tab, c, h_spec, 1)
    (z3,) = _fft_gate(tab, z, z2, x2, bias_l[1], chain=False)
    return z3.reshape(b, s, c)


def _attn_kernel(*refs, n_q, n_pieces, ncomp, scale, post_scale):
    q_refs = refs[:n_q]
    pos = n_q
    pieces = []
    for _ in range(n_pieces):
        pieces.append((refs[pos:pos + n_q], refs[pos + n_q]))
        pos += n_q + 1
    if ncomp == 2:
        lam_ref, g_ref = refs[pos:pos + 2]
        pos += 2
    o_ref, kcat_ref, vcat_ref = refs[pos:pos + 3]
    dv = o_ref.shape[3]

    @pl.when(pl.program_id(2) == 0)
    def _():
        row = 0
        for k_refs, v_ref in pieces:
            n = v_ref.shape[2]
            for c in range(ncomp):
                parts = [k_ref[0, c if k_ref.shape[1] == ncomp else 0] for k_ref in k_refs]
                kcat_ref[c, row:row + n, :] = parts[0] if n_q == 1 else jnp.concatenate(parts, axis=1)
            vcat_ref[row:row + n, :dv] = v_ref[0, 0]
            vcat_ref[row:row + n, dv:] = jnp.ones((n, dv), BF16)
            row += n

    outs = []
    for c in range(ncomp):
        q = q_refs[0][0, c] if n_q == 1 else jnp.concatenate([q_ref[0, c] for q_ref in q_refs], axis=1)
        s = _dot_nt((q * (scale * math.log2(math.e))).astype(BF16), kcat_ref[c])
        m = jnp.max(s, axis=-1, keepdims=True)
        p = jnp.exp2((s - m).astype(BF16))
        ol = jnp.dot(p, vcat_ref[...], preferred_element_type=F32)
        outs.append(ol[:, :dv] / ol[:, dv:dv + 1])
    if ncomp == 2:
        o = outs[0] - lam_ref[0] * outs[1]
        o = _rms(o) * g_ref[...] * post_scale
    else:
        o = outs[0]
    o_ref[0, 0] = o


def _attention(q_parts, pieces, heads, ncomp, scale, tq=256, lam=None, subln_g=None, post_scale=1.0):
    b, _, sq, _ = q_parts[0].shape
    dv = pieces[0][1].shape[3]
    tq = min(tq, sq)
    ins = list(q_parts)
    in_specs = [pl.BlockSpec((1, ncomp, tq, q.shape[3]), lambda i, h, j: (i, h, j, 0)) for q in q_parts]
    for k_parts, v in pieces:
        for k in k_parts:
            ins.append(k)
            if k.shape[1] == 1:
                in_specs.append(pl.BlockSpec((1, 1) + k.shape[2:], lambda i, h, j: (i, 0, 0, 0)))
            else:
                in_specs.append(pl.BlockSpec((1, ncomp) + k.shape[2:], lambda i, h, j: (i, h, 0, 0)))
        ins.append(v)
        in_specs.append(pl.BlockSpec((1, 1) + v.shape[2:], lambda i, h, j: (i, h, 0, 0)))
    if ncomp == 2:
        ins += [lam.reshape(1), subln_g.reshape(1, dv)]
        in_specs += [pl.BlockSpec(memory_space=pltpu.SMEM), _const_spec((1, dv))]
    sk = sum(v.shape[2] for _, v in pieces)
    dqk = sum(q.shape[3] for q in q_parts)
    return pl.pallas_call(
        functools.partial(_attn_kernel, n_q=len(q_parts), n_pieces=len(pieces), ncomp=ncomp, scale=scale,
                          post_scale=post_scale),
        grid=(b, heads, sq // tq),
        in_specs=in_specs,
        out_specs=pl.BlockSpec((1, 1, tq, dv), lambda i, h, j: (i, h, j, 0)),
        out_shape=jax.ShapeDtypeStruct((b, heads, sq, dv), F32),
        scratch_shapes=[pltpu.VMEM((ncomp, sk, dqk), BF16), pltpu.VMEM((sk, 2 * dv), BF16)],
        compiler_params=_params('parallel', 'parallel', 'arbitrary'),
        name='attention',
    )(*ins)


def _forget_terms(f, log_lb, log_1m_lb, one_m_lb):
    log_sig = jnp.minimum(f, 0.0) - jnp.log1p(jnp.exp(-jnp.abs(f)))
    b = log_1m_lb + log_sig
    log_g = jnp.maximum(log_lb, b) + jnp.log1p(jnp.exp(-jnp.abs(log_lb - b)))
    return log_g, one_m_lb * jax.nn.sigmoid(-f)


def _hg_tables():
    ck, sub = HG_CHUNK, HG_SUB
    t = np.arange(ck)
    cum_mats, half_masks, group_masks, keeps = [], [], [], []
    for rev in (False, True):
        mats = [(t[None, :] >= t[:, None]) if rev else (t[None, :] <= t[:, None])]
        halves = []
        hs = ck // 2
        while hs >= sub:
            pos = t % (2 * hs)
            b = t - pos + hs
            mats.append((t[None, :] >= b[:, None]) if rev else (t[None, :] < b[:, None]))
            q_half = (pos < hs) if rev else (pos >= hs)
            halves.append(np.stack([q_half, ~q_half]))
            if not rev:
                grp = (t[:, None] // (2 * hs)) == (t[None, :] // (2 * hs))
                group_masks.append(np.concatenate([grp, grp], axis=0))
            hs //= 2
        cum_mats.append(np.concatenate(mats, axis=0))
        half_masks.append(np.stack(halves))
        c = np.arange(ck * sub)
        tt, ss = (c // sub) % sub, c % sub
        keeps.append((ss >= tt) if rev else (ss <= tt))
    lanes = 2 * HG_KEY_DIM
    ln = np.arange(lanes)
    bd = (ln[:, None] // HG_KEY_DIM) == (ln[None, :] // HG_KEY_DIM)
    hm = np.broadcast_to(np.stack(half_masks)[..., None], (2, len(half_masks[0]), 2, ck, lanes))
    return (jnp.asarray(np.stack(cum_mats), BF16), jnp.asarray(hm, F32), jnp.asarray(np.stack(group_masks), F32),
            jnp.asarray(bd, F32), jnp.asarray(np.broadcast_to(np.stack(keeps)[..., None], (2, ck * sub, lanes)), F32))


def _split3(x):
    a = x.astype(BF16)
    r = x - a.astype(F32)
    b = r.astype(BF16)
    return a, b, (r - b.astype(F32)).astype(BF16)


def _hg_chunk(q, k, v, lg, st, rev, cm, hm, gm, keep, bd, m0, m1):
    ck, sub = HG_CHUNK, HG_SUB
    lanes = lg.shape[1]
    c3 = jnp.dot(cm, jnp.concatenate(_split3(lg), axis=1), preferred_element_type=F32)
    call = c3[:, :lanes] + c3[:, lanes:2 * lanes] + c3[:, 2 * lanes:]
    cum = call[0:ck]
    tot = cum[0:1] if rev else cum[ck - 1:ck]
    o = _dot_nt((q * jnp.exp(cum)).astype(BF16), st.astype(BF16))
    kd = (k * jnp.exp(tot - cum)).astype(BF16)
    st_new = st * jnp.exp(tot) + bd * jnp.dot(v.T.astype(BF16), kd, preferred_element_type=F32)
    s2 = None
    for lv in range(gm.shape[0]):
        cb = call[(lv + 1) * ck:(lv + 2) * ck]
        qd = q * jnp.exp(jnp.minimum(cum - cb, 0.0)) * hm[lv, 0]
        kf = (k * jnp.exp(jnp.minimum(cb - cum, 0.0)) * hm[lv, 1]).astype(BF16)
        q2 = jnp.concatenate([qd * m0, qd * m1], axis=0).astype(BF16)
        term = _dot_nt(q2, kf) * gm[lv]
        s2 = term if s2 is None else s2 + term
    r = jnp.dot(s2.astype(BF16), v.astype(BF16), preferred_element_type=F32)
    o = o + m0 * r[:ck] + m1 * r[ck:]
    rows, vts = [], []
    for i in range(ck // sub):
        lo, hi = i * sub, (i + 1) * sub
        ki, ci = k[lo:hi], cum[lo:hi]
        for t in range(lo, hi):
            rows.append(q[t:t + 1] * ki * jnp.exp(jnp.minimum(cum[t:t + 1] - ci, 0.0)))
            vts.append(v[lo:hi])
    sc = jnp.dot(jnp.concatenate(rows, axis=0).astype(BF16), bd.astype(BF16), preferred_element_type=F32)
    prod = sc * jnp.concatenate(vts, axis=0) * keep
    o = o + jnp.sum(prod.reshape(ck, sub, lanes), axis=1)
    return o, st_new


def _hgrn_kernel(q_ref, ff_ref, fb_ref, i_ref, qc_ref, ffc_ref, fbc_ref, ic_ref, lb_ref, g_ref,
                 cm_ref, hm_ref, gm_ref, bd_ref, keep_ref, o_ref, oc_ref, or_ref, ocr_ref, st_ref, *, n_lat, n_ctx):
    ck = HG_CHUNK
    lanes = o_ref.shape[-1]
    lane = lax.broadcasted_iota(jnp.int32, (1, lanes), 1)
    m0 = (lane < HG_KEY_DIM).astype(F32)
    m1 = 1.0 - m0
    bd = bd_ref[...]
    gm = gm_ref[...]

    def one(q, f, v, rev):
        d = 1 if rev else 0
        lg, k = _forget_terms(f, lb_ref[d, 0:1, :], lb_ref[d, 1:2, :], lb_ref[d, 2:3, :])
        o, st = _hg_chunk(q, k, v, lg, st_ref[d], rev, cm_ref[d], hm_ref[d], gm, keep_ref[d], bd, m0, m1)
        st_ref[d] = st
        return o

    def sweep(qr, ffr, fbr, ir, out_f, out_r, n):
        nc = n // ck

        def body(step, carry):
            idf = pl.ds(pl.multiple_of(step * ck, ck), ck)
            idr = pl.ds(pl.multiple_of((nc - 1 - step) * ck, ck), ck)
            out_f[0, idf, :] = one(qr[0, idf, :], ffr[0, idf, :], ir[0, idf, :], False)
            out_r[idr, :] = one(qr[0, idr, :], fbr[0, idr, :], ir[0, idr, :], True)
            return carry

        lax.fori_loop(0, nc, body, 0, unroll=HG_UNROLL)

    st_ref[...] = jnp.zeros(st_ref.shape, F32)
    sweep(qc_ref, ffc_ref, fbc_ref, ic_ref, oc_ref, ocr_ref, n_ctx)
    sweep(q_ref, ff_ref, fb_ref, i_ref, o_ref, or_ref, n_lat)

    mean_mat = bd * (1.0 / HG_VAL_DIM)

    def readout(out, out_r, n):
        tile = min(n, 512)

        def body(step, carry):
            idx = pl.ds(pl.multiple_of(step * tile, tile), tile)
            x = out[0, idx, :] + out_r[idx, :]
            ms = jnp.dot(x * x, mean_mat, precision=HIGHEST, preferred_element_type=F32)
            out[0, idx, :] = x * lax.rsqrt(ms + NORM_EPS) * g_ref[...]
            return carry

        lax.fori_loop(0, n // tile, body, 0)

    readout(oc_ref, ocr_ref, n_ctx)
    readout(o_ref, or_ref, n_lat)


def _hgrn(q, ff, fb, iv, qc, ffc, fbc, ic, lb_terms, norm_g):
    b, n_lat, width = q.shape
    n_ctx = qc.shape[1]
    lanes = 2 * HG_KEY_DIM
    tables = _hg_tables()
    lat = pl.BlockSpec((1, n_lat, lanes), lambda i, j: (i, 0, j))
    ctx = pl.BlockSpec((1, n_ctx, lanes), lambda i, j: (i, 0, j))
    g2 = jnp.tile(norm_g, 2).reshape(1, lanes)
    return pl.pallas_call(
        functools.partial(_hgrn_kernel, n_lat=n_lat, n_ctx=n_ctx),
        grid=(b, width // lanes),
        in_specs=[lat] * 4 + [ctx] * 4 + [pl.BlockSpec((2, 3, lanes), lambda i, j: (0, 0, j)), _const_spec((1, lanes))]
                 + [_const_spec(t.shape) for t in tables],
        out_specs=[lat, ctx],
        out_shape=[jax.ShapeDtypeStruct(q.shape, F32), jax.ShapeDtypeStruct(qc.shape, F32)],
        scratch_shapes=[pltpu.VMEM((n_lat, lanes), F32), pltpu.VMEM((n_ctx, lanes), F32), pltpu.VMEM((2, lanes, lanes), F32)],
        compiler_params=_params('parallel', 'parallel'),
        name='hgrn2',
    )(q, ff, fb, iv, qc, ffc, fbc, ic, lb_terms, g2, *tables)


def _out_proj_kernel(hy_ref, da_ref, mla_ref, hg_ref, gate_ref, x_ref, g1_ref, w_ref, o_ref):
    c = hy_ref.shape[2]
    acc = jnp.dot(hy_ref[0].astype(BF16), w_ref[0:c, :], preferred_element_type=F32)
    for i, head_ref in ((1, da_ref), (2, mla_ref)):
        dv = head_ref.shape[3]
        for h in range(head_ref.shape[1]):
            lo = i * c + h * dv
            acc = acc + jnp.dot(head_ref[0, h].astype(BF16), w_ref[lo:lo + dv, :], preferred_element_type=F32)
    hg = hg_ref[0] * _silu(gate_ref[0])
    acc = acc + jnp.dot(hg.astype(BF16), w_ref[3 * c:4 * c, :], preferred_element_type=F32)
    o_ref[0] = x_ref[0] + g1_ref[0] * acc


def _out_proj(y_hy, y_da, y_mla, y_hg, gate, x, g1, w_out, tm=512):
    b, s, d = x.shape
    tm = min(tm, s)
    c = y_hy.shape[2]
    part = pl.BlockSpec((1, tm, c), lambda i, j: (i, j, 0))
    headed = lambda a: pl.BlockSpec((1, a.shape[1], tm, a.shape[3]), lambda i, j: (i, 0, j, 0))
    row = pl.BlockSpec((1, tm, d), lambda i, j: (i, j, 0))
    return pl.pallas_call(
        _out_proj_kernel,
        grid=(b, s // tm),
        in_specs=[part, headed(y_da), headed(y_mla), part, part, row, pl.BlockSpec((1, 1, d), lambda i, j: (i, 0, 0)),
                  _const_spec(w_out.shape)],
        out_specs=row,
        out_shape=jax.ShapeDtypeStruct(x.shape, F32),
        compiler_params=_params('parallel', 'parallel'),
        name='out_proj',
    )(y_hy, y_da, y_mla, y_hg, gate, x, g1, w_out)


def _router_kernel(x_ref, g_ref, sc_ref, sh_ref, wrt_ref, bias_ref, *rest, compact):
    h = _rms(x_ref[0]) * g_ref[...] * (1.0 + sc_ref[0]) + sh_ref[0]
    tm = h.shape[0]
    scores = jax.nn.sigmoid(_dot_nt(wrt_ref[...], h, precision=HIGHEST))
    choice = scores + bias_ref[...]
    per = N_EXPERTS // N_EXPERT_GROUPS
    neg = -jnp.inf
    iota_g = lax.broadcasted_iota(jnp.int32, (per, tm), 0)
    grp_rows = []
    for gi in range(N_EXPERT_GROUPS):
        blk = choice[gi * per:(gi + 1) * per]
        m1 = jnp.max(blk, axis=0, keepdims=True)
        first = jnp.min(jnp.where(blk == m1, iota_g, per), axis=0, keepdims=True)
        m2 = jnp.max(jnp.where(iota_g == first, neg, blk), axis=0, keepdims=True)
        grp_rows.append(m1 + m2)
    grp = jnp.concatenate(grp_rows, axis=0)
    iota_n = lax.broadcasted_iota(jnp.int32, (N_EXPERT_GROUPS, tm), 0)
    gsel = jnp.zeros((N_EXPERT_GROUPS, tm), F32)
    for _ in range(TOPK_GROUPS):
        m = jnp.max(grp, axis=0, keepdims=True)
        first = jnp.min(jnp.where(grp == m, iota_n, N_EXPERT_GROUPS), axis=0, keepdims=True)
        hit = iota_n == first
        gsel = jnp.where(hit, 1.0, gsel)
        grp = jnp.where(hit, neg, grp)
    emask = jnp.concatenate([jnp.broadcast_to(gsel[gi:gi + 1], (per, tm)) for gi in range(N_EXPERT_GROUPS)], axis=0)
    cand = jnp.where(emask > 0.0, choice, neg)
    iota_e = lax.broadcasted_iota(jnp.int32, (N_EXPERTS, tm), 0)
    sel = jnp.zeros((N_EXPERTS, tm), F32)
    chosen = []
    for _ in range(TOP_K):
        m = jnp.max(cand, axis=0, keepdims=True)
        first = jnp.min(jnp.where(cand == m, iota_e, N_EXPERTS), axis=0, keepdims=True)
        hit = iota_e == first
        sel = jnp.where(hit, 1.0, sel)
        cand = jnp.where(hit, neg, cand)
        chosen.append(first)
    w = scores * sel
    gate = w / jnp.sum(w, axis=0, keepdims=True) * ROUTED_SCALE
    if not compact:
        h_ref, gate_ref = rest
        h_ref[0] = h.astype(BF16)
        gate_ref[0] = gate
        return
    hp_ref, eid_ref, rank_ref, w_ref, cnt_out_ref, cnt_ref = rest
    hp_ref[0] = _pack_halves(h)

    @pl.when((pl.program_id(0) == 0) & (pl.program_id(1) == 0))
    def _():
        cnt_ref[...] = jnp.zeros(cnt_ref.shape, F32)

    src = lax.broadcasted_iota(jnp.int32, (tm, tm), 0)
    dst = lax.broadcasted_iota(jnp.int32, (tm, tm), 1)
    running = jnp.dot(sel.astype(BF16), (src <= dst).astype(BF16), preferred_element_type=F32)
    rank_dense = cnt_ref[:, 0:1] + running - 1.0
    e_rows, r_rows, w_rows = [], [], []
    for first in chosen:
        hit = iota_e == first
        e_rows.append(first)
        r_rows.append(jnp.sum(jnp.where(hit, rank_dense, 0.0), axis=0, keepdims=True))
        w_rows.append(jnp.sum(jnp.where(hit, gate, 0.0), axis=0, keepdims=True))
    eid_ref[...] = jnp.concatenate(e_rows, axis=0)
    rank_ref[...] = jnp.concatenate(r_rows, axis=0).astype(jnp.int32)
    w_ref[...] = jnp.concatenate(w_rows, axis=0)
    cnt_ref[...] = cnt_ref[...] + running[:, tm - 1:tm]
    cnt_out_ref[...] = cnt_ref[...]


def _router(x, g, scale, shift, w_router, e_bias, tm=512, compact=False):
    b, s, d = x.shape
    tm = min(tm, s)
    e = w_router.shape[1]
    row = pl.BlockSpec((1, tm, d), lambda i, j: (i, j, 0))
    mod = pl.BlockSpec((1, 1, d), lambda i, j: (i, 0, 0))
    if compact:
        nj = s // tm
        tok = pl.BlockSpec((TOP_K, tm), lambda i, j: (0, i * nj + j))
        out_specs = [pl.BlockSpec((1, tm, d // 2), lambda i, j: (i, j, 0)), tok, tok, tok, _const_spec((e, LANES))]
        out_shape = [jax.ShapeDtypeStruct((b, s, d // 2), jnp.int32), jax.ShapeDtypeStruct((TOP_K, b * s), jnp.int32),
                     jax.ShapeDtypeStruct((TOP_K, b * s), jnp.int32), jax.ShapeDtypeStruct((TOP_K, b * s), F32),
                     jax.ShapeDtypeStruct((e, LANES), F32)]
        scratch = [pltpu.VMEM((e, LANES), F32)]
        semantics = ('arbitrary', 'arbitrary')
    else:
        out_specs = [row, pl.BlockSpec((1, e, tm), lambda i, j: (i, 0, j))]
        out_shape = [jax.ShapeDtypeStruct((b, s, d), BF16), jax.ShapeDtypeStruct((b, e, s), F32)]
        scratch = []
        semantics = ('parallel', 'parallel')
    return pl.pallas_call(
        functools.partial(_router_kernel, compact=compact),
        grid=(b, s // tm),
        in_specs=[row, _const_spec((1, d)), mod, mod, _const_spec((e, d)), _const_spec((e, 1))],
        out_specs=out_specs,
        out_shape=out_shape,
        scratch_shapes=scratch,
        compiler_params=_params(*semantics),
        name='router',
    )(x, g.reshape(1, d), scale, shift, w_router.T, e_bias.reshape(e, 1))


def _moe_kernel(h_ref, x_ref, gate_ref, g2_ref, wg_ref, wu_ref, wd_ref, sg_ref, su_ref, sd_ref, *rest, final):
    if final:
        fg_ref, o_ref, acc_ref = rest
    else:
        o_ref, acc_ref = rest
    e = pl.program_id(2)
    h = h_ref[0]

    @pl.when(e == 0)
    def _():
        a = jnp.dot(h, sg_ref[...], preferred_element_type=F32)
        u = jnp.dot(h, su_ref[...], preferred_element_type=F32)
        acc_ref[...] = jnp.dot((_silu(a) * u).astype(BF16), sd_ref[...], preferred_element_type=F32)

    lane = lax.broadcasted_iota(jnp.int32, gate_ref.shape[1:], 1)
    gcol = jnp.sum(jnp.where(lane == e, gate_ref[0], 0.0), axis=-1, keepdims=True)
    a = jnp.dot(h, wg_ref[0].astype(BF16), preferred_element_type=F32)
    u = jnp.dot(h, wu_ref[0].astype(BF16), preferred_element_type=F32)
    acc_ref[...] += jnp.dot((_silu(a) * u * gcol).astype(BF16), wd_ref[0].astype(BF16), preferred_element_type=F32)

    @pl.when(e == pl.num_programs(2) - 1)
    def _():
        y = x_ref[0] + g2_ref[0] * acc_ref[...]
        if final:
            y = _rms(y) * fg_ref[...]
        o_ref[0] = y


def _moe(h2, x, gate, g2, layer, w_gate, w_up, w_down, s_gate, s_up, s_down, final_g=None, tm=1024):
    b, s, d = x.shape
    tm = min(tm, s)
    _, e, _, ff = w_gate.shape
    row = pl.BlockSpec((1, tm, d), lambda i, j, k: (i, j, 0))
    ins = [h2, x, gate, g2, w_gate, w_up, w_down, s_gate, s_up, s_down]
    in_specs = [row, row, pl.BlockSpec((1, tm, e), lambda i, j, k: (i, j, 0)),
                pl.BlockSpec((1, 1, d), lambda i, j, k: (i, 0, 0)),
                pl.BlockSpec((None, 1, d, ff), lambda i, j, k: (layer, k, 0, 0)),
                pl.BlockSpec((None, 1, d, ff), lambda i, j, k: (layer, k, 0, 0)),
                pl.BlockSpec((None, 1, ff, d), lambda i, j, k: (layer, k, 0, 0)),
                _const_spec(s_gate.shape), _const_spec(s_up.shape), _const_spec(s_down.shape)]
    if final_g is not None:
        ins.append(final_g.reshape(1, d))
        in_specs.append(_const_spec((1, d)))
    return pl.pallas_call(
        functools.partial(_moe_kernel, final=final_g is not None),
        grid=(b, s // tm, e),
        in_specs=in_specs,
        out_specs=row,
        out_shape=jax.ShapeDtypeStruct(x.shape, F32),
        scratch_shapes=[pltpu.VMEM((tm, d), F32)],
        compiler_params=_params('parallel', 'parallel', 'arbitrary'),
        name='moe',
    )(*ins)


MOE_ROW_TILE = 512
SC_ROWS = 128
V7X_SC_CORES = 2
V7X_SC_SUBCORES = 16


def _pack_halves(x):
    n = x.shape[1] // 2
    lo = pltpu.bitcast(x[:, :n].astype(BF16).astype(F32), jnp.int32)
    hi = pltpu.bitcast(x[:, n:].astype(BF16).astype(F32), jnp.int32)
    return jnp.bitwise_or(jnp.bitwise_and(hi, -65536), lax.shift_right_logical(lo, 16))


def _unpack_halves(p):
    lo = pltpu.bitcast(lax.shift_left(p, 16), F32).astype(BF16)
    hi = pltpu.bitcast(jnp.bitwise_and(p, -65536), F32).astype(BF16)
    return lo, hi


def _route_pos_kernel(off_ref, eid_ref, rank_ref, pos_ref):
    eid = eid_ref[...]
    base = jnp.zeros(eid.shape, jnp.int32)
    for e in range(N_EXPERTS):
        base = jnp.where(eid == e, off_ref[e], base)
    pos_ref[...] = base + rank_ref[...]


def _route_pos(offsets, eid, rank):
    return pl.pallas_call(
        _route_pos_kernel,
        grid=(1,),
        in_specs=[pl.BlockSpec(memory_space=pltpu.SMEM), _const_spec(eid.shape), _const_spec(rank.shape)],
        out_specs=_const_spec(eid.shape),
        out_shape=jax.ShapeDtypeStruct(eid.shape, jnp.int32),
        compiler_params=_params('arbitrary'),
        name='route_pos',
    )(offsets, eid, rank)


def _sc_mesh():
    return plsc.VectorSubcoreMesh(core_axis_name='c', subcore_axis_name='s', num_cores=V7X_SC_CORES,
                                  num_subcores=V7X_SC_SUBCORES)


def _sc_dispatch(hp, pos, n_rows):
    t, w = hp.shape
    k = pos.shape[0]
    workers = V7X_SC_CORES * V7X_SC_SUBCORES
    per_worker = t // workers
    pos_flat = pos.reshape(k * t)

    @functools.partial(pl.kernel, mesh=_sc_mesh(), out_type=jax.ShapeDtypeStruct((n_rows, w), jnp.int32),
                       scratch_types=[pltpu.VMEM((SC_ROWS,), jnp.int32), pltpu.VMEM((SC_ROWS, w), jnp.int32),
                                      pltpu.SemaphoreType.DMA])
    def scatter(hp_hbm, pos_hbm, out_hbm, idx_v, rows_v, sem):
        wid = lax.axis_index('s') * V7X_SC_CORES + lax.axis_index('c')

        @pl.loop(0, per_worker // SC_ROWS)
        def _(i):
            t0 = pl.multiple_of(wid * per_worker + i * SC_ROWS, SC_ROWS)
            pltpu.sync_copy(hp_hbm.at[pl.ds(t0, SC_ROWS)], rows_v)
            for j in range(k):
                pltpu.sync_copy(pos_hbm.at[pl.ds(pl.multiple_of(j * t + t0, SC_ROWS), SC_ROWS)], idx_v)
                pltpu.async_copy(rows_v, out_hbm.at[idx_v], sem).wait()

    return scatter(hp, pos_flat)


def _sc_collect(yp, pos):
    _, w = yp.shape
    k, t = pos.shape
    workers = V7X_SC_CORES * V7X_SC_SUBCORES
    per_worker = k * t // workers
    pos_flat = pos.reshape(k * t)

    @functools.partial(pl.kernel, mesh=_sc_mesh(), out_type=jax.ShapeDtypeStruct((k * t, w), jnp.int32),
                       scratch_types=[pltpu.VMEM((SC_ROWS,), jnp.int32), pltpu.VMEM((SC_ROWS, w), jnp.int32),
                                      pltpu.SemaphoreType.DMA])
    def gather(yp_hbm, pos_hbm, out_hbm, idx_v, rows_v, sem):
        wid = lax.axis_index('s') * V7X_SC_CORES + lax.axis_index('c')

        @pl.loop(0, per_worker // SC_ROWS)
        def _(i):
            r0 = pl.multiple_of(wid * per_worker + i * SC_ROWS, SC_ROWS)
            pltpu.sync_copy(pos_hbm.at[pl.ds(r0, SC_ROWS)], idx_v)
            pltpu.async_copy(yp_hbm.at[idx_v], rows_v, sem).wait()
            pltpu.sync_copy(rows_v, out_hbm.at[pl.ds(r0, SC_ROWS)])

    return gather(yp, pos_flat)


def _expert_kernel(te_ref, nu_ref, x_ref, wg_ref, wu_ref, wd_ref, o_ref, wg_s, wu_s, wd_s):
    i = pl.program_id(0)

    @pl.when(i < nu_ref[0])
    def _():
        @pl.when((i == 0) | (te_ref[i] != te_ref[jnp.maximum(i - 1, 0)]))
        def _():
            wg_s[...] = wg_ref[0].astype(BF16)
            wu_s[...] = wu_ref[0].astype(BF16)
            wd_s[...] = wd_ref[0].astype(BF16)

        lo, hi = _unpack_halves(x_ref[...])
        half = lo.shape[1]
        a = (jnp.dot(lo, wg_s[:half, :], preferred_element_type=F32)
             + jnp.dot(hi, wg_s[half:, :], preferred_element_type=F32))
        u = (jnp.dot(lo, wu_s[:half, :], preferred_element_type=F32)
             + jnp.dot(hi, wu_s[half:, :], preferred_element_type=F32))
        y = jnp.dot((_silu(a) * u).astype(BF16), wd_s[...], preferred_element_type=F32)
        o_ref[...] = _pack_halves(y)


def _experts(xp, tile_expert, n_used, layer, w_gate, w_up, w_down):
    n_rows, half = xp.shape
    _, _, d, ff = w_gate.shape
    r = MOE_ROW_TILE
    row = pl.BlockSpec((r, half), lambda i, te, nu: (i, 0))
    grid_spec = pltpu.PrefetchScalarGridSpec(
        num_scalar_prefetch=2,
        grid=(n_rows // r,),
        in_specs=[row,
                  pl.BlockSpec((None, 1, d, ff), lambda i, te, nu: (layer, te[i], 0, 0)),
                  pl.BlockSpec((None, 1, d, ff), lambda i, te, nu: (layer, te[i], 0, 0)),
                  pl.BlockSpec((None, 1, ff, d), lambda i, te, nu: (layer, te[i], 0, 0))],
        out_specs=row,
        scratch_shapes=[pltpu.VMEM((d, ff), BF16), pltpu.VMEM((d, ff), BF16), pltpu.VMEM((ff, d), BF16)],
    )
    return pl.pallas_call(
        _expert_kernel,
        grid_spec=grid_spec,
        out_shape=jax.ShapeDtypeStruct((n_rows, half), jnp.int32),
        compiler_params=_params('arbitrary'),
        name='experts',
    )(tile_expert, n_used, xp, w_gate, w_up, w_down)


def _combine_kernel(yg_ref, w_ref, hp_ref, x_ref, g2_ref, sg_ref, su_ref, sd_ref, *rest, final):
    if final:
        fg_ref, o_ref = rest
    else:
        (o_ref,) = rest
    half = hp_ref.shape[2]
    lo, hi = _unpack_halves(hp_ref[0])
    sg, su = sg_ref[...], su_ref[...]
    a = jnp.dot(lo, sg[:half], preferred_element_type=F32) + jnp.dot(hi, sg[half:], preferred_element_type=F32)
    u = jnp.dot(lo, su[:half], preferred_element_type=F32) + jnp.dot(hi, su[half:], preferred_element_type=F32)
    acc = jnp.dot((_silu(a) * u).astype(BF16), sd_ref[...], preferred_element_type=F32)
    acc_lo, acc_hi = acc[:, :half], acc[:, half:]
    wts = w_ref[0]
    for k in range(yg_ref.shape[0]):
        ylo, yhi = _unpack_halves(yg_ref[k, 0])
        wk = wts[:, k:k + 1]
        acc_lo = acc_lo + wk * ylo.astype(F32)
        acc_hi = acc_hi + wk * yhi.astype(F32)
    y = x_ref[0] + g2_ref[0] * jnp.concatenate([acc_lo, acc_hi], axis=1)
    if final:
        y = _rms(y) * fg_ref[...]
    o_ref[0] = y


def _combine(yg, wts, hp, x, g2, s_gate, s_up, s_down, final_g=None, tm=256):
    b, s, d = x.shape
    k = yg.shape[0]
    half = d // 2
    row = pl.BlockSpec((1, tm, d), lambda i, j: (i, j, 0))
    prow = pl.BlockSpec((1, tm, half), lambda i, j: (i, j, 0))
    ins = [yg, wts, hp, x, g2, s_gate, s_up, s_down]
    in_specs = [pl.BlockSpec((k, 1, tm, half), lambda i, j: (0, i, j, 0)), pl.BlockSpec((1, tm, k), lambda i, j: (i, j, 0)),
                prow, row, pl.BlockSpec((1, 1, d), lambda i, j: (i, 0, 0)),
                _const_spec(s_gate.shape), _const_spec(s_up.shape), _const_spec(s_down.shape)]
    if final_g is not None:
        ins.append(final_g.reshape(1, d))
        in_specs.append(_const_spec((1, d)))
    return pl.pallas_call(
        functools.partial(_combine_kernel, final=final_g is not None),
        grid=(b, s // tm),
        in_specs=in_specs,
        out_specs=row,
        out_shape=jax.ShapeDtypeStruct(x.shape, F32),
        compiler_params=_params('parallel', 'parallel'),
        name='moe_combine',
    )(*ins)


def _routed_moe(x, g, scale, shift, g2, w_router, e_bias, layer, w_gate, w_up, w_down, s_gate, s_up, s_down, final_g=None):
    b, s, d = x.shape
    t = b * s
    hp, eid, rank, wts, counts = _router(x, g, scale, shift, w_router, e_bias, compact=True)
    counts = counts[:, 0].astype(jnp.int32)
    r = MOE_ROW_TILE
    padded = (counts + (r - 1)) // r * r
    ends = jnp.cumsum(padded)
    offsets = ends - padded
    n_rows = t * TOP_K + N_EXPERTS * r
    tile_start = jnp.arange(n_rows // r, dtype=jnp.int32) * r
    tile_expert = jnp.minimum(jnp.sum((tile_start[:, None] >= ends[None, :]).astype(jnp.int32), axis=1), N_EXPERTS - 1)
    n_used = (ends[-1] // r).reshape(1).astype(jnp.int32)
    pos = _route_pos(offsets.astype(jnp.int32), eid, rank)
    xp = _sc_dispatch(hp.reshape(t, d // 2), pos, n_rows)
    yp = _experts(xp, tile_expert.astype(jnp.int32), n_used, layer, w_gate, w_up, w_down)
    yg = _sc_collect(yp, pos).reshape(TOP_K, b, s, d // 2)
    return _combine(yg, wts.T.reshape(b, s, TOP_K), hp, x, g2, s_gate, s_up, s_down, final_g)


def _mixers(p, pc, ctx_out, prm, l, lam_init, rope_tabs, lb_terms):
    s = p['hy_v'].shape[1]
    sc = pc['hy_v'].shape[1]

    hy_args = (prm['hy_w1'][l], prm['hy_b1'][l], prm['hy_w2'][l], prm['hy_b2'][l], prm['hy_w3'][l], prm['hy_b3'][l],
               prm['hy_sin_freq'][l], prm['hy_decay'][l])
    y_hy = _hyena([p['hy_v'], p['hy_x1'], p['hy_x2']], prm['hy_conv_w'][l], prm['hy_conv_b'][l],
                  _hy_filters(s, *hy_args), prm['hy_bias'][l], inner=128)
    yc_hy = None
    if ctx_out:
        yc_hy = _hyena([pc['hy_v'], pc['hy_x1'], pc['hy_x2']], prm['hy_conv_w'][l], prm['hy_conv_b'][l],
                       _hy_filters(sc, *hy_args), prm['hy_bias'][l], inner=32)

    lp = prm['da_lambda'][l].astype(F32)
    lam = jnp.exp(jnp.sum(lp[0] * lp[1])) - jnp.exp(jnp.sum(lp[2] * lp[3])) + lam_init
    da_kw = dict(heads=DA_HEADS, ncomp=2, scale=DA_HEAD_DIM ** -0.5, lam=lam, subln_g=prm['da_subln_g'][l],
                 post_scale=1.0 - lam_init)
    da_ctx = ([pc['da_k']], pc['da_v'])
    y_da = _attention([p['da_q']], [da_ctx, ([p['da_k']], p['da_v'])], **da_kw)
    yc_da = _attention([pc['da_q']], [da_ctx], **da_kw) if ctx_out else None

    wq = prm['mla_w_q_up'][l].reshape(MLA_Q_RANK, MLA_HEADS, MLA_NOPE_DIM + MLA_ROPE_DIM)
    wq_n = wq[:, :, :MLA_NOPE_DIM].reshape(MLA_Q_RANK, -1).astype(BF16)
    wq_r = wq[:, :, MLA_NOPE_DIM:].reshape(MLA_Q_RANK, -1).astype(BF16)
    wkv = prm['mla_w_kv_up'][l].reshape(MLA_KV_RANK, MLA_HEADS, MLA_NOPE_DIM + MLA_V_DIM)
    wkv_n = wkv[:, :, :MLA_NOPE_DIM].reshape(MLA_KV_RANK, -1).astype(BF16)
    wkv_v = wkv[:, :, MLA_NOPE_DIM:].reshape(MLA_KV_RANK, -1).astype(BF16)

    def queries(qd, tabs):
        return _norm_proj(qd, prm['mla_q_norm_g'][l], [(wq_n, F32, False, MLA_HEADS), (wq_r, F32, True, MLA_HEADS)],
                          rope_tabs=tabs)

    def keys_values(kvd):
        return _norm_proj(kvd, prm['mla_kv_norm_g'][l], [(wkv_n, BF16, False, MLA_HEADS), (wkv_v, BF16, False, MLA_HEADS)])

    kn_l, v_l = keys_values(p['mla_kv'])
    kn_c, v_c = keys_values(pc['mla_kv'])
    mla_kw = dict(heads=MLA_HEADS, ncomp=1, scale=(MLA_NOPE_DIM + MLA_ROPE_DIM) ** -0.5)
    mla_ctx = ([kn_c, pc['mla_kr']], v_c)
    y_mla = _attention(queries(p['mla_q'], rope_tabs), [mla_ctx, ([kn_l, p['mla_kr']], v_l)], **mla_kw)
    yc_mla = _attention(queries(pc['mla_q'], None), [mla_ctx], **mla_kw) if ctx_out else None

    o, oc = _hgrn(p['hg_q'], p['hg_ff'], p['hg_fb'], p['hg_i'], pc['hg_q'], pc['hg_ff'], pc['hg_fb'], pc['hg_i'],
                  lb_terms, prm['hg_norm_g'][l])
    return (y_hy, y_da, y_mla, o), (yc_hy, yc_da, yc_mla, oc)


def kernel(x, c, ctx, c_ctx, w_ada, b_ada, norm1_g, norm2_g, w_in, w_out, hy_conv_w, hy_conv_b, hy_w1, hy_b1, hy_w2, hy_b2, hy_w3, hy_b3, hy_sin_freq, hy_decay, hy_bias, da_lambda, da_subln_g, mla_q_norm_g, mla_w_q_up, mla_kv_norm_g, mla_w_kv_up, hg_lower_bounds, hg_norm_g, moe_w_router, moe_bias, moe_w_gate, moe_w_up, moe_w_down, moe_sh_gate, moe_sh_up, moe_sh_down, final_norm_g):
    prm = dict(hy_conv_w=hy_conv_w, hy_conv_b=hy_conv_b, hy_w1=hy_w1, hy_b1=hy_b1, hy_w2=hy_w2, hy_b2=hy_b2,
               hy_w3=hy_w3, hy_b3=hy_b3, hy_sin_freq=hy_sin_freq, hy_decay=hy_decay, hy_bias=hy_bias,
               da_lambda=da_lambda, da_subln_g=da_subln_g, mla_q_norm_g=mla_q_norm_g, mla_w_q_up=mla_w_q_up,
               mla_kv_norm_g=mla_kv_norm_g, mla_w_kv_up=mla_w_kv_up, hg_norm_g=hg_norm_g)
    b, n_lat, d = x.shape
    depth = w_in.shape[0]
    rows = n_lat // GRID_W
    row_pos = jnp.repeat(jnp.arange(rows, dtype=jnp.int32), GRID_W)
    col_pos = jnp.tile(jnp.arange(GRID_W, dtype=jnp.int32), rows)
    rope_tabs = _rope_tables(row_pos, col_pos, 2 * DA_HEADS * DA_HEAD_DIM)
    lbs = jnp.cumsum(jax.nn.softmax(hg_lower_bounds.astype(F32), axis=1), axis=1)
    lbs = lbs - lbs[:, :1]
    cond = jnp.concatenate([c, c_ctx[None], jnp.zeros((8 - b - 1, d), F32)], axis=0)

    for l in range(depth):
        ctx_out = l < depth - 1
        mods = _ada(cond, w_ada[l], b_ada[l])
        sh1, sc1, g1, sh2, sc2, g2 = [m[:, None, :] for m in jnp.split(mods[:b], 6, axis=-1)]
        mc = [jnp.broadcast_to(m[:, None, :], (b, 1, d)) for m in jnp.split(mods[b:b + 1], 6, axis=-1)]

        off = 0
        outs = []
        for _, wdt, dt, rope, split, rep in _SEGMENTS:
            w = w_in[l][:, off:off + wdt].astype(BF16)
            outs.append((jnp.tile(w, (1, rep)) if rep > 1 else w, dt, rope, split))
            off += wdt
        names = [seg[0] for seg in _SEGMENTS]
        p = dict(zip(names, _norm_proj(x, norm1_g[l], outs, sc1, sh1, rope_tabs=rope_tabs)))
        pc = dict(zip(names, _norm_proj(ctx, norm1_g[l], outs, mc[1], mc[0])))

        lb = lbs[:, l]
        lb_terms = jnp.stack([jnp.log(lb), jnp.log1p(-lb), 1.0 - lb], axis=1)
        lam_init = 0.8 - 0.6 * math.exp(-0.3 * l)
        lat_parts, ctx_parts = _mixers(p, pc, ctx_out, prm, l, lam_init, rope_tabs, lb_terms)

        w_out_b = w_out[l].astype(BF16)
        moe_w = (l, moe_w_gate, moe_w_up, moe_w_down,
                 moe_sh_gate[l].astype(BF16), moe_sh_up[l].astype(BF16), moe_sh_down[l].astype(BF16))

        if ctx_out:
            ctx = _out_proj(*ctx_parts, pc['hg_g'], ctx, mc[2], w_out_b)
            flat = ctx.reshape(1, -1, d)
            h2c, gate_c = _router(flat, norm2_g[l], mc[4][:1], mc[3][:1], moe_w_router[l], moe_bias[l])
            ctx = _moe(h2c, flat, gate_c.transpose(0, 2, 1), mc[5][:1], *moe_w).reshape(ctx.shape)

        x = _out_proj(*lat_parts, p['hg_g'], x, g1, w_out_b)
        x = _routed_moe(x, norm2_g[l], sc2, sh2, g2, moe_w_router[l], moe_bias[l], *moe_w,
                        final_g=None if ctx_out else final_norm_g)

    return x
```

```python
import functools
import math

import numpy as np
import jax
import jax.numpy as jnp
from jax import lax
from jax.experimental import pallas as pl
from jax.experimental.pallas import tpu as pltpu
from jax.experimental.pallas import tpu_sc as plsc

F32 = jnp.float32
BF16 = jnp.bfloat16
HIGHEST = lax.Precision.HIGHEST

D_MODEL = 1024
GRID_W = 64
HY_WIDTH = 256
HY_ORDER = 2
HY_BANDS = 16
DA_HEADS = 4
DA_HEAD_DIM = 32
MLA_HEADS = 4
MLA_Q_RANK = 192
MLA_KV_RANK = 128
MLA_NOPE_DIM = 64
MLA_ROPE_DIM = 32
MLA_V_DIM = 64
HG_HEADS = 4
HG_KEY_DIM = 64
HG_VAL_DIM = 64
HG_CHUNK = 64
HG_SUB = 8
HG_UNROLL = 4
N_EXPERTS = 64
N_EXPERT_GROUPS = 8
TOPK_GROUPS = 4
TOP_K = 8
EXPERT_FF = 256
ROUTED_SCALE = 2.5
ROPE_BASE = 10000.0
NORM_EPS = 1e-6

V7X_VMEM_LIMIT_BYTES = 56 * 1024 * 1024
LANES = 128

_SEGMENTS = (
    ('hy_v', HY_WIDTH, F32, False, 0, 1), ('hy_x1', HY_WIDTH, F32, False, 0, 1), ('hy_x2', HY_WIDTH, F32, False, 0, 1),
    ('da_q', 2 * DA_HEADS * DA_HEAD_DIM, F32, True, 2 * DA_HEADS, 1),
    ('da_k', 2 * DA_HEADS * DA_HEAD_DIM, BF16, True, 2 * DA_HEADS, 1),
    ('da_v', 2 * DA_HEADS * DA_HEAD_DIM, BF16, False, DA_HEADS, 1),
    ('mla_q', MLA_Q_RANK, F32, False, 0, 1), ('mla_kv', MLA_KV_RANK, F32, False, 0, 1),
    ('mla_kr', MLA_ROPE_DIM, BF16, True, MLA_HEADS, MLA_HEADS),
    ('hg_q', HG_HEADS * HG_KEY_DIM, F32, False, 0, 1), ('hg_ff', HG_HEADS * HG_KEY_DIM, F32, False, 0, 1),
    ('hg_fb', HG_HEADS * HG_KEY_DIM, F32, False, 0, 1), ('hg_i', HG_HEADS * HG_VAL_DIM, F32, False, 0, 1),
    ('hg_g', HG_HEADS * HG_VAL_DIM, F32, False, 0, 1),
)


def _params(*semantics):
    return pltpu.CompilerParams(dimension_semantics=semantics, vmem_limit_bytes=V7X_VMEM_LIMIT_BYTES)


def _const_spec(shape):
    nd = len(shape)
    return pl.BlockSpec(shape, lambda *_: (0,) * nd)


def _rms(x, eps=NORM_EPS):
    return x * lax.rsqrt(jnp.mean(x * x, axis=-1, keepdims=True) + eps)


def _silu(x):
    return x * jax.nn.sigmoid(x)


def _dot_nt(a, b, **kw):
    return lax.dot_general(a, b, (((1,), (1,)), ((), ())), preferred_element_type=F32, **kw)


def _ada_kernel(c_ref, w_ref, b_ref, o_ref):
    s = _silu(c_ref[...])
    o_ref[...] = jnp.dot(s, w_ref[...], precision=HIGHEST, preferred_element_type=F32) + b_ref[...]


def _ada(cond, w, b):
    r, d = cond.shape
    n = w.shape[1]
    tn = 1536
    return pl.pallas_call(
        _ada_kernel,
        grid=(n // tn,),
        in_specs=[_const_spec((r, d)), pl.BlockSpec((d, tn), lambda j: (0, j)), pl.BlockSpec((1, tn), lambda j: (0, j))],
        out_specs=pl.BlockSpec((r, tn), lambda j: (0, j)),
        out_shape=jax.ShapeDtypeStruct((r, n), F32),
        compiler_params=_params('arbitrary'),
        name='ada',
    )(cond, w, b.reshape(1, n))


ROPE_UNIT = 32


def _rope_tables(row, col, width):
    n = ROPE_UNIT // 4
    inv = ROPE_BASE ** (-jnp.arange(n, dtype=F32) / n)
    units = width // ROPE_UNIT
    parts_c, parts_a, parts_b = [], [], []
    zero = jnp.zeros((row.shape[0], n), F32)
    for pos in (row, col):
        ang = pos.astype(F32)[:, None] * inv
        cos, sin = jnp.cos(ang), jnp.sin(ang)
        parts_c += [cos, cos]
        parts_a += [zero, sin]
        parts_b += [-sin, zero]
    tile = lambda ps: jnp.tile(jnp.concatenate(ps, axis=1), (1, units))
    return tile(parts_c), tile(parts_a), tile(parts_b)


def _norm_proj_kernel(*refs, n_w, modulate, ropes, splits):
    x_ref, g_ref = refs[0], refs[1]
    pos = 2
    if modulate:
        sc_ref, sh_ref = refs[2], refs[3]
        pos = 4
    if any(ropes):
        rc_ref, ra_ref, rb_ref = refs[pos:pos + 3]
        pos += 3
    w_refs = refs[pos:pos + n_w]
    o_refs = refs[pos + n_w:]
    y = _rms(x_ref[0]) * g_ref[...]
    if modulate:
        y = y * (1.0 + sc_ref[0]) + sh_ref[0]
    yb = y.astype(BF16)
    for w_ref, o_ref, rope, split in zip(w_refs, o_refs, ropes, splits):
        o = jnp.dot(yb, w_ref[...], preferred_element_type=F32)
        if rope:
            wd = o.shape[1]
            shift = ROPE_UNIT // 4
            o = (o * rc_ref[:, :wd] + pltpu.roll(o, shift, axis=1) * ra_ref[:, :wd]
                 + pltpu.roll(o, wd - shift, axis=1) * rb_ref[:, :wd])
        if split:
            unit = o.shape[1] // split
            for u in range(split):
                o_ref[0, u] = o[:, u * unit:(u + 1) * unit].astype(o_ref.dtype)
        else:
            o_ref[0] = o.astype(o_ref.dtype)


def _norm_proj(x, g, outs, scale=None, shift=None, rope_tabs=None, tm=512):
    b, s, k = x.shape
    tm = min(tm, s)
    modulate = scale is not None
    ropes = tuple(bool(o[2]) and rope_tabs is not None for o in outs)
    splits = tuple(o[3] for o in outs)
    ins = [x, g.reshape(1, k)]
    in_specs = [pl.BlockSpec((1, tm, k), lambda i, j: (i, j, 0)), _const_spec((1, k))]
    if modulate:
        ins += [scale, shift]
        in_specs += [pl.BlockSpec((1, 1, k), lambda i, j: (i, 0, 0))] * 2
    if any(ropes):
        ins += list(rope_tabs)
        in_specs += [pl.BlockSpec((tm, rope_tabs[0].shape[1]), lambda i, j: (j, 0))] * 3
    out_specs, out_shape = [], []
    for w, dt, _, split in outs:
        ins.append(w)
        in_specs.append(_const_spec(w.shape))
        n = w.shape[1]
        if split:
            out_specs.append(pl.BlockSpec((1, split, tm, n // split), lambda i, j: (i, 0, j, 0)))
            out_shape.append(jax.ShapeDtypeStruct((b, split, s, n // split), dt))
        else:
            out_specs.append(pl.BlockSpec((1, tm, n), lambda i, j: (i, j, 0)))
            out_shape.append(jax.ShapeDtypeStruct((b, s, n), dt))
    return pl.pallas_call(
        functools.partial(_norm_proj_kernel, n_w=len(outs), modulate=modulate, ropes=ropes, splits=splits),
        grid=(b, s // tm),
        in_specs=in_specs,
        out_specs=out_specs,
        out_shape=out_shape,
        compiler_params=_params('parallel', 'parallel'),
        name='norm_proj',
    )(*ins)


def _hy_filter_kernel(w1t_ref, w1s_ref, w1c_ref, b1_ref, w2_ref, b2_ref, w3_ref, b3_ref, fr_ref, dec_ref, o_ref, *, n):
    t = lax.broadcasted_iota(jnp.int32, (n, 1), 0).astype(F32) / n
    bands = lax.broadcasted_iota(jnp.int32, (1, HY_BANDS), 1).astype(F32) + 1.0
    ang = (2.0 * jnp.pi) * t * bands
    pre = (t * w1t_ref[...]
           + jnp.dot(jnp.sin(ang), w1s_ref[...], precision=HIGHEST, preferred_element_type=F32)
           + jnp.dot(jnp.cos(ang), w1c_ref[...], precision=HIGHEST, preferred_element_type=F32)
           + b1_ref[...])
    hid = jnp.sin(fr_ref[0:1, :] * pre)
    hid = jnp.sin(fr_ref[1:2, :] * (jnp.dot(hid, w2_ref[...], precision=HIGHEST, preferred_element_type=F32) + b2_ref[...]))
    filt = jnp.dot(hid, w3_ref[...], precision=HIGHEST, preferred_element_type=F32) + b3_ref[...]
    filt = filt * jnp.exp(-t * jnp.abs(dec_ref[...]))
    col = jnp.sum(jnp.abs(filt), axis=0, keepdims=True) - jnp.abs(filt[0:1, :])
    w = HY_WIDTH
    for o in range(HY_ORDER):
        lo = o * 2 * w
        f0 = filt[0:1, lo:lo + w] + filt[0:1, lo + w:lo + 2 * w]
        inv = 1.0 / (col[:, lo:lo + w] + col[:, lo + w:lo + 2 * w] + jnp.abs(f0))
        o_ref[:, lo:lo + w] = filt[:, lo:lo + w] * inv
        o_ref[:, lo + w:lo + 2 * w] = filt[:, lo + w:lo + 2 * w] * inv


def _hy_filters(n, w1, b1, w2, b2, w3, b3, freq, decay):
    cols = w3.shape[1]
    ins = [w1[0:1], w1[1:1 + HY_BANDS], w1[1 + HY_BANDS:], b1.reshape(1, -1), w2, b2.reshape(1, -1), w3,
           b3.reshape(1, -1), freq, decay.reshape(1, -1)]
    out = pl.pallas_call(
        functools.partial(_hy_filter_kernel, n=n),
        grid=(1,),
        in_specs=[_const_spec(a.shape) for a in ins],
        out_specs=_const_spec((n, cols)),
        out_shape=jax.ShapeDtypeStruct((n, cols), F32),
        compiler_params=_params('arbitrary'),
        name='hy_filter',
    )(*ins)
    return out.reshape(n, HY_ORDER, 2, HY_WIDTH)


def _two_sided(filt_n):
    n = filt_n.shape[0]
    hf, hb = filt_n[:, :, 0], filt_n[:, :, 1]
    h = jnp.concatenate([hf[:1] + hb[:1], hf[1:], jnp.zeros((1,) + hf.shape[1:], F32), hb[:0:-1]], axis=0)
    return h.reshape(2 * n, HY_ORDER * HY_WIDTH)


def _short_conv_kernel(*refs, s):
    x_refs, w_refs, b_refs, o_refs = refs[0:3], refs[3:6], refs[6:9], refs[9:12]
    row = lax.broadcasted_iota(jnp.int32, (s, 1), 0)
    for x_ref, w_ref, b_ref, o_ref in zip(x_refs, w_refs, b_refs, o_refs):
        x = x_ref[0]
        prev = jnp.where(row == 0, 0.0, pltpu.roll(x, 1, axis=0))
        nxt = jnp.where(row == s - 1, 0.0, pltpu.roll(x, s - 1, axis=0))
        o_ref[0] = prev * w_ref[0:1, :] + x * w_ref[1:2, :] + nxt * w_ref[2:3, :] + b_ref[...]


def _short_conv(parts, conv_w, conv_b):
    b, s, c = parts[0].shape
    tc = LANES
    ws = [conv_w[:, i * c:(i + 1) * c] for i in range(3)]
    bs = [conv_b[i * c:(i + 1) * c].reshape(1, c) for i in range(3)]
    xspec = pl.BlockSpec((1, s, tc), lambda i, j: (i, 0, j))
    return pl.pallas_call(
        functools.partial(_short_conv_kernel, s=s),
        grid=(b, c // tc),
        in_specs=[xspec] * 3 + [pl.BlockSpec((3, tc), lambda i, j: (0, j))] * 3 + [pl.BlockSpec((1, tc), lambda i, j: (0, j))] * 3,
        out_specs=[xspec] * 3,
        out_shape=[jax.ShapeDtypeStruct((b, s, c), F32)] * 3,
        compiler_params=_params('parallel', 'parallel'),
        name='short_conv',
    )(*parts, *ws, *bs)


def _dft_cos_sin(rows, cols, period):
    ang = 2.0 * np.pi * ((np.arange(rows)[:, None] * np.arange(cols)[None, :]) % period) / period
    return np.cos(ang), np.sin(ang)


def _fft_tables(n, inner):
    big = 2 * n
    n1 = big // inner
    c1, s1 = _dft_cos_sin(n1, n1, n1)
    h = n1 // 2
    outer_data = np.block([[c1[:, :h], s1[:, :h]], [-s1[:, :h], c1[:, :h]]])
    outer_real = np.concatenate([c1, -s1], axis=0)
    outer_inv = np.block([[c1[:h, :], -s1[:h, :]], [s1[:h, :], c1[:h, :]]]) / big
    c2, s2 = _dft_cos_sin(inner, inner, inner)
    inner_fwd = np.block([[c2, s2], [-s2, c2]])
    inner_inv = np.block([[c2, -s2], [s2, c2]])
    ct, st = _dft_cos_sin(n1, inner, big)
    f = lambda a: jnp.asarray(a, F32)
    return dict(n1=n1, inner=inner, outer_data=_hi_lo_cols(outer_data), outer_real=_hi_lo_cols(outer_real),
                outer_inv=_hi_lo_cols(outer_inv),
                inner_fwd=_hi_lo_cols(inner_fwd), inner_inv=_hi_lo_cols(inner_inv),
                tw_cos=f(ct).reshape(n1, inner, 1), tw_sin=f(st).reshape(n1, inner, 1))


def _left_mm_kernel(m_ref, x_ref, o_ref):
    o_ref[0] = jnp.dot(m_ref[...], _hi_lo_rows(x_ref[0]), preferred_element_type=F32)


def _left_mm(m, x, tl=4096):
    p, k, l = x.shape
    mm = m.shape[0]
    tl = min(tl, l)
    return pl.pallas_call(
        _left_mm_kernel,
        grid=(p, l // tl),
        in_specs=[_const_spec(m.shape), pl.BlockSpec((1, k, tl), lambda i, j: (i, 0, j))],
        out_specs=pl.BlockSpec((1, mm, tl), lambda i, j: (i, 0, j)),
        out_shape=jax.ShapeDtypeStruct((p, mm, l), F32),
        compiler_params=_params('parallel', 'parallel'),
        name='fft_outer',
    )(m, x)


def _hi_lo_cols(m):
    m = np.asarray(m, np.float32)
    hi = m.astype(BF16)
    lo = (m - hi.astype(np.float32)).astype(BF16)
    return jnp.asarray(np.concatenate([hi, hi, lo], axis=1))


def _hi_lo_rows(x):
    hi = x.astype(BF16)
    lo = (x - hi.astype(F32)).astype(BF16)
    return jnp.concatenate([hi, lo, hi], axis=0)


def _inner_kernel(a_ref, twc_ref, tws_ref, gf_ref, *rest, convolve, inner, kb):
    for s in range(kb):
        ar, ai = a_ref[0, 0, s], a_ref[0, 1, s]
        tc, ts = twc_ref[s], tws_ref[s]
        br = ar * tc + ai * ts
        bi = ai * tc - ar * ts
        x = jnp.dot(gf_ref[...], _hi_lo_rows(jnp.concatenate([br, bi], axis=0)), preferred_element_type=F32)
        if not convolve:
            o_ref = rest[0]
            o_ref[0, 0, s] = x[:inner]
            o_ref[0, 1, s] = x[inner:]
            continue
        h_ref, gi_ref, o_ref = rest
        xr, xi = x[:inner], x[inner:]
        hr, hi = h_ref[0, 0, s], h_ref[0, 1, s]
        yr = xr * hr - xi * hi
        yi = xr * hi + xi * hr
        z = jnp.dot(gi_ref[...], _hi_lo_rows(jnp.concatenate([yr, yi], axis=0)), preferred_element_type=F32)
        zr, zi = z[:inner], z[inner:]
        o_ref[0, 0, s] = zr * tc - zi * ts
        o_ref[0, 1, s] = zi * tc + zr * ts


def _fft_inner(a, tab, c, h=None, h_block=0):
    p = a.shape[0]
    n1, inner = tab['n1'], tab['inner']
    a5 = a.reshape(p, 2, n1, inner, c)
    tc = 2 * LANES
    kb = 4
    blk = pl.BlockSpec((1, 2, kb, inner, tc), lambda i, k, j: (i, 0, k, 0, j))
    tw_spec = pl.BlockSpec((kb, inner, 1), lambda i, k, j: (k, 0, 0))
    ins = [a5, tab['tw_cos'], tab['tw_sin'], tab['inner_fwd']]
    in_specs = [blk, tw_spec, tw_spec, _const_spec(tab['inner_fwd'].shape)]
    if h is not None:
        ch = h.shape[-1] // inner
        nb = c // tc
        ins += [h.reshape(1, 2, n1, inner, ch), tab['inner_inv']]
        in_specs += [pl.BlockSpec((1, 2, kb, inner, tc), lambda i, k, j: (0, 0, k, 0, h_block * nb + j)),
                     _const_spec(tab['inner_inv'].shape)]
    out = pl.pallas_call(
        functools.partial(_inner_kernel, convolve=h is not None, inner=inner, kb=kb),
        grid=(p, n1 // kb, c // tc),
        in_specs=in_specs,
        out_specs=blk,
        out_shape=jax.ShapeDtypeStruct(a5.shape, F32),
        compiler_params=_params('parallel', 'parallel', 'parallel'),
        name='fft_inner',
    )(*ins)
    return out.reshape(p, 2 * n1, inner * c)


def _gate_kernel(m_ref, z_ref, u_ref, x_ref, bias_ref, *rest, chain):
    y = jnp.dot(m_ref[...], _hi_lo_rows(z_ref[0]), preferred_element_type=F32)
    nxt = x_ref[0] * (y + u_ref[0] * bias_ref[...])
    if chain:
        mf_ref, o_ref, a_ref = rest
        o_ref[0] = nxt
        a_ref[0] = jnp.dot(mf_ref[...], _hi_lo_rows(nxt), preferred_element_type=F32)
    else:
        rest[0][0] = nxt


def _fft_gate(tab, z, u, x, bias_l, chain, tl=4096):
    p, k2, l = z.shape
    n1 = tab['n1']
    tl = min(tl, l)
    row = pl.BlockSpec((1, n1, tl), lambda i, j: (i, 0, j))
    ins = [tab['outer_inv'], z, u, x, bias_l]
    in_specs = [_const_spec(tab['outer_inv'].shape), pl.BlockSpec((1, k2, tl), lambda i, j: (i, 0, j)), row, row,
                pl.BlockSpec((1, tl), lambda i, j: (0, j))]
    out_specs = [row]
    out_shape = [jax.ShapeDtypeStruct((p, n1, l), F32)]
    if chain:
        ins.append(tab['outer_data'])
        in_specs.append(_const_spec(tab['outer_data'].shape))
        out_specs.append(pl.BlockSpec((1, k2, tl), lambda i, j: (i, 0, j)))
        out_shape.append(jax.ShapeDtypeStruct((p, k2, l), F32))
    return pl.pallas_call(
        functools.partial(_gate_kernel, chain=chain),
        grid=(p, l // tl),
        in_specs=in_specs,
        out_specs=out_specs,
        out_shape=out_shape,
        compiler_params=_params('parallel', 'parallel'),
        name='fft_gate',
    )(*ins)


def _hyena(parts, conv_w, conv_b, filt_n, bias, inner):
    b, s, c = parts[0].shape
    tab = _fft_tables(s, inner)
    n1 = tab['n1']
    lanes = inner * c
    h_taps = _two_sided(filt_n).reshape(1, n1, inner * HY_ORDER * c)
    h_spec = _fft_inner(_left_mm(tab['outer_real'], h_taps), tab, HY_ORDER * c)
    v, x1, x2 = [a.reshape(b // 2, n1, lanes) for a in _short_conv(parts, conv_w, conv_b)]
    bias_l = [jnp.tile(bias[o], inner).reshape(1, lanes) for o in range(HY_ORDER)]
    a = _left_mm(tab['outer_data'], v)
    z = _fft_inner(a, tab, c, h_spec, 0)
    z2, a = _fft_gate(tab, z, v, x1, bias_l[0], chain=True)
    z = _fft_inner(a, tab, c, h_spec, 1)
    (z3,) = _fft_gate(tab, z, z2, x2, bias_l[1], chain=False)
    return z3.reshape(b, s, c)


def _attn_kernel(*refs, n_q, n_pieces, ncomp, scale, post_scale):
    q_refs = refs[:n_q]
    pos = n_q
    pieces = []
    for _ in range(n_pieces):
        pieces.append((refs[pos:pos + n_q], refs[pos + n_q]))
        pos += n_q + 1
    if ncomp == 2:
        lam_ref, g_ref = refs[pos:pos + 2]
        pos += 2
    o_ref, kt_ref, vcat_ref = refs[pos:pos + 3]
    dv = o_ref.shape[3]
    tq = o_ref.shape[2]
    dpad = kt_ref.shape[1]
    width = sum(q_ref.shape[3] for q_ref in q_refs)
    step = 512

    @pl.when(pl.program_id(2) == 0)
    def _():
        row = 0
        for k_refs, v_ref in pieces:
            n = v_ref.shape[2]
            for c in range(ncomp):
                for lo in range(0, n, step):
                    hi = min(lo + step, n)
                    parts = [k_ref[0, c if k_ref.shape[1] == ncomp else 0, lo:hi, :].astype(F32) for k_ref in k_refs]
                    kp = jnp.concatenate(parts + [jnp.zeros((hi - lo, dpad - width), F32)], axis=1)
                    kt_ref[c, :, row + lo:row + hi] = kp.T.astype(BF16)
            vcat_ref[row:row + n, :dv] = v_ref[0, 0]
            vcat_ref[row:row + n, dv:] = jnp.ones((n, dv), BF16)
            row += n

    outs = []
    for c in range(ncomp):
        q = jnp.concatenate([q_ref[0, c] for q_ref in q_refs] + [jnp.zeros((tq, dpad - width), F32)], axis=1)
        s = jnp.dot((q * (scale * math.log2(math.e))).astype(BF16), kt_ref[c], preferred_element_type=F32)
        m = jnp.max(s, axis=-1, keepdims=True)
        p = jnp.exp2((s - m).astype(BF16))
        ol = jnp.dot(p, vcat_ref[...], preferred_element_type=F32)
        outs.append(ol[:, :dv] / ol[:, dv:dv + 1])
    if ncomp == 2:
        o = outs[0] - lam_ref[0] * outs[1]
        o = _rms(o) * g_ref[...] * post_scale
    else:
        o = outs[0]
    o_ref[0, 0] = o


def _attention(q_parts, pieces, heads, ncomp, scale, tq=256, lam=None, subln_g=None, post_scale=1.0):
    b, _, sq, _ = q_parts[0].shape
    dv = pieces[0][1].shape[3]
    tq = min(tq, sq)
    ins = list(q_parts)
    in_specs = [pl.BlockSpec((1, ncomp, tq, q.shape[3]), lambda i, h, j: (i, h, j, 0)) for q in q_parts]
    for k_parts, v in pieces:
        for k in k_parts:
            ins.append(k)
            if k.shape[1] == 1:
                in_specs.append(pl.BlockSpec((1, 1) + k.shape[2:], lambda i, h, j: (i, 0, 0, 0)))
            else:
                in_specs.append(pl.BlockSpec((1, ncomp) + k.shape[2:], lambda i, h, j: (i, h, 0, 0)))
        ins.append(v)
        in_specs.append(pl.BlockSpec((1, 1) + v.shape[2:], lambda i, h, j: (i, h, 0, 0)))
    if ncomp == 2:
        ins += [lam.reshape(1), subln_g.reshape(1, dv)]
        in_specs += [pl.BlockSpec(memory_space=pltpu.SMEM), _const_spec((1, dv))]
    sk = sum(v.shape[2] for _, v in pieces)
    return pl.pallas_call(
        functools.partial(_attn_kernel, n_q=len(q_parts), n_pieces=len(pieces), ncomp=ncomp, scale=scale,
                          post_scale=post_scale),
        grid=(b, heads, sq // tq),
        in_specs=in_specs,
        out_specs=pl.BlockSpec((1, 1, tq, dv), lambda i, h, j: (i, h, j, 0)),
        out_shape=jax.ShapeDtypeStruct((b, heads, sq, dv), F32),
        scratch_shapes=[pltpu.VMEM((ncomp, LANES, sk), BF16), pltpu.VMEM((sk, 2 * dv), BF16)],
        compiler_params=_params('parallel', 'parallel', 'arbitrary'),
        name='attention',
    )(*ins)


def _forget_terms(f, log_lb, log_1m_lb, one_m_lb):
    log_sig = jnp.minimum(f, 0.0) - jnp.log1p(jnp.exp(-jnp.abs(f)))
    b = log_1m_lb + log_sig
    log_g = jnp.maximum(log_lb, b) + jnp.log1p(jnp.exp(-jnp.abs(log_lb - b)))
    return log_g, one_m_lb * jax.nn.sigmoid(-f)


def _hg_tables():
    ck, sub = HG_CHUNK, HG_SUB
    t = np.arange(ck)
    cum_mats, half_masks, group_masks, keeps = [], [], [], []
    for rev in (False, True):
        mats = [(t[None, :] >= t[:, None]) if rev else (t[None, :] <= t[:, None])]
        halves = []
        hs = ck // 2
        while hs >= sub:
            pos = t % (2 * hs)
            b = t - pos + hs
            mats.append((t[None, :] >= b[:, None]) if rev else (t[None, :] < b[:, None]))
            q_half = (pos < hs) if rev else (pos >= hs)
            halves.append(np.stack([q_half, ~q_half]))
            if not rev:
                grp = (t[:, None] // (2 * hs)) == (t[None, :] // (2 * hs))
                group_masks.append(np.concatenate([grp, grp], axis=0))
            hs //= 2
        cum_mats.append(np.concatenate(mats, axis=0))
        half_masks.append(np.stack(halves))
        c = np.arange(ck * sub)
        tt, ss = (c // sub) % sub, c % sub
        keeps.append((ss >= tt) if rev else (ss <= tt))
    lanes = 2 * HG_KEY_DIM
    ln = np.arange(lanes)
    bd = (ln[:, None] // HG_KEY_DIM) == (ln[None, :] // HG_KEY_DIM)
    hm = np.broadcast_to(np.stack(half_masks)[..., None], (2, len(half_masks[0]), 2, ck, lanes))
    return (jnp.asarray(np.stack(cum_mats), BF16), jnp.asarray(hm, F32), jnp.asarray(np.stack(group_masks), F32),
            jnp.asarray(bd, F32), jnp.asarray(np.broadcast_to(np.stack(keeps)[..., None], (2, ck * sub, lanes)), F32))


def _split3(x):
    a = x.astype(BF16)
    r = x - a.astype(F32)
    b = r.astype(BF16)
    return a, b, (r - b.astype(F32)).astype(BF16)


def _hg_chunk(q, k, v, lg, st, rev, cm, hm, gm, keep, bd, m0, m1):
    ck, sub = HG_CHUNK, HG_SUB
    lanes = lg.shape[1]
    c3 = jnp.dot(cm, jnp.concatenate(_split3(lg), axis=1), preferred_element_type=F32)
    call = c3[:, :lanes] + c3[:, lanes:2 * lanes] + c3[:, 2 * lanes:]
    cum = call[0:ck]
    tot = cum[0:1] if rev else cum[ck - 1:ck]
    o = _dot_nt((q * jnp.exp(cum)).astype(BF16), st.astype(BF16))
    kd = (k * jnp.exp(tot - cum)).astype(BF16)
    st_new = st * jnp.exp(tot) + bd * jnp.dot(v.T.astype(BF16), kd, preferred_element_type=F32)
    s2 = None
    for lv in range(gm.shape[0]):
        cb = call[(lv + 1) * ck:(lv + 2) * ck]
        qd = q * jnp.exp(jnp.minimum(cum - cb, 0.0)) * hm[lv, 0]
        kf = (k * jnp.exp(jnp.minimum(cb - cum, 0.0)) * hm[lv, 1]).astype(BF16)
        q2 = jnp.concatenate([qd * m0, qd * m1], axis=0).astype(BF16)
        term = _dot_nt(q2, kf) * gm[lv]
        s2 = term if s2 is None else s2 + term
    r = jnp.dot(s2.astype(BF16), v.astype(BF16), preferred_element_type=F32)
    o = o + m0 * r[:ck] + m1 * r[ck:]
    rows, vts = [], []
    for i in range(ck // sub):
        lo, hi = i * sub, (i + 1) * sub
        ki, ci = k[lo:hi], cum[lo:hi]
        for t in range(lo, hi):
            rows.append(q[t:t + 1] * ki * jnp.exp(jnp.minimum(cum[t:t + 1] - ci, 0.0)))
            vts.append(v[lo:hi])
    sc = jnp.dot(jnp.concatenate(rows, axis=0).astype(BF16), bd.astype(BF16), preferred_element_type=F32)
    prod = sc * jnp.concatenate(vts, axis=0) * keep
    o = o + jnp.sum(prod.reshape(ck, sub, lanes), axis=1)
    return o, st_new


def _hgrn_kernel(q_ref, ff_ref, fb_ref, i_ref, qc_ref, ffc_ref, fbc_ref, ic_ref, lb_ref, g_ref,
                 cm_ref, hm_ref, gm_ref, bd_ref, keep_ref, o_ref, oc_ref, or_ref, ocr_ref, st_ref, *, n_lat, n_ctx):
    ck = HG_CHUNK
    lanes = o_ref.shape[-1]
    lane = lax.broadcasted_iota(jnp.int32, (1, lanes), 1)
    m0 = (lane < HG_KEY_DIM).astype(F32)
    m1 = 1.0 - m0
    bd = bd_ref[...]
    gm = gm_ref[...]

    def one(q, f, v, rev):
        d = 1 if rev else 0
        lg, k = _forget_terms(f, lb_ref[d, 0:1, :], lb_ref[d, 1:2, :], lb_ref[d, 2:3, :])
        o, st = _hg_chunk(q, k, v, lg, st_ref[d], rev, cm_ref[d], hm_ref[d], gm, keep_ref[d], bd, m0, m1)
        st_ref[d] = st
        return o

    def sweep(qr, ffr, fbr, ir, out_f, out_r, n):
        nc = n // ck

        def body(step, carry):
            idf = pl.ds(pl.multiple_of(step * ck, ck), ck)
            idr = pl.ds(pl.multiple_of((nc - 1 - step) * ck, ck), ck)
            out_f[0, idf, :] = one(qr[0, idf, :], ffr[0, idf, :], ir[0, idf, :], False)
            out_r[idr, :] = one(qr[0, idr, :], fbr[0, idr, :], ir[0, idr, :], True)
            return carry

        lax.fori_loop(0, nc, body, 0, unroll=HG_UNROLL)

    st_ref[...] = jnp.zeros(st_ref.shape, F32)
    sweep(qc_ref, ffc_ref, fbc_ref, ic_ref, oc_ref, ocr_ref, n_ctx)
    sweep(q_ref, ff_ref, fb_ref, i_ref, o_ref, or_ref, n_lat)

    mean_mat = bd * (1.0 / HG_VAL_DIM)

    def readout(out, out_r, n):
        tile = min(n, 512)

        def body(step, carry):
            idx = pl.ds(pl.multiple_of(step * tile, tile), tile)
            x = out[0, idx, :] + out_r[idx, :]
            ms = jnp.dot(x * x, mean_mat, precision=HIGHEST, preferred_element_type=F32)
            out[0, idx, :] = x * lax.rsqrt(ms + NORM_EPS) * g_ref[...]
            return carry

        lax.fori_loop(0, n // tile, body, 0)

    readout(oc_ref, ocr_ref, n_ctx)
    readout(o_ref, or_ref, n_lat)


def _hgrn(q, ff, fb, iv, qc, ffc, fbc, ic, lb_terms, norm_g):
    b, n_lat, width = q.shape
    n_ctx = qc.shape[1]
    lanes = 2 * HG_KEY_DIM
    tables = _hg_tables()
    lat = pl.BlockSpec((1, n_lat, lanes), lambda i, j: (i, 0, j))
    ctx = pl.BlockSpec((1, n_ctx, lanes), lambda i, j: (i, 0, j))
    g2 = jnp.tile(norm_g, 2).reshape(1, lanes)
    return pl.pallas_call(
        functools.partial(_hgrn_kernel, n_lat=n_lat, n_ctx=n_ctx),
        grid=(b, width // lanes),
        in_specs=[lat] * 4 + [ctx] * 4 + [pl.BlockSpec((2, 3, lanes), lambda i, j: (0, 0, j)), _const_spec((1, lanes))]
                 + [_const_spec(t.shape) for t in tables],
        out_specs=[lat, ctx],
        out_shape=[jax.ShapeDtypeStruct(q.shape, F32), jax.ShapeDtypeStruct(qc.shape, F32)],
        scratch_shapes=[pltpu.VMEM((n_lat, lanes), F32), pltpu.VMEM((n_ctx, lanes), F32), pltpu.VMEM((2, lanes, lanes), F32)],
        compiler_params=_params('parallel', 'parallel'),
        name='hgrn2',
    )(q, ff, fb, iv, qc, ffc, fbc, ic, lb_terms, g2, *tables)


def _out_proj_kernel(hy_ref, da_ref, mla_ref, hg_ref, gate_ref, x_ref, g1_ref, w_ref, o_ref):
    c = hy_ref.shape[2]
    acc = jnp.dot(hy_ref[0].astype(BF16), w_ref[0:c, :], preferred_element_type=F32)
    for i, head_ref in ((1, da_ref), (2, mla_ref)):
        dv = head_ref.shape[3]
        for h in range(head_ref.shape[1]):
            lo = i * c + h * dv
            acc = acc + jnp.dot(head_ref[0, h].astype(BF16), w_ref[lo:lo + dv, :], preferred_element_type=F32)
    hg = hg_ref[0] * _silu(gate_ref[0])
    acc = acc + jnp.dot(hg.astype(BF16), w_ref[3 * c:4 * c, :], preferred_element_type=F32)
    o_ref[0] = x_ref[0] + g1_ref[0] * acc


def _out_proj(y_hy, y_da, y_mla, y_hg, gate, x, g1, w_out, tm=512):
    b, s, d = x.shape
    tm = min(tm, s)
    c = y_hy.shape[2]
    part = pl.BlockSpec((1, tm, c), lambda i, j: (i, j, 0))
    headed = lambda a: pl.BlockSpec((1, a.shape[1], tm, a.shape[3]), lambda i, j: (i, 0, j, 0))
    row = pl.BlockSpec((1, tm, d), lambda i, j: (i, j, 0))
    return pl.pallas_call(
        _out_proj_kernel,
        grid=(b, s // tm),
        in_specs=[part, headed(y_da), headed(y_mla), part, part, row, pl.BlockSpec((1, 1, d), lambda i, j: (i, 0, 0)),
                  _const_spec(w_out.shape)],
        out_specs=row,
        out_shape=jax.ShapeDtypeStruct(x.shape, F32),
        compiler_params=_params('parallel', 'parallel'),
        name='out_proj',
    )(y_hy, y_da, y_mla, y_hg, gate, x, g1, w_out)


def _router_kernel(x_ref, g_ref, sc_ref, sh_ref, wrt_ref, bias_ref, *rest, compact):
    h = _rms(x_ref[0]) * g_ref[...] * (1.0 + sc_ref[0]) + sh_ref[0]
    tm = h.shape[0]
    scores = jax.nn.sigmoid(_dot_nt(wrt_ref[...], h, precision=HIGHEST))
    choice = scores + bias_ref[...]
    per = N_EXPERTS // N_EXPERT_GROUPS
    neg = -jnp.inf
    iota_g = lax.broadcasted_iota(jnp.int32, (per, tm), 0)
    grp_rows = []
    for gi in range(N_EXPERT_GROUPS):
        blk = choice[gi * per:(gi + 1) * per]
        m1 = jnp.max(blk, axis=0, keepdims=True)
        first = jnp.min(jnp.where(blk == m1, iota_g, per), axis=0, keepdims=True)
        m2 = jnp.max(jnp.where(iota_g == first, neg, blk), axis=0, keepdims=True)
        grp_rows.append(m1 + m2)
    grp = jnp.concatenate(grp_rows, axis=0)
    iota_n = lax.broadcasted_iota(jnp.int32, (N_EXPERT_GROUPS, tm), 0)
    gsel = jnp.zeros((N_EXPERT_GROUPS, tm), F32)
    for _ in range(TOPK_GROUPS):
        m = jnp.max(grp, axis=0, keepdims=True)
        first = jnp.min(jnp.where(grp == m, iota_n, N_EXPERT_GROUPS), axis=0, keepdims=True)
        hit = iota_n == first
        gsel = jnp.where(hit, 1.0, gsel)
        grp = jnp.where(hit, neg, grp)
    emask = jnp.concatenate([jnp.broadcast_to(gsel[gi:gi + 1], (per, tm)) for gi in range(N_EXPERT_GROUPS)], axis=0)
    cand = jnp.where(emask > 0.0, choice, neg)
    iota_e = lax.broadcasted_iota(jnp.int32, (N_EXPERTS, tm), 0)
    sel = jnp.zeros((N_EXPERTS, tm), F32)
    chosen = []
    for _ in range(TOP_K):
        m = jnp.max(cand, axis=0, keepdims=True)
        first = jnp.min(jnp.where(cand == m, iota_e, N_EXPERTS), axis=0, keepdims=True)
        hit = iota_e == first
        sel = jnp.where(hit, 1.0, sel)
        cand = jnp.where(hit, neg, cand)
        chosen.append(first)
    w = scores * sel
    gate = w / jnp.sum(w, axis=0, keepdims=True) * ROUTED_SCALE
    if not compact:
        h_ref, gate_ref = rest
        h_ref[0] = h.astype(BF16)
        gate_ref[0] = gate
        return
    hp_ref, eid_ref, rank_ref, w_ref, cnt_out_ref, cnt_ref = rest
    hp_ref[0] = _pack_halves(h)

    @pl.when((pl.program_id(0) == 0) & (pl.program_id(1) == 0))
    def _():
        cnt_ref[...] = jnp.zeros(cnt_ref.shape, F32)

    src = lax.broadcasted_iota(jnp.int32, (tm, tm), 0)
    dst = lax.broadcasted_iota(jnp.int32, (tm, tm), 1)
    running = jnp.dot(sel.astype(BF16), (src <= dst).astype(BF16), preferred_element_type=F32)
    rank_dense = cnt_ref[:, 0:1] + running - 1.0
    e_rows, r_rows, w_rows = [], [], []
    for first in chosen:
        hit = iota_e == first
        e_rows.append(first)
        r_rows.append(jnp.sum(jnp.where(hit, rank_dense, 0.0), axis=0, keepdims=True))
        w_rows.append(jnp.sum(jnp.where(hit, gate, 0.0), axis=0, keepdims=True))
    eid_ref[...] = jnp.concatenate(e_rows, axis=0)
    rank_ref[...] = jnp.concatenate(r_rows, axis=0).astype(jnp.int32)
    w_ref[...] = jnp.concatenate(w_rows, axis=0)
    cnt_ref[...] = cnt_ref[...] + running[:, tm - 1:tm]
    cnt_out_ref[...] = cnt_ref[...]


def _router(x, g, scale, shift, w_router, e_bias, tm=512, compact=False):
    b, s, d = x.shape
    tm = min(tm, s)
    e = w_router.shape[1]
    row = pl.BlockSpec((1, tm, d), lambda i, j: (i, j, 0))
    mod = pl.BlockSpec((1, 1, d), lambda i, j: (i, 0, 0))
    if compact:
        nj = s // tm
        tok = pl.BlockSpec((TOP_K, tm), lambda i, j: (0, i * nj + j))
        out_specs = [pl.BlockSpec((1, tm, d // 2), lambda i, j: (i, j, 0)), tok, tok, tok, _const_spec((e, LANES))]
        out_shape = [jax.ShapeDtypeStruct((b, s, d // 2), jnp.int32), jax.ShapeDtypeStruct((TOP_K, b * s), jnp.int32),
                     jax.ShapeDtypeStruct((TOP_K, b * s), jnp.int32), jax.ShapeDtypeStruct((TOP_K, b * s), F32),
                     jax.ShapeDtypeStruct((e, LANES), F32)]
        scratch = [pltpu.VMEM((e, LANES), F32)]
        semantics = ('arbitrary', 'arbitrary')
    else:
        out_specs = [row, pl.BlockSpec((1, e, tm), lambda i, j: (i, 0, j))]
        out_shape = [jax.ShapeDtypeStruct((b, s, d), BF16), jax.ShapeDtypeStruct((b, e, s), F32)]
        scratch = []
        semantics = ('parallel', 'parallel')
    return pl.pallas_call(
        functools.partial(_router_kernel, compact=compact),
        grid=(b, s // tm),
        in_specs=[row, _const_spec((1, d)), mod, mod, _const_spec((e, d)), _const_spec((e, 1))],
        out_specs=out_specs,
        out_shape=out_shape,
        scratch_shapes=scratch,
        compiler_params=_params(*semantics),
        name='router',
    )(x, g.reshape(1, d), scale, shift, w_router.T, e_bias.reshape(e, 1))


def _moe_kernel(h_ref, x_ref, gate_ref, g2_ref, wg_ref, wu_ref, wd_ref, sg_ref, su_ref, sd_ref, *rest, final):
    if final:
        fg_ref, o_ref, acc_ref = rest
    else:
        o_ref, acc_ref = rest
    e = pl.program_id(2)
    h = h_ref[0]

    @pl.when(e == 0)
    def _():
        a = jnp.dot(h, sg_ref[...], preferred_element_type=F32)
        u = jnp.dot(h, su_ref[...], preferred_element_type=F32)
        acc_ref[...] = jnp.dot((_silu(a) * u).astype(BF16), sd_ref[...], preferred_element_type=F32)

    lane = lax.broadcasted_iota(jnp.int32, gate_ref.shape[1:], 1)
    gcol = jnp.sum(jnp.where(lane == e, gate_ref[0], 0.0), axis=-1, keepdims=True)
    a = jnp.dot(h, wg_ref[0].astype(BF16), preferred_element_type=F32)
    u = jnp.dot(h, wu_ref[0].astype(BF16), preferred_element_type=F32)
    acc_ref[...] += jnp.dot((_silu(a) * u * gcol).astype(BF16), wd_ref[0].astype(BF16), preferred_element_type=F32)

    @pl.when(e == pl.num_programs(2) - 1)
    def _():
        y = x_ref[0] + g2_ref[0] * acc_ref[...]
        if final:
            y = _rms(y) * fg_ref[...]
        o_ref[0] = y


def _moe(h2, x, gate, g2, layer, w_gate, w_up, w_down, s_gate, s_up, s_down, final_g=None, tm=1024):
    b, s, d = x.shape
    tm = min(tm, s)
    _, e, _, ff = w_gate.shape
    row = pl.BlockSpec((1, tm, d), lambda i, j, k: (i, j, 0))
    ins = [h2, x, gate, g2, w_gate, w_up, w_down, s_gate, s_up, s_down]
    in_specs = [row, row, pl.BlockSpec((1, tm, e), lambda i, j, k: (i, j, 0)),
                pl.BlockSpec((1, 1, d), lambda i, j, k: (i, 0, 0)),
                pl.BlockSpec((None, 1, d, ff), lambda i, j, k: (layer, k, 0, 0)),
                pl.BlockSpec((None, 1, d, ff), lambda i, j, k: (layer, k, 0, 0)),
                pl.BlockSpec((None, 1, ff, d), lambda i, j, k: (layer, k, 0, 0)),
                _const_spec(s_gate.shape), _const_spec(s_up.shape), _const_spec(s_down.shape)]
    if final_g is not None:
        ins.append(final_g.reshape(1, d))
        in_specs.append(_const_spec((1, d)))
    return pl.pallas_call(
        functools.partial(_moe_kernel, final=final_g is not None),
        grid=(b, s // tm, e),
        in_specs=in_specs,
        out_specs=row,
        out_shape=jax.ShapeDtypeStruct(x.shape, F32),
        scratch_shapes=[pltpu.VMEM((tm, d), F32)],
        compiler_params=_params('parallel', 'parallel', 'arbitrary'),
        name='moe',
    )(*ins)


MOE_ROW_TILE = 512
SC_ROWS = 128
V7X_SC_CORES = 2
V7X_SC_SUBCORES = 16


def _pack_halves(x):
    n = x.shape[1] // 2
    lo = pltpu.bitcast(x[:, :n].astype(BF16).astype(F32), jnp.int32)
    hi = pltpu.bitcast(x[:, n:].astype(BF16).astype(F32), jnp.int32)
    return jnp.bitwise_or(jnp.bitwise_and(hi, -65536), lax.shift_right_logical(lo, 16))


def _unpack_halves(p):
    lo = pltpu.bitcast(lax.shift_left(p, 16), F32).astype(BF16)
    hi = pltpu.bitcast(jnp.bitwise_and(p, -65536), F32).astype(BF16)
    return lo, hi


def _route_pos_kernel(off_ref, eid_ref, rank_ref, pos_ref):
    eid = eid_ref[...]
    base = jnp.zeros(eid.shape, jnp.int32)
    for e in range(N_EXPERTS):
        base = jnp.where(eid == e, off_ref[e], base)
    pos_ref[...] = base + rank_ref[...]


def _route_pos(offsets, eid, rank):
    return pl.pallas_call(
        _route_pos_kernel,
        grid=(1,),
        in_specs=[pl.BlockSpec(memory_space=pltpu.SMEM), _const_spec(eid.shape), _const_spec(rank.shape)],
        out_specs=_const_spec(eid.shape),
        out_shape=jax.ShapeDtypeStruct(eid.shape, jnp.int32),
        compiler_params=_params('arbitrary'),
        name='route_pos',
    )(offsets, eid, rank)


def _sc_mesh():
    return plsc.VectorSubcoreMesh(core_axis_name='c', subcore_axis_name='s', num_cores=V7X_SC_CORES,
                                  num_subcores=V7X_SC_SUBCORES)


def _sc_dispatch(hp, pos, n_rows):
    t, w = hp.shape
    k = pos.shape[0]
    workers = V7X_SC_CORES * V7X_SC_SUBCORES
    per_worker = t // workers
    pos_flat = pos.reshape(k * t)

    @functools.partial(pl.kernel, mesh=_sc_mesh(), out_type=jax.ShapeDtypeStruct((n_rows, w), jnp.int32),
                       scratch_types=[pltpu.VMEM((SC_ROWS,), jnp.int32), pltpu.VMEM((SC_ROWS, w), jnp.int32),
                                      pltpu.SemaphoreType.DMA])
    def scatter(hp_hbm, pos_hbm, out_hbm, idx_v, rows_v, sem):
        wid = lax.axis_index('s') * V7X_SC_CORES + lax.axis_index('c')

        @pl.loop(0, per_worker // SC_ROWS)
        def _(i):
            t0 = pl.multiple_of(wid * per_worker + i * SC_ROWS, SC_ROWS)
            pltpu.sync_copy(hp_hbm.at[pl.ds(t0, SC_ROWS)], rows_v)
            for j in range(k):
                pltpu.sync_copy(pos_hbm.at[pl.ds(pl.multiple_of(j * t + t0, SC_ROWS), SC_ROWS)], idx_v)
                pltpu.async_copy(rows_v, out_hbm.at[idx_v], sem).wait()

    return scatter(hp, pos_flat)


def _sc_collect(yp, pos):
    _, w = yp.shape
    k, t = pos.shape
    workers = V7X_SC_CORES * V7X_SC_SUBCORES
    per_worker = k * t // workers
    pos_flat = pos.reshape(k * t)

    @functools.partial(pl.kernel, mesh=_sc_mesh(), out_type=jax.ShapeDtypeStruct((k * t, w), jnp.int32),
                       scratch_types=[pltpu.VMEM((SC_ROWS,), jnp.int32), pltpu.VMEM((SC_ROWS, w), jnp.int32),
                                      pltpu.SemaphoreType.DMA])
    def gather(yp_hbm, pos_hbm, out_hbm, idx_v, rows_v, sem):
        wid = lax.axis_index('s') * V7X_SC_CORES + lax.axis_index('c')

        @pl.loop(0, per_worker // SC_ROWS)
        def _(i):
            r0 = pl.multiple_of(wid * per_worker + i * SC_ROWS, SC_ROWS)
            pltpu.sync_copy(pos_hbm.at[pl.ds(r0, SC_ROWS)], idx_v)
            pltpu.async_copy(yp_hbm.at[idx_v], rows_v, sem).wait()
            pltpu.sync_copy(rows_v, out_hbm.at[pl.ds(r0, SC_ROWS)])

    return gather(yp, pos_flat)


def _expert_kernel(te_ref, nu_ref, x_ref, wg_ref, wu_ref, wd_ref, o_ref, wg_s, wu_s, wd_s):
    i = pl.program_id(0)

    @pl.when(i < nu_ref[0])
    def _():
        @pl.when((i == 0) | (te_ref[i] != te_ref[jnp.maximum(i - 1, 0)]))
        def _():
            wg_s[...] = wg_ref[0].astype(BF16)
            wu_s[...] = wu_ref[0].astype(BF16)
            wd_s[...] = wd_ref[0].astype(BF16)

        lo, hi = _unpack_halves(x_ref[...])
        half = lo.shape[1]
        a = (jnp.dot(lo, wg_s[:half, :], preferred_element_type=F32)
             + jnp.dot(hi, wg_s[half:, :], preferred_element_type=F32))
        u = (jnp.dot(lo, wu_s[:half, :], preferred_element_type=F32)
             + jnp.dot(hi, wu_s[half:, :], preferred_element_type=F32))
        y = jnp.dot((_silu(a) * u).astype(BF16), wd_s[...], preferred_element_type=F32)
        o_ref[...] = _pack_halves(y)


def _experts(xp, tile_expert, n_used, layer, w_gate, w_up, w_down):
    n_rows, half = xp.shape
    _, _, d, ff = w_gate.shape
    r = MOE_ROW_TILE
    row = pl.BlockSpec((r, half), lambda i, te, nu: (i, 0))
    grid_spec = pltpu.PrefetchScalarGridSpec(
        num_scalar_prefetch=2,
        grid=(n_rows // r,),
        in_specs=[row,
                  pl.BlockSpec((None, 1, d, ff), lambda i, te, nu: (layer, te[i], 0, 0)),
                  pl.BlockSpec((None, 1, d, ff), lambda i, te, nu: (layer, te[i], 0, 0)),
                  pl.BlockSpec((None, 1, ff, d), lambda i, te, nu: (layer, te[i], 0, 0))],
        out_specs=row,
        scratch_shapes=[pltpu.VMEM((d, ff), BF16), pltpu.VMEM((d, ff), BF16), pltpu.VMEM((ff, d), BF16)],
    )
    return pl.pallas_call(
        _expert_kernel,
        grid_spec=grid_spec,
        out_shape=jax.ShapeDtypeStruct((n_rows, half), jnp.int32),
        compiler_params=_params('arbitrary'),
        name='experts',
    )(tile_expert, n_used, xp, w_gate, w_up, w_down)


def _combine_kernel(yg_ref, w_ref, hp_ref, x_ref, g2_ref, sg_ref, su_ref, sd_ref, *rest, final):
    if final:
        fg_ref, o_ref = rest
    else:
        (o_ref,) = rest
    half = hp_ref.shape[2]
    lo, hi = _unpack_halves(hp_ref[0])
    sg, su = sg_ref[...], su_ref[...]
    a = jnp.dot(lo, sg[:half], preferred_element_type=F32) + jnp.dot(hi, sg[half:], preferred_element_type=F32)
    u = jnp.dot(lo, su[:half], preferred_element_type=F32) + jnp.dot(hi, su[half:], preferred_element_type=F32)
    acc = jnp.dot((_silu(a) * u).astype(BF16), sd_ref[...], preferred_element_type=F32)
    acc_lo, acc_hi = acc[:, :half], acc[:, half:]
    wts = w_ref[0]
    for k in range(yg_ref.shape[0]):
        ylo, yhi = _unpack_halves(yg_ref[k, 0])
        wk = wts[:, k:k + 1]
        acc_lo = acc_lo + wk * ylo.astype(F32)
        acc_hi = acc_hi + wk * yhi.astype(F32)
    y = x_ref[0] + g2_ref[0] * jnp.concatenate([acc_lo, acc_hi], axis=1)
    if final:
        y = _rms(y) * fg_ref[...]
    o_ref[0] = y


def _combine(yg, wts, hp, x, g2, s_gate, s_up, s_down, final_g=None, tm=256):
    b, s, d = x.shape
    k = yg.shape[0]
    half = d // 2
    row = pl.BlockSpec((1, tm, d), lambda i, j: (i, j, 0))
    prow = pl.BlockSpec((1, tm, half), lambda i, j: (i, j, 0))
    ins = [yg, wts, hp, x, g2, s_gate, s_up, s_down]
    in_specs = [pl.BlockSpec((k, 1, tm, half), lambda i, j: (0, i, j, 0)), pl.BlockSpec((1, tm, k), lambda i, j: (i, j, 0)),
                prow, row, pl.BlockSpec((1, 1, d), lambda i, j: (i, 0, 0)),
                _const_spec(s_gate.shape), _const_spec(s_up.shape), _const_spec(s_down.shape)]
    if final_g is not None:
        ins.append(final_g.reshape(1, d))
        in_specs.append(_const_spec((1, d)))
    return pl.pallas_call(
        functools.partial(_combine_kernel, final=final_g is not None),
        grid=(b, s // tm),
        in_specs=in_specs,
        out_specs=row,
        out_shape=jax.ShapeDtypeStruct(x.shape, F32),
        compiler_params=_params('parallel', 'parallel'),
        name='moe_combine',
    )(*ins)


def _routed_moe(x, g, scale, shift, g2, w_router, e_bias, layer, w_gate, w_up, w_down, s_gate, s_up, s_down, final_g=None):
    b, s, d = x.shape
    t = b * s
    hp, eid, rank, wts, counts = _router(x, g, scale, shift, w_router, e_bias, compact=True)
    counts = counts[:, 0].astype(jnp.int32)
    r = MOE_ROW_TILE
    padded = (counts + (r - 1)) // r * r
    ends = jnp.cumsum(padded)
    offsets = ends - padded
    n_rows = t * TOP_K + N_EXPERTS * r
    tile_start = jnp.arange(n_rows // r, dtype=jnp.int32) * r
    tile_expert = jnp.minimum(jnp.sum((tile_start[:, None] >= ends[None, :]).astype(jnp.int32), axis=1), N_EXPERTS - 1)
    n_used = (ends[-1] // r).reshape(1).astype(jnp.int32)
    pos = _route_pos(offsets.astype(jnp.int32), eid, rank)
    xp = _sc_dispatch(hp.reshape(t, d // 2), pos, n_rows)
    yp = _experts(xp, tile_expert.astype(jnp.int32), n_used, layer, w_gate, w_up, w_down)
    yg = _sc_collect(yp, pos).reshape(TOP_K, b, s, d // 2)
    return _combine(yg, wts.T.reshape(b, s, TOP_K), hp, x, g2, s_gate, s_up, s_down, final_g)


def _mixers(p, pc, ctx_out, prm, l, lam_init, rope_tabs, lb_terms):
    s = p['hy_v'].shape[1]
    sc = pc['hy_v'].shape[1]

    hy_args = (prm['hy_w1'][l], prm['hy_b1'][l], prm['hy_w2'][l], prm['hy_b2'][l], prm['hy_w3'][l], prm['hy_b3'][l],
               prm['hy_sin_freq'][l], prm['hy_decay'][l])
    y_hy = _hyena([p['hy_v'], p['hy_x1'], p['hy_x2']], prm['hy_conv_w'][l], prm['hy_conv_b'][l],
                  _hy_filters(s, *hy_args), prm['hy_bias'][l], inner=128)
    yc_hy = None
    if ctx_out:
        yc_hy = _hyena([pc['hy_v'], pc['hy_x1'], pc['hy_x2']], prm['hy_conv_w'][l], prm['hy_conv_b'][l],
                       _hy_filters(sc, *hy_args), prm['hy_bias'][l], inner=32)

    lp = prm['da_lambda'][l].astype(F32)
    lam = jnp.exp(jnp.sum(lp[0] * lp[1])) - jnp.exp(jnp.sum(lp[2] * lp[3])) + lam_init
    da_kw = dict(heads=DA_HEADS, ncomp=2, scale=DA_HEAD_DIM ** -0.5, lam=lam, subln_g=prm['da_subln_g'][l],
                 post_scale=1.0 - lam_init)
    da_ctx = ([pc['da_k']], pc['da_v'])
    y_da = _attention([p['da_q']], [da_ctx, ([p['da_k']], p['da_v'])], **da_kw)
    yc_da = _attention([pc['da_q']], [da_ctx], **da_kw) if ctx_out else None

    wq = prm['mla_w_q_up'][l].reshape(MLA_Q_RANK, MLA_HEADS, MLA_NOPE_DIM + MLA_ROPE_DIM)
    wq_n = wq[:, :, :MLA_NOPE_DIM].reshape(MLA_Q_RANK, -1).astype(BF16)
    wq_r = wq[:, :, MLA_NOPE_DIM:].reshape(MLA_Q_RANK, -1).astype(BF16)
    wkv = prm['mla_w_kv_up'][l].reshape(MLA_KV_RANK, MLA_HEADS, MLA_NOPE_DIM + MLA_V_DIM)
    wkv_n = wkv[:, :, :MLA_NOPE_DIM].reshape(MLA_KV_RANK, -1).astype(BF16)
    wkv_v = wkv[:, :, MLA_NOPE_DIM:].reshape(MLA_KV_RANK, -1).astype(BF16)

    def queries(qd, tabs):
        return _norm_proj(qd, prm['mla_q_norm_g'][l], [(wq_n, F32, False, MLA_HEADS), (wq_r, F32, True, MLA_HEADS)],
                          rope_tabs=tabs)

    def keys_values(kvd):
        return _norm_proj(kvd, prm['mla_kv_norm_g'][l], [(wkv_n, BF16, False, MLA_HEADS), (wkv_v, BF16, False, MLA_HEADS)])

    kn_l, v_l = keys_values(p['mla_kv'])
    kn_c, v_c = keys_values(pc['mla_kv'])
    mla_kw = dict(heads=MLA_HEADS, ncomp=1, scale=(MLA_NOPE_DIM + MLA_ROPE_DIM) ** -0.5)
    mla_ctx = ([kn_c, pc['mla_kr']], v_c)
    y_mla = _attention(queries(p['mla_q'], rope_tabs), [mla_ctx, ([kn_l, p['mla_kr']], v_l)], **mla_kw)
    yc_mla = _attention(queries(pc['mla_q'], None), [mla_ctx], **mla_kw) if ctx_out else None

    o, oc = _hgrn(p['hg_q'], p['hg_ff'], p['hg_fb'], p['hg_i'], pc['hg_q'], pc['hg_ff'], pc['hg_fb'], pc['hg_i'],
                  lb_terms, prm['hg_norm_g'][l])
    return (y_hy, y_da, y_mla, o), (yc_hy, yc_da, yc_mla, oc)


def kernel(x, c, ctx, c_ctx, w_ada, b_ada, norm1_g, norm2_g, w_in, w_out, hy_conv_w, hy_conv_b, hy_w1, hy_b1, hy_w2, hy_b2, hy_w3, hy_b3, hy_sin_freq, hy_decay, hy_bias, da_lambda, da_subln_g, mla_q_norm_g, mla_w_q_up, mla_kv_norm_g, mla_w_kv_up, hg_lower_bounds, hg_norm_g, moe_w_router, moe_bias, moe_w_gate, moe_w_up, moe_w_down, moe_sh_gate, moe_sh_up, moe_sh_down, final_norm_g):
    prm = dict(hy_conv_w=hy_conv_w, hy_conv_b=hy_conv_b, hy_w1=hy_w1, hy_b1=hy_b1, hy_w2=hy_w2, hy_b2=hy_b2,
               hy_w3=hy_w3, hy_b3=hy_b3, hy_sin_freq=hy_sin_freq, hy_decay=hy_decay, hy_bias=hy_bias,
               da_lambda=da_lambda, da_subln_g=da_subln_g, mla_q_norm_g=mla_q_norm_g, mla_w_q_up=mla_w_q_up,
               mla_kv_norm_g=mla_kv_norm_g, mla_w_kv_up=mla_w_kv_up, hg_norm_g=hg_norm_g)
    b, n_lat, d = x.shape
    depth = w_in.shape[0]
    rows = n_lat // GRID_W
    row_pos = jnp.repeat(jnp.arange(rows, dtype=jnp.int32), GRID_W)
    col_pos = jnp.tile(jnp.arange(GRID_W, dtype=jnp.int32), rows)
    rope_tabs = _rope_tables(row_pos, col_pos, 2 * DA_HEADS * DA_HEAD_DIM)
    lbs = jnp.cumsum(jax.nn.softmax(hg_lower_bounds.astype(F32), axis=1), axis=1)
    lbs = lbs - lbs[:, :1]
    cond = jnp.concatenate([c, c_ctx[None], jnp.zeros((8 - b - 1, d), F32)], axis=0)

    for l in range(depth):
        ctx_out = l < depth - 1
        mods = _ada(cond, w_ada[l], b_ada[l])
        sh1, sc1, g1, sh2, sc2, g2 = [m[:, None, :] for m in jnp.split(mods[:b], 6, axis=-1)]
        mc = [jnp.broadcast_to(m[:, None, :], (b, 1, d)) for m in jnp.split(mods[b:b + 1], 6, axis=-1)]

        off = 0
        outs = []
        for _, wdt, dt, rope, split, rep in _SEGMENTS:
            w = w_in[l][:, off:off + wdt].astype(BF16)
            outs.append((jnp.tile(w, (1, rep)) if rep > 1 else w, dt, rope, split))
            off += wdt
        names = [seg[0] for seg in _SEGMENTS]
        p = dict(zip(names, _norm_proj(x, norm1_g[l], outs, sc1, sh1, rope_tabs=rope_tabs)))
        pc = dict(zip(names, _norm_proj(ctx, norm1_g[l], outs, mc[1], mc[0])))

        lb = lbs[:, l]
        lb_terms = jnp.stack([jnp.log(lb), jnp.log1p(-lb), 1.0 - lb], axis=1)
        lam_init = 0.8 - 0.6 * math.exp(-0.3 * l)
        lat_parts, ctx_parts = _mixers(p, pc, ctx_out, prm, l, lam_init, rope_tabs, lb_terms)

        w_out_b = w_out[l].astype(BF16)
        moe_w = (l, moe_w_gate, moe_w_up, moe_w_down,
                 moe_sh_gate[l].astype(BF16), moe_sh_up[l].astype(BF16), moe_sh_down[l].astype(BF16))

        if ctx_out:
            ctx = _out_proj(*ctx_parts, pc['hg_g'], ctx, mc[2], w_out_b)
            flat = ctx.reshape(1, -1, d)
            h2c, gate_c = _router(flat, norm2_g[l], mc[4][:1], mc[3][:1], moe_w_router[l], moe_bias[l])
            ctx = _moe(h2c, flat, gate_c.transpose(0, 2, 1), mc[5][:1], *moe_w).reshape(ctx.shape)

        x = _out_proj(*lat_parts, p['hg_g'], x, g1, w_out_b)
        x = _routed_moe(x, norm2_g[l], sc2, sh2, g2, moe_w_router[l], moe_bias[l], *moe_w,
                        final_g=None if ctx_out else final_norm_g)

    return x
```

```python
import functools
import math

import numpy as np
import jax
import jax.numpy as jnp
from jax import lax
from jax.experimental import pallas as pl
from jax.experimental.pallas import tpu as pltpu
from jax.experimental.pallas import tpu_sc as plsc

F32 = jnp.float32
BF16 = jnp.bfloat16
HIGHEST = lax.Precision.HIGHEST

D_MODEL = 1024
GRID_W = 64
HY_WIDTH = 256
HY_ORDER = 2
HY_BANDS = 16
DA_HEADS = 4
DA_HEAD_DIM = 32
MLA_HEADS = 4
MLA_Q_RANK = 192
MLA_KV_RANK = 128
MLA_NOPE_DIM = 64
MLA_ROPE_DIM = 32
MLA_V_DIM = 64
HG_HEADS = 4
HG_KEY_DIM = 64
HG_VAL_DIM = 64
HG_CHUNK = 64
HG_SUB = 8
HG_UNROLL = 4
N_EXPERTS = 64
N_EXPERT_GROUPS = 8
TOPK_GROUPS = 4
TOP_K = 8
EXPERT_FF = 256
ROUTED_SCALE = 2.5
ROPE_BASE = 10000.0
NORM_EPS = 1e-6

V7X_VMEM_LIMIT_BYTES = 56 * 1024 * 1024
LANES = 128

_SEGMENTS = (
    ('hy_v', HY_WIDTH, F32, False, 0, 1), ('hy_x1', HY_WIDTH, F32, False, 0, 1), ('hy_x2', HY_WIDTH, F32, False, 0, 1),
    ('da_q', 2 * DA_HEADS * DA_HEAD_DIM, F32, True, 2 * DA_HEADS, 1),
    ('da_k', 2 * DA_HEADS * DA_HEAD_DIM, BF16, True, 2 * DA_HEADS, 1),
    ('da_v', 2 * DA_HEADS * DA_HEAD_DIM, BF16, False, DA_HEADS, 1),
    ('mla_q', MLA_Q_RANK, F32, False, 0, 1), ('mla_kv', MLA_KV_RANK, F32, False, 0, 1),
    ('mla_kr', MLA_ROPE_DIM, BF16, True, MLA_HEADS, MLA_HEADS),
    ('hg_q', HG_HEADS * HG_KEY_DIM, F32, False, 0, 1), ('hg_ff', HG_HEADS * HG_KEY_DIM, F32, False, 0, 1),
    ('hg_fb', HG_HEADS * HG_KEY_DIM, F32, False, 0, 1), ('hg_i', HG_HEADS * HG_VAL_DIM, F32, False, 0, 1),
    ('hg_g', HG_HEADS * HG_VAL_DIM, F32, False, 0, 1),
)


def _params(*semantics):
    return pltpu.CompilerParams(dimension_semantics=semantics, vmem_limit_bytes=V7X_VMEM_LIMIT_BYTES)


def _const_spec(shape):
    nd = len(shape)
    return pl.BlockSpec(shape, lambda *_: (0,) * nd)


def _rms(x, eps=NORM_EPS):
    return x * lax.rsqrt(jnp.mean(x * x, axis=-1, keepdims=True) + eps)


def _silu(x):
    return x * jax.nn.sigmoid(x)


def _dot_nt(a, b, **kw):
    return lax.dot_general(a, b, (((1,), (1,)), ((), ())), preferred_element_type=F32, **kw)


def _ada_kernel(c_ref, w_ref, b_ref, o_ref):
    s = _silu(c_ref[...])
    o_ref[...] = jnp.dot(s, w_ref[...], precision=HIGHEST, preferred_element_type=F32) + b_ref[...]


def _ada(cond, w, b):
    r, d = cond.shape
    n = w.shape[1]
    tn = 1536
    return pl.pallas_call(
        _ada_kernel,
        grid=(n // tn,),
        in_specs=[_const_spec((r, d)), pl.BlockSpec((d, tn), lambda j: (0, j)), pl.BlockSpec((1, tn), lambda j: (0, j))],
        out_specs=pl.BlockSpec((r, tn), lambda j: (0, j)),
        out_shape=jax.ShapeDtypeStruct((r, n), F32),
        compiler_params=_params('arbitrary'),
        name='ada',
    )(cond, w, b.reshape(1, n))


ROPE_UNIT = 32


def _rope_tables(row, col, width):
    n = ROPE_UNIT // 4
    inv = ROPE_BASE ** (-jnp.arange(n, dtype=F32) / n)
    units = width // ROPE_UNIT
    parts_c, parts_a, parts_b = [], [], []
    zero = jnp.zeros((row.shape[0], n), F32)
    for pos in (row, col):
        ang = pos.astype(F32)[:, None] * inv
        cos, sin = jnp.cos(ang), jnp.sin(ang)
        parts_c += [cos, cos]
        parts_a += [zero, sin]
        parts_b += [-sin, zero]
    tile = lambda ps: jnp.tile(jnp.concatenate(ps, axis=1), (1, units))
    return tile(parts_c), tile(parts_a), tile(parts_b)


def _norm_proj_kernel(*refs, n_w, modulate, ropes, splits):
    x_ref, g_ref = refs[0], refs[1]
    pos = 2
    if modulate:
        sc_ref, sh_ref = refs[2], refs[3]
        pos = 4
    if any(ropes):
        rc_ref, ra_ref, rb_ref = refs[pos:pos + 3]
        pos += 3
    w_refs = refs[pos:pos + n_w]
    o_refs = refs[pos + n_w:]
    y = _rms(x_ref[0]) * g_ref[...]
    if modulate:
        y = y * (1.0 + sc_ref[0]) + sh_ref[0]
    yb = y.astype(BF16)
    for w_ref, o_ref, rope, split in zip(w_refs, o_refs, ropes, splits):
        o = jnp.dot(yb, w_ref[...], preferred_element_type=F32)
        if rope:
            wd = o.shape[1]
            shift = ROPE_UNIT // 4
            o = (o * rc_ref[:, :wd] + pltpu.roll(o, shift, axis=1) * ra_ref[:, :wd]
                 + pltpu.roll(o, wd - shift, axis=1) * rb_ref[:, :wd])
        if split:
            unit = o.shape[1] // split
            for u in range(split):
                o_ref[0, u] = o[:, u * unit:(u + 1) * unit].astype(o_ref.dtype)
        else:
            o_ref[0] = o.astype(o_ref.dtype)


def _norm_proj(x, g, outs, scale=None, shift=None, rope_tabs=None, tm=512):
    b, s, k = x.shape
    tm = min(tm, s)
    modulate = scale is not None
    ropes = tuple(bool(o[2]) and rope_tabs is not None for o in outs)
    splits = tuple(o[3] for o in outs)
    ins = [x, g.reshape(1, k)]
    in_specs = [pl.BlockSpec((1, tm, k), lambda i, j: (i, j, 0)), _const_spec((1, k))]
    if modulate:
        ins += [scale, shift]
        in_specs += [pl.BlockSpec((1, 1, k), lambda i, j: (i, 0, 0))] * 2
    if any(ropes):
        ins += list(rope_tabs)
        in_specs += [pl.BlockSpec((tm, rope_tabs[0].shape[1]), lambda i, j: (j, 0))] * 3
    out_specs, out_shape = [], []
    for w, dt, _, split in outs:
        ins.append(w)
        in_specs.append(_const_spec(w.shape))
        n = w.shape[1]
        if split:
            out_specs.append(pl.BlockSpec((1, split, tm, n // split), lambda i, j: (i, 0, j, 0)))
            out_shape.append(jax.ShapeDtypeStruct((b, split, s, n // split), dt))
        else:
            out_specs.append(pl.BlockSpec((1, tm, n), lambda i, j: (i, j, 0)))
            out_shape.append(jax.ShapeDtypeStruct((b, s, n), dt))
    return pl.pallas_call(
        functools.partial(_norm_proj_kernel, n_w=len(outs), modulate=modulate, ropes=ropes, splits=splits),
        grid=(b, s // tm),
        in_specs=in_specs,
        out_specs=out_specs,
        out_shape=out_shape,
        compiler_params=_params('parallel', 'parallel'),
        name='norm_proj',
    )(*ins)


def _hy_filter_kernel(w1t_ref, w1s_ref, w1c_ref, b1_ref, w2_ref, b2_ref, w3_ref, b3_ref, fr_ref, dec_ref, o_ref, *, n):
    t = lax.broadcasted_iota(jnp.int32, (n, 1), 0).astype(F32) / n
    bands = lax.broadcasted_iota(jnp.int32, (1, HY_BANDS), 1).astype(F32) + 1.0
    ang = (2.0 * jnp.pi) * t * bands
    pre = (t * w1t_ref[...]
           + jnp.dot(jnp.sin(ang), w1s_ref[...], precision=HIGHEST, preferred_element_type=F32)
           + jnp.dot(jnp.cos(ang), w1c_ref[...], precision=HIGHEST, preferred_element_type=F32)
           + b1_ref[...])
    hid = jnp.sin(fr_ref[0:1, :] * pre)
    hid = jnp.sin(fr_ref[1:2, :] * (jnp.dot(hid, w2_ref[...], precision=HIGHEST, preferred_element_type=F32) + b2_ref[...]))
    filt = jnp.dot(hid, w3_ref[...], precision=HIGHEST, preferred_element_type=F32) + b3_ref[...]
    filt = filt * jnp.exp(-t * jnp.abs(dec_ref[...]))
    col = jnp.sum(jnp.abs(filt), axis=0, keepdims=True) - jnp.abs(filt[0:1, :])
    w = HY_WIDTH
    for o in range(HY_ORDER):
        lo = o * 2 * w
        f0 = filt[0:1, lo:lo + w] + filt[0:1, lo + w:lo + 2 * w]
        inv = 1.0 / (col[:, lo:lo + w] + col[:, lo + w:lo + 2 * w] + jnp.abs(f0))
        o_ref[:, lo:lo + w] = filt[:, lo:lo + w] * inv
        o_ref[:, lo + w:lo + 2 * w] = filt[:, lo + w:lo + 2 * w] * inv


def _hy_filters(n, w1, b1, w2, b2, w3, b3, freq, decay):
    cols = w3.shape[1]
    ins = [w1[0:1], w1[1:1 + HY_BANDS], w1[1 + HY_BANDS:], b1.reshape(1, -1), w2, b2.reshape(1, -1), w3,
           b3.reshape(1, -1), freq, decay.reshape(1, -1)]
    out = pl.pallas_call(
        functools.partial(_hy_filter_kernel, n=n),
        grid=(1,),
        in_specs=[_const_spec(a.shape) for a in ins],
        out_specs=_const_spec((n, cols)),
        out_shape=jax.ShapeDtypeStruct((n, cols), F32),
        compiler_params=_params('arbitrary'),
        name='hy_filter',
    )(*ins)
    return out.reshape(n, HY_ORDER, 2, HY_WIDTH)


def _two_sided(filt_n):
    n = filt_n.shape[0]
    hf, hb = filt_n[:, :, 0], filt_n[:, :, 1]
    h = jnp.concatenate([hf[:1] + hb[:1], hf[1:], jnp.zeros((1,) + hf.shape[1:], F32), hb[:0:-1]], axis=0)
    return h.reshape(2 * n, HY_ORDER * HY_WIDTH)


def _short_conv_kernel(*refs, s):
    x_refs, w_refs, b_refs, o_refs = refs[0:3], refs[3:6], refs[6:9], refs[9:12]
    row = lax.broadcasted_iota(jnp.int32, (s, 1), 0)
    for x_ref, w_ref, b_ref, o_ref in zip(x_refs, w_refs, b_refs, o_refs):
        x = x_ref[0]
        prev = jnp.where(row == 0, 0.0, pltpu.roll(x, 1, axis=0))
        nxt = jnp.where(row == s - 1, 0.0, pltpu.roll(x, s - 1, axis=0))
        o_ref[0] = prev * w_ref[0:1, :] + x * w_ref[1:2, :] + nxt * w_ref[2:3, :] + b_ref[...]


def _short_conv(parts, conv_w, conv_b):
    b, s, c = parts[0].shape
    tc = LANES
    ws = [conv_w[:, i * c:(i + 1) * c] for i in range(3)]
    bs = [conv_b[i * c:(i + 1) * c].reshape(1, c) for i in range(3)]
    xspec = pl.BlockSpec((1, s, tc), lambda i, j: (i, 0, j))
    return pl.pallas_call(
        functools.partial(_short_conv_kernel, s=s),
        grid=(b, c // tc),
        in_specs=[xspec] * 3 + [pl.BlockSpec((3, tc), lambda i, j: (0, j))] * 3 + [pl.BlockSpec((1, tc), lambda i, j: (0, j))] * 3,
        out_specs=[xspec] * 3,
        out_shape=[jax.ShapeDtypeStruct((b, s, c), F32)] * 3,
        compiler_params=_params('parallel', 'parallel'),
        name='short_conv',
    )(*parts, *ws, *bs)


def _dft_cos_sin(rows, cols, period):
    ang = 2.0 * np.pi * ((np.arange(rows)[:, None] * np.arange(cols)[None, :]) % period) / period
    return np.cos(ang), np.sin(ang)


def _fft_tables(n, inner):
    big = 2 * n
    n1 = big // inner
    c1, s1 = _dft_cos_sin(n1, n1, n1)
    h = n1 // 2
    outer_data = np.block([[c1[:, :h], s1[:, :h]], [-s1[:, :h], c1[:, :h]]])
    outer_real = np.concatenate([c1, -s1], axis=0)
    outer_inv = np.block([[c1[:h, :], -s1[:h, :]], [s1[:h, :], c1[:h, :]]]) / big
    c2, s2 = _dft_cos_sin(inner, inner, inner)
    inner_fwd = np.block([[c2, s2], [-s2, c2]])
    inner_inv = np.block([[c2, -s2], [s2, c2]])
    ct, st = _dft_cos_sin(n1, inner, big)
    f = lambda a: jnp.asarray(a, F32)
    return dict(n1=n1, inner=inner, outer_data=_hi_lo_cols(outer_data), outer_real=_hi_lo_cols(outer_real),
                outer_inv=_hi_lo_cols(outer_inv),
                inner_fwd=_hi_lo_cols(inner_fwd), inner_inv=_hi_lo_cols(inner_inv),
                tw_cos=f(ct).reshape(n1, inner, 1), tw_sin=f(st).reshape(n1, inner, 1))


def _left_mm_kernel(m_ref, x_ref, o_ref):
    o_ref[0] = jnp.dot(m_ref[...], _hi_lo_rows(x_ref[0]), preferred_element_type=F32)


def _left_mm(m, x, tl=4096):
    p, k, l = x.shape
    mm = m.shape[0]
    tl = min(tl, l)
    return pl.pallas_call(
        _left_mm_kernel,
        grid=(p, l // tl),
        in_specs=[_const_spec(m.shape), pl.BlockSpec((1, k, tl), lambda i, j: (i, 0, j))],
        out_specs=pl.BlockSpec((1, mm, tl), lambda i, j: (i, 0, j)),
        out_shape=jax.ShapeDtypeStruct((p, mm, l), F32),
        compiler_params=_params('parallel', 'parallel'),
        name='fft_outer',
    )(m, x)


def _hi_lo_cols(m):
    m = np.asarray(m, np.float32)
    hi = m.astype(BF16)
    lo = (m - hi.astype(np.float32)).astype(BF16)
    return jnp.asarray(np.concatenate([hi, hi, lo], axis=1))


def _hi_lo_rows(x):
    hi = x.astype(BF16)
    lo = (x - hi.astype(F32)).astype(BF16)
    return jnp.concatenate([hi, lo, hi], axis=0)


def _inner_kernel(a_ref, twc_ref, tws_ref, gf_ref, *rest, convolve, inner, kb):
    for s in range(kb):
        ar, ai = a_ref[0, 0, s], a_ref[0, 1, s]
        tc, ts = twc_ref[s], tws_ref[s]
        br = ar * tc + ai * ts
        bi = ai * tc - ar * ts
        x = jnp.dot(gf_ref[...], _hi_lo_rows(jnp.concatenate([br, bi], axis=0)), preferred_element_type=F32)
        if not convolve:
            o_ref = rest[0]
            o_ref[0, 0, s] = x[:inner]
            o_ref[0, 1, s] = x[inner:]
            continue
        h_ref, gi_ref, o_ref = rest
        xr, xi = x[:inner], x[inner:]
        hr, hi = h_ref[0, 0, s], h_ref[0, 1, s]
        yr = xr * hr - xi * hi
        yi = xr * hi + xi * hr
        z = jnp.dot(gi_ref[...], _hi_lo_rows(jnp.concatenate([yr, yi], axis=0)), preferred_element_type=F32)
        zr, zi = z[:inner], z[inner:]
        o_ref[0, 0, s] = zr * tc - zi * ts
        o_ref[0, 1, s] = zi * tc + zr * ts


def _fft_inner(a, tab, c, h=None, h_block=0):
    p = a.shape[0]
    n1, inner = tab['n1'], tab['inner']
    a5 = a.reshape(p, 2, n1, inner, c)
    tc = 2 * LANES
    kb = 4
    blk = pl.BlockSpec((1, 2, kb, inner, tc), lambda k, j, i: (i, 0, k, 0, j))
    tw_spec = pl.BlockSpec((kb, inner, 1), lambda k, j, i: (k, 0, 0))
    ins = [a5, tab['tw_cos'], tab['tw_sin'], tab['inner_fwd']]
    in_specs = [blk, tw_spec, tw_spec, _const_spec(tab['inner_fwd'].shape)]
    if h is not None:
        ch = h.shape[-1] // inner
        nb = c // tc
        ins += [h.reshape(1, 2, n1, inner, ch), tab['inner_inv']]
        in_specs += [pl.BlockSpec((1, 2, kb, inner, tc), lambda k, j, i: (0, 0, k, 0, h_block * nb + j)),
                     _const_spec(tab['inner_inv'].shape)]
    out = pl.pallas_call(
        functools.partial(_inner_kernel, convolve=h is not None, inner=inner, kb=kb),
        grid=(n1 // kb, c // tc, p),
        in_specs=in_specs,
        out_specs=blk,
        out_shape=jax.ShapeDtypeStruct(a5.shape, F32),
        compiler_params=_params('parallel', 'parallel', 'parallel'),
        name='fft_inner',
    )(*ins)
    return out.reshape(p, 2 * n1, inner * c)


def _gate_kernel(m_ref, z_ref, u_ref, x_ref, bias_ref, *rest, chain):
    y = jnp.dot(m_ref[...], _hi_lo_rows(z_ref[0]), preferred_element_type=F32)
    nxt = x_ref[0] * (y + u_ref[0] * bias_ref[...])
    if chain:
        mf_ref, o_ref, a_ref = rest
        o_ref[0] = nxt
        a_ref[0] = jnp.dot(mf_ref[...], _hi_lo_rows(nxt), preferred_element_type=F32)
    else:
        rest[0][0] = nxt


def _fft_gate(tab, z, u, x, bias_l, chain, tl=4096):
    p, k2, l = z.shape
    n1 = tab['n1']
    tl = min(tl, l)
    row = pl.BlockSpec((1, n1, tl), lambda i, j: (i, 0, j))
    ins = [tab['outer_inv'], z, u, x, bias_l]
    in_specs = [_const_spec(tab['outer_inv'].shape), pl.BlockSpec((1, k2, tl), lambda i, j: (i, 0, j)), row, row,
                pl.BlockSpec((1, tl), lambda i, j: (0, j))]
    out_specs = [row]
    out_shape = [jax.ShapeDtypeStruct((p, n1, l), F32)]
    if chain:
        ins.append(tab['outer_data'])
        in_specs.append(_const_spec(tab['outer_data'].shape))
        out_specs.append(pl.BlockSpec((1, k2, tl), lambda i, j: (i, 0, j)))
        out_shape.append(jax.ShapeDtypeStruct((p, k2, l), F32))
    return pl.pallas_call(
        functools.partial(_gate_kernel, chain=chain),
        grid=(p, l // tl),
        in_specs=in_specs,
        out_specs=out_specs,
        out_shape=out_shape,
        compiler_params=_params('parallel', 'parallel'),
        name='fft_gate',
    )(*ins)


def _hyena(parts, conv_w, conv_b, filt_n, bias, inner):
    b, s, c = parts[0].shape
    tab = _fft_tables(s, inner)
    n1 = tab['n1']
    lanes = inner * c
    h_taps = _two_sided(filt_n).reshape(1, n1, inner * HY_ORDER * c)
    h_spec = _fft_inner(_left_mm(tab['outer_real'], h_taps), tab, HY_ORDER * c)
    v, x1, x2 = [a.reshape(b // 2, n1, lanes) for a in _short_conv(parts, conv_w, conv_b)]
    bias_l = [jnp.tile(bias[o], inner).reshape(1, lanes) for o in range(HY_ORDER)]
    a = _left_mm(tab['outer_data'], v)
    z = _fft_inner(a, tab, c, h_spec, 0)
    z2, a = _fft_gate(tab, z, v, x1, bias_l[0], chain=True)
    z = _fft_inner(a, tab, c, h_spec, 1)
    (z3,) = _fft_gate(tab, z, z2, x2, bias_l[1], chain=False)
    return z3.reshape(b, s, c)


def _attn_kernel(*refs, n_q, n_pieces, ncomp, scale, post_scale):
    q_refs = refs[:n_q]
    pos = n_q
    pieces = []
    for _ in range(n_pieces):
        pieces.append((refs[pos:pos + n_q], refs[pos + n_q]))
        pos += n_q + 1
    if ncomp == 2:
        lam_ref, g_ref = refs[pos:pos + 2]
        pos += 2
    o_ref, kcat_ref, vcat_ref = refs[pos:pos + 3]
    dv = o_ref.shape[3]

    @pl.when(pl.program_id(2) == 0)
    def _():
        row = 0
        for k_refs, v_ref in pieces:
            n = v_ref.shape[2]
            for c in range(ncomp):
                parts = [k_ref[0, c if k_ref.shape[1] == ncomp else 0] for k_ref in k_refs]
                kcat_ref[c, row:row + n, :] = parts[0] if n_q == 1 else jnp.concatenate(parts, axis=1)
            vcat_ref[row:row + n, :dv] = v_ref[0, 0]
            vcat_ref[row:row + n, dv:] = jnp.ones((n, dv), BF16)
            row += n

    outs = []
    for c in range(ncomp):
        q = q_refs[0][0, c] if n_q == 1 else jnp.concatenate([q_ref[0, c] for q_ref in q_refs], axis=1)
        s = _dot_nt((q * (scale * math.log2(math.e))).astype(BF16), kcat_ref[c])
        m = jnp.max(s, axis=-1, keepdims=True)
        p = jnp.exp2((s - m).astype(BF16))
        ol = jnp.dot(p, vcat_ref[...], preferred_element_type=F32)
        outs.append(ol[:, :dv] / ol[:, dv:dv + 1])
    if ncomp == 2:
        o = outs[0] - lam_ref[0] * outs[1]
        o = _rms(o) * g_ref[...] * post_scale
    else:
        o = outs[0]
    o_ref[0, 0] = o


def _attention(q_parts, pieces, heads, ncomp, scale, tq=256, lam=None, subln_g=None, post_scale=1.0):
    b, _, sq, _ = q_parts[0].shape
    dv = pieces[0][1].shape[3]
    tq = min(tq, sq)
    ins = list(q_parts)
    in_specs = [pl.BlockSpec((1, ncomp, tq, q.shape[3]), lambda i, h, j: (i, h, j, 0)) for q in q_parts]
    for k_parts, v in pieces:
        for k in k_parts:
            ins.append(k)
            if k.shape[1] == 1:
                in_specs.append(pl.BlockSpec((1, 1) + k.shape[2:], lambda i, h, j: (i, 0, 0, 0)))
            else:
                in_specs.append(pl.BlockSpec((1, ncomp) + k.shape[2:], lambda i, h, j: (i, h, 0, 0)))
        ins.append(v)
        in_specs.append(pl.BlockSpec((1, 1) + v.shape[2:], lambda i, h, j: (i, h, 0, 0)))
    if ncomp == 2:
        ins += [lam.reshape(1), subln_g.reshape(1, dv)]
        in_specs += [pl.BlockSpec(memory_space=pltpu.SMEM), _const_spec((1, dv))]
    sk = sum(v.shape[2] for _, v in pieces)
    dqk = sum(q.shape[3] for q in q_parts)
    return pl.pallas_call(
        functools.partial(_attn_kernel, n_q=len(q_parts), n_pieces=len(pieces), ncomp=ncomp, scale=scale,
                          post_scale=post_scale),
        grid=(b, heads, sq // tq),
        in_specs=in_specs,
        out_specs=pl.BlockSpec((1, 1, tq, dv), lambda i, h, j: (i, h, j, 0)),
        out_shape=jax.ShapeDtypeStruct((b, heads, sq, dv), F32),
        scratch_shapes=[pltpu.VMEM((ncomp, sk, dqk), BF16), pltpu.VMEM((sk, 2 * dv), BF16)],
        compiler_params=_params('parallel', 'parallel', 'arbitrary'),
        name='attention',
    )(*ins)


def _forget_terms(f, log_lb, log_1m_lb, one_m_lb):
    log_sig = jnp.minimum(f, 0.0) - jnp.log1p(jnp.exp(-jnp.abs(f)))
    b = log_1m_lb + log_sig
    log_g = jnp.maximum(log_lb, b) + jnp.log1p(jnp.exp(-jnp.abs(log_lb - b)))
    return log_g, one_m_lb * jax.nn.sigmoid(-f)


def _hg_tables():
    ck, sub = HG_CHUNK, HG_SUB
    t = np.arange(ck)
    cum_mats, half_masks, group_masks, keeps = [], [], [], []
    for rev in (False, True):
        mats = [(t[None, :] >= t[:, None]) if rev else (t[None, :] <= t[:, None])]
        halves = []
        hs = ck // 2
        while hs >= sub:
            pos = t % (2 * hs)
            b = t - pos + hs
            mats.append((t[None, :] >= b[:, None]) if rev else (t[None, :] < b[:, None]))
            q_half = (pos < hs) if rev else (pos >= hs)
            halves.append(np.stack([q_half, ~q_half]))
            if not rev:
                grp = (t[:, None] // (2 * hs)) == (t[None, :] // (2 * hs))
                group_masks.append(np.concatenate([grp, grp], axis=0))
            hs //= 2
        cum_mats.append(np.concatenate(mats, axis=0))
        half_masks.append(np.stack(halves))
        c = np.arange(ck * sub)
        tt, ss = (c // sub) % sub, c % sub
        keeps.append((ss >= tt) if rev else (ss <= tt))
    lanes = 2 * HG_KEY_DIM
    ln = np.arange(lanes)
    bd = (ln[:, None] // HG_KEY_DIM) == (ln[None, :] // HG_KEY_DIM)
    hm = np.broadcast_to(np.stack(half_masks)[..., None], (2, len(half_masks[0]), 2, ck, lanes))
    return (jnp.asarray(np.stack(cum_mats), BF16), jnp.asarray(hm, F32), jnp.asarray(np.stack(group_masks), F32),
            jnp.asarray(bd, F32), jnp.asarray(np.broadcast_to(np.stack(keeps)[..., None], (2, ck * sub, lanes)), F32))


def _split2(x):
    a = x.astype(BF16)
    return a, (x - a.astype(F32)).astype(BF16)


def _hg_chunk(q, k, v, lg, st, rev, cm, hm, gm, keep, bd, m0, m1):
    ck, sub = HG_CHUNK, HG_SUB
    lanes = lg.shape[1]
    c2 = jnp.dot(cm, jnp.concatenate(_split2(lg), axis=1), preferred_element_type=F32)
    call = c2[:, :lanes] + c2[:, lanes:]
    cum = call[0:ck]
    tot = cum[0:1] if rev else cum[ck - 1:ck]
    o = _dot_nt((q * jnp.exp(cum)).astype(BF16), st.astype(BF16))
    kd = (k * jnp.exp(tot - cum)).astype(BF16)
    st_new = st * jnp.exp(tot) + bd * jnp.dot(v.T.astype(BF16), kd, preferred_element_type=F32)
    s2 = None
    for lv in range(gm.shape[0]):
        cb = call[(lv + 1) * ck:(lv + 2) * ck]
        qd = q * jnp.exp(jnp.minimum(cum - cb, 0.0)) * hm[lv, 0]
        kf = (k * jnp.exp(jnp.minimum(cb - cum, 0.0)) * hm[lv, 1]).astype(BF16)
        q2 = jnp.concatenate([qd * m0, qd * m1], axis=0).astype(BF16)
        term = _dot_nt(q2, kf) * gm[lv]
        s2 = term if s2 is None else s2 + term
    r = jnp.dot(s2.astype(BF16), v.astype(BF16), preferred_element_type=F32)
    o = o + m0 * r[:ck] + m1 * r[ck:]
    rows, vts = [], []
    for i in range(ck // sub):
        lo, hi = i * sub, (i + 1) * sub
        ki, ci = k[lo:hi], cum[lo:hi]
        for t in range(lo, hi):
            rows.append(q[t:t + 1] * ki * jnp.exp(jnp.minimum(cum[t:t + 1] - ci, 0.0)))
            vts.append(v[lo:hi])
    sc = jnp.dot(jnp.concatenate(rows, axis=0).astype(BF16), bd.astype(BF16), preferred_element_type=F32)
    prod = sc * jnp.concatenate(vts, axis=0) * keep
    o = o + jnp.sum(prod.reshape(ck, sub, lanes), axis=1)
    return o, st_new


def _hgrn_kernel(q_ref, ff_ref, fb_ref, i_ref, qc_ref, ffc_ref, fbc_ref, ic_ref, lb_ref, g_ref,
                 cm_ref, hm_ref, gm_ref, bd_ref, keep_ref, o_ref, oc_ref, or_ref, ocr_ref, st_ref, *, n_lat, n_ctx):
    ck = HG_CHUNK
    lanes = o_ref.shape[-1]
    lane = lax.broadcasted_iota(jnp.int32, (1, lanes), 1)
    m0 = (lane < HG_KEY_DIM).astype(F32)
    m1 = 1.0 - m0
    bd = bd_ref[...]
    gm = gm_ref[...]

    def one(q, f, v, rev):
        d = 1 if rev else 0
        lg, k = _forget_terms(f, lb_ref[d, 0:1, :], lb_ref[d, 1:2, :], lb_ref[d, 2:3, :])
        o, st = _hg_chunk(q, k, v, lg, st_ref[d], rev, cm_ref[d], hm_ref[d], gm, keep_ref[d], bd, m0, m1)
        st_ref[d] = st
        return o

    def sweep(qr, ffr, fbr, ir, out_f, out_r, n):
        nc = n // ck

        def body(step, carry):
            idf = pl.ds(pl.multiple_of(step * ck, ck), ck)
            idr = pl.ds(pl.multiple_of((nc - 1 - step) * ck, ck), ck)
            out_f[0, idf, :] = one(qr[0, idf, :], ffr[0, idf, :], ir[0, idf, :], False)
            out_r[idr, :] = one(qr[0, idr, :], fbr[0, idr, :], ir[0, idr, :], True)
            return carry

        lax.fori_loop(0, nc, body, 0, unroll=HG_UNROLL)

    st_ref[...] = jnp.zeros(st_ref.shape, F32)
    sweep(qc_ref, ffc_ref, fbc_ref, ic_ref, oc_ref, ocr_ref, n_ctx)
    sweep(q_ref, ff_ref, fb_ref, i_ref, o_ref, or_ref, n_lat)

    mean_mat = bd * (1.0 / HG_VAL_DIM)

    def readout(out, out_r, n):
        tile = min(n, 512)

        def body(step, carry):
            idx = pl.ds(pl.multiple_of(step * tile, tile), tile)
            x = out[0, idx, :] + out_r[idx, :]
            ms = jnp.dot(x * x, mean_mat, precision=HIGHEST, preferred_element_type=F32)
            out[0, idx, :] = x * lax.rsqrt(ms + NORM_EPS) * g_ref[...]
            return carry

        lax.fori_loop(0, n // tile, body, 0)

    readout(oc_ref, ocr_ref, n_ctx)
    readout(o_ref, or_ref, n_lat)


def _hgrn(q, ff, fb, iv, qc, ffc, fbc, ic, lb_terms, norm_g):
    b, n_lat, width = q.shape
    n_ctx = qc.shape[1]
    lanes = 2 * HG_KEY_DIM
    tables = _hg_tables()
    lat = pl.BlockSpec((1, n_lat, lanes), lambda i, j: (i, 0, j))
    ctx = pl.BlockSpec((1, n_ctx, lanes), lambda i, j: (i, 0, j))
    g2 = jnp.tile(norm_g, 2).reshape(1, lanes)
    return pl.pallas_call(
        functools.partial(_hgrn_kernel, n_lat=n_lat, n_ctx=n_ctx),
        grid=(b, width // lanes),
        in_specs=[lat] * 4 + [ctx] * 4 + [pl.BlockSpec((2, 3, lanes), lambda i, j: (0, 0, j)), _const_spec((1, lanes))]
                 + [_const_spec(t.shape) for t in tables],
        out_specs=[lat, ctx],
        out_shape=[jax.ShapeDtypeStruct(q.shape, F32), jax.ShapeDtypeStruct(qc.shape, F32)],
        scratch_shapes=[pltpu.VMEM((n_lat, lanes), F32), pltpu.VMEM((n_ctx, lanes), F32), pltpu.VMEM((2, lanes, lanes), F32)],
        compiler_params=_params('parallel', 'parallel'),
        name='hgrn2',
    )(q, ff, fb, iv, qc, ffc, fbc, ic, lb_terms, g2, *tables)


def _out_proj_kernel(hy_ref, da_ref, mla_ref, hg_ref, gate_ref, x_ref, g1_ref, w_ref, o_ref):
    c = hy_ref.shape[2]
    acc = jnp.dot(hy_ref[0].astype(BF16), w_ref[0:c, :], preferred_element_type=F32)
    for i, head_ref in ((1, da_ref), (2, mla_ref)):
        dv = head_ref.shape[3]
        for h in range(head_ref.shape[1]):
            lo = i * c + h * dv
            acc = acc + jnp.dot(head_ref[0, h].astype(BF16), w_ref[lo:lo + dv, :], preferred_element_type=F32)
    hg = hg_ref[0] * _silu(gate_ref[0])
    acc = acc + jnp.dot(hg.astype(BF16), w_ref[3 * c:4 * c, :], preferred_element_type=F32)
    o_ref[0] = x_ref[0] + g1_ref[0] * acc


def _out_proj(y_hy, y_da, y_mla, y_hg, gate, x, g1, w_out, tm=512):
    b, s, d = x.shape
    tm = min(tm, s)
    c = y_hy.shape[2]
    part = pl.BlockSpec((1, tm, c), lambda i, j: (i, j, 0))
    headed = lambda a: pl.BlockSpec((1, a.shape[1], tm, a.shape[3]), lambda i, j: (i, 0, j, 0))
    row = pl.BlockSpec((1, tm, d), lambda i, j: (i, j, 0))
    return pl.pallas_call(
        _out_proj_kernel,
        grid=(b, s // tm),
        in_specs=[part, headed(y_da), headed(y_mla), part, part, row, pl.BlockSpec((1, 1, d), lambda i, j: (i, 0, 0)),
                  _const_spec(w_out.shape)],
        out_specs=row,
        out_shape=jax.ShapeDtypeStruct(x.shape, F32),
        compiler_params=_params('parallel', 'parallel'),
        name='out_proj',
    )(y_hy, y_da, y_mla, y_hg, gate, x, g1, w_out)


def _router_kernel(x_ref, g_ref, sc_ref, sh_ref, wrt_ref, bias_ref, *rest, compact):
    h = _rms(x_ref[0]) * g_ref[...] * (1.0 + sc_ref[0]) + sh_ref[0]
    tm = h.shape[0]
    scores = jax.nn.sigmoid(_dot_nt(wrt_ref[...], h, precision=HIGHEST))
    choice = scores + bias_ref[...]
    per = N_EXPERTS // N_EXPERT_GROUPS
    neg = -jnp.inf
    iota_g = lax.broadcasted_iota(jnp.int32, (per, tm), 0)
    grp_rows = []
    for gi in range(N_EXPERT_GROUPS):
        blk = choice[gi * per:(gi + 1) * per]
        m1 = jnp.max(blk, axis=0, keepdims=True)
        first = jnp.min(jnp.where(blk == m1, iota_g, per), axis=0, keepdims=True)
        m2 = jnp.max(jnp.where(iota_g == first, neg, blk), axis=0, keepdims=True)
        grp_rows.append(m1 + m2)
    grp = jnp.concatenate(grp_rows, axis=0)
    iota_n = lax.broadcasted_iota(jnp.int32, (N_EXPERT_GROUPS, tm), 0)
    gsel = jnp.zeros((N_EXPERT_GROUPS, tm), F32)
    for _ in range(TOPK_GROUPS):
        m = jnp.max(grp, axis=0, keepdims=True)
        first = jnp.min(jnp.where(grp == m, iota_n, N_EXPERT_GROUPS), axis=0, keepdims=True)
        hit = iota_n == first
        gsel = jnp.where(hit, 1.0, gsel)
        grp = jnp.where(hit, neg, grp)
    emask = jnp.concatenate([jnp.broadcast_to(gsel[gi:gi + 1], (per, tm)) for gi in range(N_EXPERT_GROUPS)], axis=0)
    cand = jnp.where(emask > 0.0, choice, neg)
    iota_e = lax.broadcasted_iota(jnp.int32, (N_EXPERTS, tm), 0)
    sel = jnp.zeros((N_EXPERTS, tm), F32)
    chosen = []
    for _ in range(TOP_K):
        m = jnp.max(cand, axis=0, keepdims=True)
        first = jnp.min(jnp.where(cand == m, iota_e, N_EXPERTS), axis=0, keepdims=True)
        hit = iota_e == first
        sel = jnp.where(hit, 1.0, sel)
        cand = jnp.where(hit, neg, cand)
        chosen.append(first)
    w = scores * sel
    gate = w / jnp.sum(w, axis=0, keepdims=True) * ROUTED_SCALE
    if not compact:
        h_ref, gate_ref = rest
        h_ref[0] = h.astype(BF16)
        gate_ref[0] = gate
        return
    hp_ref, eid_ref, rank_ref, w_ref, cnt_out_ref, cnt_ref = rest
    hp_ref[0] = _pack_halves(h)

    @pl.when((pl.program_id(0) == 0) & (pl.program_id(1) == 0))
    def _():
        cnt_ref[...] = jnp.zeros(cnt_ref.shape, F32)

    src = lax.broadcasted_iota(jnp.int32, (tm, tm), 0)
    dst = lax.broadcasted_iota(jnp.int32, (tm, tm), 1)
    running = jnp.dot(sel.astype(BF16), (src <= dst).astype(BF16), preferred_element_type=F32)
    rank_dense = cnt_ref[:, 0:1] + running - 1.0
    e_rows, r_rows, w_rows = [], [], []
    for first in chosen:
        hit = iota_e == first
        e_rows.append(first)
        r_rows.append(jnp.sum(jnp.where(hit, rank_dense, 0.0), axis=0, keepdims=True))
        w_rows.append(jnp.sum(jnp.where(hit, gate, 0.0), axis=0, keepdims=True))
    eid_ref[...] = jnp.concatenate(e_rows, axis=0)
    rank_ref[...] = jnp.concatenate(r_rows, axis=0).astype(jnp.int32)
    w_ref[...] = jnp.concatenate(w_rows, axis=0)
    cnt_ref[...] = cnt_ref[...] + running[:, tm - 1:tm]
    cnt_out_ref[...] = cnt_ref[...]


def _router(x, g, scale, shift, w_router, e_bias, tm=512, compact=False):
    b, s, d = x.shape
    tm = min(tm, s)
    e = w_router.shape[1]
    row = pl.BlockSpec((1, tm, d), lambda i, j: (i, j, 0))
    mod = pl.BlockSpec((1, 1, d), lambda i, j: (i, 0, 0))
    if compact:
        nj = s // tm
        tok = pl.BlockSpec((TOP_K, tm), lambda i, j: (0, i * nj + j))
        out_specs = [pl.BlockSpec((1, tm, d // 2), lambda i, j: (i, j, 0)), tok, tok, tok, _const_spec((e, LANES))]
        out_shape = [jax.ShapeDtypeStruct((b, s, d // 2), jnp.int32), jax.ShapeDtypeStruct((TOP_K, b * s), jnp.int32),
                     jax.ShapeDtypeStruct((TOP_K, b * s), jnp.int32), jax.ShapeDtypeStruct((TOP_K, b * s), F32),
                     jax.ShapeDtypeStruct((e, LANES), F32)]
        scratch = [pltpu.VMEM((e, LANES), F32)]
        semantics = ('arbitrary', 'arbitrary')
    else:
        out_specs = [row, pl.BlockSpec((1, e, tm), lambda i, j: (i, 0, j))]
        out_shape = [jax.ShapeDtypeStruct((b, s, d), BF16), jax.ShapeDtypeStruct((b, e, s), F32)]
        scratch = []
        semantics = ('parallel', 'parallel')
    return pl.pallas_call(
        functools.partial(_router_kernel, compact=compact),
        grid=(b, s // tm),
        in_specs=[row, _const_spec((1, d)), mod, mod, _const_spec((e, d)), _const_spec((e, 1))],
        out_specs=out_specs,
        out_shape=out_shape,
        scratch_shapes=scratch,
        compiler_params=_params(*semantics),
        name='router',
    )(x, g.reshape(1, d), scale, shift, w_router.T, e_bias.reshape(e, 1))


def _moe_kernel(h_ref, x_ref, gate_ref, g2_ref, wg_ref, wu_ref, wd_ref, sg_ref, su_ref, sd_ref, *rest, final):
    if final:
        fg_ref, o_ref, acc_ref = rest
    else:
        o_ref, acc_ref = rest
    e = pl.program_id(2)
    h = h_ref[0]

    @pl.when(e == 0)
    def _():
        a = jnp.dot(h, sg_ref[...], preferred_element_type=F32)
        u = jnp.dot(h, su_ref[...], preferred_element_type=F32)
        acc_ref[...] = jnp.dot((_silu(a) * u).astype(BF16), sd_ref[...], preferred_element_type=F32)

    lane = lax.broadcasted_iota(jnp.int32, gate_ref.shape[1:], 1)
    gcol = jnp.sum(jnp.where(lane == e, gate_ref[0], 0.0), axis=-1, keepdims=True)
    a = jnp.dot(h, wg_ref[0].astype(BF16), preferred_element_type=F32)
    u = jnp.dot(h, wu_ref[0].astype(BF16), preferred_element_type=F32)
    acc_ref[...] += jnp.dot((_silu(a) * u * gcol).astype(BF16), wd_ref[0].astype(BF16), preferred_element_type=F32)

    @pl.when(e == pl.num_programs(2) - 1)
    def _():
        y = x_ref[0] + g2_ref[0] * acc_ref[...]
        if final:
            y = _rms(y) * fg_ref[...]
        o_ref[0] = y


def _moe(h2, x, gate, g2, layer, w_gate, w_up, w_down, s_gate, s_up, s_down, final_g=None, tm=1024):
    b, s, d = x.shape
    tm = min(tm, s)
    _, e, _, ff = w_gate.shape
    row = pl.BlockSpec((1, tm, d), lambda i, j, k: (i, j, 0))
    ins = [h2, x, gate, g2, w_gate, w_up, w_down, s_gate, s_up, s_down]
    in_specs = [row, row, pl.BlockSpec((1, tm, e), lambda i, j, k: (i, j, 0)),
                pl.BlockSpec((1, 1, d), lambda i, j, k: (i, 0, 0)),
                pl.BlockSpec((None, 1, d, ff), lambda i, j, k: (layer, k, 0, 0)),
                pl.BlockSpec((None, 1, d, ff), lambda i, j, k: (layer, k, 0, 0)),
                pl.BlockSpec((None, 1, ff, d), lambda i, j, k: (layer, k, 0, 0)),
                _const_spec(s_gate.shape), _const_spec(s_up.shape), _const_spec(s_down.shape)]
    if final_g is not None:
        ins.append(final_g.reshape(1, d))
        in_specs.append(_const_spec((1, d)))
    return pl.pallas_call(
        functools.partial(_moe_kernel, final=final_g is not None),
        grid=(b, s // tm, e),
        in_specs=in_specs,
        out_specs=row,
        out_shape=jax.ShapeDtypeStruct(x.shape, F32),
        scratch_shapes=[pltpu.VMEM((tm, d), F32)],
        compiler_params=_params('parallel', 'parallel', 'arbitrary'),
        name='moe',
    )(*ins)


MOE_ROW_TILE = 512
SC_ROWS = 128
V7X_SC_CORES = 2
V7X_SC_SUBCORES = 16


def _pack_halves(x):
    n = x.shape[1] // 2
    lo = pltpu.bitcast(x[:, :n].astype(BF16).astype(F32), jnp.int32)
    hi = pltpu.bitcast(x[:, n:].astype(BF16).astype(F32), jnp.int32)
    return jnp.bitwise_or(jnp.bitwise_and(hi, -65536), lax.shift_right_logical(lo, 16))


def _unpack_halves(p):
    lo = pltpu.bitcast(lax.shift_left(p, 16), F32).astype(BF16)
    hi = pltpu.bitcast(jnp.bitwise_and(p, -65536), F32).astype(BF16)
    return lo, hi


def _route_pos_kernel(off_ref, eid_ref, rank_ref, pos_ref):
    eid = eid_ref[...]
    base = jnp.zeros(eid.shape, jnp.int32)
    for e in range(N_EXPERTS):
        base = jnp.where(eid == e, off_ref[e], base)
    pos_ref[...] = base + rank_ref[...]


def _route_pos(offsets, eid, rank):
    return pl.pallas_call(
        _route_pos_kernel,
        grid=(1,),
        in_specs=[pl.BlockSpec(memory_space=pltpu.SMEM), _const_spec(eid.shape), _const_spec(rank.shape)],
        out_specs=_const_spec(eid.shape),
        out_shape=jax.ShapeDtypeStruct(eid.shape, jnp.int32),
        compiler_params=_params('arbitrary'),
        name='route_pos',
    )(offsets, eid, rank)


def _sc_mesh():
    return plsc.VectorSubcoreMesh(core_axis_name='c', subcore_axis_name='s', num_cores=V7X_SC_CORES,
                                  num_subcores=V7X_SC_SUBCORES)


def _sc_dispatch(hp, pos, n_rows):
    t, w = hp.shape
    k = pos.shape[0]
    workers = V7X_SC_CORES * V7X_SC_SUBCORES
    per_worker = t // workers
    pos_flat = pos.reshape(k * t)

    @functools.partial(pl.kernel, mesh=_sc_mesh(), out_type=jax.ShapeDtypeStruct((n_rows, w), jnp.int32),
                       scratch_types=[pltpu.VMEM((SC_ROWS,), jnp.int32), pltpu.VMEM((SC_ROWS, w), jnp.int32),
                                      pltpu.SemaphoreType.DMA])
    def scatter(hp_hbm, pos_hbm, out_hbm, idx_v, rows_v, sem):
        wid = lax.axis_index('s') * V7X_SC_CORES + lax.axis_index('c')

        @pl.loop(0, per_worker // SC_ROWS)
        def _(i):
            t0 = pl.multiple_of(wid * per_worker + i * SC_ROWS, SC_ROWS)
            pltpu.sync_copy(hp_hbm.at[pl.ds(t0, SC_ROWS)], rows_v)
            for j in range(k):
                pltpu.sync_copy(pos_hbm.at[pl.ds(pl.multiple_of(j * t + t0, SC_ROWS), SC_ROWS)], idx_v)
                pltpu.async_copy(rows_v, out_hbm.at[idx_v], sem).wait()

    return scatter(hp, pos_flat)


def _sc_collect(yp, pos):
    _, w = yp.shape
    k, t = pos.shape
    workers = V7X_SC_CORES * V7X_SC_SUBCORES
    per_worker = k * t // workers
    pos_flat = pos.reshape(k * t)

    @functools.partial(pl.kernel, mesh=_sc_mesh(), out_type=jax.ShapeDtypeStruct((k * t, w), jnp.int32),
                       scratch_types=[pltpu.VMEM((SC_ROWS,), jnp.int32), pltpu.VMEM((SC_ROWS, w), jnp.int32),
                                      pltpu.SemaphoreType.DMA])
    def gather(yp_hbm, pos_hbm, out_hbm, idx_v, rows_v, sem):
        wid = lax.axis_index('s') * V7X_SC_CORES + lax.axis_index('c')

        @pl.loop(0, per_worker // SC_ROWS)
        def _(i):
            r0 = pl.multiple_of(wid * per_worker + i * SC_ROWS, SC_ROWS)
            pltpu.sync_copy(pos_hbm.at[pl.ds(r0, SC_ROWS)], idx_v)
            pltpu.async_copy(yp_hbm.at[idx_v], rows_v, sem).wait()
            pltpu.sync_copy(rows_v, out_hbm.at[pl.ds(r0, SC_ROWS)])

    return gather(yp, pos_flat)


def _expert_kernel(te_ref, nu_ref, x_ref, wg_ref, wu_ref, wd_ref, o_ref, wg_s, wu_s, wd_s):
    i = pl.program_id(0)

    @pl.when(i < nu_ref[0])
    def _():
        @pl.when((i == 0) | (te_ref[i] != te_ref[jnp.maximum(i - 1, 0)]))
        def _():
            wg_s[...] = wg_ref[0].astype(BF16)
            wu_s[...] = wu_ref[0].astype(BF16)
            wd_s[...] = wd_ref[0].astype(BF16)

        lo, hi = _unpack_halves(x_ref[...])
        half = lo.shape[1]
        a = (jnp.dot(lo, wg_s[:half, :], preferred_element_type=F32)
             + jnp.dot(hi, wg_s[half:, :], preferred_element_type=F32))
        u = (jnp.dot(lo, wu_s[:half, :], preferred_element_type=F32)
             + jnp.dot(hi, wu_s[half:, :], preferred_element_type=F32))
        y = jnp.dot((_silu(a) * u).astype(BF16), wd_s[...], preferred_element_type=F32)
        o_ref[...] = _pack_halves(y)


def _experts(xp, tile_expert, n_used, layer, w_gate, w_up, w_down):
    n_rows, half = xp.shape
    _, _, d, ff = w_gate.shape
    r = MOE_ROW_TILE
    row = pl.BlockSpec((r, half), lambda i, te, nu: (i, 0))
    grid_spec = pltpu.PrefetchScalarGridSpec(
        num_scalar_prefetch=2,
        grid=(n_rows // r,),
        in_specs=[row,
                  pl.BlockSpec((None, 1, d, ff), lambda i, te, nu: (layer, te[i], 0, 0)),
                  pl.BlockSpec((None, 1, d, ff), lambda i, te, nu: (layer, te[i], 0, 0)),
                  pl.BlockSpec((None, 1, ff, d), lambda i, te, nu: (layer, te[i], 0, 0))],
        out_specs=row,
        scratch_shapes=[pltpu.VMEM((d, ff), BF16), pltpu.VMEM((d, ff), BF16), pltpu.VMEM((ff, d), BF16)],
    )
    return pl.pallas_call(
        _expert_kernel,
        grid_spec=grid_spec,
        out_shape=jax.ShapeDtypeStruct((n_rows, half), jnp.int32),
        compiler_params=_params('arbitrary'),
        name='experts',
    )(tile_expert, n_used, xp, w_gate, w_up, w_down)


def _combine_kernel(yg_ref, w_ref, hp_ref, x_ref, g2_ref, sg_ref, su_ref, sd_ref, *rest, final):
    if final:
        fg_ref, o_ref = rest
    else:
        (o_ref,) = rest
    half = hp_ref.shape[2]
    lo, hi = _unpack_halves(hp_ref[0])
    sg, su = sg_ref[...], su_ref[...]
    a = jnp.dot(lo, sg[:half], preferred_element_type=F32) + jnp.dot(hi, sg[half:], preferred_element_type=F32)
    u = jnp.dot(lo, su[:half], preferred_element_type=F32) + jnp.dot(hi, su[half:], preferred_element_type=F32)
    acc = jnp.dot((_silu(a) * u).astype(BF16), sd_ref[...], preferred_element_type=F32)
    acc_lo, acc_hi = acc[:, :half], acc[:, half:]
    wts = w_ref[0]
    for k in range(yg_ref.shape[0]):
        ylo, yhi = _unpack_halves(yg_ref[k, 0])
        wk = wts[:, k:k + 1]
        acc_lo = acc_lo + wk * ylo.astype(F32)
        acc_hi = acc_hi + wk * yhi.astype(F32)
    y = x_ref[0] + g2_ref[0] * jnp.concatenate([acc_lo, acc_hi], axis=1)
    if final:
        y = _rms(y) * fg_ref[...]
    o_ref[0] = y


def _combine(yg, wts, hp, x, g2, s_gate, s_up, s_down, final_g=None, tm=256):
    b, s, d = x.shape
    k = yg.shape[0]
    half = d // 2
    row = pl.BlockSpec((1, tm, d), lambda i, j: (i, j, 0))
    prow = pl.BlockSpec((1, tm, half), lambda i, j: (i, j, 0))
    ins = [yg, wts, hp, x, g2, s_gate, s_up, s_down]
    in_specs = [pl.BlockSpec((k, 1, tm, half), lambda i, j: (0, i, j, 0)), pl.BlockSpec((1, tm, k), lambda i, j: (i, j, 0)),
                prow, row, pl.BlockSpec((1, 1, d), lambda i, j: (i, 0, 0)),
                _const_spec(s_gate.shape), _const_spec(s_up.shape), _const_spec(s_down.shape)]
    if final_g is not None:
        ins.append(final_g.reshape(1, d))
        in_specs.append(_const_spec((1, d)))
    return pl.pallas_call(
        functools.partial(_combine_kernel, final=final_g is not None),
        grid=(b, s // tm),
        in_specs=in_specs,
        out_specs=row,
        out_shape=jax.ShapeDtypeStruct(x.shape, F32),
        compiler_params=_params('parallel', 'parallel'),
        name='moe_combine',
    )(*ins)


def _routed_moe(x, g, scale, shift, g2, w_router, e_bias, layer, w_gate, w_up, w_down, s_gate, s_up, s_down, final_g=None):
    b, s, d = x.shape
    t = b * s
    hp, eid, rank, wts, counts = _router(x, g, scale, shift, w_router, e_bias, compact=True)
    counts = counts[:, 0].astype(jnp.int32)
    r = MOE_ROW_TILE
    padded = (counts + (r - 1)) // r * r
    ends = jnp.cumsum(padded)
    offsets = ends - padded
    n_rows = t * TOP_K + N_EXPERTS * r
    tile_start = jnp.arange(n_rows // r, dtype=jnp.int32) * r
    tile_expert = jnp.minimum(jnp.sum((tile_start[:, None] >= ends[None, :]).astype(jnp.int32), axis=1), N_EXPERTS - 1)
    n_used = (ends[-1] // r).reshape(1).astype(jnp.int32)
    pos = _route_pos(offsets.astype(jnp.int32), eid, rank)
    xp = _sc_dispatch(hp.reshape(t, d // 2), pos, n_rows)
    yp = _experts(xp, tile_expert.astype(jnp.int32), n_used, layer, w_gate, w_up, w_down)
    yg = _sc_collect(yp, pos).reshape(TOP_K, b, s, d // 2)
    return _combine(yg, wts.T.reshape(b, s, TOP_K), hp, x, g2, s_gate, s_up, s_down, final_g)


def _mixers(p, pc, ctx_out, prm, l, lam_init, rope_tabs, lb_terms):
    s = p['hy_v'].shape[1]
    sc = pc['hy_v'].shape[1]

    hy_args = (prm['hy_w1'][l], prm['hy_b1'][l], prm['hy_w2'][l], prm['hy_b2'][l], prm['hy_w3'][l], prm['hy_b3'][l],
               prm['hy_sin_freq'][l], prm['hy_decay'][l])
    y_hy = _hyena([p['hy_v'], p['hy_x1'], p['hy_x2']], prm['hy_conv_w'][l], prm['hy_conv_b'][l],
                  _hy_filters(s, *hy_args), prm['hy_bias'][l], inner=128)
    yc_hy = None
    if ctx_out:
        yc_hy = _hyena([pc['hy_v'], pc['hy_x1'], pc['hy_x2']], prm['hy_conv_w'][l], prm['hy_conv_b'][l],
                       _hy_filters(sc, *hy_args), prm['hy_bias'][l], inner=32)

    lp = prm['da_lambda'][l].astype(F32)
    lam = jnp.exp(jnp.sum(lp[0] * lp[1])) - jnp.exp(jnp.sum(lp[2] * lp[3])) + lam_init
    da_kw = dict(heads=DA_HEADS, ncomp=2, scale=DA_HEAD_DIM ** -0.5, lam=lam, subln_g=prm['da_subln_g'][l],
                 post_scale=1.0 - lam_init)
    da_ctx = ([pc['da_k']], pc['da_v'])
    y_da = _attention([p['da_q']], [da_ctx, ([p['da_k']], p['da_v'])], **da_kw)
    yc_da = _attention([pc['da_q']], [da_ctx], **da_kw) if ctx_out else None

    wq = prm['mla_w_q_up'][l].reshape(MLA_Q_RANK, MLA_HEADS, MLA_NOPE_DIM + MLA_ROPE_DIM)
    wq_n = wq[:, :, :MLA_NOPE_DIM].reshape(MLA_Q_RANK, -1).astype(BF16)
    wq_r = wq[:, :, MLA_NOPE_DIM:].reshape(MLA_Q_RANK, -1).astype(BF16)
    wkv = prm['mla_w_kv_up'][l].reshape(MLA_KV_RANK, MLA_HEADS, MLA_NOPE_DIM + MLA_V_DIM)
    wkv_n = wkv[:, :, :MLA_NOPE_DIM].reshape(MLA_KV_RANK, -1).astype(BF16)
    wkv_v = wkv[:, :, MLA_NOPE_DIM:].reshape(MLA_KV_RANK, -1).astype(BF16)

    def queries(qd, tabs):
        return _norm_proj(qd, prm['mla_q_norm_g'][l], [(wq_n, F32, False, MLA_HEADS), (wq_r, F32, True, MLA_HEADS)],
                          rope_tabs=tabs)

    def keys_values(kvd):
        return _norm_proj(kvd, prm['mla_kv_norm_g'][l], [(wkv_n, BF16, False, MLA_HEADS), (wkv_v, BF16, False, MLA_HEADS)])

    kn_l, v_l = keys_values(p['mla_kv'])
    kn_c, v_c = keys_values(pc['mla_kv'])
    mla_kw = dict(heads=MLA_HEADS, ncomp=1, scale=(MLA_NOPE_DIM + MLA_ROPE_DIM) ** -0.5)
    mla_ctx = ([kn_c, pc['mla_kr']], v_c)
    y_mla = _attention(queries(p['mla_q'], rope_tabs), [mla_ctx, ([kn_l, p['mla_kr']], v_l)], **mla_kw)
    yc_mla = _attention(queries(pc['mla_q'], None), [mla_ctx], **mla_kw) if ctx_out else None

    o, oc = _hgrn(p['hg_q'], p['hg_ff'], p['hg_fb'], p['hg_i'], pc['hg_q'], pc['hg_ff'], pc['hg_fb'], pc['hg_i'],
                  lb_terms, prm['hg_norm_g'][l])
    return (y_hy, y_da, y_mla, o), (yc_hy, yc_da, yc_mla, oc)


def kernel(x, c, ctx, c_ctx, w_ada, b_ada, norm1_g, norm2_g, w_in, w_out, hy_conv_w, hy_conv_b, hy_w1, hy_b1, hy_w2, hy_b2, hy_w3, hy_b3, hy_sin_freq, hy_decay, hy_bias, da_lambda, da_subln_g, mla_q_norm_g, mla_w_q_up, mla_kv_norm_g, mla_w_kv_up, hg_lower_bounds, hg_norm_g, moe_w_router, moe_bias, moe_w_gate, moe_w_up, moe_w_down, moe_sh_gate, moe_sh_up, moe_sh_down, final_norm_g):
    prm = dict(hy_conv_w=hy_conv_w, hy_conv_b=hy_conv_b, hy_w1=hy_w1, hy_b1=hy_b1, hy_w2=hy_w2, hy_b2=hy_b2,
               hy_w3=hy_w3, hy_b3=hy_b3, hy_sin_freq=hy_sin_freq, hy_decay=hy_decay, hy_bias=hy_bias,
               da_lambda=da_lambda, da_subln_g=da_subln_g, mla_q_norm_g=mla_q_norm_g, mla_w_q_up=mla_w_q_up,
               mla_kv_norm_g=mla_kv_norm_g, mla_w_kv_up=mla_w_kv_up, hg_norm_g=hg_norm_g)
    b, n_lat, d = x.shape
    depth = w_in.shape[0]
    rows = n_lat // GRID_W
    row_pos = jnp.repeat(jnp.arange(rows, dtype=jnp.int32), GRID_W)
    col_pos = jnp.tile(jnp.arange(GRID_W, dtype=jnp.int32), rows)
    rope_tabs = _rope_tables(row_pos, col_pos, 2 * DA_HEADS * DA_HEAD_DIM)
    lbs = jnp.cumsum(jax.nn.softmax(hg_lower_bounds.astype(F32), axis=1), axis=1)
    lbs = lbs - lbs[:, :1]
    cond = jnp.concatenate([c, c_ctx[None], jnp.zeros((8 - b - 1, d), F32)], axis=0)

    for l in range(depth):
        ctx_out = l < depth - 1
        mods = _ada(cond, w_ada[l], b_ada[l])
        sh1, sc1, g1, sh2, sc2, g2 = [m[:, None, :] for m in jnp.split(mods[:b], 6, axis=-1)]
        mc = [jnp.broadcast_to(m[:, None, :], (b, 1, d)) for m in jnp.split(mods[b:b + 1], 6, axis=-1)]

        off = 0
        outs = []
        for _, wdt, dt, rope, split, rep in _SEGMENTS:
            w = w_in[l][:, off:off + wdt].astype(BF16)
            outs.append((jnp.tile(w, (1, rep)) if rep > 1 else w, dt, rope, split))
            off += wdt
        names = [seg[0] for seg in _SEGMENTS]
        p = dict(zip(names, _norm_proj(x, norm1_g[l], outs, sc1, sh1, rope_tabs=rope_tabs)))
        pc = dict(zip(names, _norm_proj(ctx, norm1_g[l], outs, mc[1], mc[0])))

        lb = lbs[:, l]
        lb_terms = jnp.stack([jnp.log(lb), jnp.log1p(-lb), 1.0 - lb], axis=1)
        lam_init = 0.8 - 0.6 * math.exp(-0.3 * l)
        lat_parts, ctx_parts = _mixers(p, pc, ctx_out, prm, l, lam_init, rope_tabs, lb_terms)

        w_out_b = w_out[l].astype(BF16)
        moe_w = (l, moe_w_gate, moe_w_up, moe_w_down,
                 moe_sh_gate[l].astype(BF16), moe_sh_up[l].astype(BF16), moe_sh_down[l].astype(BF16))

        if ctx_out:
            ctx = _out_proj(*ctx_parts, pc['hg_g'], ctx, mc[2], w_out_b)
            flat = ctx.reshape(1, -1, d)
            h2c, gate_c = _router(flat, norm2_g[l], mc[4][:1], mc[3][:1], moe_w_router[l], moe_bias[l])
            ctx = _moe(h2c, flat, gate_c.transpose(0, 2, 1), mc[5][:1], *moe_w).reshape(ctx.shape)

        x = _out_proj(*lat_parts, p['hg_g'], x, g1, w_out_b)
        x = _routed_moe(x, norm2_g[l], sc2, sh2, g2, moe_w_router[l], moe_bias[l], *moe_w,
                        final_g=None if ctx_out else final_norm_g)

    return x
```

```python
import functools
import math

import numpy as np
import jax
import jax.numpy as jnp
from jax import lax
from jax.experimental import pallas as pl
from jax.experimental.pallas import tpu as pltpu
from jax.experimental.pallas import tpu_sc as plsc

F32 = jnp.float32
BF16 = jnp.bfloat16
HIGHEST = lax.Precision.HIGHEST

D_MODEL = 1024
GRID_W = 64
HY_WIDTH = 256
HY_ORDER = 2
HY_BANDS = 16
DA_HEADS = 4
DA_HEAD_DIM = 32
MLA_HEADS = 4
MLA_Q_RANK = 192
MLA_KV_RANK = 128
MLA_NOPE_DIM = 64
MLA_ROPE_DIM = 32
MLA_V_DIM = 64
HG_HEADS = 4
HG_KEY_DIM = 64
HG_VAL_DIM = 64
HG_CHUNK = 64
HG_SUB = 8
HG_UNROLL = 4
N_EXPERTS = 64
N_EXPERT_GROUPS = 8
TOPK_GROUPS = 4
TOP_K = 8
EXPERT_FF = 256
ROUTED_SCALE = 2.5
ROPE_BASE = 10000.0
NORM_EPS = 1e-6

V7X_VMEM_LIMIT_BYTES = 56 * 1024 * 1024
LANES = 128

_SEGMENTS = (
    ('hy_v', HY_WIDTH, F32, False, 0, 1), ('hy_x1', HY_WIDTH, F32, False, 0, 1), ('hy_x2', HY_WIDTH, F32, False, 0, 1),
    ('da_q', 2 * DA_HEADS * DA_HEAD_DIM, F32, True, 2 * DA_HEADS, 1),
    ('da_k', 2 * DA_HEADS * DA_HEAD_DIM, BF16, True, 2 * DA_HEADS, 1),
    ('da_v', 2 * DA_HEADS * DA_HEAD_DIM, BF16, False, DA_HEADS, 1),
    ('mla_q', MLA_Q_RANK, F32, False, 0, 1), ('mla_kv', MLA_KV_RANK, F32, False, 0, 1),
    ('mla_kr', MLA_ROPE_DIM, BF16, True, MLA_HEADS, MLA_HEADS),
    ('hg_q', HG_HEADS * HG_KEY_DIM, F32, False, 0, 1), ('hg_ff', HG_HEADS * HG_KEY_DIM, F32, False, 0, 1),
    ('hg_fb', HG_HEADS * HG_KEY_DIM, F32, False, 0, 1), ('hg_i', HG_HEADS * HG_VAL_DIM, F32, False, 0, 1),
    ('hg_g', HG_HEADS * HG_VAL_DIM, F32, False, 0, 1),
)


def _params(*semantics):
    return pltpu.CompilerParams(dimension_semantics=semantics, vmem_limit_bytes=V7X_VMEM_LIMIT_BYTES)


def _const_spec(shape):
    nd = len(shape)
    return pl.BlockSpec(shape, lambda *_: (0,) * nd)


def _rms(x, eps=NORM_EPS):
    return x * lax.rsqrt(jnp.mean(x * x, axis=-1, keepdims=True) + eps)


def _silu(x):
    return x * jax.nn.sigmoid(x)


def _dot_nt(a, b, **kw):
    return lax.dot_general(a, b, (((1,), (1,)), ((), ())), preferred_element_type=F32, **kw)


def _ada_kernel(c_ref, w_ref, b_ref, o_ref):
    s = _silu(c_ref[...])
    o_ref[...] = jnp.dot(s, w_ref[...], precision=HIGHEST, preferred_element_type=F32) + b_ref[...]


def _ada(cond, w, b):
    r, d = cond.shape
    n = w.shape[1]
    tn = 1536
    return pl.pallas_call(
        _ada_kernel,
        grid=(n // tn,),
        in_specs=[_const_spec((r, d)), pl.BlockSpec((d, tn), lambda j: (0, j)), pl.BlockSpec((1, tn), lambda j: (0, j))],
        out_specs=pl.BlockSpec((r, tn), lambda j: (0, j)),
        out_shape=jax.ShapeDtypeStruct((r, n), F32),
        compiler_params=_params('arbitrary'),
        name='ada',
    )(cond, w, b.reshape(1, n))


ROPE_UNIT = 32


def _rope_tables(row, col, width):
    n = ROPE_UNIT // 4
    inv = ROPE_BASE ** (-jnp.arange(n, dtype=F32) / n)
    units = width // ROPE_UNIT
    parts_c, parts_a, parts_b = [], [], []
    zero = jnp.zeros((row.shape[0], n), F32)
    for pos in (row, col):
        ang = pos.astype(F32)[:, None] * inv
        cos, sin = jnp.cos(ang), jnp.sin(ang)
        parts_c += [cos, cos]
        parts_a += [zero, sin]
        parts_b += [-sin, zero]
    tile = lambda ps: jnp.tile(jnp.concatenate(ps, axis=1), (1, units))
    return tile(parts_c), tile(parts_a), tile(parts_b)


def _norm_proj_kernel(*refs, n_w, modulate, ropes, splits):
    x_ref, g_ref = refs[0], refs[1]
    pos = 2
    if modulate:
        sc_ref, sh_ref = refs[2], refs[3]
        pos = 4
    if any(ropes):
        rc_ref, ra_ref, rb_ref = refs[pos:pos + 3]
        pos += 3
    w_refs = refs[pos:pos + n_w]
    o_refs = refs[pos + n_w:]
    y = _rms(x_ref[0]) * g_ref[...]
    if modulate:
        y = y * (1.0 + sc_ref[0]) + sh_ref[0]
    yb = y.astype(BF16)
    for w_ref, o_ref, rope, split in zip(w_refs, o_refs, ropes, splits):
        o = jnp.dot(yb, w_ref[...], preferred_element_type=F32)
        if rope:
            wd = o.shape[1]
            shift = ROPE_UNIT // 4
            o = (o * rc_ref[:, :wd] + pltpu.roll(o, shift, axis=1) * ra_ref[:, :wd]
                 + pltpu.roll(o, wd - shift, axis=1) * rb_ref[:, :wd])
        if split:
            unit = o.shape[1] // split
            for u in range(split):
                o_ref[0, u] = o[:, u * unit:(u + 1) * unit].astype(o_ref.dtype)
        else:
            o_ref[0] = o.astype(o_ref.dtype)


def _norm_proj(x, g, outs, scale=None, shift=None, rope_tabs=None, tm=512):
    b, s, k = x.shape
    tm = min(tm, s)
    modulate = scale is not None
    ropes = tuple(bool(o[2]) and rope_tabs is not None for o in outs)
    splits = tuple(o[3] for o in outs)
    ins = [x, g.reshape(1, k)]
    in_specs = [pl.BlockSpec((1, tm, k), lambda i, j: (i, j, 0)), _const_spec((1, k))]
    if modulate:
        ins += [scale, shift]
        in_specs += [pl.BlockSpec((1, 1, k), lambda i, j: (i, 0, 0))] * 2
    if any(ropes):
        ins += list(rope_tabs)
        in_specs += [pl.BlockSpec((tm, rope_tabs[0].shape[1]), lambda i, j: (j, 0))] * 3
    out_specs, out_shape = [], []
    for w, dt, _, split in outs:
        ins.append(w)
        in_specs.append(_const_spec(w.shape))
        n = w.shape[1]
        if split:
            out_specs.append(pl.BlockSpec((1, split, tm, n // split), lambda i, j: (i, 0, j, 0)))
            out_shape.append(jax.ShapeDtypeStruct((b, split, s, n // split), dt))
        else:
            out_specs.append(pl.BlockSpec((1, tm, n), lambda i, j: (i, j, 0)))
            out_shape.append(jax.ShapeDtypeStruct((b, s, n), dt))
    return pl.pallas_call(
        functools.partial(_norm_proj_kernel, n_w=len(outs), modulate=modulate, ropes=ropes, splits=splits),
        grid=(b, s // tm),
        in_specs=in_specs,
        out_specs=out_specs,
        out_shape=out_shape,
        compiler_params=_params('parallel', 'parallel'),
        name='norm_proj',
    )(*ins)


def _hy_filter_kernel(w1t_ref, w1s_ref, w1c_ref, b1_ref, w2_ref, b2_ref, w3_ref, b3_ref, fr_ref, dec_ref, o_ref, *, n):
    t = lax.broadcasted_iota(jnp.int32, (n, 1), 0).astype(F32) / n
    bands = lax.broadcasted_iota(jnp.int32, (1, HY_BANDS), 1).astype(F32) + 1.0
    ang = (2.0 * jnp.pi) * t * bands
    pre = (t * w1t_ref[...]
           + jnp.dot(jnp.sin(ang), w1s_ref[...], precision=HIGHEST, preferred_element_type=F32)
           + jnp.dot(jnp.cos(ang), w1c_ref[...], precision=HIGHEST, preferred_element_type=F32)
           + b1_ref[...])
    hid = jnp.sin(fr_ref[0:1, :] * pre)
    hid = jnp.sin(fr_ref[1:2, :] * (jnp.dot(hid, w2_ref[...], precision=HIGHEST, preferred_element_type=F32) + b2_ref[...]))
    filt = jnp.dot(hid, w3_ref[...], precision=HIGHEST, preferred_element_type=F32) + b3_ref[...]
    filt = filt * jnp.exp(-t * jnp.abs(dec_ref[...]))
    col = jnp.sum(jnp.abs(filt), axis=0, keepdims=True) - jnp.abs(filt[0:1, :])
    w = HY_WIDTH
    for o in range(HY_ORDER):
        lo = o * 2 * w
        f0 = filt[0:1, lo:lo + w] + filt[0:1, lo + w:lo + 2 * w]
        inv = 1.0 / (col[:, lo:lo + w] + col[:, lo + w:lo + 2 * w] + jnp.abs(f0))
        o_ref[:, lo:lo + w] = filt[:, lo:lo + w] * inv
        o_ref[:, lo + w:lo + 2 * w] = filt[:, lo + w:lo + 2 * w] * inv


def _hy_filters(n, w1, b1, w2, b2, w3, b3, freq, decay):
    cols = w3.shape[1]
    ins = [w1[0:1], w1[1:1 + HY_BANDS], w1[1 + HY_BANDS:], b1.reshape(1, -1), w2, b2.reshape(1, -1), w3,
           b3.reshape(1, -1), freq, decay.reshape(1, -1)]
    out = pl.pallas_call(
        functools.partial(_hy_filter_kernel, n=n),
        grid=(1,),
        in_specs=[_const_spec(a.shape) for a in ins],
        out_specs=_const_spec((n, cols)),
        out_shape=jax.ShapeDtypeStruct((n, cols), F32),
        compiler_params=_params('arbitrary'),
        name='hy_filter',
    )(*ins)
    return out.reshape(n, HY_ORDER, 2, HY_WIDTH)


def _two_sided(filt_n):
    n = filt_n.shape[0]
    hf, hb = filt_n[:, :, 0], filt_n[:, :, 1]
    h = jnp.concatenate([hf[:1] + hb[:1], hf[1:], jnp.zeros((1,) + hf.shape[1:], F32), hb[:0:-1]], axis=0)
    return h.reshape(2 * n, HY_ORDER * HY_WIDTH)


def _short_conv_kernel(*refs, s):
    x_refs, w_refs, b_refs, o_refs = refs[0:3], refs[3:6], refs[6:9], refs[9:12]
    row = lax.broadcasted_iota(jnp.int32, (s, 1), 0)
    for x_ref, w_ref, b_ref, o_ref in zip(x_refs, w_refs, b_refs, o_refs):
        x = x_ref[0]
        prev = jnp.where(row == 0, 0.0, pltpu.roll(x, 1, axis=0))
        nxt = jnp.where(row == s - 1, 0.0, pltpu.roll(x, s - 1, axis=0))
        o_ref[0] = prev * w_ref[0:1, :] + x * w_ref[1:2, :] + nxt * w_ref[2:3, :] + b_ref[...]


def _short_conv(parts, conv_w, conv_b):
    b, s, c = parts[0].shape
    tc = LANES
    ws = [conv_w[:, i * c:(i + 1) * c] for i in range(3)]
    bs = [conv_b[i * c:(i + 1) * c].reshape(1, c) for i in range(3)]
    xspec = pl.BlockSpec((1, s, tc), lambda i, j: (i, 0, j))
    return pl.pallas_call(
        functools.partial(_short_conv_kernel, s=s),
        grid=(b, c // tc),
        in_specs=[xspec] * 3 + [pl.BlockSpec((3, tc), lambda i, j: (0, j))] * 3 + [pl.BlockSpec((1, tc), lambda i, j: (0, j))] * 3,
        out_specs=[xspec] * 3,
        out_shape=[jax.ShapeDtypeStruct((b, s, c), F32)] * 3,
        compiler_params=_params('parallel', 'parallel'),
        name='short_conv',
    )(*parts, *ws, *bs)


def _dft_cos_sin(rows, cols, period):
    ang = 2.0 * np.pi * ((np.arange(rows)[:, None] * np.arange(cols)[None, :]) % period) / period
    return np.cos(ang), np.sin(ang)


def _fft_tables(n, inner):
    big = 2 * n
    n1 = big // inner
    c1, s1 = _dft_cos_sin(n1, n1, n1)
    h = n1 // 2
    outer_data = np.block([[c1[:, :h], s1[:, :h]], [-s1[:, :h], c1[:, :h]]])
    outer_real = np.concatenate([c1, -s1], axis=0)
    outer_inv = np.block([[c1[:h, :], -s1[:h, :]], [s1[:h, :], c1[:h, :]]]) / big
    c2, s2 = _dft_cos_sin(inner, inner, inner)
    inner_fwd = np.block([[c2, s2], [-s2, c2]])
    inner_inv = np.block([[c2, -s2], [s2, c2]])
    ct, st = _dft_cos_sin(n1, inner, big)
    f = lambda a: jnp.asarray(a, F32)
    return dict(n1=n1, inner=inner, outer_data=_hi_lo_cols(outer_data), outer_real=_hi_lo_cols(outer_real),
                outer_inv=_hi_lo_cols(outer_inv),
                inner_fwd=_hi_lo_cols(inner_fwd), inner_inv=_hi_lo_cols(inner_inv),
                tw_cos=f(ct).reshape(n1, inner, 1), tw_sin=f(st).reshape(n1, inner, 1))


def _left_mm_kernel(m_ref, x_ref, o_ref):
    o_ref[0] = jnp.dot(m_ref[...], _hi_lo_rows(x_ref[0]), preferred_element_type=F32)


def _left_mm(m, x, tl=4096):
    p, k, l = x.shape
    mm = m.shape[0]
    tl = min(tl, l)
    return pl.pallas_call(
        _left_mm_kernel,
        grid=(p, l // tl),
        in_specs=[_const_spec(m.shape), pl.BlockSpec((1, k, tl), lambda i, j: (i, 0, j))],
        out_specs=pl.BlockSpec((1, mm, tl), lambda i, j: (i, 0, j)),
        out_shape=jax.ShapeDtypeStruct((p, mm, l), F32),
        compiler_params=_params('parallel', 'parallel'),
        name='fft_outer',
    )(m, x)


def _hi_lo_cols(m):
    m = np.asarray(m, np.float32)
    hi = m.astype(BF16)
    lo = (m - hi.astype(np.float32)).astype(BF16)
    return jnp.asarray(np.concatenate([hi, hi, lo], axis=1))


def _hi_lo_rows(x):
    hi = x.astype(BF16)
    lo = (x - hi.astype(F32)).astype(BF16)
    return jnp.concatenate([hi, lo, hi], axis=0)


def _inner_kernel(a_ref, twc_ref, tws_ref, gf_ref, *rest, convolve, inner, kb):
    for s in range(kb):
        ar, ai = a_ref[0, 0, s], a_ref[0, 1, s]
        tc, ts = twc_ref[s], tws_ref[s]
        br = ar * tc + ai * ts
        bi = ai * tc - ar * ts
        x = jnp.dot(gf_ref[...], _hi_lo_rows(jnp.concatenate([br, bi], axis=0)), preferred_element_type=F32)
        if not convolve:
            o_ref = rest[0]
            o_ref[0, 0, s] = x[:inner]
            o_ref[0, 1, s] = x[inner:]
            continue
        h_ref, gi_ref, o_ref = rest
        xr, xi = x[:inner], x[inner:]
        hr, hi = h_ref[0, 0, s], h_ref[0, 1, s]
        yr = xr * hr - xi * hi
        yi = xr * hi + xi * hr
        z = jnp.dot(gi_ref[...], _hi_lo_rows(jnp.concatenate([yr, yi], axis=0)), preferred_element_type=F32)
        zr, zi = z[:inner], z[inner:]
        o_ref[0, 0, s] = zr * tc - zi * ts
        o_ref[0, 1, s] = zi * tc + zr * ts


def _fft_inner(a, tab, c, h=None, h_block=0):
    p = a.shape[0]
    n1, inner = tab['n1'], tab['inner']
    a5 = a.reshape(p, 2, n1, inner, c)
    tc = 2 * LANES
    kb = 4
    blk = pl.BlockSpec((1, 2, kb, inner, tc), lambda k, j, i: (i, 0, k, 0, j))
    tw_spec = pl.BlockSpec((kb, inner, 1), lambda k, j, i: (k, 0, 0))
    ins = [a5, tab['tw_cos'], tab['tw_sin'], tab['inner_fwd']]
    in_specs = [blk, tw_spec, tw_spec, _const_spec(tab['inner_fwd'].shape)]
    if h is not None:
        ch = h.shape[-1] // inner
        nb = c // tc
        ins += [h.reshape(1, 2, n1, inner, ch), tab['inner_inv']]
        in_specs += [pl.BlockSpec((1, 2, kb, inner, tc), lambda k, j, i: (0, 0, k, 0, h_block * nb + j)),
                     _const_spec(tab['inner_inv'].shape)]
    out = pl.pallas_call(
        functools.partial(_inner_kernel, convolve=h is not None, inner=inner, kb=kb),
        grid=(n1 // kb, c // tc, p),
        in_specs=in_specs,
        out_specs=blk,
        out_shape=jax.ShapeDtypeStruct(a5.shape, F32),
        compiler_params=_params('parallel', 'parallel', 'parallel'),
        name='fft_inner',
    )(*ins)
    return out.reshape(p, 2 * n1, inner * c)


def _gate_kernel(m_ref, z_ref, u_ref, x_ref, bias_ref, *rest, chain):
    y = jnp.dot(m_ref[...], _hi_lo_rows(z_ref[0]), preferred_element_type=F32)
    nxt = x_ref[0] * (y + u_ref[0] * bias_ref[...])
    if chain:
        mf_ref, o_ref, a_ref = rest
        o_ref[0] = nxt
        a_ref[0] = jnp.dot(mf_ref[...], _hi_lo_rows(nxt), preferred_element_type=F32)
    else:
        rest[0][0] = nxt


def _fft_gate(tab, z, u, x, bias_l, chain, tl=4096):
    p, k2, l = z.shape
    n1 = tab['n1']
    tl = min(tl, l)
    row = pl.BlockSpec((1, n1, tl), lambda i, j: (i, 0, j))
    ins = [tab['outer_inv'], z, u, x, bias_l]
    in_specs = [_const_spec(tab['outer_inv'].shape), pl.BlockSpec((1, k2, tl), lambda i, j: (i, 0, j)), row, row,
                pl.BlockSpec((1, tl), lambda i, j: (0, j))]
    out_specs = [row]
    out_shape = [jax.ShapeDtypeStruct((p, n1, l), F32)]
    if chain:
        ins.append(tab['outer_data'])
        in_specs.append(_const_spec(tab['outer_data'].shape))
        out_specs.append(pl.BlockSpec((1, k2, tl), lambda i, j: (i, 0, j)))
        out_shape.append(jax.ShapeDtypeStruct((p, k2, l), F32))
    return pl.pallas_call(
        functools.partial(_gate_kernel, chain=chain),
        grid=(p, l // tl),
        in_specs=in_specs,
        out_specs=out_specs,
        out_shape=out_shape,
        compiler_params=_params('parallel', 'parallel'),
        name='fft_gate',
    )(*ins)


def _hyena(parts, conv_w, conv_b, filt_n, bias, inner):
    b, s, c = parts[0].shape
    tab = _fft_tables(s, inner)
    n1 = tab['n1']
    lanes = inner * c
    h_taps = _two_sided(filt_n).reshape(1, n1, inner * HY_ORDER * c)
    h_spec = _fft_inner(_left_mm(tab['outer_real'], h_taps), tab, HY_ORDER * c)
    v, x1, x2 = [a.reshape(b // 2, n1, lanes) for a in _short_conv(parts, conv_w, conv_b)]
    bias_l = [jnp.tile(bias[o], inner).reshape(1, lanes) for o in range(HY_ORDER)]
    a = _left_mm(tab['outer_data'], v)
    z = _fft_inner(a, tab, c, h_spec, 0)
    z2, a = _fft_gate(tab, z, v, x1, bias_l[0], chain=True)
    z = _fft_inner(a, tab, c, h_spec, 1)
    (z3,) = _fft_gate(tab, z, z2, x2, bias_l[1], chain=False)
    return z3.reshape(b, s, c)


def _attn_kernel(*refs, n_q, n_pieces, ncomp, scale, post_scale):
    q_refs = refs[:n_q]
    pos = n_q
    pieces = []
    for _ in range(n_pieces):
        pieces.append((refs[pos:pos + n_q], refs[pos + n_q]))
        pos += n_q + 1
    if ncomp == 2:
        lam_ref, g_ref = refs[pos:pos + 2]
        pos += 2
    o_ref, kcat_ref, vcat_ref = refs[pos:pos + 3]
    dv = o_ref.shape[3]

    @pl.when(pl.program_id(2) == 0)
    def _():
        row = 0
        for k_refs, v_ref in pieces:
            n = v_ref.shape[2]
            for c in range(ncomp):
                parts = [k_ref[0, c if k_ref.shape[1] == ncomp else 0] for k_ref in k_refs]
                kcat_ref[c, row:row + n, :] = parts[0] if n_q == 1 else jnp.concatenate(parts, axis=1)
            vcat_ref[row:row + n, :dv] = v_ref[0, 0]
            vcat_ref[row:row + n, dv:] = jnp.ones((n, dv), BF16)
            row += n

    outs = []
    for c in range(ncomp):
        q = q_refs[0][0, c] if n_q == 1 else jnp.concatenate([q_ref[0, c] for q_ref in q_refs], axis=1)
        s = _dot_nt((q * (scale * math.log2(math.e))).astype(BF16), kcat_ref[c])
        m = jnp.max(s, axis=-1, keepdims=True)
        p = jnp.exp2((s - m).astype(BF16))
        ol = jnp.dot(p, vcat_ref[...], preferred_element_type=F32)
        outs.append(ol[:, :dv] / ol[:, dv:dv + 1])
    if ncomp == 2:
        o = outs[0] - lam_ref[0] * outs[1]
        o = _rms(o) * g_ref[...] * post_scale
    else:
        o = outs[0]
    o_ref[0, 0] = o


def _attention(q_parts, pieces, heads, ncomp, scale, tq=256, lam=None, subln_g=None, post_scale=1.0):
    b, _, sq, _ = q_parts[0].shape
    dv = pieces[0][1].shape[3]
    tq = min(tq, sq)
    ins = list(q_parts)
    in_specs = [pl.BlockSpec((1, ncomp, tq, q.shape[3]), lambda i, h, j: (i, h, j, 0)) for q in q_parts]
    for k_parts, v in pieces:
        for k in k_parts:
            ins.append(k)
            if k.shape[1] == 1:
                in_specs.append(pl.BlockSpec((1, 1) + k.shape[2:], lambda i, h, j: (i, 0, 0, 0)))
            else:
                in_specs.append(pl.BlockSpec((1, ncomp) + k.shape[2:], lambda i, h, j: (i, h, 0, 0)))
        ins.append(v)
        in_specs.append(pl.BlockSpec((1, 1) + v.shape[2:], lambda i, h, j: (i, h, 0, 0)))
    if ncomp == 2:
        ins += [lam.reshape(1), subln_g.reshape(1, dv)]
        in_specs += [pl.BlockSpec(memory_space=pltpu.SMEM), _const_spec((1, dv))]
    sk = sum(v.shape[2] for _, v in pieces)
    dqk = sum(q.shape[3] for q in q_parts)
    return pl.pallas_call(
        functools.partial(_attn_kernel, n_q=len(q_parts), n_pieces=len(pieces), ncomp=ncomp, scale=scale,
                          post_scale=post_scale),
        grid=(b, heads, sq // tq),
        in_specs=in_specs,
        out_specs=pl.BlockSpec((1, 1, tq, dv), lambda i, h, j: (i, h, j, 0)),
        out_shape=jax.ShapeDtypeStruct((b, heads, sq, dv), F32),
        scratch_shapes=[pltpu.VMEM((ncomp, sk, dqk), BF16), pltpu.VMEM((sk, 2 * dv), BF16)],
        compiler_params=_params('parallel', 'parallel', 'arbitrary'),
        name='attention',
    )(*ins)


def _forget_terms(f, log_lb, log_1m_lb, one_m_lb):
    log_sig = jnp.minimum(f, 0.0) - jnp.log1p(jnp.exp(-jnp.abs(f)))
    b = log_1m_lb + log_sig
    log_g = jnp.maximum(log_lb, b) + jnp.log1p(jnp.exp(-jnp.abs(log_lb - b)))
    return log_g, one_m_lb * jax.nn.sigmoid(-f)


def _hg_tables():
    ck, sub = HG_CHUNK, HG_SUB
    t = np.arange(ck)
    cum_mats, half_masks, group_masks, keeps = [], [], [], []
    for rev in (False, True):
        mats = [(t[None, :] >= t[:, None]) if rev else (t[None, :] <= t[:, None])]
        halves = []
        hs = ck // 2
        while hs >= sub:
            pos = t % (2 * hs)
            b = t - pos + hs
            mats.append((t[None, :] >= b[:, None]) if rev else (t[None, :] < b[:, None]))
            q_half = (pos < hs) if rev else (pos >= hs)
            halves.append(np.stack([q_half, ~q_half]))
            if not rev:
                grp = (t[:, None] // (2 * hs)) == (t[None, :] // (2 * hs))
                group_masks.append(np.concatenate([grp, grp], axis=0))
            hs //= 2
        cum_mats.append(np.concatenate(mats, axis=0))
        half_masks.append(np.stack(halves))
        c = np.arange(ck * sub)
        tt, ss = (c // sub) % sub, c % sub
        keeps.append((ss >= tt) if rev else (ss <= tt))
    lanes = 2 * HG_KEY_DIM
    ln = np.arange(lanes)
    bd = (ln[:, None] // HG_KEY_DIM) == (ln[None, :] // HG_KEY_DIM)
    hm = np.broadcast_to(np.stack(half_masks)[..., None], (2, len(half_masks[0]), 2, ck, lanes))
    return (jnp.asarray(np.stack(cum_mats), BF16), jnp.asarray(hm, F32), jnp.asarray(np.stack(group_masks), F32),
            jnp.asarray(bd, F32), jnp.asarray(np.broadcast_to(np.stack(keeps)[..., None], (2, ck * sub, lanes)), F32))


def _split2(x):
    a = x.astype(BF16)
    return a, (x - a.astype(F32)).astype(BF16)


def _hg_chunk(q, k, v, lg, st, rev, cm, hm, gm, keep, bd, m0, m1):
    ck, sub = HG_CHUNK, HG_SUB
    lanes = lg.shape[1]
    c2 = jnp.dot(cm, jnp.concatenate(_split2(lg), axis=1), preferred_element_type=F32)
    call = c2[:, :lanes] + c2[:, lanes:]
    cum = call[0:ck]
    tot = cum[0:1] if rev else cum[ck - 1:ck]
    o = _dot_nt((q * jnp.exp(cum)).astype(BF16), st.astype(BF16))
    kd = (k * jnp.exp(tot - cum)).astype(BF16)
    st_new = st * jnp.exp(tot) + bd * jnp.dot(v.T.astype(BF16), kd, preferred_element_type=F32)
    q2s, kfs = [], []
    for lv in range(gm.shape[0]):
        cb = call[(lv + 1) * ck:(lv + 2) * ck]
        qd = q * jnp.exp(jnp.minimum(cum - cb, 0.0)) * hm[lv, 0]
        kfs.append(k * jnp.exp(jnp.minimum(cb - cum, 0.0)) * hm[lv, 1])
        q2s += [qd * m0, qd * m1]
    cross = _dot_nt(jnp.concatenate(q2s, axis=0).astype(BF16), jnp.concatenate(kfs, axis=0).astype(BF16))
    s2 = None
    for lv in range(gm.shape[0]):
        term = cross[2 * lv * ck:2 * (lv + 1) * ck, lv * ck:(lv + 1) * ck] * gm[lv]
        s2 = term if s2 is None else s2 + term
    r = jnp.dot(s2.astype(BF16), v.astype(BF16), preferred_element_type=F32)
    o = o + m0 * r[:ck] + m1 * r[ck:]
    rows, vts = [], []
    for i in range(ck // sub):
        lo, hi = i * sub, (i + 1) * sub
        ki, ci = k[lo:hi], cum[lo:hi]
        for t in range(lo, hi):
            rows.append(q[t:t + 1] * ki * jnp.exp(jnp.minimum(cum[t:t + 1] - ci, 0.0)))
            vts.append(v[lo:hi])
    sc = jnp.dot(jnp.concatenate(rows, axis=0).astype(BF16), bd.astype(BF16), preferred_element_type=F32)
    prod = sc * jnp.concatenate(vts, axis=0) * keep
    o = o + jnp.sum(prod.reshape(ck, sub, lanes), axis=1)
    return o, st_new


def _hgrn_kernel(q_ref, ff_ref, fb_ref, i_ref, qc_ref, ffc_ref, fbc_ref, ic_ref, lb_ref, g_ref,
                 cm_ref, hm_ref, gm_ref, bd_ref, keep_ref, o_ref, oc_ref, or_ref, ocr_ref, st_ref, *, n_lat, n_ctx):
    ck = HG_CHUNK
    lanes = o_ref.shape[-1]
    lane = lax.broadcasted_iota(jnp.int32, (1, lanes), 1)
    m0 = (lane < HG_KEY_DIM).astype(F32)
    m1 = 1.0 - m0
    bd = bd_ref[...]
    gm = gm_ref[...]

    def one(q, f, v, rev):
        d = 1 if rev else 0
        lg, k = _forget_terms(f, lb_ref[d, 0:1, :], lb_ref[d, 1:2, :], lb_ref[d, 2:3, :])
        o, st = _hg_chunk(q, k, v, lg, st_ref[d], rev, cm_ref[d], hm_ref[d], gm, keep_ref[d], bd, m0, m1)
        st_ref[d] = st
        return o

    def sweep(qr, ffr, fbr, ir, out_f, out_r, n):
        nc = n // ck

        def body(step, carry):
            idf = pl.ds(pl.multiple_of(step * ck, ck), ck)
            idr = pl.ds(pl.multiple_of((nc - 1 - step) * ck, ck), ck)
            out_f[0, idf, :] = one(qr[0, idf, :], ffr[0, idf, :], ir[0, idf, :], False)
            out_r[idr, :] = one(qr[0, idr, :], fbr[0, idr, :], ir[0, idr, :], True)
            return carry

        lax.fori_loop(0, nc, body, 0, unroll=HG_UNROLL)

    st_ref[...] = jnp.zeros(st_ref.shape, F32)
    sweep(qc_ref, ffc_ref, fbc_ref, ic_ref, oc_ref, ocr_ref, n_ctx)
    sweep(q_ref, ff_ref, fb_ref, i_ref, o_ref, or_ref, n_lat)

    mean_mat = bd * (1.0 / HG_VAL_DIM)

    def readout(out, out_r, n):
        tile = min(n, 512)

        def body(step, carry):
            idx = pl.ds(pl.multiple_of(step * tile, tile), tile)
            x = out[0, idx, :] + out_r[idx, :]
            ms = jnp.dot(x * x, mean_mat, precision=HIGHEST, preferred_element_type=F32)
            out[0, idx, :] = x * lax.rsqrt(ms + NORM_EPS) * g_ref[...]
            return carry

        lax.fori_loop(0, n // tile, body, 0)

    readout(oc_ref, ocr_ref, n_ctx)
    readout(o_ref, or_ref, n_lat)


def _hgrn(q, ff, fb, iv, qc, ffc, fbc, ic, lb_terms, norm_g):
    b, n_lat, width = q.shape
    n_ctx = qc.shape[1]
    lanes = 2 * HG_KEY_DIM
    tables = _hg_tables()
    lat = pl.BlockSpec((1, n_lat, lanes), lambda i, j: (i, 0, j))
    ctx = pl.BlockSpec((1, n_ctx, lanes), lambda i, j: (i, 0, j))
    g2 = jnp.tile(norm_g, 2).reshape(1, lanes)
    return pl.pallas_call(
        functools.partial(_hgrn_kernel, n_lat=n_lat, n_ctx=n_ctx),
        grid=(b, width // lanes),
        in_specs=[lat] * 4 + [ctx] * 4 + [pl.BlockSpec((2, 3, lanes), lambda i, j: (0, 0, j)), _const_spec((1, lanes))]
                 + [_const_spec(t.shape) for t in tables],
        out_specs=[lat, ctx],
        out_shape=[jax.ShapeDtypeStruct(q.shape, F32), jax.ShapeDtypeStruct(qc.shape, F32)],
        scratch_shapes=[pltpu.VMEM((n_lat, lanes), F32), pltpu.VMEM((n_ctx, lanes), F32), pltpu.VMEM((2, lanes, lanes), F32)],
        compiler_params=_params('parallel', 'parallel'),
        name='hgrn2',
    )(q, ff, fb, iv, qc, ffc, fbc, ic, lb_terms, g2, *tables)


def _out_proj_kernel(hy_ref, da_ref, mla_ref, hg_ref, gate_ref, x_ref, g1_ref, w_ref, o_ref):
    c = hy_ref.shape[2]
    acc = jnp.dot(hy_ref[0].astype(BF16), w_ref[0:c, :], preferred_element_type=F32)
    for i, head_ref in ((1, da_ref), (2, mla_ref)):
        dv = head_ref.shape[3]
        for h in range(head_ref.shape[1]):
            lo = i * c + h * dv
            acc = acc + jnp.dot(head_ref[0, h].astype(BF16), w_ref[lo:lo + dv, :], preferred_element_type=F32)
    hg = hg_ref[0] * _silu(gate_ref[0])
    acc = acc + jnp.dot(hg.astype(BF16), w_ref[3 * c:4 * c, :], preferred_element_type=F32)
    o_ref[0] = x_ref[0] + g1_ref[0] * acc


def _out_proj(y_hy, y_da, y_mla, y_hg, gate, x, g1, w_out, tm=512):
    b, s, d = x.shape
    tm = min(tm, s)
    c = y_hy.shape[2]
    part = pl.BlockSpec((1, tm, c), lambda i, j: (i, j, 0))
    headed = lambda a: pl.BlockSpec((1, a.shape[1], tm, a.shape[3]), lambda i, j: (i, 0, j, 0))
    row = pl.BlockSpec((1, tm, d), lambda i, j: (i, j, 0))
    return pl.pallas_call(
        _out_proj_kernel,
        grid=(b, s // tm),
        in_specs=[part, headed(y_da), headed(y_mla), part, part, row, pl.BlockSpec((1, 1, d), lambda i, j: (i, 0, 0)),
                  _const_spec(w_out.shape)],
        out_specs=row,
        out_shape=jax.ShapeDtypeStruct(x.shape, F32),
        compiler_params=_params('parallel', 'parallel'),
        name='out_proj',
    )(y_hy, y_da, y_mla, y_hg, gate, x, g1, w_out)


def _router_kernel(x_ref, g_ref, sc_ref, sh_ref, wrt_ref, bias_ref, *rest, compact):
    h = _rms(x_ref[0]) * g_ref[...] * (1.0 + sc_ref[0]) + sh_ref[0]
    tm = h.shape[0]
    scores = jax.nn.sigmoid(_dot_nt(wrt_ref[...], h, precision=HIGHEST))
    choice = scores + bias_ref[...]
    per = N_EXPERTS // N_EXPERT_GROUPS
    neg = -jnp.inf
    iota_g = lax.broadcasted_iota(jnp.int32, (per, tm), 0)
    grp_rows = []
    for gi in range(N_EXPERT_GROUPS):
        blk = choice[gi * per:(gi + 1) * per]
        m1 = jnp.max(blk, axis=0, keepdims=True)
        first = jnp.min(jnp.where(blk == m1, iota_g, per), axis=0, keepdims=True)
        m2 = jnp.max(jnp.where(iota_g == first, neg, blk), axis=0, keepdims=True)
        grp_rows.append(m1 + m2)
    grp = jnp.concatenate(grp_rows, axis=0)
    iota_n = lax.broadcasted_iota(jnp.int32, (N_EXPERT_GROUPS, tm), 0)
    gsel = jnp.zeros((N_EXPERT_GROUPS, tm), F32)
    for _ in range(TOPK_GROUPS):
        m = jnp.max(grp, axis=0, keepdims=True)
        first = jnp.min(jnp.where(grp == m, iota_n, N_EXPERT_GROUPS), axis=0, keepdims=True)
        hit = iota_n == first
        gsel = jnp.where(hit, 1.0, gsel)
        grp = jnp.where(hit, neg, grp)
    emask = jnp.concatenate([jnp.broadcast_to(gsel[gi:gi + 1], (per, tm)) for gi in range(N_EXPERT_GROUPS)], axis=0)
    cand = jnp.where(emask > 0.0, choice, neg)
    iota_e = lax.broadcasted_iota(jnp.int32, (N_EXPERTS, tm), 0)
    sel = jnp.zeros((N_EXPERTS, tm), F32)
    chosen = []
    for _ in range(TOP_K):
        m = jnp.max(cand, axis=0, keepdims=True)
        first = jnp.min(jnp.where(cand == m, iota_e, N_EXPERTS), axis=0, keepdims=True)
        hit = iota_e == first
        sel = jnp.where(hit, 1.0, sel)
        cand = jnp.where(hit, neg, cand)
        chosen.append(first)
    w = scores * sel
    gate = w / jnp.sum(w, axis=0, keepdims=True) * ROUTED_SCALE
    if not compact:
        h_ref, gate_ref = rest
        h_ref[0] = h.astype(BF16)
        gate_ref[0] = gate
        return
    hp_ref, eid_ref, rank_ref, w_ref, cnt_out_ref, cnt_ref = rest
    hp_ref[0] = _pack_halves(h)

    @pl.when((pl.program_id(0) == 0) & (pl.program_id(1) == 0))
    def _():
        cnt_ref[...] = jnp.zeros(cnt_ref.shape, F32)

    src = lax.broadcasted_iota(jnp.int32, (tm, tm), 0)
    dst = lax.broadcasted_iota(jnp.int32, (tm, tm), 1)
    running = jnp.dot(sel.astype(BF16), (src <= dst).astype(BF16), preferred_element_type=F32)
    rank_dense = cnt_ref[:, 0:1] + running - 1.0
    e_rows, r_rows, w_rows = [], [], []
    for first in chosen:
        hit = iota_e == first
        e_rows.append(first)
        r_rows.append(jnp.sum(jnp.where(hit, rank_dense, 0.0), axis=0, keepdims=True))
        w_rows.append(jnp.sum(jnp.where(hit, gate, 0.0), axis=0, keepdims=True))
    eid_ref[...] = jnp.concatenate(e_rows, axis=0)
    rank_ref[...] = jnp.concatenate(r_rows, axis=0).astype(jnp.int32)
    w_ref[...] = jnp.concatenate(w_rows, axis=0)
    cnt_ref[...] = cnt_ref[...] + running[:, tm - 1:tm]
    cnt_out_ref[...] = cnt_ref[...]


def _router(x, g, scale, shift, w_router, e_bias, tm=512, compact=False):
    b, s, d = x.shape
    tm = min(tm, s)
    e = w_router.shape[1]
    row = pl.BlockSpec((1, tm, d), lambda i, j: (i, j, 0))
    mod = pl.BlockSpec((1, 1, d), lambda i, j: (i, 0, 0))
    if compact:
        nj = s // tm
        tok = pl.BlockSpec((TOP_K, tm), lambda i, j: (0, i * nj + j))
        out_specs = [pl.BlockSpec((1, tm, d // 2), lambda i, j: (i, j, 0)), tok, tok, tok, _const_spec((e, LANES))]
        out_shape = [jax.ShapeDtypeStruct((b, s, d // 2), jnp.int32), jax.ShapeDtypeStruct((TOP_K, b * s), jnp.int32),
                     jax.ShapeDtypeStruct((TOP_K, b * s), jnp.int32), jax.ShapeDtypeStruct((TOP_K, b * s), F32),
                     jax.ShapeDtypeStruct((e, LANES), F32)]
        scratch = [pltpu.VMEM((e, LANES), F32)]
        semantics = ('arbitrary', 'arbitrary')
    else:
        out_specs = [row, pl.BlockSpec((1, e, tm), lambda i, j: (i, 0, j))]
        out_shape = [jax.ShapeDtypeStruct((b, s, d), BF16), jax.ShapeDtypeStruct((b, e, s), F32)]
        scratch = []
        semantics = ('parallel', 'parallel')
    return pl.pallas_call(
        functools.partial(_router_kernel, compact=compact),
        grid=(b, s // tm),
        in_specs=[row, _const_spec((1, d)), mod, mod, _const_spec((e, d)), _const_spec((e, 1))],
        out_specs=out_specs,
        out_shape=out_shape,
        scratch_shapes=scratch,
        compiler_params=_params(*semantics),
        name='router',
    )(x, g.reshape(1, d), scale, shift, w_router.T, e_bias.reshape(e, 1))


def _moe_kernel(h_ref, x_ref, gate_ref, g2_ref, wg_ref, wu_ref, wd_ref, sg_ref, su_ref, sd_ref, *rest, final):
    if final:
        fg_ref, o_ref, acc_ref = rest
    else:
        o_ref, acc_ref = rest
    e = pl.program_id(2)
    h = h_ref[0]

    @pl.when(e == 0)
    def _():
        a = jnp.dot(h, sg_ref[...], preferred_element_type=F32)
        u = jnp.dot(h, su_ref[...], preferred_element_type=F32)
        acc_ref[...] = jnp.dot((_silu(a) * u).astype(BF16), sd_ref[...], preferred_element_type=F32)

    lane = lax.broadcasted_iota(jnp.int32, gate_ref.shape[1:], 1)
    gcol = jnp.sum(jnp.where(lane == e, gate_ref[0], 0.0), axis=-1, keepdims=True)
    a = jnp.dot(h, wg_ref[0].astype(BF16), preferred_element_type=F32)
    u = jnp.dot(h, wu_ref[0].astype(BF16), preferred_element_type=F32)
    acc_ref[...] += jnp.dot((_silu(a) * u * gcol).astype(BF16), wd_ref[0].astype(BF16), preferred_element_type=F32)

    @pl.when(e == pl.num_programs(2) - 1)
    def _():
        y = x_ref[0] + g2_ref[0] * acc_ref[...]
        if final:
            y = _rms(y) * fg_ref[...]
        o_ref[0] = y


def _moe(h2, x, gate, g2, layer, w_gate, w_up, w_down, s_gate, s_up, s_down, final_g=None, tm=1024):
    b, s, d = x.shape
    tm = min(tm, s)
    _, e, _, ff = w_gate.shape
    row = pl.BlockSpec((1, tm, d), lambda i, j, k: (i, j, 0))
    ins = [h2, x, gate, g2, w_gate, w_up, w_down, s_gate, s_up, s_down]
    in_specs = [row, row, pl.BlockSpec((1, tm, e), lambda i, j, k: (i, j, 0)),
                pl.BlockSpec((1, 1, d), lambda i, j, k: (i, 0, 0)),
                pl.BlockSpec((None, 1, d, ff), lambda i, j, k: (layer, k, 0, 0)),
                pl.BlockSpec((None, 1, d, ff), lambda i, j, k: (layer, k, 0, 0)),
                pl.BlockSpec((None, 1, ff, d), lambda i, j, k: (layer, k, 0, 0)),
                _const_spec(s_gate.shape), _const_spec(s_up.shape), _const_spec(s_down.shape)]
    if final_g is not None:
        ins.append(final_g.reshape(1, d))
        in_specs.append(_const_spec((1, d)))
    return pl.pallas_call(
        functools.partial(_moe_kernel, final=final_g is not None),
        grid=(b, s // tm, e),
        in_specs=in_specs,
        out_specs=row,
        out_shape=jax.ShapeDtypeStruct(x.shape, F32),
        scratch_shapes=[pltpu.VMEM((tm, d), F32)],
        compiler_params=_params('parallel', 'parallel', 'arbitrary'),
        name='moe',
    )(*ins)


MOE_ROW_TILE = 512
SC_ROWS = 128
V7X_SC_CORES = 2
V7X_SC_SUBCORES = 16


def _pack_halves(x):
    n = x.shape[1] // 2
    lo = pltpu.bitcast(x[:, :n].astype(BF16).astype(F32), jnp.int32)
    hi = pltpu.bitcast(x[:, n:].astype(BF16).astype(F32), jnp.int32)
    return jnp.bitwise_or(jnp.bitwise_and(hi, -65536), lax.shift_right_logical(lo, 16))


def _unpack_halves(p):
    lo = pltpu.bitcast(lax.shift_left(p, 16), F32).astype(BF16)
    hi = pltpu.bitcast(jnp.bitwise_and(p, -65536), F32).astype(BF16)
    return lo, hi


def _route_pos_kernel(off_ref, eid_ref, rank_ref, pos_ref):
    eid = eid_ref[...]
    base = jnp.zeros(eid.shape, jnp.int32)
    for e in range(N_EXPERTS):
        base = jnp.where(eid == e, off_ref[e], base)
    pos_ref[...] = base + rank_ref[...]


def _route_pos(offsets, eid, rank):
    return pl.pallas_call(
        _route_pos_kernel,
        grid=(1,),
        in_specs=[pl.BlockSpec(memory_space=pltpu.SMEM), _const_spec(eid.shape), _const_spec(rank.shape)],
        out_specs=_const_spec(eid.shape),
        out_shape=jax.ShapeDtypeStruct(eid.shape, jnp.int32),
        compiler_params=_params('arbitrary'),
        name='route_pos',
    )(offsets, eid, rank)


def _sc_mesh():
    return plsc.VectorSubcoreMesh(core_axis_name='c', subcore_axis_name='s', num_cores=V7X_SC_CORES,
                                  num_subcores=V7X_SC_SUBCORES)


def _sc_dispatch(hp, pos, n_rows):
    t, w = hp.shape
    k = pos.shape[0]
    workers = V7X_SC_CORES * V7X_SC_SUBCORES
    per_worker = t // workers
    pos_flat = pos.reshape(k * t)

    @functools.partial(pl.kernel, mesh=_sc_mesh(), out_type=jax.ShapeDtypeStruct((n_rows, w), jnp.int32),
                       scratch_types=[pltpu.VMEM((SC_ROWS,), jnp.int32), pltpu.VMEM((SC_ROWS, w), jnp.int32),
                                      pltpu.SemaphoreType.DMA])
    def scatter(hp_hbm, pos_hbm, out_hbm, idx_v, rows_v, sem):
        wid = lax.axis_index('s') * V7X_SC_CORES + lax.axis_index('c')

        @pl.loop(0, per_worker // SC_ROWS)
        def _(i):
            t0 = pl.multiple_of(wid * per_worker + i * SC_ROWS, SC_ROWS)
            pltpu.sync_copy(hp_hbm.at[pl.ds(t0, SC_ROWS)], rows_v)
            for j in range(k):
                pltpu.sync_copy(pos_hbm.at[pl.ds(pl.multiple_of(j * t + t0, SC_ROWS), SC_ROWS)], idx_v)
                pltpu.async_copy(rows_v, out_hbm.at[idx_v], sem).wait()

    return scatter(hp, pos_flat)


def _sc_collect(yp, pos):
    _, w = yp.shape
    k, t = pos.shape
    workers = V7X_SC_CORES * V7X_SC_SUBCORES
    per_worker = k * t // workers
    pos_flat = pos.reshape(k * t)

    @functools.partial(pl.kernel, mesh=_sc_mesh(), out_type=jax.ShapeDtypeStruct((k * t, w), jnp.int32),
                       scratch_types=[pltpu.VMEM((SC_ROWS,), jnp.int32), pltpu.VMEM((SC_ROWS, w), jnp.int32),
                                      pltpu.SemaphoreType.DMA])
    def gather(yp_hbm, pos_hbm, out_hbm, idx_v, rows_v, sem):
        wid = lax.axis_index('s') * V7X_SC_CORES + lax.axis_index('c')

        @pl.loop(0, per_worker // SC_ROWS)
        def _(i):
            r0 = pl.multiple_of(wid * per_worker + i * SC_ROWS, SC_ROWS)
            pltpu.sync_copy(pos_hbm.at[pl.ds(r0, SC_ROWS)], idx_v)
            pltpu.async_copy(yp_hbm.at[idx_v], rows_v, sem).wait()
            pltpu.sync_copy(rows_v, out_hbm.at[pl.ds(r0, SC_ROWS)])

    return gather(yp, pos_flat)


def _expert_kernel(te_ref, nu_ref, x_ref, wg_ref, wu_ref, wd_ref, o_ref, wg_s, wu_s, wd_s):
    i = pl.program_id(0)

    @pl.when(i < nu_ref[0])
    def _():
        @pl.when((i == 0) | (te_ref[i] != te_ref[jnp.maximum(i - 1, 0)]))
        def _():
            wg_s[...] = wg_ref[0].astype(BF16)
            wu_s[...] = wu_ref[0].astype(BF16)
            wd_s[...] = wd_ref[0].astype(BF16)

        lo, hi = _unpack_halves(x_ref[...])
        half = lo.shape[1]
        a = (jnp.dot(lo, wg_s[:half, :], preferred_element_type=F32)
             + jnp.dot(hi, wg_s[half:, :], preferred_element_type=F32))
        u = (jnp.dot(lo, wu_s[:half, :], preferred_element_type=F32)
             + jnp.dot(hi, wu_s[half:, :], preferred_element_type=F32))
        y = jnp.dot((_silu(a) * u).astype(BF16), wd_s[...], preferred_element_type=F32)
        o_ref[...] = _pack_halves(y)


def _experts(xp, tile_expert, n_used, layer, w_gate, w_up, w_down):
    n_rows, half = xp.shape
    _, _, d, ff = w_gate.shape
    r = MOE_ROW_TILE
    row = pl.BlockSpec((r, half), lambda i, te, nu: (i, 0))
    grid_spec = pltpu.PrefetchScalarGridSpec(
        num_scalar_prefetch=2,
        grid=(n_rows // r,),
        in_specs=[row,
                  pl.BlockSpec((None, 1, d, ff), lambda i, te, nu: (layer, te[i], 0, 0)),
                  pl.BlockSpec((None, 1, d, ff), lambda i, te, nu: (layer, te[i], 0, 0)),
                  pl.BlockSpec((None, 1, ff, d), lambda i, te, nu: (layer, te[i], 0, 0))],
        out_specs=row,
        scratch_shapes=[pltpu.VMEM((d, ff), BF16), pltpu.VMEM((d, ff), BF16), pltpu.VMEM((ff, d), BF16)],
    )
    return pl.pallas_call(
        _expert_kernel,
        grid_spec=grid_spec,
        out_shape=jax.ShapeDtypeStruct((n_rows, half), jnp.int32),
        compiler_params=_params('arbitrary'),
        name='experts',
    )(tile_expert, n_used, xp, w_gate, w_up, w_down)


def _combine_kernel(yg_ref, w_ref, hp_ref, x_ref, g2_ref, sg_ref, su_ref, sd_ref, *rest, final):
    if final:
        fg_ref, o_ref = rest
    else:
        (o_ref,) = rest
    half = hp_ref.shape[2]
    lo, hi = _unpack_halves(hp_ref[0])
    sg, su = sg_ref[...], su_ref[...]
    a = jnp.dot(lo, sg[:half], preferred_element_type=F32) + jnp.dot(hi, sg[half:], preferred_element_type=F32)
    u = jnp.dot(lo, su[:half], preferred_element_type=F32) + jnp.dot(hi, su[half:], preferred_element_type=F32)
    acc = jnp.dot((_silu(a) * u).astype(BF16), sd_ref[...], preferred_element_type=F32)
    acc_lo, acc_hi = acc[:, :half], acc[:, half:]
    wts = w_ref[0]
    for k in range(yg_ref.shape[0]):
        ylo, yhi = _unpack_halves(yg_ref[k, 0])
        wk = wts[:, k:k + 1]
        acc_lo = acc_lo + wk * ylo.astype(F32)
        acc_hi = acc_hi + wk * yhi.astype(F32)
    y = x_ref[0] + g2_ref[0] * jnp.concatenate([acc_lo, acc_hi], axis=1)
    if final:
        y = _rms(y) * fg_ref[...]
    o_ref[0] = y


def _combine(yg, wts, hp, x, g2, s_gate, s_up, s_down, final_g=None, tm=256):
    b, s, d = x.shape
    k = yg.shape[0]
    half = d // 2
    row = pl.BlockSpec((1, tm, d), lambda i, j: (i, j, 0))
    prow = pl.BlockSpec((1, tm, half), lambda i, j: (i, j, 0))
    ins = [yg, wts, hp, x, g2, s_gate, s_up, s_down]
    in_specs = [pl.BlockSpec((k, 1, tm, half), lambda i, j: (0, i, j, 0)), pl.BlockSpec((1, tm, k), lambda i, j: (i, j, 0)),
                prow, row, pl.BlockSpec((1, 1, d), lambda i, j: (i, 0, 0)),
                _const_spec(s_gate.shape), _const_spec(s_up.shape), _const_spec(s_down.shape)]
    if final_g is not None:
        ins.append(final_g.reshape(1, d))
        in_specs.append(_const_spec((1, d)))
    return pl.pallas_call(
        functools.partial(_combine_kernel, final=final_g is not None),
        grid=(b, s // tm),
        in_specs=in_specs,
        out_specs=row,
        out_shape=jax.ShapeDtypeStruct(x.shape, F32),
        compiler_params=_params('parallel', 'parallel'),
        name='moe_combine',
    )(*ins)


def _routed_moe(x, g, scale, shift, g2, w_router, e_bias, layer, w_gate, w_up, w_down, s_gate, s_up, s_down, final_g=None):
    b, s, d = x.shape
    t = b * s
    hp, eid, rank, wts, counts = _router(x, g, scale, shift, w_router, e_bias, compact=True)
    counts = counts[:, 0].astype(jnp.int32)
    r = MOE_ROW_TILE
    padded = (counts + (r - 1)) // r * r
    ends = jnp.cumsum(padded)
    offsets = ends - padded
    n_rows = t * TOP_K + N_EXPERTS * r
    tile_start = jnp.arange(n_rows // r, dtype=jnp.int32) * r
    tile_expert = jnp.minimum(jnp.sum((tile_start[:, None] >= ends[None, :]).astype(jnp.int32), axis=1), N_EXPERTS - 1)
    n_used = (ends[-1] // r).reshape(1).astype(jnp.int32)
    pos = _route_pos(offsets.astype(jnp.int32), eid, rank)
    xp = _sc_dispatch(hp.reshape(t, d // 2), pos, n_rows)
    yp = _experts(xp, tile_expert.astype(jnp.int32), n_used, layer, w_gate, w_up, w_down)
    yg = _sc_collect(yp, pos).reshape(TOP_K, b, s, d // 2)
    return _combine(yg, wts.T.reshape(b, s, TOP_K), hp, x, g2, s_gate, s_up, s_down, final_g)


def _mixers(p, pc, ctx_out, prm, l, lam_init, rope_tabs, lb_terms):
    s = p['hy_v'].shape[1]
    sc = pc['hy_v'].shape[1]

    hy_args = (prm['hy_w1'][l], prm['hy_b1'][l], prm['hy_w2'][l], prm['hy_b2'][l], prm['hy_w3'][l], prm['hy_b3'][l],
               prm['hy_sin_freq'][l], prm['hy_decay'][l])
    y_hy = _hyena([p['hy_v'], p['hy_x1'], p['hy_x2']], prm['hy_conv_w'][l], prm['hy_conv_b'][l],
                  _hy_filters(s, *hy_args), prm['hy_bias'][l], inner=128)
    yc_hy = None
    if ctx_out:
        yc_hy = _hyena([pc['hy_v'], pc['hy_x1'], pc['hy_x2']], prm['hy_conv_w'][l], prm['hy_conv_b'][l],
                       _hy_filters(sc, *hy_args), prm['hy_bias'][l], inner=32)

    lp = prm['da_lambda'][l].astype(F32)
    lam = jnp.exp(jnp.sum(lp[0] * lp[1])) - jnp.exp(jnp.sum(lp[2] * lp[3])) + lam_init
    da_kw = dict(heads=DA_HEADS, ncomp=2, scale=DA_HEAD_DIM ** -0.5, lam=lam, subln_g=prm['da_subln_g'][l],
                 post_scale=1.0 - lam_init)
    da_ctx = ([pc['da_k']], pc['da_v'])
    y_da = _attention([p['da_q']], [da_ctx, ([p['da_k']], p['da_v'])], **da_kw)
    yc_da = _attention([pc['da_q']], [da_ctx], **da_kw) if ctx_out else None

    wq = prm['mla_w_q_up'][l].reshape(MLA_Q_RANK, MLA_HEADS, MLA_NOPE_DIM + MLA_ROPE_DIM)
    wq_n = wq[:, :, :MLA_NOPE_DIM].reshape(MLA_Q_RANK, -1).astype(BF16)
    wq_r = wq[:, :, MLA_NOPE_DIM:].reshape(MLA_Q_RANK, -1).astype(BF16)
    wkv = prm['mla_w_kv_up'][l].reshape(MLA_KV_RANK, MLA_HEADS, MLA_NOPE_DIM + MLA_V_DIM)
    wkv_n = wkv[:, :, :MLA_NOPE_DIM].reshape(MLA_KV_RANK, -1).astype(BF16)
    wkv_v = wkv[:, :, MLA_NOPE_DIM:].reshape(MLA_KV_RANK, -1).astype(BF16)

    def queries(qd, tabs):
        return _norm_proj(qd, prm['mla_q_norm_g'][l], [(wq_n, F32, False, MLA_HEADS), (wq_r, F32, True, MLA_HEADS)],
                          rope_tabs=tabs)

    def keys_values(kvd):
        return _norm_proj(kvd, prm['mla_kv_norm_g'][l], [(wkv_n, BF16, False, MLA_HEADS), (wkv_v, BF16, False, MLA_HEADS)])

    kn_l, v_l = keys_values(p['mla_kv'])
    kn_c, v_c = keys_values(pc['mla_kv'])
    mla_kw = dict(heads=MLA_HEADS, ncomp=1, scale=(MLA_NOPE_DIM + MLA_ROPE_DIM) ** -0.5)
    mla_ctx = ([kn_c, pc['mla_kr']], v_c)
    y_mla = _attention(queries(p['mla_q'], rope_tabs), [mla_ctx, ([kn_l, p['mla_kr']], v_l)], **mla_kw)
    yc_mla = _attention(queries(pc['mla_q'], None), [mla_ctx], **mla_kw) if ctx_out else None

    o, oc = _hgrn(p['hg_q'], p['hg_ff'], p['hg_fb'], p['hg_i'], pc['hg_q'], pc['hg_ff'], pc['hg_fb'], pc['hg_i'],
                  lb_terms, prm['hg_norm_g'][l])
    return (y_hy, y_da, y_mla, o), (yc_hy, yc_da, yc_mla, oc)


def kernel(x, c, ctx, c_ctx, w_ada, b_ada, norm1_g, norm2_g, w_in, w_out, hy_conv_w, hy_conv_b, hy_w1, hy_b1, hy_w2, hy_b2, hy_w3, hy_b3, hy_sin_freq, hy_decay, hy_bias, da_lambda, da_subln_g, mla_q_norm_g, mla_w_q_up, mla_kv_norm_g, mla_w_kv_up, hg_lower_bounds, hg_norm_g, moe_w_router, moe_bias, moe_w_gate, moe_w_up, moe_w_down, moe_sh_gate, moe_sh_up, moe_sh_down, final_norm_g):
    prm = dict(hy_conv_w=hy_conv_w, hy_conv_b=hy_conv_b, hy_w1=hy_w1, hy_b1=hy_b1, hy_w2=hy_w2, hy_b2=hy_b2,
               hy_w3=hy_w3, hy_b3=hy_b3, hy_sin_freq=hy_sin_freq, hy_decay=hy_decay, hy_bias=hy_bias,
               da_lambda=da_lambda, da_subln_g=da_subln_g, mla_q_norm_g=mla_q_norm_g, mla_w_q_up=mla_w_q_up,
               mla_kv_norm_g=mla_kv_norm_g, mla_w_kv_up=mla_w_kv_up, hg_norm_g=hg_norm_g)
    b, n_lat, d = x.shape
    depth = w_in.shape[0]
    rows = n_lat // GRID_W
    row_pos = jnp.repeat(jnp.arange(rows, dtype=jnp.int32), GRID_W)
    col_pos = jnp.tile(jnp.arange(GRID_W, dtype=jnp.int32), rows)
    rope_tabs = _rope_tables(row_pos, col_pos, 2 * DA_HEADS * DA_HEAD_DIM)
    lbs = jnp.cumsum(jax.nn.softmax(hg_lower_bounds.astype(F32), axis=1), axis=1)
    lbs = lbs - lbs[:, :1]
    cond = jnp.concatenate([c, c_ctx[None], jnp.zeros((8 - b - 1, d), F32)], axis=0)

    for l in range(depth):
        ctx_out = l < depth - 1
        mods = _ada(cond, w_ada[l], b_ada[l])
        sh1, sc1, g1, sh2, sc2, g2 = [m[:, None, :] for m in jnp.split(mods[:b], 6, axis=-1)]
        mc = [jnp.broadcast_to(m[:, None, :], (b, 1, d)) for m in jnp.split(mods[b:b + 1], 6, axis=-1)]

        off = 0
        outs = []
        for _, wdt, dt, rope, split, rep in _SEGMENTS:
            w = w_in[l][:, off:off + wdt].astype(BF16)
            outs.append((jnp.tile(w, (1, rep)) if rep > 1 else w, dt, rope, split))
            off += wdt
        names = [seg[0] for seg in _SEGMENTS]
        p = dict(zip(names, _norm_proj(x, norm1_g[l], outs, sc1, sh1, rope_tabs=rope_tabs)))
        pc = dict(zip(names, _norm_proj(ctx, norm1_g[l], outs, mc[1], mc[0])))

        lb = lbs[:, l]
        lb_terms = jnp.stack([jnp.log(lb), jnp.log1p(-lb), 1.0 - lb], axis=1)
        lam_init = 0.8 - 0.6 * math.exp(-0.3 * l)
        lat_parts, ctx_parts = _mixers(p, pc, ctx_out, prm, l, lam_init, rope_tabs, lb_terms)

        w_out_b = w_out[l].astype(BF16)
        moe_w = (l, moe_w_gate, moe_w_up, moe_w_down,
                 moe_sh_gate[l].astype(BF16), moe_sh_up[l].astype(BF16), moe_sh_down[l].astype(BF16))

        if ctx_out:
            ctx = _out_proj(*ctx_parts, pc['hg_g'], ctx, mc[2], w_out_b)
            flat = ctx.reshape(1, -1, d)
            h2c, gate_c = _router(flat, norm2_g[l], mc[4][:1], mc[3][:1], moe_w_router[l], moe_bias[l])
            ctx = _moe(h2c, flat, gate_c.transpose(0, 2, 1), mc[5][:1], *moe_w).reshape(ctx.shape)

        x = _out_proj(*lat_parts, p['hg_g'], x, g1, w_out_b)
        x = _routed_moe(x, norm2_g[l], sc2, sh2, g2, moe_w_router[l], moe_bias[l], *moe_w,
                        final_g=None if ctx_out else final_norm_g)

    return x
```

```python
import functools
import math

import numpy as np
import jax
import jax.numpy as jnp
from jax import lax
from jax.experimental import pallas as pl
from jax.experimental.pallas import tpu as pltpu
from jax.experimental.pallas import tpu_sc as plsc

F32 = jnp.float32
BF16 = jnp.bfloat16
HIGHEST = lax.Precision.HIGHEST

D_MODEL = 1024
GRID_W = 64
HY_WIDTH = 256
HY_ORDER = 2
HY_BANDS = 16
DA_HEADS = 4
DA_HEAD_DIM = 32
MLA_HEADS = 4
MLA_Q_RANK = 192
MLA_KV_RANK = 128
MLA_NOPE_DIM = 64
MLA_ROPE_DIM = 32
MLA_V_DIM = 64
HG_HEADS = 4
HG_KEY_DIM = 64
HG_VAL_DIM = 64
HG_CHUNK = 64
HG_SUB = 8
HG_UNROLL = 4
N_EXPERTS = 64
N_EXPERT_GROUPS = 8
TOPK_GROUPS = 4
TOP_K = 8
EXPERT_FF = 256
ROUTED_SCALE = 2.5
ROPE_BASE = 10000.0
NORM_EPS = 1e-6

V7X_VMEM_LIMIT_BYTES = 56 * 1024 * 1024
LANES = 128

_SEGMENTS = (
    ('hy_v', HY_WIDTH, F32, False, 0, 1), ('hy_x1', HY_WIDTH, F32, False, 0, 1), ('hy_x2', HY_WIDTH, F32, False, 0, 1),
    ('da_q', 2 * DA_HEADS * DA_HEAD_DIM, F32, True, 2 * DA_HEADS, 1),
    ('da_k', 2 * DA_HEADS * DA_HEAD_DIM, BF16, True, 2 * DA_HEADS, 1),
    ('da_v', 2 * DA_HEADS * DA_HEAD_DIM, BF16, False, DA_HEADS, 1),
    ('mla_q', MLA_Q_RANK, F32, False, 0, 1), ('mla_kv', MLA_KV_RANK, F32, False, 0, 1),
    ('mla_kr', MLA_ROPE_DIM, BF16, True, MLA_HEADS, MLA_HEADS),
    ('hg_q', HG_HEADS * HG_KEY_DIM, F32, False, 0, 1), ('hg_ff', HG_HEADS * HG_KEY_DIM, F32, False, 0, 1),
    ('hg_fb', HG_HEADS * HG_KEY_DIM, F32, False, 0, 1), ('hg_i', HG_HEADS * HG_VAL_DIM, F32, False, 0, 1),
    ('hg_g', HG_HEADS * HG_VAL_DIM, F32, False, 0, 1),
)


def _params(*semantics):
    return pltpu.CompilerParams(dimension_semantics=semantics, vmem_limit_bytes=V7X_VMEM_LIMIT_BYTES)


def _const_spec(shape):
    nd = len(shape)
    return pl.BlockSpec(shape, lambda *_: (0,) * nd)


def _rms(x, eps=NORM_EPS):
    return x * lax.rsqrt(jnp.mean(x * x, axis=-1, keepdims=True) + eps)


def _silu(x):
    return x * jax.nn.sigmoid(x)


def _dot_nt(a, b, **kw):
    return lax.dot_general(a, b, (((1,), (1,)), ((), ())), preferred_element_type=F32, **kw)


def _ada_kernel(c_ref, w_ref, b_ref, o_ref):
    s = _silu(c_ref[...])
    o_ref[...] = jnp.dot(s, w_ref[...], precision=HIGHEST, preferred_element_type=F32) + b_ref[...]


def _ada(cond, w, b):
    r, d = cond.shape
    n = w.shape[1]
    tn = 1536
    return pl.pallas_call(
        _ada_kernel,
        grid=(n // tn,),
        in_specs=[_const_spec((r, d)), pl.BlockSpec((d, tn), lambda j: (0, j)), pl.BlockSpec((1, tn), lambda j: (0, j))],
        out_specs=pl.BlockSpec((r, tn), lambda j: (0, j)),
        out_shape=jax.ShapeDtypeStruct((r, n), F32),
        compiler_params=_params('arbitrary'),
        name='ada',
    )(cond, w, b.reshape(1, n))


ROPE_UNIT = 32


def _rope_tables(row, col, width):
    n = ROPE_UNIT // 4
    inv = ROPE_BASE ** (-jnp.arange(n, dtype=F32) / n)
    units = width // ROPE_UNIT
    parts_c, parts_a, parts_b = [], [], []
    zero = jnp.zeros((row.shape[0], n), F32)
    for pos in (row, col):
        ang = pos.astype(F32)[:, None] * inv
        cos, sin = jnp.cos(ang), jnp.sin(ang)
        parts_c += [cos, cos]
        parts_a += [zero, sin]
        parts_b += [-sin, zero]
    tile = lambda ps: jnp.tile(jnp.concatenate(ps, axis=1), (1, units))
    return tile(parts_c), tile(parts_a), tile(parts_b)


def _norm_proj_kernel(*refs, n_w, modulate, ropes, splits):
    x_ref, g_ref = refs[0], refs[1]
    pos = 2
    if modulate:
        sc_ref, sh_ref = refs[2], refs[3]
        pos = 4
    if any(ropes):
        rc_ref, ra_ref, rb_ref = refs[pos:pos + 3]
        pos += 3
    w_refs = refs[pos:pos + n_w]
    o_refs = refs[pos + n_w:]
    y = _rms(x_ref[0]) * g_ref[...]
    if modulate:
        y = y * (1.0 + sc_ref[0]) + sh_ref[0]
    yb = y.astype(BF16)
    for w_ref, o_ref, rope, split in zip(w_refs, o_refs, ropes, splits):
        o = jnp.dot(yb, w_ref[...], preferred_element_type=F32)
        if rope:
            wd = o.shape[1]
            shift = ROPE_UNIT // 4
            o = (o * rc_ref[:, :wd] + pltpu.roll(o, shift, axis=1) * ra_ref[:, :wd]
                 + pltpu.roll(o, wd - shift, axis=1) * rb_ref[:, :wd])
        if split:
            unit = o.shape[1] // split
            for u in range(split):
                o_ref[0, u] = o[:, u * unit:(u + 1) * unit].astype(o_ref.dtype)
        else:
            o_ref[0] = o.astype(o_ref.dtype)


def _norm_proj(x, g, outs, scale=None, shift=None, rope_tabs=None, tm=512):
    b, s, k = x.shape
    tm = min(tm, s)
    modulate = scale is not None
    ropes = tuple(bool(o[2]) and rope_tabs is not None for o in outs)
    splits = tuple(o[3] for o in outs)
    ins = [x, g.reshape(1, k)]
    in_specs = [pl.BlockSpec((1, tm, k), lambda i, j: (i, j, 0)), _const_spec((1, k))]
    if modulate:
        ins += [scale, shift]
        in_specs += [pl.BlockSpec((1, 1, k), lambda i, j: (i, 0, 0))] * 2
    if any(ropes):
        ins += list(rope_tabs)
        in_specs += [pl.BlockSpec((tm, rope_tabs[0].shape[1]), lambda i, j: (j, 0))] * 3
    out_specs, out_shape = [], []
    for w, dt, _, split in outs:
        ins.append(w)
        in_specs.append(_const_spec(w.shape))
        n = w.shape[1]
        if split:
            out_specs.append(pl.BlockSpec((1, split, tm, n // split), lambda i, j: (i, 0, j, 0)))
            out_shape.append(jax.ShapeDtypeStruct((b, split, s, n // split), dt))
        else:
            out_specs.append(pl.BlockSpec((1, tm, n), lambda i, j: (i, j, 0)))
            out_shape.append(jax.ShapeDtypeStruct((b, s, n), dt))
    return pl.pallas_call(
        functools.partial(_norm_proj_kernel, n_w=len(outs), modulate=modulate, ropes=ropes, splits=splits),
        grid=(b, s // tm),
        in_specs=in_specs,
        out_specs=out_specs,
        out_shape=out_shape,
        compiler_params=_params('parallel', 'parallel'),
        name='norm_proj',
    )(*ins)


def _hy_filter_kernel(w1t_ref, w1s_ref, w1c_ref, b1_ref, w2_ref, b2_ref, w3_ref, b3_ref, fr_ref, dec_ref, o_ref, *, n):
    t = lax.broadcasted_iota(jnp.int32, (n, 1), 0).astype(F32) / n
    bands = lax.broadcasted_iota(jnp.int32, (1, HY_BANDS), 1).astype(F32) + 1.0
    ang = (2.0 * jnp.pi) * t * bands
    pre = (t * w1t_ref[...]
           + jnp.dot(jnp.sin(ang), w1s_ref[...], precision=HIGHEST, preferred_element_type=F32)
           + jnp.dot(jnp.cos(ang), w1c_ref[...], precision=HIGHEST, preferred_element_type=F32)
           + b1_ref[...])
    hid = jnp.sin(fr_ref[0:1, :] * pre)
    hid = jnp.sin(fr_ref[1:2, :] * (jnp.dot(hid, w2_ref[...], precision=HIGHEST, preferred_element_type=F32) + b2_ref[...]))
    filt = jnp.dot(hid, w3_ref[...], precision=HIGHEST, preferred_element_type=F32) + b3_ref[...]
    filt = filt * jnp.exp(-t * jnp.abs(dec_ref[...]))
    col = jnp.sum(jnp.abs(filt), axis=0, keepdims=True) - jnp.abs(filt[0:1, :])
    w = HY_WIDTH
    for o in range(HY_ORDER):
        lo = o * 2 * w
        f0 = filt[0:1, lo:lo + w] + filt[0:1, lo + w:lo + 2 * w]
        inv = 1.0 / (col[:, lo:lo + w] + col[:, lo + w:lo + 2 * w] + jnp.abs(f0))
        o_ref[:, lo:lo + w] = filt[:, lo:lo + w] * inv
        o_ref[:, lo + w:lo + 2 * w] = filt[:, lo + w:lo + 2 * w] * inv


def _hy_filters(n, w1, b1, w2, b2, w3, b3, freq, decay):
    cols = w3.shape[1]
    ins = [w1[0:1], w1[1:1 + HY_BANDS], w1[1 + HY_BANDS:], b1.reshape(1, -1), w2, b2.reshape(1, -1), w3,
           b3.reshape(1, -1), freq, decay.reshape(1, -1)]
    out = pl.pallas_call(
        functools.partial(_hy_filter_kernel, n=n),
        grid=(1,),
        in_specs=[_const_spec(a.shape) for a in ins],
        out_specs=_const_spec((n, cols)),
        out_shape=jax.ShapeDtypeStruct((n, cols), F32),
        compiler_params=_params('arbitrary'),
        name='hy_filter',
    )(*ins)
    return out.reshape(n, HY_ORDER, 2, HY_WIDTH)


def _two_sided(filt_n):
    n = filt_n.shape[0]
    hf, hb = filt_n[:, :, 0], filt_n[:, :, 1]
    h = jnp.concatenate([hf[:1] + hb[:1], hf[1:], jnp.zeros((1,) + hf.shape[1:], F32), hb[:0:-1]], axis=0)
    return h.reshape(2 * n, HY_ORDER * HY_WIDTH)


def _short_conv_kernel(*refs, s):
    x_refs, w_refs, b_refs, o_refs = refs[0:3], refs[3:6], refs[6:9], refs[9:12]
    row = lax.broadcasted_iota(jnp.int32, (s, 1), 0)
    for x_ref, w_ref, b_ref, o_ref in zip(x_refs, w_refs, b_refs, o_refs):
        x = x_ref[0]
        prev = jnp.where(row == 0, 0.0, pltpu.roll(x, 1, axis=0))
        nxt = jnp.where(row == s - 1, 0.0, pltpu.roll(x, s - 1, axis=0))
        o_ref[0] = prev * w_ref[0:1, :] + x * w_ref[1:2, :] + nxt * w_ref[2:3, :] + b_ref[...]


def _short_conv(parts, conv_w, conv_b):
    b, s, c = parts[0].shape
    tc = LANES
    ws = [conv_w[:, i * c:(i + 1) * c] for i in range(3)]
    bs = [conv_b[i * c:(i + 1) * c].reshape(1, c) for i in range(3)]
    xspec = pl.BlockSpec((1, s, tc), lambda i, j: (i, 0, j))
    return pl.pallas_call(
        functools.partial(_short_conv_kernel, s=s),
        grid=(b, c // tc),
        in_specs=[xspec] * 3 + [pl.BlockSpec((3, tc), lambda i, j: (0, j))] * 3 + [pl.BlockSpec((1, tc), lambda i, j: (0, j))] * 3,
        out_specs=[xspec] * 3,
        out_shape=[jax.ShapeDtypeStruct((b, s, c), F32)] * 3,
        compiler_params=_params('parallel', 'parallel'),
        name='short_conv',
    )(*parts, *ws, *bs)


def _dft_cos_sin(rows, cols, period):
    ang = 2.0 * np.pi * ((np.arange(rows)[:, None] * np.arange(cols)[None, :]) % period) / period
    return np.cos(ang), np.sin(ang)


def _fft_tables(n, inner):
    big = 2 * n
    n1 = big // inner
    c1, s1 = _dft_cos_sin(n1, n1, n1)
    h = n1 // 2
    outer_data = np.block([[c1[:, :h], s1[:, :h]], [-s1[:, :h], c1[:, :h]]])
    outer_real = np.concatenate([c1, -s1], axis=0)
    outer_inv = np.block([[c1[:h, :], -s1[:h, :]], [s1[:h, :], c1[:h, :]]]) / big
    c2, s2 = _dft_cos_sin(inner, inner, inner)
    inner_fwd = np.block([[c2, s2], [-s2, c2]])
    inner_inv = np.block([[c2, -s2], [s2, c2]])
    ct, st = _dft_cos_sin(n1, inner, big)
    f = lambda a: jnp.asarray(a, F32)
    return dict(n1=n1, inner=inner, outer_data=_hi_lo_cols(outer_data), outer_real=_hi_lo_cols(outer_real),
                outer_inv=_hi_lo_cols(outer_inv),
                inner_fwd=_hi_lo_cols(inner_fwd), inner_inv=_hi_lo_cols(inner_inv),
                tw_cos=f(ct).reshape(n1, inner, 1), tw_sin=f(st).reshape(n1, inner, 1))


def _left_mm_kernel(m_ref, x_ref, o_ref):
    o_ref[0] = jnp.dot(m_ref[...], _hi_lo_rows(x_ref[0]), preferred_element_type=F32)


def _left_mm(m, x, tl=4096):
    p, k, l = x.shape
    mm = m.shape[0]
    tl = min(tl, l)
    return pl.pallas_call(
        _left_mm_kernel,
        grid=(p, l // tl),
        in_specs=[_const_spec(m.shape), pl.BlockSpec((1, k, tl), lambda i, j: (i, 0, j))],
        out_specs=pl.BlockSpec((1, mm, tl), lambda i, j: (i, 0, j)),
        out_shape=jax.ShapeDtypeStruct((p, mm, l), F32),
        compiler_params=_params('parallel', 'parallel'),
        name='fft_outer',
    )(m, x)


def _hi_lo_cols(m):
    m = np.asarray(m, np.float32)
    hi = m.astype(BF16)
    lo = (m - hi.astype(np.float32)).astype(BF16)
    return jnp.asarray(np.concatenate([hi, hi, lo], axis=1))


def _hi_lo_rows(x):
    hi = x.astype(BF16)
    lo = (x - hi.astype(F32)).astype(BF16)
    return jnp.concatenate([hi, lo, hi], axis=0)


def _inner_kernel(a_ref, twc_ref, tws_ref, gf_ref, *rest, convolve, inner, kb):
    for s in range(kb):
        ar, ai = a_ref[0, 0, s], a_ref[0, 1, s]
        tc, ts = twc_ref[s], tws_ref[s]
        br = ar * tc + ai * ts
        bi = ai * tc - ar * ts
        x = jnp.dot(gf_ref[...], _hi_lo_rows(jnp.concatenate([br, bi], axis=0)), preferred_element_type=F32)
        if not convolve:
            o_ref = rest[0]
            o_ref[0, 0, s] = x[:inner]
            o_ref[0, 1, s] = x[inner:]
            continue
        h_ref, gi_ref, o_ref = rest
        xr, xi = x[:inner], x[inner:]
        hr, hi = h_ref[0, 0, s], h_ref[0, 1, s]
        yr = xr * hr - xi * hi
        yi = xr * hi + xi * hr
        z = jnp.dot(gi_ref[...], _hi_lo_rows(jnp.concatenate([yr, yi], axis=0)), preferred_element_type=F32)
        zr, zi = z[:inner], z[inner:]
        o_ref[0, 0, s] = zr * tc - zi * ts
        o_ref[0, 1, s] = zi * tc + zr * ts


def _fft_inner(a, tab, c, h=None, h_block=0):
    p = a.shape[0]
    n1, inner = tab['n1'], tab['inner']
    a5 = a.reshape(p, 2, n1, inner, c)
    tc = 2 * LANES
    kb = 4
    blk = pl.BlockSpec((1, 2, kb, inner, tc), lambda k, j, i: (i, 0, k, 0, j))
    tw_spec = pl.BlockSpec((kb, inner, 1), lambda k, j, i: (k, 0, 0))
    ins = [a5, tab['tw_cos'], tab['tw_sin'], tab['inner_fwd']]
    in_specs = [blk, tw_spec, tw_spec, _const_spec(tab['inner_fwd'].shape)]
    if h is not None:
        ch = h.shape[-1] // inner
        nb = c // tc
        ins += [h.reshape(1, 2, n1, inner, ch), tab['inner_inv']]
        in_specs += [pl.BlockSpec((1, 2, kb, inner, tc), lambda k, j, i: (0, 0, k, 0, h_block * nb + j)),
                     _const_spec(tab['inner_inv'].shape)]
    out = pl.pallas_call(
        functools.partial(_inner_kernel, convolve=h is not None, inner=inner, kb=kb),
        grid=(n1 // kb, c // tc, p),
        in_specs=in_specs,
        out_specs=blk,
        out_shape=jax.ShapeDtypeStruct(a5.shape, F32),
        compiler_params=_params('parallel', 'parallel', 'parallel'),
        name='fft_inner',
    )(*ins)
    return out.reshape(p, 2 * n1, inner * c)


def _gate_kernel(m_ref, z_ref, u_ref, x_ref, bias_ref, *rest, chain):
    y = jnp.dot(m_ref[...], _hi_lo_rows(z_ref[0]), preferred_element_type=F32)
    nxt = x_ref[0] * (y + u_ref[0] * bias_ref[...])
    if chain:
        mf_ref, o_ref, a_ref = rest
        o_ref[0] = nxt
        a_ref[0] = jnp.dot(mf_ref[...], _hi_lo_rows(nxt), preferred_element_type=F32)
    else:
        rest[0][0] = nxt


def _fft_gate(tab, z, u, x, bias_l, chain, tl=4096):
    p, k2, l = z.shape
    n1 = tab['n1']
    tl = min(tl, l)
    row = pl.BlockSpec((1, n1, tl), lambda i, j: (i, 0, j))
    ins = [tab['outer_inv'], z, u, x, bias_l]
    in_specs = [_const_spec(tab['outer_inv'].shape), pl.BlockSpec((1, k2, tl), lambda i, j: (i, 0, j)), row, row,
                pl.BlockSpec((1, tl), lambda i, j: (0, j))]
    out_specs = [row]
    out_shape = [jax.ShapeDtypeStruct((p, n1, l), F32)]
    if chain:
        ins.append(tab['outer_data'])
        in_specs.append(_const_spec(tab['outer_data'].shape))
        out_specs.append(pl.BlockSpec((1, k2, tl), lambda i, j: (i, 0, j)))
        out_shape.append(jax.ShapeDtypeStruct((p, k2, l), F32))
    return pl.pallas_call(
        functools.partial(_gate_kernel, chain=chain),
        grid=(p, l // tl),
        in_specs=in_specs,
        out_specs=out_specs,
        out_shape=out_shape,
        compiler_params=_params('parallel', 'parallel'),
        name='fft_gate',
    )(*ins)


def _hyena(parts, conv_w, conv_b, filt_n, bias, inner):
    b, s, c = parts[0].shape
    tab = _fft_tables(s, inner)
    n1 = tab['n1']
    lanes = inner * c
    h_taps = _two_sided(filt_n).reshape(1, n1, inner * HY_ORDER * c)
    h_spec = _fft_inner(_left_mm(tab['outer_real'], h_taps), tab, HY_ORDER * c)
    v, x1, x2 = [a.reshape(b // 2, n1, lanes) for a in _short_conv(parts, conv_w, conv_b)]
    bias_l = [jnp.tile(bias[o], inner).reshape(1, lanes) for o in range(HY_ORDER)]
    a = _left_mm(tab['outer_data'], v)
    z = _fft_inner(a, tab, c, h_spec, 0)
    z2, a = _fft_gate(tab, z, v, x1, bias_l[0], chain=True)
    z = _fft_inner(a, tab, c, h_spec, 1)
    (z3,) = _fft_gate(tab, z, z2, x2, bias_l[1], chain=False)
    return z3.reshape(b, s, c)


def _attn_kernel(*refs, n_q, n_pieces, ncomp, scale, post_scale):
    q_refs = refs[:n_q]
    pos = n_q
    pieces = []
    for _ in range(n_pieces):
        pieces.append((refs[pos:pos + n_q], refs[pos + n_q]))
        pos += n_q + 1
    if ncomp == 2:
        lam_ref, g_ref = refs[pos:pos + 2]
        pos += 2
    o_ref, kcat_ref, vcat_ref = refs[pos:pos + 3]
    dv = o_ref.shape[3]

    @pl.when(pl.program_id(2) == 0)
    def _():
        row = 0
        for k_refs, v_ref in pieces:
            n = v_ref.shape[2]
            for c in range(ncomp):
                parts = [k_ref[0, c if k_ref.shape[1] == ncomp else 0] for k_ref in k_refs]
                kcat_ref[c, row:row + n, :] = parts[0] if n_q == 1 else jnp.concatenate(parts, axis=1)
            vcat_ref[row:row + n, :dv] = v_ref[0, 0]
            vcat_ref[row:row + n, dv:] = jnp.ones((n, dv), BF16)
            row += n

    outs = []
    for c in range(ncomp):
        q = q_refs[0][0, c] if n_q == 1 else jnp.concatenate([q_ref[0, c] for q_ref in q_refs], axis=1)
        s = _dot_nt((q * (scale * math.log2(math.e))).astype(BF16), kcat_ref[c])
        m = jnp.max(s, axis=-1, keepdims=True)
        p = jnp.exp2((s - m).astype(BF16))
        ol = jnp.dot(p, vcat_ref[...], preferred_element_type=F32)
        outs.append(ol[:, :dv] / ol[:, dv:dv + 1])
    if ncomp == 2:
        o = outs[0] - lam_ref[0] * outs[1]
        o = _rms(o) * g_ref[...] * post_scale
    else:
        o = outs[0]
    o_ref[0, 0] = o


def _attention(q_parts, pieces, heads, ncomp, scale, tq=256, lam=None, subln_g=None, post_scale=1.0):
    b, _, sq, _ = q_parts[0].shape
    dv = pieces[0][1].shape[3]
    tq = min(tq, sq)
    ins = list(q_parts)
    in_specs = [pl.BlockSpec((1, ncomp, tq, q.shape[3]), lambda i, h, j: (i, h, j, 0)) for q in q_parts]
    for k_parts, v in pieces:
        for k in k_parts:
            ins.append(k)
            if k.shape[1] == 1:
                in_specs.append(pl.BlockSpec((1, 1) + k.shape[2:], lambda i, h, j: (i, 0, 0, 0)))
            else:
                in_specs.append(pl.BlockSpec((1, ncomp) + k.shape[2:], lambda i, h, j: (i, h, 0, 0)))
        ins.append(v)
        in_specs.append(pl.BlockSpec((1, 1) + v.shape[2:], lambda i, h, j: (i, h, 0, 0)))
    if ncomp == 2:
        ins += [lam.reshape(1), subln_g.reshape(1, dv)]
        in_specs += [pl.BlockSpec(memory_space=pltpu.SMEM), _const_spec((1, dv))]
    sk = sum(v.shape[2] for _, v in pieces)
    dqk = sum(q.shape[3] for q in q_parts)
    return pl.pallas_call(
        functools.partial(_attn_kernel, n_q=len(q_parts), n_pieces=len(pieces), ncomp=ncomp, scale=scale,
                          post_scale=post_scale),
        grid=(b, heads, sq // tq),
        in_specs=in_specs,
        out_specs=pl.BlockSpec((1, 1, tq, dv), lambda i, h, j: (i, h, j, 0)),
        out_shape=jax.ShapeDtypeStruct((b, heads, sq, dv), F32),
        scratch_shapes=[pltpu.VMEM((ncomp, sk, dqk), BF16), pltpu.VMEM((sk, 2 * dv), BF16)],
        compiler_params=_params('parallel', 'parallel', 'arbitrary'),
        name='attention',
    )(*ins)


def _forget_terms(f, log_lb, log_1m_lb, one_m_lb):
    log_sig = jnp.minimum(f, 0.0) - jnp.log1p(jnp.exp(-jnp.abs(f)))
    b = log_1m_lb + log_sig
    log_g = jnp.maximum(log_lb, b) + jnp.log1p(jnp.exp(-jnp.abs(log_lb - b)))
    return log_g, one_m_lb * jax.nn.sigmoid(-f)


def _hg_tables():
    ck, sub = HG_CHUNK, HG_SUB
    t = np.arange(ck)
    cum_mats, half_masks, group_masks, keeps = [], [], [], []
    for rev in (False, True):
        cum_mats.append((t[None, :] >= t[:, None]) if rev else (t[None, :] <= t[:, None]))
        halves = []
        hs = ck // 2
        while hs >= sub:
            pos = t % (2 * hs)
            q_half = (pos < hs) if rev else (pos >= hs)
            halves.append(np.stack([q_half, ~q_half]))
            if not rev:
                grp = (t[:, None] // (2 * hs)) == (t[None, :] // (2 * hs))
                group_masks.append(np.concatenate([grp, grp], axis=0))
            hs //= 2
        half_masks.append(np.stack(halves))
        c = np.arange(ck * sub)
        tt, ss = (c // sub) % sub, c % sub
        keeps.append((ss >= tt) if rev else (ss <= tt))
    lanes = 2 * HG_KEY_DIM
    ln = np.arange(lanes)
    bd = (ln[:, None] // HG_KEY_DIM) == (ln[None, :] // HG_KEY_DIM)
    hm = np.broadcast_to(np.stack(half_masks)[..., None], (2, len(half_masks[0]), 2, ck, lanes))
    return (jnp.asarray(np.stack(cum_mats), BF16), jnp.asarray(hm, F32), jnp.asarray(np.stack(group_masks), F32),
            jnp.asarray(bd, F32), jnp.asarray(np.broadcast_to(np.stack(keeps)[..., None], (2, ck * sub, lanes)), F32))


def _split2(x):
    a = x.astype(BF16)
    return a, (x - a.astype(F32)).astype(BF16)


def _hg_chunk(q, k, v, lg, st, rev, cm, hm, gm, keep, bd, m0, m1):
    ck, sub = HG_CHUNK, HG_SUB
    lanes = lg.shape[1]
    c2 = jnp.dot(cm, jnp.concatenate(_split2(lg), axis=1), preferred_element_type=F32)
    cum = c2[:, :lanes] + c2[:, lanes:]
    tot = cum[0:1] if rev else cum[ck - 1:ck]
    o = _dot_nt((q * jnp.exp(cum)).astype(BF16), st.astype(BF16))
    kd = (k * jnp.exp(tot - cum)).astype(BF16)
    st_new = st * jnp.exp(tot) + bd * jnp.dot(v.T.astype(BF16), kd, preferred_element_type=F32)
    s2 = None
    for lv in range(gm.shape[0]):
        hs = ck >> (lv + 1)
        edge = [(g + hs) if rev else (g + hs - 1) for g in range(0, ck, 2 * hs)]
        cb = jnp.concatenate([jnp.broadcast_to(cum[e:e + 1], (2 * hs, lanes)) for e in edge], axis=0)
        qd = q * jnp.exp(jnp.minimum(cum - cb, 0.0)) * hm[lv, 0]
        kf = (k * jnp.exp(jnp.minimum(cb - cum, 0.0)) * hm[lv, 1]).astype(BF16)
        q2 = jnp.concatenate([qd * m0, qd * m1], axis=0).astype(BF16)
        term = _dot_nt(q2, kf) * gm[lv]
        s2 = term if s2 is None else s2 + term
    r = jnp.dot(s2.astype(BF16), v.astype(BF16), preferred_element_type=F32)
    o = o + m0 * r[:ck] + m1 * r[ck:]
    rows, vts = [], []
    for i in range(ck // sub):
        lo, hi = i * sub, (i + 1) * sub
        ki, ci = k[lo:hi], cum[lo:hi]
        for t in range(lo, hi):
            rows.append(q[t:t + 1] * ki * jnp.exp(jnp.minimum(cum[t:t + 1] - ci, 0.0)))
            vts.append(v[lo:hi])
    sc = jnp.dot(jnp.concatenate(rows, axis=0).astype(BF16), bd.astype(BF16), preferred_element_type=F32)
    prod = sc * jnp.concatenate(vts, axis=0) * keep
    o = o + jnp.sum(prod.reshape(ck, sub, lanes), axis=1)
    return o, st_new


def _hgrn_kernel(q_ref, ff_ref, fb_ref, i_ref, qc_ref, ffc_ref, fbc_ref, ic_ref, lb_ref, g_ref,
                 cm_ref, hm_ref, gm_ref, bd_ref, keep_ref, o_ref, oc_ref, or_ref, ocr_ref, st_ref, *, n_lat, n_ctx):
    ck = HG_CHUNK
    lanes = o_ref.shape[-1]
    lane = lax.broadcasted_iota(jnp.int32, (1, lanes), 1)
    m0 = (lane < HG_KEY_DIM).astype(F32)
    m1 = 1.0 - m0
    bd = bd_ref[...]
    gm = gm_ref[...]

    def one(q, f, v, rev):
        d = 1 if rev else 0
        lg, k = _forget_terms(f, lb_ref[d, 0:1, :], lb_ref[d, 1:2, :], lb_ref[d, 2:3, :])
        o, st = _hg_chunk(q, k, v, lg, st_ref[d], rev, cm_ref[d], hm_ref[d], gm, keep_ref[d], bd, m0, m1)
        st_ref[d] = st
        return o

    def sweep(qr, ffr, fbr, ir, out_f, out_r, n):
        nc = n // ck

        def body(step, carry):
            idf = pl.ds(pl.multiple_of(step * ck, ck), ck)
            idr = pl.ds(pl.multiple_of((nc - 1 - step) * ck, ck), ck)
            out_f[0, idf, :] = one(qr[0, idf, :], ffr[0, idf, :], ir[0, idf, :], False)
            out_r[idr, :] = one(qr[0, idr, :], fbr[0, idr, :], ir[0, idr, :], True)
            return carry

        lax.fori_loop(0, nc, body, 0, unroll=HG_UNROLL)

    st_ref[...] = jnp.zeros(st_ref.shape, F32)
    sweep(qc_ref, ffc_ref, fbc_ref, ic_ref, oc_ref, ocr_ref, n_ctx)
    sweep(q_ref, ff_ref, fb_ref, i_ref, o_ref, or_ref, n_lat)

    mean_mat = bd * (1.0 / HG_VAL_DIM)

    def readout(out, out_r, n):
        tile = min(n, 512)

        def body(step, carry):
            idx = pl.ds(pl.multiple_of(step * tile, tile), tile)
            x = out[0, idx, :] + out_r[idx, :]
            ms = jnp.dot(x * x, mean_mat, precision=HIGHEST, preferred_element_type=F32)
            out[0, idx, :] = x * lax.rsqrt(ms + NORM_EPS) * g_ref[...]
            return carry

        lax.fori_loop(0, n // tile, body, 0)

    readout(oc_ref, ocr_ref, n_ctx)
    readout(o_ref, or_ref, n_lat)


def _hgrn(q, ff, fb, iv, qc, ffc, fbc, ic, lb_terms, norm_g):
    b, n_lat, width = q.shape
    n_ctx = qc.shape[1]
    lanes = 2 * HG_KEY_DIM
    tables = _hg_tables()
    lat = pl.BlockSpec((1, n_lat, lanes), lambda i, j: (i, 0, j))
    ctx = pl.BlockSpec((1, n_ctx, lanes), lambda i, j: (i, 0, j))
    g2 = jnp.tile(norm_g, 2).reshape(1, lanes)
    return pl.pallas_call(
        functools.partial(_hgrn_kernel, n_lat=n_lat, n_ctx=n_ctx),
        grid=(b, width // lanes),
        in_specs=[lat] * 4 + [ctx] * 4 + [pl.BlockSpec((2, 3, lanes), lambda i, j: (0, 0, j)), _const_spec((1, lanes))]
                 + [_const_spec(t.shape) for t in tables],
        out_specs=[lat, ctx],
        out_shape=[jax.ShapeDtypeStruct(q.shape, F32), jax.ShapeDtypeStruct(qc.shape, F32)],
        scratch_shapes=[pltpu.VMEM((n_lat, lanes), F32), pltpu.VMEM((n_ctx, lanes), F32), pltpu.VMEM((2, lanes, lanes), F32)],
        compiler_params=_params('parallel', 'parallel'),
        name='hgrn2',
    )(q, ff, fb, iv, qc, ffc, fbc, ic, lb_terms, g2, *tables)


def _out_proj_kernel(hy_ref, da_ref, mla_ref, hg_ref, gate_ref, x_ref, g1_ref, w_ref, o_ref):
    c = hy_ref.shape[2]
    acc = jnp.dot(hy_ref[0].astype(BF16), w_ref[0:c, :], preferred_element_type=F32)
    for i, head_ref in ((1, da_ref), (2, mla_ref)):
        dv = head_ref.shape[3]
        for h in range(head_ref.shape[1]):
            lo = i * c + h * dv
            acc = acc + jnp.dot(head_ref[0, h].astype(BF16), w_ref[lo:lo + dv, :], preferred_element_type=F32)
    hg = hg_ref[0] * _silu(gate_ref[0])
    acc = acc + jnp.dot(hg.astype(BF16), w_ref[3 * c:4 * c, :], preferred_element_type=F32)
    o_ref[0] = x_ref[0] + g1_ref[0] * acc


def _out_proj(y_hy, y_da, y_mla, y_hg, gate, x, g1, w_out, tm=512):
    b, s, d = x.shape
    tm = min(tm, s)
    c = y_hy.shape[2]
    part = pl.BlockSpec((1, tm, c), lambda i, j: (i, j, 0))
    headed = lambda a: pl.BlockSpec((1, a.shape[1], tm, a.shape[3]), lambda i, j: (i, 0, j, 0))
    row = pl.BlockSpec((1, tm, d), lambda i, j: (i, j, 0))
    return pl.pallas_call(
        _out_proj_kernel,
        grid=(b, s // tm),
        in_specs=[part, headed(y_da), headed(y_mla), part, part, row, pl.BlockSpec((1, 1, d), lambda i, j: (i, 0, 0)),
                  _const_spec(w_out.shape)],
        out_specs=row,
        out_shape=jax.ShapeDtypeStruct(x.shape, F32),
        compiler_params=_params('parallel', 'parallel'),
        name='out_proj',
    )(y_hy, y_da, y_mla, y_hg, gate, x, g1, w_out)


def _router_kernel(x_ref, g_ref, sc_ref, sh_ref, wrt_ref, bias_ref, *rest, compact):
    h = _rms(x_ref[0]) * g_ref[...] * (1.0 + sc_ref[0]) + sh_ref[0]
    tm = h.shape[0]
    scores = jax.nn.sigmoid(_dot_nt(wrt_ref[...], h, precision=HIGHEST))
    choice = scores + bias_ref[...]
    per = N_EXPERTS // N_EXPERT_GROUPS
    neg = -jnp.inf
    iota_g = lax.broadcasted_iota(jnp.int32, (per, tm), 0)
    grp_rows = []
    for gi in range(N_EXPERT_GROUPS):
        blk = choice[gi * per:(gi + 1) * per]
        m1 = jnp.max(blk, axis=0, keepdims=True)
        first = jnp.min(jnp.where(blk == m1, iota_g, per), axis=0, keepdims=True)
        m2 = jnp.max(jnp.where(iota_g == first, neg, blk), axis=0, keepdims=True)
        grp_rows.append(m1 + m2)
    grp = jnp.concatenate(grp_rows, axis=0)
    iota_n = lax.broadcasted_iota(jnp.int32, (N_EXPERT_GROUPS, tm), 0)
    gsel = jnp.zeros((N_EXPERT_GROUPS, tm), F32)
    for _ in range(TOPK_GROUPS):
        m = jnp.max(grp, axis=0, keepdims=True)
        first = jnp.min(jnp.where(grp == m, iota_n, N_EXPERT_GROUPS), axis=0, keepdims=True)
        hit = iota_n == first
        gsel = jnp.where(hit, 1.0, gsel)
        grp = jnp.where(hit, neg, grp)
    emask = jnp.concatenate([jnp.broadcast_to(gsel[gi:gi + 1], (per, tm)) for gi in range(N_EXPERT_GROUPS)], axis=0)
    cand = jnp.where(emask > 0.0, choice, neg)
    iota_e = lax.broadcasted_iota(jnp.int32, (N_EXPERTS, tm), 0)
    sel = jnp.zeros((N_EXPERTS, tm), F32)
    chosen = []
    for _ in range(TOP_K):
        m = jnp.max(cand, axis=0, keepdims=True)
        first = jnp.min(jnp.where(cand == m, iota_e, N_EXPERTS), axis=0, keepdims=True)
        hit = iota_e == first
        sel = jnp.where(hit, 1.0, sel)
        cand = jnp.where(hit, neg, cand)
        chosen.append(first)
    w = scores * sel
    gate = w / jnp.sum(w, axis=0, keepdims=True) * ROUTED_SCALE
    if not compact:
        h_ref, gate_ref = rest
        h_ref[0] = h.astype(BF16)
        gate_ref[0] = gate
        return
    hp_ref, eid_ref, rank_ref, w_ref, cnt_out_ref, cnt_ref = rest
    hp_ref[0] = _pack_halves(h)

    @pl.when((pl.program_id(0) == 0) & (pl.program_id(1) == 0))
    def _():
        cnt_ref[...] = jnp.zeros(cnt_ref.shape, F32)

    src = lax.broadcasted_iota(jnp.int32, (tm, tm), 0)
    dst = lax.broadcasted_iota(jnp.int32, (tm, tm), 1)
    running = jnp.dot(sel.astype(BF16), (src <= dst).astype(BF16), preferred_element_type=F32)
    rank_dense = cnt_ref[:, 0:1] + running - 1.0
    e_rows, r_rows, w_rows = [], [], []
    for first in chosen:
        hit = iota_e == first
        e_rows.append(first)
        r_rows.append(jnp.sum(jnp.where(hit, rank_dense, 0.0), axis=0, keepdims=True))
        w_rows.append(jnp.sum(jnp.where(hit, gate, 0.0), axis=0, keepdims=True))
    eid_ref[...] = jnp.concatenate(e_rows, axis=0)
    rank_ref[...] = jnp.concatenate(r_rows, axis=0).astype(jnp.int32)
    w_ref[...] = jnp.concatenate(w_rows, axis=0)
    cnt_ref[...] = cnt_ref[...] + running[:, tm - 1:tm]
    cnt_out_ref[...] = cnt_ref[...]


def _router(x, g, scale, shift, w_router, e_bias, tm=512, compact=False):
    b, s, d = x.shape
    tm = min(tm, s)
    e = w_router.shape[1]
    row = pl.BlockSpec((1, tm, d), lambda i, j: (i, j, 0))
    mod = pl.BlockSpec((1, 1, d), lambda i, j: (i, 0, 0))
    if compact:
        nj = s // tm
        tok = pl.BlockSpec((TOP_K, tm), lambda i, j: (0, i * nj + j))
        out_specs = [pl.BlockSpec((1, tm, d // 2), lambda i, j: (i, j, 0)), tok, tok, tok, _const_spec((e, LANES))]
        out_shape = [jax.ShapeDtypeStruct((b, s, d // 2), jnp.int32), jax.ShapeDtypeStruct((TOP_K, b * s), jnp.int32),
                     jax.ShapeDtypeStruct((TOP_K, b * s), jnp.int32), jax.ShapeDtypeStruct((TOP_K, b * s), F32),
                     jax.ShapeDtypeStruct((e, LANES), F32)]
        scratch = [pltpu.VMEM((e, LANES), F32)]
        semantics = ('arbitrary', 'arbitrary')
    else:
        out_specs = [row, pl.BlockSpec((1, e, tm), lambda i, j: (i, 0, j))]
        out_shape = [jax.ShapeDtypeStruct((b, s, d), BF16), jax.ShapeDtypeStruct((b, e, s), F32)]
        scratch = []
        semantics = ('parallel', 'parallel')
    return pl.pallas_call(
        functools.partial(_router_kernel, compact=compact),
        grid=(b, s // tm),
        in_specs=[row, _const_spec((1, d)), mod, mod, _const_spec((e, d)), _const_spec((e, 1))],
        out_specs=out_specs,
        out_shape=out_shape,
        scratch_shapes=scratch,
        compiler_params=_params(*semantics),
        name='router',
    )(x, g.reshape(1, d), scale, shift, w_router.T, e_bias.reshape(e, 1))


def _moe_kernel(h_ref, x_ref, gate_ref, g2_ref, wg_ref, wu_ref, wd_ref, sg_ref, su_ref, sd_ref, *rest, final):
    if final:
        fg_ref, o_ref, acc_ref = rest
    else:
        o_ref, acc_ref = rest
    e = pl.program_id(2)
    h = h_ref[0]

    @pl.when(e == 0)
    def _():
        a = jnp.dot(h, sg_ref[...], preferred_element_type=F32)
        u = jnp.dot(h, su_ref[...], preferred_element_type=F32)
        acc_ref[...] = jnp.dot((_silu(a) * u).astype(BF16), sd_ref[...], preferred_element_type=F32)

    lane = lax.broadcasted_iota(jnp.int32, gate_ref.shape[1:], 1)
    gcol = jnp.sum(jnp.where(lane == e, gate_ref[0], 0.0), axis=-1, keepdims=True)
    a = jnp.dot(h, wg_ref[0].astype(BF16), preferred_element_type=F32)
    u = jnp.dot(h, wu_ref[0].astype(BF16), preferred_element_type=F32)
    acc_ref[...] += jnp.dot((_silu(a) * u * gcol).astype(BF16), wd_ref[0].astype(BF16), preferred_element_type=F32)

    @pl.when(e == pl.num_programs(2) - 1)
    def _():
        y = x_ref[0] + g2_ref[0] * acc_ref[...]
        if final:
            y = _rms(y) * fg_ref[...]
        o_ref[0] = y


def _moe(h2, x, gate, g2, layer, w_gate, w_up, w_down, s_gate, s_up, s_down, final_g=None, tm=1024):
    b, s, d = x.shape
    tm = min(tm, s)
    _, e, _, ff = w_gate.shape
    row = pl.BlockSpec((1, tm, d), lambda i, j, k: (i, j, 0))
    ins = [h2, x, gate, g2, w_gate, w_up, w_down, s_gate, s_up, s_down]
    in_specs = [row, row, pl.BlockSpec((1, tm, e), lambda i, j, k: (i, j, 0)),
                pl.BlockSpec((1, 1, d), lambda i, j, k: (i, 0, 0)),
                pl.BlockSpec((None, 1, d, ff), lambda i, j, k: (layer, k, 0, 0)),
                pl.BlockSpec((None, 1, d, ff), lambda i, j, k: (layer, k, 0, 0)),
                pl.BlockSpec((None, 1, ff, d), lambda i, j, k: (layer, k, 0, 0)),
                _const_spec(s_gate.shape), _const_spec(s_up.shape), _const_spec(s_down.shape)]
    if final_g is not None:
        ins.append(final_g.reshape(1, d))
        in_specs.append(_const_spec((1, d)))
    return pl.pallas_call(
        functools.partial(_moe_kernel, final=final_g is not None),
        grid=(b, s // tm, e),
        in_specs=in_specs,
        out_specs=row,
        out_shape=jax.ShapeDtypeStruct(x.shape, F32),
        scratch_shapes=[pltpu.VMEM((tm, d), F32)],
        compiler_params=_params('parallel', 'parallel', 'arbitrary'),
        name='moe',
    )(*ins)


MOE_ROW_TILE = 512
SC_ROWS = 128
V7X_SC_CORES = 2
V7X_SC_SUBCORES = 16


def _pack_halves(x):
    n = x.shape[1] // 2
    lo = pltpu.bitcast(x[:, :n].astype(BF16).astype(F32), jnp.int32)
    hi = pltpu.bitcast(x[:, n:].astype(BF16).astype(F32), jnp.int32)
    return jnp.bitwise_or(jnp.bitwise_and(hi, -65536), lax.shift_right_logical(lo, 16))


def _unpack_halves(p):
    lo = pltpu.bitcast(lax.shift_left(p, 16), F32).astype(BF16)
    hi = pltpu.bitcast(jnp.bitwise_and(p, -65536), F32).astype(BF16)
    return lo, hi


def _route_pos_kernel(off_ref, eid_ref, rank_ref, pos_ref):
    eid = eid_ref[...]
    base = jnp.zeros(eid.shape, jnp.int32)
    for e in range(N_EXPERTS):
        base = jnp.where(eid == e, off_ref[e], base)
    pos_ref[...] = base + rank_ref[...]


def _route_pos(offsets, eid, rank):
    return pl.pallas_call(
        _route_pos_kernel,
        grid=(1,),
        in_specs=[pl.BlockSpec(memory_space=pltpu.SMEM), _const_spec(eid.shape), _const_spec(rank.shape)],
        out_specs=_const_spec(eid.shape),
        out_shape=jax.ShapeDtypeStruct(eid.shape, jnp.int32),
        compiler_params=_params('arbitrary'),
        name='route_pos',
    )(offsets, eid, rank)


def _sc_mesh():
    return plsc.VectorSubcoreMesh(core_axis_name='c', subcore_axis_name='s', num_cores=V7X_SC_CORES,
                                  num_subcores=V7X_SC_SUBCORES)


def _sc_dispatch(hp, pos, n_rows):
    t, w = hp.shape
    k = pos.shape[0]
    workers = V7X_SC_CORES * V7X_SC_SUBCORES
    per_worker = t // workers
    pos_flat = pos.reshape(k * t)

    @functools.partial(pl.kernel, mesh=_sc_mesh(), out_type=jax.ShapeDtypeStruct((n_rows, w), jnp.int32),
                       scratch_types=[pltpu.VMEM((SC_ROWS,), jnp.int32), pltpu.VMEM((SC_ROWS, w), jnp.int32),
                                      pltpu.SemaphoreType.DMA])
    def scatter(hp_hbm, pos_hbm, out_hbm, idx_v, rows_v, sem):
        wid = lax.axis_index('s') * V7X_SC_CORES + lax.axis_index('c')

        @pl.loop(0, per_worker // SC_ROWS)
        def _(i):
            t0 = pl.multiple_of(wid * per_worker + i * SC_ROWS, SC_ROWS)
            pltpu.sync_copy(hp_hbm.at[pl.ds(t0, SC_ROWS)], rows_v)
            for j in range(k):
                pltpu.sync_copy(pos_hbm.at[pl.ds(pl.multiple_of(j * t + t0, SC_ROWS), SC_ROWS)], idx_v)
                pltpu.async_copy(rows_v, out_hbm.at[idx_v], sem).wait()

    return scatter(hp, pos_flat)


def _sc_collect(yp, pos):
    _, w = yp.shape
    k, t = pos.shape
    workers = V7X_SC_CORES * V7X_SC_SUBCORES
    per_worker = k * t // workers
    pos_flat = pos.reshape(k * t)

    @functools.partial(pl.kernel, mesh=_sc_mesh(), out_type=jax.ShapeDtypeStruct((k * t, w), jnp.int32),
                       scratch_types=[pltpu.VMEM((SC_ROWS,), jnp.int32), pltpu.VMEM((SC_ROWS, w), jnp.int32),
                                      pltpu.SemaphoreType.DMA])
    def gather(yp_hbm, pos_hbm, out_hbm, idx_v, rows_v, sem):
        wid = lax.axis_index('s') * V7X_SC_CORES + lax.axis_index('c')

        @pl.loop(0, per_worker // SC_ROWS)
        def _(i):
            r0 = pl.multiple_of(wid * per_worker + i * SC_ROWS, SC_ROWS)
            pltpu.sync_copy(pos_hbm.at[pl.ds(r0, SC_ROWS)], idx_v)
            pltpu.async_copy(yp_hbm.at[idx_v], rows_v, sem).wait()
            pltpu.sync_copy(rows_v, out_hbm.at[pl.ds(r0, SC_ROWS)])

    return gather(yp, pos_flat)


def _expert_kernel(te_ref, nu_ref, x_ref, wg_ref, wu_ref, wd_ref, o_ref, wg_s, wu_s, wd_s):
    i = pl.program_id(0)

    @pl.when(i < nu_ref[0])
    def _():
        @pl.when((i == 0) | (te_ref[i] != te_ref[jnp.maximum(i - 1, 0)]))
        def _():
            wg_s[...] = wg_ref[0].astype(BF16)
            wu_s[...] = wu_ref[0].astype(BF16)
            wd_s[...] = wd_ref[0].astype(BF16)

        lo, hi = _unpack_halves(x_ref[...])
        half = lo.shape[1]
        a = (jnp.dot(lo, wg_s[:half, :], preferred_element_type=F32)
             + jnp.dot(hi, wg_s[half:, :], preferred_element_type=F32))
        u = (jnp.dot(lo, wu_s[:half, :], preferred_element_type=F32)
             + jnp.dot(hi, wu_s[half:, :], preferred_element_type=F32))
        y = jnp.dot((_silu(a) * u).astype(BF16), wd_s[...], preferred_element_type=F32)
        o_ref[...] = _pack_halves(y)


def _experts(xp, tile_expert, n_used, layer, w_gate, w_up, w_down):
    n_rows, half = xp.shape
    _, _, d, ff = w_gate.shape
    r = MOE_ROW_TILE
    row = pl.BlockSpec((r, half), lambda i, te, nu: (i, 0))
    grid_spec = pltpu.PrefetchScalarGridSpec(
        num_scalar_prefetch=2,
        grid=(n_rows // r,),
        in_specs=[row,
                  pl.BlockSpec((None, 1, d, ff), lambda i, te, nu: (layer, te[i], 0, 0)),
                  pl.BlockSpec((None, 1, d, ff), lambda i, te, nu: (layer, te[i], 0, 0)),
                  pl.BlockSpec((None, 1, ff, d), lambda i, te, nu: (layer, te[i], 0, 0))],
        out_specs=row,
        scratch_shapes=[pltpu.VMEM((d, ff), BF16), pltpu.VMEM((d, ff), BF16), pltpu.VMEM((ff, d), BF16)],
    )
    return pl.pallas_call(
        _expert_kernel,
        grid_spec=grid_spec,
        out_shape=jax.ShapeDtypeStruct((n_rows, half), jnp.int32),
        compiler_params=_params('arbitrary'),
        name='experts',
    )(tile_expert, n_used, xp, w_gate, w_up, w_down)


def _combine_kernel(yg_ref, w_ref, hp_ref, x_ref, g2_ref, sg_ref, su_ref, sd_ref, *rest, final):
    if final:
        fg_ref, o_ref = rest
    else:
        (o_ref,) = rest
    half = hp_ref.shape[2]
    lo, hi = _unpack_halves(hp_ref[0])
    sg, su = sg_ref[...], su_ref[...]
    a = jnp.dot(lo, sg[:half], preferred_element_type=F32) + jnp.dot(hi, sg[half:], preferred_element_type=F32)
    u = jnp.dot(lo, su[:half], preferred_element_type=F32) + jnp.dot(hi, su[half:], preferred_element_type=F32)
    acc = jnp.dot((_silu(a) * u).astype(BF16), sd_ref[...], preferred_element_type=F32)
    acc_lo, acc_hi = acc[:, :half], acc[:, half:]
    wts = w_ref[0]
    for k in range(yg_ref.shape[0]):
        ylo, yhi = _unpack_halves(yg_ref[k, 0])
        wk = wts[:, k:k + 1]
        acc_lo = acc_lo + wk * ylo.astype(F32)
        acc_hi = acc_hi + wk * yhi.astype(F32)
    y = x_ref[0] + g2_ref[0] * jnp.concatenate([acc_lo, acc_hi], axis=1)
    if final:
        y = _rms(y) * fg_ref[...]
    o_ref[0] = y


def _combine(yg, wts, hp, x, g2, s_gate, s_up, s_down, final_g=None, tm=256):
    b, s, d = x.shape
    k = yg.shape[0]
    half = d // 2
    row = pl.BlockSpec((1, tm, d), lambda i, j: (i, j, 0))
    prow = pl.BlockSpec((1, tm, half), lambda i, j: (i, j, 0))
    ins = [yg, wts, hp, x, g2, s_gate, s_up, s_down]
    in_specs = [pl.BlockSpec((k, 1, tm, half), lambda i, j: (0, i, j, 0)), pl.BlockSpec((1, tm, k), lambda i, j: (i, j, 0)),
                prow, row, pl.BlockSpec((1, 1, d), lambda i, j: (i, 0, 0)),
                _const_spec(s_gate.shape), _const_spec(s_up.shape), _const_spec(s_down.shape)]
    if final_g is not None:
        ins.append(final_g.reshape(1, d))
        in_specs.append(_const_spec((1, d)))
    return pl.pallas_call(
        functools.partial(_combine_kernel, final=final_g is not None),
        grid=(b, s // tm),
        in_specs=in_specs,
        out_specs=row,
        out_shape=jax.ShapeDtypeStruct(x.shape, F32),
        compiler_params=_params('parallel', 'parallel'),
        name='moe_combine',
    )(*ins)


def _routed_moe(x, g, scale, shift, g2, w_router, e_bias, layer, w_gate, w_up, w_down, s_gate, s_up, s_down, final_g=None):
    b, s, d = x.shape
    t = b * s
    hp, eid, rank, wts, counts = _router(x, g, scale, shift, w_router, e_bias, compact=True)
    counts = counts[:, 0].astype(jnp.int32)
    r = MOE_ROW_TILE
    padded = (counts + (r - 1)) // r * r
    ends = jnp.cumsum(padded)
    offsets = ends - padded
    n_rows = t * TOP_K + N_EXPERTS * r
    tile_start = jnp.arange(n_rows // r, dtype=jnp.int32) * r
    tile_expert = jnp.minimum(jnp.sum((tile_start[:, None] >= ends[None, :]).astype(jnp.int32), axis=1), N_EXPERTS - 1)
    n_used = (ends[-1] // r).reshape(1).astype(jnp.int32)
    pos = _route_pos(offsets.astype(jnp.int32), eid, rank)
    xp = _sc_dispatch(hp.reshape(t, d // 2), pos, n_rows)
    yp = _experts(xp, tile_expert.astype(jnp.int32), n_used, layer, w_gate, w_up, w_down)
    yg = _sc_collect(yp, pos).reshape(TOP_K, b, s, d // 2)
    return _combine(yg, wts.T.reshape(b, s, TOP_K), hp, x, g2, s_gate, s_up, s_down, final_g)


def _mixers(p, pc, ctx_out, prm, l, lam_init, rope_tabs, lb_terms):
    s = p['hy_v'].shape[1]
    sc = pc['hy_v'].shape[1]

    hy_args = (prm['hy_w1'][l], prm['hy_b1'][l], prm['hy_w2'][l], prm['hy_b2'][l], prm['hy_w3'][l], prm['hy_b3'][l],
               prm['hy_sin_freq'][l], prm['hy_decay'][l])
    y_hy = _hyena([p['hy_v'], p['hy_x1'], p['hy_x2']], prm['hy_conv_w'][l], prm['hy_conv_b'][l],
                  _hy_filters(s, *hy_args), prm['hy_bias'][l], inner=128)
    yc_hy = None
    if ctx_out:
        yc_hy = _hyena([pc['hy_v'], pc['hy_x1'], pc['hy_x2']], prm['hy_conv_w'][l], prm['hy_conv_b'][l],
                       _hy_filters(sc, *hy_args), prm['hy_bias'][l], inner=32)

    lp = prm['da_lambda'][l].astype(F32)
    lam = jnp.exp(jnp.sum(lp[0] * lp[1])) - jnp.exp(jnp.sum(lp[2] * lp[3])) + lam_init
    da_kw = dict(heads=DA_HEADS, ncomp=2, scale=DA_HEAD_DIM ** -0.5, lam=lam, subln_g=prm['da_subln_g'][l],
                 post_scale=1.0 - lam_init)
    da_ctx = ([pc['da_k']], pc['da_v'])
    y_da = _attention([p['da_q']], [da_ctx, ([p['da_k']], p['da_v'])], **da_kw)
    yc_da = _attention([pc['da_q']], [da_ctx], **da_kw) if ctx_out else None

    wq = prm['mla_w_q_up'][l].reshape(MLA_Q_RANK, MLA_HEADS, MLA_NOPE_DIM + MLA_ROPE_DIM)
    wq_n = wq[:, :, :MLA_NOPE_DIM].reshape(MLA_Q_RANK, -1).astype(BF16)
    wq_r = wq[:, :, MLA_NOPE_DIM:].reshape(MLA_Q_RANK, -1).astype(BF16)
    wkv = prm['mla_w_kv_up'][l].reshape(MLA_KV_RANK, MLA_HEADS, MLA_NOPE_DIM + MLA_V_DIM)
    wkv_n = wkv[:, :, :MLA_NOPE_DIM].reshape(MLA_KV_RANK, -1).astype(BF16)
    wkv_v = wkv[:, :, MLA_NOPE_DIM:].reshape(MLA_KV_RANK, -1).astype(BF16)

    def queries(qd, tabs):
        return _norm_proj(qd, prm['mla_q_norm_g'][l], [(wq_n, F32, False, MLA_HEADS), (wq_r, F32, True, MLA_HEADS)],
                          rope_tabs=tabs)

    def keys_values(kvd):
        return _norm_proj(kvd, prm['mla_kv_norm_g'][l], [(wkv_n, BF16, False, MLA_HEADS), (wkv_v, BF16, False, MLA_HEADS)])

    kn_l, v_l = keys_values(p['mla_kv'])
    kn_c, v_c = keys_values(pc['mla_kv'])
    mla_kw = dict(heads=MLA_HEADS, ncomp=1, scale=(MLA_NOPE_DIM + MLA_ROPE_DIM) ** -0.5)
    mla_ctx = ([kn_c, pc['mla_kr']], v_c)
    y_mla = _attention(queries(p['mla_q'], rope_tabs), [mla_ctx, ([kn_l, p['mla_kr']], v_l)], **mla_kw)
    yc_mla = _attention(queries(pc['mla_q'], None), [mla_ctx], **mla_kw) if ctx_out else None

    o, oc = _hgrn(p['hg_q'], p['hg_ff'], p['hg_fb'], p['hg_i'], pc['hg_q'], pc['hg_ff'], pc['hg_fb'], pc['hg_i'],
                  lb_terms, prm['hg_norm_g'][l])
    return (y_hy, y_da, y_mla, o), (yc_hy, yc_da, yc_mla, oc)


def kernel(x, c, ctx, c_ctx, w_ada, b_ada, norm1_g, norm2_g, w_in, w_out, hy_conv_w, hy_conv_b, hy_w1, hy_b1, hy_w2, hy_b2, hy_w3, hy_b3, hy_sin_freq, hy_decay, hy_bias, da_lambda, da_subln_g, mla_q_norm_g, mla_w_q_up, mla_kv_norm_g, mla_w_kv_up, hg_lower_bounds, hg_norm_g, moe_w_router, moe_bias, moe_w_gate, moe_w_up, moe_w_down, moe_sh_gate, moe_sh_up, moe_sh_down, final_norm_g):
    prm = dict(hy_conv_w=hy_conv_w, hy_conv_b=hy_conv_b, hy_w1=hy_w1, hy_b1=hy_b1, hy_w2=hy_w2, hy_b2=hy_b2,
               hy_w3=hy_w3, hy_b3=hy_b3, hy_sin_freq=hy_sin_freq, hy_decay=hy_decay, hy_bias=hy_bias,
               da_lambda=da_lambda, da_subln_g=da_subln_g, mla_q_norm_g=mla_q_norm_g, mla_w_q_up=mla_w_q_up,
               mla_kv_norm_g=mla_kv_norm_g, mla_w_kv_up=mla_w_kv_up, hg_norm_g=hg_norm_g)
    b, n_lat, d = x.shape
    depth = w_in.shape[0]
    rows = n_lat // GRID_W
    row_pos = jnp.repeat(jnp.arange(rows, dtype=jnp.int32), GRID_W)
    col_pos = jnp.tile(jnp.arange(GRID_W, dtype=jnp.int32), rows)
    rope_tabs = _rope_tables(row_pos, col_pos, 2 * DA_HEADS * DA_HEAD_DIM)
    lbs = jnp.cumsum(jax.nn.softmax(hg_lower_bounds.astype(F32), axis=1), axis=1)
    lbs = lbs - lbs[:, :1]
    cond = jnp.concatenate([c, c_ctx[None], jnp.zeros((8 - b - 1, d), F32)], axis=0)

    for l in range(depth):
        ctx_out = l < depth - 1
        mods = _ada(cond, w_ada[l], b_ada[l])
        sh1, sc1, g1, sh2, sc2, g2 = [m[:, None, :] for m in jnp.split(mods[:b], 6, axis=-1)]
        mc = [jnp.broadcast_to(m[:, None, :], (b, 1, d)) for m in jnp.split(mods[b:b + 1], 6, axis=-1)]

        off = 0
        outs = []
        for _, wdt, dt, rope, split, rep in _SEGMENTS:
            w = w_in[l][:, off:off + wdt].astype(BF16)
            outs.append((jnp.tile(w, (1, rep)) if rep > 1 else w, dt, rope, split))
            off += wdt
        names = [seg[0] for seg in _SEGMENTS]
        p = dict(zip(names, _norm_proj(x, norm1_g[l], outs, sc1, sh1, rope_tabs=rope_tabs)))
        pc = dict(zip(names, _norm_proj(ctx, norm1_g[l], outs, mc[1], mc[0])))

        lb = lbs[:, l]
        lb_terms = jnp.stack([jnp.log(lb), jnp.log1p(-lb), 1.0 - lb], axis=1)
        lam_init = 0.8 - 0.6 * math.exp(-0.3 * l)
        lat_parts, ctx_parts = _mixers(p, pc, ctx_out, prm, l, lam_init, rope_tabs, lb_terms)

        w_out_b = w_out[l].astype(BF16)
        moe_w = (l, moe_w_gate, moe_w_up, moe_w_down,
                 moe_sh_gate[l].astype(BF16), moe_sh_up[l].astype(BF16), moe_sh_down[l].astype(BF16))

        if ctx_out:
            ctx = _out_proj(*ctx_parts, pc['hg_g'], ctx, mc[2], w_out_b)
            flat = ctx.reshape(1, -1, d)
            h2c, gate_c = _router(flat, norm2_g[l], mc[4][:1], mc[3][:1], moe_w_router[l], moe_bias[l])
            ctx = _moe(h2c, flat, gate_c.transpose(0, 2, 1), mc[5][:1], *moe_w).reshape(ctx.shape)

        x = _out_proj(*lat_parts, p['hg_g'], x, g1, w_out_b)
        x = _routed_moe(x, norm2_g[l], sc2, sh2, g2, moe_w_router[l], moe_bias[l], *moe_w,
                        final_g=None if ctx_out else final_norm_g)

    return x
```

```python
import functools
import math

import numpy as np
import jax
import jax.numpy as jnp
from jax import lax
from jax.experimental import pallas as pl
from jax.experimental.pallas import tpu as pltpu
from jax.experimental.pallas import tpu_sc as plsc

F32 = jnp.float32
BF16 = jnp.bfloat16
HIGHEST = lax.Precision.HIGHEST

D_MODEL = 1024
GRID_W = 64
HY_WIDTH = 256
HY_ORDER = 2
HY_BANDS = 16
DA_HEADS = 4
DA_HEAD_DIM = 32
MLA_HEADS = 4
MLA_Q_RANK = 192
MLA_KV_RANK = 128
MLA_NOPE_DIM = 64
MLA_ROPE_DIM = 32
MLA_V_DIM = 64
HG_HEADS = 4
HG_KEY_DIM = 64
HG_VAL_DIM = 64
HG_CHUNK = 64
HG_SUB = 8
HG_UNROLL = 4
N_EXPERTS = 64
N_EXPERT_GROUPS = 8
TOPK_GROUPS = 4
TOP_K = 8
EXPERT_FF = 256
ROUTED_SCALE = 2.5
ROPE_BASE = 10000.0
NORM_EPS = 1e-6

V7X_VMEM_LIMIT_BYTES = 56 * 1024 * 1024
LANES = 128

_SEGMENTS = (
    ('hy_v', HY_WIDTH, F32, False, 0, 1), ('hy_x1', HY_WIDTH, F32, False, 0, 1), ('hy_x2', HY_WIDTH, F32, False, 0, 1),
    ('da_q', 2 * DA_HEADS * DA_HEAD_DIM, F32, True, 2 * DA_HEADS, 1),
    ('da_k', 2 * DA_HEADS * DA_HEAD_DIM, BF16, True, 2 * DA_HEADS, 1),
    ('da_v', 2 * DA_HEADS * DA_HEAD_DIM, BF16, False, DA_HEADS, 1),
    ('mla_q', MLA_Q_RANK, F32, False, 0, 1), ('mla_kv', MLA_KV_RANK, F32, False, 0, 1),
    ('mla_kr', MLA_ROPE_DIM, BF16, True, MLA_HEADS, MLA_HEADS),
    ('hg_q', HG_HEADS * HG_KEY_DIM, F32, False, 0, 1), ('hg_ff', HG_HEADS * HG_KEY_DIM, F32, False, 0, 1),
    ('hg_fb', HG_HEADS * HG_KEY_DIM, F32, False, 0, 1), ('hg_i', HG_HEADS * HG_VAL_DIM, F32, False, 0, 1),
    ('hg_g', HG_HEADS * HG_VAL_DIM, F32, False, 0, 1),
)


def _params(*semantics):
    return pltpu.CompilerParams(dimension_semantics=semantics, vmem_limit_bytes=V7X_VMEM_LIMIT_BYTES)


def _const_spec(shape):
    nd = len(shape)
    return pl.BlockSpec(shape, lambda *_: (0,) * nd)


def _rms(x, eps=NORM_EPS):
    return x * lax.rsqrt(jnp.mean(x * x, axis=-1, keepdims=True) + eps)


def _silu(x):
    return x * jax.nn.sigmoid(x)


def _dot_nt(a, b, **kw):
    return lax.dot_general(a, b, (((1,), (1,)), ((), ())), preferred_element_type=F32, **kw)


def _ada_kernel(c_ref, w_ref, b_ref, o_ref):
    s = _silu(c_ref[...])
    o_ref[...] = jnp.dot(s, w_ref[...], precision=HIGHEST, preferred_element_type=F32) + b_ref[...]


def _ada(cond, w, b):
    r, d = cond.shape
    n = w.shape[1]
    tn = 1536
    return pl.pallas_call(
        _ada_kernel,
        grid=(n // tn,),
        in_specs=[_const_spec((r, d)), pl.BlockSpec((d, tn), lambda j: (0, j)), pl.BlockSpec((1, tn), lambda j: (0, j))],
        out_specs=pl.BlockSpec((r, tn), lambda j: (0, j)),
        out_shape=jax.ShapeDtypeStruct((r, n), F32),
        compiler_params=_params('arbitrary'),
        name='ada',
    )(cond, w, b.reshape(1, n))


ROPE_UNIT = 32


def _rope_tables(row, col, width):
    n = ROPE_UNIT // 4
    inv = ROPE_BASE ** (-jnp.arange(n, dtype=F32) / n)
    units = width // ROPE_UNIT
    parts_c, parts_a, parts_b = [], [], []
    zero = jnp.zeros((row.shape[0], n), F32)
    for pos in (row, col):
        ang = pos.astype(F32)[:, None] * inv
        cos, sin = jnp.cos(ang), jnp.sin(ang)
        parts_c += [cos, cos]
        parts_a += [zero, sin]
        parts_b += [-sin, zero]
    tile = lambda ps: jnp.tile(jnp.concatenate(ps, axis=1), (1, units))
    return tile(parts_c), tile(parts_a), tile(parts_b)


def _norm_proj_kernel(*refs, n_w, modulate, ropes, splits):
    x_ref, g_ref = refs[0], refs[1]
    pos = 2
    if modulate:
        sc_ref, sh_ref = refs[2], refs[3]
        pos = 4
    if any(ropes):
        rc_ref, ra_ref, rb_ref = refs[pos:pos + 3]
        pos += 3
    w_refs = refs[pos:pos + n_w]
    o_refs = refs[pos + n_w:]
    y = _rms(x_ref[0]) * g_ref[...]
    if modulate:
        y = y * (1.0 + sc_ref[0]) + sh_ref[0]
    yb = y.astype(BF16)
    for w_ref, o_ref, rope, split in zip(w_refs, o_refs, ropes, splits):
        o = jnp.dot(yb, w_ref[...], preferred_element_type=F32)
        if rope:
            wd = o.shape[1]
            shift = ROPE_UNIT // 4
            o = (o * rc_ref[:, :wd] + pltpu.roll(o, shift, axis=1) * ra_ref[:, :wd]
                 + pltpu.roll(o, wd - shift, axis=1) * rb_ref[:, :wd])
        if split:
            unit = o.shape[1] // split
            for u in range(split):
                o_ref[0, u] = o[:, u * unit:(u + 1) * unit].astype(o_ref.dtype)
        else:
            o_ref[0] = o.astype(o_ref.dtype)


def _norm_proj(x, g, outs, scale=None, shift=None, rope_tabs=None, tm=512):
    b, s, k = x.shape
    tm = min(tm, s)
    modulate = scale is not None
    ropes = tuple(bool(o[2]) and rope_tabs is not None for o in outs)
    splits = tuple(o[3] for o in outs)
    ins = [x, g.reshape(1, k)]
    in_specs = [pl.BlockSpec((1, tm, k), lambda i, j: (i, j, 0)), _const_spec((1, k))]
    if modulate:
        ins += [scale, shift]
        in_specs += [pl.BlockSpec((1, 1, k), lambda i, j: (i, 0, 0))] * 2
    if any(ropes):
        ins += list(rope_tabs)
        in_specs += [pl.BlockSpec((tm, rope_tabs[0].shape[1]), lambda i, j: (j, 0))] * 3
    out_specs, out_shape = [], []
    for w, dt, _, split in outs:
        ins.append(w)
        in_specs.append(_const_spec(w.shape))
        n = w.shape[1]
        if split:
            out_specs.append(pl.BlockSpec((1, split, tm, n // split), lambda i, j: (i, 0, j, 0)))
            out_shape.append(jax.ShapeDtypeStruct((b, split, s, n // split), dt))
        else:
            out_specs.append(pl.BlockSpec((1, tm, n), lambda i, j: (i, j, 0)))
            out_shape.append(jax.ShapeDtypeStruct((b, s, n), dt))
    return pl.pallas_call(
        functools.partial(_norm_proj_kernel, n_w=len(outs), modulate=modulate, ropes=ropes, splits=splits),
        grid=(b, s // tm),
        in_specs=in_specs,
        out_specs=out_specs,
        out_shape=out_shape,
        compiler_params=_params('parallel', 'parallel'),
        name='norm_proj',
    )(*ins)


def _hy_filter_kernel(w1t_ref, w1s_ref, w1c_ref, b1_ref, w2_ref, b2_ref, w3_ref, b3_ref, fr_ref, dec_ref, o_ref, *, n):
    t = lax.broadcasted_iota(jnp.int32, (n, 1), 0).astype(F32) / n
    bands = lax.broadcasted_iota(jnp.int32, (1, HY_BANDS), 1).astype(F32) + 1.0
    ang = (2.0 * jnp.pi) * t * bands
    pre = (t * w1t_ref[...]
           + jnp.dot(jnp.sin(ang), w1s_ref[...], precision=HIGHEST, preferred_element_type=F32)
           + jnp.dot(jnp.cos(ang), w1c_ref[...], precision=HIGHEST, preferred_element_type=F32)
           + b1_ref[...])
    hid = jnp.sin(fr_ref[0:1, :] * pre)
    hid = jnp.sin(fr_ref[1:2, :] * (jnp.dot(hid, w2_ref[...], precision=HIGHEST, preferred_element_type=F32) + b2_ref[...]))
    filt = jnp.dot(hid, w3_ref[...], precision=HIGHEST, preferred_element_type=F32) + b3_ref[...]
    filt = filt * jnp.exp(-t * jnp.abs(dec_ref[...]))
    col = jnp.sum(jnp.abs(filt), axis=0, keepdims=True) - jnp.abs(filt[0:1, :])
    w = HY_WIDTH
    for o in range(HY_ORDER):
        lo = o * 2 * w
        f0 = filt[0:1, lo:lo + w] + filt[0:1, lo + w:lo + 2 * w]
        inv = 1.0 / (col[:, lo:lo + w] + col[:, lo + w:lo + 2 * w] + jnp.abs(f0))
        o_ref[:, lo:lo + w] = filt[:, lo:lo + w] * inv
        o_ref[:, lo + w:lo + 2 * w] = filt[:, lo + w:lo + 2 * w] * inv


def _hy_filters(n, w1, b1, w2, b2, w3, b3, freq, decay):
    cols = w3.shape[1]
    ins = [w1[0:1], w1[1:1 + HY_BANDS], w1[1 + HY_BANDS:], b1.reshape(1, -1), w2, b2.reshape(1, -1), w3,
           b3.reshape(1, -1), freq, decay.reshape(1, -1)]
    out = pl.pallas_call(
        functools.partial(_hy_filter_kernel, n=n),
        grid=(1,),
        in_specs=[_const_spec(a.shape) for a in ins],
        out_specs=_const_spec((n, cols)),
        out_shape=jax.ShapeDtypeStruct((n, cols), F32),
        compiler_params=_params('arbitrary'),
        name='hy_filter',
    )(*ins)
    return out.reshape(n, HY_ORDER, 2, HY_WIDTH)


def _two_sided(filt_n):
    n = filt_n.shape[0]
    hf, hb = filt_n[:, :, 0], filt_n[:, :, 1]
    h = jnp.concatenate([hf[:1] + hb[:1], hf[1:], jnp.zeros((1,) + hf.shape[1:], F32), hb[:0:-1]], axis=0)
    return h.reshape(2 * n, HY_ORDER * HY_WIDTH)


def _short_conv_kernel(*refs, s):
    x_refs, w_refs, b_refs, o_refs = refs[0:3], refs[3:6], refs[6:9], refs[9:12]
    row = lax.broadcasted_iota(jnp.int32, (s, 1), 0)
    for x_ref, w_ref, b_ref, o_ref in zip(x_refs, w_refs, b_refs, o_refs):
        x = x_ref[0]
        prev = jnp.where(row == 0, 0.0, pltpu.roll(x, 1, axis=0))
        nxt = jnp.where(row == s - 1, 0.0, pltpu.roll(x, s - 1, axis=0))
        o_ref[0] = prev * w_ref[0:1, :] + x * w_ref[1:2, :] + nxt * w_ref[2:3, :] + b_ref[...]


def _short_conv(parts, conv_w, conv_b):
    b, s, c = parts[0].shape
    tc = LANES
    ws = [conv_w[:, i * c:(i + 1) * c] for i in range(3)]
    bs = [conv_b[i * c:(i + 1) * c].reshape(1, c) for i in range(3)]
    xspec = pl.BlockSpec((1, s, tc), lambda i, j: (i, 0, j))
    return pl.pallas_call(
        functools.partial(_short_conv_kernel, s=s),
        grid=(b, c // tc),
        in_specs=[xspec] * 3 + [pl.BlockSpec((3, tc), lambda i, j: (0, j))] * 3 + [pl.BlockSpec((1, tc), lambda i, j: (0, j))] * 3,
        out_specs=[xspec] * 3,
        out_shape=[jax.ShapeDtypeStruct((b, s, c), F32)] * 3,
        compiler_params=_params('parallel', 'parallel'),
        name='short_conv',
    )(*parts, *ws, *bs)


def _dft_cos_sin(rows, cols, period):
    ang = 2.0 * np.pi * ((np.arange(rows)[:, None] * np.arange(cols)[None, :]) % period) / period
    return np.cos(ang), np.sin(ang)


def _fft_tables(n, inner):
    big = 2 * n
    n1 = big // inner
    c1, s1 = _dft_cos_sin(n1, n1, n1)
    h = n1 // 2
    outer_data = np.block([[c1[:, :h], s1[:, :h]], [-s1[:, :h], c1[:, :h]]])
    outer_real = np.concatenate([c1, -s1], axis=0)
    outer_inv = np.block([[c1[:h, :], -s1[:h, :]], [s1[:h, :], c1[:h, :]]]) / big
    c2, s2 = _dft_cos_sin(inner, inner, inner)
    inner_fwd = np.block([[c2, s2], [-s2, c2]])
    inner_inv = np.block([[c2, -s2], [s2, c2]])
    ct, st = _dft_cos_sin(n1, inner, big)
    f = lambda a: jnp.asarray(a, F32)
    return dict(n1=n1, inner=inner, outer_data=_hi_lo_cols(outer_data), outer_real=_hi_lo_cols(outer_real),
                outer_inv=_hi_lo_cols(outer_inv),
                inner_fwd=_hi_lo_cols(inner_fwd), inner_inv=_hi_lo_cols(inner_inv),
                tw_cos=f(ct).reshape(n1, inner, 1), tw_sin=f(st).reshape(n1, inner, 1))


def _left_mm_kernel(m_ref, x_ref, o_ref):
    o_ref[0] = jnp.dot(m_ref[...], _hi_lo_rows(x_ref[0]), preferred_element_type=F32)


def _left_mm(m, x, tl=4096):
    p, k, l = x.shape
    mm = m.shape[0]
    tl = min(tl, l)
    return pl.pallas_call(
        _left_mm_kernel,
        grid=(p, l // tl),
        in_specs=[_const_spec(m.shape), pl.BlockSpec((1, k, tl), lambda i, j: (i, 0, j))],
        out_specs=pl.BlockSpec((1, mm, tl), lambda i, j: (i, 0, j)),
        out_shape=jax.ShapeDtypeStruct((p, mm, l), F32),
        compiler_params=_params('parallel', 'parallel'),
        name='fft_outer',
    )(m, x)


def _hi_lo_cols(m):
    m = np.asarray(m, np.float32)
    hi = m.astype(BF16)
    lo = (m - hi.astype(np.float32)).astype(BF16)
    return jnp.asarray(np.concatenate([hi, hi, lo], axis=1))


def _hi_lo_rows(x):
    hi = x.astype(BF16)
    lo = (x - hi.astype(F32)).astype(BF16)
    return jnp.concatenate([hi, lo, hi], axis=0)


def _inner_kernel(a_ref, twc_ref, tws_ref, gf_ref, *rest, convolve, inner, kb):
    for s in range(kb):
        ar, ai = a_ref[0, 0, s], a_ref[0, 1, s]
        tc, ts = twc_ref[s], tws_ref[s]
        br = ar * tc + ai * ts
        bi = ai * tc - ar * ts
        x = jnp.dot(gf_ref[...], _hi_lo_rows(jnp.concatenate([br, bi], axis=0)), preferred_element_type=F32)
        if not convolve:
            o_ref = rest[0]
            o_ref[0, 0, s] = x[:inner]
            o_ref[0, 1, s] = x[inner:]
            continue
        h_ref, gi_ref, o_ref = rest
        xr, xi = x[:inner], x[inner:]
        hr, hi = h_ref[0, 0, s], h_ref[0, 1, s]
        yr = xr * hr - xi * hi
        yi = xr * hi + xi * hr
        z = jnp.dot(gi_ref[...], _hi_lo_rows(jnp.concatenate([yr, yi], axis=0)), preferred_element_type=F32)
        zr, zi = z[:inner], z[inner:]
        o_ref[0, 0, s] = zr * tc - zi * ts
        o_ref[0, 1, s] = zi * tc + zr * ts


def _fft_inner(a, tab, c, h=None, h_block=0):
    p = a.shape[0]
    n1, inner = tab['n1'], tab['inner']
    a5 = a.reshape(p, 2, n1, inner, c)
    tc = 2 * LANES
    kb = 4
    blk = pl.BlockSpec((1, 2, kb, inner, tc), lambda k, j, i: (i, 0, k, 0, j))
    tw_spec = pl.BlockSpec((kb, inner, 1), lambda k, j, i: (k, 0, 0))
    ins = [a5, tab['tw_cos'], tab['tw_sin'], tab['inner_fwd']]
    in_specs = [blk, tw_spec, tw_spec, _const_spec(tab['inner_fwd'].shape)]
    if h is not None:
        ch = h.shape[-1] // inner
        nb = c // tc
        ins += [h.reshape(1, 2, n1, inner, ch), tab['inner_inv']]
        in_specs += [pl.BlockSpec((1, 2, kb, inner, tc), lambda k, j, i: (0, 0, k, 0, h_block * nb + j)),
                     _const_spec(tab['inner_inv'].shape)]
    out = pl.pallas_call(
        functools.partial(_inner_kernel, convolve=h is not None, inner=inner, kb=kb),
        grid=(n1 // kb, c // tc, p),
        in_specs=in_specs,
        out_specs=blk,
        out_shape=jax.ShapeDtypeStruct(a5.shape, F32),
        compiler_params=_params('parallel', 'parallel', 'parallel'),
        name='fft_inner',
    )(*ins)
    return out.reshape(p, 2 * n1, inner * c)


def _gate_kernel(m_ref, z_ref, u_ref, x_ref, bias_ref, *rest, chain):
    y = jnp.dot(m_ref[...], _hi_lo_rows(z_ref[0]), preferred_element_type=F32)
    nxt = x_ref[0] * (y + u_ref[0] * bias_ref[...])
    if chain:
        mf_ref, o_ref, a_ref = rest
        o_ref[0] = nxt
        a_ref[0] = jnp.dot(mf_ref[...], _hi_lo_rows(nxt), preferred_element_type=F32)
    else:
        rest[0][0] = nxt


def _fft_gate(tab, z, u, x, bias_l, chain, tl=4096):
    p, k2, l = z.shape
    n1 = tab['n1']
    tl = min(tl, l)
    row = pl.BlockSpec((1, n1, tl), lambda i, j: (i, 0, j))
    ins = [tab['outer_inv'], z, u, x, bias_l]
    in_specs = [_const_spec(tab['outer_inv'].shape), pl.BlockSpec((1, k2, tl), lambda i, j: (i, 0, j)), row, row,
                pl.BlockSpec((1, tl), lambda i, j: (0, j))]
    out_specs = [row]
    out_shape = [jax.ShapeDtypeStruct((p, n1, l), F32)]
    if chain:
        ins.append(tab['outer_data'])
        in_specs.append(_const_spec(tab['outer_data'].shape))
        out_specs.append(pl.BlockSpec((1, k2, tl), lambda i, j: (i, 0, j)))
        out_shape.append(jax.ShapeDtypeStruct((p, k2, l), F32))
    return pl.pallas_call(
        functools.partial(_gate_kernel, chain=chain),
        grid=(p, l // tl),
        in_specs=in_specs,
        out_specs=out_specs,
        out_shape=out_shape,
        compiler_params=_params('parallel', 'parallel'),
        name='fft_gate',
    )(*ins)


def _hyena(parts, conv_w, conv_b, filt_n, bias, inner):
    b, s, c = parts[0].shape
    tab = _fft_tables(s, inner)
    n1 = tab['n1']
    lanes = inner * c
    h_taps = _two_sided(filt_n).reshape(1, n1, inner * HY_ORDER * c)
    h_spec = _fft_inner(_left_mm(tab['outer_real'], h_taps), tab, HY_ORDER * c)
    v, x1, x2 = [a.reshape(b // 2, n1, lanes) for a in _short_conv(parts, conv_w, conv_b)]
    bias_l = [jnp.tile(bias[o], inner).reshape(1, lanes) for o in range(HY_ORDER)]
    a = _left_mm(tab['outer_data'], v)
    z = _fft_inner(a, tab, c, h_spec, 0)
    z2, a = _fft_gate(tab, z, v, x1, bias_l[0], chain=True)
    z = _fft_inner(a, tab, c, h_spec, 1)
    (z3,) = _fft_gate(tab, z, z2, x2, bias_l[1], chain=False)
    return z3.reshape(b, s, c)


def _attn_kernel(*refs, n_q, n_pieces, ncomp, scale, post_scale):
    q_refs = refs[:n_q]
    pos = n_q
    pieces = []
    for _ in range(n_pieces):
        pieces.append((refs[pos:pos + n_q], refs[pos + n_q]))
        pos += n_q + 1
    if ncomp == 2:
        lam_ref, g_ref = refs[pos:pos + 2]
        pos += 2
    o_ref, kcat_ref, vcat_ref = refs[pos:pos + 3]
    dv = o_ref.shape[3]

    @pl.when(pl.program_id(2) == 0)
    def _():
        row = 0
        for k_refs, v_ref in pieces:
            n = v_ref.shape[2]
            for c in range(ncomp):
                parts = [k_ref[0, c if k_ref.shape[1] == ncomp else 0] for k_ref in k_refs]
                kcat_ref[c, row:row + n, :] = parts[0] if n_q == 1 else jnp.concatenate(parts, axis=1)
            vcat_ref[row:row + n, :dv] = v_ref[0, 0]
            vcat_ref[row:row + n, dv:] = jnp.ones((n, dv), BF16)
            row += n

    outs = []
    for c in range(ncomp):
        q = q_refs[0][0, c] if n_q == 1 else jnp.concatenate([q_ref[0, c] for q_ref in q_refs], axis=1)
        s = _dot_nt((q * (scale * math.log2(math.e))).astype(BF16), kcat_ref[c])
        m = jnp.max(s, axis=-1, keepdims=True)
        p = jnp.exp2((s - m).astype(BF16))
        ol = jnp.dot(p, vcat_ref[...], preferred_element_type=F32)
        outs.append(ol[:, :dv] / ol[:, dv:dv + 1])
    if ncomp == 2:
        o = outs[0] - lam_ref[0] * outs[1]
        o = _rms(o) * g_ref[...] * post_scale
    else:
        o = outs[0]
    o_ref[0, 0] = o


def _attention(q_parts, pieces, heads, ncomp, scale, tq=256, lam=None, subln_g=None, post_scale=1.0):
    b, _, sq, _ = q_parts[0].shape
    dv = pieces[0][1].shape[3]
    tq = min(tq, sq)
    ins = list(q_parts)
    in_specs = [pl.BlockSpec((1, ncomp, tq, q.shape[3]), lambda i, h, j: (i, h, j, 0)) for q in q_parts]
    for k_parts, v in pieces:
        for k in k_parts:
            ins.append(k)
            if k.shape[1] == 1:
                in_specs.append(pl.BlockSpec((1, 1) + k.shape[2:], lambda i, h, j: (i, 0, 0, 0)))
            else:
                in_specs.append(pl.BlockSpec((1, ncomp) + k.shape[2:], lambda i, h, j: (i, h, 0, 0)))
        ins.append(v)
        in_specs.append(pl.BlockSpec((1, 1) + v.shape[2:], lambda i, h, j: (i, h, 0, 0)))
    if ncomp == 2:
        ins += [lam.reshape(1), subln_g.reshape(1, dv)]
        in_specs += [pl.BlockSpec(memory_space=pltpu.SMEM), _const_spec((1, dv))]
    sk = sum(v.shape[2] for _, v in pieces)
    dqk = sum(q.shape[3] for q in q_parts)
    return pl.pallas_call(
        functools.partial(_attn_kernel, n_q=len(q_parts), n_pieces=len(pieces), ncomp=ncomp, scale=scale,
                          post_scale=post_scale),
        grid=(b, heads, sq // tq),
        in_specs=in_specs,
        out_specs=pl.BlockSpec((1, 1, tq, dv), lambda i, h, j: (i, h, j, 0)),
        out_shape=jax.ShapeDtypeStruct((b, heads, sq, dv), F32),
        scratch_shapes=[pltpu.VMEM((ncomp, sk, dqk), BF16), pltpu.VMEM((sk, 2 * dv), BF16)],
        compiler_params=_params('parallel', 'parallel', 'arbitrary'),
        name='attention',
    )(*ins)


def _forget_terms(f, log_lb, log_1m_lb, one_m_lb):
    log_sig = jnp.minimum(f, 0.0) - jnp.log1p(jnp.exp(-jnp.abs(f)))
    b = log_1m_lb + log_sig
    log_g = jnp.maximum(log_lb, b) + jnp.log1p(jnp.exp(-jnp.abs(log_lb - b)))
    return log_g, one_m_lb * jax.nn.sigmoid(-f)


def _hg_tables():
    ck, sub = HG_CHUNK, HG_SUB
    t = np.arange(ck)
    cum_mats, half_masks, group_masks, keeps = [], [], [], []
    for rev in (False, True):
        cum_mats.append((t[None, :] >= t[:, None]) if rev else (t[None, :] <= t[:, None]))
        halves = []
        hs = ck // 2
        while hs >= sub:
            pos = t % (2 * hs)
            q_half = (pos < hs) if rev else (pos >= hs)
            halves.append(np.stack([q_half, ~q_half]))
            if not rev:
                grp = (t[:, None] // (2 * hs)) == (t[None, :] // (2 * hs))
                group_masks.append(np.concatenate([grp, grp], axis=0))
            hs //= 2
        half_masks.append(np.stack(halves))
        c = np.arange(ck * sub)
        tt, ss = (c // sub) % sub, c % sub
        keeps.append((ss >= tt) if rev else (ss <= tt))
    lanes = 2 * HG_KEY_DIM
    ln = np.arange(lanes)
    bd = (ln[:, None] // HG_KEY_DIM) == (ln[None, :] // HG_KEY_DIM)
    hm = np.broadcast_to(np.stack(half_masks)[..., None], (2, len(half_masks[0]), 2, ck, lanes))
    return (jnp.asarray(np.stack(cum_mats), BF16), jnp.asarray(hm, F32), jnp.asarray(np.stack(group_masks), F32),
            jnp.asarray(bd, F32), jnp.asarray(np.broadcast_to(np.stack(keeps)[..., None], (2, ck * sub, lanes)), F32))


def _split2(x):
    a = x.astype(BF16)
    return a, (x - a.astype(F32)).astype(BF16)


def _hg_chunk(q, k, v, lg, st, rev, cm, hm, gm, keep, bd, m0, m1):
    ck, sub = HG_CHUNK, HG_SUB
    lanes = lg.shape[1]
    c2 = jnp.dot(cm, jnp.concatenate(_split2(lg), axis=1), preferred_element_type=F32)
    cum = c2[:, :lanes] + c2[:, lanes:]
    tot = cum[0:1] if rev else cum[ck - 1:ck]
    o = _dot_nt((q * jnp.exp(cum)).astype(BF16), st.astype(BF16))
    kd = (k * jnp.exp(tot - cum)).astype(BF16)
    st_new = st * jnp.exp(tot) + bd * jnp.dot(v.T.astype(BF16), kd, preferred_element_type=F32)
    s2 = None
    for lv in range(gm.shape[0]):
        hs = ck >> (lv + 1)
        edge = [(g + hs) if rev else (g + hs - 1) for g in range(0, ck, 2 * hs)]
        cb = jnp.concatenate([jnp.broadcast_to(cum[e:e + 1], (2 * hs, lanes)) for e in edge], axis=0)
        qd = q * jnp.exp(jnp.minimum(cum - cb, 0.0)) * hm[lv, 0]
        kf = (k * jnp.exp(jnp.minimum(cb - cum, 0.0)) * hm[lv, 1]).astype(BF16)
        q2 = jnp.concatenate([qd * m0, qd * m1], axis=0).astype(BF16)
        term = _dot_nt(q2, kf) * gm[lv]
        s2 = term if s2 is None else s2 + term
    r = jnp.dot(s2.astype(BF16), v.astype(BF16), preferred_element_type=F32)
    o = o + m0 * r[:ck] + m1 * r[ck:]
    rows, vts = [], []
    for i in range(ck // sub):
        lo, hi = i * sub, (i + 1) * sub
        ki, ci = k[lo:hi], cum[lo:hi]
        for t in range(lo, hi):
            rows.append(q[t:t + 1] * ki * jnp.exp(jnp.minimum(cum[t:t + 1] - ci, 0.0)))
            vts.append(v[lo:hi])
    sc = jnp.dot(jnp.concatenate(rows, axis=0).astype(BF16), bd.astype(BF16), preferred_element_type=F32)
    prod = sc * jnp.concatenate(vts, axis=0) * keep
    o = o + jnp.sum(prod.reshape(ck, sub, lanes), axis=1)
    return o, st_new


def _hgrn_kernel(q_ref, ff_ref, fb_ref, i_ref, qc_ref, ffc_ref, fbc_ref, ic_ref, lb_ref, g_ref,
                 cm_ref, hm_ref, gm_ref, bd_ref, keep_ref, o_ref, oc_ref, or_ref, ocr_ref, st_ref, *, n_lat, n_ctx):
    ck = HG_CHUNK
    lanes = o_ref.shape[-1]
    lane = lax.broadcasted_iota(jnp.int32, (1, lanes), 1)
    m0 = (lane < HG_KEY_DIM).astype(F32)
    m1 = 1.0 - m0
    bd = bd_ref[...]
    gm = gm_ref[...]

    def one(q, f, v, rev):
        d = 1 if rev else 0
        lg, k = _forget_terms(f, lb_ref[d, 0:1, :], lb_ref[d, 1:2, :], lb_ref[d, 2:3, :])
        o, st = _hg_chunk(q, k, v, lg, st_ref[d], rev, cm_ref[d], hm_ref[d], gm, keep_ref[d], bd, m0, m1)
        st_ref[d] = st
        return o

    def sweep(qr, ffr, fbr, ir, out_f, out_r, n):
        nc = n // ck

        def body(step, carry):
            idf = pl.ds(pl.multiple_of(step * ck, ck), ck)
            idr = pl.ds(pl.multiple_of((nc - 1 - step) * ck, ck), ck)
            out_f[0, idf, :] = one(qr[0, idf, :], ffr[0, idf, :], ir[0, idf, :], False)
            out_r[idr, :] = one(qr[0, idr, :], fbr[0, idr, :], ir[0, idr, :], True)
            return carry

        lax.fori_loop(0, nc, body, 0, unroll=HG_UNROLL)

    st_ref[...] = jnp.zeros(st_ref.shape, F32)
    sweep(qc_ref, ffc_ref, fbc_ref, ic_ref, oc_ref, ocr_ref, n_ctx)
    sweep(q_ref, ff_ref, fb_ref, i_ref, o_ref, or_ref, n_lat)

    mean_mat = bd * (1.0 / HG_VAL_DIM)

    def readout(out, out_r, n):
        tile = min(n, 512)

        def body(step, carry):
            idx = pl.ds(pl.multiple_of(step * tile, tile), tile)
            x = out[0, idx, :] + out_r[idx, :]
            ms = jnp.dot(x * x, mean_mat, precision=HIGHEST, preferred_element_type=F32)
            out[0, idx, :] = x * lax.rsqrt(ms + NORM_EPS) * g_ref[...]
            return carry

        lax.fori_loop(0, n // tile, body, 0)

    readout(oc_ref, ocr_ref, n_ctx)
    readout(o_ref, or_ref, n_lat)


def _hgrn(q, ff, fb, iv, qc, ffc, fbc, ic, lb_terms, norm_g):
    b, n_lat, width = q.shape
    n_ctx = qc.shape[1]
    lanes = 2 * HG_KEY_DIM
    tables = _hg_tables()
    lat = pl.BlockSpec((1, n_lat, lanes), lambda i, j: (i, 0, j))
    ctx = pl.BlockSpec((1, n_ctx, lanes), lambda i, j: (i, 0, j))
    g2 = jnp.tile(norm_g, 2).reshape(1, lanes)
    return pl.pallas_call(
        functools.partial(_hgrn_kernel, n_lat=n_lat, n_ctx=n_ctx),
        grid=(b, width // lanes),
        in_specs=[lat] * 4 + [ctx] * 4 + [pl.BlockSpec((2, 3, lanes), lambda i, j: (0, 0, j)), _const_spec((1, lanes))]
                 + [_const_spec(t.shape) for t in tables],
        out_specs=[lat, ctx],
        out_shape=[jax.ShapeDtypeStruct(q.shape, F32), jax.ShapeDtypeStruct(qc.shape, F32)],
        scratch_shapes=[pltpu.VMEM((n_lat, lanes), F32), pltpu.VMEM((n_ctx, lanes), F32), pltpu.VMEM((2, lanes, lanes), F32)],
        compiler_params=_params('parallel', 'parallel'),
        name='hgrn2',
    )(q, ff, fb, iv, qc, ffc, fbc, ic, lb_terms, g2, *tables)


def _out_proj_kernel(hy_ref, da_ref, mla_ref, hg_ref, gate_ref, x_ref, g1_ref, w_ref, o_ref):
    c = hy_ref.shape[2]
    acc = jnp.dot(hy_ref[0].astype(BF16), w_ref[0:c, :], preferred_element_type=F32)
    for i, head_ref in ((1, da_ref), (2, mla_ref)):
        dv = head_ref.shape[3]
        for h in range(head_ref.shape[1]):
            lo = i * c + h * dv
            acc = acc + jnp.dot(head_ref[0, h].astype(BF16), w_ref[lo:lo + dv, :], preferred_element_type=F32)
    hg = hg_ref[0] * _silu(gate_ref[0])
    acc = acc + jnp.dot(hg.astype(BF16), w_ref[3 * c:4 * c, :], preferred_element_type=F32)
    o_ref[0] = x_ref[0] + g1_ref[0] * acc


def _out_proj(y_hy, y_da, y_mla, y_hg, gate, x, g1, w_out, tm=512):
    b, s, d = x.shape
    tm = min(tm, s)
    c = y_hy.shape[2]
    part = pl.BlockSpec((1, tm, c), lambda i, j: (i, j, 0))
    headed = lambda a: pl.BlockSpec((1, a.shape[1], tm, a.shape[3]), lambda i, j: (i, 0, j, 0))
    row = pl.BlockSpec((1, tm, d), lambda i, j: (i, j, 0))
    return pl.pallas_call(
        _out_proj_kernel,
        grid=(b, s // tm),
        in_specs=[part, headed(y_da), headed(y_mla), part, part, row, pl.BlockSpec((1, 1, d), lambda i, j: (i, 0, 0)),
                  _const_spec(w_out.shape)],
        out_specs=row,
        out_shape=jax.ShapeDtypeStruct(x.shape, F32),
        compiler_params=_params('parallel', 'parallel'),
        name='out_proj',
    )(y_hy, y_da, y_mla, y_hg, gate, x, g1, w_out)


def _router_kernel(x_ref, g_ref, sc_ref, sh_ref, wrt_ref, bias_ref, *rest, compact):
    h = _rms(x_ref[0]) * g_ref[...] * (1.0 + sc_ref[0]) + sh_ref[0]
    tm = h.shape[0]
    scores = jax.nn.sigmoid(_dot_nt(wrt_ref[...], h, precision=HIGHEST))
    choice = scores + bias_ref[...]
    per = N_EXPERTS // N_EXPERT_GROUPS
    neg = -jnp.inf
    iota_g = lax.broadcasted_iota(jnp.int32, (per, tm), 0)
    grp_rows = []
    for gi in range(N_EXPERT_GROUPS):
        blk = choice[gi * per:(gi + 1) * per]
        m1 = jnp.max(blk, axis=0, keepdims=True)
        first = jnp.min(jnp.where(blk == m1, iota_g, per), axis=0, keepdims=True)
        m2 = jnp.max(jnp.where(iota_g == first, neg, blk), axis=0, keepdims=True)
        grp_rows.append(m1 + m2)
    grp = jnp.concatenate(grp_rows, axis=0)
    iota_n = lax.broadcasted_iota(jnp.int32, (N_EXPERT_GROUPS, tm), 0)
    gsel = jnp.zeros((N_EXPERT_GROUPS, tm), F32)
    for _ in range(TOPK_GROUPS):
        m = jnp.max(grp, axis=0, keepdims=True)
        first = jnp.min(jnp.where(grp == m, iota_n, N_EXPERT_GROUPS), axis=0, keepdims=True)
        hit = iota_n == first
        gsel = jnp.where(hit, 1.0, gsel)
        grp = jnp.where(hit, neg, grp)
    emask = jnp.concatenate([jnp.broadcast_to(gsel[gi:gi + 1], (per, tm)) for gi in range(N_EXPERT_GROUPS)], axis=0)
    cand = jnp.where(emask > 0.0, choice, neg)
    iota_e = lax.broadcasted_iota(jnp.int32, (N_EXPERTS, tm), 0)
    sel = jnp.zeros((N_EXPERTS, tm), F32)
    chosen = []
    for _ in range(TOP_K):
        m = jnp.max(cand, axis=0, keepdims=True)
        first = jnp.min(jnp.where(cand == m, iota_e, N_EXPERTS), axis=0, keepdims=True)
        hit = iota_e == first
        sel = jnp.where(hit, 1.0, sel)
        cand = jnp.where(hit, neg, cand)
        chosen.append(first)
    w = scores * sel
    gate = w / jnp.sum(w, axis=0, keepdims=True) * ROUTED_SCALE
    if not compact:
        h_ref, gate_ref = rest
        h_ref[0] = h.astype(BF16)
        gate_ref[0] = gate
        return
    hp_ref, eid_ref, rank_ref, w_ref, cnt_out_ref, cnt_ref = rest
    hp_ref[0] = _pack_halves(h)

    @pl.when((pl.program_id(0) == 0) & (pl.program_id(1) == 0))
    def _():
        cnt_ref[...] = jnp.zeros(cnt_ref.shape, F32)

    src = lax.broadcasted_iota(jnp.int32, (tm, tm), 0)
    dst = lax.broadcasted_iota(jnp.int32, (tm, tm), 1)
    running = jnp.dot(sel.astype(BF16), (src <= dst).astype(BF16), preferred_element_type=F32)
    rank_dense = cnt_ref[:, 0:1] + running - 1.0
    e_rows, r_rows, w_rows = [], [], []
    for first in chosen:
        hit = iota_e == first
        e_rows.append(first)
        r_rows.append(jnp.sum(jnp.where(hit, rank_dense, 0.0), axis=0, keepdims=True))
        w_rows.append(jnp.sum(jnp.where(hit, gate, 0.0), axis=0, keepdims=True))
    eid_ref[...] = jnp.concatenate(e_rows, axis=0)
    rank_ref[...] = jnp.concatenate(r_rows, axis=0).astype(jnp.int32)
    w_ref[...] = jnp.concatenate(w_rows, axis=0)
    cnt_ref[...] = cnt_ref[...] + running[:, tm - 1:tm]
    cnt_out_ref[...] = cnt_ref[...]


def _router(x, g, scale, shift, w_router, e_bias, tm=512, compact=False):
    b, s, d = x.shape
    tm = min(tm, s)
    e = w_router.shape[1]
    row = pl.BlockSpec((1, tm, d), lambda i, j: (i, j, 0))
    mod = pl.BlockSpec((1, 1, d), lambda i, j: (i, 0, 0))
    if compact:
        nj = s // tm
        tok = pl.BlockSpec((TOP_K, tm), lambda i, j: (0, i * nj + j))
        out_specs = [pl.BlockSpec((1, tm, d // 2), lambda i, j: (i, j, 0)), tok, tok, tok, _const_spec((e, LANES))]
        out_shape = [jax.ShapeDtypeStruct((b, s, d // 2), jnp.int32), jax.ShapeDtypeStruct((TOP_K, b * s), jnp.int32),
                     jax.ShapeDtypeStruct((TOP_K, b * s), jnp.int32), jax.ShapeDtypeStruct((TOP_K, b * s), F32),
                     jax.ShapeDtypeStruct((e, LANES), F32)]
        scratch = [pltpu.VMEM((e, LANES), F32)]
        semantics = ('arbitrary', 'arbitrary')
    else:
        out_specs = [row, pl.BlockSpec((1, e, tm), lambda i, j: (i, 0, j))]
        out_shape = [jax.ShapeDtypeStruct((b, s, d), BF16), jax.ShapeDtypeStruct((b, e, s), F32)]
        scratch = []
        semantics = ('parallel', 'parallel')
    return pl.pallas_call(
        functools.partial(_router_kernel, compact=compact),
        grid=(b, s // tm),
        in_specs=[row, _const_spec((1, d)), mod, mod, _const_spec((e, d)), _const_spec((e, 1))],
        out_specs=out_specs,
        out_shape=out_shape,
        scratch_shapes=scratch,
        compiler_params=_params(*semantics),
        name='router',
    )(x, g.reshape(1, d), scale, shift, w_router.T, e_bias.reshape(e, 1))


def _moe_kernel(h_ref, x_ref, gate_ref, g2_ref, wg_ref, wu_ref, wd_ref, sg_ref, su_ref, sd_ref, *rest, final):
    if final:
        fg_ref, o_ref, acc_ref = rest
    else:
        o_ref, acc_ref = rest
    e = pl.program_id(2)
    h = h_ref[0]

    @pl.when(e == 0)
    def _():
        a = jnp.dot(h, sg_ref[...], preferred_element_type=F32)
        u = jnp.dot(h, su_ref[...], preferred_element_type=F32)
        acc_ref[...] = jnp.dot((_silu(a) * u).astype(BF16), sd_ref[...], preferred_element_type=F32)

    lane = lax.broadcasted_iota(jnp.int32, gate_ref.shape[1:], 1)
    gcol = jnp.sum(jnp.where(lane == e, gate_ref[0], 0.0), axis=-1, keepdims=True)
    a = jnp.dot(h, wg_ref[0].astype(BF16), preferred_element_type=F32)
    u = jnp.dot(h, wu_ref[0].astype(BF16), preferred_element_type=F32)
    acc_ref[...] += jnp.dot((_silu(a) * u * gcol).astype(BF16), wd_ref[0].astype(BF16), preferred_element_type=F32)

    @pl.when(e == pl.num_programs(2) - 1)
    def _():
        y = x_ref[0] + g2_ref[0] * acc_ref[...]
        if final:
            y = _rms(y) * fg_ref[...]
        o_ref[0] = y


def _moe(h2, x, gate, g2, layer, w_gate, w_up, w_down, s_gate, s_up, s_down, final_g=None, tm=1024):
    b, s, d = x.shape
    tm = min(tm, s)
    _, e, _, ff = w_gate.shape
    row = pl.BlockSpec((1, tm, d), lambda i, j, k: (i, j, 0))
    ins = [h2, x, gate, g2, w_gate, w_up, w_down, s_gate, s_up, s_down]
    in_specs = [row, row, pl.BlockSpec((1, tm, e), lambda i, j, k: (i, j, 0)),
                pl.BlockSpec((1, 1, d), lambda i, j, k: (i, 0, 0)),
                pl.BlockSpec((None, 1, d, ff), lambda i, j, k: (layer, k, 0, 0)),
                pl.BlockSpec((None, 1, d, ff), lambda i, j, k: (layer, k, 0, 0)),
                pl.BlockSpec((None, 1, ff, d), lambda i, j, k: (layer, k, 0, 0)),
                _const_spec(s_gate.shape), _const_spec(s_up.shape), _const_spec(s_down.shape)]
    if final_g is not None:
        ins.append(final_g.reshape(1, d))
        in_specs.append(_const_spec((1, d)))
    return pl.pallas_call(
        functools.partial(_moe_kernel, final=final_g is not None),
        grid=(b, s // tm, e),
        in_specs=in_specs,
        out_specs=row,
        out_shape=jax.ShapeDtypeStruct(x.shape, F32),
        scratch_shapes=[pltpu.VMEM((tm, d), F32)],
        compiler_params=_params('parallel', 'parallel', 'arbitrary'),
        name='moe',
    )(*ins)


MOE_ROW_TILE = 512
SC_ROWS = 128
V7X_SC_CORES = 2
V7X_SC_SUBCORES = 16


def _pack_halves(x):
    n = x.shape[1] // 2
    lo = pltpu.bitcast(x[:, :n].astype(BF16).astype(F32), jnp.int32)
    hi = pltpu.bitcast(x[:, n:].astype(BF16).astype(F32), jnp.int32)
    return jnp.bitwise_or(jnp.bitwise_and(hi, -65536), lax.shift_right_logical(lo, 16))


def _unpack_halves(p):
    lo = pltpu.bitcast(lax.shift_left(p, 16), F32).astype(BF16)
    hi = pltpu.bitcast(jnp.bitwise_and(p, -65536), F32).astype(BF16)
    return lo, hi


def _route_pos_kernel(off_ref, eid_ref, rank_ref, pos_ref):
    eid = eid_ref[...]
    base = jnp.zeros(eid.shape, jnp.int32)
    for e in range(N_EXPERTS):
        base = jnp.where(eid == e, off_ref[e], base)
    pos_ref[...] = base + rank_ref[...]


def _route_pos(offsets, eid, rank):
    return pl.pallas_call(
        _route_pos_kernel,
        grid=(1,),
        in_specs=[pl.BlockSpec(memory_space=pltpu.SMEM), _const_spec(eid.shape), _const_spec(rank.shape)],
        out_specs=_const_spec(eid.shape),
        out_shape=jax.ShapeDtypeStruct(eid.shape, jnp.int32),
        compiler_params=_params('arbitrary'),
        name='route_pos',
    )(offsets, eid, rank)


def _sc_mesh():
    return plsc.VectorSubcoreMesh(core_axis_name='c', subcore_axis_name='s', num_cores=V7X_SC_CORES,
                                  num_subcores=V7X_SC_SUBCORES)


def _sc_dispatch(hp, pos, n_rows):
    t, w = hp.shape
    k = pos.shape[0]
    workers = V7X_SC_CORES * V7X_SC_SUBCORES
    per_worker = t // workers
    pos_flat = pos.reshape(k * t)

    @functools.partial(pl.kernel, mesh=_sc_mesh(), out_type=jax.ShapeDtypeStruct((n_rows, w), jnp.int32),
                       scratch_types=[pltpu.VMEM((SC_ROWS,), jnp.int32), pltpu.VMEM((SC_ROWS, w), jnp.int32),
                                      pltpu.SemaphoreType.DMA])
    def scatter(hp_hbm, pos_hbm, out_hbm, idx_v, rows_v, sem):
        wid = lax.axis_index('s') * V7X_SC_CORES + lax.axis_index('c')

        @pl.loop(0, per_worker // SC_ROWS)
        def _(i):
            t0 = pl.multiple_of(wid * per_worker + i * SC_ROWS, SC_ROWS)
            pltpu.sync_copy(hp_hbm.at[pl.ds(t0, SC_ROWS)], rows_v)
            for j in range(k):
                pltpu.sync_copy(pos_hbm.at[pl.ds(pl.multiple_of(j * t + t0, SC_ROWS), SC_ROWS)], idx_v)
                pltpu.async_copy(rows_v, out_hbm.at[idx_v], sem).wait()

    return scatter(hp, pos_flat)


def _sc_collect(yp, pos):
    _, w = yp.shape
    k, t = pos.shape
    workers = V7X_SC_CORES * V7X_SC_SUBCORES
    per_worker = k * t // workers
    pos_flat = pos.reshape(k * t)

    @functools.partial(pl.kernel, mesh=_sc_mesh(), out_type=jax.ShapeDtypeStruct((k * t, w), jnp.int32),
                       scratch_types=[pltpu.VMEM((SC_ROWS,), jnp.int32), pltpu.VMEM((SC_ROWS, w), jnp.int32),
                                      pltpu.SemaphoreType.DMA])
    def gather(yp_hbm, pos_hbm, out_hbm, idx_v, rows_v, sem):
        wid = lax.axis_index('s') * V7X_SC_CORES + lax.axis_index('c')

        @pl.loop(0, per_worker // SC_ROWS)
        def _(i):
            r0 = pl.multiple_of(wid * per_worker + i * SC_ROWS, SC_ROWS)
            pltpu.sync_copy(pos_hbm.at[pl.ds(r0, SC_ROWS)], idx_v)
            pltpu.async_copy(yp_hbm.at[idx_v], rows_v, sem).wait()
            pltpu.sync_copy(rows_v, out_hbm.at[pl.ds(r0, SC_ROWS)])

    return gather(yp, pos_flat)


def _expert_kernel(te_ref, nu_ref, x_ref, wg_ref, wu_ref, wd_ref, o_ref, wg_s, wu_s, wd_s):
    i = pl.program_id(0)

    @pl.when(i < nu_ref[0])
    def _():
        @pl.when((i == 0) | (te_ref[i] != te_ref[jnp.maximum(i - 1, 0)]))
        def _():
            wg_s[...] = wg_ref[0].astype(BF16)
            wu_s[...] = wu_ref[0].astype(BF16)
            wd_s[...] = wd_ref[0].astype(BF16)

        x = jnp.concatenate(_unpack_halves(x_ref[...]), axis=1)
        a = jnp.dot(x, wg_s[...], preferred_element_type=F32)
        u = jnp.dot(x, wu_s[...], preferred_element_type=F32)
        y = jnp.dot((_silu(a) * u).astype(BF16), wd_s[...], preferred_element_type=F32)
        o_ref[...] = _pack_halves(y)


def _experts(xp, tile_expert, n_used, layer, w_gate, w_up, w_down):
    n_rows, half = xp.shape
    _, _, d, ff = w_gate.shape
    r = MOE_ROW_TILE
    row = pl.BlockSpec((r, half), lambda i, te, nu: (i, 0))
    grid_spec = pltpu.PrefetchScalarGridSpec(
        num_scalar_prefetch=2,
        grid=(n_rows // r,),
        in_specs=[row,
                  pl.BlockSpec((None, 1, d, ff), lambda i, te, nu: (layer, te[i], 0, 0)),
                  pl.BlockSpec((None, 1, d, ff), lambda i, te, nu: (layer, te[i], 0, 0)),
                  pl.BlockSpec((None, 1, ff, d), lambda i, te, nu: (layer, te[i], 0, 0))],
        out_specs=row,
        scratch_shapes=[pltpu.VMEM((d, ff), BF16), pltpu.VMEM((d, ff), BF16), pltpu.VMEM((ff, d), BF16)],
    )
    return pl.pallas_call(
        _expert_kernel,
        grid_spec=grid_spec,
        out_shape=jax.ShapeDtypeStruct((n_rows, half), jnp.int32),
        compiler_params=_params('arbitrary'),
        name='experts',
    )(tile_expert, n_used, xp, w_gate, w_up, w_down)


def _combine_kernel(yg_ref, w_ref, hp_ref, x_ref, g2_ref, sg_ref, su_ref, sd_ref, *rest, final):
    if final:
        fg_ref, o_ref = rest
    else:
        (o_ref,) = rest
    half = hp_ref.shape[2]
    h = jnp.concatenate(_unpack_halves(hp_ref[0]), axis=1)
    a = jnp.dot(h, sg_ref[...], preferred_element_type=F32)
    u = jnp.dot(h, su_ref[...], preferred_element_type=F32)
    acc = jnp.dot((_silu(a) * u).astype(BF16), sd_ref[...], preferred_element_type=F32)
    acc_lo, acc_hi = acc[:, :half], acc[:, half:]
    wts = w_ref[0]
    for k in range(yg_ref.shape[0]):
        ylo, yhi = _unpack_halves(yg_ref[k, 0])
        wk = wts[:, k:k + 1]
        acc_lo = acc_lo + wk * ylo.astype(F32)
        acc_hi = acc_hi + wk * yhi.astype(F32)
    y = x_ref[0] + g2_ref[0] * jnp.concatenate([acc_lo, acc_hi], axis=1)
    if final:
        y = _rms(y) * fg_ref[...]
    o_ref[0] = y


def _combine(yg, wts, hp, x, g2, s_gate, s_up, s_down, final_g=None, tm=256):
    b, s, d = x.shape
    k = yg.shape[0]
    half = d // 2
    row = pl.BlockSpec((1, tm, d), lambda i, j: (i, j, 0))
    prow = pl.BlockSpec((1, tm, half), lambda i, j: (i, j, 0))
    ins = [yg, wts, hp, x, g2, s_gate, s_up, s_down]
    in_specs = [pl.BlockSpec((k, 1, tm, half), lambda i, j: (0, i, j, 0)), pl.BlockSpec((1, tm, k), lambda i, j: (i, j, 0)),
                prow, row, pl.BlockSpec((1, 1, d), lambda i, j: (i, 0, 0)),
                _const_spec(s_gate.shape), _const_spec(s_up.shape), _const_spec(s_down.shape)]
    if final_g is not None:
        ins.append(final_g.reshape(1, d))
        in_specs.append(_const_spec((1, d)))
    return pl.pallas_call(
        functools.partial(_combine_kernel, final=final_g is not None),
        grid=(b, s // tm),
        in_specs=in_specs,
        out_specs=row,
        out_shape=jax.ShapeDtypeStruct(x.shape, F32),
        compiler_params=_params('parallel', 'parallel'),
        name='moe_combine',
    )(*ins)


def _routed_moe(x, g, scale, shift, g2, w_router, e_bias, layer, w_gate, w_up, w_down, s_gate, s_up, s_down, final_g=None):
    b, s, d = x.shape
    t = b * s
    hp, eid, rank, wts, counts = _router(x, g, scale, shift, w_router, e_bias, compact=True)
    counts = counts[:, 0].astype(jnp.int32)
    r = MOE_ROW_TILE
    padded = (counts + (r - 1)) // r * r
    ends = jnp.cumsum(padded)
    offsets = ends - padded
    n_rows = t * TOP_K + N_EXPERTS * r
    tile_start = jnp.arange(n_rows // r, dtype=jnp.int32) * r
    tile_expert = jnp.minimum(jnp.sum((tile_start[:, None] >= ends[None, :]).astype(jnp.int32), axis=1), N_EXPERTS - 1)
    n_used = (ends[-1] // r).reshape(1).astype(jnp.int32)
    pos = _route_pos(offsets.astype(jnp.int32), eid, rank)
    xp = _sc_dispatch(hp.reshape(t, d // 2), pos, n_rows)
    yp = _experts(xp, tile_expert.astype(jnp.int32), n_used, layer, w_gate, w_up, w_down)
    yg = _sc_collect(yp, pos).reshape(TOP_K, b, s, d // 2)
    return _combine(yg, wts.T.reshape(b, s, TOP_K), hp, x, g2, s_gate, s_up, s_down, final_g)


def _mixers(p, pc, ctx_out, prm, l, lam_init, rope_tabs, lb_terms):
    s = p['hy_v'].shape[1]
    sc = pc['hy_v'].shape[1]

    hy_args = (prm['hy_w1'][l], prm['hy_b1'][l], prm['hy_w2'][l], prm['hy_b2'][l], prm['hy_w3'][l], prm['hy_b3'][l],
               prm['hy_sin_freq'][l], prm['hy_decay'][l])
    y_hy = _hyena([p['hy_v'], p['hy_x1'], p['hy_x2']], prm['hy_conv_w'][l], prm['hy_conv_b'][l],
                  _hy_filters(s, *hy_args), prm['hy_bias'][l], inner=128)
    yc_hy = None
    if ctx_out:
        yc_hy = _hyena([pc['hy_v'], pc['hy_x1'], pc['hy_x2']], prm['hy_conv_w'][l], prm['hy_conv_b'][l],
                       _hy_filters(sc, *hy_args), prm['hy_bias'][l], inner=32)

    lp = prm['da_lambda'][l].astype(F32)
    lam = jnp.exp(jnp.sum(lp[0] * lp[1])) - jnp.exp(jnp.sum(lp[2] * lp[3])) + lam_init
    da_kw = dict(heads=DA_HEADS, ncomp=2, scale=DA_HEAD_DIM ** -0.5, lam=lam, subln_g=prm['da_subln_g'][l],
                 post_scale=1.0 - lam_init)
    da_ctx = ([pc['da_k']], pc['da_v'])
    y_da = _attention([p['da_q']], [da_ctx, ([p['da_k']], p['da_v'])], **da_kw)
    yc_da = _attention([pc['da_q']], [da_ctx], **da_kw) if ctx_out else None

    wq = prm['mla_w_q_up'][l].reshape(MLA_Q_RANK, MLA_HEADS, MLA_NOPE_DIM + MLA_ROPE_DIM)
    wq_n = wq[:, :, :MLA_NOPE_DIM].reshape(MLA_Q_RANK, -1).astype(BF16)
    wq_r = wq[:, :, MLA_NOPE_DIM:].reshape(MLA_Q_RANK, -1).astype(BF16)
    wkv = prm['mla_w_kv_up'][l].reshape(MLA_KV_RANK, MLA_HEADS, MLA_NOPE_DIM + MLA_V_DIM)
    wkv_n = wkv[:, :, :MLA_NOPE_DIM].reshape(MLA_KV_RANK, -1).astype(BF16)
    wkv_v = wkv[:, :, MLA_NOPE_DIM:].reshape(MLA_KV_RANK, -1).astype(BF16)

    def queries(qd, tabs):
        return _norm_proj(qd, prm['mla_q_norm_g'][l], [(wq_n, F32, False, MLA_HEADS), (wq_r, F32, True, MLA_HEADS)],
                          rope_tabs=tabs)

    def keys_values(kvd):
        return _norm_proj(kvd, prm['mla_kv_norm_g'][l], [(wkv_n, BF16, False, MLA_HEADS), (wkv_v, BF16, False, MLA_HEADS)])

    kn_l, v_l = keys_values(p['mla_kv'])
    kn_c, v_c = keys_values(pc['mla_kv'])
    mla_kw = dict(heads=MLA_HEADS, ncomp=1, scale=(MLA_NOPE_DIM + MLA_ROPE_DIM) ** -0.5)
    mla_ctx = ([kn_c, pc['mla_kr']], v_c)
    y_mla = _attention(queries(p['mla_q'], rope_tabs), [mla_ctx, ([kn_l, p['mla_kr']], v_l)], **mla_kw)
    yc_mla = _attention(queries(pc['mla_q'], None), [mla_ctx], **mla_kw) if ctx_out else None

    o, oc = _hgrn(p['hg_q'], p['hg_ff'], p['hg_fb'], p['hg_i'], pc['hg_q'], pc['hg_ff'], pc['hg_fb'], pc['hg_i'],
                  lb_terms, prm['hg_norm_g'][l])
    return (y_hy, y_da, y_mla, o), (yc_hy, yc_da, yc_mla, oc)


def kernel(x, c, ctx, c_ctx, w_ada, b_ada, norm1_g, norm2_g, w_in, w_out, hy_conv_w, hy_conv_b, hy_w1, hy_b1, hy_w2, hy_b2, hy_w3, hy_b3, hy_sin_freq, hy_decay, hy_bias, da_lambda, da_subln_g, mla_q_norm_g, mla_w_q_up, mla_kv_norm_g, mla_w_kv_up, hg_lower_bounds, hg_norm_g, moe_w_router, moe_bias, moe_w_gate, moe_w_up, moe_w_down, moe_sh_gate, moe_sh_up, moe_sh_down, final_norm_g):
    prm = dict(hy_conv_w=hy_conv_w, hy_conv_b=hy_conv_b, hy_w1=hy_w1, hy_b1=hy_b1, hy_w2=hy_w2, hy_b2=hy_b2,
               hy_w3=hy_w3, hy_b3=hy_b3, hy_sin_freq=hy_sin_freq, hy_decay=hy_decay, hy_bias=hy_bias,
               da_lambda=da_lambda, da_subln_g=da_subln_g, mla_q_norm_g=mla_q_norm_g, mla_w_q_up=mla_w_q_up,
               mla_kv_norm_g=mla_kv_norm_g, mla_w_kv_up=mla_w_kv_up, hg_norm_g=hg_norm_g)
    b, n_lat, d = x.shape
    depth = w_in.shape[0]
    rows = n_lat // GRID_W
    row_pos = jnp.repeat(jnp.arange(rows, dtype=jnp.int32), GRID_W)
    col_pos = jnp.tile(jnp.arange(GRID_W, dtype=jnp.int32), rows)
    rope_tabs = _rope_tables(row_pos, col_pos, 2 * DA_HEADS * DA_HEAD_DIM)
    lbs = jnp.cumsum(jax.nn.softmax(hg_lower_bounds.astype(F32), axis=1), axis=1)
    lbs = lbs - lbs[:, :1]
    cond = jnp.concatenate([c, c_ctx[None], jnp.zeros((8 - b - 1, d), F32)], axis=0)

    for l in range(depth):
        ctx_out = l < depth - 1
        mods = _ada(cond, w_ada[l], b_ada[l])
        sh1, sc1, g1, sh2, sc2, g2 = [m[:, None, :] for m in jnp.split(mods[:b], 6, axis=-1)]
        mc = [jnp.broadcast_to(m[:, None, :], (b, 1, d)) for m in jnp.split(mods[b:b + 1], 6, axis=-1)]

        off = 0
        outs = []
        for _, wdt, dt, rope, split, rep in _SEGMENTS:
            w = w_in[l][:, off:off + wdt].astype(BF16)
            outs.append((jnp.tile(w, (1, rep)) if rep > 1 else w, dt, rope, split))
            off += wdt
        names = [seg[0] for seg in _SEGMENTS]
        p = dict(zip(names, _norm_proj(x, norm1_g[l], outs, sc1, sh1, rope_tabs=rope_tabs)))
        pc = dict(zip(names, _norm_proj(ctx, norm1_g[l], outs, mc[1], mc[0])))

        lb = lbs[:, l]
        lb_terms = jnp.stack([jnp.log(lb), jnp.log1p(-lb), 1.0 - lb], axis=1)
        lam_init = 0.8 - 0.6 * math.exp(-0.3 * l)
        lat_parts, ctx_parts = _mixers(p, pc, ctx_out, prm, l, lam_init, rope_tabs, lb_terms)

        w_out_b = w_out[l].astype(BF16)
        moe_w = (l, moe_w_gate, moe_w_up, moe_w_down,
                 moe_sh_gate[l].astype(BF16), moe_sh_up[l].astype(BF16), moe_sh_down[l].astype(BF16))

        if ctx_out:
            ctx = _out_proj(*ctx_parts, pc['hg_g'], ctx, mc[2], w_out_b)
            flat = ctx.reshape(1, -1, d)
            h2c, gate_c = _router(flat, norm2_g[l], mc[4][:1], mc[3][:1], moe_w_router[l], moe_bias[l])
            ctx = _moe(h2c, flat, gate_c.transpose(0, 2, 1), mc[5][:1], *moe_w).reshape(ctx.shape)

        x = _out_proj(*lat_parts, p['hg_g'], x, g1, w_out_b)
        x = _routed_moe(x, norm2_g[l], sc2, sh2, g2, moe_w_router[l], moe_bias[l], *moe_w,
                        final_g=None if ctx_out else final_norm_g)

    return x
```

```python
import functools
import math

import numpy as np
import jax
import jax.numpy as jnp
from jax import lax
from jax.experimental import pallas as pl
from jax.experimental.pallas import tpu as pltpu
from jax.experimental.pallas import tpu_sc as plsc

F32 = jnp.float32
BF16 = jnp.bfloat16
HIGHEST = lax.Precision.HIGHEST

GRID_W = 64
HY_WIDTH = 256
HY_ORDER = 2
HY_BANDS = 16
DA_HEADS = 4
DA_HEAD_DIM = 32
MLA_HEADS = 4
MLA_Q_RANK = 192
MLA_KV_RANK = 128
MLA_NOPE_DIM = 64
MLA_ROPE_DIM = 32
MLA_V_DIM = 64
HG_HEADS = 4
HG_KEY_DIM = 64
HG_VAL_DIM = 64
HG_CHUNK = 64
HG_SUB = 8
HG_UNROLL = 4
N_EXPERTS = 64
N_EXPERT_GROUPS = 8
TOPK_GROUPS = 4
TOP_K = 8
ROUTED_SCALE = 2.5
ROPE_BASE = 10000.0
NORM_EPS = 1e-6

V7X_VMEM_LIMIT_BYTES = 56 * 1024 * 1024
LANES = 128

_SEGMENTS = (
    ('hy_v', HY_WIDTH, F32, False, 0, 1), ('hy_x1', HY_WIDTH, F32, False, 0, 1), ('hy_x2', HY_WIDTH, F32, False, 0, 1),
    ('da_q', 2 * DA_HEADS * DA_HEAD_DIM, F32, True, 2 * DA_HEADS, 1),
    ('da_k', 2 * DA_HEADS * DA_HEAD_DIM, BF16, True, 2 * DA_HEADS, 1),
    ('da_v', 2 * DA_HEADS * DA_HEAD_DIM, BF16, False, DA_HEADS, 1),
    ('mla_q', MLA_Q_RANK, F32, False, 0, 1), ('mla_kv', MLA_KV_RANK, F32, False, 0, 1),
    ('mla_kr', MLA_ROPE_DIM, BF16, True, MLA_HEADS, MLA_HEADS),
    ('hg_q', HG_HEADS * HG_KEY_DIM, F32, False, 0, 1), ('hg_ff', HG_HEADS * HG_KEY_DIM, F32, False, 0, 1),
    ('hg_fb', HG_HEADS * HG_KEY_DIM, F32, False, 0, 1), ('hg_i', HG_HEADS * HG_VAL_DIM, F32, False, 0, 1),
    ('hg_g', HG_HEADS * HG_VAL_DIM, F32, False, 0, 1),
)


def _params(*semantics):
    return pltpu.CompilerParams(dimension_semantics=semantics, vmem_limit_bytes=V7X_VMEM_LIMIT_BYTES)


def _const_spec(shape):
    nd = len(shape)
    return pl.BlockSpec(shape, lambda *_: (0,) * nd)


def _rms(x, eps=NORM_EPS):
    return x * lax.rsqrt(jnp.mean(x * x, axis=-1, keepdims=True) + eps)


def _silu(x):
    return x * jax.nn.sigmoid(x)


def _dot_nt(a, b, **kw):
    return lax.dot_general(a, b, (((1,), (1,)), ((), ())), preferred_element_type=F32, **kw)


def _ada_kernel(c_ref, w_ref, b_ref, o_ref):
    s = _silu(c_ref[...])
    o_ref[...] = jnp.dot(s, w_ref[...], precision=HIGHEST, preferred_element_type=F32) + b_ref[...]


def _ada(cond, w, b):
    r, d = cond.shape
    n = w.shape[1]
    tn = 1536
    return pl.pallas_call(
        _ada_kernel,
        grid=(n // tn,),
        in_specs=[_const_spec((r, d)), pl.BlockSpec((d, tn), lambda j: (0, j)), pl.BlockSpec((1, tn), lambda j: (0, j))],
        out_specs=pl.BlockSpec((r, tn), lambda j: (0, j)),
        out_shape=jax.ShapeDtypeStruct((r, n), F32),
        compiler_params=_params('arbitrary'),
        name='ada',
    )(cond, w, b.reshape(1, n))


ROPE_UNIT = 32


def _rope_tables(row, col, width):
    n = ROPE_UNIT // 4
    inv = ROPE_BASE ** (-jnp.arange(n, dtype=F32) / n)
    units = width // ROPE_UNIT
    parts_c, parts_a, parts_b = [], [], []
    zero = jnp.zeros((row.shape[0], n), F32)
    for pos in (row, col):
        ang = pos.astype(F32)[:, None] * inv
        cos, sin = jnp.cos(ang), jnp.sin(ang)
        parts_c += [cos, cos]
        parts_a += [zero, sin]
        parts_b += [-sin, zero]
    tile = lambda ps: jnp.tile(jnp.concatenate(ps, axis=1), (1, units))
    return tile(parts_c), tile(parts_a), tile(parts_b)


def _norm_proj_kernel(*refs, n_w, modulate, ropes, splits):
    x_ref, g_ref = refs[0], refs[1]
    pos = 2
    if modulate:
        sc_ref, sh_ref = refs[2], refs[3]
        pos = 4
    if any(ropes):
        rc_ref, ra_ref, rb_ref = refs[pos:pos + 3]
        pos += 3
    w_refs = refs[pos:pos + n_w]
    o_refs = refs[pos + n_w:]
    y = _rms(x_ref[0]) * g_ref[...]
    if modulate:
        y = y * (1.0 + sc_ref[0]) + sh_ref[0]
    yb = y.astype(BF16)
    for w_ref, o_ref, rope, split in zip(w_refs, o_refs, ropes, splits):
        o = jnp.dot(yb, w_ref[...], preferred_element_type=F32)
        if rope:
            wd = o.shape[1]
            shift = ROPE_UNIT // 4
            o = (o * rc_ref[:, :wd] + pltpu.roll(o, shift, axis=1) * ra_ref[:, :wd]
                 + pltpu.roll(o, wd - shift, axis=1) * rb_ref[:, :wd])
        if split:
            unit = o.shape[1] // split
            for u in range(split):
                o_ref[0, u] = o[:, u * unit:(u + 1) * unit].astype(o_ref.dtype)
        else:
            o_ref[0] = o.astype(o_ref.dtype)


def _norm_proj(x, g, outs, scale=None, shift=None, rope_tabs=None, tm=512):
    b, s, k = x.shape
    tm = min(tm, s)
    modulate = scale is not None
    ropes = tuple(bool(o[2]) and rope_tabs is not None for o in outs)
    splits = tuple(o[3] for o in outs)
    ins = [x, g.reshape(1, k)]
    in_specs = [pl.BlockSpec((1, tm, k), lambda i, j: (i, j, 0)), _const_spec((1, k))]
    if modulate:
        ins += [scale, shift]
        in_specs += [pl.BlockSpec((1, 1, k), lambda i, j: (i, 0, 0))] * 2
    if any(ropes):
        ins += list(rope_tabs)
        in_specs += [pl.BlockSpec((tm, rope_tabs[0].shape[1]), lambda i, j: (j, 0))] * 3
    out_specs, out_shape = [], []
    for w, dt, _, split in outs:
        ins.append(w)
        in_specs.append(_const_spec(w.shape))
        n = w.shape[1]
        if split:
            out_specs.append(pl.BlockSpec((1, split, tm, n // split), lambda i, j: (i, 0, j, 0)))
            out_shape.append(jax.ShapeDtypeStruct((b, split, s, n // split), dt))
        else:
            out_specs.append(pl.BlockSpec((1, tm, n), lambda i, j: (i, j, 0)))
            out_shape.append(jax.ShapeDtypeStruct((b, s, n), dt))
    return pl.pallas_call(
        functools.partial(_norm_proj_kernel, n_w=len(outs), modulate=modulate, ropes=ropes, splits=splits),
        grid=(b, s // tm),
        in_specs=in_specs,
        out_specs=out_specs,
        out_shape=out_shape,
        compiler_params=_params('parallel', 'parallel'),
        name='norm_proj',
    )(*ins)


def _hy_filter_kernel(w1t_ref, w1s_ref, w1c_ref, b1_ref, w2_ref, b2_ref, w3_ref, b3_ref, fr_ref, dec_ref, o_ref, *, n):
    t = lax.broadcasted_iota(jnp.int32, (n, 1), 0).astype(F32) / n
    bands = lax.broadcasted_iota(jnp.int32, (1, HY_BANDS), 1).astype(F32) + 1.0
    ang = (2.0 * jnp.pi) * t * bands
    pre = (t * w1t_ref[...]
           + jnp.dot(jnp.sin(ang), w1s_ref[...], precision=HIGHEST, preferred_element_type=F32)
           + jnp.dot(jnp.cos(ang), w1c_ref[...], precision=HIGHEST, preferred_element_type=F32)
           + b1_ref[...])
    hid = jnp.sin(fr_ref[0:1, :] * pre)
    hid = jnp.sin(fr_ref[1:2, :] * (jnp.dot(hid, w2_ref[...], precision=HIGHEST, preferred_element_type=F32) + b2_ref[...]))
    filt = jnp.dot(hid, w3_ref[...], precision=HIGHEST, preferred_element_type=F32) + b3_ref[...]
    filt = filt * jnp.exp(-t * jnp.abs(dec_ref[...]))
    col = jnp.sum(jnp.abs(filt), axis=0, keepdims=True) - jnp.abs(filt[0:1, :])
    w = HY_WIDTH
    for o in range(HY_ORDER):
        lo = o * 2 * w
        f0 = filt[0:1, lo:lo + w] + filt[0:1, lo + w:lo + 2 * w]
        inv = 1.0 / (col[:, lo:lo + w] + col[:, lo + w:lo + 2 * w] + jnp.abs(f0))
        o_ref[:, lo:lo + w] = filt[:, lo:lo + w] * inv
        o_ref[:, lo + w:lo + 2 * w] = filt[:, lo + w:lo + 2 * w] * inv


def _hy_filters(n, w1, b1, w2, b2, w3, b3, freq, decay):
    cols = w3.shape[1]
    ins = [w1[0:1], w1[1:1 + HY_BANDS], w1[1 + HY_BANDS:], b1.reshape(1, -1), w2, b2.reshape(1, -1), w3,
           b3.reshape(1, -1), freq, decay.reshape(1, -1)]
    out = pl.pallas_call(
        functools.partial(_hy_filter_kernel, n=n),
        grid=(1,),
        in_specs=[_const_spec(a.shape) for a in ins],
        out_specs=_const_spec((n, cols)),
        out_shape=jax.ShapeDtypeStruct((n, cols), F32),
        compiler_params=_params('arbitrary'),
        name='hy_filter',
    )(*ins)
    return out.reshape(n, HY_ORDER, 2, HY_WIDTH)


def _two_sided(filt_n):
    n = filt_n.shape[0]
    hf, hb = filt_n[:, :, 0], filt_n[:, :, 1]
    h = jnp.concatenate([hf[:1] + hb[:1], hf[1:], jnp.zeros((1,) + hf.shape[1:], F32), hb[:0:-1]], axis=0)
    return h.reshape(2 * n, HY_ORDER * HY_WIDTH)


def _short_conv_kernel(*refs, s):
    x_refs, w_refs, b_refs, o_refs = refs[0:3], refs[3:6], refs[6:9], refs[9:12]
    row = lax.broadcasted_iota(jnp.int32, (s, 1), 0)
    for x_ref, w_ref, b_ref, o_ref in zip(x_refs, w_refs, b_refs, o_refs):
        x = x_ref[0]
        prev = jnp.where(row == 0, 0.0, pltpu.roll(x, 1, axis=0))
        nxt = jnp.where(row == s - 1, 0.0, pltpu.roll(x, s - 1, axis=0))
        o_ref[0] = prev * w_ref[0:1, :] + x * w_ref[1:2, :] + nxt * w_ref[2:3, :] + b_ref[...]


def _short_conv(parts, conv_w, conv_b):
    b, s, c = parts[0].shape
    tc = LANES
    ws = [conv_w[:, i * c:(i + 1) * c] for i in range(3)]
    bs = [conv_b[i * c:(i + 1) * c].reshape(1, c) for i in range(3)]
    xspec = pl.BlockSpec((1, s, tc), lambda i, j: (i, 0, j))
    return pl.pallas_call(
        functools.partial(_short_conv_kernel, s=s),
        grid=(b, c // tc),
        in_specs=[xspec] * 3 + [pl.BlockSpec((3, tc), lambda i, j: (0, j))] * 3 + [pl.BlockSpec((1, tc), lambda i, j: (0, j))] * 3,
        out_specs=[xspec] * 3,
        out_shape=[jax.ShapeDtypeStruct((b, s, c), F32)] * 3,
        compiler_params=_params('parallel', 'parallel'),
        name='short_conv',
    )(*parts, *ws, *bs)


def _dft_cos_sin(rows, cols, period):
    ang = 2.0 * np.pi * ((np.arange(rows)[:, None] * np.arange(cols)[None, :]) % period) / period
    return np.cos(ang), np.sin(ang)


def _fft_tables(n, inner):
    big = 2 * n
    n1 = big // inner
    c1, s1 = _dft_cos_sin(n1, n1, n1)
    h = n1 // 2
    outer_data = np.block([[c1[:, :h], s1[:, :h]], [-s1[:, :h], c1[:, :h]]])
    outer_real = np.concatenate([c1, -s1], axis=0)
    outer_inv = np.block([[c1[:h, :], -s1[:h, :]], [s1[:h, :], c1[:h, :]]]) / big
    c2, s2 = _dft_cos_sin(inner, inner, inner)
    inner_fwd = np.block([[c2, s2], [-s2, c2]])
    inner_inv = np.block([[c2, -s2], [s2, c2]])
    ct, st = _dft_cos_sin(n1, inner, big)
    f = lambda a: jnp.asarray(a, F32)
    return dict(n1=n1, inner=inner, outer_data=_hi_lo_cols(outer_data), outer_real=_hi_lo_cols(outer_real),
                outer_inv=_hi_lo_cols(outer_inv),
                inner_fwd=_hi_lo_cols(inner_fwd), inner_inv=_hi_lo_cols(inner_inv),
                tw_cos=f(ct).reshape(n1, inner, 1), tw_sin=f(st).reshape(n1, inner, 1))


def _left_mm_kernel(m_ref, x_ref, o_ref):
    o_ref[0] = jnp.dot(m_ref[...], _hi_lo_rows(x_ref[0]), preferred_element_type=F32)


def _left_mm(m, x, tl=4096):
    p, k, l = x.shape
    mm = m.shape[0]
    tl = min(tl, l)
    return pl.pallas_call(
        _left_mm_kernel,
        grid=(p, l // tl),
        in_specs=[_const_spec(m.shape), pl.BlockSpec((1, k, tl), lambda i, j: (i, 0, j))],
        out_specs=pl.BlockSpec((1, mm, tl), lambda i, j: (i, 0, j)),
        out_shape=jax.ShapeDtypeStruct((p, mm, l), F32),
        compiler_params=_params('parallel', 'parallel'),
        name='fft_outer',
    )(m, x)


def _hi_lo_cols(m):
    m = np.asarray(m, np.float32)
    hi = m.astype(BF16)
    lo = (m - hi.astype(np.float32)).astype(BF16)
    return jnp.asarray(np.concatenate([hi, hi, lo], axis=1))


def _hi_lo_rows(x):
    hi = x.astype(BF16)
    lo = (x - hi.astype(F32)).astype(BF16)
    return jnp.concatenate([hi, lo, hi], axis=0)


def _inner_kernel(a_ref, twc_ref, tws_ref, gf_ref, *rest, convolve, inner, kb):
    for s in range(kb):
        ar, ai = a_ref[0, 0, s], a_ref[0, 1, s]
        tc, ts = twc_ref[s], tws_ref[s]
        br = ar * tc + ai * ts
        bi = ai * tc - ar * ts
        x = jnp.dot(gf_ref[...], _hi_lo_rows(jnp.concatenate([br, bi], axis=0)), preferred_element_type=F32)
        if not convolve:
            o_ref = rest[0]
            o_ref[0, 0, s] = x[:inner]
            o_ref[0, 1, s] = x[inner:]
            continue
        h_ref, gi_ref, o_ref = rest
        xr, xi = x[:inner], x[inner:]
        hr, hi = h_ref[0, 0, s], h_ref[0, 1, s]
        yr = xr * hr - xi * hi
        yi = xr * hi + xi * hr
        z = jnp.dot(gi_ref[...], _hi_lo_rows(jnp.concatenate([yr, yi], axis=0)), preferred_element_type=F32)
        zr, zi = z[:inner], z[inner:]
        o_ref[0, 0, s] = zr * tc - zi * ts
        o_ref[0, 1, s] = zi * tc + zr * ts


def _fft_inner(a, tab, c, h=None, h_block=0):
    p = a.shape[0]
    n1, inner = tab['n1'], tab['inner']
    a5 = a.reshape(p, 2, n1, inner, c)
    tc = 2 * LANES
    kb = 4
    blk = pl.BlockSpec((1, 2, kb, inner, tc), lambda k, j, i: (i, 0, k, 0, j))
    tw_spec = pl.BlockSpec((kb, inner, 1), lambda k, j, i: (k, 0, 0))
    ins = [a5, tab['tw_cos'], tab['tw_sin'], tab['inner_fwd']]
    in_specs = [blk, tw_spec, tw_spec, _const_spec(tab['inner_fwd'].shape)]
    if h is not None:
        ch = h.shape[-1] // inner
        nb = c // tc
        ins += [h.reshape(1, 2, n1, inner, ch), tab['inner_inv']]
        in_specs += [pl.BlockSpec((1, 2, kb, inner, tc), lambda k, j, i: (0, 0, k, 0, h_block * nb + j)),
                     _const_spec(tab['inner_inv'].shape)]
    out = pl.pallas_call(
        functools.partial(_inner_kernel, convolve=h is not None, inner=inner, kb=kb),
        grid=(n1 // kb, c // tc, p),
        in_specs=in_specs,
        out_specs=blk,
        out_shape=jax.ShapeDtypeStruct(a5.shape, F32),
        compiler_params=_params('parallel', 'parallel', 'parallel'),
        name='fft_inner',
    )(*ins)
    return out.reshape(p, 2 * n1, inner * c)


def _gate_kernel(m_ref, z_ref, u_ref, x_ref, bias_ref, *rest, chain):
    y = jnp.dot(m_ref[...], _hi_lo_rows(z_ref[0]), preferred_element_type=F32)
    nxt = x_ref[0] * (y + u_ref[0] * bias_ref[...])
    if chain:
        mf_ref, o_ref, a_ref = rest
        o_ref[0] = nxt
        a_ref[0] = jnp.dot(mf_ref[...], _hi_lo_rows(nxt), preferred_element_type=F32)
    else:
        rest[0][0] = nxt


def _fft_gate(tab, z, u, x, bias_l, chain, tl=4096):
    p, k2, l = z.shape
    n1 = tab['n1']
    tl = min(tl, l)
    row = pl.BlockSpec((1, n1, tl), lambda i, j: (i, 0, j))
    ins = [tab['outer_inv'], z, u, x, bias_l]
    in_specs = [_const_spec(tab['outer_inv'].shape), pl.BlockSpec((1, k2, tl), lambda i, j: (i, 0, j)), row, row,
                pl.BlockSpec((1, tl), lambda i, j: (0, j))]
    out_specs = [row]
    out_shape = [jax.ShapeDtypeStruct((p, n1, l), F32)]
    if chain:
        ins.append(tab['outer_data'])
        in_specs.append(_const_spec(tab['outer_data'].shape))
        out_specs.append(pl.BlockSpec((1, k2, tl), lambda i, j: (i, 0, j)))
        out_shape.append(jax.ShapeDtypeStruct((p, k2, l), F32))
    return pl.pallas_call(
        functools.partial(_gate_kernel, chain=chain),
        grid=(p, l // tl),
        in_specs=in_specs,
        out_specs=out_specs,
        out_shape=out_shape,
        compiler_params=_params('parallel', 'parallel'),
        name='fft_gate',
    )(*ins)


def _hyena(parts, conv_w, conv_b, filt_n, bias, inner):
    b, s, c = parts[0].shape
    tab = _fft_tables(s, inner)
    n1 = tab['n1']
    lanes = inner * c
    h_taps = _two_sided(filt_n).reshape(1, n1, inner * HY_ORDER * c)
    h_spec = _fft_inner(_left_mm(tab['outer_real'], h_taps), tab, HY_ORDER * c)
    v, x1, x2 = [a.reshape(b // 2, n1, lanes) for a in _short_conv(parts, conv_w, conv_b)]
    bias_l = [jnp.tile(bias[o], inner).reshape(1, lanes) for o in range(HY_ORDER)]
    a = _left_mm(tab['outer_data'], v)
    z = _fft_inner(a, tab, c, h_spec, 0)
    z2, a = _fft_gate(tab, z, v, x1, bias_l[0], chain=True)
    z = _fft_inner(a, tab, c, h_spec, 1)
    (z3,) = _fft_gate(tab, z, z2, x2, bias_l[1], chain=False)
    return z3.reshape(b, s, c)


def _attn_kernel(*refs, n_q, n_pieces, ncomp, scale, post_scale):
    q_refs = refs[:n_q]
    pos = n_q
    pieces = []
    for _ in range(n_pieces):
        pieces.append((refs[pos:pos + n_q], refs[pos + n_q]))
        pos += n_q + 1
    if ncomp == 2:
        lam_ref, g_ref = refs[pos:pos + 2]
        pos += 2
    o_ref, kcat_ref, vcat_ref = refs[pos:pos + 3]
    dv = o_ref.shape[3]

    @pl.when(pl.program_id(2) == 0)
    def _():
        row = 0
        for k_refs, v_ref in pieces:
            n = v_ref.shape[2]
            for c in range(ncomp):
                parts = [k_ref[0, c if k_ref.shape[1] == ncomp else 0] for k_ref in k_refs]
                kcat_ref[c, row:row + n, :] = parts[0] if n_q == 1 else jnp.concatenate(parts, axis=1)
            vcat_ref[row:row + n, :dv] = v_ref[0, 0]
            vcat_ref[row:row + n, dv:] = jnp.ones((n, dv), BF16)
            row += n

    outs = []
    for c in range(ncomp):
        q = q_refs[0][0, c] if n_q == 1 else jnp.concatenate([q_ref[0, c] for q_ref in q_refs], axis=1)
        s = _dot_nt((q * (scale * math.log2(math.e))).astype(BF16), kcat_ref[c])
        m = jnp.max(s, axis=-1, keepdims=True)
        p = jnp.exp2((s - m).astype(BF16))
        ol = jnp.dot(p, vcat_ref[...], preferred_element_type=F32)
        outs.append(ol[:, :dv] / ol[:, dv:dv + 1])
    if ncomp == 2:
        o = outs[0] - lam_ref[0] * outs[1]
        o = _rms(o) * g_ref[...] * post_scale
    else:
        o = outs[0]
    o_ref[0, 0] = o


def _attention(q_parts, pieces, heads, ncomp, scale, tq=256, lam=None, subln_g=None, post_scale=1.0):
    b, _, sq, _ = q_parts[0].shape
    dv = pieces[0][1].shape[3]
    tq = min(tq, sq)
    ins = list(q_parts)
    in_specs = [pl.BlockSpec((1, ncomp, tq, q.shape[3]), lambda i, h, j: (i, h, j, 0)) for q in q_parts]
    for k_parts, v in pieces:
        for k in k_parts:
            ins.append(k)
            if k.shape[1] == 1:
                in_specs.append(pl.BlockSpec((1, 1) + k.shape[2:], lambda i, h, j: (i, 0, 0, 0)))
            else:
                in_specs.append(pl.BlockSpec((1, ncomp) + k.shape[2:], lambda i, h, j: (i, h, 0, 0)))
        ins.append(v)
        in_specs.append(pl.BlockSpec((1, 1) + v.shape[2:], lambda i, h, j: (i, h, 0, 0)))
    if ncomp == 2:
        ins += [lam.reshape(1), subln_g.reshape(1, dv)]
        in_specs += [pl.BlockSpec(memory_space=pltpu.SMEM), _const_spec((1, dv))]
    sk = sum(v.shape[2] for _, v in pieces)
    dqk = sum(q.shape[3] for q in q_parts)
    return pl.pallas_call(
        functools.partial(_attn_kernel, n_q=len(q_parts), n_pieces=len(pieces), ncomp=ncomp, scale=scale,
                          post_scale=post_scale),
        grid=(b, heads, sq // tq),
        in_specs=in_specs,
        out_specs=pl.BlockSpec((1, 1, tq, dv), lambda i, h, j: (i, h, j, 0)),
        out_shape=jax.ShapeDtypeStruct((b, heads, sq, dv), F32),
        scratch_shapes=[pltpu.VMEM((ncomp, sk, dqk), BF16), pltpu.VMEM((sk, 2 * dv), BF16)],
        compiler_params=_params('parallel', 'parallel', 'arbitrary'),
        name='attention',
    )(*ins)


def _forget_terms(f, log_lb, log_1m_lb, one_m_lb):
    log_sig = jnp.minimum(f, 0.0) - jnp.log1p(jnp.exp(-jnp.abs(f)))
    b = log_1m_lb + log_sig
    log_g = jnp.maximum(log_lb, b) + jnp.log1p(jnp.exp(-jnp.abs(log_lb - b)))
    return log_g, one_m_lb * jax.nn.sigmoid(-f)


def _hg_tables():
    ck, sub = HG_CHUNK, HG_SUB
    t = np.arange(ck)
    cum_mats, half_masks, group_masks, keeps = [], [], [], []
    for rev in (False, True):
        cum_mats.append((t[None, :] >= t[:, None]) if rev else (t[None, :] <= t[:, None]))
        halves = []
        hs = ck // 2
        while hs >= sub:
            pos = t % (2 * hs)
            q_half = (pos < hs) if rev else (pos >= hs)
            halves.append(np.stack([q_half, ~q_half]))
            if not rev:
                grp = (t[:, None] // (2 * hs)) == (t[None, :] // (2 * hs))
                group_masks.append(np.concatenate([grp, grp], axis=0))
            hs //= 2
        half_masks.append(np.stack(halves))
        c = np.arange(ck * sub)
        tt, ss = (c // sub) % sub, c % sub
        keeps.append((ss >= tt) if rev else (ss <= tt))
    lanes = 2 * HG_KEY_DIM
    ln = np.arange(lanes)
    bd = (ln[:, None] // HG_KEY_DIM) == (ln[None, :] // HG_KEY_DIM)
    hm = np.broadcast_to(np.stack(half_masks)[..., None], (2, len(half_masks[0]), 2, ck, lanes))
    return (jnp.asarray(np.stack(cum_mats), BF16), jnp.asarray(hm, F32), jnp.asarray(np.stack(group_masks), F32),
            jnp.asarray(bd, F32), jnp.asarray(np.broadcast_to(np.stack(keeps)[..., None], (2, ck * sub, lanes)), F32))


def _split2(x):
    a = x.astype(BF16)
    return a, (x - a.astype(F32)).astype(BF16)


def _hg_chunk(q, k, v, lg, st, rev, cm, hm, gm, keep, bd, m0, m1):
    ck, sub = HG_CHUNK, HG_SUB
    lanes = lg.shape[1]
    c2 = jnp.dot(cm, jnp.concatenate(_split2(lg), axis=1), preferred_element_type=F32)
    cum = c2[:, :lanes] + c2[:, lanes:]
    tot = cum[0:1] if rev else cum[ck - 1:ck]
    o = _dot_nt((q * jnp.exp(cum)).astype(BF16), st.astype(BF16))
    kd = (k * jnp.exp(tot - cum)).astype(BF16)
    st_new = st * jnp.exp(tot) + bd * jnp.dot(v.T.astype(BF16), kd, preferred_element_type=F32)
    s2 = None
    for lv in range(gm.shape[0]):
        hs = ck >> (lv + 1)
        edge = [(g + hs) if rev else (g + hs - 1) for g in range(0, ck, 2 * hs)]
        cb = jnp.concatenate([jnp.broadcast_to(cum[e:e + 1], (2 * hs, lanes)) for e in edge], axis=0)
        qd = q * jnp.exp(jnp.minimum(cum - cb, 0.0)) * hm[lv, 0]
        kf = (k * jnp.exp(jnp.minimum(cb - cum, 0.0)) * hm[lv, 1]).astype(BF16)
        q2 = jnp.concatenate([qd * m0, qd * m1], axis=0).astype(BF16)
        term = _dot_nt(q2, kf) * gm[lv]
        s2 = term if s2 is None else s2 + term
    r = jnp.dot(s2.astype(BF16), v.astype(BF16), preferred_element_type=F32)
    o = o + m0 * r[:ck] + m1 * r[ck:]
    rows, vts = [], []
    for i in range(ck // sub):
        lo, hi = i * sub, (i + 1) * sub
        ki, ci = k[lo:hi], cum[lo:hi]
        for t in range(lo, hi):
            rows.append(q[t:t + 1] * ki * jnp.exp(jnp.minimum(cum[t:t + 1] - ci, 0.0)))
            vts.append(v[lo:hi])
    sc = jnp.dot(jnp.concatenate(rows, axis=0).astype(BF16), bd.astype(BF16), preferred_element_type=F32)
    prod = sc * jnp.concatenate(vts, axis=0) * keep
    o = o + jnp.sum(prod.reshape(ck, sub, lanes), axis=1)
    return o, st_new


def _hgrn_kernel(q_ref, ff_ref, fb_ref, i_ref, qc_ref, ffc_ref, fbc_ref, ic_ref, lb_ref, g_ref,
                 cm_ref, hm_ref, gm_ref, bd_ref, keep_ref, o_ref, oc_ref, or_ref, ocr_ref, st_ref, *, n_lat, n_ctx):
    ck = HG_CHUNK
    lanes = o_ref.shape[-1]
    lane = lax.broadcasted_iota(jnp.int32, (1, lanes), 1)
    m0 = (lane < HG_KEY_DIM).astype(F32)
    m1 = 1.0 - m0
    bd = bd_ref[...]
    gm = gm_ref[...]

    def one(q, f, v, rev):
        d = 1 if rev else 0
        lg, k = _forget_terms(f, lb_ref[d, 0:1, :], lb_ref[d, 1:2, :], lb_ref[d, 2:3, :])
        o, st = _hg_chunk(q, k, v, lg, st_ref[d], rev, cm_ref[d], hm_ref[d], gm, keep_ref[d], bd, m0, m1)
        st_ref[d] = st
        return o

    def sweep(qr, ffr, fbr, ir, out_f, out_r, n):
        nc = n // ck

        def body(step, carry):
            idf = pl.ds(pl.multiple_of(step * ck, ck), ck)
            idr = pl.ds(pl.multiple_of((nc - 1 - step) * ck, ck), ck)
            out_f[0, idf, :] = one(qr[0, idf, :], ffr[0, idf, :], ir[0, idf, :], False)
            out_r[idr, :] = one(qr[0, idr, :], fbr[0, idr, :], ir[0, idr, :], True)
            return carry

        lax.fori_loop(0, nc, body, 0, unroll=HG_UNROLL)

    st_ref[...] = jnp.zeros(st_ref.shape, F32)
    sweep(qc_ref, ffc_ref, fbc_ref, ic_ref, oc_ref, ocr_ref, n_ctx)
    sweep(q_ref, ff_ref, fb_ref, i_ref, o_ref, or_ref, n_lat)

    mean_mat = bd * (1.0 / HG_VAL_DIM)

    def readout(out, out_r, n):
        tile = min(n, 512)

        def body(step, carry):
            idx = pl.ds(pl.multiple_of(step * tile, tile), tile)
            x = out[0, idx, :] + out_r[idx, :]
            ms = jnp.dot(x * x, mean_mat, precision=HIGHEST, preferred_element_type=F32)
            out[0, idx, :] = x * lax.rsqrt(ms + NORM_EPS) * g_ref[...]
            return carry

        lax.fori_loop(0, n // tile, body, 0)

    readout(oc_ref, ocr_ref, n_ctx)
    readout(o_ref, or_ref, n_lat)


def _hgrn(q, ff, fb, iv, qc, ffc, fbc, ic, lb_terms, norm_g):
    b, n_lat, width = q.shape
    n_ctx = qc.shape[1]
    lanes = 2 * HG_KEY_DIM
    tables = _hg_tables()
    lat = pl.BlockSpec((1, n_lat, lanes), lambda i, j: (i, 0, j))
    ctx = pl.BlockSpec((1, n_ctx, lanes), lambda i, j: (i, 0, j))
    g2 = jnp.tile(norm_g, 2).reshape(1, lanes)
    return pl.pallas_call(
        functools.partial(_hgrn_kernel, n_lat=n_lat, n_ctx=n_ctx),
        grid=(b, width // lanes),
        in_specs=[lat] * 4 + [ctx] * 4 + [pl.BlockSpec((2, 3, lanes), lambda i, j: (0, 0, j)), _const_spec((1, lanes))]
                 + [_const_spec(t.shape) for t in tables],
        out_specs=[lat, ctx],
        out_shape=[jax.ShapeDtypeStruct(q.shape, F32), jax.ShapeDtypeStruct(qc.shape, F32)],
        scratch_shapes=[pltpu.VMEM((n_lat, lanes), F32), pltpu.VMEM((n_ctx, lanes), F32), pltpu.VMEM((2, lanes, lanes), F32)],
        compiler_params=_params('parallel', 'parallel'),
        name='hgrn2',
    )(q, ff, fb, iv, qc, ffc, fbc, ic, lb_terms, g2, *tables)


def _out_proj_kernel(hy_ref, da_ref, mla_ref, hg_ref, gate_ref, x_ref, g1_ref, w_ref, o_ref):
    c = hy_ref.shape[2]
    acc = jnp.dot(hy_ref[0].astype(BF16), w_ref[0:c, :], preferred_element_type=F32)
    for i, head_ref in ((1, da_ref), (2, mla_ref)):
        dv = head_ref.shape[3]
        for h in range(head_ref.shape[1]):
            lo = i * c + h * dv
            acc = acc + jnp.dot(head_ref[0, h].astype(BF16), w_ref[lo:lo + dv, :], preferred_element_type=F32)
    hg = hg_ref[0] * _silu(gate_ref[0])
    acc = acc + jnp.dot(hg.astype(BF16), w_ref[3 * c:4 * c, :], preferred_element_type=F32)
    o_ref[0] = x_ref[0] + g1_ref[0] * acc


def _out_proj(y_hy, y_da, y_mla, y_hg, gate, x, g1, w_out, tm=512):
    b, s, d = x.shape
    tm = min(tm, s)
    c = y_hy.shape[2]
    part = pl.BlockSpec((1, tm, c), lambda i, j: (i, j, 0))
    headed = lambda a: pl.BlockSpec((1, a.shape[1], tm, a.shape[3]), lambda i, j: (i, 0, j, 0))
    row = pl.BlockSpec((1, tm, d), lambda i, j: (i, j, 0))
    return pl.pallas_call(
        _out_proj_kernel,
        grid=(b, s // tm),
        in_specs=[part, headed(y_da), headed(y_mla), part, part, row, pl.BlockSpec((1, 1, d), lambda i, j: (i, 0, 0)),
                  _const_spec(w_out.shape)],
        out_specs=row,
        out_shape=jax.ShapeDtypeStruct(x.shape, F32),
        compiler_params=_params('parallel', 'parallel'),
        name='out_proj',
    )(y_hy, y_da, y_mla, y_hg, gate, x, g1, w_out)


def _router_kernel(x_ref, g_ref, sc_ref, sh_ref, wrt_ref, bias_ref, *rest, compact):
    h = _rms(x_ref[0]) * g_ref[...] * (1.0 + sc_ref[0]) + sh_ref[0]
    tm = h.shape[0]
    scores = jax.nn.sigmoid(_dot_nt(wrt_ref[...], h, precision=HIGHEST))
    choice = scores + bias_ref[...]
    per = N_EXPERTS // N_EXPERT_GROUPS
    neg = -jnp.inf
    iota_g = lax.broadcasted_iota(jnp.int32, (per, tm), 0)
    grp_rows = []
    for gi in range(N_EXPERT_GROUPS):
        blk = choice[gi * per:(gi + 1) * per]
        m1 = jnp.max(blk, axis=0, keepdims=True)
        first = jnp.min(jnp.where(blk == m1, iota_g, per), axis=0, keepdims=True)
        m2 = jnp.max(jnp.where(iota_g == first, neg, blk), axis=0, keepdims=True)
        grp_rows.append(m1 + m2)
    grp = jnp.concatenate(grp_rows, axis=0)
    iota_n = lax.broadcasted_iota(jnp.int32, (N_EXPERT_GROUPS, tm), 0)
    gsel = jnp.zeros((N_EXPERT_GROUPS, tm), F32)
    for _ in range(TOPK_GROUPS):
        m = jnp.max(grp, axis=0, keepdims=True)
        first = jnp.min(jnp.where(grp == m, iota_n, N_EXPERT_GROUPS), axis=0, keepdims=True)
        hit = iota_n == first
        gsel = jnp.where(hit, 1.0, gsel)
        grp = jnp.where(hit, neg, grp)
    emask = jnp.concatenate([jnp.broadcast_to(gsel[gi:gi + 1], (per, tm)) for gi in range(N_EXPERT_GROUPS)], axis=0)
    cand = jnp.where(emask > 0.0, choice, neg)
    iota_e = lax.broadcasted_iota(jnp.int32, (N_EXPERTS, tm), 0)
    sel = jnp.zeros((N_EXPERTS, tm), F32)
    chosen = []
    for _ in range(TOP_K):
        m = jnp.max(cand, axis=0, keepdims=True)
        first = jnp.min(jnp.where(cand == m, iota_e, N_EXPERTS), axis=0, keepdims=True)
        hit = iota_e == first
        sel = jnp.where(hit, 1.0, sel)
        cand = jnp.where(hit, neg, cand)
        chosen.append(first)
    w = scores * sel
    gate = w / jnp.sum(w, axis=0, keepdims=True) * ROUTED_SCALE
    if not compact:
        h_ref, gate_ref = rest
        h_ref[0] = h.astype(BF16)
        gate_ref[0] = gate
        return
    hp_ref, eid_ref, rank_ref, w_ref, cnt_out_ref, cnt_ref = rest
    hp_ref[0] = _pack_halves(h)

    @pl.when((pl.program_id(0) == 0) & (pl.program_id(1) == 0))
    def _():
        cnt_ref[...] = jnp.zeros(cnt_ref.shape, F32)

    src = lax.broadcasted_iota(jnp.int32, (tm, tm), 0)
    dst = lax.broadcasted_iota(jnp.int32, (tm, tm), 1)
    running = jnp.dot(sel.astype(BF16), (src <= dst).astype(BF16), preferred_element_type=F32)
    rank_dense = cnt_ref[:, 0:1] + running - 1.0
    e_rows, r_rows, w_rows = [], [], []
    for first in chosen:
        hit = iota_e == first
        e_rows.append(first)
        r_rows.append(jnp.sum(jnp.where(hit, rank_dense, 0.0), axis=0, keepdims=True))
        w_rows.append(jnp.sum(jnp.where(hit, gate, 0.0), axis=0, keepdims=True))
    eid_ref[...] = jnp.concatenate(e_rows, axis=0)
    rank_ref[...] = jnp.concatenate(r_rows, axis=0).astype(jnp.int32)
    w_ref[...] = jnp.concatenate(w_rows, axis=0)
    cnt_ref[...] = cnt_ref[...] + running[:, tm - 1:tm]
    cnt_out_ref[...] = cnt_ref[...]


def _router(x, g, scale, shift, w_router, e_bias, tm=512, compact=False):
    b, s, d = x.shape
    tm = min(tm, s)
    e = w_router.shape[1]
    row = pl.BlockSpec((1, tm, d), lambda i, j: (i, j, 0))
    mod = pl.BlockSpec((1, 1, d), lambda i, j: (i, 0, 0))
    if compact:
        nj = s // tm
        tok = pl.BlockSpec((TOP_K, tm), lambda i, j: (0, i * nj + j))
        out_specs = [pl.BlockSpec((1, tm, d // 2), lambda i, j: (i, j, 0)), tok, tok, tok, _const_spec((e, LANES))]
        out_shape = [jax.ShapeDtypeStruct((b, s, d // 2), jnp.int32), jax.ShapeDtypeStruct((TOP_K, b * s), jnp.int32),
                     jax.ShapeDtypeStruct((TOP_K, b * s), jnp.int32), jax.ShapeDtypeStruct((TOP_K, b * s), F32),
                     jax.ShapeDtypeStruct((e, LANES), F32)]
        scratch = [pltpu.VMEM((e, LANES), F32)]
        semantics = ('arbitrary', 'arbitrary')
    else:
        out_specs = [row, pl.BlockSpec((1, e, tm), lambda i, j: (i, 0, j))]
        out_shape = [jax.ShapeDtypeStruct((b, s, d), BF16), jax.ShapeDtypeStruct((b, e, s), F32)]
        scratch = []
        semantics = ('parallel', 'parallel')
    return pl.pallas_call(
        functools.partial(_router_kernel, compact=compact),
        grid=(b, s // tm),
        in_specs=[row, _const_spec((1, d)), mod, mod, _const_spec((e, d)), _const_spec((e, 1))],
        out_specs=out_specs,
        out_shape=out_shape,
        scratch_shapes=scratch,
        compiler_params=_params(*semantics),
        name='router',
    )(x, g.reshape(1, d), scale, shift, w_router.T, e_bias.reshape(e, 1))


def _moe_kernel(h_ref, x_ref, gate_ref, g2_ref, wg_ref, wu_ref, wd_ref, sg_ref, su_ref, sd_ref, *rest, final):
    if final:
        fg_ref, o_ref, acc_ref = rest
    else:
        o_ref, acc_ref = rest
    e = pl.program_id(2)
    h = h_ref[0]

    @pl.when(e == 0)
    def _():
        a = jnp.dot(h, sg_ref[...], preferred_element_type=F32)
        u = jnp.dot(h, su_ref[...], preferred_element_type=F32)
        acc_ref[...] = jnp.dot((_silu(a) * u).astype(BF16), sd_ref[...], preferred_element_type=F32)

    lane = lax.broadcasted_iota(jnp.int32, gate_ref.shape[1:], 1)
    gcol = jnp.sum(jnp.where(lane == e, gate_ref[0], 0.0), axis=-1, keepdims=True)
    a = jnp.dot(h, wg_ref[0].astype(BF16), preferred_element_type=F32)
    u = jnp.dot(h, wu_ref[0].astype(BF16), preferred_element_type=F32)
    acc_ref[...] += jnp.dot((_silu(a) * u * gcol).astype(BF16), wd_ref[0].astype(BF16), preferred_element_type=F32)

    @pl.when(e == pl.num_programs(2) - 1)
    def _():
        y = x_ref[0] + g2_ref[0] * acc_ref[...]
        if final:
            y = _rms(y) * fg_ref[...]
        o_ref[0] = y


def _moe(h2, x, gate, g2, layer, w_gate, w_up, w_down, s_gate, s_up, s_down, final_g=None, tm=1024):
    b, s, d = x.shape
    tm = min(tm, s)
    _, e, _, ff = w_gate.shape
    row = pl.BlockSpec((1, tm, d), lambda i, j, k: (i, j, 0))
    ins = [h2, x, gate, g2, w_gate, w_up, w_down, s_gate, s_up, s_down]
    in_specs = [row, row, pl.BlockSpec((1, tm, e), lambda i, j, k: (i, j, 0)),
                pl.BlockSpec((1, 1, d), lambda i, j, k: (i, 0, 0)),
                pl.BlockSpec((None, 1, d, ff), lambda i, j, k: (layer, k, 0, 0)),
                pl.BlockSpec((None, 1, d, ff), lambda i, j, k: (layer, k, 0, 0)),
                pl.BlockSpec((None, 1, ff, d), lambda i, j, k: (layer, k, 0, 0)),
                _const_spec(s_gate.shape), _const_spec(s_up.shape), _const_spec(s_down.shape)]
    if final_g is not None:
        ins.append(final_g.reshape(1, d))
        in_specs.append(_const_spec((1, d)))
    return pl.pallas_call(
        functools.partial(_moe_kernel, final=final_g is not None),
        grid=(b, s // tm, e),
        in_specs=in_specs,
        out_specs=row,
        out_shape=jax.ShapeDtypeStruct(x.shape, F32),
        scratch_shapes=[pltpu.VMEM((tm, d), F32)],
        compiler_params=_params('parallel', 'parallel', 'arbitrary'),
        name='moe',
    )(*ins)


MOE_ROW_TILE = 512
SC_ROWS = 128
V7X_SC_CORES = 2
V7X_SC_SUBCORES = 16


def _pack_halves(x):
    n = x.shape[1] // 2
    lo = pltpu.bitcast(x[:, :n].astype(BF16).astype(F32), jnp.int32)
    hi = pltpu.bitcast(x[:, n:].astype(BF16).astype(F32), jnp.int32)
    return jnp.bitwise_or(jnp.bitwise_and(hi, -65536), lax.shift_right_logical(lo, 16))


def _unpack_halves(p):
    lo = pltpu.bitcast(lax.shift_left(p, 16), F32).astype(BF16)
    hi = pltpu.bitcast(jnp.bitwise_and(p, -65536), F32).astype(BF16)
    return lo, hi


def _route_pos_kernel(off_ref, eid_ref, rank_ref, pos_ref):
    eid = eid_ref[...]
    base = jnp.zeros(eid.shape, jnp.int32)
    for e in range(N_EXPERTS):
        base = jnp.where(eid == e, off_ref[e], base)
    pos_ref[...] = base + rank_ref[...]


def _route_pos(offsets, eid, rank):
    return pl.pallas_call(
        _route_pos_kernel,
        grid=(1,),
        in_specs=[pl.BlockSpec(memory_space=pltpu.SMEM), _const_spec(eid.shape), _const_spec(rank.shape)],
        out_specs=_const_spec(eid.shape),
        out_shape=jax.ShapeDtypeStruct(eid.shape, jnp.int32),
        compiler_params=_params('arbitrary'),
        name='route_pos',
    )(offsets, eid, rank)


def _sc_mesh():
    return plsc.VectorSubcoreMesh(core_axis_name='c', subcore_axis_name='s', num_cores=V7X_SC_CORES,
                                  num_subcores=V7X_SC_SUBCORES)


def _sc_dispatch(hp, pos, n_rows):
    t, w = hp.shape
    k = pos.shape[0]
    workers = V7X_SC_CORES * V7X_SC_SUBCORES
    per_worker = t // workers
    pos_flat = pos.reshape(k * t)

    @functools.partial(pl.kernel, mesh=_sc_mesh(), out_type=jax.ShapeDtypeStruct((n_rows, w), jnp.int32),
                       scratch_types=[pltpu.VMEM((SC_ROWS,), jnp.int32), pltpu.VMEM((SC_ROWS, w), jnp.int32),
                                      pltpu.SemaphoreType.DMA])
    def scatter(hp_hbm, pos_hbm, out_hbm, idx_v, rows_v, sem):
        wid = lax.axis_index('s') * V7X_SC_CORES + lax.axis_index('c')

        @pl.loop(0, per_worker // SC_ROWS)
        def _(i):
            t0 = pl.multiple_of(wid * per_worker + i * SC_ROWS, SC_ROWS)
            pltpu.sync_copy(hp_hbm.at[pl.ds(t0, SC_ROWS)], rows_v)
            for j in range(k):
                pltpu.sync_copy(pos_hbm.at[pl.ds(pl.multiple_of(j * t + t0, SC_ROWS), SC_ROWS)], idx_v)
                pltpu.async_copy(rows_v, out_hbm.at[idx_v], sem).wait()

    return scatter(hp, pos_flat)


def _sc_collect(yp, pos):
    _, w = yp.shape
    k, t = pos.shape
    workers = V7X_SC_CORES * V7X_SC_SUBCORES
    per_worker = k * t // workers
    pos_flat = pos.reshape(k * t)

    @functools.partial(pl.kernel, mesh=_sc_mesh(), out_type=jax.ShapeDtypeStruct((k * t, w), jnp.int32),
                       scratch_types=[pltpu.VMEM((SC_ROWS,), jnp.int32), pltpu.VMEM((SC_ROWS, w), jnp.int32),
                                      pltpu.SemaphoreType.DMA])
    def gather(yp_hbm, pos_hbm, out_hbm, idx_v, rows_v, sem):
        wid = lax.axis_index('s') * V7X_SC_CORES + lax.axis_index('c')

        @pl.loop(0, per_worker // SC_ROWS)
        def _(i):
            r0 = pl.multiple_of(wid * per_worker + i * SC_ROWS, SC_ROWS)
            pltpu.sync_copy(pos_hbm.at[pl.ds(r0, SC_ROWS)], idx_v)
            pltpu.async_copy(yp_hbm.at[idx_v], rows_v, sem).wait()
            pltpu.sync_copy(rows_v, out_hbm.at[pl.ds(r0, SC_ROWS)])

    return gather(yp, pos_flat)


def _expert_kernel(te_ref, nu_ref, x_ref, wg_ref, wu_ref, wd_ref, o_ref, wg_s, wu_s, wd_s):
    i = pl.program_id(0)

    @pl.when(i < nu_ref[0])
    def _():
        @pl.when((i == 0) | (te_ref[i] != te_ref[jnp.maximum(i - 1, 0)]))
        def _():
            wg_s[...] = wg_ref[0].astype(BF16)
            wu_s[...] = wu_ref[0].astype(BF16)
            wd_s[...] = wd_ref[0].astype(BF16)

        x = jnp.concatenate(_unpack_halves(x_ref[...]), axis=1)
        a = jnp.dot(x, wg_s[...], preferred_element_type=F32)
        u = jnp.dot(x, wu_s[...], preferred_element_type=F32)
        y = jnp.dot((_silu(a) * u).astype(BF16), wd_s[...], preferred_element_type=F32)
        o_ref[...] = _pack_halves(y)


def _experts(xp, tile_expert, n_used, layer, w_gate, w_up, w_down):
    n_rows, half = xp.shape
    _, _, d, ff = w_gate.shape
    r = MOE_ROW_TILE
    row = pl.BlockSpec((r, half), lambda i, te, nu: (i, 0))
    grid_spec = pltpu.PrefetchScalarGridSpec(
        num_scalar_prefetch=2,
        grid=(n_rows // r,),
        in_specs=[row,
                  pl.BlockSpec((None, 1, d, ff), lambda i, te, nu: (layer, te[i], 0, 0)),
                  pl.BlockSpec((None, 1, d, ff), lambda i, te, nu: (layer, te[i], 0, 0)),
                  pl.BlockSpec((None, 1, ff, d), lambda i, te, nu: (layer, te[i], 0, 0))],
        out_specs=row,
        scratch_shapes=[pltpu.VMEM((d, ff), BF16), pltpu.VMEM((d, ff), BF16), pltpu.VMEM((ff, d), BF16)],
    )
    return pl.pallas_call(
        _expert_kernel,
        grid_spec=grid_spec,
        out_shape=jax.ShapeDtypeStruct((n_rows, half), jnp.int32),
        compiler_params=_params('arbitrary'),
        name='experts',
    )(tile_expert, n_used, xp, w_gate, w_up, w_down)


def _combine_kernel(yg_ref, w_ref, hp_ref, x_ref, g2_ref, sg_ref, su_ref, sd_ref, *rest, final):
    if final:
        fg_ref, o_ref = rest
    else:
        (o_ref,) = rest
    half = hp_ref.shape[2]
    h = jnp.concatenate(_unpack_halves(hp_ref[0]), axis=1)
    a = jnp.dot(h, sg_ref[...], preferred_element_type=F32)
    u = jnp.dot(h, su_ref[...], preferred_element_type=F32)
    acc = jnp.dot((_silu(a) * u).astype(BF16), sd_ref[...], preferred_element_type=F32)
    acc_lo, acc_hi = acc[:, :half], acc[:, half:]
    wts = w_ref[0]
    for k in range(yg_ref.shape[0]):
        ylo, yhi = _unpack_halves(yg_ref[k, 0])
        wk = wts[:, k:k + 1]
        acc_lo = acc_lo + wk * ylo.astype(F32)
        acc_hi = acc_hi + wk * yhi.astype(F32)
    y = x_ref[0] + g2_ref[0] * jnp.concatenate([acc_lo, acc_hi], axis=1)
    if final:
        y = _rms(y) * fg_ref[...]
    o_ref[0] = y


def _combine(yg, wts, hp, x, g2, s_gate, s_up, s_down, final_g=None, tm=256):
    b, s, d = x.shape
    k = yg.shape[0]
    half = d // 2
    row = pl.BlockSpec((1, tm, d), lambda i, j: (i, j, 0))
    prow = pl.BlockSpec((1, tm, half), lambda i, j: (i, j, 0))
    ins = [yg, wts, hp, x, g2, s_gate, s_up, s_down]
    in_specs = [pl.BlockSpec((k, 1, tm, half), lambda i, j: (0, i, j, 0)), pl.BlockSpec((1, tm, k), lambda i, j: (i, j, 0)),
                prow, row, pl.BlockSpec((1, 1, d), lambda i, j: (i, 0, 0)),
                _const_spec(s_gate.shape), _const_spec(s_up.shape), _const_spec(s_down.shape)]
    if final_g is not None:
        ins.append(final_g.reshape(1, d))
        in_specs.append(_const_spec((1, d)))
    return pl.pallas_call(
        functools.partial(_combine_kernel, final=final_g is not None),
        grid=(b, s // tm),
        in_specs=in_specs,
        out_specs=row,
        out_shape=jax.ShapeDtypeStruct(x.shape, F32),
        compiler_params=_params('parallel', 'parallel'),
        name='moe_combine',
    )(*ins)


def _routed_moe(x, g, scale, shift, g2, w_router, e_bias, layer, w_gate, w_up, w_down, s_gate, s_up, s_down, final_g=None):
    b, s, d = x.shape
    t = b * s
    hp, eid, rank, wts, counts = _router(x, g, scale, shift, w_router, e_bias, compact=True)
    counts = counts[:, 0].astype(jnp.int32)
    r = MOE_ROW_TILE
    padded = (counts + (r - 1)) // r * r
    ends = jnp.cumsum(padded)
    offsets = ends - padded
    n_rows = t * TOP_K + N_EXPERTS * r
    tile_start = jnp.arange(n_rows // r, dtype=jnp.int32) * r
    tile_expert = jnp.minimum(jnp.sum((tile_start[:, None] >= ends[None, :]).astype(jnp.int32), axis=1), N_EXPERTS - 1)
    n_used = (ends[-1] // r).reshape(1).astype(jnp.int32)
    pos = _route_pos(offsets.astype(jnp.int32), eid, rank)
    xp = _sc_dispatch(hp.reshape(t, d // 2), pos, n_rows)
    yp = _experts(xp, tile_expert.astype(jnp.int32), n_used, layer, w_gate, w_up, w_down)
    yg = _sc_collect(yp, pos).reshape(TOP_K, b, s, d // 2)
    return _combine(yg, wts.T.reshape(b, s, TOP_K), hp, x, g2, s_gate, s_up, s_down, final_g)


def _mixers(p, pc, ctx_out, prm, l, lam_init, rope_tabs, lb_terms):
    s = p['hy_v'].shape[1]
    sc = pc['hy_v'].shape[1]

    hy_args = (prm['hy_w1'][l], prm['hy_b1'][l], prm['hy_w2'][l], prm['hy_b2'][l], prm['hy_w3'][l], prm['hy_b3'][l],
               prm['hy_sin_freq'][l], prm['hy_decay'][l])
    y_hy = _hyena([p['hy_v'], p['hy_x1'], p['hy_x2']], prm['hy_conv_w'][l], prm['hy_conv_b'][l],
                  _hy_filters(s, *hy_args), prm['hy_bias'][l], inner=128)
    yc_hy = None
    if ctx_out:
        yc_hy = _hyena([pc['hy_v'], pc['hy_x1'], pc['hy_x2']], prm['hy_conv_w'][l], prm['hy_conv_b'][l],
                       _hy_filters(sc, *hy_args), prm['hy_bias'][l], inner=32)

    lp = prm['da_lambda'][l].astype(F32)
    lam = jnp.exp(jnp.sum(lp[0] * lp[1])) - jnp.exp(jnp.sum(lp[2] * lp[3])) + lam_init
    da_kw = dict(heads=DA_HEADS, ncomp=2, scale=DA_HEAD_DIM ** -0.5, lam=lam, subln_g=prm['da_subln_g'][l],
                 post_scale=1.0 - lam_init)
    da_ctx = ([pc['da_k']], pc['da_v'])
    y_da = _attention([p['da_q']], [da_ctx, ([p['da_k']], p['da_v'])], **da_kw)
    yc_da = _attention([pc['da_q']], [da_ctx], **da_kw) if ctx_out else None

    wq = prm['mla_w_q_up'][l].reshape(MLA_Q_RANK, MLA_HEADS, MLA_NOPE_DIM + MLA_ROPE_DIM)
    wq_n = wq[:, :, :MLA_NOPE_DIM].reshape(MLA_Q_RANK, -1).astype(BF16)
    wq_r = wq[:, :, MLA_NOPE_DIM:].reshape(MLA_Q_RANK, -1).astype(BF16)
    wkv = prm['mla_w_kv_up'][l].reshape(MLA_KV_RANK, MLA_HEADS, MLA_NOPE_DIM + MLA_V_DIM)
    wkv_n = wkv[:, :, :MLA_NOPE_DIM].reshape(MLA_KV_RANK, -1).astype(BF16)
    wkv_v = wkv[:, :, MLA_NOPE_DIM:].reshape(MLA_KV_RANK, -1).astype(BF16)

    def queries(qd, tabs):
        return _norm_proj(qd, prm['mla_q_norm_g'][l], [(wq_n, F32, False, MLA_HEADS), (wq_r, F32, True, MLA_HEADS)],
                          rope_tabs=tabs)

    def keys_values(kvd):
        return _norm_proj(kvd, prm['mla_kv_norm_g'][l], [(wkv_n, BF16, False, MLA_HEADS), (wkv_v, BF16, False, MLA_HEADS)])

    kn_l, v_l = keys_values(p['mla_kv'])
    kn_c, v_c = keys_values(pc['mla_kv'])
    mla_kw = dict(heads=MLA_HEADS, ncomp=1, scale=(MLA_NOPE_DIM + MLA_ROPE_DIM) ** -0.5)
    mla_ctx = ([kn_c, pc['mla_kr']], v_c)
    y_mla = _attention(queries(p['mla_q'], rope_tabs), [mla_ctx, ([kn_l, p['mla_kr']], v_l)], **mla_kw)
    yc_mla = _attention(queries(pc['mla_q'], None), [mla_ctx], **mla_kw) if ctx_out else None

    o, oc = _hgrn(p['hg_q'], p['hg_ff'], p['hg_fb'], p['hg_i'], pc['hg_q'], pc['hg_ff'], pc['hg_fb'], pc['hg_i'],
                  lb_terms, prm['hg_norm_g'][l])
    return (y_hy, y_da, y_mla, o), (yc_hy, yc_da, yc_mla, oc)


def kernel(x, c, ctx, c_ctx, w_ada, b_ada, norm1_g, norm2_g, w_in, w_out, hy_conv_w, hy_conv_b, hy_w1, hy_b1, hy_w2, hy_b2, hy_w3, hy_b3, hy_sin_freq, hy_decay, hy_bias, da_lambda, da_subln_g, mla_q_norm_g, mla_w_q_up, mla_kv_norm_g, mla_w_kv_up, hg_lower_bounds, hg_norm_g, moe_w_router, moe_bias, moe_w_gate, moe_w_up, moe_w_down, moe_sh_gate, moe_sh_up, moe_sh_down, final_norm_g):
    prm = dict(hy_conv_w=hy_conv_w, hy_conv_b=hy_conv_b, hy_w1=hy_w1, hy_b1=hy_b1, hy_w2=hy_w2, hy_b2=hy_b2,
               hy_w3=hy_w3, hy_b3=hy_b3, hy_sin_freq=hy_sin_freq, hy_decay=hy_decay, hy_bias=hy_bias,
               da_lambda=da_lambda, da_subln_g=da_subln_g, mla_q_norm_g=mla_q_norm_g, mla_w_q_up=mla_w_q_up,
               mla_kv_norm_g=mla_kv_norm_g, mla_w_kv_up=mla_w_kv_up, hg_norm_g=hg_norm_g)
    b, n_lat, d = x.shape
    depth = w_in.shape[0]
    rows = n_lat // GRID_W
    row_pos = jnp.repeat(jnp.arange(rows, dtype=jnp.int32), GRID_W)
    col_pos = jnp.tile(jnp.arange(GRID_W, dtype=jnp.int32), rows)
    rope_tabs = _rope_tables(row_pos, col_pos, 2 * DA_HEADS * DA_HEAD_DIM)
    lbs = jnp.cumsum(jax.nn.softmax(hg_lower_bounds.astype(F32), axis=1), axis=1)
    lbs = lbs - lbs[:, :1]
    cond = jnp.concatenate([c, c_ctx[None], jnp.zeros((8 - b - 1, d), F32)], axis=0)

    for l in range(depth):
        ctx_out = l < depth - 1
        mods = _ada(cond, w_ada[l], b_ada[l])
        sh1, sc1, g1, sh2, sc2, g2 = [m[:, None, :] for m in jnp.split(mods[:b], 6, axis=-1)]
        mc = [jnp.broadcast_to(m[:, None, :], (b, 1, d)) for m in jnp.split(mods[b:b + 1], 6, axis=-1)]

        off = 0
        outs = []
        for _, wdt, dt, rope, split, rep in _SEGMENTS:
            w = w_in[l][:, off:off + wdt].astype(BF16)
            outs.append((jnp.tile(w, (1, rep)) if rep > 1 else w, dt, rope, split))
            off += wdt
        names = [seg[0] for seg in _SEGMENTS]
        p = dict(zip(names, _norm_proj(x, norm1_g[l], outs, sc1, sh1, rope_tabs=rope_tabs)))
        pc = dict(zip(names, _norm_proj(ctx, norm1_g[l], outs, mc[1], mc[0])))

        lb = lbs[:, l]
        lb_terms = jnp.stack([jnp.log(lb), jnp.log1p(-lb), 1.0 - lb], axis=1)
        lam_init = 0.8 - 0.6 * math.exp(-0.3 * l)
        lat_parts, ctx_parts = _mixers(p, pc, ctx_out, prm, l, lam_init, rope_tabs, lb_terms)

        w_out_b = w_out[l].astype(BF16)
        moe_w = (l, moe_w_gate, moe_w_up, moe_w_down,
                 moe_sh_gate[l].astype(BF16), moe_sh_up[l].astype(BF16), moe_sh_down[l].astype(BF16))

        if ctx_out:
            ctx = _out_proj(*ctx_parts, pc['hg_g'], ctx, mc[2], w_out_b)
            flat = ctx.reshape(1, -1, d)
            h2c, gate_c = _router(flat, norm2_g[l], mc[4][:1], mc[3][:1], moe_w_router[l], moe_bias[l])
            ctx = _moe(h2c, flat, gate_c.transpose(0, 2, 1), mc[5][:1], *moe_w).reshape(ctx.shape)

        x = _out_proj(*lat_parts, p['hg_g'], x, g1, w_out_b)
        x = _routed_moe(x, norm2_g[l], sc2, sh2, g2, moe_w_router[l], moe_bias[l], *moe_w,
                        final_g=None if ctx_out else final_norm_g)

    return x
```

```python
import functools
import math

import numpy as np
import jax
import jax.numpy as jnp
from jax import lax
from jax.experimental import pallas as pl
from jax.experimental.pallas import tpu as pltpu
from jax.experimental.pallas import tpu_sc as plsc

F32 = jnp.float32
BF16 = jnp.bfloat16
HIGHEST = lax.Precision.HIGHEST

GRID_W = 64
HY_WIDTH = 256
HY_ORDER = 2
HY_BANDS = 16
DA_HEADS = 4
DA_HEAD_DIM = 32
MLA_HEADS = 4
MLA_Q_RANK = 192
MLA_KV_RANK = 128
MLA_NOPE_DIM = 64
MLA_ROPE_DIM = 32
MLA_V_DIM = 64
HG_HEADS = 4
HG_KEY_DIM = 64
HG_VAL_DIM = 64
HG_CHUNK = 64
HG_SUB = 8
HG_UNROLL = 4
N_EXPERTS = 64
N_EXPERT_GROUPS = 8
TOPK_GROUPS = 4
TOP_K = 8
ROUTED_SCALE = 2.5
ROPE_BASE = 10000.0
NORM_EPS = 1e-6

V7X_VMEM_LIMIT_BYTES = 56 * 1024 * 1024
LANES = 128

_SEGMENTS = (
    ('hy_v', HY_WIDTH, F32, False, 0, 1), ('hy_x1', HY_WIDTH, F32, False, 0, 1), ('hy_x2', HY_WIDTH, F32, False, 0, 1),
    ('da_q', 2 * DA_HEADS * DA_HEAD_DIM, F32, True, 2 * DA_HEADS, 1),
    ('da_k', 2 * DA_HEADS * DA_HEAD_DIM, BF16, True, 2 * DA_HEADS, 1),
    ('da_v', 2 * DA_HEADS * DA_HEAD_DIM, BF16, False, DA_HEADS, 1),
    ('mla_q', MLA_Q_RANK, F32, False, 0, 1), ('mla_kv', MLA_KV_RANK, F32, False, 0, 1),
    ('mla_kr', MLA_ROPE_DIM, BF16, True, MLA_HEADS, MLA_HEADS),
    ('hg_q', HG_HEADS * HG_KEY_DIM, F32, False, 0, 1), ('hg_ff', HG_HEADS * HG_KEY_DIM, F32, False, 0, 1),
    ('hg_fb', HG_HEADS * HG_KEY_DIM, F32, False, 0, 1), ('hg_i', HG_HEADS * HG_VAL_DIM, F32, False, 0, 1),
    ('hg_g', HG_HEADS * HG_VAL_DIM, F32, False, 0, 1),
)


def _params(*semantics):
    return pltpu.CompilerParams(dimension_semantics=semantics, vmem_limit_bytes=V7X_VMEM_LIMIT_BYTES)


def _const_spec(shape):
    nd = len(shape)
    return pl.BlockSpec(shape, lambda *_: (0,) * nd)


def _rms(x, eps=NORM_EPS):
    return x * lax.rsqrt(jnp.mean(x * x, axis=-1, keepdims=True) + eps)


def _silu(x):
    return x * jax.nn.sigmoid(x)


def _dot_nt(a, b, **kw):
    return lax.dot_general(a, b, (((1,), (1,)), ((), ())), preferred_element_type=F32, **kw)


def _ada_kernel(c_ref, w_ref, b_ref, o_ref):
    s = _silu(c_ref[...])
    o_ref[...] = jnp.dot(s, w_ref[...], precision=HIGHEST, preferred_element_type=F32) + b_ref[...]


def _ada(cond, w, b):
    r, d = cond.shape
    n = w.shape[1]
    tn = 1536
    return pl.pallas_call(
        _ada_kernel,
        grid=(n // tn,),
        in_specs=[_const_spec((r, d)), pl.BlockSpec((d, tn), lambda j: (0, j)), pl.BlockSpec((1, tn), lambda j: (0, j))],
        out_specs=pl.BlockSpec((r, tn), lambda j: (0, j)),
        out_shape=jax.ShapeDtypeStruct((r, n), F32),
        compiler_params=_params('arbitrary'),
        name='ada',
    )(cond, w, b.reshape(1, n))


ROPE_UNIT = 32


def _rope_tables(row, col, width):
    n = ROPE_UNIT // 4
    inv = ROPE_BASE ** (-jnp.arange(n, dtype=F32) / n)
    units = width // ROPE_UNIT
    parts_c, parts_a, parts_b = [], [], []
    zero = jnp.zeros((row.shape[0], n), F32)
    for pos in (row, col):
        ang = pos.astype(F32)[:, None] * inv
        cos, sin = jnp.cos(ang), jnp.sin(ang)
        parts_c += [cos, cos]
        parts_a += [zero, sin]
        parts_b += [-sin, zero]
    tile = lambda ps: jnp.tile(jnp.concatenate(ps, axis=1), (1, units))
    return tile(parts_c), tile(parts_a), tile(parts_b)


def _norm_proj_kernel(*refs, n_w, modulate, ropes, splits):
    x_ref, g_ref = refs[0], refs[1]
    pos = 2
    if modulate:
        sc_ref, sh_ref = refs[2], refs[3]
        pos = 4
    if any(ropes):
        rc_ref, ra_ref, rb_ref = refs[pos:pos + 3]
        pos += 3
    w_refs = refs[pos:pos + n_w]
    o_refs = refs[pos + n_w:]
    y = _rms(x_ref[0]) * g_ref[...]
    if modulate:
        y = y * (1.0 + sc_ref[0]) + sh_ref[0]
    yb = y.astype(BF16)
    for w_ref, o_ref, rope, split in zip(w_refs, o_refs, ropes, splits):
        o = jnp.dot(yb, w_ref[...], preferred_element_type=F32)
        if rope:
            wd = o.shape[1]
            shift = ROPE_UNIT // 4
            o = (o * rc_ref[:, :wd] + pltpu.roll(o, shift, axis=1) * ra_ref[:, :wd]
                 + pltpu.roll(o, wd - shift, axis=1) * rb_ref[:, :wd])
        if split:
            unit = o.shape[1] // split
            for u in range(split):
                o_ref[0, u] = o[:, u * unit:(u + 1) * unit].astype(o_ref.dtype)
        else:
            o_ref[0] = o.astype(o_ref.dtype)


def _norm_proj(x, g, outs, scale=None, shift=None, rope_tabs=None, tm=512):
    b, s, k = x.shape
    tm = min(tm, s)
    modulate = scale is not None
    ropes = tuple(bool(o[2]) and rope_tabs is not None for o in outs)
    splits = tuple(o[3] for o in outs)
    ins = [x, g.reshape(1, k)]
    in_specs = [pl.BlockSpec((1, tm, k), lambda i, j: (i, j, 0)), _const_spec((1, k))]
    if modulate:
        ins += [scale, shift]
        in_specs += [pl.BlockSpec((1, 1, k), lambda i, j: (i, 0, 0))] * 2
    if any(ropes):
        ins += list(rope_tabs)
        in_specs += [pl.BlockSpec((tm, rope_tabs[0].shape[1]), lambda i, j: (j, 0))] * 3
    out_specs, out_shape = [], []
    for w, dt, _, split in outs:
        ins.append(w)
        in_specs.append(_const_spec(w.shape))
        n = w.shape[1]
        if split:
            out_specs.append(pl.BlockSpec((1, split, tm, n // split), lambda i, j: (i, 0, j, 0)))
            out_shape.append(jax.ShapeDtypeStruct((b, split, s, n // split), dt))
        else:
            out_specs.append(pl.BlockSpec((1, tm, n), lambda i, j: (i, j, 0)))
            out_shape.append(jax.ShapeDtypeStruct((b, s, n), dt))
    return pl.pallas_call(
        functools.partial(_norm_proj_kernel, n_w=len(outs), modulate=modulate, ropes=ropes, splits=splits),
        grid=(b, s // tm),
        in_specs=in_specs,
        out_specs=out_specs,
        out_shape=out_shape,
        compiler_params=_params('parallel', 'parallel'),
        name='norm_proj',
    )(*ins)


def _hy_filter_kernel(w1t_ref, w1s_ref, w1c_ref, b1_ref, w2_ref, b2_ref, w3_ref, b3_ref, fr_ref, dec_ref, o_ref, *, n):
    t = lax.broadcasted_iota(jnp.int32, (n, 1), 0).astype(F32) / n
    bands = lax.broadcasted_iota(jnp.int32, (1, HY_BANDS), 1).astype(F32) + 1.0
    ang = (2.0 * jnp.pi) * t * bands
    pre = (t * w1t_ref[...]
           + jnp.dot(jnp.sin(ang), w1s_ref[...], precision=HIGHEST, preferred_element_type=F32)
           + jnp.dot(jnp.cos(ang), w1c_ref[...], precision=HIGHEST, preferred_element_type=F32)
           + b1_ref[...])
    hid = jnp.sin(fr_ref[0:1, :] * pre)
    hid = jnp.sin(fr_ref[1:2, :] * (jnp.dot(hid, w2_ref[...], precision=HIGHEST, preferred_element_type=F32) + b2_ref[...]))
    filt = jnp.dot(hid, w3_ref[...], precision=HIGHEST, preferred_element_type=F32) + b3_ref[...]
    filt = filt * jnp.exp(-t * jnp.abs(dec_ref[...]))
    col = jnp.sum(jnp.abs(filt), axis=0, keepdims=True) - jnp.abs(filt[0:1, :])
    w = HY_WIDTH
    for o in range(HY_ORDER):
        lo = o * 2 * w
        f0 = filt[0:1, lo:lo + w] + filt[0:1, lo + w:lo + 2 * w]
        inv = 1.0 / (col[:, lo:lo + w] + col[:, lo + w:lo + 2 * w] + jnp.abs(f0))
        o_ref[:, lo:lo + w] = filt[:, lo:lo + w] * inv
        o_ref[:, lo + w:lo + 2 * w] = filt[:, lo + w:lo + 2 * w] * inv


def _hy_filters(n, w1, b1, w2, b2, w3, b3, freq, decay):
    cols = w3.shape[1]
    ins = [w1[0:1], w1[1:1 + HY_BANDS], w1[1 + HY_BANDS:], b1.reshape(1, -1), w2, b2.reshape(1, -1), w3,
           b3.reshape(1, -1), freq, decay.reshape(1, -1)]
    out = pl.pallas_call(
        functools.partial(_hy_filter_kernel, n=n),
        grid=(1,),
        in_specs=[_const_spec(a.shape) for a in ins],
        out_specs=_const_spec((n, cols)),
        out_shape=jax.ShapeDtypeStruct((n, cols), F32),
        compiler_params=_params('arbitrary'),
        name='hy_filter',
    )(*ins)
    return out.reshape(n, HY_ORDER, 2, HY_WIDTH)


def _two_sided(filt_n):
    n = filt_n.shape[0]
    hf, hb = filt_n[:, :, 0], filt_n[:, :, 1]
    h = jnp.concatenate([hf[:1] + hb[:1], hf[1:], jnp.zeros((1,) + hf.shape[1:], F32), hb[:0:-1]], axis=0)
    return h.reshape(2 * n, HY_ORDER * HY_WIDTH)


def _short_conv_kernel(*refs, s):
    x_refs, w_refs, b_refs, o_refs = refs[0:3], refs[3:6], refs[6:9], refs[9:12]
    row = lax.broadcasted_iota(jnp.int32, (s, 1), 0)
    for x_ref, w_ref, b_ref, o_ref in zip(x_refs, w_refs, b_refs, o_refs):
        x = x_ref[0]
        prev = jnp.where(row == 0, 0.0, pltpu.roll(x, 1, axis=0))
        nxt = jnp.where(row == s - 1, 0.0, pltpu.roll(x, s - 1, axis=0))
        o_ref[0] = prev * w_ref[0:1, :] + x * w_ref[1:2, :] + nxt * w_ref[2:3, :] + b_ref[...]


def _short_conv(parts, conv_w, conv_b):
    b, s, c = parts[0].shape
    tc = LANES
    ws = [conv_w[:, i * c:(i + 1) * c] for i in range(3)]
    bs = [conv_b[i * c:(i + 1) * c].reshape(1, c) for i in range(3)]
    xspec = pl.BlockSpec((1, s, tc), lambda i, j: (i, 0, j))
    return pl.pallas_call(
        functools.partial(_short_conv_kernel, s=s),
        grid=(b, c // tc),
        in_specs=[xspec] * 3 + [pl.BlockSpec((3, tc), lambda i, j: (0, j))] * 3 + [pl.BlockSpec((1, tc), lambda i, j: (0, j))] * 3,
        out_specs=[xspec] * 3,
        out_shape=[jax.ShapeDtypeStruct((b, s, c), F32)] * 3,
        compiler_params=_params('parallel', 'parallel'),
        name='short_conv',
    )(*parts, *ws, *bs)


def _dft_cos_sin(rows, cols, period):
    ang = 2.0 * np.pi * ((np.arange(rows)[:, None] * np.arange(cols)[None, :]) % period) / period
    return np.cos(ang), np.sin(ang)


def _fft_tables(n, inner):
    big = 2 * n
    n1 = big // inner
    c1, s1 = _dft_cos_sin(n1, n1, n1)
    h = n1 // 2
    outer_data = np.block([[c1[:, :h], s1[:, :h]], [-s1[:, :h], c1[:, :h]]])
    outer_real = np.concatenate([c1, -s1], axis=0)
    outer_inv = np.block([[c1[:h, :], -s1[:h, :]], [s1[:h, :], c1[:h, :]]]) / big
    c2, s2 = _dft_cos_sin(inner, inner, inner)
    inner_fwd = np.block([[c2, s2], [-s2, c2]])
    inner_inv = np.block([[c2, -s2], [s2, c2]])
    ct, st = _dft_cos_sin(n1, inner, big)
    f = lambda a: jnp.asarray(a, F32)
    return dict(n1=n1, inner=inner, outer_data=_hi_lo_cols(outer_data), outer_real=_hi_lo_cols(outer_real),
                outer_inv=_hi_lo_cols(outer_inv),
                inner_fwd=_hi_lo_cols(inner_fwd), inner_inv=_hi_lo_cols(inner_inv),
                tw_cos=f(ct).reshape(n1, inner, 1), tw_sin=f(st).reshape(n1, inner, 1))


def _left_mm_kernel(m_ref, x_ref, o_ref):
    o_ref[0] = jnp.dot(m_ref[...], _hi_lo_rows(x_ref[0]), preferred_element_type=F32)


def _left_mm(m, x, tl=4096):
    p, k, l = x.shape
    mm = m.shape[0]
    tl = min(tl, l)
    return pl.pallas_call(
        _left_mm_kernel,
        grid=(p, l // tl),
        in_specs=[_const_spec(m.shape), pl.BlockSpec((1, k, tl), lambda i, j: (i, 0, j))],
        out_specs=pl.BlockSpec((1, mm, tl), lambda i, j: (i, 0, j)),
        out_shape=jax.ShapeDtypeStruct((p, mm, l), F32),
        compiler_params=_params('parallel', 'parallel'),
        name='fft_outer',
    )(m, x)


def _hi_lo_cols(m):
    m = np.asarray(m, np.float32)
    hi = m.astype(BF16)
    lo = (m - hi.astype(np.float32)).astype(BF16)
    return jnp.asarray(np.concatenate([hi, hi, lo], axis=1))


def _hi_lo_rows(x):
    hi = x.astype(BF16)
    lo = (x - hi.astype(F32)).astype(BF16)
    return jnp.concatenate([hi, lo, hi], axis=0)


def _inner_kernel(a_ref, twc_ref, tws_ref, gf_ref, *rest, convolve, inner, kb):
    for s in range(kb):
        ar, ai = a_ref[0, 0, s], a_ref[0, 1, s]
        tc, ts = twc_ref[s], tws_ref[s]
        br = ar * tc + ai * ts
        bi = ai * tc - ar * ts
        x = jnp.dot(gf_ref[...], _hi_lo_rows(jnp.concatenate([br, bi], axis=0)), preferred_element_type=F32)
        if not convolve:
            o_ref = rest[0]
            o_ref[0, 0, s] = x[:inner]
            o_ref[0, 1, s] = x[inner:]
            continue
        h_ref, gi_ref, o_ref = rest
        xr, xi = x[:inner], x[inner:]
        hr, hi = h_ref[0, 0, s], h_ref[0, 1, s]
        yr = xr * hr - xi * hi
        yi = xr * hi + xi * hr
        z = jnp.dot(gi_ref[...], _hi_lo_rows(jnp.concatenate([yr, yi], axis=0)), preferred_element_type=F32)
        zr, zi = z[:inner], z[inner:]
        o_ref[0, 0, s] = zr * tc - zi * ts
        o_ref[0, 1, s] = zi * tc + zr * ts


def _fft_inner(a, tab, c, h=None, h_block=0):
    p = a.shape[0]
    n1, inner = tab['n1'], tab['inner']
    a5 = a.reshape(p, 2, n1, inner, c)
    tc = 2 * LANES
    kb = 4
    blk = pl.BlockSpec((1, 2, kb, inner, tc), lambda k, j, i: (i, 0, k, 0, j))
    tw_spec = pl.BlockSpec((kb, inner, 1), lambda k, j, i: (k, 0, 0))
    ins = [a5, tab['tw_cos'], tab['tw_sin'], tab['inner_fwd']]
    in_specs = [blk, tw_spec, tw_spec, _const_spec(tab['inner_fwd'].shape)]
    if h is not None:
        ch = h.shape[-1] // inner
        nb = c // tc
        ins += [h.reshape(1, 2, n1, inner, ch), tab['inner_inv']]
        in_specs += [pl.BlockSpec((1, 2, kb, inner, tc), lambda k, j, i: (0, 0, k, 0, h_block * nb + j)),
                     _const_spec(tab['inner_inv'].shape)]
    out = pl.pallas_call(
        functools.partial(_inner_kernel, convolve=h is not None, inner=inner, kb=kb),
        grid=(n1 // kb, c // tc, p),
        in_specs=in_specs,
        out_specs=blk,
        out_shape=jax.ShapeDtypeStruct(a5.shape, F32),
        compiler_params=_params('parallel', 'parallel', 'parallel'),
        name='fft_inner',
    )(*ins)
    return out.reshape(p, 2 * n1, inner * c)


def _gate_kernel(m_ref, z_ref, u_ref, x_ref, bias_ref, *rest, chain):
    y = jnp.dot(m_ref[...], _hi_lo_rows(z_ref[0]), preferred_element_type=F32)
    nxt = x_ref[0] * (y + u_ref[0] * bias_ref[...])
    if chain:
        mf_ref, o_ref, a_ref = rest
        o_ref[0] = nxt
        a_ref[0] = jnp.dot(mf_ref[...], _hi_lo_rows(nxt), preferred_element_type=F32)
    else:
        rest[0][0] = nxt


def _fft_gate(tab, z, u, x, bias_l, chain, tl=4096):
    p, k2, l = z.shape
    n1 = tab['n1']
    tl = min(tl, l)
    row = pl.BlockSpec((1, n1, tl), lambda i, j: (i, 0, j))
    ins = [tab['outer_inv'], z, u, x, bias_l]
    in_specs = [_const_spec(tab['outer_inv'].shape), pl.BlockSpec((1, k2, tl), lambda i, j: (i, 0, j)), row, row,
                pl.BlockSpec((1, tl), lambda i, j: (0, j))]
    out_specs = [row]
    out_shape = [jax.ShapeDtypeStruct((p, n1, l), F32)]
    if chain:
        ins.append(tab['outer_data'])
        in_specs.append(_const_spec(tab['outer_data'].shape))
        out_specs.append(pl.BlockSpec((1, k2, tl), lambda i, j: (i, 0, j)))
        out_shape.append(jax.ShapeDtypeStruct((p, k2, l), F32))
    return pl.pallas_call(
        functools.partial(_gate_kernel, chain=chain),
        grid=(p, l // tl),
        in_specs=in_specs,
        out_specs=out_specs,
        out_shape=out_shape,
        compiler_params=_params('parallel', 'parallel'),
        name='fft_gate',
    )(*ins)


def _hyena(parts, conv_w, conv_b, filt_n, bias, inner):
    b, s, c = parts[0].shape
    tab = _fft_tables(s, inner)
    n1 = tab['n1']
    lanes = inner * c
    h_taps = _two_sided(filt_n).reshape(1, n1, inner * HY_ORDER * c)
    h_spec = _fft_inner(_left_mm(tab['outer_real'], h_taps), tab, HY_ORDER * c)
    v, x1, x2 = [a.reshape(b // 2, n1, lanes) for a in _short_conv(parts, conv_w, conv_b)]
    bias_l = [jnp.tile(bias[o], inner).reshape(1, lanes) for o in range(HY_ORDER)]
    a = _left_mm(tab['outer_data'], v)
    z = _fft_inner(a, tab, c, h_spec, 0)
    z2, a = _fft_gate(tab, z, v, x1, bias_l[0], chain=True)
    z = _fft_inner(a, tab, c, h_spec, 1)
    (z3,) = _fft_gate(tab, z, z2, x2, bias_l[1], chain=False)
    return z3.reshape(b, s, c)


def _attn_kernel(*refs, n_q, n_pieces, ncomp, scale, post_scale):
    q_refs = refs[:n_q]
    pos = n_q
    pieces = []
    for _ in range(n_pieces):
        pieces.append((refs[pos:pos + n_q], refs[pos + n_q]))
        pos += n_q + 1
    if ncomp == 2:
        lam_ref, g_ref = refs[pos:pos + 2]
        pos += 2
    o_ref, kcat_ref, vcat_ref = refs[pos:pos + 3]
    dv = o_ref.shape[3]

    @pl.when(pl.program_id(2) == 0)
    def _():
        row = 0
        for k_refs, v_ref in pieces:
            n = v_ref.shape[2]
            for c in range(ncomp):
                parts = [k_ref[0, c if k_ref.shape[1] == ncomp else 0] for k_ref in k_refs]
                kcat_ref[c, row:row + n, :] = parts[0] if n_q == 1 else jnp.concatenate(parts, axis=1)
            vcat_ref[row:row + n, :dv] = v_ref[0, 0]
            vcat_ref[row:row + n, dv:] = jnp.ones((n, dv), BF16)
            row += n

    outs = []
    for c in range(ncomp):
        q = q_refs[0][0, c] if n_q == 1 else jnp.concatenate([q_ref[0, c] for q_ref in q_refs], axis=1)
        s = _dot_nt((q * (scale * math.log2(math.e))).astype(BF16), kcat_ref[c])
        m = jnp.max(s, axis=-1, keepdims=True)
        p = jnp.exp2((s - m).astype(BF16))
        ol = jnp.dot(p, vcat_ref[...], preferred_element_type=F32)
        outs.append(ol[:, :dv] / ol[:, dv:dv + 1])
    if ncomp == 2:
        o = outs[0] - lam_ref[0] * outs[1]
        o = _rms(o) * g_ref[...] * post_scale
    else:
        o = outs[0]
    o_ref[0, 0] = o


def _attention(q_parts, pieces, heads, ncomp, scale, tq=256, lam=None, subln_g=None, post_scale=1.0):
    b, _, sq, _ = q_parts[0].shape
    dv = pieces[0][1].shape[3]
    tq = min(tq, sq)
    ins = list(q_parts)
    in_specs = [pl.BlockSpec((1, ncomp, tq, q.shape[3]), lambda i, h, j: (i, h, j, 0)) for q in q_parts]
    for k_parts, v in pieces:
        for k in k_parts:
            ins.append(k)
            if k.shape[1] == 1:
                in_specs.append(pl.BlockSpec((1, 1) + k.shape[2:], lambda i, h, j: (i, 0, 0, 0)))
            else:
                in_specs.append(pl.BlockSpec((1, ncomp) + k.shape[2:], lambda i, h, j: (i, h, 0, 0)))
        ins.append(v)
        in_specs.append(pl.BlockSpec((1, 1) + v.shape[2:], lambda i, h, j: (i, h, 0, 0)))
    if ncomp == 2:
        ins += [lam.reshape(1), subln_g.reshape(1, dv)]
        in_specs += [pl.BlockSpec(memory_space=pltpu.SMEM), _const_spec((1, dv))]
    sk = sum(v.shape[2] for _, v in pieces)
    dqk = sum(q.shape[3] for q in q_parts)
    return pl.pallas_call(
        functools.partial(_attn_kernel, n_q=len(q_parts), n_pieces=len(pieces), ncomp=ncomp, scale=scale,
                          post_scale=post_scale),
        grid=(b, heads, sq // tq),
        in_specs=in_specs,
        out_specs=pl.BlockSpec((1, 1, tq, dv), lambda i, h, j: (i, h, j, 0)),
        out_shape=jax.ShapeDtypeStruct((b, heads, sq, dv), F32),
        scratch_shapes=[pltpu.VMEM((ncomp, sk, dqk), BF16), pltpu.VMEM((sk, 2 * dv), BF16)],
        compiler_params=_params('parallel', 'parallel', 'arbitrary'),
        name='attention',
    )(*ins)


def _forget_terms(f, log_lb, log_1m_lb, one_m_lb):
    log_sig = jnp.minimum(f, 0.0) - jnp.log1p(jnp.exp(-jnp.abs(f)))
    b = log_1m_lb + log_sig
    log_g = jnp.maximum(log_lb, b) + jnp.log1p(jnp.exp(-jnp.abs(log_lb - b)))
    return log_g, one_m_lb * jax.nn.sigmoid(-f)


def _hg_tables():
    ck, sub = HG_CHUNK, HG_SUB
    t = np.arange(ck)
    cum_mats, half_masks, group_masks, keeps = [], [], [], []
    for rev in (False, True):
        cum_mats.append((t[None, :] >= t[:, None]) if rev else (t[None, :] <= t[:, None]))
        halves = []
        hs = ck // 2
        while hs >= sub:
            pos = t % (2 * hs)
            q_half = (pos < hs) if rev else (pos >= hs)
            halves.append(np.stack([q_half, ~q_half]))
            if not rev:
                grp = (t[:, None] // (2 * hs)) == (t[None, :] // (2 * hs))
                group_masks.append(np.concatenate([grp, grp], axis=0))
            hs //= 2
        half_masks.append(np.stack(halves))
        c = np.arange(ck * sub)
        tt, ss = (c // sub) % sub, c % sub
        keeps.append((ss >= tt) if rev else (ss <= tt))
    lanes = 2 * HG_KEY_DIM
    ln = np.arange(lanes)
    bd = (ln[:, None] // HG_KEY_DIM) == (ln[None, :] // HG_KEY_DIM)
    hm = np.broadcast_to(np.stack(half_masks)[..., None], (2, len(half_masks[0]), 2, ck, lanes))
    return (jnp.asarray(np.stack(cum_mats), BF16), jnp.asarray(hm, F32), jnp.asarray(np.stack(group_masks), F32),
            jnp.asarray(bd, F32), jnp.asarray(np.broadcast_to(np.stack(keeps)[..., None], (2, ck * sub, lanes)), F32))


def _split2(x):
    a = x.astype(BF16)
    return a, (x - a.astype(F32)).astype(BF16)


def _hg_chunk(q, k, v, lg, st, rev, cm, hm, gm, keep, bd, m0, m1):
    ck, sub = HG_CHUNK, HG_SUB
    lanes = lg.shape[1]
    c2 = jnp.dot(cm, jnp.concatenate(_split2(lg), axis=1), preferred_element_type=F32)
    cum = c2[:, :lanes] + c2[:, lanes:]
    tot = cum[0:1] if rev else cum[ck - 1:ck]
    o = _dot_nt((q * jnp.exp(cum)).astype(BF16), st.astype(BF16))
    kd = (k * jnp.exp(tot - cum)).astype(BF16)
    st_new = st * jnp.exp(tot) + bd * jnp.dot(v.T.astype(BF16), kd, preferred_element_type=F32)
    s2 = None
    for lv in range(gm.shape[0]):
        hs = ck >> (lv + 1)
        edge = [(g + hs) if rev else (g + hs - 1) for g in range(0, ck, 2 * hs)]
        cb = jnp.concatenate([jnp.broadcast_to(cum[e:e + 1], (2 * hs, lanes)) for e in edge], axis=0)
        qd = q * jnp.exp(jnp.minimum(cum - cb, 0.0)) * hm[lv, 0]
        kf = (k * jnp.exp(jnp.minimum(cb - cum, 0.0)) * hm[lv, 1]).astype(BF16)
        q2 = jnp.concatenate([qd * m0, qd * m1], axis=0).astype(BF16)
        term = _dot_nt(q2, kf) * gm[lv]
        s2 = term if s2 is None else s2 + term
    r = jnp.dot(s2.astype(BF16), v.astype(BF16), preferred_element_type=F32)
    o = o + m0 * r[:ck] + m1 * r[ck:]
    rows, vts = [], []
    for i in range(ck // sub):
        lo, hi = i * sub, (i + 1) * sub
        ki, ci = k[lo:hi], cum[lo:hi]
        for t in range(lo, hi):
            rows.append(q[t:t + 1] * ki * jnp.exp(jnp.minimum(cum[t:t + 1] - ci, 0.0)))
            vts.append(v[lo:hi])
    sc = jnp.dot(jnp.concatenate(rows, axis=0).astype(BF16), bd.astype(BF16), preferred_element_type=F32)
    prod = sc * jnp.concatenate(vts, axis=0) * keep
    o = o + jnp.sum(prod.reshape(ck, sub, lanes), axis=1)
    return o, st_new


def _hgrn_kernel(q_ref, ff_ref, fb_ref, i_ref, qc_ref, ffc_ref, fbc_ref, ic_ref, lb_ref, g_ref,
                 cm_ref, hm_ref, gm_ref, bd_ref, keep_ref, o_ref, oc_ref, or_ref, ocr_ref, st_ref, *, n_lat, n_ctx):
    ck = HG_CHUNK
    lanes = o_ref.shape[-1]
    lane = lax.broadcasted_iota(jnp.int32, (1, lanes), 1)
    m0 = (lane < HG_KEY_DIM).astype(F32)
    m1 = 1.0 - m0
    bd = bd_ref[...]
    gm = gm_ref[...]

    def one(q, f, v, rev):
        d = 1 if rev else 0
        lg, k = _forget_terms(f, lb_ref[d, 0:1, :], lb_ref[d, 1:2, :], lb_ref[d, 2:3, :])
        o, st = _hg_chunk(q, k, v, lg, st_ref[d], rev, cm_ref[d], hm_ref[d], gm, keep_ref[d], bd, m0, m1)
        st_ref[d] = st
        return o

    def sweep(qr, ffr, fbr, ir, out_f, out_r, n):
        nc = n // ck

        def body(step, carry):
            idf = pl.ds(pl.multiple_of(step * ck, ck), ck)
            idr = pl.ds(pl.multiple_of((nc - 1 - step) * ck, ck), ck)
            out_f[0, idf, :] = one(qr[0, idf, :], ffr[0, idf, :], ir[0, idf, :], False)
            out_r[idr, :] = one(qr[0, idr, :], fbr[0, idr, :], ir[0, idr, :], True)
            return carry

        lax.fori_loop(0, nc, body, 0, unroll=HG_UNROLL)

    st_ref[...] = jnp.zeros(st_ref.shape, F32)
    sweep(qc_ref, ffc_ref, fbc_ref, ic_ref, oc_ref, ocr_ref, n_ctx)
    sweep(q_ref, ff_ref, fb_ref, i_ref, o_ref, or_ref, n_lat)

    mean_mat = bd * (1.0 / HG_VAL_DIM)

    def readout(out, out_r, n):
        tile = min(n, 512)

        def body(step, carry):
            idx = pl.ds(pl.multiple_of(step * tile, tile), tile)
            x = out[0, idx, :] + out_r[idx, :]
            ms = jnp.dot(x * x, mean_mat, precision=HIGHEST, preferred_element_type=F32)
            out[0, idx, :] = x * lax.rsqrt(ms + NORM_EPS) * g_ref[...]
            return carry

        lax.fori_loop(0, n // tile, body, 0)

    readout(oc_ref, ocr_ref, n_ctx)
    readout(o_ref, or_ref, n_lat)


def _hgrn(q, ff, fb, iv, qc, ffc, fbc, ic, lb_terms, norm_g):
    b, n_lat, width = q.shape
    n_ctx = qc.shape[1]
    lanes = 2 * HG_KEY_DIM
    tables = _hg_tables()
    lat = pl.BlockSpec((1, n_lat, lanes), lambda i, j: (i, 0, j))
    ctx = pl.BlockSpec((1, n_ctx, lanes), lambda i, j: (i, 0, j))
    g2 = jnp.tile(norm_g, 2).reshape(1, lanes)
    return pl.pallas_call(
        functools.partial(_hgrn_kernel, n_lat=n_lat, n_ctx=n_ctx),
        grid=(b, width // lanes),
        in_specs=[lat] * 4 + [ctx] * 4 + [pl.BlockSpec((2, 3, lanes), lambda i, j: (0, 0, j)), _const_spec((1, lanes))]
                 + [_const_spec(t.shape) for t in tables],
        out_specs=[lat, ctx],
        out_shape=[jax.ShapeDtypeStruct(q.shape, F32), jax.ShapeDtypeStruct(qc.shape, F32)],
        scratch_shapes=[pltpu.VMEM((n_lat, lanes), F32), pltpu.VMEM((n_ctx, lanes), F32), pltpu.VMEM((2, lanes, lanes), F32)],
        compiler_params=_params('parallel', 'parallel'),
        name='hgrn2',
    )(q, ff, fb, iv, qc, ffc, fbc, ic, lb_terms, g2, *tables)


def _out_proj_kernel(hy_ref, da_ref, mla_ref, hg_ref, gate_ref, x_ref, g1_ref, w_ref, o_ref):
    c = hy_ref.shape[2]
    acc = jnp.dot(hy_ref[0].astype(BF16), w_ref[0:c, :], preferred_element_type=F32)
    for i, head_ref in ((1, da_ref), (2, mla_ref)):
        dv = head_ref.shape[3]
        for h in range(head_ref.shape[1]):
            lo = i * c + h * dv
            acc = acc + jnp.dot(head_ref[0, h].astype(BF16), w_ref[lo:lo + dv, :], preferred_element_type=F32)
    hg = hg_ref[0] * _silu(gate_ref[0])
    acc = acc + jnp.dot(hg.astype(BF16), w_ref[3 * c:4 * c, :], preferred_element_type=F32)
    o_ref[0] = x_ref[0] + g1_ref[0] * acc


def _out_proj(y_hy, y_da, y_mla, y_hg, gate, x, g1, w_out, tm=512):
    b, s, d = x.shape
    tm = min(tm, s)
    c = y_hy.shape[2]
    part = pl.BlockSpec((1, tm, c), lambda i, j: (i, j, 0))
    headed = lambda a: pl.BlockSpec((1, a.shape[1], tm, a.shape[3]), lambda i, j: (i, 0, j, 0))
    row = pl.BlockSpec((1, tm, d), lambda i, j: (i, j, 0))
    return pl.pallas_call(
        _out_proj_kernel,
        grid=(b, s // tm),
        in_specs=[part, headed(y_da), headed(y_mla), part, part, row, pl.BlockSpec((1, 1, d), lambda i, j: (i, 0, 0)),
                  _const_spec(w_out.shape)],
        out_specs=row,
        out_shape=jax.ShapeDtypeStruct(x.shape, F32),
        compiler_params=_params('parallel', 'parallel'),
        name='out_proj',
    )(y_hy, y_da, y_mla, y_hg, gate, x, g1, w_out)


def _router_kernel(x_ref, g_ref, sc_ref, sh_ref, wrt_ref, bias_ref, *rest, compact):
    h = _rms(x_ref[0]) * g_ref[...] * (1.0 + sc_ref[0]) + sh_ref[0]
    tm = h.shape[0]
    scores = jax.nn.sigmoid(_dot_nt(wrt_ref[...], h, precision=HIGHEST))
    choice = scores + bias_ref[...]
    per = N_EXPERTS // N_EXPERT_GROUPS
    neg = -jnp.inf
    iota_g = lax.broadcasted_iota(jnp.int32, (per, tm), 0)
    grp_rows = []
    for gi in range(N_EXPERT_GROUPS):
        blk = choice[gi * per:(gi + 1) * per]
        m1 = jnp.max(blk, axis=0, keepdims=True)
        first = jnp.min(jnp.where(blk == m1, iota_g, per), axis=0, keepdims=True)
        m2 = jnp.max(jnp.where(iota_g == first, neg, blk), axis=0, keepdims=True)
        grp_rows.append(m1 + m2)
    grp = jnp.concatenate(grp_rows, axis=0)
    iota_n = lax.broadcasted_iota(jnp.int32, (N_EXPERT_GROUPS, tm), 0)
    gsel = jnp.zeros((N_EXPERT_GROUPS, tm), F32)
    for _ in range(TOPK_GROUPS):
        m = jnp.max(grp, axis=0, keepdims=True)
        first = jnp.min(jnp.where(grp == m, iota_n, N_EXPERT_GROUPS), axis=0, keepdims=True)
        hit = iota_n == first
        gsel = jnp.where(hit, 1.0, gsel)
        grp = jnp.where(hit, neg, grp)
    emask = jnp.concatenate([jnp.broadcast_to(gsel[gi:gi + 1], (per, tm)) for gi in range(N_EXPERT_GROUPS)], axis=0)
    cand = jnp.where(emask > 0.0, choice, neg)
    iota_e = lax.broadcasted_iota(jnp.int32, (N_EXPERTS, tm), 0)
    sel = jnp.zeros((N_EXPERTS, tm), F32)
    chosen = []
    for _ in range(TOP_K):
        m = jnp.max(cand, axis=0, keepdims=True)
        first = jnp.min(jnp.where(cand == m, iota_e, N_EXPERTS), axis=0, keepdims=True)
        hit = iota_e == first
        sel = jnp.where(hit, 1.0, sel)
        cand = jnp.where(hit, neg, cand)
        chosen.append(first)
    w = scores * sel
    gate = w / jnp.sum(w, axis=0, keepdims=True) * ROUTED_SCALE
    if not compact:
        h_ref, gate_ref = rest
        h_ref[0] = h.astype(BF16)
        gate_ref[0] = gate
        return
    hp_ref, eid_ref, rank_ref, w_ref, cnt_out_ref, cnt_ref = rest
    hp_ref[0] = _pack_halves(h)

    @pl.when((pl.program_id(0) == 0) & (pl.program_id(1) == 0))
    def _():
        cnt_ref[...] = jnp.zeros(cnt_ref.shape, F32)

    src = lax.broadcasted_iota(jnp.int32, (tm, tm), 0)
    dst = lax.broadcasted_iota(jnp.int32, (tm, tm), 1)
    running = jnp.dot(sel.astype(BF16), (src <= dst).astype(BF16), preferred_element_type=F32)
    rank_dense = cnt_ref[:, 0:1] + running - 1.0
    e_rows, r_rows, w_rows = [], [], []
    for first in chosen:
        hit = iota_e == first
        e_rows.append(first)
        r_rows.append(jnp.sum(jnp.where(hit, rank_dense, 0.0), axis=0, keepdims=True))
        w_rows.append(jnp.sum(jnp.where(hit, gate, 0.0), axis=0, keepdims=True))
    eid_ref[...] = jnp.concatenate(e_rows, axis=0)
    rank_ref[...] = jnp.concatenate(r_rows, axis=0).astype(jnp.int32)
    w_ref[...] = jnp.concatenate(w_rows, axis=0)
    cnt_ref[...] = cnt_ref[...] + running[:, tm - 1:tm]
    cnt_out_ref[...] = cnt_ref[...]


def _router(x, g, scale, shift, w_router, e_bias, tm=512, compact=False):
    b, s, d = x.shape
    tm = min(tm, s)
    e = w_router.shape[1]
    row = pl.BlockSpec((1, tm, d), lambda i, j: (i, j, 0))
    mod = pl.BlockSpec((1, 1, d), lambda i, j: (i, 0, 0))
    if compact:
        nj = s // tm
        tok = pl.BlockSpec((TOP_K, tm), lambda i, j: (0, i * nj + j))
        out_specs = [pl.BlockSpec((1, tm, d // 2), lambda i, j: (i, j, 0)), tok, tok, tok, _const_spec((e, LANES))]
        out_shape = [jax.ShapeDtypeStruct((b, s, d // 2), jnp.int32), jax.ShapeDtypeStruct((TOP_K, b * s), jnp.int32),
                     jax.ShapeDtypeStruct((TOP_K, b * s), jnp.int32), jax.ShapeDtypeStruct((TOP_K, b * s), F32),
                     jax.ShapeDtypeStruct((e, LANES), F32)]
        scratch = [pltpu.VMEM((e, LANES), F32)]
        semantics = ('arbitrary', 'arbitrary')
    else:
        out_specs = [row, pl.BlockSpec((1, e, tm), lambda i, j: (i, 0, j))]
        out_shape = [jax.ShapeDtypeStruct((b, s, d), BF16), jax.ShapeDtypeStruct((b, e, s), F32)]
        scratch = []
        semantics = ('parallel', 'parallel')
    return pl.pallas_call(
        functools.partial(_router_kernel, compact=compact),
        grid=(b, s // tm),
        in_specs=[row, _const_spec((1, d)), mod, mod, _const_spec((e, d)), _const_spec((e, 1))],
        out_specs=out_specs,
        out_shape=out_shape,
        scratch_shapes=scratch,
        compiler_params=_params(*semantics),
        name='router',
    )(x, g.reshape(1, d), scale, shift, w_router.T, e_bias.reshape(e, 1))


def _moe_kernel(h_ref, x_ref, gate_ref, g2_ref, wg_ref, wu_ref, wd_ref, sg_ref, su_ref, sd_ref, *rest, final):
    if final:
        fg_ref, o_ref, acc_ref = rest
    else:
        o_ref, acc_ref = rest
    e = pl.program_id(2)
    h = h_ref[0]

    @pl.when(e == 0)
    def _():
        a = jnp.dot(h, sg_ref[...], preferred_element_type=F32)
        u = jnp.dot(h, su_ref[...], preferred_element_type=F32)
        acc_ref[...] = jnp.dot((_silu(a) * u).astype(BF16), sd_ref[...], preferred_element_type=F32)

    lane = lax.broadcasted_iota(jnp.int32, gate_ref.shape[1:], 1)
    gcol = jnp.sum(jnp.where(lane == e, gate_ref[0], 0.0), axis=-1, keepdims=True)
    a = jnp.dot(h, wg_ref[0].astype(BF16), preferred_element_type=F32)
    u = jnp.dot(h, wu_ref[0].astype(BF16), preferred_element_type=F32)
    acc_ref[...] += jnp.dot((_silu(a) * u * gcol).astype(BF16), wd_ref[0].astype(BF16), preferred_element_type=F32)

    @pl.when(e == pl.num_programs(2) - 1)
    def _():
        y = x_ref[0] + g2_ref[0] * acc_ref[...]
        if final:
            y = _rms(y) * fg_ref[...]
        o_ref[0] = y


def _moe(h2, x, gate, g2, layer, w_gate, w_up, w_down, s_gate, s_up, s_down, final_g=None, tm=1024):
    b, s, d = x.shape
    tm = min(tm, s)
    _, e, _, ff = w_gate.shape
    row = pl.BlockSpec((1, tm, d), lambda i, j, k: (i, j, 0))
    ins = [h2, x, gate, g2, w_gate, w_up, w_down, s_gate, s_up, s_down]
    in_specs = [row, row, pl.BlockSpec((1, tm, e), lambda i, j, k: (i, j, 0)),
                pl.BlockSpec((1, 1, d), lambda i, j, k: (i, 0, 0)),
                pl.BlockSpec((None, 1, d, ff), lambda i, j, k: (layer, k, 0, 0)),
                pl.BlockSpec((None, 1, d, ff), lambda i, j, k: (layer, k, 0, 0)),
                pl.BlockSpec((None, 1, ff, d), lambda i, j, k: (layer, k, 0, 0)),
                _const_spec(s_gate.shape), _const_spec(s_up.shape), _const_spec(s_down.shape)]
    if final_g is not None:
        ins.append(final_g.reshape(1, d))
        in_specs.append(_const_spec((1, d)))
    return pl.pallas_call(
        functools.partial(_moe_kernel, final=final_g is not None),
        grid=(b, s // tm, e),
        in_specs=in_specs,
        out_specs=row,
        out_shape=jax.ShapeDtypeStruct(x.shape, F32),
        scratch_shapes=[pltpu.VMEM((tm, d), F32)],
        compiler_params=_params('parallel', 'parallel', 'arbitrary'),
        name='moe',
    )(*ins)


MOE_ROW_TILE = 512
SC_ROWS = 128
V7X_SC_CORES = 2
V7X_SC_SUBCORES = 16


def _pack_halves(x):
    n = x.shape[1] // 2
    lo = pltpu.bitcast(x[:, :n].astype(BF16).astype(F32), jnp.int32)
    hi = pltpu.bitcast(x[:, n:].astype(BF16).astype(F32), jnp.int32)
    return jnp.bitwise_or(jnp.bitwise_and(hi, -65536), lax.shift_right_logical(lo, 16))


def _unpack_halves(p):
    lo = pltpu.bitcast(lax.shift_left(p, 16), F32).astype(BF16)
    hi = pltpu.bitcast(jnp.bitwise_and(p, -65536), F32).astype(BF16)
    return lo, hi


def _route_pos_kernel(off_ref, eid_ref, rank_ref, pos_ref):
    eid = eid_ref[...]
    base = jnp.zeros(eid.shape, jnp.int32)
    for e in range(N_EXPERTS):
        base = jnp.where(eid == e, off_ref[e], base)
    pos_ref[...] = base + rank_ref[...]


def _route_pos(offsets, eid, rank):
    return pl.pallas_call(
        _route_pos_kernel,
        grid=(1,),
        in_specs=[pl.BlockSpec(memory_space=pltpu.SMEM), _const_spec(eid.shape), _const_spec(rank.shape)],
        out_specs=_const_spec(eid.shape),
        out_shape=jax.ShapeDtypeStruct(eid.shape, jnp.int32),
        compiler_params=_params('arbitrary'),
        name='route_pos',
    )(offsets, eid, rank)


def _sc_mesh():
    return plsc.VectorSubcoreMesh(core_axis_name='c', subcore_axis_name='s', num_cores=V7X_SC_CORES,
                                  num_subcores=V7X_SC_SUBCORES)


def _sc_dispatch(hp, pos, n_rows):
    t, w = hp.shape
    k = pos.shape[0]
    workers = V7X_SC_CORES * V7X_SC_SUBCORES
    per_worker = t // workers
    pos_flat = pos.reshape(k * t)

    @functools.partial(pl.kernel, mesh=_sc_mesh(), out_type=jax.ShapeDtypeStruct((n_rows, w), jnp.int32),
                       scratch_types=[pltpu.VMEM((SC_ROWS,), jnp.int32), pltpu.VMEM((SC_ROWS, w), jnp.int32),
                                      pltpu.SemaphoreType.DMA])
    def scatter(hp_hbm, pos_hbm, out_hbm, idx_v, rows_v, sem):
        wid = lax.axis_index('s') * V7X_SC_CORES + lax.axis_index('c')

        @pl.loop(0, per_worker // SC_ROWS)
        def _(i):
            t0 = pl.multiple_of(wid * per_worker + i * SC_ROWS, SC_ROWS)
            pltpu.sync_copy(hp_hbm.at[pl.ds(t0, SC_ROWS)], rows_v)
            for j in range(k):
                pltpu.sync_copy(pos_hbm.at[pl.ds(pl.multiple_of(j * t + t0, SC_ROWS), SC_ROWS)], idx_v)
                pltpu.async_copy(rows_v, out_hbm.at[idx_v], sem).wait()

    return scatter(hp, pos_flat)


def _sc_collect(yp, pos):
    _, w = yp.shape
    k, t = pos.shape
    workers = V7X_SC_CORES * V7X_SC_SUBCORES
    per_worker = k * t // workers
    pos_flat = pos.reshape(k * t)

    @functools.partial(pl.kernel, mesh=_sc_mesh(), out_type=jax.ShapeDtypeStruct((k * t, w), jnp.int32),
                       scratch_types=[pltpu.VMEM((SC_ROWS,), jnp.int32), pltpu.VMEM((SC_ROWS, w), jnp.int32),
                                      pltpu.SemaphoreType.DMA])
    def gather(yp_hbm, pos_hbm, out_hbm, idx_v, rows_v, sem):
        wid = lax.axis_index('s') * V7X_SC_CORES + lax.axis_index('c')

        @pl.loop(0, per_worker // SC_ROWS)
        def _(i):
            r0 = pl.multiple_of(wid * per_worker + i * SC_ROWS, SC_ROWS)
            pltpu.sync_copy(pos_hbm.at[pl.ds(r0, SC_ROWS)], idx_v)
            pltpu.async_copy(yp_hbm.at[idx_v], rows_v, sem).wait()
            pltpu.sync_copy(rows_v, out_hbm.at[pl.ds(r0, SC_ROWS)])

    return gather(yp, pos_flat)


def _expert_kernel(te_ref, nu_ref, x_ref, wg_ref, wu_ref, wd_ref, o_ref, wg_s, wu_s, wd_s):
    i = pl.program_id(0)

    @pl.when(i < nu_ref[0])
    def _():
        @pl.when((i == 0) | (te_ref[i] != te_ref[jnp.maximum(i - 1, 0)]))
        def _():
            wg_s[...] = wg_ref[0].astype(BF16)
            wu_s[...] = wu_ref[0].astype(BF16)
            wd_s[...] = wd_ref[0].astype(BF16)

        x = jnp.concatenate(_unpack_halves(x_ref[...]), axis=1)
        a = jnp.dot(x, wg_s[...], preferred_element_type=F32)
        u = jnp.dot(x, wu_s[...], preferred_element_type=F32)
        y = jnp.dot((_silu(a) * u).astype(BF16), wd_s[...], preferred_element_type=F32)
        o_ref[...] = _pack_halves(y)


def _experts(xp, tile_expert, n_used, layer, w_gate, w_up, w_down):
    n_rows, half = xp.shape
    _, _, d, ff = w_gate.shape
    r = MOE_ROW_TILE
    row = pl.BlockSpec((r, half), lambda i, te, nu: (i, 0))
    grid_spec = pltpu.PrefetchScalarGridSpec(
        num_scalar_prefetch=2,
        grid=(n_rows // r,),
        in_specs=[row,
                  pl.BlockSpec((None, 1, d, ff), lambda i, te, nu: (layer, te[i], 0, 0)),
                  pl.BlockSpec((None, 1, d, ff), lambda i, te, nu: (layer, te[i], 0, 0)),
                  pl.BlockSpec((None, 1, ff, d), lambda i, te, nu: (layer, te[i], 0, 0))],
        out_specs=row,
        scratch_shapes=[pltpu.VMEM((d, ff), BF16), pltpu.VMEM((d, ff), BF16), pltpu.VMEM((ff, d), BF16)],
    )
    return pl.pallas_call(
        _expert_kernel,
        grid_spec=grid_spec,
        out_shape=jax.ShapeDtypeStruct((n_rows, half), jnp.int32),
        compiler_params=_params('arbitrary'),
        name='experts',
    )(tile_expert, n_used, xp, w_gate, w_up, w_down)


def _shared_kernel(hp_ref, x_ref, g2_ref, sg_ref, su_ref, sd_ref, o_ref):
    h = jnp.concatenate(_unpack_halves(hp_ref[0]), axis=1)
    a = jnp.dot(h, sg_ref[...], preferred_element_type=F32)
    u = jnp.dot(h, su_ref[...], preferred_element_type=F32)
    shared = jnp.dot((_silu(a) * u).astype(BF16), sd_ref[...], preferred_element_type=F32)
    o_ref[0] = x_ref[0] + g2_ref[0] * shared


def _shared_expert(hp, x, g2, s_gate, s_up, s_down, tm=512):
    b, s, d = x.shape
    row = pl.BlockSpec((1, tm, d), lambda i, j: (i, j, 0))
    return pl.pallas_call(
        _shared_kernel,
        grid=(b, s // tm),
        in_specs=[pl.BlockSpec((1, tm, d // 2), lambda i, j: (i, j, 0)), row, pl.BlockSpec((1, 1, d), lambda i, j: (i, 0, 0)),
                  _const_spec(s_gate.shape), _const_spec(s_up.shape), _const_spec(s_down.shape)],
        out_specs=row,
        out_shape=jax.ShapeDtypeStruct(x.shape, F32),
        compiler_params=_params('parallel', 'parallel'),
        name='moe_shared',
    )(hp, x, g2, s_gate, s_up, s_down)


def _combine_kernel(yg_ref, w_ref, x_ref, g2_ref, *rest, final):
    if final:
        fg_ref, o_ref = rest
    else:
        (o_ref,) = rest
    half = yg_ref.shape[3]
    wts = w_ref[0]
    acc_lo = jnp.zeros((wts.shape[0], half), F32)
    acc_hi = jnp.zeros((wts.shape[0], half), F32)
    for k in range(yg_ref.shape[0]):
        ylo, yhi = _unpack_halves(yg_ref[k, 0])
        wk = wts[:, k:k + 1]
        acc_lo = acc_lo + wk * ylo.astype(F32)
        acc_hi = acc_hi + wk * yhi.astype(F32)
    y = x_ref[0] + g2_ref[0] * jnp.concatenate([acc_lo, acc_hi], axis=1)
    if final:
        y = _rms(y) * fg_ref[...]
    o_ref[0] = y


def _combine(yg, wts, x, g2, final_g=None, tm=256):
    b, s, d = x.shape
    k = yg.shape[0]
    half = d // 2
    row = pl.BlockSpec((1, tm, d), lambda i, j: (i, j, 0))
    ins = [yg, wts, x, g2]
    in_specs = [pl.BlockSpec((k, 1, tm, half), lambda i, j: (0, i, j, 0)), pl.BlockSpec((1, tm, k), lambda i, j: (i, j, 0)),
                row, pl.BlockSpec((1, 1, d), lambda i, j: (i, 0, 0))]
    if final_g is not None:
        ins.append(final_g.reshape(1, d))
        in_specs.append(_const_spec((1, d)))
    return pl.pallas_call(
        functools.partial(_combine_kernel, final=final_g is not None),
        grid=(b, s // tm),
        in_specs=in_specs,
        out_specs=row,
        out_shape=jax.ShapeDtypeStruct(x.shape, F32),
        compiler_params=_params('parallel', 'parallel'),
        name='moe_combine',
    )(*ins)


def _routed_moe(x, g, scale, shift, g2, w_router, e_bias, layer, w_gate, w_up, w_down, s_gate, s_up, s_down, final_g=None):
    b, s, d = x.shape
    t = b * s
    hp, eid, rank, wts, counts = _router(x, g, scale, shift, w_router, e_bias, compact=True)
    counts = counts[:, 0].astype(jnp.int32)
    r = MOE_ROW_TILE
    padded = (counts + (r - 1)) // r * r
    ends = jnp.cumsum(padded)
    offsets = ends - padded
    n_rows = t * TOP_K + N_EXPERTS * r
    tile_start = jnp.arange(n_rows // r, dtype=jnp.int32) * r
    tile_expert = jnp.minimum(jnp.sum((tile_start[:, None] >= ends[None, :]).astype(jnp.int32), axis=1), N_EXPERTS - 1)
    n_used = (ends[-1] // r).reshape(1).astype(jnp.int32)
    pos = _route_pos(offsets.astype(jnp.int32), eid, rank)
    xp = _sc_dispatch(hp.reshape(t, d // 2), pos, n_rows)
    yp = _experts(xp, tile_expert.astype(jnp.int32), n_used, layer, w_gate, w_up, w_down)
    base = _shared_expert(hp, x, g2, s_gate, s_up, s_down)
    yg = _sc_collect(yp, pos).reshape(TOP_K, b, s, d // 2)
    return _combine(yg, wts.T.reshape(b, s, TOP_K), base, g2, final_g)


def _mixers(p, pc, ctx_out, prm, l, lam_init, rope_tabs, lb_terms):
    s = p['hy_v'].shape[1]
    sc = pc['hy_v'].shape[1]

    hy_args = (prm['hy_w1'][l], prm['hy_b1'][l], prm['hy_w2'][l], prm['hy_b2'][l], prm['hy_w3'][l], prm['hy_b3'][l],
               prm['hy_sin_freq'][l], prm['hy_decay'][l])
    y_hy = _hyena([p['hy_v'], p['hy_x1'], p['hy_x2']], prm['hy_conv_w'][l], prm['hy_conv_b'][l],
                  _hy_filters(s, *hy_args), prm['hy_bias'][l], inner=128)
    yc_hy = None
    if ctx_out:
        yc_hy = _hyena([pc['hy_v'], pc['hy_x1'], pc['hy_x2']], prm['hy_conv_w'][l], prm['hy_conv_b'][l],
                       _hy_filters(sc, *hy_args), prm['hy_bias'][l], inner=32)

    lp = prm['da_lambda'][l].astype(F32)
    lam = jnp.exp(jnp.sum(lp[0] * lp[1])) - jnp.exp(jnp.sum(lp[2] * lp[3])) + lam_init
    da_kw = dict(heads=DA_HEADS, ncomp=2, scale=DA_HEAD_DIM ** -0.5, lam=lam, subln_g=prm['da_subln_g'][l],
                 post_scale=1.0 - lam_init)
    da_ctx = ([pc['da_k']], pc['da_v'])
    y_da = _attention([p['da_q']], [da_ctx, ([p['da_k']], p['da_v'])], **da_kw)
    yc_da = _attention([pc['da_q']], [da_ctx], **da_kw) if ctx_out else None

    wq = prm['mla_w_q_up'][l].reshape(MLA_Q_RANK, MLA_HEADS, MLA_NOPE_DIM + MLA_ROPE_DIM)
    wq_n = wq[:, :, :MLA_NOPE_DIM].reshape(MLA_Q_RANK, -1).astype(BF16)
    wq_r = wq[:, :, MLA_NOPE_DIM:].reshape(MLA_Q_RANK, -1).astype(BF16)
    wkv = prm['mla_w_kv_up'][l].reshape(MLA_KV_RANK, MLA_HEADS, MLA_NOPE_DIM + MLA_V_DIM)
    wkv_n = wkv[:, :, :MLA_NOPE_DIM].reshape(MLA_KV_RANK, -1).astype(BF16)
    wkv_v = wkv[:, :, MLA_NOPE_DIM:].reshape(MLA_KV_RANK, -1).astype(BF16)

    def queries(qd, tabs):
        return _norm_proj(qd, prm['mla_q_norm_g'][l], [(wq_n, F32, False, MLA_HEADS), (wq_r, F32, True, MLA_HEADS)],
                          rope_tabs=tabs)

    def keys_values(kvd):
        return _norm_proj(kvd, prm['mla_kv_norm_g'][l], [(wkv_n, BF16, False, MLA_HEADS), (wkv_v, BF16, False, MLA_HEADS)])

    kn_l, v_l = keys_values(p['mla_kv'])
    kn_c, v_c = keys_values(pc['mla_kv'])
    mla_kw = dict(heads=MLA_HEADS, ncomp=1, scale=(MLA_NOPE_DIM + MLA_ROPE_DIM) ** -0.5)
    mla_ctx = ([kn_c, pc['mla_kr']], v_c)
    y_mla = _attention(queries(p['mla_q'], rope_tabs), [mla_ctx, ([kn_l, p['mla_kr']], v_l)], **mla_kw)
    yc_mla = _attention(queries(pc['mla_q'], None), [mla_ctx], **mla_kw) if ctx_out else None

    o, oc = _hgrn(p['hg_q'], p['hg_ff'], p['hg_fb'], p['hg_i'], pc['hg_q'], pc['hg_ff'], pc['hg_fb'], pc['hg_i'],
                  lb_terms, prm['hg_norm_g'][l])
    return (y_hy, y_da, y_mla, o), (yc_hy, yc_da, yc_mla, oc)


def kernel(x, c, ctx, c_ctx, w_ada, b_ada, norm1_g, norm2_g, w_in, w_out, hy_conv_w, hy_conv_b, hy_w1, hy_b1, hy_w2, hy_b2, hy_w3, hy_b3, hy_sin_freq, hy_decay, hy_bias, da_lambda, da_subln_g, mla_q_norm_g, mla_w_q_up, mla_kv_norm_g, mla_w_kv_up, hg_lower_bounds, hg_norm_g, moe_w_router, moe_bias, moe_w_gate, moe_w_up, moe_w_down, moe_sh_gate, moe_sh_up, moe_sh_down, final_norm_g):
    prm = dict(hy_conv_w=hy_conv_w, hy_conv_b=hy_conv_b, hy_w1=hy_w1, hy_b1=hy_b1, hy_w2=hy_w2, hy_b2=hy_b2,
               hy_w3=hy_w3, hy_b3=hy_b3, hy_sin_freq=hy_sin_freq, hy_decay=hy_decay, hy_bias=hy_bias,
               da_lambda=da_lambda, da_subln_g=da_subln_g, mla_q_norm_g=mla_q_norm_g, mla_w_q_up=mla_w_q_up,
               mla_kv_norm_g=mla_kv_norm_g, mla_w_kv_up=mla_w_kv_up, hg_norm_g=hg_norm_g)
    b, n_lat, d = x.shape
    depth = w_in.shape[0]
    rows = n_lat // GRID_W
    row_pos = jnp.repeat(jnp.arange(rows, dtype=jnp.int32), GRID_W)
    col_pos = jnp.tile(jnp.arange(GRID_W, dtype=jnp.int32), rows)
    rope_tabs = _rope_tables(row_pos, col_pos, 2 * DA_HEADS * DA_HEAD_DIM)
    lbs = jnp.cumsum(jax.nn.softmax(hg_lower_bounds.astype(F32), axis=1), axis=1)
    lbs = lbs - lbs[:, :1]
    cond = jnp.concatenate([c, c_ctx[None], jnp.zeros((8 - b - 1, d), F32)], axis=0)

    for l in range(depth):
        ctx_out = l < depth - 1
        mods = _ada(cond, w_ada[l], b_ada[l])
        sh1, sc1, g1, sh2, sc2, g2 = [m[:, None, :] for m in jnp.split(mods[:b], 6, axis=-1)]
        mc = [jnp.broadcast_to(m[:, None, :], (b, 1, d)) for m in jnp.split(mods[b:b + 1], 6, axis=-1)]

        off = 0
        outs = []
        for _, wdt, dt, rope, split, rep in _SEGMENTS:
            w = w_in[l][:, off:off + wdt].astype(BF16)
            outs.append((jnp.tile(w, (1, rep)) if rep > 1 else w, dt, rope, split))
            off += wdt
        names = [seg[0] for seg in _SEGMENTS]
        p = dict(zip(names, _norm_proj(x, norm1_g[l], outs, sc1, sh1, rope_tabs=rope_tabs)))
        pc = dict(zip(names, _norm_proj(ctx, norm1_g[l], outs, mc[1], mc[0])))

        lb = lbs[:, l]
        lb_terms = jnp.stack([jnp.log(lb), jnp.log1p(-lb), 1.0 - lb], axis=1)
        lam_init = 0.8 - 0.6 * math.exp(-0.3 * l)
        lat_parts, ctx_parts = _mixers(p, pc, ctx_out, prm, l, lam_init, rope_tabs, lb_terms)

        w_out_b = w_out[l].astype(BF16)
        moe_w = (l, moe_w_gate, moe_w_up, moe_w_down,
                 moe_sh_gate[l].astype(BF16), moe_sh_up[l].astype(BF16), moe_sh_down[l].astype(BF16))

        if ctx_out:
            ctx = _out_proj(*ctx_parts, pc['hg_g'], ctx, mc[2], w_out_b)
            flat = ctx.reshape(1, -1, d)
            h2c, gate_c = _router(flat, norm2_g[l], mc[4][:1], mc[3][:1], moe_w_router[l], moe_bias[l])
            ctx = _moe(h2c, flat, gate_c.transpose(0, 2, 1), mc[5][:1], *moe_w).reshape(ctx.shape)

        x = _out_proj(*lat_parts, p['hg_g'], x, g1, w_out_b)
        x = _routed_moe(x, norm2_g[l], sc2, sh2, g2, moe_w_router[l], moe_bias[l], *moe_w,
                        final_g=None if ctx_out else final_norm_g)

    return x
```
